```python
import jax, jax.numpy as jnp
from jax import lax
import numpy as np

D_MODEL = 1024
BATCH = 8
SEQ = 8192
DEPTH = 4

GRID_W = 64
CTX_LEN = 256
N_MIXERS = 3
EPS = 1e-6
POOL_WINDOWS = (2, 4, 8, 16)
POOL_GROUPS = 4
POOL_GW = D_MODEL // POOL_GROUPS
HEAD_DIM = 128
N_HEADS = D_MODEL // HEAD_DIM
N_KV_HEADS = N_HEADS // 2
QKV_WIDTH = (N_HEADS + 2 * N_KV_HEADS) * HEAD_DIM
ROPE_BASE = 10000.0
Q_BLOCK = 128
CHUNK = 128
GMLP_HALF = 2 * D_MODEL
GMLP_GROUPS = 8
GMLP_GW = GMLP_HALF // GMLP_GROUPS
D_FF = 4 * D_MODEL

kernel_name = "hybrid_pool_gqa_gmlp_dit_block"


def n_layers_of(kind):
    return len(range(kind, DEPTH, N_MIXERS))


def rms_norm(x, g):
    xf = x.astype(jnp.float32)
    y = xf * lax.rsqrt(jnp.mean(xf * xf, axis=-1, keepdims=True) + EPS)
    return (y * g.astype(jnp.float32)).astype(x.dtype)


def layer_norm(x, g, b):
    xf = x.astype(jnp.float32)
    mu = jnp.mean(xf, axis=-1, keepdims=True)
    xc = xf - mu
    y = xc * lax.rsqrt(jnp.mean(xc * xc, axis=-1, keepdims=True) + EPS)
    return (y * g.astype(jnp.float32) + b.astype(jnp.float32)).astype(x.dtype)


def modulate(h, shift, scale):
    return h * (1 + scale[:, None, :]) + shift[:, None, :]


def pool_mix(h, w, scale):
    B, L, D = h.shape
    hf = h.astype(jnp.float32)
    cs = jnp.concatenate([jnp.zeros((B, 1, D), jnp.float32), jnp.cumsum(hf, axis=1)], axis=1)
    csg = cs.reshape(B, L + 1, POOL_GROUPS, POOL_GW)
    hg = hf.reshape(B, L, POOL_GROUPS, POOL_GW)
    pos = jnp.arange(L)
    outs = []
    for g, win in enumerate(POOL_WINDOWS):
        lo = jnp.clip(pos - win // 2, 0, L)
        hi = jnp.clip(pos + win - win // 2, 0, L)
        s = jnp.take(csg[:, :, g], hi, axis=1) - jnp.take(csg[:, :, g], lo, axis=1)
        cnt = (hi - lo).astype(jnp.float32)[None, :, None]
        outs.append(s / cnt - hg[:, :, g])
    p = jnp.stack(outs, axis=2).astype(h.dtype)
    y = jnp.einsum("blgc,gcd->blgd", p, w).reshape(B, L, D)
    return y * scale


def axial_rope_tables(L):
    rows_n = L // GRID_W
    row = jnp.repeat(jnp.arange(rows_n), GRID_W).astype(jnp.float32)
    col = jnp.tile(jnp.arange(GRID_W), rows_n).astype(jnp.float32)
    half = HEAD_DIM // 2
    inv = ROPE_BASE ** (-jnp.arange(0, half, 2, dtype=jnp.float32) / half)
    ang_r = row[:, None] * inv[None, :]
    ang_c = col[:, None] * inv[None, :]
    return jnp.cos(ang_r), jnp.sin(ang_r), jnp.cos(ang_c), jnp.sin(ang_c)


def rotate(x, cos, sin):
    x1, x2 = jnp.split(x, 2, axis=-1)
    cos = cos[None, :, None, :]
    sin = sin[None, :, None, :]
    return jnp.concatenate([x1 * cos - x2 * sin, x1 * sin + x2 * cos], axis=-1)


def apply_axial_rope(x, tables):
    cr, sr, cc, scol = tables
    xf = x.astype(jnp.float32)
    xr, xc = jnp.split(xf, 2, axis=-1)
    return jnp.concatenate([rotate(xr, cr, sr), rotate(xc, cc, scol)], axis=-1).astype(x.dtype)


def gqa_core(q, k, v):
    B, Lq = q.shape[0], q.shape[1]
    G = N_HEADS // N_KV_HEADS
    qg = q.reshape(B, Lq, N_KV_HEADS, G, HEAD_DIM)
    s = jnp.einsum("bqkgd,bskd->bkgqs", qg, k).astype(jnp.float32) * (HEAD_DIM ** -0.5)
    p = jax.nn.softmax(s, axis=-1).astype(v.dtype)
    o = jnp.einsum("bkgqs,bskd->bqkgd", p, v)
    return o.reshape(B, Lq, N_HEADS * HEAD_DIM)


def qkv_proj(h, w_qkv, q_g, k_g):
    B, L, _ = h.shape
    qkv = h @ w_qkv
    q, k, v = jnp.split(qkv, [N_HEADS * HEAD_DIM, (N_HEADS + N_KV_HEADS) * HEAD_DIM], axis=-1)
    q = rms_norm(q.reshape(B, L, N_HEADS, HEAD_DIM), q_g)
    k = rms_norm(k.reshape(B, L, N_KV_HEADS, HEAD_DIM), k_g)
    v = v.reshape(B, L, N_KV_HEADS, HEAD_DIM)
    return q, k, v


def attn_mix(h_ctx, h_lat, w_qkv, w_o, q_g, k_g, ctx_out):
    B, S, _ = h_lat.shape
    qc, kc, vc = qkv_proj(h_ctx, w_qkv, q_g, k_g)
    ql, kl, vl = qkv_proj(h_lat, w_qkv, q_g, k_g)
    tables = axial_rope_tables(S)
    ql = apply_axial_rope(ql, tables)
    kl = apply_axial_rope(kl, tables)
    k_all = jnp.concatenate([kc, kl], axis=1)
    v_all = jnp.concatenate([vc, vl], axis=1)
    nb = S // Q_BLOCK
    qb = ql.reshape(B, nb, Q_BLOCK, N_HEADS, HEAD_DIM).transpose(1, 0, 2, 3, 4)
    ob = lax.map(lambda qq: gqa_core(qq, k_all, v_all), qb)
    y_lat = ob.transpose(1, 0, 2, 3).reshape(B, S, N_HEADS * HEAD_DIM) @ w_o
    y_ctx = gqa_core(qc, kc, vc) @ w_o if ctx_out else None
    return y_ctx, y_lat


def gmlp_mix(h, w_in, ln_g, ln_b, ws, bs, w_out):
    B, L, _ = h.shape
    z = jax.nn.gelu(h @ w_in)
    u, v = jnp.split(z, 2, axis=-1)
    v = layer_norm(v, ln_g, ln_b)
    vg = v.reshape(B, L // CHUNK, CHUNK, GMLP_GROUPS, GMLP_GW)
    sv = jnp.einsum("gqp,bnpgc->bnqgc", ws, vg) + bs.T[None, None, :, :, None]
    return (u * sv.reshape(B, L, GMLP_HALF)) @ w_out


def sq_relu_mlp(h, w1, w2):
    return jnp.square(jax.nn.relu(h @ w1)) @ w2


def _fwd_setup_inputs(seed: int = 0) -> dict:
    key = jax.random.key(seed)
    ks = jax.random.split(key, 24)
    f32 = jnp.float32
    D = D_MODEL
    nP, nA, nG = n_layers_of(0), n_layers_of(1), n_layers_of(2)

    def nrm(k, shape, s):
        return jax.random.normal(k, shape, f32) * s

    return {
        "x": nrm(ks[0], (BATCH, SEQ, D), 1.0),
        "c": nrm(ks[1], (BATCH, D), 1.0),
        "ctx": nrm(ks[2], (BATCH, CTX_LEN, D), 1.0),
        "c_ctx": nrm(ks[3], (D,), 1.0),
        "ada_w": nrm(ks[4], (DEPTH, D, 6 * D), 0.5 * D ** -0.5),
        "ada_b": nrm(ks[5], (DEPTH, 6 * D), 0.02),
        "norm_g": 1.0 + nrm(ks[6], (DEPTH, 2, D), 0.02),
        "mlp_w1": nrm(ks[7], (DEPTH, D, D_FF), D ** -0.5),
        "mlp_w2": nrm(ks[8], (DEPTH, D_FF, D), D_FF ** -0.5),
        "pool_w": nrm(ks[9], (nP, POOL_GROUPS, POOL_GW, POOL_GW), POOL_GW ** -0.5),
        "pool_scale": 1.0 + nrm(ks[10], (nP, D), 0.02),
        "attn_w_qkv": nrm(ks[11], (nA, D, QKV_WIDTH), D ** -0.5),
        "attn_w_o": nrm(ks[12], (nA, N_HEADS * HEAD_DIM, D), (N_HEADS * HEAD_DIM) ** -0.5),
        "attn_q_g": 1.0 + nrm(ks[13], (nA, HEAD_DIM), 0.02),
        "attn_k_g": 1.0 + nrm(ks[14], (nA, HEAD_DIM), 0.02),
        "gm_w_in": nrm(ks[15], (nG, D, 2 * GMLP_HALF), D ** -0.5),
        "gm_ln_g": 1.0 + nrm(ks[16], (nG, GMLP_HALF), 0.02),
        "gm_ln_b": nrm(ks[17], (nG, GMLP_HALF), 0.02),
        "gm_ws": nrm(ks[18], (nG, GMLP_GROUPS, CHUNK, CHUNK), CHUNK ** -0.5),
        "gm_bs": nrm(ks[19], (nG, GMLP_GROUPS, CHUNK), 0.02),
        "gm_w_out": nrm(ks[20], (nG, GMLP_HALF, D), GMLP_HALF ** -0.5),
        "final_g": 1.0 + nrm(ks[21], (D,), 0.02),
    }


def _fwd_reference(x, c, ctx, c_ctx, ada_w, ada_b, norm_g, mlp_w1, mlp_w2, pool_w, pool_scale,
              attn_w_qkv, attn_w_o, attn_q_g, attn_k_g, gm_w_in, gm_ln_g, gm_ln_b, gm_ws, gm_bs,
              gm_w_out, final_g):
    last_ctx_read = max([i for i in range(DEPTH) if i % N_MIXERS == 1], default=-1)
    s_lat = jax.nn.silu(c)
    s_ctx = jax.nn.silu(c_ctx)[None, :]
    h_lat, h_ctx = x, ctx
    for i in range(DEPTH):
        kind, j = i % N_MIXERS, i // N_MIXERS
        ctx_in = i <= last_ctx_read
        ctx_out = i < last_ctx_read
        sh1, sc1, g1, sh2, sc2, g2 = jnp.split(s_lat @ ada_w[i] + ada_b[i], 6, axis=-1)
        a_l = modulate(rms_norm(h_lat, norm_g[i, 0]), sh1, sc1)
        if ctx_in:
            csh1, csc1, cg1, csh2, csc2, cg2 = jnp.split(s_ctx @ ada_w[i] + ada_b[i], 6, axis=-1)
            a_c = modulate(rms_norm(h_ctx, norm_g[i, 0]), csh1, csc1)
        y_c = None
        if kind == 0:
            y_l = pool_mix(a_l, pool_w[j], pool_scale[j])
            if ctx_out:
                y_c = pool_mix(a_c, pool_w[j], pool_scale[j])
        elif kind == 1:
            y_c, y_l = attn_mix(a_c, a_l, attn_w_qkv[j], attn_w_o[j], attn_q_g[j], attn_k_g[j], ctx_out)
        else:
            y_l = gmlp_mix(a_l, gm_w_in[j], gm_ln_g[j], gm_ln_b[j], gm_ws[j], gm_bs[j], gm_w_out[j])
            if ctx_out:
                y_c = gmlp_mix(a_c, gm_w_in[j], gm_ln_g[j], gm_ln_b[j], gm_ws[j], gm_bs[j], gm_w_out[j])
        h_lat = h_lat + g1[:, None, :] * y_l
        m_l = modulate(rms_norm(h_lat, norm_g[i, 1]), sh2, sc2)
        h_lat = h_lat + g2[:, None, :] * sq_relu_mlp(m_l, mlp_w1[i], mlp_w2[i])
        if ctx_out:
            h_ctx = h_ctx + cg1[:, None, :] * y_c
            m_c = modulate(rms_norm(h_ctx, norm_g[i, 1]), csh2, csc2)
            h_ctx = h_ctx + cg2[:, None, :] * sq_relu_mlp(m_c, mlp_w1[i], mlp_w2[i])
    return rms_norm(h_lat, final_g)


import jax as _jax
import jax.numpy as _jnp

TWIN_FORMAT = 'train_step'
FWD_PARAMS = ['x', 'c', 'ctx', 'c_ctx', 'ada_w', 'ada_b', 'norm_g', 'mlp_w1', 'mlp_w2', 'pool_w', 'pool_scale', 'attn_w_qkv', 'attn_w_o', 'attn_q_g', 'attn_k_g', 'gm_w_in', 'gm_ln_g', 'gm_ln_b', 'gm_ws', 'gm_bs', 'gm_w_out', 'final_g']
TWIN_WEIGHTS = ['c_ctx', 'ada_w', 'ada_b', 'norm_g', 'mlp_w1', 'mlp_w2', 'pool_w', 'pool_scale', 'attn_w_qkv', 'attn_w_o', 'attn_q_g', 'attn_k_g', 'gm_w_in', 'gm_ln_g', 'gm_ln_b', 'gm_ws', 'gm_bs', 'gm_w_out', 'final_g']
TWIN_DIFF_INPUT = 'x'
TWIN_INPUTS = ['x', 'c', 'ctx', 'c_ctx', 'ada_w', 'ada_b', 'norm_g', 'mlp_w1', 'mlp_w2', 'pool_w', 'pool_scale', 'attn_w_qkv', 'attn_w_o', 'attn_q_g', 'attn_k_g', 'gm_w_in', 'gm_ln_g', 'gm_ln_b', 'gm_ws', 'gm_bs', 'gm_w_out', 'final_g', 'loss_target', 'm_c_ctx', 'm_ada_w', 'm_ada_b', 'm_norm_g', 'm_mlp_w1', 'm_mlp_w2', 'm_pool_w', 'm_pool_scale', 'm_attn_w_qkv', 'm_attn_w_o', 'm_attn_q_g', 'm_attn_k_g', 'm_gm_w_in', 'm_gm_ln_g', 'm_gm_ln_b', 'm_gm_ws', 'm_gm_bs', 'm_gm_w_out', 'm_final_g', 'v_c_ctx', 'v_ada_w', 'v_ada_b', 'v_norm_g', 'v_mlp_w1', 'v_mlp_w2', 'v_pool_w', 'v_pool_scale', 'v_attn_w_qkv', 'v_attn_w_o', 'v_attn_q_g', 'v_attn_k_g', 'v_gm_w_in', 'v_gm_ln_g', 'v_gm_ln_b', 'v_gm_ws', 'v_gm_bs', 'v_gm_w_out', 'v_final_g']
TWIN_OUTPUTS = ['loss', 'grad_x', 'grad_c_ctx', 'grad_ada_w', 'grad_ada_b', 'grad_norm_g', 'grad_mlp_w1', 'grad_mlp_w2', 'grad_pool_w', 'grad_pool_scale', 'grad_attn_w_qkv', 'grad_attn_w_o', 'grad_attn_q_g', 'grad_attn_k_g', 'grad_gm_w_in', 'grad_gm_ln_g', 'grad_gm_ln_b', 'grad_gm_ws', 'grad_gm_bs', 'grad_gm_w_out', 'grad_final_g', 'delta_c_ctx', 'delta_ada_w', 'delta_ada_b', 'delta_norm_g', 'delta_mlp_w1', 'delta_mlp_w2', 'delta_pool_w', 'delta_pool_scale', 'delta_attn_w_qkv', 'delta_attn_w_o', 'delta_attn_q_g', 'delta_attn_k_g', 'delta_gm_w_in', 'delta_gm_ln_g', 'delta_gm_ln_b', 'delta_gm_ws', 'delta_gm_bs', 'delta_gm_w_out', 'delta_final_g', 'new_m_c_ctx', 'new_m_ada_w', 'new_m_ada_b', 'new_m_norm_g', 'new_m_mlp_w1', 'new_m_mlp_w2', 'new_m_pool_w', 'new_m_pool_scale', 'new_m_attn_w_qkv', 'new_m_attn_w_o', 'new_m_attn_q_g', 'new_m_attn_k_g', 'new_m_gm_w_in', 'new_m_gm_ln_g', 'new_m_gm_ln_b', 'new_m_gm_ws', 'new_m_gm_bs', 'new_m_gm_w_out', 'new_m_final_g', 'new_v_c_ctx', 'new_v_ada_w', 'new_v_ada_b', 'new_v_norm_g', 'new_v_mlp_w1', 'new_v_mlp_w2', 'new_v_pool_w', 'new_v_pool_scale', 'new_v_attn_w_qkv', 'new_v_attn_w_o', 'new_v_attn_q_g', 'new_v_attn_k_g', 'new_v_gm_w_in', 'new_v_gm_ln_g', 'new_v_gm_ln_b', 'new_v_gm_ws', 'new_v_gm_bs', 'new_v_gm_w_out', 'new_v_final_g']
TWIN_LEAF_KINDS = {'loss': 'loss', 'grad_x': 'grad_x', 'grad_c_ctx': 'grad_w', 'grad_ada_w': 'grad_w', 'grad_ada_b': 'grad_w', 'grad_norm_g': 'grad_w', 'grad_mlp_w1': 'grad_w', 'grad_mlp_w2': 'grad_w', 'grad_pool_w': 'grad_w', 'grad_pool_scale': 'grad_w', 'grad_attn_w_qkv': 'grad_w', 'grad_attn_w_o': 'grad_w', 'grad_attn_q_g': 'grad_w', 'grad_attn_k_g': 'grad_w', 'grad_gm_w_in': 'grad_w', 'grad_gm_ln_g': 'grad_w', 'grad_gm_ln_b': 'grad_w', 'grad_gm_ws': 'grad_w', 'grad_gm_bs': 'grad_w', 'grad_gm_w_out': 'grad_w', 'grad_final_g': 'grad_w', 'delta_c_ctx': 'delta_w', 'delta_ada_w': 'delta_w', 'delta_ada_b': 'delta_w', 'delta_norm_g': 'delta_w', 'delta_mlp_w1': 'delta_w', 'delta_mlp_w2': 'delta_w', 'delta_pool_w': 'delta_w', 'delta_pool_scale': 'delta_w', 'delta_attn_w_qkv': 'delta_w', 'delta_attn_w_o': 'delta_w', 'delta_attn_q_g': 'delta_w', 'delta_attn_k_g': 'delta_w', 'delta_gm_w_in': 'delta_w', 'delta_gm_ln_g': 'delta_w', 'delta_gm_ln_b': 'delta_w', 'delta_gm_ws': 'delta_w', 'delta_gm_bs': 'delta_w', 'delta_gm_w_out': 'delta_w', 'delta_final_g': 'delta_w', 'new_m_c_ctx': 'new_m', 'new_m_ada_w': 'new_m', 'new_m_ada_b': 'new_m', 'new_m_norm_g': 'new_m', 'new_m_mlp_w1': 'new_m', 'new_m_mlp_w2': 'new_m', 'new_m_pool_w': 'new_m', 'new_m_pool_scale': 'new_m', 'new_m_attn_w_qkv': 'new_m', 'new_m_attn_w_o': 'new_m', 'new_m_attn_q_g': 'new_m', 'new_m_attn_k_g': 'new_m', 'new_m_gm_w_in': 'new_m', 'new_m_gm_ln_g': 'new_m', 'new_m_gm_ln_b': 'new_m', 'new_m_gm_ws': 'new_m', 'new_m_gm_bs': 'new_m', 'new_m_gm_w_out': 'new_m', 'new_m_final_g': 'new_m', 'new_v_c_ctx': 'new_v', 'new_v_ada_w': 'new_v', 'new_v_ada_b': 'new_v', 'new_v_norm_g': 'new_v', 'new_v_mlp_w1': 'new_v', 'new_v_mlp_w2': 'new_v', 'new_v_pool_w': 'new_v', 'new_v_pool_scale': 'new_v', 'new_v_attn_w_qkv': 'new_v', 'new_v_attn_w_o': 'new_v', 'new_v_attn_q_g': 'new_v', 'new_v_attn_k_g': 'new_v', 'new_v_gm_w_in': 'new_v', 'new_v_gm_ln_g': 'new_v', 'new_v_gm_ln_b': 'new_v', 'new_v_gm_ws': 'new_v', 'new_v_gm_bs': 'new_v', 'new_v_gm_w_out': 'new_v', 'new_v_final_g': 'new_v'}


def _forward(args):
    return _fwd_reference(*[args[k] for k in FWD_PARAMS])


def _output_shape():
    def fwd():
        inp = _fwd_setup_inputs(0)
        return _fwd_reference(*[inp[k] for k in FWD_PARAMS])
    out = _jax.eval_shape(fwd)
    return out.shape, out.dtype

N_MICROBATCH = 1
ADAM_LR = 0.001
ADAM_B1 = 0.9
ADAM_B2 = 0.999
ADAM_EPS = 1e-08
ADAM_WD = 0.01
ADAM_STEP = 10
PER_EXAMPLE_BATCH_AXIS = {'x': 0, 'c': 0, 'ctx': 0, 'loss_target': 0}
SHARED_INPUTS = []
_WEIGHT_DTYPES = {'c_ctx': _jnp.float32, 'ada_w': _jnp.float32, 'ada_b': _jnp.float32, 'norm_g': _jnp.float32, 'mlp_w1': _jnp.float32, 'mlp_w2': _jnp.float32, 'pool_w': _jnp.float32, 'pool_scale': _jnp.float32, 'attn_w_qkv': _jnp.float32, 'attn_w_o': _jnp.float32, 'attn_q_g': _jnp.float32, 'attn_k_g': _jnp.float32, 'gm_w_in': _jnp.float32, 'gm_ln_g': _jnp.float32, 'gm_ln_b': _jnp.float32, 'gm_ws': _jnp.float32, 'gm_bs': _jnp.float32, 'gm_w_out': _jnp.float32, 'final_g': _jnp.float32}
MOMENT_SCALE = {'c_ctx': 1.403767e-02, 'ada_w': 1.044378e-01, 'ada_b': 1.803718e-01, 'norm_g': 8.299129e-02, 'mlp_w1': 5.298256e-02, 'mlp_w2': 9.702621e-02, 'pool_w': 6.209831e-02, 'pool_scale': 9.033070e-02, 'attn_w_qkv': 1.922320e-02, 'attn_w_o': 2.665488e-02, 'attn_q_g': 1.920080e-02, 'attn_k_g': 1.902172e-02, 'gm_w_in': 3.536057e-02, 'gm_ln_g': 3.174855e-02, 'gm_ln_b': 3.337156e-02, 'gm_ws': 4.582532e-02, 'gm_bs': 4.757628e-02, 'gm_w_out': 4.572023e-02, 'final_g': 6.461233e+01}


def _to_microbatches(a, axis):
    t = _jnp.moveaxis(a, axis, 0)
    t = t.reshape((N_MICROBATCH, t.shape[0] // N_MICROBATCH) + t.shape[1:])
    return _jnp.moveaxis(t, 1, axis + 1)


def setup_inputs(seed: int = 0) -> dict:
    inp = _fwd_setup_inputs(seed)
    key = _jax.random.fold_in(_jax.random.key(seed), 7919)
    shape, _ = _output_shape()
    out = dict(inp)
    out["loss_target"] = _jax.random.normal(_jax.random.fold_in(key, 0), shape, _jnp.float32)
    for i, name in enumerate(TWIN_WEIGHTS):
        w = inp[name].astype(_jnp.float32)
        if MOMENT_SCALE is None:
            s = _jnp.sqrt(_jnp.mean(_jnp.square(w)) + 1e-30)
        else:
            s = MOMENT_SCALE[name]
        km, kv = _jax.random.split(_jax.random.fold_in(key, i + 1))
        out[name] = w
        out["m_" + name] = s * _jax.random.normal(km, w.shape, _jnp.float32)
        out["v_" + name] = (s * s) * _jax.random.uniform(kv, w.shape, _jnp.float32, 0.5, 1.5)
    if N_MICROBATCH > 1:
        for name, axis in PER_EXAMPLE_BATCH_AXIS.items():
            out[name] = _to_microbatches(out[name], axis)
    return {'x': out['x'], 'c': out['c'], 'ctx': out['ctx'], 'c_ctx': out['c_ctx'], 'ada_w': out['ada_w'], 'ada_b': out['ada_b'], 'norm_g': out['norm_g'], 'mlp_w1': out['mlp_w1'], 'mlp_w2': out['mlp_w2'], 'pool_w': out['pool_w'], 'pool_scale': out['pool_scale'], 'attn_w_qkv': out['attn_w_qkv'], 'attn_w_o': out['attn_w_o'], 'attn_q_g': out['attn_q_g'], 'attn_k_g': out['attn_k_g'], 'gm_w_in': out['gm_w_in'], 'gm_ln_g': out['gm_ln_g'], 'gm_ln_b': out['gm_ln_b'], 'gm_ws': out['gm_ws'], 'gm_bs': out['gm_bs'], 'gm_w_out': out['gm_w_out'], 'final_g': out['final_g'], 'loss_target': out['loss_target'], 'm_c_ctx': out['m_c_ctx'], 'm_ada_w': out['m_ada_w'], 'm_ada_b': out['m_ada_b'], 'm_norm_g': out['m_norm_g'], 'm_mlp_w1': out['m_mlp_w1'], 'm_mlp_w2': out['m_mlp_w2'], 'm_pool_w': out['m_pool_w'], 'm_pool_scale': out['m_pool_scale'], 'm_attn_w_qkv': out['m_attn_w_qkv'], 'm_attn_w_o': out['m_attn_w_o'], 'm_attn_q_g': out['m_attn_q_g'], 'm_attn_k_g': out['m_attn_k_g'], 'm_gm_w_in': out['m_gm_w_in'], 'm_gm_ln_g': out['m_gm_ln_g'], 'm_gm_ln_b': out['m_gm_ln_b'], 'm_gm_ws': out['m_gm_ws'], 'm_gm_bs': out['m_gm_bs'], 'm_gm_w_out': out['m_gm_w_out'], 'm_final_g': out['m_final_g'], 'v_c_ctx': out['v_c_ctx'], 'v_ada_w': out['v_ada_w'], 'v_ada_b': out['v_ada_b'], 'v_norm_g': out['v_norm_g'], 'v_mlp_w1': out['v_mlp_w1'], 'v_mlp_w2': out['v_mlp_w2'], 'v_pool_w': out['v_pool_w'], 'v_pool_scale': out['v_pool_scale'], 'v_attn_w_qkv': out['v_attn_w_qkv'], 'v_attn_w_o': out['v_attn_w_o'], 'v_attn_q_g': out['v_attn_q_g'], 'v_attn_k_g': out['v_attn_k_g'], 'v_gm_w_in': out['v_gm_w_in'], 'v_gm_ln_g': out['v_gm_ln_g'], 'v_gm_ln_b': out['v_gm_ln_b'], 'v_gm_ws': out['v_gm_ws'], 'v_gm_bs': out['v_gm_bs'], 'v_gm_w_out': out['v_gm_w_out'], 'v_final_g': out['v_final_g']}


def _loss(weights, diff, rest, loss_target):
    with _jax.named_scope("forward"):
        args = {**rest, TWIN_DIFF_INPUT: diff, **{k: w.astype(_WEIGHT_DTYPES[k]) for k, w in weights.items()}}
        y = _forward(args)
    with _jax.named_scope("loss_head"):
        err = _jnp.square(y.astype(_jnp.float32) - loss_target)
        return 0.5 * _jnp.sum(_jnp.mean(err, axis=-1)) if err.ndim else 0.5 * err


def _adamw(w, g, m, v):
    m = ADAM_B1 * m + (1.0 - ADAM_B1) * g
    v = ADAM_B2 * v + (1.0 - ADAM_B2) * _jnp.square(g)
    m_hat = m / (1.0 - ADAM_B1 ** ADAM_STEP)
    v_hat = v / (1.0 - ADAM_B2 ** ADAM_STEP)
    delta = -ADAM_LR * (m_hat / (_jnp.sqrt(v_hat) + ADAM_EPS) + ADAM_WD * w)
    return delta, m, v


def reference(x, c, ctx, c_ctx, ada_w, ada_b, norm_g, mlp_w1, mlp_w2, pool_w, pool_scale, attn_w_qkv, attn_w_o, attn_q_g, attn_k_g, gm_w_in, gm_ln_g, gm_ln_b, gm_ws, gm_bs, gm_w_out, final_g, loss_target, m_c_ctx, m_ada_w, m_ada_b, m_norm_g, m_mlp_w1, m_mlp_w2, m_pool_w, m_pool_scale, m_attn_w_qkv, m_attn_w_o, m_attn_q_g, m_attn_k_g, m_gm_w_in, m_gm_ln_g, m_gm_ln_b, m_gm_ws, m_gm_bs, m_gm_w_out, m_final_g, v_c_ctx, v_ada_w, v_ada_b, v_norm_g, v_mlp_w1, v_mlp_w2, v_pool_w, v_pool_scale, v_attn_w_qkv, v_attn_w_o, v_attn_q_g, v_attn_k_g, v_gm_w_in, v_gm_ln_g, v_gm_ln_b, v_gm_ws, v_gm_bs, v_gm_w_out, v_final_g):
    given = dict(x=x, c=c, ctx=ctx, c_ctx=c_ctx, ada_w=ada_w, ada_b=ada_b, norm_g=norm_g, mlp_w1=mlp_w1, mlp_w2=mlp_w2, pool_w=pool_w, pool_scale=pool_scale, attn_w_qkv=attn_w_qkv, attn_w_o=attn_w_o, attn_q_g=attn_q_g, attn_k_g=attn_k_g, gm_w_in=gm_w_in, gm_ln_g=gm_ln_g, gm_ln_b=gm_ln_b, gm_ws=gm_ws, gm_bs=gm_bs, gm_w_out=gm_w_out, final_g=final_g, loss_target=loss_target, m_c_ctx=m_c_ctx, m_ada_w=m_ada_w, m_ada_b=m_ada_b, m_norm_g=m_norm_g, m_mlp_w1=m_mlp_w1, m_mlp_w2=m_mlp_w2, m_pool_w=m_pool_w, m_pool_scale=m_pool_scale, m_attn_w_qkv=m_attn_w_qkv, m_attn_w_o=m_attn_w_o, m_attn_q_g=m_attn_q_g, m_attn_k_g=m_attn_k_g, m_gm_w_in=m_gm_w_in, m_gm_ln_g=m_gm_ln_g, m_gm_ln_b=m_gm_ln_b, m_gm_ws=m_gm_ws, m_gm_bs=m_gm_bs, m_gm_w_out=m_gm_w_out, m_final_g=m_final_g, v_c_ctx=v_c_ctx, v_ada_w=v_ada_w, v_ada_b=v_ada_b, v_norm_g=v_norm_g, v_mlp_w1=v_mlp_w1, v_mlp_w2=v_mlp_w2, v_pool_w=v_pool_w, v_pool_scale=v_pool_scale, v_attn_w_qkv=v_attn_w_qkv, v_attn_w_o=v_attn_w_o, v_attn_q_g=v_attn_q_g, v_attn_k_g=v_attn_k_g, v_gm_w_in=v_gm_w_in, v_gm_ln_g=v_gm_ln_g, v_gm_ln_b=v_gm_ln_b, v_gm_ws=v_gm_ws, v_gm_bs=v_gm_bs, v_gm_w_out=v_gm_w_out, v_final_g=v_final_g)
    weights = {n: given[n] for n in TWIN_WEIGHTS}
    shared = {n: given[n] for n in SHARED_INPUTS}
    per_example = {n: given[n] for n in ['x', 'c', 'ctx']}
    grad_fn = _jax.value_and_grad(_loss, argnums=(0, 1))

    def one_microbatch(ex, loss_target):
        ex = dict(ex)
        diff = ex.pop(TWIN_DIFF_INPUT)
        return grad_fn(weights, diff, {**shared, **ex}, loss_target)

    if N_MICROBATCH == 1:
        loss, (grad_w, grad_x) = one_microbatch(per_example, given["loss_target"])
    else:
        def body(carry, xs):
            loss_sum, grad_sum = carry
            l_k, (gw_k, gx_k) = one_microbatch(xs[0], xs[1])
            with _jax.named_scope("update"):
                return (loss_sum + l_k, _jax.tree.map(_jnp.add, grad_sum, gw_k)), gx_k

        init = (_jnp.zeros((), _jnp.float32), _jax.tree.map(_jnp.zeros_like, weights))
        (loss, grad_w), grad_x = _jax.lax.scan(body, init, (per_example, given["loss_target"]))
    with _jax.named_scope("update"):
        delta_w, new_m, new_v = {}, {}, {}
        for n in TWIN_WEIGHTS:
            delta_w[n], new_m[n], new_v[n] = _adamw(weights[n], grad_w[n], given["m_" + n], given["v_" + n])
    return (loss, grad_x, *[grad_w[n] for n in TWIN_WEIGHTS], *[delta_w[n] for n in TWIN_WEIGHTS],
            *[new_m[n] for n in TWIN_WEIGHTS], *[new_v[n] for n in TWIN_WEIGHTS])
```

```python
import functools
import math

import numpy as np
import jax
import jax.numpy as jnp
from jax import lax
from jax.experimental import pallas as pl
from jax.experimental.pallas import tpu as pltpu

F32 = jnp.float32
BF16 = jnp.bfloat16
MESH_ID = pl.DeviceIdType.MESH

N_DEV = 8
EPS = 1e-6
HEAD_DIM = 128
GRID_W = 64
ROPE_BASE = 10000.0
CHUNK = 128
POOL_WINDOWS = (2, 4, 8, 16)
POOL_GROUPS = 4
POOL_HALO = 8
GMLP_GROUPS = 8
ADAM_LR, ADAM_B1, ADAM_B2, ADAM_EPS, ADAM_WD, ADAM_STEP = 0.001, 0.9, 0.999, 1e-08, 0.01, 10

V7X_VMEM_BYTES = 64 << 20
VMEM_LIMIT_BIG = V7X_VMEM_BYTES - (8 << 20)
FLASH_TK_CAP = 768
MLP_FC = 512


def _params(sem, vmem=VMEM_LIMIT_BIG):
    return pltpu.CompilerParams(dimension_semantics=sem, vmem_limit_bytes=vmem)


def _const_spec(shape):
    nd = len(shape)
    return pl.BlockSpec(shape, lambda *_: (0,) * nd, pipeline_mode=pl.Buffered(1))


def _dot(a, b):
    return jnp.dot(a, b, preferred_element_type=F32)


def _dot_nt(a, b):
    return lax.dot_general(a, b, (((1,), (1,)), ((), ())), preferred_element_type=F32)


def _dot_tn(a, b):
    return lax.dot_general(a, b, (((0,), (0,)), ((), ())), preferred_element_type=F32)


def _colsum(x):
    return jnp.sum(x, axis=0, keepdims=True)


def _sid(i, nct):
    if nct == 0:
        return 1
    return jnp.where(i >= nct, 1, 0)


def _n_streams(nct):
    return 2 if nct else 1


def _stat_sid(i, nct):
    return _sid(i, nct) if nct else 0


def _first_of_stream(i, nct):
    if nct == 0:
        return i == 0
    return jnp.logical_or(i == 0, i == nct)


def _normmod(h, ng, sh, sc):
    rstd = lax.rsqrt(jnp.mean(h * h, axis=-1, keepdims=True) + EPS)
    xhat = h * rstd
    n = xhat * ng
    return n * (1.0 + sc) + sh, xhat, rstd, n


def _normmod_bwd(da, xhat, rstd, n, ng, sc):
    dsh = _colsum(da)
    dsc = _colsum(da * n)
    dn = da * (1.0 + sc)
    dng = _colsum(dn * xhat)
    dxhat = dn * ng
    dh = rstd * (dxhat - xhat * jnp.mean(dxhat * xhat, axis=-1, keepdims=True))
    return dh, dsh, dsc, dng


def _acc_rows(ref, first, rows):
    @pl.when(first)
    def _():
        ref[...] = jnp.zeros_like(ref)
    for r, val in enumerate(rows):
        ref[r:r + 1, :] = ref[r:r + 1, :] + val


_GELU_C = math.sqrt(2.0 / math.pi)


def _gelu(x):
    t = jnp.tanh(_GELU_C * (x + 0.044715 * x * x * x))
    return 0.5 * x * (1.0 + t)


def _gelu_grad(x):
    x2 = x * x
    t = jnp.tanh(_GELU_C * (x + 0.044715 * x * x2))
    return 0.5 * (1.0 + t) + 0.5 * x * (1.0 - t * t) * _GELU_C * (1.0 + 3.0 * 0.044715 * x2)


def _coords():
    return lax.axis_index("x"), lax.axis_index("y"), lax.axis_index("c")


def _dev_index(px, py, pc):
    return 4 * px + 2 * py + pc


def _all_gather(xs, name):
    n = len(xs)

    def body(*refs):
        x_refs, o_refs = refs[:n], refs[n:2 * n]
        send_sems, recv_sems, local_sems = refs[2 * n:]
        x, y, c = _coords()
        me, sibling = (x, y, c), (x, y, 1 - c)
        chips = [(1 - x, y), (x, 1 - y), (1 - x, 1 - y)]

        def copy(a, k, block, to, src=None):
            dst = o_refs[a].at[_dev_index(*block)]
            return pltpu.make_async_remote_copy(
                src_ref=dst if src is None else src, dst_ref=dst,
                send_sem=send_sems.at[7 * a + k], recv_sem=recv_sems.at[7 * a + k],
                device_id=to, device_id_type=MESH_ID)

        mine = [pltpu.make_async_copy(x_refs[a], o_refs[a].at[_dev_index(*me)], local_sems.at[a])
                for a in range(n)]
        for cp in mine:
            cp.start()
        first = []
        for a in range(n):
            first.append(copy(a, 0, me, sibling, src=x_refs[a]))
            first += [copy(a, 1 + j, me, (*chip, c), src=x_refs[a]) for j, chip in enumerate(chips)]
        for cp in first:
            cp.start()
        passed = []
        for a in range(n):
            for j, chip in enumerate(chips):
                copy(a, 1 + j, (*chip, c), me).wait_recv()
                fwd = copy(a, 4 + j, (*chip, c), sibling)
                fwd.start()
                passed.append(fwd)
        for a in range(n):
            copy(a, 0, sibling, me).wait_recv()
            for j, chip in enumerate(chips):
                copy(a, 4 + j, (*chip, 1 - c), me).wait_recv()
        for cp in first + passed:
            cp.wait_send()
        for cp in mine:
            cp.wait()

    any_spec = pl.BlockSpec(memory_space=pl.ANY)
    outs = pl.pallas_call(
        body, name=name,
        out_shape=[jax.ShapeDtypeStruct((N_DEV,) + x.shape, x.dtype) for x in xs],
        in_specs=[any_spec] * n, out_specs=[any_spec] * n,
        scratch_shapes=[pltpu.SemaphoreType.DMA((7 * n,)), pltpu.SemaphoreType.DMA((7 * n,)),
                        pltpu.SemaphoreType.DMA((n,))],
    )(*xs)
    return list(outs)


def _all_to_all(xs, name):
    n = len(xs)

    def body(*refs):
        x_refs, o_refs = refs[:n], refs[n:2 * n]
        send_sems, recv_sems, local_sems = refs[2 * n:]
        x, y, c = _coords()
        me_i = _dev_index(x, y, c)

        def peer(r):
            px = 1 - x if r & 4 else x
            py = 1 - y if r & 2 else y
            pc = 1 - c if r & 1 else c
            return (px, py, pc)

        def copy(a, r):
            p = peer(r)
            p_i = _dev_index(*p)
            return pltpu.make_async_remote_copy(
                src_ref=x_refs[a].at[p_i], dst_ref=o_refs[a].at[me_i],
                send_sem=send_sems.at[7 * a + r - 1], recv_sem=recv_sems.at[7 * a + r - 1],
                device_id=p, device_id_type=MESH_ID)

        def arrival(a, r):
            p = peer(r)
            p_i = _dev_index(*p)
            return pltpu.make_async_remote_copy(
                src_ref=x_refs[a].at[p_i], dst_ref=o_refs[a].at[p_i],
                send_sem=send_sems.at[7 * a + r - 1], recv_sem=recv_sems.at[7 * a + r - 1],
                device_id=p, device_id_type=MESH_ID)

        mine = [pltpu.make_async_copy(x_refs[a].at[me_i], o_refs[a].at[me_i], local_sems.at[a])
                for a in range(n)]
        for cp in mine:
            cp.start()
        sends = [copy(a, r) for a in range(n) for r in range(1, 8)]
        for cp in sends:
            cp.start()
        for a in range(n):
            for r in range(1, 8):
                arrival(a, r).wait_recv()
        for cp in sends:
            cp.wait_send()
        for cp in mine:
            cp.wait()

    any_spec = pl.BlockSpec(memory_space=pl.ANY)
    outs = pl.pallas_call(
        body, name=name,
        out_shape=[jax.ShapeDtypeStruct(x.shape, x.dtype) for x in xs],
        in_specs=[any_spec] * n, out_specs=[any_spec] * n,
        scratch_shapes=[pltpu.SemaphoreType.DMA((7 * n,)), pltpu.SemaphoreType.DMA((7 * n,)),
                        pltpu.SemaphoreType.DMA((n,))],
    )(*xs)
    return list(outs)


def _silu(x):
    return x * (1.0 / (1.0 + jnp.exp(-x)))


def _cond_rows(c_all, c_ctx):
    d = c_all.shape[-1]
    s = jnp.concatenate([c_all, jnp.zeros((8, d), F32)], axis=0)
    row = lax.broadcasted_iota(jnp.int32, (16, d), 0)
    s = jnp.where(row == 8, c_ctx, s)
    return jnp.where(row <= 8, _silu(s), 0.0)


def _mods_local(c_all, c_ctx, ada_w, ada_b_loc):
    nl, d, n = ada_w.shape

    def body(c_ref, cc_ref, w_ref, b_ref, o_ref):
        s = _cond_rows(c_ref[...], cc_ref[...])
        o_ref[...] = jnp.dot(s, w_ref[...], preferred_element_type=F32,
                             precision=lax.Precision.HIGHEST) + b_ref[...]

    return pl.pallas_call(
        body, name="mods_local", grid=(nl,),
        out_shape=jax.ShapeDtypeStruct((nl, 16, n), F32),
        in_specs=[pl.BlockSpec((8, d), lambda i: (0, 0)), pl.BlockSpec((1, d), lambda i: (0, 0)),
                  pl.BlockSpec((None, d, n), lambda i: (i, 0, 0)),
                  pl.BlockSpec((None, 1, n), lambda i: (i, 0, 0))],
        out_specs=pl.BlockSpec((None, 16, n), lambda i: (i, 0, 0)),
        compiler_params=_params(("arbitrary",)),
    )(c_all, c_ctx, ada_w, ada_b_loc)


def _ada_grads(c_all, c_ctx, ada_w, dm_lat, dm_ctx):
    nl, d, n = ada_w.shape

    def body(c_ref, cc_ref, w_ref, dml_ref, dmc_ref, gw_ref, ds_ref):
        i = pl.program_id(0)
        s = _cond_rows(c_ref[...], cc_ref[...])
        csum = dmc_ref[0:1, :]
        for k in range(1, N_DEV):
            csum = csum + dmc_ref[k:k + 1, :]
        row = lax.broadcasted_iota(jnp.int32, (8, n), 0)
        dm_c = jnp.where(row == 0, csum, 0.0)
        dm = jnp.concatenate([dml_ref[...], dm_c], axis=0)
        gw_ref[...] = lax.dot_general(s, dm, (((0,), (0,)), ((), ())), preferred_element_type=F32,
                                      precision=lax.Precision.HIGHEST)
        ds = lax.dot_general(dm_c, w_ref[...], (((1,), (1,)), ((), ())),
                             preferred_element_type=F32, precision=lax.Precision.HIGHEST)

        @pl.when(i == 0)
        def _():
            ds_ref[...] = jnp.zeros_like(ds_ref)
        ds_ref[...] += ds

    return pl.pallas_call(
        body, name="ada_grads", grid=(nl,),
        out_shape=[jax.ShapeDtypeStruct((nl, d, n), F32), jax.ShapeDtypeStruct((8, d), F32)],
        in_specs=[pl.BlockSpec((8, d), lambda i: (0, 0)), pl.BlockSpec((1, d), lambda i: (0, 0)),
                  pl.BlockSpec((None, d, n), lambda i: (i, 0, 0)),
                  pl.BlockSpec((None, 8, n), lambda i: (i, 0, 0)),
                  pl.BlockSpec((None, 8, n), lambda i: (i, 0, 0))],
        out_specs=[pl.BlockSpec((None, d, n), lambda i: (i, 0, 0)),
                   pl.BlockSpec((8, d), lambda i: (0, 0))],
        compiler_params=_params(("arbitrary",)),
    )(c_all, c_ctx, ada_w, dm_lat, dm_ctx)


def _cctx_grad(ds_parts, c_ctx):
    d = c_ctx.shape[-1]

    def body(p_ref, c_ref, o_ref):
        ds = p_ref[0]
        for k in range(1, N_DEV):
            ds = ds + p_ref[k]
        x = c_ref[...]
        sg = 1.0 / (1.0 + jnp.exp(-x))
        o_ref[...] = ds[0:1, :] * (sg * (1.0 + x * (1.0 - sg)))

    return pl.pallas_call(body, name="cctx_grad", out_shape=jax.ShapeDtypeStruct((1, d), F32))(ds_parts, c_ctx)


def _mlp_fwd(h1, mods, ng, w1, w2, nct, tm, row_off=0):
    d = h1.shape[1]
    r = h1.shape[0] - row_off * tm
    f = w1.shape[1]
    fc = min(MLP_FC, f)

    def body(h_ref, mod_ref, ng_ref, w1_ref, w2_ref, h2_ref, p_ref, y_ref):
        h = h_ref[...]
        a, _, _, _ = _normmod(h, ng_ref[...], mod_ref[3:4, :], mod_ref[4:5, :])
        ab = a.astype(BF16)
        acc = jnp.zeros((tm, d), F32)
        for j in range(f // fc):
            sl = slice(j * fc, (j + 1) * fc)
            p = jnp.maximum(_dot(ab, w1_ref[:, sl]), 0.0)
            p_ref[:, sl] = p.astype(BF16)
            acc = acc + _dot((p * p).astype(BF16), w2_ref[sl, :])
        y_ref[...] = acc.astype(BF16)
        h2_ref[...] = h + mod_ref[5:6, :] * acc

    return pl.pallas_call(
        body, name="mlp_fwd", grid=(r // tm,),
        out_shape=[jax.ShapeDtypeStruct((r, d), F32), jax.ShapeDtypeStruct((r, f), BF16),
                   jax.ShapeDtypeStruct((r, d), BF16)],
        in_specs=[pl.BlockSpec((tm, d), lambda i: (i + row_off, 0)),
                  pl.BlockSpec((None, 8, d), lambda i: (_sid(i, nct), 0, 0)),
                  _const_spec((1, d)), _const_spec(w1.shape), _const_spec(w2.shape)],
        out_specs=[pl.BlockSpec((tm, d), lambda i: (i, 0)), pl.BlockSpec((tm, f), lambda i: (i, 0)),
                   pl.BlockSpec((tm, d), lambda i: (i, 0))],
        compiler_params=_params(("arbitrary",)),
    )(h1, mods, ng, w1, w2)


def _mlp_bwd(dh2, h1, p, y, mods, ng, w1, w2, nct, tm, row_off=0):
    r_rows, d = dh2.shape
    f = w1.shape[1]
    fc = min(MLP_FC, f)

    def body(dh_ref, h_ref, p_ref, y_ref, mod_ref, ng_ref, w1_ref, w2_ref,
             dh1_ref, m_ref, du_ref, r_ref, dacc_ref, st_ref):
        i = pl.program_id(0)
        dh = dh_ref[...]
        ngv, sc, gate = ng_ref[...], mod_ref[4:5, :], mod_ref[5:6, :]
        a, xhat, rstd, n = _normmod(h_ref[...], ngv, mod_ref[3:4, :], sc)
        m_ref[...] = a.astype(BF16)
        dgate = _colsum(dh * y_ref[...].astype(F32))
        dacc = (gate * dh).astype(BF16)
        dacc_ref[...] = dacc
        dm = jnp.zeros((tm, d), F32)
        for j in range(f // fc):
            sl = slice(j * fc, (j + 1) * fc)
            pj = p_ref[:, sl].astype(F32)
            du = (_dot_nt(dacc, w2_ref[sl, :]) * (2.0 * pj)).astype(BF16)
            du_ref[:, sl] = du
            r_ref[:, sl] = (pj * pj).astype(BF16)
            dm = dm + _dot_nt(du, w1_ref[:, sl])
        dhn, dsh, dsc, dng = _normmod_bwd(dm, xhat, rstd, n, ngv, sc)
        dh1_ref[...] = dh + dhn
        _acc_rows(st_ref, _first_of_stream(i, nct), [dsh, dsc, dgate, dng])

    return pl.pallas_call(
        body, name="mlp_bwd", grid=(r_rows // tm,),
        out_shape=[jax.ShapeDtypeStruct((r_rows, d), F32), jax.ShapeDtypeStruct((r_rows, d), BF16),
                   jax.ShapeDtypeStruct((r_rows, f), BF16), jax.ShapeDtypeStruct((r_rows, f), BF16),
                   jax.ShapeDtypeStruct((r_rows, d), BF16), jax.ShapeDtypeStruct((_n_streams(nct), 8, d), F32)],
        in_specs=[pl.BlockSpec((tm, d), lambda i: (i, 0)),
                  pl.BlockSpec((tm, d), lambda i: (i + row_off, 0)),
                  pl.BlockSpec((tm, f), lambda i: (i, 0)), pl.BlockSpec((tm, d), lambda i: (i, 0)),
                  pl.BlockSpec((None, 8, d), lambda i: (_sid(i, nct), 0, 0)),
                  _const_spec((1, d)), _const_spec(w1.shape), _const_spec(w2.shape)],
        out_specs=[pl.BlockSpec((tm, d), lambda i: (i, 0)), pl.BlockSpec((tm, d), lambda i: (i, 0)),
                   pl.BlockSpec((tm, f), lambda i: (i, 0)), pl.BlockSpec((tm, f), lambda i: (i, 0)),
                   pl.BlockSpec((tm, d), lambda i: (i, 0)),
                   pl.BlockSpec((None, 8, d), lambda i: (_stat_sid(i, nct), 0, 0))],
        compiler_params=_params(("arbitrary",)),
    )(dh2, h1, p, y, mods, ng, w1, w2)


def _pick(n, cands):
    for cand in cands:
        if n % cand == 0:
            return cand
    return n


def _tn_matmul(x, y, name, groups=0):
    rows, k1 = x.shape
    k2 = y.shape[1]
    bt = _pick(rows, (1024, 768, 512, 384, 256, 128))
    if groups:
        bk1, bk2 = k1 // groups, k2 // groups
        grid = (groups, 1, rows // bt)
        x_map = lambda g, j, t: (t, g)
        y_map = lambda g, j, t: (t, g)
        out_shape = jax.ShapeDtypeStruct((groups, bk1, bk2), BF16)
        out_spec = pl.BlockSpec((None, bk1, bk2), lambda g, j, t: (g, 0, 0))
    else:
        bk1, bk2 = min(k1, 1024), min(k2, 1024)
        grid = (k1 // bk1, k2 // bk2, rows // bt)
        x_map = lambda i, j, t: (t, i)
        y_map = lambda i, j, t: (t, j)
        out_shape = jax.ShapeDtypeStruct((k1, k2), BF16)
        out_spec = pl.BlockSpec((bk1, bk2), lambda i, j, t: (i, j))
    nt = rows // bt

    def body(x_ref, y_ref, o_ref, acc_ref):
        t = pl.program_id(2)

        @pl.when(t == 0)
        def _():
            acc_ref[...] = jnp.zeros_like(acc_ref)
        acc_ref[...] += _dot_tn(x_ref[...], y_ref[...])

        @pl.when(t == nt - 1)
        def _():
            o_ref[...] = acc_ref[...].astype(BF16)

    return pl.pallas_call(
        body, name=name, grid=grid, out_shape=out_shape,
        in_specs=[pl.BlockSpec((bt, bk1), x_map), pl.BlockSpec((bt, bk2), y_map)],
        out_specs=out_spec,
        scratch_shapes=[pltpu.VMEM((bk1, bk2), F32)],
        compiler_params=_params(("parallel", "parallel", "arbitrary")),
    )(x, y)


def _pool_bands(tm):
    k = tm + 128
    t = np.arange(tm)[:, None]
    e = np.arange(k)[None, :]
    fwd, bwd = [], []
    for w in POOL_WINDOWS:
        lo = POOL_HALO + t - w // 2
        fwd.append(((e >= lo) & (e <= lo + w - 1)).astype(np.float32))
        lo_t = POOL_HALO + t - w // 2 + 1
        bwd.append(((e >= lo_t) & (e <= lo_t + w - 1)).astype(np.float32))
    return jnp.asarray(np.stack(fwd), BF16), jnp.asarray(np.stack(bwd), BF16)


def _pool_geometry(i, nct, n_tiles, tm, c_len, l_len):
    if nct == 0:
        pos0 = i * tm
        ls = l_len
        has_prev = i > 0
        has_next = i < n_tiles - 1
    else:
        in_ctx = i < nct
        pos0 = jnp.where(in_ctx, i, i - nct) * tm
        ls = jnp.where(in_ctx, c_len, l_len)
        has_prev = jnp.logical_and(i != 0, i != nct)
        has_next = jnp.logical_and(i != nct - 1, i != n_tiles - 1)
    return pos0, ls, has_prev, has_next


def _window_inv_counts(pos, ls):
    out = []
    for w in POOL_WINDOWS:
        lo = jnp.maximum(pos - w // 2, 0)
        hi = jnp.minimum(pos + w - w // 2, ls)
        cnt = jnp.maximum(hi - lo, 1).astype(F32)
        out.append(1.0 / cnt)
    return out


def _split_bf16(x):
    hi = x.astype(BF16)
    return hi, (x - hi.astype(F32)).astype(BF16)


def _extend(prev, tile, nxt, has_prev, has_next):
    w = tile.shape[1]
    prev = jnp.where(has_prev, prev, 0.0)
    nxt = jnp.where(has_next, nxt, 0.0)
    return jnp.concatenate([prev, tile, nxt, jnp.zeros((128 - 2 * POOL_HALO, w), F32)], axis=0)


def _pool_specs(tm, d, n_rows):
    last8 = n_rows // POOL_HALO - 1
    per = tm // POOL_HALO
    return [pl.BlockSpec((tm, d), lambda i: (i, 0)),
            pl.BlockSpec((POOL_HALO, d), lambda i: (jnp.maximum(i * per - 1, 0), 0)),
            pl.BlockSpec((POOL_HALO, d), lambda i: (jnp.minimum((i + 1) * per, last8), 0))]


def _pool_fwd(h, mods, ng, w, scale, bands, nct, tm, c_len, l_len):
    r, d = h.shape
    gw = d // POOL_GROUPS
    n_tiles = r // tm
    kx = tm + 128

    def body(h_ref, hp_ref, hn_ref, mod_ref, ng_ref, w_ref, sc_ref, band_ref, y_ref, h1_ref):
        i = pl.program_id(0)
        pos0, ls, has_prev, has_next = _pool_geometry(i, nct, n_tiles, tm, c_len, l_len)
        ngv, sh, sc = ng_ref[...], mod_ref[0:1, :], mod_ref[1:2, :]
        h = h_ref[...]
        a = _normmod(h, ngv, sh, sc)[0]
        a_ext = _extend(_normmod(hp_ref[...], ngv, sh, sc)[0], a, _normmod(hn_ref[...], ngv, sh, sc)[0],
                        has_prev, has_next)
        pos = pos0 + lax.broadcasted_iota(jnp.int32, (tm, 1), 0)
        inv = _window_inv_counts(pos, ls)
        ys = []
        for g in range(POOL_GROUPS):
            cols = slice(g * gw, (g + 1) * gw)
            hi, lo = _split_bf16(a_ext[:, cols])
            s = _dot(band_ref[g], hi) + _dot(band_ref[g], lo)
            pg = s * inv[g] - a[:, cols]
            ys.append(_dot(pg.astype(BF16), w_ref[g]))
        y = jnp.concatenate(ys, axis=1) * sc_ref[...]
        y_ref[...] = y.astype(BF16)
        h1_ref[...] = h + mod_ref[2:3, :] * y

    return pl.pallas_call(
        body, name="pool_fwd", grid=(n_tiles,),
        out_shape=[jax.ShapeDtypeStruct((r, d), BF16), jax.ShapeDtypeStruct((r, d), F32)],
        in_specs=_pool_specs(tm, d, r) + [
            pl.BlockSpec((None, 8, d), lambda i: (_sid(i, nct), 0, 0)),
            _const_spec((1, d)), _const_spec(w.shape), _const_spec((1, d)), _const_spec((4, tm, kx))],
        out_specs=[pl.BlockSpec((tm, d), lambda i: (i, 0)), pl.BlockSpec((tm, d), lambda i: (i, 0))],
        compiler_params=_params(("arbitrary",)),
    )(h, h, h, mods, ng, w, scale, bands[0])


def _pool_bwd(dh1, h, y, mods, ng, w, scale, bands, nct, tm, c_len, l_len, latent_out):
    r, d = h.shape
    gw = d // POOL_GROUPS
    n_tiles = r // tm
    kx = tm + 128
    out_rows = l_len if latent_out else r
    out_off = nct if latent_out else 0

    def body(dh_ref, dhp_ref, dhn_ref, h_ref, hp_ref, hn_ref, y_ref, mod_ref, ng_ref, w_ref, sc_ref,
             bf_ref, bb_ref, dho_ref, dw_ref, st_ref):
        i = pl.program_id(0)
        pos0, ls, has_prev, has_next = _pool_geometry(i, nct, n_tiles, tm, c_len, l_len)
        ngv, sh, sc, gate = ng_ref[...], mod_ref[0:1, :], mod_ref[1:2, :], mod_ref[2:3, :]
        scale_v = sc_ref[...]
        h = h_ref[...]
        a, xhat, rstd, n = _normmod(h, ngv, sh, sc)
        a_ext = _extend(_normmod(hp_ref[...], ngv, sh, sc)[0], a, _normmod(hn_ref[...], ngv, sh, sc)[0],
                        has_prev, has_next)
        dh = dh_ref[...]
        dgate = _colsum(dh * y_ref[...].astype(F32))
        dy = gate * dh
        dy_ext = _extend(gate * dhp_ref[...], dy, gate * dhn_ref[...], has_prev, has_next)
        dyp_ext = (dy_ext * scale_v).astype(BF16)
        dyp = (dy * scale_v).astype(BF16)
        pos = pos0 + lax.broadcasted_iota(jnp.int32, (tm, 1), 0)
        inv = _window_inv_counts(pos, ls)
        pos_e = pos0 - POOL_HALO + lax.broadcasted_iota(jnp.int32, (kx, 1), 0)
        inv_e = _window_inv_counts(pos_e, ls)

        @pl.when(i == 0)
        def _():
            dw_ref[...] = jnp.zeros_like(dw_ref)
        das, dscale = [], []
        for g in range(POOL_GROUPS):
            cols = slice(g * gw, (g + 1) * gw)
            hi, lo = _split_bf16(a_ext[:, cols])
            pg = ((_dot(bf_ref[g], hi) + _dot(bf_ref[g], lo)) * inv[g] - a[:, cols]).astype(BF16)
            dscale.append(_colsum(dy[:, cols] * _dot(pg, w_ref[g])))
            dyp_g = dyp_ext[:, cols]
            dw_ref[g] += _dot_tn(pg, dyp[:, cols])
            dp_ext = _dot_nt(dyp_g, w_ref[g])
            hi, lo = _split_bf16(dp_ext * inv_e[g])
            das.append(_dot(bb_ref[g], hi) + _dot(bb_ref[g], lo) - dp_ext[POOL_HALO:POOL_HALO + tm, :])
        da = jnp.concatenate(das, axis=1)
        dhn, dsh, dsc, dng = _normmod_bwd(da, xhat, rstd, n, ngv, sc)
        dho_ref[...] = dh + dhn
        _acc_rows(st_ref, _first_of_stream(i, nct), [dsh, dsc, dgate, dng, jnp.concatenate(dscale, axis=1)])

    return pl.pallas_call(
        body, name="pool_bwd", grid=(n_tiles,),
        out_shape=[jax.ShapeDtypeStruct((out_rows, d), F32),
                   jax.ShapeDtypeStruct((POOL_GROUPS, gw, gw), F32),
                   jax.ShapeDtypeStruct((_n_streams(nct), 8, d), F32)],
        in_specs=_pool_specs(tm, d, r) + _pool_specs(tm, d, r) + [
            pl.BlockSpec((tm, d), lambda i: (i, 0)),
            pl.BlockSpec((None, 8, d), lambda i: (_sid(i, nct), 0, 0)),
            _const_spec((1, d)), _const_spec(w.shape), _const_spec((1, d)),
            _const_spec((4, tm, kx)), _const_spec((4, tm, kx))],
        out_specs=[pl.BlockSpec((tm, d), lambda i: (jnp.maximum(i - out_off, 0), 0)),
                   pl.BlockSpec((POOL_GROUPS, gw, gw), lambda i: (0, 0, 0)),
                   pl.BlockSpec((None, 8, d), lambda i: (_stat_sid(i, nct), 0, 0))],
        compiler_params=_params(("arbitrary",)),
    )(dh1, dh1, dh1, h, h, h, y, mods, ng, w, scale, bands[0], bands[1])


def _rope_tables(c_len, l_len):
    half = HEAD_DIM // 2
    t = np.arange(l_len)
    row = (t // GRID_W).astype(np.float32)
    col = (t % GRID_W).astype(np.float32)
    inv = (np.float32(ROPE_BASE) ** (-np.arange(0, half, 2, dtype=np.float32) / np.float32(half))).astype(np.float32)
    ang_r = row[:, None] * inv[None, :]
    ang_c = col[:, None] * inv[None, :]
    cos = np.concatenate([np.cos(ang_r), np.cos(ang_r), np.cos(ang_c), np.cos(ang_c)], axis=1)
    sin = np.concatenate([-np.sin(ang_r), np.sin(ang_r), -np.sin(ang_c), np.sin(ang_c)], axis=1)
    cos = np.concatenate([np.ones((c_len, HEAD_DIM), np.float32), cos.astype(np.float32)], axis=0)
    sin = np.concatenate([np.zeros((c_len, HEAD_DIM), np.float32), sin.astype(np.float32)], axis=0)
    return jnp.asarray(cos, F32), jnp.asarray(sin, F32)


def _swap_pairs(x):
    lane = lax.broadcasted_iota(jnp.int32, x.shape, 1)
    return jnp.where((lane % 64) < 32, pltpu.roll(x, 96, 1), pltpu.roll(x, 32, 1))


def _head_norm(x, g):
    rstd = lax.rsqrt(jnp.mean(x * x, axis=-1, keepdims=True) + EPS)
    xhat = x * rstd
    return xhat * g, xhat, rstd


def _qkv_fwd(h, mods, ng, w, qg, kg, cos, sin, n_heads, n_kv, nct, tm):
    t_rows, d = h.shape
    qw, kw = n_heads * HEAD_DIM, n_kv * HEAD_DIM
    scale = HEAD_DIM ** -0.5

    def body(h_ref, mod_ref, ng_ref, w_ref, qg_ref, kg_ref, cos_ref, sin_ref, q_ref, k_ref, v_ref):
        a = _normmod(h_ref[...], ng_ref[...], mod_ref[0:1, :], mod_ref[1:2, :])[0]
        qkv = _dot(a.astype(BF16), w_ref[...])
        cosv, sinv = cos_ref[...], sin_ref[...]
        for hd in range(n_heads + n_kv):
            cols = slice(hd * HEAD_DIM, (hd + 1) * HEAD_DIM)
            xn = _head_norm(qkv[:, cols], qg_ref[...] if hd < n_heads else kg_ref[...])[0]
            xr = xn * cosv + _swap_pairs(xn) * sinv
            if hd < n_heads:
                q_ref[:, cols] = (xr * scale).astype(BF16)
            else:
                k_ref[:, (hd - n_heads) * HEAD_DIM:(hd - n_heads + 1) * HEAD_DIM] = xr.astype(BF16)
        v_ref[...] = qkv[:, qw + kw:].astype(BF16)

    return pl.pallas_call(
        body, name="qkv_fwd", grid=(t_rows // tm,),
        out_shape=[jax.ShapeDtypeStruct((t_rows, qw), BF16), jax.ShapeDtypeStruct((t_rows, kw), BF16),
                   jax.ShapeDtypeStruct((t_rows, kw), BF16)],
        in_specs=[pl.BlockSpec((tm, d), lambda i: (i, 0)),
                  pl.BlockSpec((None, 8, d), lambda i: (_sid(i, nct), 0, 0)),
                  _const_spec((1, d)), _const_spec(w.shape), _const_spec((1, HEAD_DIM)),
                  _const_spec((1, HEAD_DIM)),
                  pl.BlockSpec((tm, HEAD_DIM), lambda i: (i, 0)), pl.BlockSpec((tm, HEAD_DIM), lambda i: (i, 0))],
        out_specs=[pl.BlockSpec((tm, qw), lambda i: (i, 0)), pl.BlockSpec((tm, kw), lambda i: (i, 0)),
                   pl.BlockSpec((tm, kw), lambda i: (i, 0))],
        compiler_params=_params(("arbitrary",)),
    )(h, mods, ng, w, qg, kg, cos, sin)


def _stack_heads(x):
    return jnp.concatenate([x[:, :HEAD_DIM], x[:, HEAD_DIM:]], axis=0)


def _unstack_heads(x, tq):
    return jnp.concatenate([x[:tq], x[tq:]], axis=1)


def _flash_tk(t_rows, tm):
    best = tm
    k = tm
    while k <= FLASH_TK_CAP:
        if t_rows % k == 0:
            best = k
        k += tm
    return best


def _flash_fwd(q, k, v, n_kv, nct, tq):
    t_rows = k.shape[0]
    l_rows = t_rows - nct * tq
    tk = _flash_tk(t_rows, tq)
    nk = t_rows // tk
    gq = 2 * HEAD_DIM

    def body(q_ref, k_ref, v_ref, o_ref, lse_ref, m_s, l_s, acc_s):
        ki = pl.program_id(2)

        @pl.when(ki == 0)
        def _():
            m_s[...] = jnp.full_like(m_s, -jnp.inf)
            l_s[...] = jnp.zeros_like(l_s)
            acc_s[...] = jnp.zeros_like(acc_s)
        q2 = _stack_heads(q_ref[...])
        s = _dot_nt(q2, k_ref[...])
        m_prev = m_s[...]
        m_new = jnp.maximum(m_prev, jnp.max(s, axis=-1, keepdims=True))
        alpha = jnp.exp(m_prev - m_new)
        p = jnp.exp(s - m_new)
        l_s[...] = alpha * l_s[...] + jnp.sum(p, axis=-1, keepdims=True)
        acc_s[...] = alpha * acc_s[...] + _dot(p.astype(BF16), v_ref[...])
        m_s[...] = m_new

        @pl.when(ki == nk - 1)
        def _():
            l = l_s[...]
            o_ref[...] = _unstack_heads(acc_s[...] * (1.0 / l), tq).astype(BF16)
            lse_ref[...] = _unstack_heads(m_s[...] + jnp.log(l), tq)

    return pl.pallas_call(
        body, name="flash_fwd", grid=(n_kv, l_rows // tq, nk),
        out_shape=[jax.ShapeDtypeStruct((l_rows, n_kv * gq), BF16),
                   jax.ShapeDtypeStruct((n_kv, l_rows, 2), F32)],
        in_specs=[pl.BlockSpec((tq, gq), lambda g, i, j: (i + nct, g)),
                  pl.BlockSpec((tk, HEAD_DIM), lambda g, i, j: (j, g)),
                  pl.BlockSpec((tk, HEAD_DIM), lambda g, i, j: (j, g))],
        out_specs=[pl.BlockSpec((tq, gq), lambda g, i, j: (i, g)),
                   pl.BlockSpec((None, tq, 2), lambda g, i, j: (g, i, 0))],
        scratch_shapes=[pltpu.VMEM((2 * tq, 1), F32), pltpu.VMEM((2 * tq, 1), F32),
                        pltpu.VMEM((2 * tq, HEAD_DIM), F32)],
        compiler_params=_params(("parallel", "parallel", "arbitrary")),
    )(q, k, v)


def _flash_bwd(q, k, v, do, lse, delta, n_kv, nct, tq):
    t_rows = k.shape[0]
    l_rows = t_rows - nct * tq
    tk = _flash_tk(t_rows, tq)
    nq = l_rows // tq
    gq = 2 * HEAD_DIM

    def body(q_ref, k_ref, v_ref, do_ref, lse_ref, dl_ref, dq_ref, dk_ref, dv_ref):
        ki, qi = pl.program_id(1), pl.program_id(2)
        rows = pl.ds(pl.multiple_of(qi * tq, tq), tq)

        @pl.when(qi == 0)
        def _():
            dk_ref[...] = jnp.zeros_like(dk_ref)
            dv_ref[...] = jnp.zeros_like(dv_ref)

        @pl.when(ki == 0)
        def _():
            dq_ref[rows, :] = jnp.zeros((tq, gq), F32)
        q2 = _stack_heads(q_ref[...])
        do2 = _stack_heads(do_ref[...])
        lse2 = _stack_heads_narrow(lse_ref[...])
        dl2 = _stack_heads_narrow(dl_ref[...])
        kk, vv = k_ref[...], v_ref[...]
        p = jnp.exp(_dot_nt(q2, kk) - lse2)
        dv_ref[...] += _dot_tn(p.astype(BF16), do2)
        ds = (p * (_dot_nt(do2, vv) - dl2)).astype(BF16)
        dk_ref[...] += _dot_tn(ds, q2)
        dq_ref[rows, :] += _unstack_heads(_dot(ds, kk), tq)

    return pl.pallas_call(
        body, name="flash_bwd", grid=(n_kv, t_rows // tk, nq),
        out_shape=[jax.ShapeDtypeStruct((l_rows, n_kv * gq), F32),
                   jax.ShapeDtypeStruct((t_rows, n_kv * HEAD_DIM), F32),
                   jax.ShapeDtypeStruct((t_rows, n_kv * HEAD_DIM), F32)],
        in_specs=[pl.BlockSpec((tq, gq), lambda g, j, i: (i + nct, g)),
                  pl.BlockSpec((tk, HEAD_DIM), lambda g, j, i: (j, g)),
                  pl.BlockSpec((tk, HEAD_DIM), lambda g, j, i: (j, g)),
                  pl.BlockSpec((tq, gq), lambda g, j, i: (i, g)),
                  pl.BlockSpec((None, tq, 2), lambda g, j, i: (g, i, 0)),
                  pl.BlockSpec((None, tq, 2), lambda g, j, i: (g, i, 0))],
        out_specs=[pl.BlockSpec((l_rows, gq), lambda g, j, i: (0, g)),
                   pl.BlockSpec((tk, HEAD_DIM), lambda g, j, i: (j, g)),
                   pl.BlockSpec((tk, HEAD_DIM), lambda g, j, i: (j, g))],
        compiler_params=_params(("parallel", "arbitrary", "arbitrary")),
    )(q, k, v, do, lse, delta)


def _stack_heads_narrow(x):
    return jnp.concatenate([x[:, 0:1], x[:, 1:2]], axis=0)


def _wo_fwd(h, o, wo, mods, nct, tm):
    l_rows, z = o.shape
    d = h.shape[1]

    def body(h_ref, o_ref, w_ref, mod_ref, y_ref, h1_ref):
        y = _dot(o_ref[...], w_ref[...])
        y_ref[...] = y.astype(BF16)
        h1_ref[...] = h_ref[...] + mod_ref[2:3, :] * y

    return pl.pallas_call(
        body, name="wo_fwd", grid=(l_rows // tm,),
        out_shape=[jax.ShapeDtypeStruct((l_rows, d), BF16), jax.ShapeDtypeStruct((l_rows, d), F32)],
        in_specs=[pl.BlockSpec((tm, d), lambda i: (i + nct, 0)), pl.BlockSpec((tm, z), lambda i: (i, 0)),
                  _const_spec(wo.shape), pl.BlockSpec((None, 8, d), lambda i: (1, 0, 0))],
        out_specs=[pl.BlockSpec((tm, d), lambda i: (i, 0)), pl.BlockSpec((tm, d), lambda i: (i, 0))],
        compiler_params=_params(("arbitrary",)),
    )(h, o, wo, mods)


def _wo_bwd(dh1, y, o, wo, mods, n_kv, tm):
    l_rows, z = o.shape
    d = dh1.shape[1]

    def body(dh_ref, y_ref, o_ref, w_ref, mod_ref, dy_ref, do_ref, dl_ref, st_ref):
        i = pl.program_id(0)
        dh = dh_ref[...]
        dgate = _colsum(dh * y_ref[...].astype(F32))
        dy = (mod_ref[2:3, :] * dh).astype(BF16)
        dy_ref[...] = dy
        do = _dot_nt(dy, w_ref[...])
        do_ref[...] = do.astype(BF16)
        prod = do * o_ref[...].astype(F32)
        for g in range(n_kv):
            d0 = jnp.sum(prod[:, (2 * g) * HEAD_DIM:(2 * g + 1) * HEAD_DIM], axis=-1, keepdims=True)
            d1 = jnp.sum(prod[:, (2 * g + 1) * HEAD_DIM:(2 * g + 2) * HEAD_DIM], axis=-1, keepdims=True)
            dl_ref[g] = jnp.concatenate([d0, d1], axis=1)
        zero = jnp.zeros((1, d), F32)
        _acc_rows(st_ref, i == 0, [zero, zero, dgate])

    return pl.pallas_call(
        body, name="wo_bwd", grid=(l_rows // tm,),
        out_shape=[jax.ShapeDtypeStruct((l_rows, d), BF16), jax.ShapeDtypeStruct((l_rows, z), BF16),
                   jax.ShapeDtypeStruct((n_kv, l_rows, 2), F32), jax.ShapeDtypeStruct((8, d), F32)],
        in_specs=[pl.BlockSpec((tm, d), lambda i: (i, 0)), pl.BlockSpec((tm, d), lambda i: (i, 0)),
                  pl.BlockSpec((tm, z), lambda i: (i, 0)), _const_spec(wo.shape),
                  pl.BlockSpec((None, 8, d), lambda i: (1, 0, 0))],
        out_specs=[pl.BlockSpec((tm, d), lambda i: (i, 0)), pl.BlockSpec((tm, z), lambda i: (i, 0)),
                   pl.BlockSpec((n_kv, tm, 2), lambda i: (0, i, 0)), pl.BlockSpec((8, d), lambda i: (0, 0))],
        compiler_params=_params(("arbitrary",)),
    )(dh1, y, o, wo, mods)


def _qkv_bwd(h, dh_lat, dq, dk, dv, mods, ng, w, qg, kg, cos, sin, n_heads, n_kv, nct, tm):
    t_rows, d = h.shape
    qw, kw = n_heads * HEAD_DIM, n_kv * HEAD_DIM
    scale = HEAD_DIM ** -0.5

    def body(h_ref, dhl_ref, dq_ref, dk_ref, dv_ref, mod_ref, ng_ref, w_ref, qg_ref, kg_ref, cos_ref,
             sin_ref, dh_ref, a_ref, dqkv_ref, st_ref, dg_ref):
        i = pl.program_id(0)
        lat = (i >= nct).astype(F32)
        ngv, sc = ng_ref[...], mod_ref[1:2, :]
        a, xhat, rstd, n = _normmod(h_ref[...], ngv, mod_ref[0:1, :], sc)
        ab = a.astype(BF16)
        a_ref[...] = ab
        qkv = _dot(ab, w_ref[...])
        cosv, sinv = cos_ref[...], sin_ref[...]
        dqg = jnp.zeros((1, HEAD_DIM), F32)
        dkg = jnp.zeros((1, HEAD_DIM), F32)
        for hd in range(n_heads + n_kv):
            cols = slice(hd * HEAD_DIM, (hd + 1) * HEAD_DIM)
            is_q = hd < n_heads
            g = qg_ref[...] if is_q else kg_ref[...]
            _, hx, hr = _head_norm(qkv[:, cols], g)
            if is_q:
                dxr = dq_ref[:, cols] * (scale * lat)
            else:
                dxr = dk_ref[:, (hd - n_heads) * HEAD_DIM:(hd - n_heads + 1) * HEAD_DIM]
            dxn = dxr * cosv + _swap_pairs(dxr * sinv)
            if is_q:
                dqg = dqg + _colsum(dxn * hx)
            else:
                dkg = dkg + _colsum(dxn * hx)
            dxh = dxn * g
            dx = hr * (dxh - hx * jnp.mean(dxh * hx, axis=-1, keepdims=True))
            dqkv_ref[:, cols] = dx.astype(BF16)
        dqkv_ref[:, qw + kw:] = dv_ref[...].astype(BF16)
        da = _dot_nt(dqkv_ref[...], w_ref[...])
        dhn, dsh, dsc, dng = _normmod_bwd(da, xhat, rstd, n, ngv, sc)
        dh_ref[...] = dhl_ref[...] * lat + dhn
        _acc_rows(st_ref, _first_of_stream(i, nct), [dsh, dsc, jnp.zeros((1, d), F32), dng])
        _acc_rows(dg_ref, i == 0, [dqg, dkg])

    lat_map = lambda i: (jnp.maximum(i - nct, 0), 0)
    return pl.pallas_call(
        body, name="qkv_bwd", grid=(t_rows // tm,),
        out_shape=[jax.ShapeDtypeStruct((t_rows, d), F32), jax.ShapeDtypeStruct((t_rows, d), BF16),
                   jax.ShapeDtypeStruct((t_rows, qw + 2 * kw), BF16), jax.ShapeDtypeStruct((2, 8, d), F32),
                   jax.ShapeDtypeStruct((8, HEAD_DIM), F32)],
        in_specs=[pl.BlockSpec((tm, d), lambda i: (i, 0)), pl.BlockSpec((tm, d), lat_map),
                  pl.BlockSpec((tm, qw), lat_map), pl.BlockSpec((tm, kw), lambda i: (i, 0)),
                  pl.BlockSpec((tm, kw), lambda i: (i, 0)),
                  pl.BlockSpec((None, 8, d), lambda i: (_sid(i, nct), 0, 0)),
                  _const_spec((1, d)), _const_spec(w.shape), _const_spec((1, HEAD_DIM)),
                  _const_spec((1, HEAD_DIM)),
                  pl.BlockSpec((tm, HEAD_DIM), lambda i: (i, 0)), pl.BlockSpec((tm, HEAD_DIM), lambda i: (i, 0))],
        out_specs=[pl.BlockSpec((tm, d), lambda i: (i, 0)), pl.BlockSpec((tm, d), lambda i: (i, 0)),
                   pl.BlockSpec((tm, qw + 2 * kw), lambda i: (i, 0)),
                   pl.BlockSpec((None, 8, d), lambda i: (_sid(i, nct), 0, 0)),
                   pl.BlockSpec((8, HEAD_DIM), lambda i: (0, 0))],
        compiler_params=_params(("arbitrary",)),
    )(h, dh_lat, dq, dk, dv, mods, ng, w, qg, kg, cos, sin)


def _gmlp_core(a_bf, win_ref, lng, lnb, ws_ref, bst_ref, tm, half):
    z = _dot(a_bf, win_ref[...])
    zu, zv = z[:, :half], z[:, half:]
    u, v = _gelu(zu), _gelu(zv)
    mu = jnp.mean(v, axis=-1, keepdims=True)
    vc = v - mu
    rstd_v = lax.rsqrt(jnp.mean(vc * vc, axis=-1, keepdims=True) + EPS)
    vhat = vc * rstd_v
    vln = (vhat * lng + lnb).astype(BF16)
    gw = half // GMLP_GROUPS
    rows = []
    for ch in range(tm // CHUNK):
        rs = slice(ch * CHUNK, (ch + 1) * CHUNK)
        cols = []
        for g in range(GMLP_GROUPS):
            cs = slice(g * gw, (g + 1) * gw)
            cols.append(_dot(ws_ref[g], vln[rs, cs]) + bst_ref[:, g:g + 1])
        rows.append(jnp.concatenate(cols, axis=1))
    sv = rows[0] if len(rows) == 1 else jnp.concatenate(rows, axis=0)
    return zu, zv, u, vhat, rstd_v, vln, sv


def _gmlp_fwd(h, mods, ng, win, lng, lnb, ws, bst, wout, tm):
    l_rows, d = h.shape
    half = wout.shape[0]

    def body(h_ref, mod_ref, ng_ref, win_ref, lng_ref, lnb_ref, ws_ref, bst_ref, wout_ref, y_ref, h1_ref):
        hv = h_ref[...]
        a = _normmod(hv, ng_ref[...], mod_ref[0:1, :], mod_ref[1:2, :])[0]
        _, _, u, _, _, _, sv = _gmlp_core(a.astype(BF16), win_ref, lng_ref[...], lnb_ref[...], ws_ref,
                                          bst_ref, tm, half)
        y = _dot((u * sv).astype(BF16), wout_ref[...])
        y_ref[...] = y.astype(BF16)
        h1_ref[...] = hv + mod_ref[2:3, :] * y

    return pl.pallas_call(
        body, name="gmlp_fwd", grid=(l_rows // tm,),
        out_shape=[jax.ShapeDtypeStruct((l_rows, d), BF16), jax.ShapeDtypeStruct((l_rows, d), F32)],
        in_specs=[pl.BlockSpec((tm, d), lambda i: (i, 0)), pl.BlockSpec((None, 8, d), lambda i: (1, 0, 0)),
                  _const_spec((1, d)), _const_spec(win.shape), _const_spec((1, half)), _const_spec((1, half)),
                  _const_spec(ws.shape), _const_spec(bst.shape), _const_spec(wout.shape)],
        out_specs=[pl.BlockSpec((tm, d), lambda i: (i, 0)), pl.BlockSpec((tm, d), lambda i: (i, 0))],
        compiler_params=_params(("arbitrary",)),
    )(h, mods, ng, win, lng, lnb, ws, bst, wout)


def _gmlp_bwd(dh1, h, y, mods, ng, win, lng, lnb, ws, wst, bst, wout, tm):
    l_rows, d = h.shape
    half = wout.shape[0]
    gw = half // GMLP_GROUPS

    def body(dh_ref, h_ref, y_ref, mod_ref, ng_ref, win_ref, lng_ref, lnb_ref, ws_ref, wst_ref, bst_ref,
             wout_ref, dho_ref, a_ref, dz_ref, gt_ref, dy_ref, st_ref, ln_ref, dws_ref, dbs_ref):
        i = pl.program_id(0)
        ngv, sc = ng_ref[...], mod_ref[1:2, :]
        lngv = lng_ref[...]
        a, xhat, rstd, n = _normmod(h_ref[...], ngv, mod_ref[0:1, :], sc)
        ab = a.astype(BF16)
        a_ref[...] = ab
        zu, zv, u, vhat, rstd_v, vln, sv = _gmlp_core(ab, win_ref, lngv, lnb_ref[...], ws_ref, bst_ref,
                                                      tm, half)
        gt_ref[...] = (u * sv).astype(BF16)
        dh = dh_ref[...]
        dgate = _colsum(dh * y_ref[...].astype(F32))
        dy = (mod_ref[2:3, :] * dh).astype(BF16)
        dy_ref[...] = dy
        dgated = _dot_nt(dy, wout_ref[...])
        du = dgated * sv
        dsv = (dgated * u).astype(BF16)

        @pl.when(i == 0)
        def _():
            dws_ref[...] = jnp.zeros_like(dws_ref)
            dbs_ref[...] = jnp.zeros_like(dbs_ref)
        lane = lax.broadcasted_iota(jnp.int32, (CHUNK, 128), 1)
        dbs = jnp.zeros((CHUNK, 128), F32)
        rows = []
        for ch in range(tm // CHUNK):
            rs = slice(ch * CHUNK, (ch + 1) * CHUNK)
            cols = []
            for g in range(GMLP_GROUPS):
                cs = slice(g * gw, (g + 1) * gw)
                dsv_cg = dsv[rs, cs]
                dws_ref[g] += _dot_nt(dsv_cg, vln[rs, cs])
                cols.append(_dot(wst_ref[g], dsv_cg))
                dbs = dbs + jnp.where(lane == g, jnp.sum(dsv_cg.astype(F32), axis=-1, keepdims=True), 0.0)
            rows.append(jnp.concatenate(cols, axis=1))
        dbs_ref[...] += dbs
        dvln = rows[0] if len(rows) == 1 else jnp.concatenate(rows, axis=0)
        dlng = _colsum(dvln * vhat)
        dlnb = _colsum(dvln)
        dvh = dvln * lngv
        dv = rstd_v * (dvh - jnp.mean(dvh, axis=-1, keepdims=True)
                       - vhat * jnp.mean(dvh * vhat, axis=-1, keepdims=True))
        dz_ref[:, :half] = (du * _gelu_grad(zu)).astype(BF16)
        dz_ref[:, half:] = (dv * _gelu_grad(zv)).astype(BF16)
        da = _dot_nt(dz_ref[...], win_ref[...])
        dhn, dsh, dsc, dng = _normmod_bwd(da, xhat, rstd, n, ngv, sc)
        dho_ref[...] = dh + dhn
        _acc_rows(st_ref, i == 0, [dsh, dsc, dgate, dng])
        _acc_rows(ln_ref, i == 0, [dlng, dlnb])

    row = lambda w: pl.BlockSpec((tm, w), lambda i: (i, 0))
    return pl.pallas_call(
        body, name="gmlp_bwd", grid=(l_rows // tm,),
        out_shape=[jax.ShapeDtypeStruct((l_rows, d), F32), jax.ShapeDtypeStruct((l_rows, d), BF16),
                   jax.ShapeDtypeStruct((l_rows, 2 * half), BF16), jax.ShapeDtypeStruct((l_rows, half), BF16),
                   jax.ShapeDtypeStruct((l_rows, d), BF16), jax.ShapeDtypeStruct((8, d), F32),
                   jax.ShapeDtypeStruct((8, half), F32), jax.ShapeDtypeStruct(ws.shape, F32),
                   jax.ShapeDtypeStruct((CHUNK, 128), F32)],
        in_specs=[row(d), row(d), row(d), pl.BlockSpec((None, 8, d), lambda i: (1, 0, 0)),
                  _const_spec((1, d)), _const_spec(win.shape), _const_spec((1, half)), _const_spec((1, half)),
                  _const_spec(ws.shape), _const_spec(ws.shape), _const_spec(bst.shape), _const_spec(wout.shape)],
        out_specs=[row(d), row(d), row(2 * half), row(half), row(d),
                   pl.BlockSpec((8, d), lambda i: (0, 0)), pl.BlockSpec((8, half), lambda i: (0, 0)),
                   pl.BlockSpec(ws.shape, lambda i: (0, 0, 0)), pl.BlockSpec((CHUNK, 128), lambda i: (0, 0))],
        compiler_params=_params(("arbitrary",)),
    )(dh1, h, y, mods, ng, win, lng, lnb, ws, wst, bst, wout)


def _head(h, final_g, target, tm):
    l_rows, d = h.shape
    n_tiles = l_rows // tm

    def body(h_ref, g_ref, t_ref, dh_ref, loss_ref, dg_ref, acc_ref):
        i = pl.program_id(0)
        g = g_ref[...]
        hv = h_ref[...]
        rstd = lax.rsqrt(jnp.mean(hv * hv, axis=-1, keepdims=True) + EPS)
        xhat = hv * rstd
        e = xhat * g - t_ref[...]
        dout = e * (1.0 / d)
        dxhat = dout * g
        dh_ref[...] = rstd * (dxhat - xhat * jnp.mean(dxhat * xhat, axis=-1, keepdims=True))
        _acc_rows(dg_ref, i == 0, [_colsum(dout * xhat)])
        _acc_rows(acc_ref, i == 0, [_colsum(e * e)])

        @pl.when(i == n_tiles - 1)
        def _():
            total = jnp.sum(acc_ref[0:1, :], axis=-1, keepdims=True) * (0.5 / d)
            loss_ref[...] = jnp.broadcast_to(total, loss_ref.shape)

    return pl.pallas_call(
        body, name="loss_head", grid=(n_tiles,),
        out_shape=[jax.ShapeDtypeStruct((l_rows, d), F32), jax.ShapeDtypeStruct((8, 128), F32),
                   jax.ShapeDtypeStruct((8, d), F32)],
        in_specs=[pl.BlockSpec((tm, d), lambda i: (i, 0)), _const_spec((1, d)),
                  pl.BlockSpec((tm, d), lambda i: (i, 0))],
        out_specs=[pl.BlockSpec((tm, d), lambda i: (i, 0)), pl.BlockSpec((8, 128), lambda i: (0, 0)),
                   pl.BlockSpec((8, d), lambda i: (0, 0))],
        scratch_shapes=[pltpu.VMEM((8, d), F32)],
        compiler_params=_params(("arbitrary",)),
    )(h, final_g, target)


def _adamw(w, gparts, m, v, name):
    shape = w.shape
    cols = shape[-1]
    rows = int(np.prod(shape[:-1])) if len(shape) > 1 else 1
    nparts = gparts.shape[0]
    w2, m2, v2 = (t.reshape(rows, cols) for t in (w, m, v))
    g2 = gparts.reshape(nparts, rows, cols)
    tr = rows
    part_bytes = nparts * cols * gparts.dtype.itemsize
    for cand in (1024, 512, 256, 128, 64, 32, 16, 8):
        if rows * max(part_bytes, cols * 4) <= (2 << 20):
            break
        if rows % cand == 0 and cand < rows:
            tr = cand
            if cand * max(part_bytes, cols * 4) <= (2 << 20):
                break
    c1 = 1.0 - ADAM_B1 ** ADAM_STEP
    c2 = 1.0 - ADAM_B2 ** ADAM_STEP

    def body(w_ref, g_ref, m_ref, v_ref, go_ref, d_ref, mo_ref, vo_ref):
        g = g_ref[0].astype(F32)
        for k in range(1, nparts):
            g = g + g_ref[k].astype(F32)
        mn = ADAM_B1 * m_ref[...] + (1.0 - ADAM_B1) * g
        vn = ADAM_B2 * v_ref[...] + (1.0 - ADAM_B2) * (g * g)
        go_ref[...] = g
        mo_ref[...] = mn
        vo_ref[...] = vn
        d_ref[...] = -ADAM_LR * ((mn / c1) / (jnp.sqrt(vn / c2) + ADAM_EPS) + ADAM_WD * w_ref[...])

    spec = pl.BlockSpec((tr, cols), lambda i: (i, 0))
    outs = pl.pallas_call(
        body, name=name, grid=(rows // tr,),
        out_shape=[jax.ShapeDtypeStruct((rows, cols), F32)] * 4,
        in_specs=[spec, pl.BlockSpec((nparts, tr, cols), lambda i: (0, i, 0)), spec, spec],
        out_specs=[spec] * 4,
        compiler_params=_params(("parallel",)),
    )(w2, g2, m2, v2)
    return tuple(o.reshape(shape) for o in outs)


def _natural_cols(g):
    return jnp.moveaxis(g, 0, -2).reshape(g.shape[1:-1] + (N_DEV * g.shape[-1],))


def _natural_rows(g):
    return jnp.moveaxis(g, 0, -3).reshape(g.shape[1:-2] + (N_DEV * g.shape[-2], g.shape[-1]))


def _shard_cols(full):
    n = full.shape[-1] // N_DEV
    return jnp.moveaxis(full.reshape(full.shape[:-1] + (N_DEV, n)), -2, 0)


def _shard_rows(full):
    r = full.shape[-2] // N_DEV
    return jnp.moveaxis(full.reshape(full.shape[:-2] + (N_DEV, r, full.shape[-1])), -3, 0)


def _my_cols(gathered, me, n):
    return lax.dynamic_slice_in_dim(gathered, me * n, n, axis=gathered.ndim - 1)


def kernel(x, c, ctx, c_ctx, ada_w, ada_b, norm_g, mlp_w1, mlp_w2, pool_w, pool_scale, attn_w_qkv, attn_w_o, attn_q_g, attn_k_g, gm_w_in, gm_ln_g, gm_ln_b, gm_ws, gm_bs, gm_w_out, final_g, loss_target, m_c_ctx, m_ada_w, m_ada_b, m_norm_g, m_mlp_w1, m_mlp_w2, m_pool_w, m_pool_scale, m_attn_w_qkv, m_attn_w_o, m_attn_q_g, m_attn_k_g, m_gm_w_in, m_gm_ln_g, m_gm_ln_b, m_gm_ws, m_gm_bs, m_gm_w_out, m_final_g, v_c_ctx, v_ada_w, v_ada_b, v_norm_g, v_mlp_w1, v_mlp_w2, v_pool_w, v_pool_scale, v_attn_w_qkv, v_attn_w_o, v_attn_q_g, v_attn_k_g, v_gm_w_in, v_gm_ln_g, v_gm_ln_b, v_gm_ws, v_gm_bs, v_gm_w_out, v_final_g):
    l_len, d = x.shape[1], x.shape[2]
    c_len = ctx.shape[1]
    n_layers = ada_w.shape[0]
    assert n_layers == 4 and x.shape[0] == 1
    n_heads = d // HEAD_DIM
    n_kv = n_heads // 2
    half = gm_w_out.shape[1] * N_DEV
    tm = c_len if c_len <= 256 else 256
    assert c_len % tm == 0 and l_len % tm == 0 and tm % CHUNK == 0 and l_len % GRID_W == 0
    nct = c_len // tm
    me = _dev_index(*_coords())
    n_ada = ada_w.shape[-1]

    big = [t.astype(BF16) for t in (mlp_w1, mlp_w2, pool_w, attn_w_qkv[0], attn_w_o[0], gm_w_in[0], gm_w_out[0])]
    w1_g, w2_g, pool_g, qkv_g, wo_g, gin_g, gout_g = _all_gather(big, "gather_weights")
    small = [c, norm_g.reshape(n_layers * 2, -1), pool_scale, gm_ln_g, gm_ln_b]
    c_all, ng_g, ps_g, lng_g, lnb_g = _all_gather(small, "gather_small")
    c_all = c_all.reshape(N_DEV, d)
    w1 = _natural_cols(w1_g)
    w2 = _natural_rows(w2_g)
    pool_wf = _natural_rows(pool_g)
    wqkv = _natural_cols(qkv_g)
    wo = _natural_rows(wo_g)
    win = _natural_cols(gin_g)
    wout = _natural_rows(gout_g)
    ng_full = _natural_cols(ng_g.reshape(N_DEV, n_layers * 2, 1, -1)).reshape(n_layers, 2, 1, d)
    ps_full = _natural_cols(ps_g.reshape(N_DEV, 2, 1, -1))
    lng_full = _natural_cols(lng_g.reshape(N_DEV, 1, -1))
    lnb_full = _natural_cols(lnb_g.reshape(N_DEV, 1, -1))

    c_ctx2 = c_ctx.reshape(1, d)
    ada_b_loc = lax.dynamic_slice_in_dim(ada_b, me * n_ada, n_ada, axis=1).reshape(n_layers, 1, n_ada)
    (mod_g,) = _all_gather([_mods_local(c_all, c_ctx2, ada_w, ada_b_loc)], "gather_mods")
    mod_full = jnp.moveaxis(mod_g, 0, 2).reshape(n_layers, 16, 6, d)
    mod_lat = lax.dynamic_index_in_dim(mod_full, me, axis=1, keepdims=False)
    mod_ctx = mod_full[:, 8]
    mods = jnp.stack([mod_ctx, mod_lat], axis=1)
    mods = jnp.concatenate([mods, jnp.zeros((n_layers, 2, 2, d), F32)], axis=2)

    bands = _pool_bands(tm)
    cos, sin = _rope_tables(c_len, l_len)
    ws_bf = gm_ws[0].astype(BF16)
    wst_bf = jnp.swapaxes(gm_ws[0], 1, 2).astype(BF16)
    bst = jnp.zeros((CHUNK, 128), F32).at[:, :GMLP_GROUPS].set(gm_bs[0].T)
    ng = lambda i, j: ng_full[i, j]

    h0 = jnp.concatenate([ctx[0], x[0]], axis=0)
    y0, h1 = _pool_fwd(h0, mods[0], ng(0, 0), pool_wf[0].astype(BF16), ps_full[0], bands, nct, tm, c_len, l_len)
    h2, p0, ym0 = _mlp_fwd(h1, mods[0], ng(0, 1), w1[0], w2[0], nct, tm)
    q, k, v = _qkv_fwd(h2, mods[1], ng(1, 0), wqkv, attn_q_g, attn_k_g, cos, sin, n_heads, n_kv, nct, tm)
    o, lse = _flash_fwd(q, k, v, n_kv, nct, tm)
    y1, h3 = _wo_fwd(h2, o, wo, mods[1], nct, tm)
    h4, p1, ym1 = _mlp_fwd(h3, mods[1], ng(1, 1), w1[1], w2[1], 0, tm)
    y2, h5 = _gmlp_fwd(h4, mods[2], ng(2, 0), win, lng_full, lnb_full, ws_bf, bst, wout, CHUNK)
    h6, p2, ym2 = _mlp_fwd(h5, mods[2], ng(2, 1), w1[2], w2[2], 0, tm)
    y3, h7 = _pool_fwd(h6, mods[3], ng(3, 0), pool_wf[1].astype(BF16), ps_full[1], bands, 0, tm, c_len, l_len)
    h8, p3, ym3 = _mlp_fwd(h7, mods[3], ng(3, 1), w1[3], w2[3], 0, tm)
    dh, loss_part, dfinal = _head(h8, final_g.reshape(1, d), loss_target[0], tm)

    dw1, dw2, st_mlp = [None] * 4, [None] * 4, [None] * 4

    def mlp_back(i, dh, h_in, p, ym, nct_i):
        dh_in, m_bf, du, r_bf, dacc, st = _mlp_bwd(dh, h_in, p, ym, mods[i], ng(i, 1), w1[i], w2[i], nct_i, tm)
        dw1[i] = _tn_matmul(m_bf, du, "tn_w1")
        dw2[i] = _tn_matmul(r_bf, dacc, "tn_w2")
        st_mlp[i] = st
        return dh_in

    dh = mlp_back(3, dh, h7, p3, ym3, 0)
    dh, dpw1, st_pool3 = _pool_bwd(dh, h6, y3, mods[3], ng(3, 0), pool_wf[1].astype(BF16), ps_full[1], bands,
                                   0, tm, c_len, l_len, False)
    dh = mlp_back(2, dh, h5, p2, ym2, 0)
    dh, a_bf, dz, gated, dy, st_g, st_ln, dws, dbst = _gmlp_bwd(
        dh, h4, y2, mods[2], ng(2, 0), win, lng_full, lnb_full, ws_bf, wst_bf, bst, wout, CHUNK)
    dwin = _tn_matmul(a_bf, dz, "tn_gm_in")
    dwout = _tn_matmul(gated, dy, "tn_gm_out")
    dh = mlp_back(1, dh, h3, p1, ym1, 0)
    dy1, do, delta, st_wo = _wo_bwd(dh, y1, o, wo, mods[1], n_kv, tm)
    dwo = _tn_matmul(o, dy1, "tn_wo")
    dq, dk, dv = _flash_bwd(q, k, v, do, lse, delta, n_kv, nct, tm)
    dh, a_bf, dqkv, st_q, dgains = _qkv_bwd(h2, dh, dq, dk, dv, mods[1], ng(1, 0), wqkv, attn_q_g, attn_k_g,
                                            cos, sin, n_heads, n_kv, nct, tm)
    dwqkv = _tn_matmul(a_bf, dqkv, "tn_qkv")
    dh = mlp_back(0, dh, h1, p0, ym0, nct)
    grad_x, dpw0, st_pool0 = _pool_bwd(dh, h0, y0, mods[0], ng(0, 0), pool_wf[0].astype(BF16), ps_full[0], bands,
                                       nct, tm, c_len, l_len, True)

    grads_big = [_shard_cols(jnp.stack(dw1)), _shard_rows(jnp.stack(dw2)),
                 _shard_rows(jnp.stack([dpw0, dpw1]).astype(BF16)), _shard_cols(dwqkv), _shard_rows(dwo),
                 _shard_cols(dwin), _shard_rows(dwout)]
    g_w1, g_w2, g_pool, g_qkv, g_wo, g_gin, g_gout = _all_to_all(grads_big, "exchange_grads")

    mix_lat = [st_pool0[-1], st_q[1] + st_wo, st_g, st_pool3[-1]]
    mlp_lat = [st[-1] for st in st_mlp]
    dmod_lat = jnp.stack([jnp.concatenate([mix_lat[i][0:3], mlp_lat[i][0:3]]) for i in range(n_layers)])
    dmod_ctx = jnp.stack([jnp.concatenate([st_pool0[0][0:3], st_mlp[0][0][0:3]]),
                          jnp.concatenate([st_q[0][0:2], jnp.zeros((4, d), F32)]),
                          jnp.zeros((6, d), F32), jnp.zeros((6, d), F32)])
    dng_part = jnp.stack([jnp.stack([mix_lat[0][3] + st_pool0[0][3], mlp_lat[0][3] + st_mlp[0][0][3]]),
                          jnp.stack([mix_lat[1][3] + st_q[0][3], mlp_lat[1][3]]),
                          jnp.stack([mix_lat[2][3], mlp_lat[2][3]]),
                          jnp.stack([mix_lat[3][3], mlp_lat[3][3]])])
    dps_part = jnp.stack([mix_lat[0][4] + st_pool0[0][4], mix_lat[3][4]])
    small_parts = [dmod_lat.reshape(n_layers * 6, d), dmod_ctx.reshape(n_layers * 6, d),
                   dng_part.reshape(n_layers * 2, d), dps_part, st_ln, dgains, dws.reshape(-1, CHUNK),
                   dbst, dfinal]
    (gm_lat, gm_ctx, g_ng, g_ps, g_ln, g_gains, g_ws, g_bst, g_final) = _all_gather(small_parts, "gather_small_grads")

    gm_lat4 = gm_lat.reshape(N_DEV, n_layers, 6 * d)
    gm_ctx4 = gm_ctx.reshape(N_DEV, n_layers, 6 * d)
    dm_lat_loc = jnp.moveaxis(_my_cols(gm_lat4, me, n_ada), 0, 1)
    dm_ctx_loc = jnp.moveaxis(_my_cols(gm_ctx4, me, n_ada), 0, 1)
    g_ada_w, ds_part = _ada_grads(c_all, c_ctx2, ada_w, dm_lat_loc, dm_ctx_loc)
    (ds_all,) = _all_gather([ds_part], "gather_dsctx")
    g_c_ctx = _cctx_grad(ds_all, c_ctx2).reshape(d)

    n_ng = norm_g.shape[-1]
    n_ps = pool_scale.shape[-1]
    n_ln = gm_ln_g.shape[-1]
    gparts = {
        "c_ctx": g_c_ctx[None],
        "ada_w": g_ada_w[None],
        "ada_b": jnp.concatenate([gm_lat4, gm_ctx4], axis=0),
        "norm_g": _my_cols(g_ng.reshape(N_DEV, n_layers, 2, d), me, n_ng),
        "mlp_w1": g_w1, "mlp_w2": g_w2, "pool_w": g_pool,
        "pool_scale": _my_cols(g_ps, me, n_ps),
        "attn_w_qkv": g_qkv[:, None], "attn_w_o": g_wo[:, None],
        "attn_q_g": g_gains[:, 0:1], "attn_k_g": g_gains[:, 1:2],
        "gm_w_in": g_gin[:, None],
        "gm_ln_g": _my_cols(g_ln[:, 0:1], me, n_ln), "gm_ln_b": _my_cols(g_ln[:, 1:2], me, n_ln),
        "gm_ws": g_ws.reshape((N_DEV,) + gm_ws.shape),
        "gm_bs": jnp.swapaxes(g_bst[:, :, :GMLP_GROUPS], 1, 2)[:, None],
        "gm_w_out": g_gout[:, None],
        "final_g": g_final[:, 0],
    }
    weights = dict(c_ctx=(c_ctx, m_c_ctx, v_c_ctx), ada_w=(ada_w, m_ada_w, v_ada_w), ada_b=(ada_b, m_ada_b, v_ada_b),
                   norm_g=(norm_g, m_norm_g, v_norm_g), mlp_w1=(mlp_w1, m_mlp_w1, v_mlp_w1),
                   mlp_w2=(mlp_w2, m_mlp_w2, v_mlp_w2), pool_w=(pool_w, m_pool_w, v_pool_w),
                   pool_scale=(pool_scale, m_pool_scale, v_pool_scale),
                   attn_w_qkv=(attn_w_qkv, m_attn_w_qkv, v_attn_w_qkv), attn_w_o=(attn_w_o, m_attn_w_o, v_attn_w_o),
                   attn_q_g=(attn_q_g, m_attn_q_g, v_attn_q_g), attn_k_g=(attn_k_g, m_attn_k_g, v_attn_k_g),
                   gm_w_in=(gm_w_in, m_gm_w_in, v_gm_w_in), gm_ln_g=(gm_ln_g, m_gm_ln_g, v_gm_ln_g),
                   gm_ln_b=(gm_ln_b, m_gm_ln_b, v_gm_ln_b), gm_ws=(gm_ws, m_gm_ws, v_gm_ws),
                   gm_bs=(gm_bs, m_gm_bs, v_gm_bs), gm_w_out=(gm_w_out, m_gm_w_out, v_gm_w_out),
                   final_g=(final_g, m_final_g, v_final_g))
    grads, deltas, new_m, new_v = [], [], [], []
    for wname, (w_, m_, v_) in weights.items():
        g_, d_, nm_, nv_ = _adamw(w_, gparts[wname], m_, v_, "adamw_" + wname)
        grads.append(g_)
        deltas.append(d_)
        new_m.append(nm_)
        new_v.append(nv_)

    loss = lax.psum(loss_part[0, 0], ("x", "y", "c"))
    return (loss, grad_x[None], *grads, *deltas, *new_m, *new_v)
```

```python
import functools
import math

import numpy as np
import jax
import jax.numpy as jnp
from jax import lax
from jax.experimental import pallas as pl
from jax.experimental.pallas import tpu as pltpu

F32 = jnp.float32
BF16 = jnp.bfloat16
MESH_ID = pl.DeviceIdType.MESH

N_DEV = 8
EPS = 1e-6
HEAD_DIM = 128
GRID_W = 64
ROPE_BASE = 10000.0
CHUNK = 128
POOL_WINDOWS = (2, 4, 8, 16)
POOL_GROUPS = 4
POOL_HALO = 8
GMLP_GROUPS = 8
ADAM_LR, ADAM_B1, ADAM_B2, ADAM_EPS, ADAM_WD, ADAM_STEP = 0.001, 0.9, 0.999, 1e-08, 0.01, 10

V7X_VMEM_BYTES = 64 << 20
VMEM_LIMIT_BIG = V7X_VMEM_BYTES - (8 << 20)
FLASH_TK_CAP = 768
LOG2E = math.log2(math.e)
Q_SCALE = HEAD_DIM ** -0.5 * LOG2E
MLP_FC = 512


def _params(sem, vmem=VMEM_LIMIT_BIG):
    return pltpu.CompilerParams(dimension_semantics=sem, vmem_limit_bytes=vmem)


def _const_spec(shape):
    nd = len(shape)
    return pl.BlockSpec(shape, lambda *_: (0,) * nd, pipeline_mode=pl.Buffered(1))


def _dot(a, b):
    return jnp.dot(a, b, preferred_element_type=F32)


def _dot_nt(a, b):
    return lax.dot_general(a, b, (((1,), (1,)), ((), ())), preferred_element_type=F32)


def _dot_tn(a, b):
    return lax.dot_general(a, b, (((0,), (0,)), ((), ())), preferred_element_type=F32)


def _colsum(x):
    return jnp.sum(x, axis=0, keepdims=True)


def _sid(i, nct):
    if nct == 0:
        return 1
    return jnp.where(i >= nct, 1, 0)


def _n_streams(nct):
    return 2 if nct else 1


def _stat_sid(i, nct):
    return _sid(i, nct) if nct else 0


def _first_of_stream(i, nct):
    if nct == 0:
        return i == 0
    return jnp.logical_or(i == 0, i == nct)


def _normmod(h, ng, sh, sc):
    rstd = lax.rsqrt(jnp.mean(h * h, axis=-1, keepdims=True) + EPS)
    xhat = h * rstd
    n = xhat * ng
    return n * (1.0 + sc) + sh, xhat, rstd, n


def _normmod_bwd(da, xhat, rstd, n, ng, sc):
    dsh = _colsum(da)
    dsc = _colsum(da * n)
    dn = da * (1.0 + sc)
    dng = _colsum(dn * xhat)
    dxhat = dn * ng
    dh = rstd * (dxhat - xhat * jnp.mean(dxhat * xhat, axis=-1, keepdims=True))
    return dh, dsh, dsc, dng


def _acc_rows(ref, first, rows):
    @pl.when(first)
    def _():
        ref[...] = jnp.zeros_like(ref)
    for r, val in enumerate(rows):
        ref[r:r + 1, :] = ref[r:r + 1, :] + val


_GELU_C = math.sqrt(2.0 / math.pi)


def _gelu(x):
    t = jnp.tanh(_GELU_C * (x + 0.044715 * x * x * x))
    return 0.5 * x * (1.0 + t)


def _gelu_grad(x):
    x2 = x * x
    t = jnp.tanh(_GELU_C * (x + 0.044715 * x * x2))
    return 0.5 * (1.0 + t) + 0.5 * x * (1.0 - t * t) * _GELU_C * (1.0 + 3.0 * 0.044715 * x2)


def _coords():
    return lax.axis_index("x"), lax.axis_index("y"), lax.axis_index("c")


def _dev_index(px, py, pc):
    return 4 * px + 2 * py + pc


def _all_gather(xs, name):
    n = len(xs)

    def body(*refs):
        x_refs, o_refs = refs[:n], refs[n:2 * n]
        send_sems, recv_sems, local_sems = refs[2 * n:]
        x, y, c = _coords()
        me, sibling = (x, y, c), (x, y, 1 - c)
        chips = [(1 - x, y), (x, 1 - y), (1 - x, 1 - y)]

        def copy(a, k, block, to, src=None):
            dst = o_refs[a].at[_dev_index(*block)]
            return pltpu.make_async_remote_copy(
                src_ref=dst if src is None else src, dst_ref=dst,
                send_sem=send_sems.at[7 * a + k], recv_sem=recv_sems.at[7 * a + k],
                device_id=to, device_id_type=MESH_ID)

        mine = [pltpu.make_async_copy(x_refs[a], o_refs[a].at[_dev_index(*me)], local_sems.at[a])
                for a in range(n)]
        for cp in mine:
            cp.start()
        first = []
        for a in range(n):
            first.append(copy(a, 0, me, sibling, src=x_refs[a]))
            first += [copy(a, 1 + j, me, (*chip, c), src=x_refs[a]) for j, chip in enumerate(chips)]
        for cp in first:
            cp.start()
        passed = []
        for a in range(n):
            for j, chip in enumerate(chips):
                copy(a, 1 + j, (*chip, c), me).wait_recv()
                fwd = copy(a, 4 + j, (*chip, c), sibling)
                fwd.start()
                passed.append(fwd)
        for a in range(n):
            copy(a, 0, sibling, me).wait_recv()
            for j, chip in enumerate(chips):
                copy(a, 4 + j, (*chip, 1 - c), me).wait_recv()
        for cp in first + passed:
            cp.wait_send()
        for cp in mine:
            cp.wait()

    any_spec = pl.BlockSpec(memory_space=pl.ANY)
    outs = pl.pallas_call(
        body, name=name,
        out_shape=[jax.ShapeDtypeStruct((N_DEV,) + x.shape, x.dtype) for x in xs],
        in_specs=[any_spec] * n, out_specs=[any_spec] * n,
        scratch_shapes=[pltpu.SemaphoreType.DMA((7 * n,)), pltpu.SemaphoreType.DMA((7 * n,)),
                        pltpu.SemaphoreType.DMA((n,))],
    )(*xs)
    return list(outs)


class _Exchange:
    per_array = 0
    has_local = True

    def __init__(self, xs):
        self.xs = list(xs)
        n = len(self.xs)
        self.out_shapes = self._out_shapes()
        self.scratch = [pltpu.SemaphoreType.DMA((self.per_array * n,)),
                        pltpu.SemaphoreType.DMA((self.per_array * n,)),
                        pltpu.SemaphoreType.DMA((n,))]

    def _out_shapes(self):
        raise NotImplementedError

    def _copies(self, x_refs, o_refs, sems):
        raise NotImplementedError

    def start(self, x_refs, o_refs, sems):
        mine, sends, _ = self._copies(x_refs, o_refs, sems)
        for cp in mine + sends:
            cp.start()

    def finish(self, x_refs, o_refs, sems):
        mine, sends, arrivals = self._copies(x_refs, o_refs, sems)
        for make in arrivals:
            make().wait_recv()
        for cp in sends:
            cp.wait_send()
        for cp in mine:
            cp.wait()


def _remote(src, dst, sems, k, to):
    return pltpu.make_async_remote_copy(src_ref=src, dst_ref=dst, send_sem=sems[0].at[k], recv_sem=sems[1].at[k],
                                        device_id=to, device_id_type=MESH_ID)


class _GatherAcrossChips(_Exchange):
    per_array = 4

    def _out_shapes(self):
        return [jax.ShapeDtypeStruct((N_DEV,) + x.shape, x.dtype) for x in self.xs]

    def _copies(self, x_refs, o_refs, sems):
        x, y, c = _coords()
        targets = [(x, y, 1 - c), (1 - x, y, c), (x, 1 - y, c), (1 - x, 1 - y, c)]
        mine, sends, arrivals = [], [], []
        for a, (x_ref, o_ref) in enumerate(zip(x_refs, o_refs)):
            own = o_ref.at[_dev_index(x, y, c)]
            mine.append(pltpu.make_async_copy(x_ref, own, sems[2].at[a]))
            for k, to in enumerate(targets):
                sends.append(_remote(x_ref, own, sems, 4 * a + k, to))
                arrivals.append(functools.partial(_remote, x_ref, o_ref.at[_dev_index(*to)], sems, 4 * a + k, to))
        return mine, sends, arrivals


class _ForwardToSibling(_Exchange):
    per_array = 3

    def _out_shapes(self):
        return [jax.ShapeDtypeStruct(x.shape, x.dtype) for x in self.xs]

    def _copies(self, x_refs, o_refs, sems):
        x, y, c = _coords()
        chips = [(1 - x, y), (x, 1 - y), (1 - x, 1 - y)]
        sends, arrivals = [], []
        for a, (x_ref, o_ref) in enumerate(zip(x_refs, o_refs)):
            for j, chip in enumerate(chips):
                held = _dev_index(*chip, c)
                sends.append(_remote(x_ref.at[held], o_ref.at[held], sems, 3 * a + j, (x, y, 1 - c)))
                theirs = _dev_index(*chip, 1 - c)
                arrivals.append(functools.partial(_remote, x_ref.at[theirs], o_ref.at[theirs], sems, 3 * a + j,
                                                  (x, y, 1 - c)))
        return [], sends, arrivals


class _AllToAll(_Exchange):
    per_array = 7

    def _out_shapes(self):
        return [jax.ShapeDtypeStruct(x.shape, x.dtype) for x in self.xs]

    def _copies(self, x_refs, o_refs, sems):
        x, y, c = _coords()
        me_i = _dev_index(x, y, c)
        mine, sends, arrivals = [], [], []
        for a, (x_ref, o_ref) in enumerate(zip(x_refs, o_refs)):
            mine.append(pltpu.make_async_copy(x_ref.at[me_i], o_ref.at[me_i], sems[2].at[a]))
            for r in range(1, 8):
                to = (1 - x if r & 4 else x, 1 - y if r & 2 else y, 1 - c if r & 1 else c)
                to_i = _dev_index(*to)
                sends.append(_remote(x_ref.at[to_i], o_ref.at[me_i], sems, 7 * a + r - 1, to))
                arrivals.append(functools.partial(_remote, x_ref.at[to_i], o_ref.at[to_i], sems, 7 * a + r - 1, to))
        return mine, sends, arrivals


def _exchange_call(ex, name, in_place=False):
    n = len(ex.xs)

    def body(*refs):
        x_refs, o_refs, sems = refs[:n], refs[n:2 * n], refs[2 * n:]
        ex.start(x_refs, o_refs, sems)
        ex.finish(x_refs, o_refs, sems)

    any_spec = pl.BlockSpec(memory_space=pl.ANY)
    outs = pl.pallas_call(
        body, name=name, out_shape=ex.out_shapes, in_specs=[any_spec] * n, out_specs=[any_spec] * n,
        scratch_shapes=ex.scratch, input_output_aliases={a: a for a in range(n)} if in_place else {},
    )(*ex.xs)
    return list(outs)


def _rider_parts(rider):
    if rider is None:
        return [], [], [], [], []
    any_spec = pl.BlockSpec(memory_space=pl.ANY)
    n = len(rider.xs)
    return rider.xs, [any_spec] * n, rider.out_shapes, [any_spec] * n, rider.scratch


def _grid_ends(grid):
    first = pl.program_id(0) == 0
    last = pl.program_id(0) == grid[0] - 1
    for ax in range(1, len(grid)):
        first = jnp.logical_and(first, pl.program_id(ax) == 0)
        last = jnp.logical_and(last, pl.program_id(ax) == grid[ax] - 1)
    return first, last


def _silu(x):
    return x * (1.0 / (1.0 + jnp.exp(-x)))


def _cond_rows(c_all, c_ctx):
    d = c_all.shape[-1]
    s = jnp.concatenate([c_all, jnp.zeros((8, d), F32)], axis=0)
    row = lax.broadcasted_iota(jnp.int32, (16, d), 0)
    s = jnp.where(row == 8, c_ctx, s)
    return jnp.where(row <= 8, _silu(s), 0.0)


def _mods_local(c_all, c_ctx, ada_w, ada_b_loc):
    nl, d, n = ada_w.shape

    def body(c_ref, cc_ref, w_ref, b_ref, o_ref):
        s = _cond_rows(c_ref[...], cc_ref[...])
        o_ref[...] = jnp.dot(s, w_ref[...], preferred_element_type=F32,
                             precision=lax.Precision.HIGHEST) + b_ref[...]

    return pl.pallas_call(
        body, name="mods_local", grid=(nl,),
        out_shape=jax.ShapeDtypeStruct((nl, 16, n), F32),
        in_specs=[pl.BlockSpec((8, d), lambda i: (0, 0)), pl.BlockSpec((1, d), lambda i: (0, 0)),
                  pl.BlockSpec((None, d, n), lambda i: (i, 0, 0)),
                  pl.BlockSpec((None, 1, n), lambda i: (i, 0, 0))],
        out_specs=pl.BlockSpec((None, 16, n), lambda i: (i, 0, 0)),
        compiler_params=_params(("arbitrary",)),
    )(c_all, c_ctx, ada_w, ada_b_loc)


def _ada_grads(c_all, c_ctx, ada_w, dm_lat, dm_ctx):
    nl, d, n = ada_w.shape

    def body(c_ref, cc_ref, w_ref, dml_ref, dmc_ref, gw_ref, ds_ref):
        i = pl.program_id(0)
        s = _cond_rows(c_ref[...], cc_ref[...])
        csum = dmc_ref[0:1, :]
        for k in range(1, N_DEV):
            csum = csum + dmc_ref[k:k + 1, :]
        row = lax.broadcasted_iota(jnp.int32, (8, n), 0)
        dm_c = jnp.where(row == 0, csum, 0.0)
        dm = jnp.concatenate([dml_ref[...], dm_c], axis=0)
        gw_ref[...] = lax.dot_general(s, dm, (((0,), (0,)), ((), ())), preferred_element_type=F32,
                                      precision=lax.Precision.HIGHEST)
        ds = lax.dot_general(dm_c, w_ref[...], (((1,), (1,)), ((), ())),
                             preferred_element_type=F32, precision=lax.Precision.HIGHEST)

        @pl.when(i == 0)
        def _():
            ds_ref[...] = jnp.zeros_like(ds_ref)
        ds_ref[...] += ds

    return pl.pallas_call(
        body, name="ada_grads", grid=(nl,),
        out_shape=[jax.ShapeDtypeStruct((nl, d, n), F32), jax.ShapeDtypeStruct((8, d), F32)],
        in_specs=[pl.BlockSpec((8, d), lambda i: (0, 0)), pl.BlockSpec((1, d), lambda i: (0, 0)),
                  pl.BlockSpec((None, d, n), lambda i: (i, 0, 0)),
                  pl.BlockSpec((None, 8, n), lambda i: (i, 0, 0)),
                  pl.BlockSpec((None, 8, n), lambda i: (i, 0, 0))],
        out_specs=[pl.BlockSpec((None, d, n), lambda i: (i, 0, 0)),
                   pl.BlockSpec((8, d), lambda i: (0, 0))],
        compiler_params=_params(("arbitrary",)),
    )(c_all, c_ctx, ada_w, dm_lat, dm_ctx)


def _cctx_grad(ds_parts, c_ctx):
    d = c_ctx.shape[-1]

    def body(p_ref, c_ref, o_ref):
        ds = p_ref[0]
        for k in range(1, N_DEV):
            ds = ds + p_ref[k]
        x = c_ref[...]
        sg = 1.0 / (1.0 + jnp.exp(-x))
        o_ref[...] = ds[0:1, :] * (sg * (1.0 + x * (1.0 - sg)))

    return pl.pallas_call(body, name="cctx_grad", out_shape=jax.ShapeDtypeStruct((1, d), F32))(ds_parts, c_ctx)


def _mlp_fwd(h1, mods, ng, w1, w2, nct, tm, row_off=0):
    d = h1.shape[1]
    r = h1.shape[0] - row_off * tm
    f = w1.shape[1]
    fc = min(MLP_FC, f)

    def body(h_ref, mod_ref, ng_ref, w1_ref, w2_ref, h2_ref, p_ref, y_ref):
        h = h_ref[...]
        a, _, _, _ = _normmod(h, ng_ref[...], mod_ref[3:4, :], mod_ref[4:5, :])
        ab = a.astype(BF16)
        acc = jnp.zeros((tm, d), F32)
        for j in range(f // fc):
            sl = slice(j * fc, (j + 1) * fc)
            p = jnp.maximum(_dot(ab, w1_ref[:, sl]), 0.0)
            p_ref[:, sl] = p.astype(BF16)
            acc = acc + _dot((p * p).astype(BF16), w2_ref[sl, :])
        y_ref[...] = acc.astype(BF16)
        h2_ref[...] = h + mod_ref[5:6, :] * acc

    return pl.pallas_call(
        body, name="mlp_fwd", grid=(r // tm,),
        out_shape=[jax.ShapeDtypeStruct((r, d), F32), jax.ShapeDtypeStruct((r, f), BF16),
                   jax.ShapeDtypeStruct((r, d), BF16)],
        in_specs=[pl.BlockSpec((tm, d), lambda i: (i + row_off, 0)),
                  pl.BlockSpec((None, 8, d), lambda i: (_sid(i, nct), 0, 0)),
                  _const_spec((1, d)), _const_spec(w1.shape), _const_spec(w2.shape)],
        out_specs=[pl.BlockSpec((tm, d), lambda i: (i, 0)), pl.BlockSpec((tm, f), lambda i: (i, 0)),
                   pl.BlockSpec((tm, d), lambda i: (i, 0))],
        compiler_params=_params(("arbitrary",)),
    )(h1, mods, ng, w1, w2)


def _mlp_bwd(dh2, h1, p, y, mods, ng, w1, w2, nct, tm, row_off=0):
    r_rows, d = dh2.shape
    f = w1.shape[1]
    fc = min(MLP_FC, f)

    def body(dh_ref, h_ref, p_ref, y_ref, mod_ref, ng_ref, w1_ref, w2_ref,
             dh1_ref, m_ref, du_ref, r_ref, dacc_ref, st_ref):
        i = pl.program_id(0)
        dh = dh_ref[...]
        ngv, sc, gate = ng_ref[...], mod_ref[4:5, :], mod_ref[5:6, :]
        a, xhat, rstd, n = _normmod(h_ref[...], ngv, mod_ref[3:4, :], sc)
        m_ref[...] = a.astype(BF16)
        dgate = _colsum(dh * y_ref[...].astype(F32))
        dacc = (gate * dh).astype(BF16)
        dacc_ref[...] = dacc
        dm = jnp.zeros((tm, d), F32)
        for j in range(f // fc):
            sl = slice(j * fc, (j + 1) * fc)
            pj = p_ref[:, sl].astype(F32)
            du = (_dot_nt(dacc, w2_ref[sl, :]) * (2.0 * pj)).astype(BF16)
            du_ref[:, sl] = du
            r_ref[:, sl] = (pj * pj).astype(BF16)
            dm = dm + _dot_nt(du, w1_ref[:, sl])
        dhn, dsh, dsc, dng = _normmod_bwd(dm, xhat, rstd, n, ngv, sc)
        dh1_ref[...] = dh + dhn
        _acc_rows(st_ref, _first_of_stream(i, nct), [dsh, dsc, dgate, dng])

    return pl.pallas_call(
        body, name="mlp_bwd", grid=(r_rows // tm,),
        out_shape=[jax.ShapeDtypeStruct((r_rows, d), F32), jax.ShapeDtypeStruct((r_rows, d), BF16),
                   jax.ShapeDtypeStruct((r_rows, f), BF16), jax.ShapeDtypeStruct((r_rows, f), BF16),
                   jax.ShapeDtypeStruct((r_rows, d), BF16), jax.ShapeDtypeStruct((_n_streams(nct), 8, d), F32)],
        in_specs=[pl.BlockSpec((tm, d), lambda i: (i, 0)),
                  pl.BlockSpec((tm, d), lambda i: (i + row_off, 0)),
                  pl.BlockSpec((tm, f), lambda i: (i, 0)), pl.BlockSpec((tm, d), lambda i: (i, 0)),
                  pl.BlockSpec((None, 8, d), lambda i: (_sid(i, nct), 0, 0)),
                  _const_spec((1, d)), _const_spec(w1.shape), _const_spec(w2.shape)],
        out_specs=[pl.BlockSpec((tm, d), lambda i: (i, 0)), pl.BlockSpec((tm, d), lambda i: (i, 0)),
                   pl.BlockSpec((tm, f), lambda i: (i, 0)), pl.BlockSpec((tm, f), lambda i: (i, 0)),
                   pl.BlockSpec((tm, d), lambda i: (i, 0)),
                   pl.BlockSpec((None, 8, d), lambda i: (_stat_sid(i, nct), 0, 0))],
        compiler_params=_params(("arbitrary",)),
    )(dh2, h1, p, y, mods, ng, w1, w2)


def _pick(n, cands):
    for cand in cands:
        if n % cand == 0:
            return cand
    return n


def _tn_matmul(x, y, name, groups=0):
    rows, k1 = x.shape
    k2 = y.shape[1]
    bt = _pick(rows, (1024, 768, 512, 384, 256, 128))
    if groups:
        bk1, bk2 = k1 // groups, k2 // groups
        grid = (groups, 1, rows // bt)
        x_map = lambda g, j, t: (t, g)
        y_map = lambda g, j, t: (t, g)
        out_shape = jax.ShapeDtypeStruct((groups, bk1, bk2), BF16)
        out_spec = pl.BlockSpec((None, bk1, bk2), lambda g, j, t: (g, 0, 0))
    else:
        bk1, bk2 = min(k1, 1024), min(k2, 1024)
        grid = (k1 // bk1, k2 // bk2, rows // bt)
        x_map = lambda i, j, t: (t, i)
        y_map = lambda i, j, t: (t, j)
        out_shape = jax.ShapeDtypeStruct((k1, k2), BF16)
        out_spec = pl.BlockSpec((bk1, bk2), lambda i, j, t: (i, j))
    nt = rows // bt

    def body(x_ref, y_ref, o_ref, acc_ref):
        t = pl.program_id(2)

        @pl.when(t == 0)
        def _():
            acc_ref[...] = jnp.zeros_like(acc_ref)
        acc_ref[...] += _dot_tn(x_ref[...], y_ref[...])

        @pl.when(t == nt - 1)
        def _():
            o_ref[...] = acc_ref[...].astype(BF16)

    return pl.pallas_call(
        body, name=name, grid=grid, out_shape=out_shape,
        in_specs=[pl.BlockSpec((bt, bk1), x_map), pl.BlockSpec((bt, bk2), y_map)],
        out_specs=out_spec,
        scratch_shapes=[pltpu.VMEM((bk1, bk2), F32)],
        compiler_params=_params(("parallel", "parallel", "arbitrary")),
    )(x, y)


def _pool_bands(tm):
    k = tm + 128
    t = np.arange(tm)[:, None]
    e = np.arange(k)[None, :]
    fwd, bwd = [], []
    for w in POOL_WINDOWS:
        lo = POOL_HALO + t - w // 2
        fwd.append(((e >= lo) & (e <= lo + w - 1)).astype(np.float32))
        lo_t = POOL_HALO + t - w // 2 + 1
        bwd.append(((e >= lo_t) & (e <= lo_t + w - 1)).astype(np.float32))
    return jnp.asarray(np.stack(fwd), BF16), jnp.asarray(np.stack(bwd), BF16)


def _pool_geometry(i, nct, n_tiles, tm, c_len, l_len):
    if nct == 0:
        pos0 = i * tm
        ls = l_len
        has_prev = i > 0
        has_next = i < n_tiles - 1
    else:
        in_ctx = i < nct
        pos0 = jnp.where(in_ctx, i, i - nct) * tm
        ls = jnp.where(in_ctx, c_len, l_len)
        has_prev = jnp.logical_and(i != 0, i != nct)
        has_next = jnp.logical_and(i != nct - 1, i != n_tiles - 1)
    return pos0, ls, has_prev, has_next


def _window_inv_counts(pos, ls):
    out = []
    for w in POOL_WINDOWS:
        lo = jnp.maximum(pos - w // 2, 0)
        hi = jnp.minimum(pos + w - w // 2, ls)
        cnt = jnp.maximum(hi - lo, 1).astype(F32)
        out.append(1.0 / cnt)
    return out


def _split_bf16(x):
    hi = x.astype(BF16)
    return hi, (x - hi.astype(F32)).astype(BF16)


def _extend(prev, tile, nxt, has_prev, has_next):
    w = tile.shape[1]
    prev = jnp.where(has_prev, prev, 0.0)
    nxt = jnp.where(has_next, nxt, 0.0)
    return jnp.concatenate([prev, tile, nxt, jnp.zeros((128 - 2 * POOL_HALO, w), F32)], axis=0)


def _pool_specs(tm, d, n_rows):
    last8 = n_rows // POOL_HALO - 1
    per = tm // POOL_HALO
    return [pl.BlockSpec((tm, d), lambda i: (i, 0)),
            pl.BlockSpec((POOL_HALO, d), lambda i: (jnp.maximum(i * per - 1, 0), 0)),
            pl.BlockSpec((POOL_HALO, d), lambda i: (jnp.minimum((i + 1) * per, last8), 0))]


def _pool_fwd(h, mods, ng, w, scale, bands, nct, tm, c_len, l_len):
    r, d = h.shape
    gw = d // POOL_GROUPS
    n_tiles = r // tm
    kx = tm + 128

    def body(h_ref, hp_ref, hn_ref, mod_ref, ng_ref, w_ref, sc_ref, band_ref, y_ref, h1_ref):
        i = pl.program_id(0)
        pos0, ls, has_prev, has_next = _pool_geometry(i, nct, n_tiles, tm, c_len, l_len)
        ngv, sh, sc = ng_ref[...], mod_ref[0:1, :], mod_ref[1:2, :]
        h = h_ref[...]
        a = _normmod(h, ngv, sh, sc)[0]
        a_ext = _extend(_normmod(hp_ref[...], ngv, sh, sc)[0], a, _normmod(hn_ref[...], ngv, sh, sc)[0],
                        has_prev, has_next)
        pos = pos0 + lax.broadcasted_iota(jnp.int32, (tm, 1), 0)
        inv = _window_inv_counts(pos, ls)
        ys = []
        for g in range(POOL_GROUPS):
            cols = slice(g * gw, (g + 1) * gw)
            hi, lo = _split_bf16(a_ext[:, cols])
            s = _dot(band_ref[g], hi) + _dot(band_ref[g], lo)
            pg = s * inv[g] - a[:, cols]
            ys.append(_dot(pg.astype(BF16), w_ref[g]))
        y = jnp.concatenate(ys, axis=1) * sc_ref[...]
        y_ref[...] = y.astype(BF16)
        h1_ref[...] = h + mod_ref[2:3, :] * y

    return pl.pallas_call(
        body, name="pool_fwd", grid=(n_tiles,),
        out_shape=[jax.ShapeDtypeStruct((r, d), BF16), jax.ShapeDtypeStruct((r, d), F32)],
        in_specs=_pool_specs(tm, d, r) + [
            pl.BlockSpec((None, 8, d), lambda i: (_sid(i, nct), 0, 0)),
            _const_spec((1, d)), _const_spec(w.shape), _const_spec((1, d)), _const_spec((4, tm, kx))],
        out_specs=[pl.BlockSpec((tm, d), lambda i: (i, 0)), pl.BlockSpec((tm, d), lambda i: (i, 0))],
        compiler_params=_params(("arbitrary",)),
    )(h, h, h, mods, ng, w, scale, bands[0])


def _pool_bwd(dh1, h, y, mods, ng, w, scale, bands, nct, tm, c_len, l_len, latent_out):
    r, d = h.shape
    gw = d // POOL_GROUPS
    n_tiles = r // tm
    kx = tm + 128
    out_rows = l_len if latent_out else r
    out_off = nct if latent_out else 0

    def body(dh_ref, dhp_ref, dhn_ref, h_ref, hp_ref, hn_ref, y_ref, mod_ref, ng_ref, w_ref, sc_ref,
             bf_ref, bb_ref, dho_ref, dw_ref, st_ref):
        i = pl.program_id(0)
        pos0, ls, has_prev, has_next = _pool_geometry(i, nct, n_tiles, tm, c_len, l_len)
        ngv, sh, sc, gate = ng_ref[...], mod_ref[0:1, :], mod_ref[1:2, :], mod_ref[2:3, :]
        scale_v = sc_ref[...]
        h = h_ref[...]
        a, xhat, rstd, n = _normmod(h, ngv, sh, sc)
        a_ext = _extend(_normmod(hp_ref[...], ngv, sh, sc)[0], a, _normmod(hn_ref[...], ngv, sh, sc)[0],
                        has_prev, has_next)
        dh = dh_ref[...]
        dgate = _colsum(dh * y_ref[...].astype(F32))
        dy = gate * dh
        dy_ext = _extend(gate * dhp_ref[...], dy, gate * dhn_ref[...], has_prev, has_next)
        dyp_ext = (dy_ext * scale_v).astype(BF16)
        dyp = (dy * scale_v).astype(BF16)
        pos = pos0 + lax.broadcasted_iota(jnp.int32, (tm, 1), 0)
        inv = _window_inv_counts(pos, ls)
        pos_e = pos0 - POOL_HALO + lax.broadcasted_iota(jnp.int32, (kx, 1), 0)
        inv_e = _window_inv_counts(pos_e, ls)

        @pl.when(i == 0)
        def _():
            dw_ref[...] = jnp.zeros_like(dw_ref)
        das, dscale = [], []
        for g in range(POOL_GROUPS):
            cols = slice(g * gw, (g + 1) * gw)
            hi, lo = _split_bf16(a_ext[:, cols])
            pg = ((_dot(bf_ref[g], hi) + _dot(bf_ref[g], lo)) * inv[g] - a[:, cols]).astype(BF16)
            dscale.append(_colsum(dy[:, cols] * _dot(pg, w_ref[g])))
            dyp_g = dyp_ext[:, cols]
            dw_ref[g] += _dot_tn(pg, dyp[:, cols])
            dp_ext = _dot_nt(dyp_g, w_ref[g])
            hi, lo = _split_bf16(dp_ext * inv_e[g])
            das.append(_dot(bb_ref[g], hi) + _dot(bb_ref[g], lo) - dp_ext[POOL_HALO:POOL_HALO + tm, :])
        da = jnp.concatenate(das, axis=1)
        dhn, dsh, dsc, dng = _normmod_bwd(da, xhat, rstd, n, ngv, sc)
        dho_ref[...] = dh + dhn
        _acc_rows(st_ref, _first_of_stream(i, nct), [dsh, dsc, dgate, dng, jnp.concatenate(dscale, axis=1)])

    return pl.pallas_call(
        body, name="pool_bwd", grid=(n_tiles,),
        out_shape=[jax.ShapeDtypeStruct((out_rows, d), F32),
                   jax.ShapeDtypeStruct((POOL_GROUPS, gw, gw), F32),
                   jax.ShapeDtypeStruct((_n_streams(nct), 8, d), F32)],
        in_specs=_pool_specs(tm, d, r) + _pool_specs(tm, d, r) + [
            pl.BlockSpec((tm, d), lambda i: (i, 0)),
            pl.BlockSpec((None, 8, d), lambda i: (_sid(i, nct), 0, 0)),
            _const_spec((1, d)), _const_spec(w.shape), _const_spec((1, d)),
            _const_spec((4, tm, kx)), _const_spec((4, tm, kx))],
        out_specs=[pl.BlockSpec((tm, d), lambda i: (jnp.maximum(i - out_off, 0), 0)),
                   pl.BlockSpec((POOL_GROUPS, gw, gw), lambda i: (0, 0, 0)),
                   pl.BlockSpec((None, 8, d), lambda i: (_stat_sid(i, nct), 0, 0))],
        compiler_params=_params(("arbitrary",)),
    )(dh1, dh1, dh1, h, h, h, y, mods, ng, w, scale, bands[0], bands[1])


def _rope_tables(c_len, l_len):
    half = HEAD_DIM // 2
    t = np.arange(l_len)
    row = (t // GRID_W).astype(np.float32)
    col = (t % GRID_W).astype(np.float32)
    inv = (np.float32(ROPE_BASE) ** (-np.arange(0, half, 2, dtype=np.float32) / np.float32(half))).astype(np.float32)
    ang_r = row[:, None] * inv[None, :]
    ang_c = col[:, None] * inv[None, :]
    cos = np.concatenate([np.cos(ang_r), np.cos(ang_r), np.cos(ang_c), np.cos(ang_c)], axis=1)
    sin = np.concatenate([-np.sin(ang_r), np.sin(ang_r), -np.sin(ang_c), np.sin(ang_c)], axis=1)
    cos = np.concatenate([np.ones((c_len, HEAD_DIM), np.float32), cos.astype(np.float32)], axis=0)
    sin = np.concatenate([np.zeros((c_len, HEAD_DIM), np.float32), sin.astype(np.float32)], axis=0)
    return jnp.asarray(cos, F32), jnp.asarray(sin, F32)


def _swap_pairs(x):
    lane = lax.broadcasted_iota(jnp.int32, x.shape, 1)
    return jnp.where((lane % 64) < 32, pltpu.roll(x, 96, 1), pltpu.roll(x, 32, 1))


def _head_norm(x, g):
    rstd = lax.rsqrt(jnp.mean(x * x, axis=-1, keepdims=True) + EPS)
    xhat = x * rstd
    return xhat * g, xhat, rstd


def _qkv_fwd(h, mods, ng, w, qg, kg, cos, sin, n_heads, n_kv, nct, tm):
    t_rows, d = h.shape
    qw, kw = n_heads * HEAD_DIM, n_kv * HEAD_DIM

    def body(h_ref, mod_ref, ng_ref, w_ref, qg_ref, kg_ref, cos_ref, sin_ref, q_ref, k_ref, v_ref):
        a = _normmod(h_ref[...], ng_ref[...], mod_ref[0:1, :], mod_ref[1:2, :])[0]
        qkv = _dot(a.astype(BF16), w_ref[...])
        cosv, sinv = cos_ref[...], sin_ref[...]
        ones = jnp.ones((tm, HEAD_DIM), BF16)
        for hd in range(n_heads + n_kv):
            cols = slice(hd * HEAD_DIM, (hd + 1) * HEAD_DIM)
            xn = _head_norm(qkv[:, cols], qg_ref[...] if hd < n_heads else kg_ref[...])[0]
            xr = xn * cosv + _swap_pairs(xn) * sinv
            if hd < n_heads:
                q_ref[:, cols] = (xr * Q_SCALE).astype(BF16)
            else:
                k_ref[:, (hd - n_heads) * HEAD_DIM:(hd - n_heads + 1) * HEAD_DIM] = xr.astype(BF16)
        for g in range(n_kv):
            v_ref[:, (2 * g) * HEAD_DIM:(2 * g + 1) * HEAD_DIM] = (
                qkv[:, qw + kw + g * HEAD_DIM:qw + kw + (g + 1) * HEAD_DIM].astype(BF16))
            v_ref[:, (2 * g + 1) * HEAD_DIM:(2 * g + 2) * HEAD_DIM] = ones

    return pl.pallas_call(
        body, name="qkv_fwd", grid=(t_rows // tm,),
        out_shape=[jax.ShapeDtypeStruct((t_rows - nct * tm, qw), BF16), jax.ShapeDtypeStruct((t_rows, kw), BF16),
                   jax.ShapeDtypeStruct((t_rows, 2 * kw), BF16)],
        in_specs=[pl.BlockSpec((tm, d), lambda i: (i, 0)),
                  pl.BlockSpec((None, 8, d), lambda i: (_sid(i, nct), 0, 0)),
                  _const_spec((1, d)), _const_spec(w.shape), _const_spec((1, HEAD_DIM)),
                  _const_spec((1, HEAD_DIM)),
                  pl.BlockSpec((tm, HEAD_DIM), lambda i: (i, 0)), pl.BlockSpec((tm, HEAD_DIM), lambda i: (i, 0))],
        out_specs=[pl.BlockSpec((tm, qw), lambda i: (jnp.maximum(i - nct, 0), 0)),
                   pl.BlockSpec((tm, kw), lambda i: (i, 0)), pl.BlockSpec((tm, 2 * kw), lambda i: (i, 0))],
        compiler_params=_params(("arbitrary",)),
    )(h, mods, ng, w, qg, kg, cos, sin)


def _stack_heads(x):
    return jnp.concatenate([x[:, :HEAD_DIM], x[:, HEAD_DIM:]], axis=0)


def _unstack_heads(x, tq):
    return jnp.concatenate([x[:tq], x[tq:]], axis=1)


def _flash_tk(t_rows, tm):
    best = tm
    k = tm
    while k <= FLASH_TK_CAP:
        if t_rows % k == 0:
            best = k
        k += tm
    return best


def _flash_fwd(q, k, v1, n_kv, tq, tm, rider=None):
    t_rows = k.shape[0]
    l_rows = q.shape[0]
    tk = _flash_tk(t_rows, tm)
    nk = t_rows // tk
    gq = 2 * HEAD_DIM

    def body(q_ref, k_ref, v_ref, o_ref, lse_ref, m_s, acc_s):
        ki = pl.program_id(2)

        @pl.when(ki == 0)
        def _():
            m_s[...] = jnp.full_like(m_s, -jnp.inf)
            acc_s[...] = jnp.zeros_like(acc_s)
        kk, vv = k_ref[...], v_ref[...]
        for hh in range(2):
            s = _dot_nt(q_ref[:, hh * HEAD_DIM:(hh + 1) * HEAD_DIM], kk)
            m_prev = m_s[hh]
            m_new = jnp.maximum(m_prev, jnp.max(s, axis=-1, keepdims=True))
            alpha = jnp.exp2(m_prev - m_new)
            p = jnp.exp2(s - jnp.tile(m_new, (1, tk // HEAD_DIM)))
            acc_s[hh] = jnp.tile(alpha, (1, 2)) * acc_s[hh] + _dot(p.astype(BF16), vv)
            m_s[hh] = m_new

        @pl.when(ki == nk - 1)
        def _():
            for hh in range(2):
                acc = acc_s[hh]
                l = acc[:, HEAD_DIM:]
                o_ref[:, hh * HEAD_DIM:(hh + 1) * HEAD_DIM] = (acc[:, :HEAD_DIM] / l).astype(BF16)
                lse_ref[:, hh:hh + 1] = (m_s[hh] + jnp.log2(l))[:, 0:1]

    grid = (n_kv, l_rows // tq, nk)
    r_in, r_in_specs, r_out, r_out_specs, r_scratch = _rider_parts(rider)
    nr = len(r_in)

    def body_with_rider(*refs):
        ins, r_x = refs[:3], refs[3:3 + nr]
        outs, r_o = refs[3 + nr:5 + nr], refs[5 + nr:5 + 2 * nr]
        scratch, r_sems = refs[5 + 2 * nr:7 + 2 * nr], refs[7 + 2 * nr:]
        first, last = _grid_ends(grid)
        if rider is not None:
            pl.when(first)(lambda: rider.start(r_x, r_o, r_sems))
        body(*ins, *outs, *scratch)
        if rider is not None:
            pl.when(last)(lambda: rider.finish(r_x, r_o, r_sems))

    res = pl.pallas_call(
        body_with_rider, name="flash_fwd", grid=grid,
        out_shape=[jax.ShapeDtypeStruct((l_rows, n_kv * gq), BF16),
                   jax.ShapeDtypeStruct((n_kv, l_rows, 2), F32)] + r_out,
        in_specs=[pl.BlockSpec((tq, gq), lambda g, i, j: (i, g)),
                  pl.BlockSpec((tk, HEAD_DIM), lambda g, i, j: (j, g)),
                  pl.BlockSpec((tk, gq), lambda g, i, j: (j, g))] + r_in_specs,
        out_specs=[pl.BlockSpec((tq, gq), lambda g, i, j: (i, g)),
                   pl.BlockSpec((None, tq, 2), lambda g, i, j: (g, i, 0))] + r_out_specs,
        scratch_shapes=[pltpu.VMEM((2, tq, HEAD_DIM), F32), pltpu.VMEM((2, tq, gq), F32)] + r_scratch,
        compiler_params=_params(("arbitrary", "arbitrary", "arbitrary")),
    )(q, k, v1, *r_in)
    return res[0], res[1], list(res[2:])


def _flash_bwd(q, k, v1, do, lse, delta, n_kv, tq, tm, rider=None):
    t_rows = k.shape[0]
    l_rows = q.shape[0]
    tk = _flash_tk(t_rows, tm)
    nq = l_rows // tq
    gq = 2 * HEAD_DIM

    def body(q_ref, k_ref, v_ref, do_ref, lse_ref, dl_ref, dq_ref, dk_ref, dv_ref):
        ki, qi = pl.program_id(1), pl.program_id(2)
        rows = pl.ds(pl.multiple_of(qi * tq, tq), tq)

        @pl.when(qi == 0)
        def _():
            dk_ref[...] = jnp.zeros_like(dk_ref)
            dv_ref[...] = jnp.zeros_like(dv_ref)

        @pl.when(ki == 0)
        def _():
            dq_ref[rows, :] = jnp.zeros((tq, gq), F32)
        q2 = _stack_heads(q_ref[...])
        do2 = _stack_heads(do_ref[...])
        lse2 = _stack_heads_narrow(lse_ref[...])
        dl2 = _stack_heads_narrow(dl_ref[...])
        kk, vv = k_ref[...], v_ref[:, :HEAD_DIM]
        p = jnp.exp2(_dot_nt(q2, kk) - lse2)
        dv_ref[...] += _dot_tn(p.astype(BF16), do2)
        ds = (p * (_dot_nt(do2, vv) - dl2)).astype(BF16)
        dk_ref[...] += _dot_tn(ds, q2)
        dq_ref[rows, :] += _unstack_heads(_dot(ds, kk), tq)

    grid = (n_kv, t_rows // tk, nq)
    r_in, r_in_specs, r_out, r_out_specs, r_scratch = _rider_parts(rider)
    nr = len(r_in)

    def body_with_rider(*refs):
        ins, r_x = refs[:6], refs[6:6 + nr]
        outs, r_o = refs[6 + nr:9 + nr], refs[9 + nr:9 + 2 * nr]
        r_sems = refs[9 + 2 * nr:]
        first, last = _grid_ends(grid)
        if rider is not None:
            pl.when(first)(lambda: rider.start(r_x, r_o, r_sems))
        body(*ins, *outs)
        if rider is not None:
            pl.when(last)(lambda: rider.finish(r_x, r_o, r_sems))

    res = pl.pallas_call(
        body_with_rider, name="flash_bwd", grid=grid,
        out_shape=[jax.ShapeDtypeStruct((l_rows, n_kv * gq), F32),
                   jax.ShapeDtypeStruct((t_rows, n_kv * HEAD_DIM), F32),
                   jax.ShapeDtypeStruct((t_rows, n_kv * HEAD_DIM), F32)] + r_out,
        in_specs=[pl.BlockSpec((tq, gq), lambda g, j, i: (i, g)),
                  pl.BlockSpec((tk, HEAD_DIM), lambda g, j, i: (j, g)),
                  pl.BlockSpec((tk, gq), lambda g, j, i: (j, g)),
                  pl.BlockSpec((tq, gq), lambda g, j, i: (i, g)),
                  pl.BlockSpec((None, tq, 2), lambda g, j, i: (g, i, 0)),
                  pl.BlockSpec((None, tq, 2), lambda g, j, i: (g, i, 0))] + r_in_specs,
        out_specs=[pl.BlockSpec((l_rows, gq), lambda g, j, i: (0, g)),
                   pl.BlockSpec((tk, HEAD_DIM), lambda g, j, i: (j, g)),
                   pl.BlockSpec((tk, HEAD_DIM), lambda g, j, i: (j, g))] + r_out_specs,
        scratch_shapes=r_scratch,
        compiler_params=_params(("arbitrary", "arbitrary", "arbitrary")),
    )(q, k, v1, do, lse, delta, *r_in)
    return res[0], res[1], res[2], list(res[3:])


def _stack_heads_narrow(x):
    return jnp.concatenate([x[:, 0:1], x[:, 1:2]], axis=0)


def _wo_fwd(h, o, wo, mods, nct, tm):
    l_rows, z = o.shape
    d = h.shape[1]

    def body(h_ref, o_ref, w_ref, mod_ref, y_ref, h1_ref):
        y = _dot(o_ref[...], w_ref[...])
        y_ref[...] = y.astype(BF16)
        h1_ref[...] = h_ref[...] + mod_ref[2:3, :] * y

    return pl.pallas_call(
        body, name="wo_fwd", grid=(l_rows // tm,),
        out_shape=[jax.ShapeDtypeStruct((l_rows, d), BF16), jax.ShapeDtypeStruct((l_rows, d), F32)],
        in_specs=[pl.BlockSpec((tm, d), lambda i: (i + nct, 0)), pl.BlockSpec((tm, z), lambda i: (i, 0)),
                  _const_spec(wo.shape), pl.BlockSpec((None, 8, d), lambda i: (1, 0, 0))],
        out_specs=[pl.BlockSpec((tm, d), lambda i: (i, 0)), pl.BlockSpec((tm, d), lambda i: (i, 0))],
        compiler_params=_params(("arbitrary",)),
    )(h, o, wo, mods)


def _wo_bwd(dh1, y, o, wo, mods, n_kv, tm):
    l_rows, z = o.shape
    d = dh1.shape[1]

    def body(dh_ref, y_ref, o_ref, w_ref, mod_ref, dy_ref, do_ref, dl_ref, st_ref):
        i = pl.program_id(0)
        dh = dh_ref[...]
        dgate = _colsum(dh * y_ref[...].astype(F32))
        dy = (mod_ref[2:3, :] * dh).astype(BF16)
        dy_ref[...] = dy
        do = _dot_nt(dy, w_ref[...])
        do_ref[...] = do.astype(BF16)
        prod = do * o_ref[...].astype(F32)
        for g in range(n_kv):
            d0 = jnp.sum(prod[:, (2 * g) * HEAD_DIM:(2 * g + 1) * HEAD_DIM], axis=-1, keepdims=True)
            d1 = jnp.sum(prod[:, (2 * g + 1) * HEAD_DIM:(2 * g + 2) * HEAD_DIM], axis=-1, keepdims=True)
            dl_ref[g] = jnp.concatenate([d0, d1], axis=1)
        zero = jnp.zeros((1, d), F32)
        _acc_rows(st_ref, i == 0, [zero, zero, dgate])

    return pl.pallas_call(
        body, name="wo_bwd", grid=(l_rows // tm,),
        out_shape=[jax.ShapeDtypeStruct((l_rows, d), BF16), jax.ShapeDtypeStruct((l_rows, z), BF16),
                   jax.ShapeDtypeStruct((n_kv, l_rows, 2), F32), jax.ShapeDtypeStruct((8, d), F32)],
        in_specs=[pl.BlockSpec((tm, d), lambda i: (i, 0)), pl.BlockSpec((tm, d), lambda i: (i, 0)),
                  pl.BlockSpec((tm, z), lambda i: (i, 0)), _const_spec(wo.shape),
                  pl.BlockSpec((None, 8, d), lambda i: (1, 0, 0))],
        out_specs=[pl.BlockSpec((tm, d), lambda i: (i, 0)), pl.BlockSpec((tm, z), lambda i: (i, 0)),
                   pl.BlockSpec((n_kv, tm, 2), lambda i: (0, i, 0)), pl.BlockSpec((8, d), lambda i: (0, 0))],
        compiler_params=_params(("arbitrary",)),
    )(dh1, y, o, wo, mods)


def _qkv_bwd(h, dh_lat, dq, dk, dv, mods, ng, w, qg, kg, cos, sin, n_heads, n_kv, nct, tm):
    t_rows, d = h.shape
    qw, kw = n_heads * HEAD_DIM, n_kv * HEAD_DIM
    scale = HEAD_DIM ** -0.5

    def body(h_ref, dhl_ref, dq_ref, dk_ref, dv_ref, mod_ref, ng_ref, w_ref, qg_ref, kg_ref, cos_ref,
             sin_ref, dh_ref, a_ref, dqkv_ref, st_ref, dg_ref):
        i = pl.program_id(0)
        lat = (i >= nct).astype(F32)
        ngv, sc = ng_ref[...], mod_ref[1:2, :]
        a, xhat, rstd, n = _normmod(h_ref[...], ngv, mod_ref[0:1, :], sc)
        ab = a.astype(BF16)
        a_ref[...] = ab
        qkv = _dot(ab, w_ref[...])
        cosv, sinv = cos_ref[...], sin_ref[...]
        dqg = jnp.zeros((1, HEAD_DIM), F32)
        dkg = jnp.zeros((1, HEAD_DIM), F32)
        for hd in range(n_heads + n_kv):
            cols = slice(hd * HEAD_DIM, (hd + 1) * HEAD_DIM)
            is_q = hd < n_heads
            g = qg_ref[...] if is_q else kg_ref[...]
            _, hx, hr = _head_norm(qkv[:, cols], g)
            if is_q:
                dxr = dq_ref[:, cols] * (scale * lat)
            else:
                dxr = dk_ref[:, (hd - n_heads) * HEAD_DIM:(hd - n_heads + 1) * HEAD_DIM] * (1.0 / LOG2E)
            dxn = dxr * cosv + _swap_pairs(dxr * sinv)
            if is_q:
                dqg = dqg + _colsum(dxn * hx)
            else:
                dkg = dkg + _colsum(dxn * hx)
            dxh = dxn * g
            dx = hr * (dxh - hx * jnp.mean(dxh * hx, axis=-1, keepdims=True))
            dqkv_ref[:, cols] = dx.astype(BF16)
        dqkv_ref[:, qw + kw:] = dv_ref[...].astype(BF16)
        da = _dot_nt(dqkv_ref[...], w_ref[...])
        dhn, dsh, dsc, dng = _normmod_bwd(da, xhat, rstd, n, ngv, sc)
        dh_ref[...] = dhl_ref[...] * lat + dhn
        _acc_rows(st_ref, _first_of_stream(i, nct), [dsh, dsc, jnp.zeros((1, d), F32), dng])
        _acc_rows(dg_ref, i == 0, [dqg, dkg])

    lat_map = lambda i: (jnp.maximum(i - nct, 0), 0)
    return pl.pallas_call(
        body, name="qkv_bwd", grid=(t_rows // tm,),
        out_shape=[jax.ShapeDtypeStruct((t_rows, d), F32), jax.ShapeDtypeStruct((t_rows, d), BF16),
                   jax.ShapeDtypeStruct((t_rows, qw + 2 * kw), BF16), jax.ShapeDtypeStruct((2, 8, d), F32),
                   jax.ShapeDtypeStruct((8, HEAD_DIM), F32)],
        in_specs=[pl.BlockSpec((tm, d), lambda i: (i, 0)), pl.BlockSpec((tm, d), lat_map),
                  pl.BlockSpec((tm, qw), lat_map), pl.BlockSpec((tm, kw), lambda i: (i, 0)),
                  pl.BlockSpec((tm, kw), lambda i: (i, 0)),
                  pl.BlockSpec((None, 8, d), lambda i: (_sid(i, nct), 0, 0)),
                  _const_spec((1, d)), _const_spec(w.shape), _const_spec((1, HEAD_DIM)),
                  _const_spec((1, HEAD_DIM)),
                  pl.BlockSpec((tm, HEAD_DIM), lambda i: (i, 0)), pl.BlockSpec((tm, HEAD_DIM), lambda i: (i, 0))],
        out_specs=[pl.BlockSpec((tm, d), lambda i: (i, 0)), pl.BlockSpec((tm, d), lambda i: (i, 0)),
                   pl.BlockSpec((tm, qw + 2 * kw), lambda i: (i, 0)),
                   pl.BlockSpec((None, 8, d), lambda i: (_sid(i, nct), 0, 0)),
                   pl.BlockSpec((8, HEAD_DIM), lambda i: (0, 0))],
        compiler_params=_params(("arbitrary",)),
    )(h, dh_lat, dq, dk, dv, mods, ng, w, qg, kg, cos, sin)


def _gmlp_core(a_bf, win_ref, lng, lnb, ws_ref, bst_ref, tm, half):
    z = _dot(a_bf, win_ref[...])
    zu, zv = z[:, :half], z[:, half:]
    u, v = _gelu(zu), _gelu(zv)
    mu = jnp.mean(v, axis=-1, keepdims=True)
    vc = v - mu
    rstd_v = lax.rsqrt(jnp.mean(vc * vc, axis=-1, keepdims=True) + EPS)
    vhat = vc * rstd_v
    vln = (vhat * lng + lnb).astype(BF16)
    gw = half // GMLP_GROUPS
    rows = []
    for ch in range(tm // CHUNK):
        rs = slice(ch * CHUNK, (ch + 1) * CHUNK)
        cols = []
        for g in range(GMLP_GROUPS):
            cs = slice(g * gw, (g + 1) * gw)
            cols.append(_dot(ws_ref[g], vln[rs, cs]) + bst_ref[:, g:g + 1])
        rows.append(jnp.concatenate(cols, axis=1))
    sv = rows[0] if len(rows) == 1 else jnp.concatenate(rows, axis=0)
    return zu, zv, u, vhat, rstd_v, vln, sv


def _gmlp_fwd(h, mods, ng, win, lng, lnb, ws, bst, wout, tm):
    l_rows, d = h.shape
    half = wout.shape[0]

    def body(h_ref, mod_ref, ng_ref, win_ref, lng_ref, lnb_ref, ws_ref, bst_ref, wout_ref, y_ref, h1_ref):
        hv = h_ref[...]
        a = _normmod(hv, ng_ref[...], mod_ref[0:1, :], mod_ref[1:2, :])[0]
        _, _, u, _, _, _, sv = _gmlp_core(a.astype(BF16), win_ref, lng_ref[...], lnb_ref[...], ws_ref,
                                          bst_ref, tm, half)
        y = _dot((u * sv).astype(BF16), wout_ref[...])
        y_ref[...] = y.astype(BF16)
        h1_ref[...] = hv + mod_ref[2:3, :] * y

    return pl.pallas_call(
        body, name="gmlp_fwd", grid=(l_rows // tm,),
        out_shape=[jax.ShapeDtypeStruct((l_rows, d), BF16), jax.ShapeDtypeStruct((l_rows, d), F32)],
        in_specs=[pl.BlockSpec((tm, d), lambda i: (i, 0)), pl.BlockSpec((None, 8, d), lambda i: (1, 0, 0)),
                  _const_spec((1, d)), _const_spec(win.shape), _const_spec((1, half)), _const_spec((1, half)),
                  _const_spec(ws.shape), _const_spec(bst.shape), _const_spec(wout.shape)],
        out_specs=[pl.BlockSpec((tm, d), lambda i: (i, 0)), pl.BlockSpec((tm, d), lambda i: (i, 0))],
        compiler_params=_params(("arbitrary",)),
    )(h, mods, ng, win, lng, lnb, ws, bst, wout)


def _gmlp_bwd(dh1, h, y, mods, ng, win, lng, lnb, ws, wst, bst, wout, tm):
    l_rows, d = h.shape
    half = wout.shape[0]
    gw = half // GMLP_GROUPS

    def body(dh_ref, h_ref, y_ref, mod_ref, ng_ref, win_ref, lng_ref, lnb_ref, ws_ref, wst_ref, bst_ref,
             wout_ref, dho_ref, a_ref, dz_ref, gt_ref, dy_ref, st_ref, ln_ref, dws_ref, dbs_ref):
        i = pl.program_id(0)
        ngv, sc = ng_ref[...], mod_ref[1:2, :]
        lngv = lng_ref[...]
        a, xhat, rstd, n = _normmod(h_ref[...], ngv, mod_ref[0:1, :], sc)
        ab = a.astype(BF16)
        a_ref[...] = ab
        zu, zv, u, vhat, rstd_v, vln, sv = _gmlp_core(ab, win_ref, lngv, lnb_ref[...], ws_ref, bst_ref,
                                                      tm, half)
        gt_ref[...] = (u * sv).astype(BF16)
        dh = dh_ref[...]
        dgate = _colsum(dh * y_ref[...].astype(F32))
        dy = (mod_ref[2:3, :] * dh).astype(BF16)
        dy_ref[...] = dy
        dgated = _dot_nt(dy, wout_ref[...])
        du = dgated * sv
        dsv = (dgated * u).astype(BF16)

        @pl.when(i == 0)
        def _():
            dws_ref[...] = jnp.zeros_like(dws_ref)
            dbs_ref[...] = jnp.zeros_like(dbs_ref)
        lane = lax.broadcasted_iota(jnp.int32, (CHUNK, 128), 1)
        dbs = jnp.zeros((CHUNK, 128), F32)
        rows = []
        for ch in range(tm // CHUNK):
            rs = slice(ch * CHUNK, (ch + 1) * CHUNK)
            cols = []
            for g in range(GMLP_GROUPS):
                cs = slice(g * gw, (g + 1) * gw)
                dsv_cg = dsv[rs, cs]
                dws_ref[g] += _dot_nt(dsv_cg, vln[rs, cs])
                cols.append(_dot(wst_ref[g], dsv_cg))
                dbs = dbs + jnp.where(lane == g, jnp.sum(dsv_cg.astype(F32), axis=-1, keepdims=True), 0.0)
            rows.append(jnp.concatenate(cols, axis=1))
        dbs_ref[...] += dbs
        dvln = rows[0] if len(rows) == 1 else jnp.concatenate(rows, axis=0)
        dlng = _colsum(dvln * vhat)
        dlnb = _colsum(dvln)
        dvh = dvln * lngv
        dv = rstd_v * (dvh - jnp.mean(dvh, axis=-1, keepdims=True)
                       - vhat * jnp.mean(dvh * vhat, axis=-1, keepdims=True))
        dz_ref[:, :half] = (du * _gelu_grad(zu)).astype(BF16)
        dz_ref[:, half:] = (dv * _gelu_grad(zv)).astype(BF16)
        da = _dot_nt(dz_ref[...], win_ref[...])
        dhn, dsh, dsc, dng = _normmod_bwd(da, xhat, rstd, n, ngv, sc)
        dho_ref[...] = dh + dhn
        _acc_rows(st_ref, i == 0, [dsh, dsc, dgate, dng])
        _acc_rows(ln_ref, i == 0, [dlng, dlnb])

    row = lambda w: pl.BlockSpec((tm, w), lambda i: (i, 0))
    return pl.pallas_call(
        body, name="gmlp_bwd", grid=(l_rows // tm,),
        out_shape=[jax.ShapeDtypeStruct((l_rows, d), F32), jax.ShapeDtypeStruct((l_rows, d), BF16),
                   jax.ShapeDtypeStruct((l_rows, 2 * half), BF16), jax.ShapeDtypeStruct((l_rows, half), BF16),
                   jax.ShapeDtypeStruct((l_rows, d), BF16), jax.ShapeDtypeStruct((8, d), F32),
                   jax.ShapeDtypeStruct((8, half), F32), jax.ShapeDtypeStruct(ws.shape, F32),
                   jax.ShapeDtypeStruct((CHUNK, 128), F32)],
        in_specs=[row(d), row(d), row(d), pl.BlockSpec((None, 8, d), lambda i: (1, 0, 0)),
                  _const_spec((1, d)), _const_spec(win.shape), _const_spec((1, half)), _const_spec((1, half)),
                  _const_spec(ws.shape), _const_spec(ws.shape), _const_spec(bst.shape), _const_spec(wout.shape)],
        out_specs=[row(d), row(d), row(2 * half), row(half), row(d),
                   pl.BlockSpec((8, d), lambda i: (0, 0)), pl.BlockSpec((8, half), lambda i: (0, 0)),
                   pl.BlockSpec(ws.shape, lambda i: (0, 0, 0)), pl.BlockSpec((CHUNK, 128), lambda i: (0, 0))],
        compiler_params=_params(("arbitrary",)),
    )(dh1, h, y, mods, ng, win, lng, lnb, ws, wst, bst, wout)


def _head(h, final_g, target, tm):
    l_rows, d = h.shape
    n_tiles = l_rows // tm

    def body(h_ref, g_ref, t_ref, dh_ref, loss_ref, dg_ref, acc_ref):
        i = pl.program_id(0)
        g = g_ref[...]
        hv = h_ref[...]
        rstd = lax.rsqrt(jnp.mean(hv * hv, axis=-1, keepdims=True) + EPS)
        xhat = hv * rstd
        e = xhat * g - t_ref[...]
        dout = e * (1.0 / d)
        dxhat = dout * g
        dh_ref[...] = rstd * (dxhat - xhat * jnp.mean(dxhat * xhat, axis=-1, keepdims=True))
        _acc_rows(dg_ref, i == 0, [_colsum(dout * xhat)])
        _acc_rows(acc_ref, i == 0, [_colsum(e * e)])

        @pl.when(i == n_tiles - 1)
        def _():
            total = jnp.sum(acc_ref[0:1, :], axis=-1, keepdims=True) * (0.5 / d)
            loss_ref[...] = jnp.broadcast_to(total, loss_ref.shape)

    return pl.pallas_call(
        body, name="loss_head", grid=(n_tiles,),
        out_shape=[jax.ShapeDtypeStruct((l_rows, d), F32), jax.ShapeDtypeStruct((8, 128), F32),
                   jax.ShapeDtypeStruct((8, d), F32)],
        in_specs=[pl.BlockSpec((tm, d), lambda i: (i, 0)), _const_spec((1, d)),
                  pl.BlockSpec((tm, d), lambda i: (i, 0))],
        out_specs=[pl.BlockSpec((tm, d), lambda i: (i, 0)), pl.BlockSpec((8, 128), lambda i: (0, 0)),
                   pl.BlockSpec((8, d), lambda i: (0, 0))],
        scratch_shapes=[pltpu.VMEM((8, d), F32)],
        compiler_params=_params(("arbitrary",)),
    )(h, final_g, target)


def _adamw(w, gparts, m, v, name):
    shape = w.shape
    cols = shape[-1]
    rows = int(np.prod(shape[:-1])) if len(shape) > 1 else 1
    nparts = gparts.shape[0]
    w2, m2, v2 = (t.reshape(rows, cols) for t in (w, m, v))
    g2 = gparts.reshape(nparts, rows, cols)
    tr = rows
    part_bytes = nparts * cols * gparts.dtype.itemsize
    for cand in (1024, 512, 256, 128, 64, 32, 16, 8):
        if rows * max(part_bytes, cols * 4) <= (2 << 20):
            break
        if rows % cand == 0 and cand < rows:
            tr = cand
            if cand * max(part_bytes, cols * 4) <= (2 << 20):
                break
    c1 = 1.0 - ADAM_B1 ** ADAM_STEP
    c2 = 1.0 - ADAM_B2 ** ADAM_STEP

    def body(w_ref, g_ref, m_ref, v_ref, go_ref, d_ref, mo_ref, vo_ref):
        g = g_ref[0].astype(F32)
        for k in range(1, nparts):
            g = g + g_ref[k].astype(F32)
        mn = ADAM_B1 * m_ref[...] + (1.0 - ADAM_B1) * g
        vn = ADAM_B2 * v_ref[...] + (1.0 - ADAM_B2) * (g * g)
        go_ref[...] = g
        mo_ref[...] = mn
        vo_ref[...] = vn
        d_ref[...] = -ADAM_LR * ((mn / c1) / (jnp.sqrt(vn / c2) + ADAM_EPS) + ADAM_WD * w_ref[...])

    spec = pl.BlockSpec((tr, cols), lambda i: (i, 0))
    outs = pl.pallas_call(
        body, name=name, grid=(rows // tr,),
        out_shape=[jax.ShapeDtypeStruct((rows, cols), F32)] * 4,
        in_specs=[spec, pl.BlockSpec((nparts, tr, cols), lambda i: (0, i, 0)), spec, spec],
        out_specs=[spec] * 4,
        compiler_params=_params(("parallel",)),
    )(w2, g2, m2, v2)
    return tuple(o.reshape(shape) for o in outs)


def _natural_cols(g):
    return jnp.moveaxis(g, 0, -2).reshape(g.shape[1:-1] + (N_DEV * g.shape[-1],))


def _natural_rows(g):
    return jnp.moveaxis(g, 0, -3).reshape(g.shape[1:-2] + (N_DEV * g.shape[-2], g.shape[-1]))


def _shard_cols(full):
    n = full.shape[-1] // N_DEV
    return jnp.moveaxis(full.reshape(full.shape[:-1] + (N_DEV, n)), -2, 0)


def _shard_rows(full):
    r = full.shape[-2] // N_DEV
    return jnp.moveaxis(full.reshape(full.shape[:-2] + (N_DEV, r, full.shape[-1])), -3, 0)


def _my_cols(gathered, me, n):
    return lax.dynamic_slice_in_dim(gathered, me * n, n, axis=gathered.ndim - 1)


def kernel(x, c, ctx, c_ctx, ada_w, ada_b, norm_g, mlp_w1, mlp_w2, pool_w, pool_scale, attn_w_qkv, attn_w_o, attn_q_g, attn_k_g, gm_w_in, gm_ln_g, gm_ln_b, gm_ws, gm_bs, gm_w_out, final_g, loss_target, m_c_ctx, m_ada_w, m_ada_b, m_norm_g, m_mlp_w1, m_mlp_w2, m_pool_w, m_pool_scale, m_attn_w_qkv, m_attn_w_o, m_attn_q_g, m_attn_k_g, m_gm_w_in, m_gm_ln_g, m_gm_ln_b, m_gm_ws, m_gm_bs, m_gm_w_out, m_final_g, v_c_ctx, v_ada_w, v_ada_b, v_norm_g, v_mlp_w1, v_mlp_w2, v_pool_w, v_pool_scale, v_attn_w_qkv, v_attn_w_o, v_attn_q_g, v_attn_k_g, v_gm_w_in, v_gm_ln_g, v_gm_ln_b, v_gm_ws, v_gm_bs, v_gm_w_out, v_final_g):
    l_len, d = x.shape[1], x.shape[2]
    c_len = ctx.shape[1]
    n_layers = ada_w.shape[0]
    assert n_layers == 4 and x.shape[0] == 1
    n_heads = d // HEAD_DIM
    n_kv = n_heads // 2
    half = gm_w_out.shape[1] * N_DEV
    tm = c_len if c_len <= 256 else 256
    assert c_len % tm == 0 and l_len % tm == 0 and tm % CHUNK == 0 and l_len % GRID_W == 0
    nct = c_len // tm
    me = _dev_index(*_coords())
    n_ada = ada_w.shape[-1]

    first = [t.astype(BF16) for t in (mlp_w1[0], mlp_w2[0], pool_w, attn_w_qkv[0])]
    small = [c, norm_g.reshape(n_layers * 2, -1), pool_scale, gm_ln_g, gm_ln_b]
    w1_0g, w2_0g, pool_g, qkv_g, c_all, ng_g, ps_g, lng_g, lnb_g = _all_gather(first + small, "gather_first")
    c_all = c_all.reshape(N_DEV, d)
    later = _GatherAcrossChips([t.astype(BF16) for t in
                                (mlp_w1[1:], mlp_w2[1:], attn_w_o[0], gm_w_in[0], gm_w_out[0])])
    pool_wf = _natural_rows(pool_g)
    wqkv = _natural_cols(qkv_g)
    ng_full = _natural_cols(ng_g.reshape(N_DEV, n_layers * 2, 1, -1)).reshape(n_layers, 2, 1, d)
    ps_full = _natural_cols(ps_g.reshape(N_DEV, 2, 1, -1))
    lng_full = _natural_cols(lng_g.reshape(N_DEV, 1, -1))
    lnb_full = _natural_cols(lnb_g.reshape(N_DEV, 1, -1))

    c_ctx2 = c_ctx.reshape(1, d)
    ada_b_loc = lax.dynamic_slice_in_dim(ada_b, me * n_ada, n_ada, axis=1).reshape(n_layers, 1, n_ada)
    (mod_g,) = _all_gather([_mods_local(c_all, c_ctx2, ada_w, ada_b_loc)], "gather_mods")
    mod_full = jnp.moveaxis(mod_g, 0, 2).reshape(n_layers, 16, 6, d)
    mod_lat = lax.dynamic_index_in_dim(mod_full, me, axis=1, keepdims=False)
    mod_ctx = mod_full[:, 8]
    mods = jnp.stack([mod_ctx, mod_lat], axis=1)
    mods = jnp.concatenate([mods, jnp.zeros((n_layers, 2, 2, d), F32)], axis=2)

    bands = _pool_bands(tm)
    cos, sin = _rope_tables(c_len, l_len)
    ws_bf = gm_ws[0].astype(BF16)
    wst_bf = jnp.swapaxes(gm_ws[0], 1, 2).astype(BF16)
    bst = jnp.zeros((CHUNK, 128), F32).at[:, :GMLP_GROUPS].set(gm_bs[0].T)
    ng = lambda i, j: ng_full[i, j]

    h0 = jnp.concatenate([ctx[0], x[0]], axis=0)
    y0, h1 = _pool_fwd(h0, mods[0], ng(0, 0), pool_wf[0].astype(BF16), ps_full[0], bands, nct, tm, c_len, l_len)
    w1 = [_natural_cols(w1_0g)]
    w2 = [_natural_rows(w2_0g)]
    h2, p0, ym0 = _mlp_fwd(h1, mods[0], ng(0, 1), w1[0], w2[0], nct, tm)
    q, k, v1 = _qkv_fwd(h2, mods[1], ng(1, 0), wqkv, attn_q_g, attn_k_g, cos, sin, n_heads, n_kv, nct, tm)
    o, lse, later_g = _flash_fwd(q, k, v1, n_kv, 2 * tm, tm, rider=later)
    w1_g, w2_g, wo_g, gin_g, gout_g = _exchange_call(_ForwardToSibling(later_g), "forward_weights", in_place=True)
    w1_rest, w2_rest = _natural_cols(w1_g), _natural_rows(w2_g)
    w1 += [w1_rest[i] for i in range(n_layers - 1)]
    w2 += [w2_rest[i] for i in range(n_layers - 1)]
    wo = _natural_rows(wo_g)
    win = _natural_cols(gin_g)
    wout = _natural_rows(gout_g)
    y1, h3 = _wo_fwd(h2, o, wo, mods[1], nct, tm)
    h4, p1, ym1 = _mlp_fwd(h3, mods[1], ng(1, 1), w1[1], w2[1], 0, tm)
    y2, h5 = _gmlp_fwd(h4, mods[2], ng(2, 0), win, lng_full, lnb_full, ws_bf, bst, wout, CHUNK)
    h6, p2, ym2 = _mlp_fwd(h5, mods[2], ng(2, 1), w1[2], w2[2], 0, tm)
    y3, h7 = _pool_fwd(h6, mods[3], ng(3, 0), pool_wf[1].astype(BF16), ps_full[1], bands, 0, tm, c_len, l_len)
    h8, p3, ym3 = _mlp_fwd(h7, mods[3], ng(3, 1), w1[3], w2[3], 0, tm)
    dh, loss_part, dfinal = _head(h8, final_g.reshape(1, d), loss_target[0], tm)

    dw1, dw2, st_mlp = [None] * 4, [None] * 4, [None] * 4

    def mlp_back(i, dh, h_in, p, ym, nct_i):
        dh_in, m_bf, du, r_bf, dacc, st = _mlp_bwd(dh, h_in, p, ym, mods[i], ng(i, 1), w1[i], w2[i], nct_i, tm)
        dw1[i] = _tn_matmul(m_bf, du, "tn_w1")
        dw2[i] = _tn_matmul(r_bf, dacc, "tn_w2")
        st_mlp[i] = st
        return dh_in

    dh = mlp_back(3, dh, h7, p3, ym3, 0)
    dh, dpw1, st_pool3 = _pool_bwd(dh, h6, y3, mods[3], ng(3, 0), pool_wf[1].astype(BF16), ps_full[1], bands,
                                   0, tm, c_len, l_len, False)
    dh = mlp_back(2, dh, h5, p2, ym2, 0)
    dh, a_bf, dz, gated, dy, st_g, st_ln, dws, dbst = _gmlp_bwd(
        dh, h4, y2, mods[2], ng(2, 0), win, lng_full, lnb_full, ws_bf, wst_bf, bst, wout, CHUNK)
    dwin = _tn_matmul(a_bf, dz, "tn_gm_in")
    dwout = _tn_matmul(gated, dy, "tn_gm_out")
    dh = mlp_back(1, dh, h3, p1, ym1, 0)
    dy1, do, delta, st_wo = _wo_bwd(dh, y1, o, wo, mods[1], n_kv, tm)
    dwo = _tn_matmul(o, dy1, "tn_wo")
    grads_mid = _AllToAll([_shard_cols(jnp.stack(dw1[1:])), _shard_rows(jnp.stack(dw2[1:])),
                           _shard_rows(dpw1.astype(BF16)), _shard_rows(dwo), _shard_cols(dwin), _shard_rows(dwout)])
    dq, dk, dv, (g_w1_rest, g_w2_rest, g_pool1, g_wo, g_gin, g_gout) = _flash_bwd(
        q, k, v1, do, lse, delta, n_kv, 2 * tm, tm, rider=grads_mid)
    dh, a_bf, dqkv, st_q, dgains = _qkv_bwd(h2, dh, dq, dk, dv, mods[1], ng(1, 0), wqkv, attn_q_g, attn_k_g,
                                            cos, sin, n_heads, n_kv, nct, tm)
    dwqkv = _tn_matmul(a_bf, dqkv, "tn_qkv")
    dh = mlp_back(0, dh, h1, p0, ym0, nct)
    grad_x, dpw0, st_pool0 = _pool_bwd(dh, h0, y0, mods[0], ng(0, 0), pool_wf[0].astype(BF16), ps_full[0], bands,
                                       nct, tm, c_len, l_len, True)

    grads_last = _AllToAll([_shard_cols(dw1[0]), _shard_rows(dw2[0]), _shard_rows(dpw0.astype(BF16)),
                            _shard_cols(dwqkv)])
    g_w1_0, g_w2_0, g_pool0, g_qkv = _exchange_call(grads_last, "exchange_grads")
    g_w1 = jnp.concatenate([g_w1_0[:, None], g_w1_rest], axis=1)
    g_w2 = jnp.concatenate([g_w2_0[:, None], g_w2_rest], axis=1)
    g_pool = jnp.stack([g_pool0, g_pool1], axis=1)

    mix_lat = [st_pool0[-1], st_q[1] + st_wo, st_g, st_pool3[-1]]
    mlp_lat = [st[-1] for st in st_mlp]
    dmod_lat = jnp.stack([jnp.concatenate([mix_lat[i][0:3], mlp_lat[i][0:3]]) for i in range(n_layers)])
    dmod_ctx = jnp.stack([jnp.concatenate([st_pool0[0][0:3], st_mlp[0][0][0:3]]),
                          jnp.concatenate([st_q[0][0:2], jnp.zeros((4, d), F32)]),
                          jnp.zeros((6, d), F32), jnp.zeros((6, d), F32)])
    dng_part = jnp.stack([jnp.stack([mix_lat[0][3] + st_pool0[0][3], mlp_lat[0][3] + st_mlp[0][0][3]]),
                          jnp.stack([mix_lat[1][3] + st_q[0][3], mlp_lat[1][3]]),
                          jnp.stack([mix_lat[2][3], mlp_lat[2][3]]),
                          jnp.stack([mix_lat[3][3], mlp_lat[3][3]])])
    dps_part = jnp.stack([mix_lat[0][4] + st_pool0[0][4], mix_lat[3][4]])
    small_parts = [dmod_lat.reshape(n_layers * 6, d), dmod_ctx.reshape(n_layers * 6, d),
                   dng_part.reshape(n_layers * 2, d), dps_part, st_ln, dgains, dws.reshape(-1, CHUNK),
                   dbst, dfinal]
    (gm_lat, gm_ctx, g_ng, g_ps, g_ln, g_gains, g_ws, g_bst, g_final) = _all_gather(small_parts, "gather_small_grads")

    gm_lat4 = gm_lat.reshape(N_DEV, n_layers, 6 * d)
    gm_ctx4 = gm_ctx.reshape(N_DEV, n_layers, 6 * d)
    dm_lat_loc = jnp.moveaxis(_my_cols(gm_lat4, me, n_ada), 0, 1)
    dm_ctx_loc = jnp.moveaxis(_my_cols(gm_ctx4, me, n_ada), 0, 1)
    g_ada_w, ds_part = _ada_grads(c_all, c_ctx2, ada_w, dm_lat_loc, dm_ctx_loc)
    (ds_all,) = _all_gather([ds_part], "gather_dsctx")
    g_c_ctx = _cctx_grad(ds_all, c_ctx2).reshape(d)

    n_ng = norm_g.shape[-1]
    n_ps = pool_scale.shape[-1]
    n_ln = gm_ln_g.shape[-1]
    gparts = {
        "c_ctx": g_c_ctx[None],
        "ada_w": g_ada_w[None],
        "ada_b": jnp.concatenate([gm_lat4, gm_ctx4], axis=0),
        "norm_g": _my_cols(g_ng.reshape(N_DEV, n_layers, 2, d), me, n_ng),
        "mlp_w1": g_w1, "mlp_w2": g_w2, "pool_w": g_pool,
        "pool_scale": _my_cols(g_ps, me, n_ps),
        "attn_w_qkv": g_qkv[:, None], "attn_w_o": g_wo[:, None],
        "attn_q_g": g_gains[:, 0:1], "attn_k_g": g_gains[:, 1:2],
        "gm_w_in": g_gin[:, None],
        "gm_ln_g": _my_cols(g_ln[:, 0:1], me, n_ln), "gm_ln_b": _my_cols(g_ln[:, 1:2], me, n_ln),
        "gm_ws": g_ws.reshape((N_DEV,) + gm_ws.shape),
        "gm_bs": jnp.swapaxes(g_bst[:, :, :GMLP_GROUPS], 1, 2)[:, None],
        "gm_w_out": g_gout[:, None],
        "final_g": g_final[:, 0],
    }
    weights = dict(c_ctx=(c_ctx, m_c_ctx, v_c_ctx), ada_w=(ada_w, m_ada_w, v_ada_w), ada_b=(ada_b, m_ada_b, v_ada_b),
                   norm_g=(norm_g, m_norm_g, v_norm_g), mlp_w1=(mlp_w1, m_mlp_w1, v_mlp_w1),
                   mlp_w2=(mlp_w2, m_mlp_w2, v_mlp_w2), pool_w=(pool_w, m_pool_w, v_pool_w),
                   pool_scale=(pool_scale, m_pool_scale, v_pool_scale),
                   attn_w_qkv=(attn_w_qkv, m_attn_w_qkv, v_attn_w_qkv), attn_w_o=(attn_w_o, m_attn_w_o, v_attn_w_o),
                   attn_q_g=(attn_q_g, m_attn_q_g, v_attn_q_g), attn_k_g=(attn_k_g, m_attn_k_g, v_attn_k_g),
                   gm_w_in=(gm_w_in, m_gm_w_in, v_gm_w_in), gm_ln_g=(gm_ln_g, m_gm_ln_g, v_gm_ln_g),
                   gm_ln_b=(gm_ln_b, m_gm_ln_b, v_gm_ln_b), gm_ws=(gm_ws, m_gm_ws, v_gm_ws),
                   gm_bs=(gm_bs, m_gm_bs, v_gm_bs), gm_w_out=(gm_w_out, m_gm_w_out, v_gm_w_out),
                   final_g=(final_g, m_final_g, v_final_g))
    grads, deltas, new_m, new_v = [], [], [], []
    for wname, (w_, m_, v_) in weights.items():
        g_, d_, nm_, nv_ = _adamw(w_, gparts[wname], m_, v_, "adamw_" + wname)
        grads.append(g_)
        deltas.append(d_)
        new_m.append(nm_)
        new_v.append(nv_)

    loss = lax.psum(loss_part[0, 0], ("x", "y", "c"))
    return (loss, grad_x[None], *grads, *deltas, *new_m, *new_v)
```

```python
import functools
import math

import numpy as np
import jax
import jax.numpy as jnp
from jax import lax
from jax.experimental import pallas as pl
from jax.experimental.pallas import tpu as pltpu

F32 = jnp.float32
BF16 = jnp.bfloat16
MESH_ID = pl.DeviceIdType.MESH

N_DEV = 8
EPS = 1e-6
HEAD_DIM = 128
GRID_W = 64
ROPE_BASE = 10000.0
CHUNK = 128
POOL_WINDOWS = (2, 4, 8, 16)
POOL_GROUPS = 4
POOL_HALO = 8
GMLP_GROUPS = 8
ADAM_LR, ADAM_B1, ADAM_B2, ADAM_EPS, ADAM_WD, ADAM_STEP = 0.001, 0.9, 0.999, 1e-08, 0.01, 10

V7X_VMEM_BYTES = 64 << 20
VMEM_LIMIT_BIG = V7X_VMEM_BYTES - (8 << 20)
FLASH_TK_CAP = 768
LOG2E = math.log2(math.e)
Q_SCALE = HEAD_DIM ** -0.5 * LOG2E
MLP_FC = 512


def _params(sem, vmem=VMEM_LIMIT_BIG):
    return pltpu.CompilerParams(dimension_semantics=sem, vmem_limit_bytes=vmem)


def _const_spec(shape):
    nd = len(shape)
    return pl.BlockSpec(shape, lambda *_: (0,) * nd, pipeline_mode=pl.Buffered(1))


def _dot(a, b):
    return jnp.dot(a, b, preferred_element_type=F32)


def _dot_nt(a, b):
    return lax.dot_general(a, b, (((1,), (1,)), ((), ())), preferred_element_type=F32)


def _dot_tn(a, b):
    return lax.dot_general(a, b, (((0,), (0,)), ((), ())), preferred_element_type=F32)


def _colsum(x):
    return jnp.sum(x, axis=0, keepdims=True)


def _sid(i, nct):
    if nct == 0:
        return 1
    return jnp.where(i >= nct, 1, 0)


def _n_streams(nct):
    return 2 if nct else 1


def _stat_sid(i, nct):
    return _sid(i, nct) if nct else 0


def _first_of_stream(i, nct):
    if nct == 0:
        return i == 0
    return jnp.logical_or(i == 0, i == nct)


def _normmod(h, ng, sh, sc):
    rstd = lax.rsqrt(jnp.mean(h * h, axis=-1, keepdims=True) + EPS)
    xhat = h * rstd
    n = xhat * ng
    return n * (1.0 + sc) + sh, xhat, rstd, n


def _normmod_bwd(da, xhat, rstd, n, ng, sc):
    dsh = _colsum(da)
    dsc = _colsum(da * n)
    dn = da * (1.0 + sc)
    dng = _colsum(dn * xhat)
    dxhat = dn * ng
    dh = rstd * (dxhat - xhat * jnp.mean(dxhat * xhat, axis=-1, keepdims=True))
    return dh, dsh, dsc, dng


def _acc_rows(ref, first, rows):
    @pl.when(first)
    def _():
        ref[...] = jnp.zeros_like(ref)
    for r, val in enumerate(rows):
        ref[r:r + 1, :] = ref[r:r + 1, :] + val


_GELU_C = math.sqrt(2.0 / math.pi)


def _gelu(x):
    t = jnp.tanh((_GELU_C * x) * (1.0 + 0.044715 * (x * x)))
    hx = 0.5 * x
    return hx + hx * t


def _gelu_and_grad(x):
    x2 = x * x
    t = jnp.tanh((_GELU_C * x) * (1.0 + 0.044715 * x2))
    hx = 0.5 * x
    g = hx + hx * t
    dg = (0.5 + 0.5 * t) + (hx * (1.0 - t * t)) * (_GELU_C + (3.0 * 0.044715 * _GELU_C) * x2)
    return g, dg


def _coords():
    return lax.axis_index("x"), lax.axis_index("y"), lax.axis_index("c")


def _dev_index(px, py, pc):
    return 4 * px + 2 * py + pc


def _all_gather(xs, name, extra=None):
    n = len(xs)
    e_in, e_in_specs, e_out, e_out_specs, e_scratch = _rider_parts(extra)
    ne = len(e_in)

    def body(*refs):
        x_refs, e_x = refs[:n], refs[n:n + ne]
        o_refs, e_o = refs[n + ne:2 * n + ne], refs[2 * n + ne:2 * n + 2 * ne]
        send_sems, recv_sems, local_sems = refs[2 * n + 2 * ne:2 * n + 2 * ne + 3]
        e_sems = refs[2 * n + 2 * ne + 3:]
        if extra is not None:
            extra.start(e_x, e_o, e_sems)
        x, y, c = _coords()
        me, sibling = (x, y, c), (x, y, 1 - c)
        chips = [(1 - x, y), (x, 1 - y), (1 - x, 1 - y)]

        def copy(a, k, block, to, src=None):
            dst = o_refs[a].at[_dev_index(*block)]
            return pltpu.make_async_remote_copy(
                src_ref=dst if src is None else src, dst_ref=dst,
                send_sem=send_sems.at[7 * a + k], recv_sem=recv_sems.at[7 * a + k],
                device_id=to, device_id_type=MESH_ID)

        mine = [pltpu.make_async_copy(x_refs[a], o_refs[a].at[_dev_index(*me)], local_sems.at[a])
                for a in range(n)]
        for cp in mine:
            cp.start()
        first = []
        for a in range(n):
            first.append(copy(a, 0, me, sibling, src=x_refs[a]))
            first += [copy(a, 1 + j, me, (*chip, c), src=x_refs[a]) for j, chip in enumerate(chips)]
        for cp in first:
            cp.start()
        passed = []
        for a in range(n):
            for j, chip in enumerate(chips):
                copy(a, 1 + j, (*chip, c), me).wait_recv()
                fwd = copy(a, 4 + j, (*chip, c), sibling)
                fwd.start()
                passed.append(fwd)
        for a in range(n):
            copy(a, 0, sibling, me).wait_recv()
            for j, chip in enumerate(chips):
                copy(a, 4 + j, (*chip, 1 - c), me).wait_recv()
        for cp in first + passed:
            cp.wait_send()
        for cp in mine:
            cp.wait()
        if extra is not None:
            extra.finish(e_x, e_o, e_sems)

    any_spec = pl.BlockSpec(memory_space=pl.ANY)
    outs = pl.pallas_call(
        body, name=name,
        out_shape=[jax.ShapeDtypeStruct((N_DEV,) + x.shape, x.dtype) for x in xs] + e_out,
        in_specs=[any_spec] * n + e_in_specs, out_specs=[any_spec] * n + e_out_specs,
        scratch_shapes=[pltpu.SemaphoreType.DMA((7 * n,)), pltpu.SemaphoreType.DMA((7 * n,)),
                        pltpu.SemaphoreType.DMA((n,))] + e_scratch,
    )(*xs, *e_in)
    return list(outs)


class _Exchange:
    per_array = 0
    has_local = True

    def __init__(self, xs):
        self.xs = list(xs)
        n = len(self.xs)
        self.out_shapes = self._out_shapes()
        self.scratch = [pltpu.SemaphoreType.DMA((self.per_array * n,)),
                        pltpu.SemaphoreType.DMA((self.per_array * n,)),
                        pltpu.SemaphoreType.DMA((n,))]

    def _out_shapes(self):
        raise NotImplementedError

    def _copies(self, x_refs, o_refs, sems):
        raise NotImplementedError

    def start(self, x_refs, o_refs, sems):
        mine, sends, _ = self._copies(x_refs, o_refs, sems)
        for cp in mine + sends:
            cp.start()

    def finish(self, x_refs, o_refs, sems):
        mine, sends, arrivals = self._copies(x_refs, o_refs, sems)
        for make in arrivals:
            make().wait_recv()
        for cp in sends:
            cp.wait_send()
        for cp in mine:
            cp.wait()


def _remote(src, dst, sems, k, to):
    return pltpu.make_async_remote_copy(src_ref=src, dst_ref=dst, send_sem=sems[0].at[k], recv_sem=sems[1].at[k],
                                        device_id=to, device_id_type=MESH_ID)


class _GatherAcrossChips(_Exchange):
    per_array = 4

    def _out_shapes(self):
        return [jax.ShapeDtypeStruct((N_DEV,) + x.shape, x.dtype) for x in self.xs]

    def _copies(self, x_refs, o_refs, sems):
        x, y, c = _coords()
        targets = [(x, y, 1 - c), (1 - x, y, c), (x, 1 - y, c), (1 - x, 1 - y, c)]
        mine, sends, arrivals = [], [], []
        for a, (x_ref, o_ref) in enumerate(zip(x_refs, o_refs)):
            own = o_ref.at[_dev_index(x, y, c)]
            mine.append(pltpu.make_async_copy(x_ref, own, sems[2].at[a]))
            for k, to in enumerate(targets):
                sends.append(_remote(x_ref, own, sems, 4 * a + k, to))
                arrivals.append(functools.partial(_remote, x_ref, o_ref.at[_dev_index(*to)], sems, 4 * a + k, to))
        return mine, sends, arrivals


class _ForwardToSibling(_Exchange):
    per_array = 3

    def _out_shapes(self):
        return [jax.ShapeDtypeStruct(x.shape, x.dtype) for x in self.xs]

    def _copies(self, x_refs, o_refs, sems):
        x, y, c = _coords()
        chips = [(1 - x, y), (x, 1 - y), (1 - x, 1 - y)]
        sends, arrivals = [], []
        for a, (x_ref, o_ref) in enumerate(zip(x_refs, o_refs)):
            for j, chip in enumerate(chips):
                held = _dev_index(*chip, c)
                sends.append(_remote(x_ref.at[held], o_ref.at[held], sems, 3 * a + j, (x, y, 1 - c)))
                theirs = _dev_index(*chip, 1 - c)
                arrivals.append(functools.partial(_remote, x_ref.at[theirs], o_ref.at[theirs], sems, 3 * a + j,
                                                  (x, y, 1 - c)))
        return [], sends, arrivals


class _AllToAll(_Exchange):
    per_array = 7

    def _out_shapes(self):
        return [jax.ShapeDtypeStruct(x.shape, x.dtype) for x in self.xs]

    def _copies(self, x_refs, o_refs, sems):
        x, y, c = _coords()
        me_i = _dev_index(x, y, c)
        mine, sends, arrivals = [], [], []
        for a, (x_ref, o_ref) in enumerate(zip(x_refs, o_refs)):
            mine.append(pltpu.make_async_copy(x_ref.at[me_i], o_ref.at[me_i], sems[2].at[a]))
            for r in range(1, 8):
                to = (1 - x if r & 4 else x, 1 - y if r & 2 else y, 1 - c if r & 1 else c)
                to_i = _dev_index(*to)
                sends.append(_remote(x_ref.at[to_i], o_ref.at[me_i], sems, 7 * a + r - 1, to))
                arrivals.append(functools.partial(_remote, x_ref.at[to_i], o_ref.at[to_i], sems, 7 * a + r - 1, to))
        return mine, sends, arrivals


def _exchange_call(ex, name, in_place=False):
    n = len(ex.xs)

    def body(*refs):
        x_refs, o_refs, sems = refs[:n], refs[n:2 * n], refs[2 * n:]
        ex.start(x_refs, o_refs, sems)
        ex.finish(x_refs, o_refs, sems)

    any_spec = pl.BlockSpec(memory_space=pl.ANY)
    outs = pl.pallas_call(
        body, name=name, out_shape=ex.out_shapes, in_specs=[any_spec] * n, out_specs=[any_spec] * n,
        scratch_shapes=ex.scratch, input_output_aliases={a: a for a in range(n)} if in_place else {},
    )(*ex.xs)
    return list(outs)


def _rider_parts(rider):
    if rider is None:
        return [], [], [], [], []
    any_spec = pl.BlockSpec(memory_space=pl.ANY)
    n = len(rider.xs)
    return rider.xs, [any_spec] * n, rider.out_shapes, [any_spec] * n, rider.scratch


def _compute_call(body, *, name, grid, in_specs, out_specs, out_shape, operands, scratch_shapes=(), rider=None):
    in_specs, out_specs, out_shape, scratch_shapes = list(in_specs), list(out_specs), list(out_shape), list(scratch_shapes)
    r_in, r_in_specs, r_out, r_out_specs, r_scratch = _rider_parts(rider)
    n_in, n_out, n_scr, nr = len(operands), len(out_shape), len(scratch_shapes), len(r_in)

    def riding_body(*refs):
        ins, refs = refs[:n_in], refs[n_in:]
        r_x, refs = refs[:nr], refs[nr:]
        outs, refs = refs[:n_out], refs[n_out:]
        r_o, refs = refs[:nr], refs[nr:]
        scratch, r_sems = refs[:n_scr], refs[n_scr:]
        if rider is not None:
            first, last = _grid_ends(grid)
            pl.when(first)(lambda: rider.start(r_x, r_o, r_sems))
        body(*ins, *outs, *scratch)
        if rider is not None:
            pl.when(last)(lambda: rider.finish(r_x, r_o, r_sems))

    res = pl.pallas_call(
        riding_body, name=name, grid=grid, out_shape=out_shape + r_out,
        in_specs=in_specs + r_in_specs, out_specs=out_specs + r_out_specs,
        scratch_shapes=scratch_shapes + r_scratch,
        compiler_params=_params(("arbitrary",) * len(grid)),
    )(*operands, *r_in)
    return list(res[:n_out]), list(res[n_out:])


def _grid_ends(grid):
    first = pl.program_id(0) == 0
    last = pl.program_id(0) == grid[0] - 1
    for ax in range(1, len(grid)):
        first = jnp.logical_and(first, pl.program_id(ax) == 0)
        last = jnp.logical_and(last, pl.program_id(ax) == grid[ax] - 1)
    return first, last


def _silu(x):
    return x * (1.0 / (1.0 + jnp.exp(-x)))


def _cond_rows(c_all, c_ctx):
    d = c_all.shape[-1]
    s = jnp.concatenate([c_all, jnp.zeros((8, d), F32)], axis=0)
    row = lax.broadcasted_iota(jnp.int32, (16, d), 0)
    s = jnp.where(row == 8, c_ctx, s)
    return jnp.where(row <= 8, _silu(s), 0.0)


def _mods_local(c_all, c_ctx, ada_w, ada_b_loc):
    nl, d, n = ada_w.shape

    def body(c_ref, cc_ref, w_ref, b_ref, o_ref):
        s = _cond_rows(c_ref[...], cc_ref[...])
        o_ref[...] = jnp.dot(s, w_ref[...], preferred_element_type=F32,
                             precision=lax.Precision.HIGHEST) + b_ref[...]

    return pl.pallas_call(
        body, name="mods_local", grid=(nl,),
        out_shape=jax.ShapeDtypeStruct((nl, 16, n), F32),
        in_specs=[pl.BlockSpec((8, d), lambda i: (0, 0)), pl.BlockSpec((1, d), lambda i: (0, 0)),
                  pl.BlockSpec((None, d, n), lambda i: (i, 0, 0)),
                  pl.BlockSpec((None, 1, n), lambda i: (i, 0, 0))],
        out_specs=pl.BlockSpec((None, 16, n), lambda i: (i, 0, 0)),
        compiler_params=_params(("arbitrary",)),
    )(c_all, c_ctx, ada_w, ada_b_loc)


def _ada_grads(c_all, c_ctx, ada_w, dm_lat, dm_ctx):
    nl, d, n = ada_w.shape

    def body(c_ref, cc_ref, w_ref, dml_ref, dmc_ref, gw_ref, ds_ref):
        i = pl.program_id(0)
        s = _cond_rows(c_ref[...], cc_ref[...])
        csum = dmc_ref[0:1, :]
        for k in range(1, N_DEV):
            csum = csum + dmc_ref[k:k + 1, :]
        row = lax.broadcasted_iota(jnp.int32, (8, n), 0)
        dm_c = jnp.where(row == 0, csum, 0.0)
        dm = jnp.concatenate([dml_ref[...], dm_c], axis=0)
        gw_ref[...] = lax.dot_general(s, dm, (((0,), (0,)), ((), ())), preferred_element_type=F32,
                                      precision=lax.Precision.HIGHEST)
        ds = lax.dot_general(dm_c, w_ref[...], (((1,), (1,)), ((), ())),
                             preferred_element_type=F32, precision=lax.Precision.HIGHEST)

        @pl.when(i == 0)
        def _():
            ds_ref[...] = jnp.zeros_like(ds_ref)
        ds_ref[...] += ds

    return pl.pallas_call(
        body, name="ada_grads", grid=(nl,),
        out_shape=[jax.ShapeDtypeStruct((nl, d, n), F32), jax.ShapeDtypeStruct((8, d), F32)],
        in_specs=[pl.BlockSpec((8, d), lambda i: (0, 0)), pl.BlockSpec((1, d), lambda i: (0, 0)),
                  pl.BlockSpec((None, d, n), lambda i: (i, 0, 0)),
                  pl.BlockSpec((None, 8, n), lambda i: (i, 0, 0)),
                  pl.BlockSpec((None, 8, n), lambda i: (i, 0, 0))],
        out_specs=[pl.BlockSpec((None, d, n), lambda i: (i, 0, 0)),
                   pl.BlockSpec((8, d), lambda i: (0, 0))],
        compiler_params=_params(("arbitrary",)),
    )(c_all, c_ctx, ada_w, dm_lat, dm_ctx)


def _cctx_grad(ds_parts, c_ctx):
    d = c_ctx.shape[-1]

    def body(p_ref, c_ref, o_ref):
        ds = p_ref[0]
        for k in range(1, N_DEV):
            ds = ds + p_ref[k]
        x = c_ref[...]
        sg = 1.0 / (1.0 + jnp.exp(-x))
        o_ref[...] = ds[0:1, :] * (sg * (1.0 + x * (1.0 - sg)))

    return pl.pallas_call(body, name="cctx_grad", out_shape=jax.ShapeDtypeStruct((1, d), F32))(ds_parts, c_ctx)


def _mlp_fwd(h1, mods, ng, w1, w2, nct, tm, row_off=0):
    d = h1.shape[1]
    r = h1.shape[0] - row_off * tm
    f = w1.shape[1]
    fc = min(MLP_FC, f)

    def body(h_ref, mod_ref, ng_ref, w1_ref, w2_ref, h2_ref, p_ref, y_ref):
        h = h_ref[...]
        a, _, _, _ = _normmod(h, ng_ref[...], mod_ref[3:4, :], mod_ref[4:5, :])
        ab = a.astype(BF16)
        acc = jnp.zeros((tm, d), F32)
        for j in range(f // fc):
            sl = slice(j * fc, (j + 1) * fc)
            p = jnp.maximum(_dot(ab, w1_ref[:, sl]), 0.0)
            p_ref[:, sl] = p.astype(BF16)
            acc = acc + _dot((p * p).astype(BF16), w2_ref[sl, :])
        y_ref[...] = acc.astype(BF16)
        h2_ref[...] = h + mod_ref[5:6, :] * acc

    return pl.pallas_call(
        body, name="mlp_fwd", grid=(r // tm,),
        out_shape=[jax.ShapeDtypeStruct((r, d), F32), jax.ShapeDtypeStruct((r, f), BF16),
                   jax.ShapeDtypeStruct((r, d), BF16)],
        in_specs=[pl.BlockSpec((tm, d), lambda i: (i + row_off, 0)),
                  pl.BlockSpec((None, 8, d), lambda i: (_sid(i, nct), 0, 0)),
                  _const_spec((1, d)), _const_spec(w1.shape), _const_spec(w2.shape)],
        out_specs=[pl.BlockSpec((tm, d), lambda i: (i, 0)), pl.BlockSpec((tm, f), lambda i: (i, 0)),
                   pl.BlockSpec((tm, d), lambda i: (i, 0))],
        compiler_params=_params(("arbitrary",)),
    )(h1, mods, ng, w1, w2)


def _mlp_bwd(dh2, h1, p, y, mods, ng, w1, w2, nct, tm, row_off=0, rider=None):
    r_rows, d = dh2.shape
    f = w1.shape[1]
    fc = min(MLP_FC, f)

    def body(dh_ref, h_ref, p_ref, y_ref, mod_ref, ng_ref, w1_ref, w2_ref,
             dh1_ref, m_ref, du_ref, dacc_ref, st_ref):
        i = pl.program_id(0)
        dh = dh_ref[...]
        ngv, sc, gate = ng_ref[...], mod_ref[4:5, :], mod_ref[5:6, :]
        a, xhat, rstd, n = _normmod(h_ref[...], ngv, mod_ref[3:4, :], sc)
        m_ref[...] = a.astype(BF16)
        dgate = _colsum(dh * y_ref[...].astype(F32))
        dacc = (gate * dh).astype(BF16)
        dacc_ref[...] = dacc
        dm = jnp.zeros((tm, d), F32)
        for j in range(f // fc):
            sl = slice(j * fc, (j + 1) * fc)
            pj = p_ref[:, sl].astype(F32)
            du = (_dot_nt(dacc, w2_ref[sl, :]) * (2.0 * pj)).astype(BF16)
            du_ref[:, sl] = du
            dm = dm + _dot_nt(du, w1_ref[:, sl])
        dhn, dsh, dsc, dng = _normmod_bwd(dm, xhat, rstd, n, ngv, sc)
        dh1_ref[...] = dh + dhn
        _acc_rows(st_ref, _first_of_stream(i, nct), [dsh, dsc, dgate, dng])

    outs, rode = _compute_call(
        body, name="mlp_bwd", grid=(r_rows // tm,), operands=(dh2, h1, p, y, mods, ng, w1, w2), rider=rider,
        out_shape=[jax.ShapeDtypeStruct((r_rows, d), F32), jax.ShapeDtypeStruct((r_rows, d), BF16),
                   jax.ShapeDtypeStruct((r_rows, f), BF16),
                   jax.ShapeDtypeStruct((r_rows, d), BF16), jax.ShapeDtypeStruct((_n_streams(nct), 8, d), F32)],
        in_specs=[pl.BlockSpec((tm, d), lambda i: (i, 0)),
                  pl.BlockSpec((tm, d), lambda i: (i + row_off, 0)),
                  pl.BlockSpec((tm, f), lambda i: (i, 0)), pl.BlockSpec((tm, d), lambda i: (i, 0)),
                  pl.BlockSpec((None, 8, d), lambda i: (_sid(i, nct), 0, 0)),
                  _const_spec((1, d)), _const_spec(w1.shape), _const_spec(w2.shape)],
        out_specs=[pl.BlockSpec((tm, d), lambda i: (i, 0)), pl.BlockSpec((tm, d), lambda i: (i, 0)),
                   pl.BlockSpec((tm, f), lambda i: (i, 0)),
                   pl.BlockSpec((tm, d), lambda i: (i, 0)),
                   pl.BlockSpec((None, 8, d), lambda i: (_stat_sid(i, nct), 0, 0))])
    return (*outs, rode)


def _pick(n, cands):
    for cand in cands:
        if n % cand == 0:
            return cand
    return n


def _tn_matmul(x, y, name, groups=0, square_x=False, rider=None):
    rows, k1 = x.shape
    k2 = y.shape[1]
    bt = _pick(rows, (1024, 768, 512, 384, 256, 128))
    if groups:
        bk1, bk2 = k1 // groups, k2 // groups
        grid = (groups, 1, rows // bt)
        x_map = lambda g, j, t: (t, g)
        y_map = lambda g, j, t: (t, g)
        out_shape = jax.ShapeDtypeStruct((groups, bk1, bk2), BF16)
        out_spec = pl.BlockSpec((None, bk1, bk2), lambda g, j, t: (g, 0, 0))
    else:
        bk1, bk2 = min(k1, 1024), min(k2, 1024)
        grid = (k1 // bk1, k2 // bk2, rows // bt)
        x_map = lambda i, j, t: (t, i)
        y_map = lambda i, j, t: (t, j)
        out_shape = jax.ShapeDtypeStruct((k1, k2), BF16)
        out_spec = pl.BlockSpec((bk1, bk2), lambda i, j, t: (i, j))
    nt = rows // bt

    def body(x_ref, y_ref, o_ref, acc_ref):
        t = pl.program_id(2)

        @pl.when(t == 0)
        def _():
            acc_ref[...] = jnp.zeros_like(acc_ref)
        xv = x_ref[...]
        acc_ref[...] += _dot_tn(xv * xv if square_x else xv, y_ref[...])

        @pl.when(t == nt - 1)
        def _():
            o_ref[...] = acc_ref[...].astype(BF16)

    (out,), rode = _compute_call(
        body, name=name, grid=grid, operands=(x, y), rider=rider, out_shape=[out_shape],
        in_specs=[pl.BlockSpec((bt, bk1), x_map), pl.BlockSpec((bt, bk2), y_map)],
        out_specs=[out_spec], scratch_shapes=[pltpu.VMEM((bk1, bk2), F32)])
    return out if rider is None else (out, rode)


def _pool_bands(tm):
    k = tm + 128
    t = np.arange(tm)[:, None]
    e = np.arange(k)[None, :]
    fwd, bwd = [], []
    for w in POOL_WINDOWS:
        lo = POOL_HALO + t - w // 2
        fwd.append(((e >= lo) & (e <= lo + w - 1)).astype(np.float32))
        lo_t = POOL_HALO + t - w // 2 + 1
        bwd.append(((e >= lo_t) & (e <= lo_t + w - 1)).astype(np.float32))
    return jnp.asarray(np.stack(fwd), BF16), jnp.asarray(np.stack(bwd), BF16)


def _pool_geometry(i, nct, n_tiles, tm, c_len, l_len):
    if nct == 0:
        pos0 = i * tm
        ls = l_len
        has_prev = i > 0
        has_next = i < n_tiles - 1
    else:
        in_ctx = i < nct
        pos0 = jnp.where(in_ctx, i, i - nct) * tm
        ls = jnp.where(in_ctx, c_len, l_len)
        has_prev = jnp.logical_and(i != 0, i != nct)
        has_next = jnp.logical_and(i != nct - 1, i != n_tiles - 1)
    return pos0, ls, has_prev, has_next


def _window_inv_counts(pos, ls):
    out = []
    for w in POOL_WINDOWS:
        lo = jnp.maximum(pos - w // 2, 0)
        hi = jnp.minimum(pos + w - w // 2, ls)
        cnt = jnp.maximum(hi - lo, 1).astype(F32)
        out.append(1.0 / cnt)
    return out


def _split_bf16(x):
    hi = x.astype(BF16)
    return hi, (x - hi.astype(F32)).astype(BF16)


def _extend(prev, tile, nxt, has_prev, has_next):
    w = tile.shape[1]
    prev = jnp.where(has_prev, prev, 0.0)
    nxt = jnp.where(has_next, nxt, 0.0)
    return jnp.concatenate([prev, tile, nxt, jnp.zeros((128 - 2 * POOL_HALO, w), F32)], axis=0)


def _pool_specs(tm, d, n_rows):
    last8 = n_rows // POOL_HALO - 1
    per = tm // POOL_HALO
    return [pl.BlockSpec((tm, d), lambda i: (i, 0)),
            pl.BlockSpec((POOL_HALO, d), lambda i: (jnp.maximum(i * per - 1, 0), 0)),
            pl.BlockSpec((POOL_HALO, d), lambda i: (jnp.minimum((i + 1) * per, last8), 0))]


def _pool_fwd(h, mods, ng, w, scale, bands, nct, tm, c_len, l_len):
    r, d = h.shape
    gw = d // POOL_GROUPS
    n_tiles = r // tm
    kx = tm + 128

    def body(h_ref, hp_ref, hn_ref, mod_ref, ng_ref, w_ref, sc_ref, band_ref, y_ref, h1_ref):
        i = pl.program_id(0)
        pos0, ls, has_prev, has_next = _pool_geometry(i, nct, n_tiles, tm, c_len, l_len)
        ngv, sh, sc = ng_ref[...], mod_ref[0:1, :], mod_ref[1:2, :]
        h = h_ref[...]
        a = _normmod(h, ngv, sh, sc)[0]
        a_ext = _extend(_normmod(hp_ref[...], ngv, sh, sc)[0], a, _normmod(hn_ref[...], ngv, sh, sc)[0],
                        has_prev, has_next)
        pos = pos0 + lax.broadcasted_iota(jnp.int32, (tm, 1), 0)
        inv = _window_inv_counts(pos, ls)
        ys = []
        for g in range(POOL_GROUPS):
            cols = slice(g * gw, (g + 1) * gw)
            hi, lo = _split_bf16(a_ext[:, cols])
            s = _dot(band_ref[g], hi) + _dot(band_ref[g], lo)
            pg = s * inv[g] - a[:, cols]
            ys.append(_dot(pg.astype(BF16), w_ref[g]))
        y = jnp.concatenate(ys, axis=1) * sc_ref[...]
        y_ref[...] = y.astype(BF16)
        h1_ref[...] = h + mod_ref[2:3, :] * y

    return pl.pallas_call(
        body, name="pool_fwd", grid=(n_tiles,),
        out_shape=[jax.ShapeDtypeStruct((r, d), BF16), jax.ShapeDtypeStruct((r, d), F32)],
        in_specs=_pool_specs(tm, d, r) + [
            pl.BlockSpec((None, 8, d), lambda i: (_sid(i, nct), 0, 0)),
            _const_spec((1, d)), _const_spec(w.shape), _const_spec((1, d)), _const_spec((4, tm, kx))],
        out_specs=[pl.BlockSpec((tm, d), lambda i: (i, 0)), pl.BlockSpec((tm, d), lambda i: (i, 0))],
        compiler_params=_params(("arbitrary",)),
    )(h, h, h, mods, ng, w, scale, bands[0])


def _pool_bwd(dh1, h, y, mods, ng, w, scale, bands, nct, tm, c_len, l_len, latent_out, rider=None):
    r, d = h.shape
    gw = d // POOL_GROUPS
    n_tiles = r // tm
    kx = tm + 128
    out_rows = l_len if latent_out else r
    out_off = nct if latent_out else 0

    def body(dh_ref, dhp_ref, dhn_ref, h_ref, hp_ref, hn_ref, y_ref, mod_ref, ng_ref, w_ref, sc_ref,
             bf_ref, bb_ref, dho_ref, dw_ref, st_ref):
        i = pl.program_id(0)
        pos0, ls, has_prev, has_next = _pool_geometry(i, nct, n_tiles, tm, c_len, l_len)
        ngv, sh, sc, gate = ng_ref[...], mod_ref[0:1, :], mod_ref[1:2, :], mod_ref[2:3, :]
        scale_v = sc_ref[...]
        h = h_ref[...]
        a, xhat, rstd, n = _normmod(h, ngv, sh, sc)
        a_ext = _extend(_normmod(hp_ref[...], ngv, sh, sc)[0], a, _normmod(hn_ref[...], ngv, sh, sc)[0],
                        has_prev, has_next)
        dh = dh_ref[...]
        dgate = _colsum(dh * y_ref[...].astype(F32))
        dy = gate * dh
        dy_ext = _extend(gate * dhp_ref[...], dy, gate * dhn_ref[...], has_prev, has_next)
        dyp_ext = (dy_ext * scale_v).astype(BF16)
        dyp = (dy * scale_v).astype(BF16)
        pos = pos0 + lax.broadcasted_iota(jnp.int32, (tm, 1), 0)
        inv = _window_inv_counts(pos, ls)
        pos_e = pos0 - POOL_HALO + lax.broadcasted_iota(jnp.int32, (kx, 1), 0)
        inv_e = _window_inv_counts(pos_e, ls)

        @pl.when(i == 0)
        def _():
            dw_ref[...] = jnp.zeros_like(dw_ref)
        das, dscale = [], []
        for g in range(POOL_GROUPS):
            cols = slice(g * gw, (g + 1) * gw)
            hi, lo = _split_bf16(a_ext[:, cols])
            pg = ((_dot(bf_ref[g], hi) + _dot(bf_ref[g], lo)) * inv[g] - a[:, cols]).astype(BF16)
            dscale.append(_colsum(dy[:, cols] * _dot(pg, w_ref[g])))
            dyp_g = dyp_ext[:, cols]
            dw_ref[g] += _dot_tn(pg, dyp[:, cols])
            dp_ext = _dot_nt(dyp_g, w_ref[g])
            hi, lo = _split_bf16(dp_ext * inv_e[g])
            das.append(_dot(bb_ref[g], hi) + _dot(bb_ref[g], lo) - dp_ext[POOL_HALO:POOL_HALO + tm, :])
        da = jnp.concatenate(das, axis=1)
        dhn, dsh, dsc, dng = _normmod_bwd(da, xhat, rstd, n, ngv, sc)
        dho_ref[...] = dh + dhn
        _acc_rows(st_ref, _first_of_stream(i, nct), [dsh, dsc, dgate, dng, jnp.concatenate(dscale, axis=1)])

    outs, rode = _compute_call(
        body, name="pool_bwd", grid=(n_tiles,), rider=rider,
        operands=(dh1, dh1, dh1, h, h, h, y, mods, ng, w, scale, bands[0], bands[1]),
        out_shape=[jax.ShapeDtypeStruct((out_rows, d), F32),
                   jax.ShapeDtypeStruct((POOL_GROUPS, gw, gw), F32),
                   jax.ShapeDtypeStruct((_n_streams(nct), 8, d), F32)],
        in_specs=_pool_specs(tm, d, r) + _pool_specs(tm, d, r) + [
            pl.BlockSpec((tm, d), lambda i: (i, 0)),
            pl.BlockSpec((None, 8, d), lambda i: (_sid(i, nct), 0, 0)),
            _const_spec((1, d)), _const_spec(w.shape), _const_spec((1, d)),
            _const_spec((4, tm, kx)), _const_spec((4, tm, kx))],
        out_specs=[pl.BlockSpec((tm, d), lambda i: (jnp.maximum(i - out_off, 0), 0)),
                   pl.BlockSpec((POOL_GROUPS, gw, gw), lambda i: (0, 0, 0)),
                   pl.BlockSpec((None, 8, d), lambda i: (_stat_sid(i, nct), 0, 0))])
    return (*outs, rode)


def _rope_tables(c_len, l_len):
    half = HEAD_DIM // 2
    t = np.arange(l_len)
    row = (t // GRID_W).astype(np.float32)
    col = (t % GRID_W).astype(np.float32)
    inv = (np.float32(ROPE_BASE) ** (-np.arange(0, half, 2, dtype=np.float32) / np.float32(half))).astype(np.float32)
    ang_r = row[:, None] * inv[None, :]
    ang_c = col[:, None] * inv[None, :]
    cos = np.concatenate([np.cos(ang_r), np.cos(ang_r), np.cos(ang_c), np.cos(ang_c)], axis=1)
    sin = np.concatenate([-np.sin(ang_r), np.sin(ang_r), -np.sin(ang_c), np.sin(ang_c)], axis=1)
    cos = np.concatenate([np.ones((c_len, HEAD_DIM), np.float32), cos.astype(np.float32)], axis=0)
    sin = np.concatenate([np.zeros((c_len, HEAD_DIM), np.float32), sin.astype(np.float32)], axis=0)
    return jnp.asarray(cos, F32), jnp.asarray(sin, F32)


def _swap_pairs(x):
    lane = lax.broadcasted_iota(jnp.int32, x.shape, 1)
    return jnp.where((lane % 64) < 32, pltpu.roll(x, 96, 1), pltpu.roll(x, 32, 1))


def _head_norm(x, g):
    rstd = lax.rsqrt(jnp.mean(x * x, axis=-1, keepdims=True) + EPS)
    xhat = x * rstd
    return xhat * g, xhat, rstd


def _qkv_fwd(h, mods, ng, w, qg, kg, cos, sin, n_heads, n_kv, nct, tm):
    t_rows, d = h.shape
    qw, kw = n_heads * HEAD_DIM, n_kv * HEAD_DIM

    def body(h_ref, mod_ref, ng_ref, w_ref, qg_ref, kg_ref, cos_ref, sin_ref, q_ref, k_ref, v_ref):
        a = _normmod(h_ref[...], ng_ref[...], mod_ref[0:1, :], mod_ref[1:2, :])[0]
        qkv = _dot(a.astype(BF16), w_ref[...])
        cosv, sinv = cos_ref[...], sin_ref[...]
        ones = jnp.ones((tm, HEAD_DIM), BF16)
        for hd in range(n_heads + n_kv):
            cols = slice(hd * HEAD_DIM, (hd + 1) * HEAD_DIM)
            xn = _head_norm(qkv[:, cols], qg_ref[...] if hd < n_heads else kg_ref[...])[0]
            xr = xn * cosv + _swap_pairs(xn) * sinv
            if hd < n_heads:
                q_ref[:, cols] = (xr * Q_SCALE).astype(BF16)
            else:
                k_ref[:, (hd - n_heads) * HEAD_DIM:(hd - n_heads + 1) * HEAD_DIM] = xr.astype(BF16)
        for g in range(n_kv):
            v_ref[:, (2 * g) * HEAD_DIM:(2 * g + 1) * HEAD_DIM] = (
                qkv[:, qw + kw + g * HEAD_DIM:qw + kw + (g + 1) * HEAD_DIM].astype(BF16))
            v_ref[:, (2 * g + 1) * HEAD_DIM:(2 * g + 2) * HEAD_DIM] = ones

    return pl.pallas_call(
        body, name="qkv_fwd", grid=(t_rows // tm,),
        out_shape=[jax.ShapeDtypeStruct((t_rows - nct * tm, qw), BF16), jax.ShapeDtypeStruct((t_rows, kw), BF16),
                   jax.ShapeDtypeStruct((t_rows, 2 * kw), BF16)],
        in_specs=[pl.BlockSpec((tm, d), lambda i: (i, 0)),
                  pl.BlockSpec((None, 8, d), lambda i: (_sid(i, nct), 0, 0)),
                  _const_spec((1, d)), _const_spec(w.shape), _const_spec((1, HEAD_DIM)),
                  _const_spec((1, HEAD_DIM)),
                  pl.BlockSpec((tm, HEAD_DIM), lambda i: (i, 0)), pl.BlockSpec((tm, HEAD_DIM), lambda i: (i, 0))],
        out_specs=[pl.BlockSpec((tm, qw), lambda i: (jnp.maximum(i - nct, 0), 0)),
                   pl.BlockSpec((tm, kw), lambda i: (i, 0)), pl.BlockSpec((tm, 2 * kw), lambda i: (i, 0))],
        compiler_params=_params(("arbitrary",)),
    )(h, mods, ng, w, qg, kg, cos, sin)


def _stack_heads(x):
    return jnp.concatenate([x[:, :HEAD_DIM], x[:, HEAD_DIM:]], axis=0)


def _unstack_heads(x, tq):
    return jnp.concatenate([x[:tq], x[tq:]], axis=1)


def _flash_tk(t_rows, tm):
    best = tm
    k = tm
    while k <= FLASH_TK_CAP:
        if t_rows % k == 0:
            best = k
        k += tm
    return best


def _flash_fwd(q, k, v1, n_kv, tq, tm, rider=None):
    t_rows = k.shape[0]
    l_rows = q.shape[0]
    tk = _flash_tk(t_rows, tm)
    nk = t_rows // tk
    gq = 2 * HEAD_DIM

    def body(q_ref, k_ref, v_ref, o_ref, lse_ref, m_s, acc_s):
        ki = pl.program_id(2)

        @pl.when(ki == 0)
        def _():
            m_s[...] = jnp.full_like(m_s, -jnp.inf)
            acc_s[...] = jnp.zeros_like(acc_s)
        kk, vv = k_ref[...], v_ref[...]
        for hh in range(2):
            s = _dot_nt(q_ref[:, hh * HEAD_DIM:(hh + 1) * HEAD_DIM], kk)
            m_prev = m_s[hh]
            m_new = jnp.maximum(m_prev, jnp.max(s, axis=-1, keepdims=True))
            alpha = jnp.exp2(m_prev - m_new)
            p = jnp.exp2(s - jnp.tile(m_new, (1, tk // HEAD_DIM)))
            acc_s[hh] = jnp.tile(alpha, (1, 2)) * acc_s[hh] + _dot(p.astype(BF16), vv)
            m_s[hh] = m_new

        @pl.when(ki == nk - 1)
        def _():
            for hh in range(2):
                acc = acc_s[hh]
                l = acc[:, HEAD_DIM:]
                o_ref[:, hh * HEAD_DIM:(hh + 1) * HEAD_DIM] = (acc[:, :HEAD_DIM] / l).astype(BF16)
                lse_ref[:, hh:hh + 1] = (m_s[hh] + jnp.log2(l))[:, 0:1]

    (o, lse), rode = _compute_call(
        body, name="flash_fwd", grid=(n_kv, l_rows // tq, nk), operands=(q, k, v1), rider=rider,
        out_shape=[jax.ShapeDtypeStruct((l_rows, n_kv * gq), BF16),
                   jax.ShapeDtypeStruct((n_kv, l_rows, 2), F32)],
        in_specs=[pl.BlockSpec((tq, gq), lambda g, i, j: (i, g)),
                  pl.BlockSpec((tk, HEAD_DIM), lambda g, i, j: (j, g)),
                  pl.BlockSpec((tk, gq), lambda g, i, j: (j, g))],
        out_specs=[pl.BlockSpec((tq, gq), lambda g, i, j: (i, g)),
                   pl.BlockSpec((None, tq, 2), lambda g, i, j: (g, i, 0))],
        scratch_shapes=[pltpu.VMEM((2, tq, HEAD_DIM), F32), pltpu.VMEM((2, tq, gq), F32)])
    return o, lse, rode


def _flash_bwd(q, k, v1, do, lse, delta, n_kv, tq, tm, rider=None):
    t_rows = k.shape[0]
    l_rows = q.shape[0]
    tk = _flash_tk(t_rows, tm)
    nq = l_rows // tq
    gq = 2 * HEAD_DIM

    def body(q_ref, k_ref, v_ref, do_ref, lse_ref, dl_ref, dq_ref, dk_ref, dv_ref):
        ki, qi = pl.program_id(1), pl.program_id(2)
        rows = pl.ds(pl.multiple_of(qi * tq, tq), tq)

        @pl.when(qi == 0)
        def _():
            dk_ref[...] = jnp.zeros_like(dk_ref)
            dv_ref[...] = jnp.zeros_like(dv_ref)

        @pl.when(ki == 0)
        def _():
            dq_ref[rows, :] = jnp.zeros((tq, gq), F32)
        kk, vv = k_ref[...], v_ref[:, :HEAD_DIM]
        dk_parts, dv_parts = [], []
        for hh in range(2):
            cols = slice(hh * HEAD_DIM, (hh + 1) * HEAD_DIM)
            qh, doh = q_ref[:, cols], do_ref[:, cols]
            p = jnp.exp2(_dot_nt(qh, kk) - lse_ref[:, hh:hh + 1])
            ds = (p * (_dot_nt(doh, vv) - dl_ref[:, hh:hh + 1])).astype(BF16)
            dv_parts.append(_dot_tn(p.astype(BF16), doh))
            dk_parts.append(_dot_tn(ds, qh))
            dq_ref[rows, cols] += _dot(ds, kk)
        dv_ref[...] += dv_parts[0] + dv_parts[1]
        dk_ref[...] += dk_parts[0] + dk_parts[1]

    (dq, dk, dv), rode = _compute_call(
        body, name="flash_bwd", grid=(n_kv, t_rows // tk, nq), operands=(q, k, v1, do, lse, delta), rider=rider,
        out_shape=[jax.ShapeDtypeStruct((l_rows, n_kv * gq), F32),
                   jax.ShapeDtypeStruct((t_rows, n_kv * HEAD_DIM), F32),
                   jax.ShapeDtypeStruct((t_rows, n_kv * HEAD_DIM), F32)],
        in_specs=[pl.BlockSpec((tq, gq), lambda g, j, i: (i, g)),
                  pl.BlockSpec((tk, HEAD_DIM), lambda g, j, i: (j, g)),
                  pl.BlockSpec((tk, gq), lambda g, j, i: (j, g)),
                  pl.BlockSpec((tq, gq), lambda g, j, i: (i, g)),
                  pl.BlockSpec((None, tq, 2), lambda g, j, i: (g, i, 0)),
                  pl.BlockSpec((None, tq, 2), lambda g, j, i: (g, i, 0))],
        out_specs=[pl.BlockSpec((l_rows, gq), lambda g, j, i: (0, g)),
                   pl.BlockSpec((tk, HEAD_DIM), lambda g, j, i: (j, g)),
                   pl.BlockSpec((tk, HEAD_DIM), lambda g, j, i: (j, g))])
    return dq, dk, dv, rode


def _stack_heads_narrow(x):
    return jnp.concatenate([x[:, 0:1], x[:, 1:2]], axis=0)


def _wo_fwd(h, o, wo, mods, nct, tm):
    l_rows, z = o.shape
    d = h.shape[1]

    def body(h_ref, o_ref, w_ref, mod_ref, y_ref, h1_ref):
        y = _dot(o_ref[...], w_ref[...])
        y_ref[...] = y.astype(BF16)
        h1_ref[...] = h_ref[...] + mod_ref[2:3, :] * y

    return pl.pallas_call(
        body, name="wo_fwd", grid=(l_rows // tm,),
        out_shape=[jax.ShapeDtypeStruct((l_rows, d), BF16), jax.ShapeDtypeStruct((l_rows, d), F32)],
        in_specs=[pl.BlockSpec((tm, d), lambda i: (i + nct, 0)), pl.BlockSpec((tm, z), lambda i: (i, 0)),
                  _const_spec(wo.shape), pl.BlockSpec((None, 8, d), lambda i: (1, 0, 0))],
        out_specs=[pl.BlockSpec((tm, d), lambda i: (i, 0)), pl.BlockSpec((tm, d), lambda i: (i, 0))],
        compiler_params=_params(("arbitrary",)),
    )(h, o, wo, mods)


def _wo_bwd(dh1, y, o, wo, mods, n_kv, tm):
    l_rows, z = o.shape
    d = dh1.shape[1]

    def body(dh_ref, y_ref, o_ref, w_ref, mod_ref, dy_ref, do_ref, dl_ref, st_ref):
        i = pl.program_id(0)
        dh = dh_ref[...]
        dgate = _colsum(dh * y_ref[...].astype(F32))
        dy = (mod_ref[2:3, :] * dh).astype(BF16)
        dy_ref[...] = dy
        do = _dot_nt(dy, w_ref[...])
        do_ref[...] = do.astype(BF16)
        prod = do * o_ref[...].astype(F32)
        for g in range(n_kv):
            d0 = jnp.sum(prod[:, (2 * g) * HEAD_DIM:(2 * g + 1) * HEAD_DIM], axis=-1, keepdims=True)
            d1 = jnp.sum(prod[:, (2 * g + 1) * HEAD_DIM:(2 * g + 2) * HEAD_DIM], axis=-1, keepdims=True)
            dl_ref[g] = jnp.concatenate([d0, d1], axis=1)
        zero = jnp.zeros((1, d), F32)
        _acc_rows(st_ref, i == 0, [zero, zero, dgate])

    return pl.pallas_call(
        body, name="wo_bwd", grid=(l_rows // tm,),
        out_shape=[jax.ShapeDtypeStruct((l_rows, d), BF16), jax.ShapeDtypeStruct((l_rows, z), BF16),
                   jax.ShapeDtypeStruct((n_kv, l_rows, 2), F32), jax.ShapeDtypeStruct((8, d), F32)],
        in_specs=[pl.BlockSpec((tm, d), lambda i: (i, 0)), pl.BlockSpec((tm, d), lambda i: (i, 0)),
                  pl.BlockSpec((tm, z), lambda i: (i, 0)), _const_spec(wo.shape),
                  pl.BlockSpec((None, 8, d), lambda i: (1, 0, 0))],
        out_specs=[pl.BlockSpec((tm, d), lambda i: (i, 0)), pl.BlockSpec((tm, z), lambda i: (i, 0)),
                   pl.BlockSpec((n_kv, tm, 2), lambda i: (0, i, 0)), pl.BlockSpec((8, d), lambda i: (0, 0))],
        compiler_params=_params(("arbitrary",)),
    )(dh1, y, o, wo, mods)


def _qkv_bwd(h, dh_lat, dq, dk, dv, mods, ng, w, qg, kg, cos, sin, n_heads, n_kv, nct, tm):
    t_rows, d = h.shape
    qw, kw = n_heads * HEAD_DIM, n_kv * HEAD_DIM
    scale = HEAD_DIM ** -0.5

    def body(h_ref, dhl_ref, dq_ref, dk_ref, dv_ref, mod_ref, ng_ref, w_ref, qg_ref, kg_ref, cos_ref,
             sin_ref, dh_ref, a_ref, dqkv_ref, st_ref, dg_ref):
        i = pl.program_id(0)
        lat = (i >= nct).astype(F32)
        ngv, sc = ng_ref[...], mod_ref[1:2, :]
        a, xhat, rstd, n = _normmod(h_ref[...], ngv, mod_ref[0:1, :], sc)
        ab = a.astype(BF16)
        a_ref[...] = ab
        qkv = _dot(ab, w_ref[...])
        cosv, sinv = cos_ref[...], sin_ref[...]
        dqg = jnp.zeros((1, HEAD_DIM), F32)
        dkg = jnp.zeros((1, HEAD_DIM), F32)
        for hd in range(n_heads + n_kv):
            cols = slice(hd * HEAD_DIM, (hd + 1) * HEAD_DIM)
            is_q = hd < n_heads
            g = qg_ref[...] if is_q else kg_ref[...]
            _, hx, hr = _head_norm(qkv[:, cols], g)
            if is_q:
                dxr = dq_ref[:, cols] * (scale * lat)
            else:
                dxr = dk_ref[:, (hd - n_heads) * HEAD_DIM:(hd - n_heads + 1) * HEAD_DIM] * (1.0 / LOG2E)
            dxn = dxr * cosv + _swap_pairs(dxr * sinv)
            if is_q:
                dqg = dqg + _colsum(dxn * hx)
            else:
                dkg = dkg + _colsum(dxn * hx)
            dxh = dxn * g
            dx = hr * (dxh - hx * jnp.mean(dxh * hx, axis=-1, keepdims=True))
            dqkv_ref[:, cols] = dx.astype(BF16)
        dqkv_ref[:, qw + kw:] = dv_ref[...].astype(BF16)
        da = _dot_nt(dqkv_ref[...], w_ref[...])
        dhn, dsh, dsc, dng = _normmod_bwd(da, xhat, rstd, n, ngv, sc)
        dh_ref[...] = dhl_ref[...] * lat + dhn
        _acc_rows(st_ref, _first_of_stream(i, nct), [dsh, dsc, jnp.zeros((1, d), F32), dng])
        _acc_rows(dg_ref, i == 0, [dqg, dkg])

    lat_map = lambda i: (jnp.maximum(i - nct, 0), 0)
    return pl.pallas_call(
        body, name="qkv_bwd", grid=(t_rows // tm,),
        out_shape=[jax.ShapeDtypeStruct((t_rows, d), F32), jax.ShapeDtypeStruct((t_rows, d), BF16),
                   jax.ShapeDtypeStruct((t_rows, qw + 2 * kw), BF16), jax.ShapeDtypeStruct((2, 8, d), F32),
                   jax.ShapeDtypeStruct((8, HEAD_DIM), F32)],
        in_specs=[pl.BlockSpec((tm, d), lambda i: (i, 0)), pl.BlockSpec((tm, d), lat_map),
                  pl.BlockSpec((tm, qw), lat_map), pl.BlockSpec((tm, kw), lambda i: (i, 0)),
                  pl.BlockSpec((tm, kw), lambda i: (i, 0)),
                  pl.BlockSpec((None, 8, d), lambda i: (_sid(i, nct), 0, 0)),
                  _const_spec((1, d)), _const_spec(w.shape), _const_spec((1, HEAD_DIM)),
                  _const_spec((1, HEAD_DIM)),
                  pl.BlockSpec((tm, HEAD_DIM), lambda i: (i, 0)), pl.BlockSpec((tm, HEAD_DIM), lambda i: (i, 0))],
        out_specs=[pl.BlockSpec((tm, d), lambda i: (i, 0)), pl.BlockSpec((tm, d), lambda i: (i, 0)),
                   pl.BlockSpec((tm, qw + 2 * kw), lambda i: (i, 0)),
                   pl.BlockSpec((None, 8, d), lambda i: (_sid(i, nct), 0, 0)),
                   pl.BlockSpec((8, HEAD_DIM), lambda i: (0, 0))],
        compiler_params=_params(("arbitrary",)),
    )(h, dh_lat, dq, dk, dv, mods, ng, w, qg, kg, cos, sin)


def _gmlp_core(a_bf, win_ref, lng, lnb, ws_ref, bst_ref, tm, half, with_grad=False):
    z = _dot(a_bf, win_ref[...])
    zu, zv = z[:, :half], z[:, half:]
    if with_grad:
        (u, zu), (v, zv) = _gelu_and_grad(zu), _gelu_and_grad(zv)
    else:
        u, v, zu, zv = _gelu(zu), _gelu(zv), None, None
    mu = jnp.mean(v, axis=-1, keepdims=True)
    vc = v - mu
    rstd_v = lax.rsqrt(jnp.mean(vc * vc, axis=-1, keepdims=True) + EPS)
    vhat = vc * rstd_v
    vln = (vhat * lng + lnb).astype(BF16)
    gw = half // GMLP_GROUPS
    rows = []
    for ch in range(tm // CHUNK):
        rs = slice(ch * CHUNK, (ch + 1) * CHUNK)
        cols = []
        for g in range(GMLP_GROUPS):
            cs = slice(g * gw, (g + 1) * gw)
            cols.append(_dot(ws_ref[g], vln[rs, cs]) + bst_ref[:, g:g + 1])
        rows.append(jnp.concatenate(cols, axis=1))
    sv = rows[0] if len(rows) == 1 else jnp.concatenate(rows, axis=0)
    return zu, zv, u, vhat, rstd_v, vln, sv


def _gmlp_fwd(h, mods, ng, win, lng, lnb, ws, bst, wout, tm):
    l_rows, d = h.shape
    half = wout.shape[0]

    def body(h_ref, mod_ref, ng_ref, win_ref, lng_ref, lnb_ref, ws_ref, bst_ref, wout_ref, y_ref, h1_ref):
        hv = h_ref[...]
        a = _normmod(hv, ng_ref[...], mod_ref[0:1, :], mod_ref[1:2, :])[0]
        _, _, u, _, _, _, sv = _gmlp_core(a.astype(BF16), win_ref, lng_ref[...], lnb_ref[...], ws_ref,
                                          bst_ref, tm, half)
        y = _dot((u * sv).astype(BF16), wout_ref[...])
        y_ref[...] = y.astype(BF16)
        h1_ref[...] = hv + mod_ref[2:3, :] * y

    return pl.pallas_call(
        body, name="gmlp_fwd", grid=(l_rows // tm,),
        out_shape=[jax.ShapeDtypeStruct((l_rows, d), BF16), jax.ShapeDtypeStruct((l_rows, d), F32)],
        in_specs=[pl.BlockSpec((tm, d), lambda i: (i, 0)), pl.BlockSpec((None, 8, d), lambda i: (1, 0, 0)),
                  _const_spec((1, d)), _const_spec(win.shape), _const_spec((1, half)), _const_spec((1, half)),
                  _const_spec(ws.shape), _const_spec(bst.shape), _const_spec(wout.shape)],
        out_specs=[pl.BlockSpec((tm, d), lambda i: (i, 0)), pl.BlockSpec((tm, d), lambda i: (i, 0))],
        compiler_params=_params(("arbitrary",)),
    )(h, mods, ng, win, lng, lnb, ws, bst, wout)


def _gmlp_bwd(dh1, h, y, mods, ng, win, lng, lnb, ws, wst, bst, wout, tm):
    l_rows, d = h.shape
    half = wout.shape[0]
    gw = half // GMLP_GROUPS

    def body(dh_ref, h_ref, y_ref, mod_ref, ng_ref, win_ref, lng_ref, lnb_ref, ws_ref, wst_ref, bst_ref,
             wout_ref, dho_ref, a_ref, dz_ref, gt_ref, dy_ref, st_ref, ln_ref, dws_ref, dbs_ref):
        i = pl.program_id(0)
        ngv, sc = ng_ref[...], mod_ref[1:2, :]
        lngv = lng_ref[...]
        a, xhat, rstd, n = _normmod(h_ref[...], ngv, mod_ref[0:1, :], sc)
        ab = a.astype(BF16)
        a_ref[...] = ab
        gu, gv, u, vhat, rstd_v, vln, sv = _gmlp_core(ab, win_ref, lngv, lnb_ref[...], ws_ref, bst_ref,
                                                      tm, half, with_grad=True)
        gt_ref[...] = (u * sv).astype(BF16)
        dh = dh_ref[...]
        dgate = _colsum(dh * y_ref[...].astype(F32))
        dy = (mod_ref[2:3, :] * dh).astype(BF16)
        dy_ref[...] = dy
        dgated = _dot_nt(dy, wout_ref[...])
        du = dgated * sv
        dsv = (dgated * u).astype(BF16)

        @pl.when(i == 0)
        def _():
            dws_ref[...] = jnp.zeros_like(dws_ref)
            dbs_ref[...] = jnp.zeros_like(dbs_ref)
        lane = lax.broadcasted_iota(jnp.int32, (CHUNK, 128), 1)
        dbs = jnp.zeros((CHUNK, 128), F32)
        rows = []
        for ch in range(tm // CHUNK):
            rs = slice(ch * CHUNK, (ch + 1) * CHUNK)
            cols = []
            for g in range(GMLP_GROUPS):
                cs = slice(g * gw, (g + 1) * gw)
                dsv_cg = dsv[rs, cs]
                dws_ref[g] += _dot_nt(dsv_cg, vln[rs, cs])
                cols.append(_dot(wst_ref[g], dsv_cg))
                dbs = dbs + jnp.where(lane == g, jnp.sum(dsv_cg.astype(F32), axis=-1, keepdims=True), 0.0)
            rows.append(jnp.concatenate(cols, axis=1))
        dbs_ref[...] += dbs
        dvln = rows[0] if len(rows) == 1 else jnp.concatenate(rows, axis=0)
        dlng = _colsum(dvln * vhat)
        dlnb = _colsum(dvln)
        dvh = dvln * lngv
        dv = rstd_v * (dvh - jnp.mean(dvh, axis=-1, keepdims=True)
                       - vhat * jnp.mean(dvh * vhat, axis=-1, keepdims=True))
        dz_ref[:, :half] = (du * gu).astype(BF16)
        dz_ref[:, half:] = (dv * gv).astype(BF16)
        da = _dot_nt(dz_ref[...], win_ref[...])
        dhn, dsh, dsc, dng = _normmod_bwd(da, xhat, rstd, n, ngv, sc)
        dho_ref[...] = dh + dhn
        _acc_rows(st_ref, i == 0, [dsh, dsc, dgate, dng])
        _acc_rows(ln_ref, i == 0, [dlng, dlnb])

    row = lambda w: pl.BlockSpec((tm, w), lambda i: (i, 0))
    return pl.pallas_call(
        body, name="gmlp_bwd", grid=(l_rows // tm,),
        out_shape=[jax.ShapeDtypeStruct((l_rows, d), F32), jax.ShapeDtypeStruct((l_rows, d), BF16),
                   jax.ShapeDtypeStruct((l_rows, 2 * half), BF16), jax.ShapeDtypeStruct((l_rows, half), BF16),
                   jax.ShapeDtypeStruct((l_rows, d), BF16), jax.ShapeDtypeStruct((8, d), F32),
                   jax.ShapeDtypeStruct((8, half), F32), jax.ShapeDtypeStruct(ws.shape, F32),
                   jax.ShapeDtypeStruct((CHUNK, 128), F32)],
        in_specs=[row(d), row(d), row(d), pl.BlockSpec((None, 8, d), lambda i: (1, 0, 0)),
                  _const_spec((1, d)), _const_spec(win.shape), _const_spec((1, half)), _const_spec((1, half)),
                  _const_spec(ws.shape), _const_spec(ws.shape), _const_spec(bst.shape), _const_spec(wout.shape)],
        out_specs=[row(d), row(d), row(2 * half), row(half), row(d),
                   pl.BlockSpec((8, d), lambda i: (0, 0)), pl.BlockSpec((8, half), lambda i: (0, 0)),
                   pl.BlockSpec(ws.shape, lambda i: (0, 0, 0)), pl.BlockSpec((CHUNK, 128), lambda i: (0, 0))],
        compiler_params=_params(("arbitrary",)),
    )(dh1, h, y, mods, ng, win, lng, lnb, ws, wst, bst, wout)


def _head(h, final_g, target, tm):
    l_rows, d = h.shape
    n_tiles = l_rows // tm

    def body(h_ref, g_ref, t_ref, dh_ref, loss_ref, dg_ref, acc_ref):
        i = pl.program_id(0)
        g = g_ref[...]
        hv = h_ref[...]
        rstd = lax.rsqrt(jnp.mean(hv * hv, axis=-1, keepdims=True) + EPS)
        xhat = hv * rstd
        e = xhat * g - t_ref[...]
        dout = e * (1.0 / d)
        dxhat = dout * g
        dh_ref[...] = rstd * (dxhat - xhat * jnp.mean(dxhat * xhat, axis=-1, keepdims=True))
        _acc_rows(dg_ref, i == 0, [_colsum(dout * xhat)])
        _acc_rows(acc_ref, i == 0, [_colsum(e * e)])

        @pl.when(i == n_tiles - 1)
        def _():
            total = jnp.sum(acc_ref[0:1, :], axis=-1, keepdims=True) * (0.5 / d)
            loss_ref[...] = jnp.broadcast_to(total, loss_ref.shape)

    return pl.pallas_call(
        body, name="loss_head", grid=(n_tiles,),
        out_shape=[jax.ShapeDtypeStruct((l_rows, d), F32), jax.ShapeDtypeStruct((8, 128), F32),
                   jax.ShapeDtypeStruct((8, d), F32)],
        in_specs=[pl.BlockSpec((tm, d), lambda i: (i, 0)), _const_spec((1, d)),
                  pl.BlockSpec((tm, d), lambda i: (i, 0))],
        out_specs=[pl.BlockSpec((tm, d), lambda i: (i, 0)), pl.BlockSpec((8, 128), lambda i: (0, 0)),
                   pl.BlockSpec((8, d), lambda i: (0, 0))],
        scratch_shapes=[pltpu.VMEM((8, d), F32)],
        compiler_params=_params(("arbitrary",)),
    )(h, final_g, target)


def _adamw(w, gparts, m, v, name):
    shape = w.shape
    cols = shape[-1]
    rows = int(np.prod(shape[:-1])) if len(shape) > 1 else 1
    nparts = gparts.shape[0]
    w2, m2, v2 = (t.reshape(rows, cols) for t in (w, m, v))
    g2 = gparts.reshape(nparts, rows, cols)
    tr = rows
    part_bytes = nparts * cols * gparts.dtype.itemsize
    for cand in (1024, 512, 256, 128, 64, 32, 16, 8):
        if rows * max(part_bytes, cols * 4) <= (2 << 20):
            break
        if rows % cand == 0 and cand < rows:
            tr = cand
            if cand * max(part_bytes, cols * 4) <= (2 << 20):
                break
    c1 = 1.0 - ADAM_B1 ** ADAM_STEP
    c2 = 1.0 - ADAM_B2 ** ADAM_STEP

    def body(w_ref, g_ref, m_ref, v_ref, go_ref, d_ref, mo_ref, vo_ref):
        g = g_ref[0].astype(F32)
        for k in range(1, nparts):
            g = g + g_ref[k].astype(F32)
        mn = ADAM_B1 * m_ref[...] + (1.0 - ADAM_B1) * g
        vn = ADAM_B2 * v_ref[...] + (1.0 - ADAM_B2) * (g * g)
        go_ref[...] = g
        mo_ref[...] = mn
        vo_ref[...] = vn
        d_ref[...] = -ADAM_LR * ((mn / c1) / (jnp.sqrt(vn / c2) + ADAM_EPS) + ADAM_WD * w_ref[...])

    spec = pl.BlockSpec((tr, cols), lambda i: (i, 0))
    outs = pl.pallas_call(
        body, name=name, grid=(rows // tr,),
        out_shape=[jax.ShapeDtypeStruct((rows, cols), F32)] * 4,
        in_specs=[spec, pl.BlockSpec((nparts, tr, cols), lambda i: (0, i, 0)), spec, spec],
        out_specs=[spec] * 4,
        compiler_params=_params(("parallel",)),
    )(w2, g2, m2, v2)
    return tuple(o.reshape(shape) for o in outs)


def _natural_cols(g):
    return jnp.moveaxis(g, 0, -2).reshape(g.shape[1:-1] + (N_DEV * g.shape[-1],))


def _natural_rows(g):
    return jnp.moveaxis(g, 0, -3).reshape(g.shape[1:-2] + (N_DEV * g.shape[-2], g.shape[-1]))


def _shard_cols(full):
    n = full.shape[-1] // N_DEV
    return jnp.moveaxis(full.reshape(full.shape[:-1] + (N_DEV, n)), -2, 0)


def _shard_rows(full):
    r = full.shape[-2] // N_DEV
    return jnp.moveaxis(full.reshape(full.shape[:-2] + (N_DEV, r, full.shape[-1])), -3, 0)


def _my_cols(gathered, me, n):
    return lax.dynamic_slice_in_dim(gathered, me * n, n, axis=gathered.ndim - 1)


def kernel(x, c, ctx, c_ctx, ada_w, ada_b, norm_g, mlp_w1, mlp_w2, pool_w, pool_scale, attn_w_qkv, attn_w_o, attn_q_g, attn_k_g, gm_w_in, gm_ln_g, gm_ln_b, gm_ws, gm_bs, gm_w_out, final_g, loss_target, m_c_ctx, m_ada_w, m_ada_b, m_norm_g, m_mlp_w1, m_mlp_w2, m_pool_w, m_pool_scale, m_attn_w_qkv, m_attn_w_o, m_attn_q_g, m_attn_k_g, m_gm_w_in, m_gm_ln_g, m_gm_ln_b, m_gm_ws, m_gm_bs, m_gm_w_out, m_final_g, v_c_ctx, v_ada_w, v_ada_b, v_norm_g, v_mlp_w1, v_mlp_w2, v_pool_w, v_pool_scale, v_attn_w_qkv, v_attn_w_o, v_attn_q_g, v_attn_k_g, v_gm_w_in, v_gm_ln_g, v_gm_ln_b, v_gm_ws, v_gm_bs, v_gm_w_out, v_final_g):
    l_len, d = x.shape[1], x.shape[2]
    c_len = ctx.shape[1]
    n_layers = ada_w.shape[0]
    assert n_layers == 4 and x.shape[0] == 1
    n_heads = d // HEAD_DIM
    n_kv = n_heads // 2
    half = gm_w_out.shape[1] * N_DEV
    tm = c_len if c_len <= 256 else 256
    assert c_len % tm == 0 and l_len % tm == 0 and tm % CHUNK == 0 and l_len % GRID_W == 0
    nct = c_len // tm
    me = _dev_index(*_coords())
    n_ada = ada_w.shape[-1]

    first = [t.astype(BF16) for t in (mlp_w1[0], mlp_w2[0], pool_w, attn_w_qkv[0])]
    small = [c, norm_g.reshape(n_layers * 2, -1), pool_scale, gm_ln_g, gm_ln_b]
    w1_0g, w2_0g, pool_g, qkv_g, c_all, ng_g, ps_g, lng_g, lnb_g = _all_gather(first + small, "gather_first")
    c_all = c_all.reshape(N_DEV, d)
    later = _GatherAcrossChips([t.astype(BF16) for t in
                                (mlp_w1[1:], mlp_w2[1:], attn_w_o[0], gm_w_in[0], gm_w_out[0])])
    pool_wf = _natural_rows(pool_g)
    wqkv = _natural_cols(qkv_g)
    ng_full = _natural_cols(ng_g.reshape(N_DEV, n_layers * 2, 1, -1)).reshape(n_layers, 2, 1, d)
    ps_full = _natural_cols(ps_g.reshape(N_DEV, 2, 1, -1))
    lng_full = _natural_cols(lng_g.reshape(N_DEV, 1, -1))
    lnb_full = _natural_cols(lnb_g.reshape(N_DEV, 1, -1))

    c_ctx2 = c_ctx.reshape(1, d)
    ada_b_loc = lax.dynamic_slice_in_dim(ada_b, me * n_ada, n_ada, axis=1).reshape(n_layers, 1, n_ada)
    (mod_g,) = _all_gather([_mods_local(c_all, c_ctx2, ada_w, ada_b_loc)], "gather_mods")
    mod_full = jnp.moveaxis(mod_g, 0, 2).reshape(n_layers, 16, 6, d)
    mod_lat = lax.dynamic_index_in_dim(mod_full, me, axis=1, keepdims=False)
    mod_ctx = mod_full[:, 8]
    mods = jnp.stack([mod_ctx, mod_lat], axis=1)
    mods = jnp.concatenate([mods, jnp.zeros((n_layers, 2, 2, d), F32)], axis=2)

    bands = _pool_bands(tm)
    cos, sin = _rope_tables(c_len, l_len)
    ws_bf = gm_ws[0].astype(BF16)
    wst_bf = jnp.swapaxes(gm_ws[0], 1, 2).astype(BF16)
    bst = jnp.zeros((CHUNK, 128), F32).at[:, :GMLP_GROUPS].set(gm_bs[0].T)
    ng = lambda i, j: ng_full[i, j]

    h0 = jnp.concatenate([ctx[0], x[0]], axis=0)
    y0, h1 = _pool_fwd(h0, mods[0], ng(0, 0), pool_wf[0].astype(BF16), ps_full[0], bands, nct, tm, c_len, l_len)
    w1 = [_natural_cols(w1_0g)]
    w2 = [_natural_rows(w2_0g)]
    h2, p0, ym0 = _mlp_fwd(h1, mods[0], ng(0, 1), w1[0], w2[0], nct, tm)
    q, k, v1 = _qkv_fwd(h2, mods[1], ng(1, 0), wqkv, attn_q_g, attn_k_g, cos, sin, n_heads, n_kv, nct, tm)
    o, lse, later_g = _flash_fwd(q, k, v1, n_kv, 2 * tm, tm, rider=later)
    w1_g, w2_g, wo_g, gin_g, gout_g = _exchange_call(_ForwardToSibling(later_g), "forward_weights", in_place=True)
    w1_rest, w2_rest = _natural_cols(w1_g), _natural_rows(w2_g)
    w1 += [w1_rest[i] for i in range(n_layers - 1)]
    w2 += [w2_rest[i] for i in range(n_layers - 1)]
    wo = _natural_rows(wo_g)
    win = _natural_cols(gin_g)
    wout = _natural_rows(gout_g)
    y1, h3 = _wo_fwd(h2, o, wo, mods[1], nct, tm)
    h4, p1, ym1 = _mlp_fwd(h3, mods[1], ng(1, 1), w1[1], w2[1], 0, tm)
    y2, h5 = _gmlp_fwd(h4, mods[2], ng(2, 0), win, lng_full, lnb_full, ws_bf, bst, wout, CHUNK)
    h6, p2, ym2 = _mlp_fwd(h5, mods[2], ng(2, 1), w1[2], w2[2], 0, tm)
    y3, h7 = _pool_fwd(h6, mods[3], ng(3, 0), pool_wf[1].astype(BF16), ps_full[1], bands, 0, tm, c_len, l_len)
    h8, p3, ym3 = _mlp_fwd(h7, mods[3], ng(3, 1), w1[3], w2[3], 0, tm)
    dh, loss_part, dfinal = _head(h8, final_g.reshape(1, d), loss_target[0], tm)

    dw1, dw2, st_mlp = [None] * 4, [None] * 4, [None] * 4

    def mlp_back(i, dh, h_in, p, ym, nct_i):
        dh_in, m_bf, du, dacc, st, _ = _mlp_bwd(dh, h_in, p, ym, mods[i], ng(i, 1), w1[i], w2[i], nct_i, tm)
        dw1[i] = _tn_matmul(m_bf, du, "tn_w1")
        dw2[i] = _tn_matmul(p, dacc, "tn_w2", square_x=True)
        st_mlp[i] = st
        return dh_in

    dh = mlp_back(3, dh, h7, p3, ym3, 0)
    dh, dpw1, st_pool3, _ = _pool_bwd(dh, h6, y3, mods[3], ng(3, 0), pool_wf[1].astype(BF16), ps_full[1], bands,
                                      0, tm, c_len, l_len, False)
    dh = mlp_back(2, dh, h5, p2, ym2, 0)
    dh, a_bf, dz, gated, dy, st_g, st_ln, dws, dbst = _gmlp_bwd(
        dh, h4, y2, mods[2], ng(2, 0), win, lng_full, lnb_full, ws_bf, wst_bf, bst, wout, CHUNK)
    dwin = _tn_matmul(a_bf, dz, "tn_gm_in")
    dwout = _tn_matmul(gated, dy, "tn_gm_out")
    dh = mlp_back(1, dh, h3, p1, ym1, 0)
    dy1, do, delta, st_wo = _wo_bwd(dh, y1, o, wo, mods[1], n_kv, tm)
    dwo = _tn_matmul(o, dy1, "tn_wo")
    grads_mid = _AllToAll([_shard_cols(jnp.stack(dw1[1:])), _shard_rows(jnp.stack(dw2[1:])),
                           _shard_rows(dpw1.astype(BF16)), _shard_rows(dwo), _shard_cols(dwin), _shard_rows(dwout)])
    dq, dk, dv, (g_w1_rest, g_w2_rest, g_pool1, g_wo, g_gin, g_gout) = _flash_bwd(
        q, k, v1, do, lse, delta, n_kv, 2 * tm, tm, rider=grads_mid)
    dh, a_bf, dqkv, st_q, dgains = _qkv_bwd(h2, dh, dq, dk, dv, mods[1], ng(1, 0), wqkv, attn_q_g, attn_k_g,
                                            cos, sin, n_heads, n_kv, nct, tm)
    dwqkv = _tn_matmul(a_bf, dqkv, "tn_qkv")
    dh, m_bf, du, dacc, st_mlp[0], (g_qkv,) = _mlp_bwd(dh, h1, p0, ym0, mods[0], ng(0, 1), w1[0], w2[0], nct, tm,
                                                       rider=_AllToAll([_shard_cols(dwqkv)]))
    dw1[0] = _tn_matmul(m_bf, du, "tn_w1")
    dw2[0], (g_w1_0,) = _tn_matmul(p0, dacc, "tn_w2", square_x=True, rider=_AllToAll([_shard_cols(dw1[0])]))
    grad_x, dpw0, st_pool0, (g_w2_0,) = _pool_bwd(dh, h0, y0, mods[0], ng(0, 0), pool_wf[0].astype(BF16), ps_full[0],
                                                  bands, nct, tm, c_len, l_len, True,
                                                  rider=_AllToAll([_shard_rows(dw2[0])]))
    g_w1 = jnp.concatenate([g_w1_0[:, None], g_w1_rest], axis=1)
    g_w2 = jnp.concatenate([g_w2_0[:, None], g_w2_rest], axis=1)

    mix_lat = [st_pool0[-1], st_q[1] + st_wo, st_g, st_pool3[-1]]
    mlp_lat = [st[-1] for st in st_mlp]
    dmod_lat = jnp.stack([jnp.concatenate([mix_lat[i][0:3], mlp_lat[i][0:3]]) for i in range(n_layers)])
    dmod_ctx = jnp.stack([jnp.concatenate([st_pool0[0][0:3], st_mlp[0][0][0:3]]),
                          jnp.concatenate([st_q[0][0:2], jnp.zeros((4, d), F32)]),
                          jnp.zeros((6, d), F32), jnp.zeros((6, d), F32)])
    dng_part = jnp.stack([jnp.stack([mix_lat[0][3] + st_pool0[0][3], mlp_lat[0][3] + st_mlp[0][0][3]]),
                          jnp.stack([mix_lat[1][3] + st_q[0][3], mlp_lat[1][3]]),
                          jnp.stack([mix_lat[2][3], mlp_lat[2][3]]),
                          jnp.stack([mix_lat[3][3], mlp_lat[3][3]])])
    dps_part = jnp.stack([mix_lat[0][4] + st_pool0[0][4], mix_lat[3][4]])
    small_parts = [dmod_lat.reshape(n_layers * 6, d), dmod_ctx.reshape(n_layers * 6, d),
                   dng_part.reshape(n_layers * 2, d), dps_part, st_ln, dgains, dws.reshape(-1, CHUNK),
                   dbst, dfinal]
    (gm_lat, gm_ctx, g_ng, g_ps, g_ln, g_gains, g_ws, g_bst, g_final, g_pool0) = _all_gather(
        small_parts, "gather_small_grads", extra=_AllToAll([_shard_rows(dpw0.astype(BF16))]))
    g_pool = jnp.stack([g_pool0, g_pool1], axis=1)

    gm_lat4 = gm_lat.reshape(N_DEV, n_layers, 6 * d)
    gm_ctx4 = gm_ctx.reshape(N_DEV, n_layers, 6 * d)
    dm_lat_loc = jnp.moveaxis(_my_cols(gm_lat4, me, n_ada), 0, 1)
    dm_ctx_loc = jnp.moveaxis(_my_cols(gm_ctx4, me, n_ada), 0, 1)
    g_ada_w, ds_part = _ada_grads(c_all, c_ctx2, ada_w, dm_lat_loc, dm_ctx_loc)
    (ds_all,) = _all_gather([ds_part], "gather_dsctx")
    g_c_ctx = _cctx_grad(ds_all, c_ctx2).reshape(d)

    n_ng = norm_g.shape[-1]
    n_ps = pool_scale.shape[-1]
    n_ln = gm_ln_g.shape[-1]
    gparts = {
        "c_ctx": g_c_ctx[None],
        "ada_w": g_ada_w[None],
        "ada_b": jnp.concatenate([gm_lat4, gm_ctx4], axis=0),
        "norm_g": _my_cols(g_ng.reshape(N_DEV, n_layers, 2, d), me, n_ng),
        "mlp_w1": g_w1, "mlp_w2": g_w2, "pool_w": g_pool,
        "pool_scale": _my_cols(g_ps, me, n_ps),
        "attn_w_qkv": g_qkv[:, None], "attn_w_o": g_wo[:, None],
        "attn_q_g": g_gains[:, 0:1], "attn_k_g": g_gains[:, 1:2],
        "gm_w_in": g_gin[:, None],
        "gm_ln_g": _my_cols(g_ln[:, 0:1], me, n_ln), "gm_ln_b": _my_cols(g_ln[:, 1:2], me, n_ln),
        "gm_ws": g_ws.reshape((N_DEV,) + gm_ws.shape),
        "gm_bs": jnp.swapaxes(g_bst[:, :, :GMLP_GROUPS], 1, 2)[:, None],
        "gm_w_out": g_gout[:, None],
        "final_g": g_final[:, 0],
    }
    weights = dict(c_ctx=(c_ctx, m_c_ctx, v_c_ctx), ada_w=(ada_w, m_ada_w, v_ada_w), ada_b=(ada_b, m_ada_b, v_ada_b),
                   norm_g=(norm_g, m_norm_g, v_norm_g), mlp_w1=(mlp_w1, m_mlp_w1, v_mlp_w1),
                   mlp_w2=(mlp_w2, m_mlp_w2, v_mlp_w2), pool_w=(pool_w, m_pool_w, v_pool_w),
                   pool_scale=(pool_scale, m_pool_scale, v_pool_scale),
                   attn_w_qkv=(attn_w_qkv, m_attn_w_qkv, v_attn_w_qkv), attn_w_o=(attn_w_o, m_attn_w_o, v_attn_w_o),
                   attn_q_g=(attn_q_g, m_attn_q_g, v_attn_q_g), attn_k_g=(attn_k_g, m_attn_k_g, v_attn_k_g),
                   gm_w_in=(gm_w_in, m_gm_w_in, v_gm_w_in), gm_ln_g=(gm_ln_g, m_gm_ln_g, v_gm_ln_g),
                   gm_ln_b=(gm_ln_b, m_gm_ln_b, v_gm_ln_b), gm_ws=(gm_ws, m_gm_ws, v_gm_ws),
                   gm_bs=(gm_bs, m_gm_bs, v_gm_bs), gm_w_out=(gm_w_out, m_gm_w_out, v_gm_w_out),
                   final_g=(final_g, m_final_g, v_final_g))
    grads, deltas, new_m, new_v = [], [], [], []
    for wname, (w_, m_, v_) in weights.items():
        g_, d_, nm_, nv_ = _adamw(w_, gparts[wname], m_, v_, "adamw_" + wname)
        grads.append(g_)
        deltas.append(d_)
        new_m.append(nm_)
        new_v.append(nv_)

    loss = lax.psum(loss_part[0, 0], ("x", "y", "c"))
    return (loss, grad_x[None], *grads, *deltas, *new_m, *new_v)
```

```python
import functools
import math

import numpy as np
import jax
import jax.numpy as jnp
from jax import lax
from jax.experimental import pallas as pl
from jax.experimental.pallas import tpu as pltpu

F32 = jnp.float32
BF16 = jnp.bfloat16
MESH_ID = pl.DeviceIdType.MESH

N_DEV = 8
EPS = 1e-6
HEAD_DIM = 128
GRID_W = 64
ROPE_BASE = 10000.0
CHUNK = 128
POOL_WINDOWS = (2, 4, 8, 16)
POOL_GROUPS = 4
POOL_HALO = 8
GMLP_GROUPS = 8
ADAM_LR, ADAM_B1, ADAM_B2, ADAM_EPS, ADAM_WD, ADAM_STEP = 0.001, 0.9, 0.999, 1e-08, 0.01, 10

V7X_VMEM_BYTES = 64 << 20
VMEM_LIMIT_BIG = V7X_VMEM_BYTES - (8 << 20)
FLASH_TK_CAP = 768
LOG2E = math.log2(math.e)
Q_SCALE = HEAD_DIM ** -0.5 * LOG2E


def _params(sem, vmem=VMEM_LIMIT_BIG):
    return pltpu.CompilerParams(dimension_semantics=sem, vmem_limit_bytes=vmem)


def _const_spec(shape):
    nd = len(shape)
    return pl.BlockSpec(shape, lambda *_: (0,) * nd, pipeline_mode=pl.Buffered(1))


def _dot(a, b):
    return jnp.dot(a, b, preferred_element_type=F32)


def _dot_nt(a, b):
    return lax.dot_general(a, b, (((1,), (1,)), ((), ())), preferred_element_type=F32)


def _dot_tn(a, b):
    return lax.dot_general(a, b, (((0,), (0,)), ((), ())), preferred_element_type=F32)


def _colsum(x):
    return jnp.sum(x, axis=0, keepdims=True)


def _sid(i, nct):
    if nct == 0:
        return 1
    return jnp.where(i >= nct, 1, 0)


def _n_streams(nct):
    return 2 if nct else 1


def _stat_sid(i, nct):
    return _sid(i, nct) if nct else 0


def _first_of_stream(i, nct):
    if nct == 0:
        return i == 0
    return jnp.logical_or(i == 0, i == nct)


def _normmod(h, ng, sh, sc):
    rstd = lax.rsqrt(jnp.mean(h * h, axis=-1, keepdims=True) + EPS)
    xhat = h * rstd
    n = xhat * ng
    return n * (1.0 + sc) + sh, xhat, rstd, n


def _normmod_bwd(da, xhat, rstd, n, ng, sc):
    dsh = _colsum(da)
    dsc = _colsum(da * n)
    dn = da * (1.0 + sc)
    dng = _colsum(dn * xhat)
    dxhat = dn * ng
    dh = rstd * (dxhat - xhat * jnp.mean(dxhat * xhat, axis=-1, keepdims=True))
    return dh, dsh, dsc, dng


def _acc_rows(ref, first, rows):
    @pl.when(first)
    def _():
        ref[...] = jnp.zeros_like(ref)
    for r, val in enumerate(rows):
        ref[r:r + 1, :] = ref[r:r + 1, :] + val


_GELU_C = math.sqrt(2.0 / math.pi)


def _gelu(x):
    t = jnp.tanh((_GELU_C * x) * (1.0 + 0.044715 * (x * x)))
    hx = 0.5 * x
    return hx + hx * t


def _gelu_and_grad(x):
    x2 = x * x
    t = jnp.tanh((_GELU_C * x) * (1.0 + 0.044715 * x2))
    hx = 0.5 * x
    g = hx + hx * t
    dg = (0.5 + 0.5 * t) + (hx * (1.0 - t * t)) * (_GELU_C + (3.0 * 0.044715 * _GELU_C) * x2)
    return g, dg


def _coords():
    return lax.axis_index("x"), lax.axis_index("y"), lax.axis_index("c")


def _dev_index(px, py, pc):
    return 4 * px + 2 * py + pc


def _all_gather(xs, name, extra=None):
    n = len(xs)
    e_in, e_in_specs, e_out, e_out_specs, e_scratch = _rider_parts(extra)
    ne = len(e_in)

    def body(*refs):
        x_refs, e_x = refs[:n], refs[n:n + ne]
        o_refs, e_o = refs[n + ne:2 * n + ne], refs[2 * n + ne:2 * n + 2 * ne]
        send_sems, recv_sems, local_sems = refs[2 * n + 2 * ne:2 * n + 2 * ne + 3]
        e_sems = refs[2 * n + 2 * ne + 3:]
        if extra is not None:
            extra.start(e_x, e_o, e_sems)
        x, y, c = _coords()
        me, sibling = (x, y, c), (x, y, 1 - c)
        chips = [(1 - x, y), (x, 1 - y), (1 - x, 1 - y)]

        def copy(a, k, block, to, src=None):
            dst = o_refs[a].at[_dev_index(*block)]
            return pltpu.make_async_remote_copy(
                src_ref=dst if src is None else src, dst_ref=dst,
                send_sem=send_sems.at[7 * a + k], recv_sem=recv_sems.at[7 * a + k],
                device_id=to, device_id_type=MESH_ID)

        mine = [pltpu.make_async_copy(x_refs[a], o_refs[a].at[_dev_index(*me)], local_sems.at[a])
                for a in range(n)]
        for cp in mine:
            cp.start()
        first = []
        for a in range(n):
            first.append(copy(a, 0, me, sibling, src=x_refs[a]))
            first += [copy(a, 1 + j, me, (*chip, c), src=x_refs[a]) for j, chip in enumerate(chips)]
        for cp in first:
            cp.start()
        passed = []
        for a in range(n):
            for j, chip in enumerate(chips):
                copy(a, 1 + j, (*chip, c), me).wait_recv()
                fwd = copy(a, 4 + j, (*chip, c), sibling)
                fwd.start()
                passed.append(fwd)
        for a in range(n):
            copy(a, 0, sibling, me).wait_recv()
            for j, chip in enumerate(chips):
                copy(a, 4 + j, (*chip, 1 - c), me).wait_recv()
        for cp in first + passed:
            cp.wait_send()
        for cp in mine:
            cp.wait()
        if extra is not None:
            extra.finish(e_x, e_o, e_sems)

    any_spec = pl.BlockSpec(memory_space=pl.ANY)
    outs = pl.pallas_call(
        body, name=name,
        out_shape=[jax.ShapeDtypeStruct((N_DEV,) + x.shape, x.dtype) for x in xs] + e_out,
        in_specs=[any_spec] * n + e_in_specs, out_specs=[any_spec] * n + e_out_specs,
        scratch_shapes=[pltpu.SemaphoreType.DMA((7 * n,)), pltpu.SemaphoreType.DMA((7 * n,)),
                        pltpu.SemaphoreType.DMA((n,))] + e_scratch,
    )(*xs, *e_in)
    return list(outs)


class _Exchange:
    per_array = 0
    has_local = True

    def __init__(self, xs):
        self.xs = list(xs)
        n = len(self.xs)
        self.out_shapes = self._out_shapes()
        self.scratch = [pltpu.SemaphoreType.DMA((self.per_array * n,)),
                        pltpu.SemaphoreType.DMA((self.per_array * n,)),
                        pltpu.SemaphoreType.DMA((n,))]

    def _out_shapes(self):
        raise NotImplementedError

    def _copies(self, x_refs, o_refs, sems):
        raise NotImplementedError

    def start(self, x_refs, o_refs, sems):
        mine, sends, _ = self._copies(x_refs, o_refs, sems)
        for cp in mine + sends:
            cp.start()

    def finish(self, x_refs, o_refs, sems):
        mine, sends, arrivals = self._copies(x_refs, o_refs, sems)
        for make in arrivals:
            make().wait_recv()
        for cp in sends:
            cp.wait_send()
        for cp in mine:
            cp.wait()


def _remote(src, dst, sems, k, to):
    return pltpu.make_async_remote_copy(src_ref=src, dst_ref=dst, send_sem=sems[0].at[k], recv_sem=sems[1].at[k],
                                        device_id=to, device_id_type=MESH_ID)


class _GatherAcrossChips(_Exchange):
    per_array = 4

    def _out_shapes(self):
        return [jax.ShapeDtypeStruct((N_DEV,) + x.shape, x.dtype) for x in self.xs]

    def _copies(self, x_refs, o_refs, sems):
        x, y, c = _coords()
        targets = [(x, y, 1 - c), (1 - x, y, c), (x, 1 - y, c), (1 - x, 1 - y, c)]
        mine, sends, arrivals = [], [], []
        for a, (x_ref, o_ref) in enumerate(zip(x_refs, o_refs)):
            own = o_ref.at[_dev_index(x, y, c)]
            mine.append(pltpu.make_async_copy(x_ref, own, sems[2].at[a]))
            for k, to in enumerate(targets):
                sends.append(_remote(x_ref, own, sems, 4 * a + k, to))
                arrivals.append(functools.partial(_remote, x_ref, o_ref.at[_dev_index(*to)], sems, 4 * a + k, to))
        return mine, sends, arrivals


class _ForwardToSibling(_Exchange):
    per_array = 3

    def _out_shapes(self):
        return [jax.ShapeDtypeStruct(x.shape, x.dtype) for x in self.xs]

    def _copies(self, x_refs, o_refs, sems):
        x, y, c = _coords()
        chips = [(1 - x, y), (x, 1 - y), (1 - x, 1 - y)]
        sends, arrivals = [], []
        for a, (x_ref, o_ref) in enumerate(zip(x_refs, o_refs)):
            for j, chip in enumerate(chips):
                held = _dev_index(*chip, c)
                sends.append(_remote(x_ref.at[held], o_ref.at[held], sems, 3 * a + j, (x, y, 1 - c)))
                theirs = _dev_index(*chip, 1 - c)
                arrivals.append(functools.partial(_remote, x_ref.at[theirs], o_ref.at[theirs], sems, 3 * a + j,
                                                  (x, y, 1 - c)))
        return [], sends, arrivals


class _AllToAll(_Exchange):
    per_array = 7

    def _out_shapes(self):
        return [jax.ShapeDtypeStruct(x.shape, x.dtype) for x in self.xs]

    def _copies(self, x_refs, o_refs, sems):
        x, y, c = _coords()
        me_i = _dev_index(x, y, c)
        mine, sends, arrivals = [], [], []
        for a, (x_ref, o_ref) in enumerate(zip(x_refs, o_refs)):
            mine.append(pltpu.make_async_copy(x_ref.at[me_i], o_ref.at[me_i], sems[2].at[a]))
            for r in range(1, 8):
                to = (1 - x if r & 4 else x, 1 - y if r & 2 else y, 1 - c if r & 1 else c)
                to_i = _dev_index(*to)
                sends.append(_remote(x_ref.at[to_i], o_ref.at[me_i], sems, 7 * a + r - 1, to))
                arrivals.append(functools.partial(_remote, x_ref.at[to_i], o_ref.at[to_i], sems, 7 * a + r - 1, to))
        return mine, sends, arrivals


def _exchange_call(ex, name, in_place=False):
    n = len(ex.xs)

    def body(*refs):
        x_refs, o_refs, sems = refs[:n], refs[n:2 * n], refs[2 * n:]
        ex.start(x_refs, o_refs, sems)
        ex.finish(x_refs, o_refs, sems)

    any_spec = pl.BlockSpec(memory_space=pl.ANY)
    outs = pl.pallas_call(
        body, name=name, out_shape=ex.out_shapes, in_specs=[any_spec] * n, out_specs=[any_spec] * n,
        scratch_shapes=ex.scratch, input_output_aliases={a: a for a in range(n)} if in_place else {},
    )(*ex.xs)
    return list(outs)


def _rider_parts(rider):
    if rider is None:
        return [], [], [], [], []
    any_spec = pl.BlockSpec(memory_space=pl.ANY)
    n = len(rider.xs)
    return rider.xs, [any_spec] * n, rider.out_shapes, [any_spec] * n, rider.scratch


def _compute_call(body, *, name, grid, in_specs, out_specs, out_shape, operands, scratch_shapes=(), rider=None):
    in_specs, out_specs, out_shape, scratch_shapes = list(in_specs), list(out_specs), list(out_shape), list(scratch_shapes)
    r_in, r_in_specs, r_out, r_out_specs, r_scratch = _rider_parts(rider)
    n_in, n_out, n_scr, nr = len(operands), len(out_shape), len(scratch_shapes), len(r_in)

    def riding_body(*refs):
        ins, refs = refs[:n_in], refs[n_in:]
        r_x, refs = refs[:nr], refs[nr:]
        outs, refs = refs[:n_out], refs[n_out:]
        r_o, refs = refs[:nr], refs[nr:]
        scratch, r_sems = refs[:n_scr], refs[n_scr:]
        if rider is not None:
            first, last = _grid_ends(grid)
            pl.when(first)(lambda: rider.start(r_x, r_o, r_sems))
        body(*ins, *outs, *scratch)
        if rider is not None:
            pl.when(last)(lambda: rider.finish(r_x, r_o, r_sems))

    res = pl.pallas_call(
        riding_body, name=name, grid=grid, out_shape=out_shape + r_out,
        in_specs=in_specs + r_in_specs, out_specs=out_specs + r_out_specs,
        scratch_shapes=scratch_shapes + r_scratch,
        compiler_params=_params(("arbitrary",) * len(grid)),
    )(*operands, *r_in)
    return list(res[:n_out]), list(res[n_out:])


def _grid_ends(grid):
    first = pl.program_id(0) == 0
    last = pl.program_id(0) == grid[0] - 1
    for ax in range(1, len(grid)):
        first = jnp.logical_and(first, pl.program_id(ax) == 0)
        last = jnp.logical_and(last, pl.program_id(ax) == grid[ax] - 1)
    return first, last


def _silu(x):
    return x * (1.0 / (1.0 + jnp.exp(-x)))


def _cond_rows(c_all, c_ctx):
    d = c_all.shape[-1]
    s = jnp.concatenate([c_all, jnp.zeros((8, d), F32)], axis=0)
    row = lax.broadcasted_iota(jnp.int32, (16, d), 0)
    s = jnp.where(row == 8, c_ctx, s)
    return jnp.where(row <= 8, _silu(s), 0.0)


def _mods_local(c_all, c_ctx, ada_w, ada_b_loc):
    nl, d, n = ada_w.shape

    def body(c_ref, cc_ref, w_ref, b_ref, o_ref):
        s = _cond_rows(c_ref[...], cc_ref[...])
        o_ref[...] = jnp.dot(s, w_ref[...], preferred_element_type=F32,
                             precision=lax.Precision.HIGHEST) + b_ref[...]

    return pl.pallas_call(
        body, name="mods_local", grid=(nl,),
        out_shape=jax.ShapeDtypeStruct((nl, 16, n), F32),
        in_specs=[pl.BlockSpec((8, d), lambda i: (0, 0)), pl.BlockSpec((1, d), lambda i: (0, 0)),
                  pl.BlockSpec((None, d, n), lambda i: (i, 0, 0)),
                  pl.BlockSpec((None, 1, n), lambda i: (i, 0, 0))],
        out_specs=pl.BlockSpec((None, 16, n), lambda i: (i, 0, 0)),
        compiler_params=_params(("arbitrary",)),
    )(c_all, c_ctx, ada_w, ada_b_loc)


def _ada_grads(c_all, c_ctx, ada_w, dm_lat, dm_ctx):
    nl, d, n = ada_w.shape

    def body(c_ref, cc_ref, w_ref, dml_ref, dmc_ref, gw_ref, ds_ref):
        i = pl.program_id(0)
        s = _cond_rows(c_ref[...], cc_ref[...])
        csum = dmc_ref[0:1, :]
        for k in range(1, N_DEV):
            csum = csum + dmc_ref[k:k + 1, :]
        row = lax.broadcasted_iota(jnp.int32, (8, n), 0)
        dm_c = jnp.where(row == 0, csum, 0.0)
        dm = jnp.concatenate([dml_ref[...], dm_c], axis=0)
        gw_ref[...] = lax.dot_general(s, dm, (((0,), (0,)), ((), ())), preferred_element_type=F32,
                                      precision=lax.Precision.HIGHEST)
        ds = lax.dot_general(dm_c, w_ref[...], (((1,), (1,)), ((), ())),
                             preferred_element_type=F32, precision=lax.Precision.HIGHEST)

        @pl.when(i == 0)
        def _():
            ds_ref[...] = jnp.zeros_like(ds_ref)
        ds_ref[...] += ds

    return pl.pallas_call(
        body, name="ada_grads", grid=(nl,),
        out_shape=[jax.ShapeDtypeStruct((nl, d, n), F32), jax.ShapeDtypeStruct((8, d), F32)],
        in_specs=[pl.BlockSpec((8, d), lambda i: (0, 0)), pl.BlockSpec((1, d), lambda i: (0, 0)),
                  pl.BlockSpec((None, d, n), lambda i: (i, 0, 0)),
                  pl.BlockSpec((None, 8, n), lambda i: (i, 0, 0)),
                  pl.BlockSpec((None, 8, n), lambda i: (i, 0, 0))],
        out_specs=[pl.BlockSpec((None, d, n), lambda i: (i, 0, 0)),
                   pl.BlockSpec((8, d), lambda i: (0, 0))],
        compiler_params=_params(("arbitrary",)),
    )(c_all, c_ctx, ada_w, dm_lat, dm_ctx)


def _cctx_grad_and_loss(ds_parts, c_ctx, loss_parts):
    d = c_ctx.shape[-1]

    def body(p_ref, c_ref, l_ref, o_ref, lo_ref):
        ds, loss = p_ref[0], l_ref[0]
        for k in range(1, N_DEV):
            ds = ds + p_ref[k]
            loss = loss + l_ref[k]
        x = c_ref[...]
        sg = 1.0 / (1.0 + jnp.exp(-x))
        o_ref[...] = ds[0:1, :] * (sg * (1.0 + x * (1.0 - sg)))
        lo_ref[...] = loss

    return pl.pallas_call(body, name="cctx_grad", out_shape=[jax.ShapeDtypeStruct((1, d), F32),
                                                             jax.ShapeDtypeStruct((8, 128), F32)])(ds_parts, c_ctx, loss_parts)


def _mlp_fwd(h1, mods, ng, w1, w2, nct, tm, row_off=0):
    d = h1.shape[1]
    r = h1.shape[0] - row_off * tm
    fc = w1.shape[2]
    f = N_DEV * fc

    def body(h_ref, mod_ref, ng_ref, w1_ref, w2_ref, h2_ref, p_ref, y_ref):
        h = h_ref[...]
        a, _, _, _ = _normmod(h, ng_ref[...], mod_ref[3:4, :], mod_ref[4:5, :])
        ab = a.astype(BF16)
        acc = jnp.zeros((tm, d), F32)
        for j in range(N_DEV):
            sl = slice(j * fc, (j + 1) * fc)
            p = jnp.maximum(_dot(ab, w1_ref[j]), 0.0)
            p_ref[:, sl] = p.astype(BF16)
            acc = acc + _dot((p * p).astype(BF16), w2_ref[j])
        y_ref[...] = acc.astype(BF16)
        h2_ref[...] = h + mod_ref[5:6, :] * acc

    return pl.pallas_call(
        body, name="mlp_fwd", grid=(r // tm,),
        out_shape=[jax.ShapeDtypeStruct((r, d), F32), jax.ShapeDtypeStruct((r, f), BF16),
                   jax.ShapeDtypeStruct((r, d), BF16)],
        in_specs=[pl.BlockSpec((tm, d), lambda i: (i + row_off, 0)),
                  pl.BlockSpec((None, 8, d), lambda i: (_sid(i, nct), 0, 0)),
                  _const_spec((1, d)), _const_spec(w1.shape), _const_spec(w2.shape)],
        out_specs=[pl.BlockSpec((tm, d), lambda i: (i, 0)), pl.BlockSpec((tm, f), lambda i: (i, 0)),
                   pl.BlockSpec((tm, d), lambda i: (i, 0))],
        compiler_params=_params(("arbitrary",)),
    )(h1, mods, ng, w1, w2)


def _mlp_bwd(dh2, h1, p, y, mods, ng, w1, w2, nct, tm, row_off=0, rider=None):
    r_rows, d = dh2.shape
    fc = w1.shape[2]
    f = N_DEV * fc

    def body(dh_ref, h_ref, p_ref, y_ref, mod_ref, ng_ref, w1_ref, w2_ref,
             dh1_ref, m_ref, du_ref, dacc_ref, st_ref):
        i = pl.program_id(0)
        dh = dh_ref[...]
        ngv, sc, gate = ng_ref[...], mod_ref[4:5, :], mod_ref[5:6, :]
        a, xhat, rstd, n = _normmod(h_ref[...], ngv, mod_ref[3:4, :], sc)
        m_ref[...] = a.astype(BF16)
        dgate = _colsum(dh * y_ref[...].astype(F32))
        dacc = (gate * dh).astype(BF16)
        dacc_ref[...] = dacc
        dm = jnp.zeros((tm, d), F32)
        for j in range(N_DEV):
            sl = slice(j * fc, (j + 1) * fc)
            pj = p_ref[:, sl].astype(F32)
            du = (_dot_nt(dacc, w2_ref[j]) * (2.0 * pj)).astype(BF16)
            du_ref[:, sl] = du
            dm = dm + _dot_nt(du, w1_ref[j])
        dhn, dsh, dsc, dng = _normmod_bwd(dm, xhat, rstd, n, ngv, sc)
        dh1_ref[...] = dh + dhn
        _acc_rows(st_ref, _first_of_stream(i, nct), [dsh, dsc, dgate, dng])

    outs, rode = _compute_call(
        body, name="mlp_bwd", grid=(r_rows // tm,), operands=(dh2, h1, p, y, mods, ng, w1, w2), rider=rider,
        out_shape=[jax.ShapeDtypeStruct((r_rows, d), F32), jax.ShapeDtypeStruct((r_rows, d), BF16),
                   jax.ShapeDtypeStruct((r_rows, f), BF16),
                   jax.ShapeDtypeStruct((r_rows, d), BF16), jax.ShapeDtypeStruct((_n_streams(nct), 8, d), F32)],
        in_specs=[pl.BlockSpec((tm, d), lambda i: (i, 0)),
                  pl.BlockSpec((tm, d), lambda i: (i + row_off, 0)),
                  pl.BlockSpec((tm, f), lambda i: (i, 0)), pl.BlockSpec((tm, d), lambda i: (i, 0)),
                  pl.BlockSpec((None, 8, d), lambda i: (_sid(i, nct), 0, 0)),
                  _const_spec((1, d)), _const_spec(w1.shape), _const_spec(w2.shape)],
        out_specs=[pl.BlockSpec((tm, d), lambda i: (i, 0)), pl.BlockSpec((tm, d), lambda i: (i, 0)),
                   pl.BlockSpec((tm, f), lambda i: (i, 0)),
                   pl.BlockSpec((tm, d), lambda i: (i, 0)),
                   pl.BlockSpec((None, 8, d), lambda i: (_stat_sid(i, nct), 0, 0))])
    return (*outs, rode)


def _pick(n, cands):
    for cand in cands:
        if n % cand == 0:
            return cand
    return n


def _tn_matmul(x, y, name, col_shards=False, square_x=False, rider=None):
    rows, k1 = x.shape
    k2 = y.shape[1]
    bt = _pick(rows, (1024, 768, 512, 384, 256, 128))
    bk1, bk2 = min(k1, 1024), min(k2, 1024)
    grid = (k1 // bk1, k2 // bk2, rows // bt)
    nt = rows // bt
    n = k2 // N_DEV
    if col_shards:
        assert bk2 % n == 0
        per = bk2 // n
        out_shape = jax.ShapeDtypeStruct((N_DEV, k1, n), BF16)
        out_spec = pl.BlockSpec((per, bk1, n), lambda i, j, t: (j, i, 0))
    else:
        out_shape = jax.ShapeDtypeStruct((k1, k2), BF16)
        out_spec = pl.BlockSpec((bk1, bk2), lambda i, j, t: (i, j))

    def body(x_ref, y_ref, o_ref, acc_ref):
        t = pl.program_id(2)

        @pl.when(t == 0)
        def _():
            acc_ref[...] = jnp.zeros_like(acc_ref)
        xv = x_ref[...]
        acc_ref[...] += _dot_tn(xv * xv if square_x else xv, y_ref[...])

        @pl.when(t == nt - 1)
        def _():
            if col_shards:
                for s in range(per):
                    o_ref[s] = acc_ref[:, s * n:(s + 1) * n].astype(BF16)
            else:
                o_ref[...] = acc_ref[...].astype(BF16)

    (out,), rode = _compute_call(
        body, name=name, grid=grid, operands=(x, y), rider=rider, out_shape=[out_shape],
        in_specs=[pl.BlockSpec((bt, bk1), lambda i, j, t: (t, i)), pl.BlockSpec((bt, bk2), lambda i, j, t: (t, j))],
        out_specs=[out_spec], scratch_shapes=[pltpu.VMEM((bk1, bk2), F32)])
    return out if rider is None else (out, rode)


def _pool_bands(tm):
    k = tm + 128
    t = np.arange(tm)[:, None]
    e = np.arange(k)[None, :]
    fwd, bwd = [], []
    for w in POOL_WINDOWS:
        lo = POOL_HALO + t - w // 2
        fwd.append(((e >= lo) & (e <= lo + w - 1)).astype(np.float32))
        lo_t = POOL_HALO + t - w // 2 + 1
        bwd.append(((e >= lo_t) & (e <= lo_t + w - 1)).astype(np.float32))
    return jnp.asarray(np.stack(fwd), BF16), jnp.asarray(np.stack(bwd), BF16)


def _pool_geometry(i, nct, n_tiles, tm, c_len, l_len):
    if nct == 0:
        pos0 = i * tm
        ls = l_len
        has_prev = i > 0
        has_next = i < n_tiles - 1
    else:
        in_ctx = i < nct
        pos0 = jnp.where(in_ctx, i, i - nct) * tm
        ls = jnp.where(in_ctx, c_len, l_len)
        has_prev = jnp.logical_and(i != 0, i != nct)
        has_next = jnp.logical_and(i != nct - 1, i != n_tiles - 1)
    return pos0, ls, has_prev, has_next


def _window_inv_counts(pos, ls):
    out = []
    for w in POOL_WINDOWS:
        lo = jnp.maximum(pos - w // 2, 0)
        hi = jnp.minimum(pos + w - w // 2, ls)
        cnt = jnp.maximum(hi - lo, 1).astype(F32)
        out.append(1.0 / cnt)
    return out


def _split_bf16(x):
    hi = x.astype(BF16)
    return hi, (x - hi.astype(F32)).astype(BF16)


def _extend(prev, tile, nxt, has_prev, has_next):
    w = tile.shape[1]
    prev = jnp.where(has_prev, prev, 0.0)
    nxt = jnp.where(has_next, nxt, 0.0)
    return jnp.concatenate([prev, tile, nxt, jnp.zeros((128 - 2 * POOL_HALO, w), F32)], axis=0)


def _pool_specs(tm, d, n_rows):
    last8 = n_rows // POOL_HALO - 1
    per = tm // POOL_HALO
    return [pl.BlockSpec((tm, d), lambda i: (i, 0)),
            pl.BlockSpec((POOL_HALO, d), lambda i: (jnp.maximum(i * per - 1, 0), 0)),
            pl.BlockSpec((POOL_HALO, d), lambda i: (jnp.minimum((i + 1) * per, last8), 0))]


def _pool_fwd(h, mods, ng, w, scale, bands, nct, tm, c_len, l_len):
    r, d = h.shape
    gw = d // POOL_GROUPS
    n_tiles = r // tm
    kx = tm + 128

    def body(h_ref, hp_ref, hn_ref, mod_ref, ng_ref, w_ref, sc_ref, band_ref, y_ref, h1_ref):
        i = pl.program_id(0)
        pos0, ls, has_prev, has_next = _pool_geometry(i, nct, n_tiles, tm, c_len, l_len)
        ngv, sh, sc = ng_ref[...], mod_ref[0:1, :], mod_ref[1:2, :]
        h = h_ref[...]
        a = _normmod(h, ngv, sh, sc)[0]
        a_ext = _extend(_normmod(hp_ref[...], ngv, sh, sc)[0], a, _normmod(hn_ref[...], ngv, sh, sc)[0],
                        has_prev, has_next)
        pos = pos0 + lax.broadcasted_iota(jnp.int32, (tm, 1), 0)
        inv = _window_inv_counts(pos, ls)
        ys = []
        for g in range(POOL_GROUPS):
            cols = slice(g * gw, (g + 1) * gw)
            hi, lo = _split_bf16(a_ext[:, cols])
            s = _dot(band_ref[g], hi) + _dot(band_ref[g], lo)
            pg = s * inv[g] - a[:, cols]
            ys.append(_dot(pg.astype(BF16), w_ref[g]))
        y = jnp.concatenate(ys, axis=1) * sc_ref[...]
        y_ref[...] = y.astype(BF16)
        h1_ref[...] = h + mod_ref[2:3, :] * y

    return pl.pallas_call(
        body, name="pool_fwd", grid=(n_tiles,),
        out_shape=[jax.ShapeDtypeStruct((r, d), BF16), jax.ShapeDtypeStruct((r, d), F32)],
        in_specs=_pool_specs(tm, d, r) + [
            pl.BlockSpec((None, 8, d), lambda i: (_sid(i, nct), 0, 0)),
            _const_spec((1, d)), _const_spec(w.shape), _const_spec((1, d)), _const_spec((4, tm, kx))],
        out_specs=[pl.BlockSpec((tm, d), lambda i: (i, 0)), pl.BlockSpec((tm, d), lambda i: (i, 0))],
        compiler_params=_params(("arbitrary",)),
    )(h, h, h, mods, ng, w, scale, bands[0])


def _pool_bwd(dh1, h, y, mods, ng, w, scale, bands, nct, tm, c_len, l_len, latent_out, rider=None):
    r, d = h.shape
    gw = d // POOL_GROUPS
    n_tiles = r // tm
    kx = tm + 128
    out_rows = l_len if latent_out else r
    out_off = nct if latent_out else 0

    def body(dh_ref, dhp_ref, dhn_ref, h_ref, hp_ref, hn_ref, y_ref, mod_ref, ng_ref, w_ref, sc_ref,
             bf_ref, bb_ref, dho_ref, dw_ref, st_ref):
        i = pl.program_id(0)
        pos0, ls, has_prev, has_next = _pool_geometry(i, nct, n_tiles, tm, c_len, l_len)
        ngv, sh, sc, gate = ng_ref[...], mod_ref[0:1, :], mod_ref[1:2, :], mod_ref[2:3, :]
        scale_v = sc_ref[...]
        h = h_ref[...]
        a, xhat, rstd, n = _normmod(h, ngv, sh, sc)
        a_ext = _extend(_normmod(hp_ref[...], ngv, sh, sc)[0], a, _normmod(hn_ref[...], ngv, sh, sc)[0],
                        has_prev, has_next)
        dh = dh_ref[...]
        dgate = _colsum(dh * y_ref[...].astype(F32))
        dy = gate * dh
        dy_ext = _extend(gate * dhp_ref[...], dy, gate * dhn_ref[...], has_prev, has_next)
        dyp_ext = (dy_ext * scale_v).astype(BF16)
        dyp = (dy * scale_v).astype(BF16)
        pos = pos0 + lax.broadcasted_iota(jnp.int32, (tm, 1), 0)
        inv = _window_inv_counts(pos, ls)
        pos_e = pos0 - POOL_HALO + lax.broadcasted_iota(jnp.int32, (kx, 1), 0)
        inv_e = _window_inv_counts(pos_e, ls)

        @pl.when(i == 0)
        def _():
            dw_ref[...] = jnp.zeros_like(dw_ref)
        das, dscale = [], []
        for g in range(POOL_GROUPS):
            cols = slice(g * gw, (g + 1) * gw)
            hi, lo = _split_bf16(a_ext[:, cols])
            pg = ((_dot(bf_ref[g], hi) + _dot(bf_ref[g], lo)) * inv[g] - a[:, cols]).astype(BF16)
            dscale.append(_colsum(dy[:, cols] * _dot(pg, w_ref[g])))
            dyp_g = dyp_ext[:, cols]
            dw_ref[g] += _dot_tn(pg, dyp[:, cols])
            dp_ext = _dot_nt(dyp_g, w_ref[g])
            hi, lo = _split_bf16(dp_ext * inv_e[g])
            das.append(_dot(bb_ref[g], hi) + _dot(bb_ref[g], lo) - dp_ext[POOL_HALO:POOL_HALO + tm, :])
        da = jnp.concatenate(das, axis=1)
        dhn, dsh, dsc, dng = _normmod_bwd(da, xhat, rstd, n, ngv, sc)
        dho_ref[...] = dh + dhn
        _acc_rows(st_ref, _first_of_stream(i, nct), [dsh, dsc, dgate, dng, jnp.concatenate(dscale, axis=1)])

    outs, rode = _compute_call(
        body, name="pool_bwd", grid=(n_tiles,), rider=rider,
        operands=(dh1, dh1, dh1, h, h, h, y, mods, ng, w, scale, bands[0], bands[1]),
        out_shape=[jax.ShapeDtypeStruct((out_rows, d), F32),
                   jax.ShapeDtypeStruct((POOL_GROUPS, gw, gw), F32),
                   jax.ShapeDtypeStruct((_n_streams(nct), 8, d), F32)],
        in_specs=_pool_specs(tm, d, r) + _pool_specs(tm, d, r) + [
            pl.BlockSpec((tm, d), lambda i: (i, 0)),
            pl.BlockSpec((None, 8, d), lambda i: (_sid(i, nct), 0, 0)),
            _const_spec((1, d)), _const_spec(w.shape), _const_spec((1, d)),
            _const_spec((4, tm, kx)), _const_spec((4, tm, kx))],
        out_specs=[pl.BlockSpec((tm, d), lambda i: (jnp.maximum(i - out_off, 0), 0)),
                   pl.BlockSpec((POOL_GROUPS, gw, gw), lambda i: (0, 0, 0)),
                   pl.BlockSpec((None, 8, d), lambda i: (_stat_sid(i, nct), 0, 0))])
    return (*outs, rode)


def _rope_tables(c_len, l_len):
    half = HEAD_DIM // 2
    t = np.arange(l_len)
    row = (t // GRID_W).astype(np.float32)
    col = (t % GRID_W).astype(np.float32)
    inv = (np.float32(ROPE_BASE) ** (-np.arange(0, half, 2, dtype=np.float32) / np.float32(half))).astype(np.float32)
    ang_r = row[:, None] * inv[None, :]
    ang_c = col[:, None] * inv[None, :]
    cos = np.concatenate([np.cos(ang_r), np.cos(ang_r), np.cos(ang_c), np.cos(ang_c)], axis=1)
    sin = np.concatenate([-np.sin(ang_r), np.sin(ang_r), -np.sin(ang_c), np.sin(ang_c)], axis=1)
    cos = np.concatenate([np.ones((c_len, HEAD_DIM), np.float32), cos.astype(np.float32)], axis=0)
    sin = np.concatenate([np.zeros((c_len, HEAD_DIM), np.float32), sin.astype(np.float32)], axis=0)
    return jnp.asarray(cos, F32), jnp.asarray(sin, F32)


def _swap_pairs(x):
    lane = lax.broadcasted_iota(jnp.int32, x.shape, 1)
    return jnp.where((lane % 64) < 32, pltpu.roll(x, 96, 1), pltpu.roll(x, 32, 1))


def _head_norm(x, g):
    rstd = lax.rsqrt(jnp.mean(x * x, axis=-1, keepdims=True) + EPS)
    xhat = x * rstd
    return xhat * g, xhat, rstd


def _qkv_fwd(h, mods, ng, w, qg, kg, cos, sin, n_heads, n_kv, nct, tm):
    t_rows, d = h.shape
    qw, kw = n_heads * HEAD_DIM, n_kv * HEAD_DIM

    def body(h_ref, mod_ref, ng_ref, w_ref, qg_ref, kg_ref, cos_ref, sin_ref, q_ref, k_ref, v_ref):
        a = _normmod(h_ref[...], ng_ref[...], mod_ref[0:1, :], mod_ref[1:2, :])[0]
        qkv = _dot(a.astype(BF16), w_ref[...])
        cosv, sinv = cos_ref[...], sin_ref[...]
        ones = jnp.ones((tm, HEAD_DIM), BF16)
        for hd in range(n_heads + n_kv):
            cols = slice(hd * HEAD_DIM, (hd + 1) * HEAD_DIM)
            xn = _head_norm(qkv[:, cols], qg_ref[...] if hd < n_heads else kg_ref[...])[0]
            xr = xn * cosv + _swap_pairs(xn) * sinv
            if hd < n_heads:
                q_ref[:, cols] = (xr * Q_SCALE).astype(BF16)
            else:
                k_ref[:, (hd - n_heads) * HEAD_DIM:(hd - n_heads + 1) * HEAD_DIM] = xr.astype(BF16)
        for g in range(n_kv):
            v_ref[:, (2 * g) * HEAD_DIM:(2 * g + 1) * HEAD_DIM] = (
                qkv[:, qw + kw + g * HEAD_DIM:qw + kw + (g + 1) * HEAD_DIM].astype(BF16))
            v_ref[:, (2 * g + 1) * HEAD_DIM:(2 * g + 2) * HEAD_DIM] = ones

    return pl.pallas_call(
        body, name="qkv_fwd", grid=(t_rows // tm,),
        out_shape=[jax.ShapeDtypeStruct((t_rows - nct * tm, qw), BF16), jax.ShapeDtypeStruct((t_rows, kw), BF16),
                   jax.ShapeDtypeStruct((t_rows, 2 * kw), BF16)],
        in_specs=[pl.BlockSpec((tm, d), lambda i: (i, 0)),
                  pl.BlockSpec((None, 8, d), lambda i: (_sid(i, nct), 0, 0)),
                  _const_spec((1, d)), _const_spec(w.shape), _const_spec((1, HEAD_DIM)),
                  _const_spec((1, HEAD_DIM)),
                  pl.BlockSpec((tm, HEAD_DIM), lambda i: (i, 0)), pl.BlockSpec((tm, HEAD_DIM), lambda i: (i, 0))],
        out_specs=[pl.BlockSpec((tm, qw), lambda i: (jnp.maximum(i - nct, 0), 0)),
                   pl.BlockSpec((tm, kw), lambda i: (i, 0)), pl.BlockSpec((tm, 2 * kw), lambda i: (i, 0))],
        compiler_params=_params(("arbitrary",)),
    )(h, mods, ng, w, qg, kg, cos, sin)


def _flash_tk(t_rows, tm):
    best = tm
    k = tm
    while k <= FLASH_TK_CAP:
        if t_rows % k == 0:
            best = k
        k += tm
    return best


def _flash_fwd(q, k, v1, n_kv, tq, tm, rider=None):
    t_rows = k.shape[0]
    l_rows = q.shape[0]
    tk = _flash_tk(t_rows, tm)
    nk = t_rows // tk
    gq = 2 * HEAD_DIM

    def body(q_ref, k_ref, v_ref, o_ref, lse_ref, m_s, acc_s):
        ki = pl.program_id(2)

        @pl.when(ki == 0)
        def _():
            m_s[...] = jnp.full_like(m_s, -jnp.inf)
            acc_s[...] = jnp.zeros_like(acc_s)
        kk, vv = k_ref[...], v_ref[...]
        for hh in range(2):
            s = _dot_nt(q_ref[:, hh * HEAD_DIM:(hh + 1) * HEAD_DIM], kk)
            m_prev = m_s[hh]
            m_new = jnp.maximum(m_prev, jnp.max(s, axis=-1, keepdims=True))
            alpha = jnp.exp2(m_prev - m_new)
            p = jnp.exp2(s - jnp.tile(m_new, (1, tk // HEAD_DIM)))
            acc_s[hh] = jnp.tile(alpha, (1, 2)) * acc_s[hh] + _dot(p.astype(BF16), vv)
            m_s[hh] = m_new

        @pl.when(ki == nk - 1)
        def _():
            for hh in range(2):
                acc = acc_s[hh]
                l = acc[:, HEAD_DIM:]
                o_ref[:, hh * HEAD_DIM:(hh + 1) * HEAD_DIM] = (acc[:, :HEAD_DIM] / l).astype(BF16)
                lse_ref[:, hh:hh + 1] = (m_s[hh] + jnp.log2(l))[:, 0:1]

    (o, lse), rode = _compute_call(
        body, name="flash_fwd", grid=(n_kv, l_rows // tq, nk), operands=(q, k, v1), rider=rider,
        out_shape=[jax.ShapeDtypeStruct((l_rows, n_kv * gq), BF16),
                   jax.ShapeDtypeStruct((n_kv, l_rows, 2), F32)],
        in_specs=[pl.BlockSpec((tq, gq), lambda g, i, j: (i, g)),
                  pl.BlockSpec((tk, HEAD_DIM), lambda g, i, j: (j, g)),
                  pl.BlockSpec((tk, gq), lambda g, i, j: (j, g))],
        out_specs=[pl.BlockSpec((tq, gq), lambda g, i, j: (i, g)),
                   pl.BlockSpec((None, tq, 2), lambda g, i, j: (g, i, 0))],
        scratch_shapes=[pltpu.VMEM((2, tq, HEAD_DIM), F32), pltpu.VMEM((2, tq, gq), F32)])
    return o, lse, rode


def _flash_bwd(q, k, v1, do, lse, delta, n_kv, tq, tm, rider=None):
    t_rows = k.shape[0]
    l_rows = q.shape[0]
    tk = _flash_tk(t_rows, tm)
    nq = l_rows // tq
    gq = 2 * HEAD_DIM

    def body(q_ref, k_ref, v_ref, do_ref, lse_ref, dl_ref, dq_ref, dk_ref, dv_ref):
        ki, qi = pl.program_id(1), pl.program_id(2)
        rows = pl.ds(pl.multiple_of(qi * tq, tq), tq)

        @pl.when(qi == 0)
        def _():
            dk_ref[...] = jnp.zeros_like(dk_ref)
            dv_ref[...] = jnp.zeros_like(dv_ref)

        @pl.when(ki == 0)
        def _():
            dq_ref[rows, :] = jnp.zeros((tq, gq), F32)
        kk, vv = k_ref[...], v_ref[:, :HEAD_DIM]
        dk_parts, dv_parts = [], []
        for hh in range(2):
            cols = slice(hh * HEAD_DIM, (hh + 1) * HEAD_DIM)
            qh, doh = q_ref[:, cols], do_ref[:, cols]
            p = jnp.exp2(_dot_nt(qh, kk) - lse_ref[:, hh:hh + 1])
            ds = (p * (_dot_nt(doh, vv) - dl_ref[:, hh:hh + 1])).astype(BF16)
            dv_parts.append(_dot_tn(p.astype(BF16), doh))
            dk_parts.append(_dot_tn(ds, qh))
            dq_ref[rows, cols] += _dot(ds, kk)
        dv_ref[...] += dv_parts[0] + dv_parts[1]
        dk_ref[...] += dk_parts[0] + dk_parts[1]

    (dq, dk, dv), rode = _compute_call(
        body, name="flash_bwd", grid=(n_kv, t_rows // tk, nq), operands=(q, k, v1, do, lse, delta), rider=rider,
        out_shape=[jax.ShapeDtypeStruct((l_rows, n_kv * gq), F32),
                   jax.ShapeDtypeStruct((t_rows, n_kv * HEAD_DIM), F32),
                   jax.ShapeDtypeStruct((t_rows, n_kv * HEAD_DIM), F32)],
        in_specs=[pl.BlockSpec((tq, gq), lambda g, j, i: (i, g)),
                  pl.BlockSpec((tk, HEAD_DIM), lambda g, j, i: (j, g)),
                  pl.BlockSpec((tk, gq), lambda g, j, i: (j, g)),
                  pl.BlockSpec((tq, gq), lambda g, j, i: (i, g)),
                  pl.BlockSpec((None, tq, 2), lambda g, j, i: (g, i, 0)),
                  pl.BlockSpec((None, tq, 2), lambda g, j, i: (g, i, 0))],
        out_specs=[pl.BlockSpec((l_rows, gq), lambda g, j, i: (0, g)),
                   pl.BlockSpec((tk, HEAD_DIM), lambda g, j, i: (j, g)),
                   pl.BlockSpec((tk, HEAD_DIM), lambda g, j, i: (j, g))])
    return dq, dk, dv, rode


def _wo_fwd(h, o, wo, mods, nct, tm):
    l_rows, z = o.shape
    d = h.shape[1]

    def body(h_ref, o_ref, w_ref, mod_ref, y_ref, h1_ref):
        y = _dot(o_ref[...], w_ref[...])
        y_ref[...] = y.astype(BF16)
        h1_ref[...] = h_ref[...] + mod_ref[2:3, :] * y

    return pl.pallas_call(
        body, name="wo_fwd", grid=(l_rows // tm,),
        out_shape=[jax.ShapeDtypeStruct((l_rows, d), BF16), jax.ShapeDtypeStruct((l_rows, d), F32)],
        in_specs=[pl.BlockSpec((tm, d), lambda i: (i + nct, 0)), pl.BlockSpec((tm, z), lambda i: (i, 0)),
                  _const_spec(wo.shape), pl.BlockSpec((None, 8, d), lambda i: (1, 0, 0))],
        out_specs=[pl.BlockSpec((tm, d), lambda i: (i, 0)), pl.BlockSpec((tm, d), lambda i: (i, 0))],
        compiler_params=_params(("arbitrary",)),
    )(h, o, wo, mods)


def _wo_bwd(dh1, y, o, wo, mods, n_kv, tm):
    l_rows, z = o.shape
    d = dh1.shape[1]

    def body(dh_ref, y_ref, o_ref, w_ref, mod_ref, dy_ref, do_ref, dl_ref, st_ref):
        i = pl.program_id(0)
        dh = dh_ref[...]
        dgate = _colsum(dh * y_ref[...].astype(F32))
        dy = (mod_ref[2:3, :] * dh).astype(BF16)
        dy_ref[...] = dy
        do = _dot_nt(dy, w_ref[...])
        do_ref[...] = do.astype(BF16)
        prod = do * o_ref[...].astype(F32)
        for g in range(n_kv):
            d0 = jnp.sum(prod[:, (2 * g) * HEAD_DIM:(2 * g + 1) * HEAD_DIM], axis=-1, keepdims=True)
            d1 = jnp.sum(prod[:, (2 * g + 1) * HEAD_DIM:(2 * g + 2) * HEAD_DIM], axis=-1, keepdims=True)
            dl_ref[g] = jnp.concatenate([d0, d1], axis=1)
        zero = jnp.zeros((1, d), F32)
        _acc_rows(st_ref, i == 0, [zero, zero, dgate])

    return pl.pallas_call(
        body, name="wo_bwd", grid=(l_rows // tm,),
        out_shape=[jax.ShapeDtypeStruct((l_rows, d), BF16), jax.ShapeDtypeStruct((l_rows, z), BF16),
                   jax.ShapeDtypeStruct((n_kv, l_rows, 2), F32), jax.ShapeDtypeStruct((8, d), F32)],
        in_specs=[pl.BlockSpec((tm, d), lambda i: (i, 0)), pl.BlockSpec((tm, d), lambda i: (i, 0)),
                  pl.BlockSpec((tm, z), lambda i: (i, 0)), _const_spec(wo.shape),
                  pl.BlockSpec((None, 8, d), lambda i: (1, 0, 0))],
        out_specs=[pl.BlockSpec((tm, d), lambda i: (i, 0)), pl.BlockSpec((tm, z), lambda i: (i, 0)),
                   pl.BlockSpec((n_kv, tm, 2), lambda i: (0, i, 0)), pl.BlockSpec((8, d), lambda i: (0, 0))],
        compiler_params=_params(("arbitrary",)),
    )(dh1, y, o, wo, mods)


def _qkv_bwd(h, dh_lat, dq, dk, dv, mods, ng, w, qg, kg, cos, sin, n_heads, n_kv, nct, tm):
    t_rows, d = h.shape
    qw, kw = n_heads * HEAD_DIM, n_kv * HEAD_DIM
    scale = HEAD_DIM ** -0.5

    def body(h_ref, dhl_ref, dq_ref, dk_ref, dv_ref, mod_ref, ng_ref, w_ref, qg_ref, kg_ref, cos_ref,
             sin_ref, dh_ref, a_ref, dqkv_ref, st_ref, dg_ref):
        i = pl.program_id(0)
        lat = (i >= nct).astype(F32)
        ngv, sc = ng_ref[...], mod_ref[1:2, :]
        a, xhat, rstd, n = _normmod(h_ref[...], ngv, mod_ref[0:1, :], sc)
        ab = a.astype(BF16)
        a_ref[...] = ab
        qkv = _dot(ab, w_ref[...])
        cosv, sinv = cos_ref[...], sin_ref[...]
        dqg = jnp.zeros((1, HEAD_DIM), F32)
        dkg = jnp.zeros((1, HEAD_DIM), F32)
        for hd in range(n_heads + n_kv):
            cols = slice(hd * HEAD_DIM, (hd + 1) * HEAD_DIM)
            is_q = hd < n_heads
            g = qg_ref[...] if is_q else kg_ref[...]
            _, hx, hr = _head_norm(qkv[:, cols], g)
            if is_q:
                dxr = dq_ref[:, cols] * (scale * lat)
            else:
                dxr = dk_ref[:, (hd - n_heads) * HEAD_DIM:(hd - n_heads + 1) * HEAD_DIM] * (1.0 / LOG2E)
            dxn = dxr * cosv + _swap_pairs(dxr * sinv)
            if is_q:
                dqg = dqg + _colsum(dxn * hx)
            else:
                dkg = dkg + _colsum(dxn * hx)
            dxh = dxn * g
            dx = hr * (dxh - hx * jnp.mean(dxh * hx, axis=-1, keepdims=True))
            dqkv_ref[:, cols] = dx.astype(BF16)
        dqkv_ref[:, qw + kw:] = dv_ref[...].astype(BF16)
        da = _dot_nt(dqkv_ref[...], w_ref[...])
        dhn, dsh, dsc, dng = _normmod_bwd(da, xhat, rstd, n, ngv, sc)
        dh_ref[...] = dhl_ref[...] * lat + dhn
        _acc_rows(st_ref, _first_of_stream(i, nct), [dsh, dsc, jnp.zeros((1, d), F32), dng])
        _acc_rows(dg_ref, i == 0, [dqg, dkg])

    lat_map = lambda i: (jnp.maximum(i - nct, 0), 0)
    return pl.pallas_call(
        body, name="qkv_bwd", grid=(t_rows // tm,),
        out_shape=[jax.ShapeDtypeStruct((t_rows, d), F32), jax.ShapeDtypeStruct((t_rows, d), BF16),
                   jax.ShapeDtypeStruct((t_rows, qw + 2 * kw), BF16), jax.ShapeDtypeStruct((2, 8, d), F32),
                   jax.ShapeDtypeStruct((8, HEAD_DIM), F32)],
        in_specs=[pl.BlockSpec((tm, d), lambda i: (i, 0)), pl.BlockSpec((tm, d), lat_map),
                  pl.BlockSpec((tm, qw), lat_map), pl.BlockSpec((tm, kw), lambda i: (i, 0)),
                  pl.BlockSpec((tm, kw), lambda i: (i, 0)),
                  pl.BlockSpec((None, 8, d), lambda i: (_sid(i, nct), 0, 0)),
                  _const_spec((1, d)), _const_spec(w.shape), _const_spec((1, HEAD_DIM)),
                  _const_spec((1, HEAD_DIM)),
                  pl.BlockSpec((tm, HEAD_DIM), lambda i: (i, 0)), pl.BlockSpec((tm, HEAD_DIM), lambda i: (i, 0))],
        out_specs=[pl.BlockSpec((tm, d), lambda i: (i, 0)), pl.BlockSpec((tm, d), lambda i: (i, 0)),
                   pl.BlockSpec((tm, qw + 2 * kw), lambda i: (i, 0)),
                   pl.BlockSpec((None, 8, d), lambda i: (_sid(i, nct), 0, 0)),
                   pl.BlockSpec((8, HEAD_DIM), lambda i: (0, 0))],
        compiler_params=_params(("arbitrary",)),
    )(h, dh_lat, dq, dk, dv, mods, ng, w, qg, kg, cos, sin)


def _gmlp_core(a_bf, win_ref, lng, lnb, ws_ref, bst_ref, tm, half, with_grad=False):
    z = _dot(a_bf, win_ref[...])
    zu, zv = z[:, :half], z[:, half:]
    if with_grad:
        (u, zu), (v, zv) = _gelu_and_grad(zu), _gelu_and_grad(zv)
    else:
        u, v, zu, zv = _gelu(zu), _gelu(zv), None, None
    mu = jnp.mean(v, axis=-1, keepdims=True)
    vc = v - mu
    rstd_v = lax.rsqrt(jnp.mean(vc * vc, axis=-1, keepdims=True) + EPS)
    vhat = vc * rstd_v
    vln = (vhat * lng + lnb).astype(BF16)
    gw = half // GMLP_GROUPS
    rows = []
    for ch in range(tm // CHUNK):
        rs = slice(ch * CHUNK, (ch + 1) * CHUNK)
        cols = []
        for g in range(GMLP_GROUPS):
            cs = slice(g * gw, (g + 1) * gw)
            cols.append(_dot(ws_ref[g], vln[rs, cs]) + bst_ref[:, g:g + 1])
        rows.append(jnp.concatenate(cols, axis=1))
    sv = rows[0] if len(rows) == 1 else jnp.concatenate(rows, axis=0)
    return zu, zv, u, vhat, rstd_v, vln, sv


def _gmlp_fwd(h, mods, ng, win, lng, lnb, ws, bst, wout, tm):
    l_rows, d = h.shape
    half = wout.shape[0]

    def body(h_ref, mod_ref, ng_ref, win_ref, lng_ref, lnb_ref, ws_ref, bst_ref, wout_ref, y_ref, h1_ref):
        hv = h_ref[...]
        a = _normmod(hv, ng_ref[...], mod_ref[0:1, :], mod_ref[1:2, :])[0]
        _, _, u, _, _, _, sv = _gmlp_core(a.astype(BF16), win_ref, lng_ref[...], lnb_ref[...], ws_ref,
                                          bst_ref, tm, half)
        y = _dot((u * sv).astype(BF16), wout_ref[...])
        y_ref[...] = y.astype(BF16)
        h1_ref[...] = hv + mod_ref[2:3, :] * y

    return pl.pallas_call(
        body, name="gmlp_fwd", grid=(l_rows // tm,),
        out_shape=[jax.ShapeDtypeStruct((l_rows, d), BF16), jax.ShapeDtypeStruct((l_rows, d), F32)],
        in_specs=[pl.BlockSpec((tm, d), lambda i: (i, 0)), pl.BlockSpec((None, 8, d), lambda i: (1, 0, 0)),
                  _const_spec((1, d)), _const_spec(win.shape), _const_spec((1, half)), _const_spec((1, half)),
                  _const_spec(ws.shape), _const_spec(bst.shape), _const_spec(wout.shape)],
        out_specs=[pl.BlockSpec((tm, d), lambda i: (i, 0)), pl.BlockSpec((tm, d), lambda i: (i, 0))],
        compiler_params=_params(("arbitrary",)),
    )(h, mods, ng, win, lng, lnb, ws, bst, wout)


def _gmlp_bwd(dh1, h, y, mods, ng, win, lng, lnb, ws, wst, bst, wout, tm):
    l_rows, d = h.shape
    half = wout.shape[0]
    gw = half // GMLP_GROUPS

    def body(dh_ref, h_ref, y_ref, mod_ref, ng_ref, win_ref, lng_ref, lnb_ref, ws_ref, wst_ref, bst_ref,
             wout_ref, dho_ref, a_ref, dz_ref, gt_ref, dy_ref, st_ref, ln_ref, dws_ref, dbs_ref):
        i = pl.program_id(0)
        ngv, sc = ng_ref[...], mod_ref[1:2, :]
        lngv = lng_ref[...]
        a, xhat, rstd, n = _normmod(h_ref[...], ngv, mod_ref[0:1, :], sc)
        ab = a.astype(BF16)
        a_ref[...] = ab
        gu, gv, u, vhat, rstd_v, vln, sv = _gmlp_core(ab, win_ref, lngv, lnb_ref[...], ws_ref, bst_ref,
                                                      tm, half, with_grad=True)
        gt_ref[...] = (u * sv).astype(BF16)
        dh = dh_ref[...]
        dgate = _colsum(dh * y_ref[...].astype(F32))
        dy = (mod_ref[2:3, :] * dh).astype(BF16)
        dy_ref[...] = dy
        dgated = _dot_nt(dy, wout_ref[...])
        du = dgated * sv
        dsv = (dgated * u).astype(BF16)

        @pl.when(i == 0)
        def _():
            dws_ref[...] = jnp.zeros_like(dws_ref)
            dbs_ref[...] = jnp.zeros_like(dbs_ref)
        lane = lax.broadcasted_iota(jnp.int32, (CHUNK, 128), 1)
        dbs = jnp.zeros((CHUNK, 128), F32)
        rows = []
        for ch in range(tm // CHUNK):
            rs = slice(ch * CHUNK, (ch + 1) * CHUNK)
            cols = []
            for g in range(GMLP_GROUPS):
                cs = slice(g * gw, (g + 1) * gw)
                dsv_cg = dsv[rs, cs]
                dws_ref[g] += _dot_nt(dsv_cg, vln[rs, cs])
                cols.append(_dot(wst_ref[g], dsv_cg))
                dbs = dbs + jnp.where(lane == g, jnp.sum(dsv_cg.astype(F32), axis=-1, keepdims=True), 0.0)
            rows.append(jnp.concatenate(cols, axis=1))
        dbs_ref[...] += dbs
        dvln = rows[0] if len(rows) == 1 else jnp.concatenate(rows, axis=0)
        dlng = _colsum(dvln * vhat)
        dlnb = _colsum(dvln)
        dvh = dvln * lngv
        dv = rstd_v * (dvh - jnp.mean(dvh, axis=-1, keepdims=True)
                       - vhat * jnp.mean(dvh * vhat, axis=-1, keepdims=True))
        dz_ref[:, :half] = (du * gu).astype(BF16)
        dz_ref[:, half:] = (dv * gv).astype(BF16)
        da = _dot_nt(dz_ref[...], win_ref[...])
        dhn, dsh, dsc, dng = _normmod_bwd(da, xhat, rstd, n, ngv, sc)
        dho_ref[...] = dh + dhn
        _acc_rows(st_ref, i == 0, [dsh, dsc, dgate, dng])
        _acc_rows(ln_ref, i == 0, [dlng, dlnb])

    row = lambda w: pl.BlockSpec((tm, w), lambda i: (i, 0))
    return pl.pallas_call(
        body, name="gmlp_bwd", grid=(l_rows // tm,),
        out_shape=[jax.ShapeDtypeStruct((l_rows, d), F32), jax.ShapeDtypeStruct((l_rows, d), BF16),
                   jax.ShapeDtypeStruct((l_rows, 2 * half), BF16), jax.ShapeDtypeStruct((l_rows, half), BF16),
                   jax.ShapeDtypeStruct((l_rows, d), BF16), jax.ShapeDtypeStruct((8, d), F32),
                   jax.ShapeDtypeStruct((8, half), F32), jax.ShapeDtypeStruct(ws.shape, F32),
                   jax.ShapeDtypeStruct((CHUNK, 128), F32)],
        in_specs=[row(d), row(d), row(d), pl.BlockSpec((None, 8, d), lambda i: (1, 0, 0)),
                  _const_spec((1, d)), _const_spec(win.shape), _const_spec((1, half)), _const_spec((1, half)),
                  _const_spec(ws.shape), _const_spec(ws.shape), _const_spec(bst.shape), _const_spec(wout.shape)],
        out_specs=[row(d), row(d), row(2 * half), row(half), row(d),
                   pl.BlockSpec((8, d), lambda i: (0, 0)), pl.BlockSpec((8, half), lambda i: (0, 0)),
                   pl.BlockSpec(ws.shape, lambda i: (0, 0, 0)), pl.BlockSpec((CHUNK, 128), lambda i: (0, 0))],
        compiler_params=_params(("arbitrary",)),
    )(dh1, h, y, mods, ng, win, lng, lnb, ws, wst, bst, wout)


def _head(h, final_g, target, tm):
    l_rows, d = h.shape
    n_tiles = l_rows // tm

    def body(h_ref, g_ref, t_ref, dh_ref, loss_ref, dg_ref, acc_ref):
        i = pl.program_id(0)
        g = g_ref[...]
        hv = h_ref[...]
        rstd = lax.rsqrt(jnp.mean(hv * hv, axis=-1, keepdims=True) + EPS)
        xhat = hv * rstd
        e = xhat * g - t_ref[...]
        dout = e * (1.0 / d)
        dxhat = dout * g
        dh_ref[...] = rstd * (dxhat - xhat * jnp.mean(dxhat * xhat, axis=-1, keepdims=True))
        _acc_rows(dg_ref, i == 0, [_colsum(dout * xhat)])
        _acc_rows(acc_ref, i == 0, [_colsum(e * e)])

        @pl.when(i == n_tiles - 1)
        def _():
            total = jnp.sum(acc_ref[0:1, :], axis=-1, keepdims=True) * (0.5 / d)
            loss_ref[...] = jnp.broadcast_to(total, loss_ref.shape)

    return pl.pallas_call(
        body, name="loss_head", grid=(n_tiles,),
        out_shape=[jax.ShapeDtypeStruct((l_rows, d), F32), jax.ShapeDtypeStruct((8, 128), F32),
                   jax.ShapeDtypeStruct((8, d), F32)],
        in_specs=[pl.BlockSpec((tm, d), lambda i: (i, 0)), _const_spec((1, d)),
                  pl.BlockSpec((tm, d), lambda i: (i, 0))],
        out_specs=[pl.BlockSpec((tm, d), lambda i: (i, 0)), pl.BlockSpec((8, 128), lambda i: (0, 0)),
                   pl.BlockSpec((8, d), lambda i: (0, 0))],
        scratch_shapes=[pltpu.VMEM((8, d), F32)],
        compiler_params=_params(("arbitrary",)),
    )(h, final_g, target)


def _adamw(w, gparts, m, v, name):
    shape = w.shape
    cols = shape[-1]
    rows = int(np.prod(shape[:-1])) if len(shape) > 1 else 1
    pieces = list(gparts) if isinstance(gparts, (list, tuple)) else [gparts]
    n_pieces = len(pieces)
    nparts = pieces[0].shape[0]
    piece_rows = rows // n_pieces
    w2, m2, v2 = (t.reshape(rows, cols) for t in (w, m, v))
    pieces = [g.reshape(nparts, piece_rows, cols) for g in pieces]
    tr = piece_rows
    part_bytes = nparts * cols * pieces[0].dtype.itemsize
    for cand in (1024, 512, 256, 128, 64, 32, 16, 8):
        if piece_rows * max(part_bytes, cols * 4) <= (2 << 20):
            break
        if piece_rows % cand == 0 and cand < piece_rows:
            tr = cand
            if cand * max(part_bytes, cols * 4) <= (2 << 20):
                break
    per_piece = piece_rows // tr
    c1 = 1.0 - ADAM_B1 ** ADAM_STEP
    c2 = 1.0 - ADAM_B2 ** ADAM_STEP

    def update(w_ref, g_ref, m_ref, v_ref, go_ref, d_ref, mo_ref, vo_ref):
        g = g_ref[0].astype(F32)
        for k in range(1, nparts):
            g = g + g_ref[k].astype(F32)
        mn = ADAM_B1 * m_ref[...] + (1.0 - ADAM_B1) * g
        vn = ADAM_B2 * v_ref[...] + (1.0 - ADAM_B2) * (g * g)
        go_ref[...] = g
        mo_ref[...] = mn
        vo_ref[...] = vn
        d_ref[...] = -ADAM_LR * ((mn / c1) / (jnp.sqrt(vn / c2) + ADAM_EPS) + ADAM_WD * w_ref[...])

    def body(w_ref, *refs):
        g_refs, (m_ref, v_ref, go_ref, d_ref, mo_ref, vo_ref) = refs[:n_pieces], refs[n_pieces:]
        if n_pieces == 1:
            update(w_ref, g_refs[0], m_ref, v_ref, go_ref, d_ref, mo_ref, vo_ref)
        else:
            piece = pl.program_id(0) // per_piece
            for k in range(n_pieces):
                pl.when(piece == k)(functools.partial(update, w_ref, g_refs[k], m_ref, v_ref, go_ref, d_ref, mo_ref, vo_ref))

    def piece_spec(k):
        return pl.BlockSpec((nparts, tr, cols), lambda i: (0, jnp.clip(i - k * per_piece, 0, per_piece - 1), 0))

    spec = pl.BlockSpec((tr, cols), lambda i: (i, 0))
    outs = pl.pallas_call(
        body, name=name, grid=(rows // tr,),
        out_shape=[jax.ShapeDtypeStruct((rows, cols), F32)] * 4,
        in_specs=[spec] + [piece_spec(k) for k in range(n_pieces)] + [spec, spec],
        out_specs=[spec] * 4,
        compiler_params=_params(("arbitrary",)),
    )(w2, *pieces, m2, v2)
    return tuple(o.reshape(shape) for o in outs)


def _natural_cols(g):
    return jnp.moveaxis(g, 0, -2).reshape(g.shape[1:-1] + (N_DEV * g.shape[-1],))


def _natural_rows(g):
    return jnp.moveaxis(g, 0, -3).reshape(g.shape[1:-2] + (N_DEV * g.shape[-2], g.shape[-1]))


def _shard_rows(full):
    r = full.shape[-2] // N_DEV
    return jnp.moveaxis(full.reshape(full.shape[:-2] + (N_DEV, r, full.shape[-1])), -3, 0)


def _my_cols(gathered, me, n):
    return lax.dynamic_slice_in_dim(gathered, me * n, n, axis=gathered.ndim - 1)


def kernel(x, c, ctx, c_ctx, ada_w, ada_b, norm_g, mlp_w1, mlp_w2, pool_w, pool_scale, attn_w_qkv, attn_w_o, attn_q_g, attn_k_g, gm_w_in, gm_ln_g, gm_ln_b, gm_ws, gm_bs, gm_w_out, final_g, loss_target, m_c_ctx, m_ada_w, m_ada_b, m_norm_g, m_mlp_w1, m_mlp_w2, m_pool_w, m_pool_scale, m_attn_w_qkv, m_attn_w_o, m_attn_q_g, m_attn_k_g, m_gm_w_in, m_gm_ln_g, m_gm_ln_b, m_gm_ws, m_gm_bs, m_gm_w_out, m_final_g, v_c_ctx, v_ada_w, v_ada_b, v_norm_g, v_mlp_w1, v_mlp_w2, v_pool_w, v_pool_scale, v_attn_w_qkv, v_attn_w_o, v_attn_q_g, v_attn_k_g, v_gm_w_in, v_gm_ln_g, v_gm_ln_b, v_gm_ws, v_gm_bs, v_gm_w_out, v_final_g):
    l_len, d = x.shape[1], x.shape[2]
    c_len = ctx.shape[1]
    n_layers = ada_w.shape[0]
    assert n_layers == 4 and x.shape[0] == 1
    n_heads = d // HEAD_DIM
    n_kv = n_heads // 2
    half = gm_w_out.shape[1] * N_DEV
    tm = c_len if c_len <= 256 else 256
    assert c_len % tm == 0 and l_len % tm == 0 and tm % CHUNK == 0 and l_len % GRID_W == 0
    nct = c_len // tm
    me = _dev_index(*_coords())
    n_ada = ada_w.shape[-1]

    first = [t.astype(BF16) for t in (mlp_w1[0], mlp_w2[0], pool_w, attn_w_qkv[0])]
    small = [c, norm_g.reshape(n_layers * 2, -1), pool_scale, gm_ln_g, gm_ln_b]
    w1_0g, w2_0g, pool_g, qkv_g, c_all, ng_g, ps_g, lng_g, lnb_g = _all_gather(first + small, "gather_first")
    c_all = c_all.reshape(N_DEV, d)
    later = _GatherAcrossChips([t.astype(BF16) for t in
                                (mlp_w1[1], mlp_w1[2], mlp_w1[3], mlp_w2[1], mlp_w2[2], mlp_w2[3],
                                 attn_w_o[0], gm_w_in[0], gm_w_out[0])])
    pool_wf = _natural_rows(pool_g)
    wqkv = _natural_cols(qkv_g)
    ng_full = _natural_cols(ng_g.reshape(N_DEV, n_layers * 2, 1, -1)).reshape(n_layers, 2, 1, d)
    ps_full = _natural_cols(ps_g.reshape(N_DEV, 2, 1, -1))
    lng_full = _natural_cols(lng_g.reshape(N_DEV, 1, -1))
    lnb_full = _natural_cols(lnb_g.reshape(N_DEV, 1, -1))

    c_ctx2 = c_ctx.reshape(1, d)
    ada_b_loc = lax.dynamic_slice_in_dim(ada_b, me * n_ada, n_ada, axis=1).reshape(n_layers, 1, n_ada)
    (mod_g,) = _all_gather([_mods_local(c_all, c_ctx2, ada_w, ada_b_loc)], "gather_mods")
    mod_full = jnp.moveaxis(mod_g, 0, 2).reshape(n_layers, 16, 6, d)
    mod_lat = lax.dynamic_index_in_dim(mod_full, me, axis=1, keepdims=False)
    mod_ctx = mod_full[:, 8]
    mods = jnp.stack([mod_ctx, mod_lat], axis=1)
    mods = jnp.concatenate([mods, jnp.zeros((n_layers, 2, 2, d), F32)], axis=2)

    bands = _pool_bands(tm)
    cos, sin = _rope_tables(c_len, l_len)
    ws_bf = gm_ws[0].astype(BF16)
    wst_bf = jnp.swapaxes(gm_ws[0], 1, 2).astype(BF16)
    bst = jnp.zeros((CHUNK, 128), F32).at[:, :GMLP_GROUPS].set(gm_bs[0].T)
    ng = lambda i, j: ng_full[i, j]

    h0 = jnp.concatenate([ctx[0], x[0]], axis=0)
    y0, h1 = _pool_fwd(h0, mods[0], ng(0, 0), pool_wf[0].astype(BF16), ps_full[0], bands, nct, tm, c_len, l_len)
    h2, p0, ym0 = _mlp_fwd(h1, mods[0], ng(0, 1), w1_0g, w2_0g, nct, tm)
    q, k, v1 = _qkv_fwd(h2, mods[1], ng(1, 0), wqkv, attn_q_g, attn_k_g, cos, sin, n_heads, n_kv, nct, tm)
    o, lse, later_g = _flash_fwd(q, k, v1, n_kv, 2 * tm, tm, rider=later)
    later_g = _exchange_call(_ForwardToSibling(later_g), "forward_weights", in_place=True)
    w1 = [w1_0g] + later_g[0:3]
    w2 = [w2_0g] + later_g[3:6]
    wo_g, gin_g, gout_g = later_g[6:]
    wo = _natural_rows(wo_g)
    win = _natural_cols(gin_g)
    wout = _natural_rows(gout_g)
    y1, h3 = _wo_fwd(h2, o, wo, mods[1], nct, tm)
    tm_lat = 2 * tm
    h4, p1, ym1 = _mlp_fwd(h3, mods[1], ng(1, 1), w1[1], w2[1], 0, tm_lat)
    y2, h5 = _gmlp_fwd(h4, mods[2], ng(2, 0), win, lng_full, lnb_full, ws_bf, bst, wout, CHUNK)
    h6, p2, ym2 = _mlp_fwd(h5, mods[2], ng(2, 1), w1[2], w2[2], 0, tm_lat)
    y3, h7 = _pool_fwd(h6, mods[3], ng(3, 0), pool_wf[1].astype(BF16), ps_full[1], bands, 0, tm, c_len, l_len)
    h8, p3, ym3 = _mlp_fwd(h7, mods[3], ng(3, 1), w1[3], w2[3], 0, tm_lat)
    dh, loss_part, dfinal = _head(h8, final_g.reshape(1, d), loss_target[0], tm)

    dw1, dw2, st_mlp = [None] * 4, [None] * 4, [None] * 4

    def mlp_back(i, dh, h_in, p, ym, nct_i):
        dh_in, m_bf, du, dacc, st, _ = _mlp_bwd(dh, h_in, p, ym, mods[i], ng(i, 1), w1[i], w2[i], nct_i, tm_lat)
        dw1[i] = _tn_matmul(m_bf, du, "tn_w1", col_shards=True)
        dw2[i] = _shard_rows(_tn_matmul(p, dacc, "tn_w2", square_x=True))
        st_mlp[i] = st
        return dh_in

    dh = mlp_back(3, dh, h7, p3, ym3, 0)
    dh, dpw1, st_pool3, _ = _pool_bwd(dh, h6, y3, mods[3], ng(3, 0), pool_wf[1].astype(BF16), ps_full[1], bands,
                                      0, tm, c_len, l_len, False)
    dh = mlp_back(2, dh, h5, p2, ym2, 0)
    dh, a_bf, dz, gated, dy, st_g, st_ln, dws, dbst = _gmlp_bwd(
        dh, h4, y2, mods[2], ng(2, 0), win, lng_full, lnb_full, ws_bf, wst_bf, bst, wout, CHUNK)
    dwin = _tn_matmul(a_bf, dz, "tn_gm_in", col_shards=True)
    dwout = _tn_matmul(gated, dy, "tn_gm_out")
    dh = mlp_back(1, dh, h3, p1, ym1, 0)
    dy1, do, delta, st_wo = _wo_bwd(dh, y1, o, wo, mods[1], n_kv, tm)
    dwo = _tn_matmul(o, dy1, "tn_wo")
    grads_mid = _AllToAll(dw1[1:] + dw2[1:] + [_shard_rows(dpw1.astype(BF16)), _shard_rows(dwo), dwin,
                                               _shard_rows(dwout)])
    dq, dk, dv, rode = _flash_bwd(q, k, v1, do, lse, delta, n_kv, 2 * tm, tm, rider=grads_mid)
    g_w1, g_w2, (g_pool1, g_wo, g_gin, g_gout) = [None] + rode[0:3], [None] + rode[3:6], rode[6:]
    dh, a_bf, dqkv, st_q, dgains = _qkv_bwd(h2, dh, dq, dk, dv, mods[1], ng(1, 0), wqkv, attn_q_g, attn_k_g,
                                            cos, sin, n_heads, n_kv, nct, tm)
    dwqkv = _tn_matmul(a_bf, dqkv, "tn_qkv", col_shards=True)
    dh, m_bf, du, dacc, st_mlp[0], (g_qkv,) = _mlp_bwd(dh, h1, p0, ym0, mods[0], ng(0, 1), w1[0], w2[0], nct, tm,
                                                       rider=_AllToAll([dwqkv]))
    dw1_0 = _tn_matmul(m_bf, du, "tn_w1", col_shards=True)
    dw2_0, (g_w1[0],) = _tn_matmul(p0, dacc, "tn_w2", square_x=True, rider=_AllToAll([dw1_0]))
    grad_x, dpw0, st_pool0, (g_w2[0],) = _pool_bwd(dh, h0, y0, mods[0], ng(0, 0), pool_wf[0].astype(BF16),
                                                   ps_full[0], bands, nct, tm, c_len, l_len, True,
                                                   rider=_AllToAll([_shard_rows(dw2_0)]))

    mix_lat = [st_pool0[-1], st_q[1] + st_wo, st_g, st_pool3[-1]]
    mlp_lat = [st[-1] for st in st_mlp]
    dmod_lat = jnp.stack([jnp.concatenate([mix_lat[i][0:3], mlp_lat[i][0:3]]) for i in range(n_layers)])
    dmod_ctx = jnp.stack([jnp.concatenate([st_pool0[0][0:3], st_mlp[0][0][0:3]]),
                          jnp.concatenate([st_q[0][0:2], jnp.zeros((4, d), F32)]),
                          jnp.zeros((6, d), F32), jnp.zeros((6, d), F32)])
    dng_part = jnp.stack([jnp.stack([mix_lat[0][3] + st_pool0[0][3], mlp_lat[0][3] + st_mlp[0][0][3]]),
                          jnp.stack([mix_lat[1][3] + st_q[0][3], mlp_lat[1][3]]),
                          jnp.stack([mix_lat[2][3], mlp_lat[2][3]]),
                          jnp.stack([mix_lat[3][3], mlp_lat[3][3]])])
    dps_part = jnp.stack([mix_lat[0][4] + st_pool0[0][4], mix_lat[3][4]])
    small_parts = [dmod_lat.reshape(n_layers * 6, d), dmod_ctx.reshape(n_layers * 6, d),
                   dng_part.reshape(n_layers * 2, d), dps_part, st_ln, dgains, dws.reshape(-1, CHUNK),
                   dbst, dfinal, loss_part]
    (gm_lat, gm_ctx, g_ng, g_ps, g_ln, g_gains, g_ws, g_bst, g_final, loss_all, g_pool0) = _all_gather(
        small_parts, "gather_small_grads", extra=_AllToAll([_shard_rows(dpw0.astype(BF16))]))
    g_pool = jnp.stack([g_pool0, g_pool1], axis=1)

    gm_lat4 = gm_lat.reshape(N_DEV, n_layers, 6 * d)
    gm_ctx4 = gm_ctx.reshape(N_DEV, n_layers, 6 * d)
    dm_lat_loc = jnp.moveaxis(_my_cols(gm_lat4, me, n_ada), 0, 1)
    dm_ctx_loc = jnp.moveaxis(_my_cols(gm_ctx4, me, n_ada), 0, 1)
    g_ada_w, ds_part = _ada_grads(c_all, c_ctx2, ada_w, dm_lat_loc, dm_ctx_loc)
    (ds_all,) = _all_gather([ds_part], "gather_dsctx")
    g_c_ctx, loss_sum = _cctx_grad_and_loss(ds_all, c_ctx2, loss_all)
    g_c_ctx = g_c_ctx.reshape(d)

    n_ng = norm_g.shape[-1]
    n_ps = pool_scale.shape[-1]
    n_ln = gm_ln_g.shape[-1]
    gparts = {
        "c_ctx": g_c_ctx[None],
        "ada_w": g_ada_w[None],
        "ada_b": jnp.concatenate([gm_lat4, gm_ctx4], axis=0),
        "norm_g": _my_cols(g_ng.reshape(N_DEV, n_layers, 2, d), me, n_ng),
        "mlp_w1": g_w1, "mlp_w2": g_w2, "pool_w": g_pool,
        "pool_scale": _my_cols(g_ps, me, n_ps),
        "attn_w_qkv": g_qkv[:, None], "attn_w_o": g_wo[:, None],
        "attn_q_g": g_gains[:, 0:1], "attn_k_g": g_gains[:, 1:2],
        "gm_w_in": g_gin[:, None],
        "gm_ln_g": _my_cols(g_ln[:, 0:1], me, n_ln), "gm_ln_b": _my_cols(g_ln[:, 1:2], me, n_ln),
        "gm_ws": g_ws.reshape((N_DEV,) + gm_ws.shape),
        "gm_bs": jnp.swapaxes(g_bst[:, :, :GMLP_GROUPS], 1, 2)[:, None],
        "gm_w_out": g_gout[:, None],
        "final_g": g_final[:, 0],
    }
    weights = dict(c_ctx=(c_ctx, m_c_ctx, v_c_ctx), ada_w=(ada_w, m_ada_w, v_ada_w), ada_b=(ada_b, m_ada_b, v_ada_b),
                   norm_g=(norm_g, m_norm_g, v_norm_g), mlp_w1=(mlp_w1, m_mlp_w1, v_mlp_w1),
                   mlp_w2=(mlp_w2, m_mlp_w2, v_mlp_w2), pool_w=(pool_w, m_pool_w, v_pool_w),
                   pool_scale=(pool_scale, m_pool_scale, v_pool_scale),
                   attn_w_qkv=(attn_w_qkv, m_attn_w_qkv, v_attn_w_qkv), attn_w_o=(attn_w_o, m_attn_w_o, v_attn_w_o),
                   attn_q_g=(attn_q_g, m_attn_q_g, v_attn_q_g), attn_k_g=(attn_k_g, m_attn_k_g, v_attn_k_g),
                   gm_w_in=(gm_w_in, m_gm_w_in, v_gm_w_in), gm_ln_g=(gm_ln_g, m_gm_ln_g, v_gm_ln_g),
                   gm_ln_b=(gm_ln_b, m_gm_ln_b, v_gm_ln_b), gm_ws=(gm_ws, m_gm_ws, v_gm_ws),
                   gm_bs=(gm_bs, m_gm_bs, v_gm_bs), gm_w_out=(gm_w_out, m_gm_w_out, v_gm_w_out),
                   final_g=(final_g, m_final_g, v_final_g))
    grads, deltas, new_m, new_v = [], [], [], []
    for wname, (w_, m_, v_) in weights.items():
        g_, d_, nm_, nv_ = _adamw(w_, gparts[wname], m_, v_, "adamw_" + wname)
        grads.append(g_)
        deltas.append(d_)
        new_m.append(nm_)
        new_v.append(nv_)

    return (loss_sum[0, 0], grad_x[None], *grads, *deltas, *new_m, *new_v)
```

```python
import functools
import math

import numpy as np
import jax
import jax.numpy as jnp
from jax import lax
from jax.experimental import pallas as pl
from jax.experimental.pallas import tpu as pltpu

F32 = jnp.float32
BF16 = jnp.bfloat16
MESH_ID = pl.DeviceIdType.MESH

N_DEV = 8
EPS = 1e-6
HEAD_DIM = 128
GRID_W = 64
ROPE_BASE = 10000.0
CHUNK = 128
POOL_WINDOWS = (2, 4, 8, 16)
POOL_GROUPS = 4
POOL_HALO = 8
GMLP_GROUPS = 8
ADAM_LR, ADAM_B1, ADAM_B2, ADAM_EPS, ADAM_WD, ADAM_STEP = 0.001, 0.9, 0.999, 1e-08, 0.01, 10

V7X_VMEM_BYTES = 64 << 20
VMEM_LIMIT_BIG = V7X_VMEM_BYTES - (8 << 20)
FLASH_TK_CAP = 768
LOG2E = math.log2(math.e)
Q_SCALE = HEAD_DIM ** -0.5 * LOG2E


def _params(sem, vmem=VMEM_LIMIT_BIG):
    return pltpu.CompilerParams(dimension_semantics=sem, vmem_limit_bytes=vmem)


def _const_spec(shape):
    nd = len(shape)
    return pl.BlockSpec(shape, lambda *_: (0,) * nd, pipeline_mode=pl.Buffered(1))


def _dot(a, b):
    return jnp.dot(a, b, preferred_element_type=F32)


def _dot_nt(a, b):
    return lax.dot_general(a, b, (((1,), (1,)), ((), ())), preferred_element_type=F32)


def _dot_tn(a, b):
    return lax.dot_general(a, b, (((0,), (0,)), ((), ())), preferred_element_type=F32)


def _colsum(x):
    return jnp.sum(x, axis=0, keepdims=True)


def _sid(i, nct):
    if nct == 0:
        return 1
    return jnp.where(i >= nct, 1, 0)


def _n_streams(nct):
    return 2 if nct else 1


def _stat_sid(i, nct):
    return _sid(i, nct) if nct else 0


def _first_of_stream(i, nct):
    if nct == 0:
        return i == 0
    return jnp.logical_or(i == 0, i == nct)


def _normmod(h, ng, sh, sc):
    rstd = lax.rsqrt(jnp.mean(h * h, axis=-1, keepdims=True) + EPS)
    xhat = h * rstd
    n = xhat * ng
    return n * (1.0 + sc) + sh, xhat, rstd, n


def _normmod_bwd(da, xhat, rstd, n, ng, sc):
    dsh = _colsum(da)
    dsc = _colsum(da * n)
    dn = da * (1.0 + sc)
    dng = _colsum(dn * xhat)
    dxhat = dn * ng
    dh = rstd * (dxhat - xhat * jnp.mean(dxhat * xhat, axis=-1, keepdims=True))
    return dh, dsh, dsc, dng


def _acc_rows(ref, first, rows):
    @pl.when(first)
    def _():
        ref[...] = jnp.zeros_like(ref)
    for r, val in enumerate(rows):
        ref[r:r + 1, :] = ref[r:r + 1, :] + val


_GELU_C = math.sqrt(2.0 / math.pi)


def _gelu(x):
    t = jnp.tanh((_GELU_C * x) * (1.0 + 0.044715 * (x * x)))
    hx = 0.5 * x
    return hx + hx * t


def _gelu_and_grad(x):
    x2 = x * x
    t = jnp.tanh((_GELU_C * x) * (1.0 + 0.044715 * x2))
    hx = 0.5 * x
    g = hx + hx * t
    dg = (0.5 + 0.5 * t) + (hx * (1.0 - t * t)) * (_GELU_C + (3.0 * 0.044715 * _GELU_C) * x2)
    return g, dg


def _coords():
    return lax.axis_index("x"), lax.axis_index("y"), lax.axis_index("c")


def _dev_index(px, py, pc):
    return 4 * px + 2 * py + pc


def _all_gather(xs, name, extra=None):
    n = len(xs)
    e_in, e_in_specs, e_out, e_out_specs, e_scratch = _rider_parts(extra)
    ne = len(e_in)

    def body(*refs):
        x_refs, e_x = refs[:n], refs[n:n + ne]
        o_refs, e_o = refs[n + ne:2 * n + ne], refs[2 * n + ne:2 * n + 2 * ne]
        send_sems, recv_sems, local_sems = refs[2 * n + 2 * ne:2 * n + 2 * ne + 3]
        e_sems = refs[2 * n + 2 * ne + 3:]
        if extra is not None:
            extra.start(e_x, e_o, e_sems)
        x, y, c = _coords()
        me, sibling = (x, y, c), (x, y, 1 - c)
        chips = [(1 - x, y), (x, 1 - y), (1 - x, 1 - y)]

        def copy(a, k, block, to, src=None):
            dst = o_refs[a].at[_dev_index(*block)]
            return pltpu.make_async_remote_copy(
                src_ref=dst if src is None else src, dst_ref=dst,
                send_sem=send_sems.at[7 * a + k], recv_sem=recv_sems.at[7 * a + k],
                device_id=to, device_id_type=MESH_ID)

        mine = [pltpu.make_async_copy(x_refs[a], o_refs[a].at[_dev_index(*me)], local_sems.at[a])
                for a in range(n)]
        for cp in mine:
            cp.start()
        first = []
        for a in range(n):
            first.append(copy(a, 0, me, sibling, src=x_refs[a]))
            first += [copy(a, 1 + j, me, (*chip, c), src=x_refs[a]) for j, chip in enumerate(chips)]
        for cp in first:
            cp.start()
        passed = []
        for a in range(n):
            for j, chip in enumerate(chips):
                copy(a, 1 + j, (*chip, c), me).wait_recv()
                fwd = copy(a, 4 + j, (*chip, c), sibling)
                fwd.start()
                passed.append(fwd)
        for a in range(n):
            copy(a, 0, sibling, me).wait_recv()
            for j, chip in enumerate(chips):
                copy(a, 4 + j, (*chip, 1 - c), me).wait_recv()
        for cp in first + passed:
            cp.wait_send()
        for cp in mine:
            cp.wait()
        if extra is not None:
            extra.finish(e_x, e_o, e_sems)

    any_spec = pl.BlockSpec(memory_space=pl.ANY)
    outs = pl.pallas_call(
        body, name=name,
        out_shape=[jax.ShapeDtypeStruct((N_DEV,) + x.shape, x.dtype) for x in xs] + e_out,
        in_specs=[any_spec] * n + e_in_specs, out_specs=[any_spec] * n + e_out_specs,
        scratch_shapes=[pltpu.SemaphoreType.DMA((7 * n,)), pltpu.SemaphoreType.DMA((7 * n,)),
                        pltpu.SemaphoreType.DMA((n,))] + e_scratch,
    )(*xs, *e_in)
    return list(outs)


class _Exchange:
    per_array = 0
    in_place = False

    def __init__(self, xs):
        self.xs = list(xs)
        n = len(self.xs)
        self.out_shapes = self._out_shapes()
        self.scratch = [pltpu.SemaphoreType.DMA((self.per_array * n,)),
                        pltpu.SemaphoreType.DMA((self.per_array * n,)),
                        pltpu.SemaphoreType.DMA((n,))]

    def _out_shapes(self):
        raise NotImplementedError

    def _copies(self, x_refs, o_refs, sems):
        raise NotImplementedError

    def start(self, x_refs, o_refs, sems):
        mine, sends, _ = self._copies(x_refs, o_refs, sems)
        for cp in mine + sends:
            cp.start()

    def finish(self, x_refs, o_refs, sems):
        mine, sends, arrivals = self._copies(x_refs, o_refs, sems)
        for make in arrivals:
            make().wait_recv()
        for cp in sends:
            cp.wait_send()
        for cp in mine:
            cp.wait()


def _remote(src, dst, sems, k, to):
    return pltpu.make_async_remote_copy(src_ref=src, dst_ref=dst, send_sem=sems[0].at[k], recv_sem=sems[1].at[k],
                                        device_id=to, device_id_type=MESH_ID)


class _GatherAcrossChips(_Exchange):
    per_array = 4

    def _out_shapes(self):
        return [jax.ShapeDtypeStruct((N_DEV,) + x.shape, x.dtype) for x in self.xs]

    def _copies(self, x_refs, o_refs, sems):
        x, y, c = _coords()
        targets = [(x, y, 1 - c), (1 - x, y, c), (x, 1 - y, c), (1 - x, 1 - y, c)]
        mine, sends, arrivals = [], [], []
        for a, (x_ref, o_ref) in enumerate(zip(x_refs, o_refs)):
            own = o_ref.at[_dev_index(x, y, c)]
            mine.append(pltpu.make_async_copy(x_ref, own, sems[2].at[a]))
            for k, to in enumerate(targets):
                sends.append(_remote(x_ref, own, sems, 4 * a + k, to))
                arrivals.append(functools.partial(_remote, x_ref, o_ref.at[_dev_index(*to)], sems, 4 * a + k, to))
        return mine, sends, arrivals


class _ForwardToSibling(_Exchange):
    per_array = 3
    in_place = True

    def _out_shapes(self):
        return [jax.ShapeDtypeStruct(x.shape, x.dtype) for x in self.xs]

    def _copies(self, x_refs, o_refs, sems):
        x, y, c = _coords()
        chips = [(1 - x, y), (x, 1 - y), (1 - x, 1 - y)]
        sends, arrivals = [], []
        for a, (x_ref, o_ref) in enumerate(zip(x_refs, o_refs)):
            for j, chip in enumerate(chips):
                held = _dev_index(*chip, c)
                sends.append(_remote(x_ref.at[held], o_ref.at[held], sems, 3 * a + j, (x, y, 1 - c)))
                theirs = _dev_index(*chip, 1 - c)
                arrivals.append(functools.partial(_remote, x_ref.at[theirs], o_ref.at[theirs], sems, 3 * a + j,
                                                  (x, y, 1 - c)))
        return [], sends, arrivals


class _AllToAll(_Exchange):
    per_array = 7

    def _out_shapes(self):
        return [jax.ShapeDtypeStruct(x.shape, x.dtype) for x in self.xs]

    def _copies(self, x_refs, o_refs, sems):
        x, y, c = _coords()
        me_i = _dev_index(x, y, c)
        mine, sends, arrivals = [], [], []
        for a, (x_ref, o_ref) in enumerate(zip(x_refs, o_refs)):
            mine.append(pltpu.make_async_copy(x_ref.at[me_i], o_ref.at[me_i], sems[2].at[a]))
            for r in range(1, 8):
                to = (1 - x if r & 4 else x, 1 - y if r & 2 else y, 1 - c if r & 1 else c)
                to_i = _dev_index(*to)
                sends.append(_remote(x_ref.at[to_i], o_ref.at[me_i], sems, 7 * a + r - 1, to))
                arrivals.append(functools.partial(_remote, x_ref.at[to_i], o_ref.at[to_i], sems, 7 * a + r - 1, to))
        return mine, sends, arrivals


def _exchange_call(ex, name):
    n = len(ex.xs)

    def body(*refs):
        x_refs, o_refs, sems = refs[:n], refs[n:2 * n], refs[2 * n:]
        ex.start(x_refs, o_refs, sems)
        ex.finish(x_refs, o_refs, sems)

    any_spec = pl.BlockSpec(memory_space=pl.ANY)
    outs = pl.pallas_call(
        body, name=name, out_shape=ex.out_shapes, in_specs=[any_spec] * n, out_specs=[any_spec] * n,
        scratch_shapes=ex.scratch, input_output_aliases={a: a for a in range(n)} if ex.in_place else {},
    )(*ex.xs)
    return list(outs)


def _rider_parts(rider):
    if rider is None:
        return [], [], [], [], []
    any_spec = pl.BlockSpec(memory_space=pl.ANY)
    n = len(rider.xs)
    return rider.xs, [any_spec] * n, rider.out_shapes, [any_spec] * n, rider.scratch


def _compute_call(body, *, name, grid, in_specs, out_specs, out_shape, operands, scratch_shapes=(), rider=None):
    in_specs, out_specs, out_shape, scratch_shapes = list(in_specs), list(out_specs), list(out_shape), list(scratch_shapes)
    r_in, r_in_specs, r_out, r_out_specs, r_scratch = _rider_parts(rider)
    n_in, n_out, n_scr, nr = len(operands), len(out_shape), len(scratch_shapes), len(r_in)

    def riding_body(*refs):
        ins, refs = refs[:n_in], refs[n_in:]
        r_x, refs = refs[:nr], refs[nr:]
        outs, refs = refs[:n_out], refs[n_out:]
        r_o, refs = refs[:nr], refs[nr:]
        scratch, r_sems = refs[:n_scr], refs[n_scr:]
        if rider is not None:
            first, last = _grid_ends(grid)
            pl.when(first)(lambda: rider.start(r_x, r_o, r_sems))
        body(*ins, *outs, *scratch)
        if rider is not None:
            pl.when(last)(lambda: rider.finish(r_x, r_o, r_sems))

    in_place = rider is not None and rider.in_place
    res = pl.pallas_call(
        riding_body, name=name, grid=grid, out_shape=out_shape + r_out,
        in_specs=in_specs + r_in_specs, out_specs=out_specs + r_out_specs,
        scratch_shapes=scratch_shapes + r_scratch,
        input_output_aliases={n_in + a: n_out + a for a in range(nr)} if in_place else {},
        compiler_params=_params(("arbitrary",) * len(grid)),
    )(*operands, *r_in)
    return list(res[:n_out]), list(res[n_out:])


def _grid_ends(grid):
    first = pl.program_id(0) == 0
    last = pl.program_id(0) == grid[0] - 1
    for ax in range(1, len(grid)):
        first = jnp.logical_and(first, pl.program_id(ax) == 0)
        last = jnp.logical_and(last, pl.program_id(ax) == grid[ax] - 1)
    return first, last


def _silu(x):
    return x * (1.0 / (1.0 + jnp.exp(-x)))


def _cond_rows(c_all, c_ctx):
    d = c_all.shape[-1]
    s = jnp.concatenate([c_all, jnp.zeros((8, d), F32)], axis=0)
    row = lax.broadcasted_iota(jnp.int32, (16, d), 0)
    s = jnp.where(row == 8, c_ctx, s)
    return jnp.where(row <= 8, _silu(s), 0.0)


def _mods_local(c_all, c_ctx, ada_w, ada_b_loc):
    nl, d, n = ada_w.shape

    def body(c_ref, cc_ref, w_ref, b_ref, o_ref):
        s = _cond_rows(c_ref[...], cc_ref[...])
        o_ref[...] = jnp.dot(s, w_ref[...], preferred_element_type=F32,
                             precision=lax.Precision.HIGHEST) + b_ref[...]

    return pl.pallas_call(
        body, name="mods_local", grid=(nl,),
        out_shape=jax.ShapeDtypeStruct((nl, 16, n), F32),
        in_specs=[pl.BlockSpec((8, d), lambda i: (0, 0)), pl.BlockSpec((1, d), lambda i: (0, 0)),
                  pl.BlockSpec((None, d, n), lambda i: (i, 0, 0)),
                  pl.BlockSpec((None, 1, n), lambda i: (i, 0, 0))],
        out_specs=pl.BlockSpec((None, 16, n), lambda i: (i, 0, 0)),
        compiler_params=_params(("arbitrary",)),
    )(c_all, c_ctx, ada_w, ada_b_loc)


def _ada_grads(c_all, c_ctx, ada_w, dm_lat, dm_ctx):
    nl, d, n = ada_w.shape

    def body(c_ref, cc_ref, w_ref, dml_ref, dmc_ref, gw_ref, ds_ref):
        i = pl.program_id(0)
        s = _cond_rows(c_ref[...], cc_ref[...])
        csum = dmc_ref[0:1, :]
        for k in range(1, N_DEV):
            csum = csum + dmc_ref[k:k + 1, :]
        row = lax.broadcasted_iota(jnp.int32, (8, n), 0)
        dm_c = jnp.where(row == 0, csum, 0.0)
        dm = jnp.concatenate([dml_ref[...], dm_c], axis=0)
        gw_ref[...] = lax.dot_general(s, dm, (((0,), (0,)), ((), ())), preferred_element_type=F32,
                                      precision=lax.Precision.HIGHEST)
        ds = lax.dot_general(dm_c, w_ref[...], (((1,), (1,)), ((), ())),
                             preferred_element_type=F32, precision=lax.Precision.HIGHEST)

        @pl.when(i == 0)
        def _():
            ds_ref[...] = jnp.zeros_like(ds_ref)
        ds_ref[...] += ds

    return pl.pallas_call(
        body, name="ada_grads", grid=(nl,),
        out_shape=[jax.ShapeDtypeStruct((nl, d, n), F32), jax.ShapeDtypeStruct((8, d), F32)],
        in_specs=[pl.BlockSpec((8, d), lambda i: (0, 0)), pl.BlockSpec((1, d), lambda i: (0, 0)),
                  pl.BlockSpec((None, d, n), lambda i: (i, 0, 0)),
                  pl.BlockSpec((None, 8, n), lambda i: (i, 0, 0)),
                  pl.BlockSpec((None, 8, n), lambda i: (i, 0, 0))],
        out_specs=[pl.BlockSpec((None, d, n), lambda i: (i, 0, 0)),
                   pl.BlockSpec((8, d), lambda i: (0, 0))],
        compiler_params=_params(("arbitrary",)),
    )(c_all, c_ctx, ada_w, dm_lat, dm_ctx)


def _cctx_grad_and_loss(ds_parts, c_ctx, loss_parts):
    d = c_ctx.shape[-1]

    def body(p_ref, c_ref, l_ref, o_ref, lo_ref):
        ds, loss = p_ref[0], l_ref[0]
        for k in range(1, N_DEV):
            ds = ds + p_ref[k]
            loss = loss + l_ref[k]
        x = c_ref[...]
        sg = 1.0 / (1.0 + jnp.exp(-x))
        o_ref[...] = ds[0:1, :] * (sg * (1.0 + x * (1.0 - sg)))
        lo_ref[...] = loss

    return pl.pallas_call(body, name="cctx_grad", out_shape=[jax.ShapeDtypeStruct((1, d), F32),
                                                             jax.ShapeDtypeStruct((8, 128), F32)])(ds_parts, c_ctx, loss_parts)


def _mlp_fwd(h1, mods, ng, w1, w2, nct, tm, row_off=0):
    d = h1.shape[1]
    r = h1.shape[0] - row_off * tm
    fc = w1.shape[2]
    f = N_DEV * fc

    def body(h_ref, mod_ref, ng_ref, w1_ref, w2_ref, h2_ref, p_ref, y_ref):
        h = h_ref[...]
        a, _, _, _ = _normmod(h, ng_ref[...], mod_ref[3:4, :], mod_ref[4:5, :])
        ab = a.astype(BF16)
        acc = jnp.zeros((tm, d), F32)
        for j in range(N_DEV):
            sl = slice(j * fc, (j + 1) * fc)
            p = jnp.maximum(_dot(ab, w1_ref[j]), 0.0)
            p_ref[:, sl] = p.astype(BF16)
            acc = acc + _dot((p * p).astype(BF16), w2_ref[j])
        y_ref[...] = acc.astype(BF16)
        h2_ref[...] = h + mod_ref[5:6, :] * acc

    return pl.pallas_call(
        body, name="mlp_fwd", grid=(r // tm,),
        out_shape=[jax.ShapeDtypeStruct((r, d), F32), jax.ShapeDtypeStruct((r, f), BF16),
                   jax.ShapeDtypeStruct((r, d), BF16)],
        in_specs=[pl.BlockSpec((tm, d), lambda i: (i + row_off, 0)),
                  pl.BlockSpec((None, 8, d), lambda i: (_sid(i, nct), 0, 0)),
                  _const_spec((1, d)), _const_spec(w1.shape), _const_spec(w2.shape)],
        out_specs=[pl.BlockSpec((tm, d), lambda i: (i, 0)), pl.BlockSpec((tm, f), lambda i: (i, 0)),
                   pl.BlockSpec((tm, d), lambda i: (i, 0))],
        compiler_params=_params(("arbitrary",)),
    )(h1, mods, ng, w1, w2)


def _mlp_bwd(dh2, h1, p, y, mods, ng, w1, w2, nct, tm, row_off=0, rider=None):
    r_rows, d = dh2.shape
    fc = w1.shape[2]
    f = N_DEV * fc

    def body(dh_ref, h_ref, p_ref, y_ref, mod_ref, ng_ref, w1_ref, w2_ref,
             dh1_ref, m_ref, du_ref, dacc_ref, st_ref):
        i = pl.program_id(0)
        dh = dh_ref[...]
        ngv, sc, gate = ng_ref[...], mod_ref[4:5, :], mod_ref[5:6, :]
        a, xhat, rstd, n = _normmod(h_ref[...], ngv, mod_ref[3:4, :], sc)
        m_ref[...] = a.astype(BF16)
        dgate = _colsum(dh * y_ref[...].astype(F32))
        dacc = (gate * dh).astype(BF16)
        dacc_ref[...] = dacc
        dm = jnp.zeros((tm, d), F32)
        for j in range(N_DEV):
            sl = slice(j * fc, (j + 1) * fc)
            pj = p_ref[:, sl].astype(F32)
            du = (_dot_nt(dacc, w2_ref[j]) * (2.0 * pj)).astype(BF16)
            du_ref[:, sl] = du
            dm = dm + _dot_nt(du, w1_ref[j])
        dhn, dsh, dsc, dng = _normmod_bwd(dm, xhat, rstd, n, ngv, sc)
        dh1_ref[...] = dh + dhn
        _acc_rows(st_ref, _first_of_stream(i, nct), [dsh, dsc, dgate, dng])

    outs, rode = _compute_call(
        body, name="mlp_bwd", grid=(r_rows // tm,), operands=(dh2, h1, p, y, mods, ng, w1, w2), rider=rider,
        out_shape=[jax.ShapeDtypeStruct((r_rows, d), F32), jax.ShapeDtypeStruct((r_rows, d), BF16),
                   jax.ShapeDtypeStruct((r_rows, f), BF16),
                   jax.ShapeDtypeStruct((r_rows, d), BF16), jax.ShapeDtypeStruct((_n_streams(nct), 8, d), F32)],
        in_specs=[pl.BlockSpec((tm, d), lambda i: (i, 0)),
                  pl.BlockSpec((tm, d), lambda i: (i + row_off, 0)),
                  pl.BlockSpec((tm, f), lambda i: (i, 0)), pl.BlockSpec((tm, d), lambda i: (i, 0)),
                  pl.BlockSpec((None, 8, d), lambda i: (_sid(i, nct), 0, 0)),
                  _const_spec((1, d)), _const_spec(w1.shape), _const_spec(w2.shape)],
        out_specs=[pl.BlockSpec((tm, d), lambda i: (i, 0)), pl.BlockSpec((tm, d), lambda i: (i, 0)),
                   pl.BlockSpec((tm, f), lambda i: (i, 0)),
                   pl.BlockSpec((tm, d), lambda i: (i, 0)),
                   pl.BlockSpec((None, 8, d), lambda i: (_stat_sid(i, nct), 0, 0))])
    return (*outs, rode)


def _pick(n, cands):
    for cand in cands:
        if n % cand == 0:
            return cand
    return n


def _tn_matmul(x, y, name, col_shards=False, square_x=False, rider=None):
    rows, k1 = x.shape
    k2 = y.shape[1]
    bt = _pick(rows, (1024, 768, 512, 384, 256, 128))
    bk1, bk2 = min(k1, 1024), min(k2, 1024)
    grid = (k1 // bk1, k2 // bk2, rows // bt)
    nt = rows // bt
    n = k2 // N_DEV
    if col_shards:
        assert bk2 % n == 0
        per = bk2 // n
        out_shape = jax.ShapeDtypeStruct((N_DEV, k1, n), BF16)
        out_spec = pl.BlockSpec((per, bk1, n), lambda i, j, t: (j, i, 0))
    else:
        out_shape = jax.ShapeDtypeStruct((k1, k2), BF16)
        out_spec = pl.BlockSpec((bk1, bk2), lambda i, j, t: (i, j))

    def body(x_ref, y_ref, o_ref, acc_ref):
        t = pl.program_id(2)

        @pl.when(t == 0)
        def _():
            acc_ref[...] = jnp.zeros_like(acc_ref)
        xv = x_ref[...]
        acc_ref[...] += _dot_tn(xv * xv if square_x else xv, y_ref[...])

        @pl.when(t == nt - 1)
        def _():
            if col_shards:
                for s in range(per):
                    o_ref[s] = acc_ref[:, s * n:(s + 1) * n].astype(BF16)
            else:
                o_ref[...] = acc_ref[...].astype(BF16)

    (out,), rode = _compute_call(
        body, name=name, grid=grid, operands=(x, y), rider=rider, out_shape=[out_shape],
        in_specs=[pl.BlockSpec((bt, bk1), lambda i, j, t: (t, i)), pl.BlockSpec((bt, bk2), lambda i, j, t: (t, j))],
        out_specs=[out_spec], scratch_shapes=[pltpu.VMEM((bk1, bk2), F32)])
    return out if rider is None else (out, rode)


def _pool_bands(tm):
    k = tm + 128
    t = np.arange(tm)[:, None]
    e = np.arange(k)[None, :]
    fwd, bwd = [], []
    for w in POOL_WINDOWS:
        lo = POOL_HALO + t - w // 2
        fwd.append(((e >= lo) & (e <= lo + w - 1)).astype(np.float32))
        lo_t = POOL_HALO + t - w // 2 + 1
        bwd.append(((e >= lo_t) & (e <= lo_t + w - 1)).astype(np.float32))
    return jnp.asarray(np.stack(fwd), BF16), jnp.asarray(np.stack(bwd), BF16)


def _pool_geometry(i, nct, n_tiles, tm, c_len, l_len):
    if nct == 0:
        pos0 = i * tm
        ls = l_len
        has_prev = i > 0
        has_next = i < n_tiles - 1
    else:
        in_ctx = i < nct
        pos0 = jnp.where(in_ctx, i, i - nct) * tm
        ls = jnp.where(in_ctx, c_len, l_len)
        has_prev = jnp.logical_and(i != 0, i != nct)
        has_next = jnp.logical_and(i != nct - 1, i != n_tiles - 1)
    return pos0, ls, has_prev, has_next


def _window_inv_counts(pos, ls):
    out = []
    for w in POOL_WINDOWS:
        lo = jnp.maximum(pos - w // 2, 0)
        hi = jnp.minimum(pos + w - w // 2, ls)
        cnt = jnp.maximum(hi - lo, 1).astype(F32)
        out.append(1.0 / cnt)
    return out


def _split_bf16(x):
    hi = x.astype(BF16)
    return hi, (x - hi.astype(F32)).astype(BF16)


def _extend(prev, tile, nxt, has_prev, has_next):
    w = tile.shape[1]
    prev = jnp.where(has_prev, prev, 0.0)
    nxt = jnp.where(has_next, nxt, 0.0)
    return jnp.concatenate([prev, tile, nxt, jnp.zeros((128 - 2 * POOL_HALO, w), F32)], axis=0)


def _pool_specs(tm, d, n_rows):
    last8 = n_rows // POOL_HALO - 1
    per = tm // POOL_HALO
    return [pl.BlockSpec((tm, d), lambda i: (i, 0)),
            pl.BlockSpec((POOL_HALO, d), lambda i: (jnp.maximum(i * per - 1, 0), 0)),
            pl.BlockSpec((POOL_HALO, d), lambda i: (jnp.minimum((i + 1) * per, last8), 0))]


def _pool_fwd(h, mods, ng, w, scale, bands, nct, tm, c_len, l_len):
    r, d = h.shape
    gw = d // POOL_GROUPS
    n_tiles = r // tm
    kx = tm + 128

    def body(h_ref, hp_ref, hn_ref, mod_ref, ng_ref, w_ref, sc_ref, band_ref, y_ref, h1_ref):
        i = pl.program_id(0)
        pos0, ls, has_prev, has_next = _pool_geometry(i, nct, n_tiles, tm, c_len, l_len)
        ngv, sh, sc = ng_ref[...], mod_ref[0:1, :], mod_ref[1:2, :]
        h = h_ref[...]
        a = _normmod(h, ngv, sh, sc)[0]
        a_ext = _extend(_normmod(hp_ref[...], ngv, sh, sc)[0], a, _normmod(hn_ref[...], ngv, sh, sc)[0],
                        has_prev, has_next)
        pos = pos0 + lax.broadcasted_iota(jnp.int32, (tm, 1), 0)
        inv = _window_inv_counts(pos, ls)
        ys = []
        for g in range(POOL_GROUPS):
            cols = slice(g * gw, (g + 1) * gw)
            hi, lo = _split_bf16(a_ext[:, cols])
            s = _dot(band_ref[g], hi) + _dot(band_ref[g], lo)
            pg = s * inv[g] - a[:, cols]
            ys.append(_dot(pg.astype(BF16), w_ref[g]))
        y = jnp.concatenate(ys, axis=1) * sc_ref[...]
        y_ref[...] = y.astype(BF16)
        h1_ref[...] = h + mod_ref[2:3, :] * y

    return pl.pallas_call(
        body, name="pool_fwd", grid=(n_tiles,),
        out_shape=[jax.ShapeDtypeStruct((r, d), BF16), jax.ShapeDtypeStruct((r, d), F32)],
        in_specs=_pool_specs(tm, d, r) + [
            pl.BlockSpec((None, 8, d), lambda i: (_sid(i, nct), 0, 0)),
            _const_spec((1, d)), _const_spec(w.shape), _const_spec((1, d)), _const_spec((4, tm, kx))],
        out_specs=[pl.BlockSpec((tm, d), lambda i: (i, 0)), pl.BlockSpec((tm, d), lambda i: (i, 0))],
        compiler_params=_params(("arbitrary",)),
    )(h, h, h, mods, ng, w, scale, bands[0])


def _pool_bwd(dh1, h, y, mods, ng, w, scale, bands, nct, tm, c_len, l_len, latent_out, rider=None):
    r, d = h.shape
    gw = d // POOL_GROUPS
    n_tiles = r // tm
    kx = tm + 128
    out_rows = l_len if latent_out else r
    out_off = nct if latent_out else 0

    def body(dh_ref, dhp_ref, dhn_ref, h_ref, hp_ref, hn_ref, y_ref, mod_ref, ng_ref, w_ref, sc_ref,
             bf_ref, bb_ref, dho_ref, dw_ref, st_ref):
        i = pl.program_id(0)
        pos0, ls, has_prev, has_next = _pool_geometry(i, nct, n_tiles, tm, c_len, l_len)
        ngv, sh, sc, gate = ng_ref[...], mod_ref[0:1, :], mod_ref[1:2, :], mod_ref[2:3, :]
        scale_v = sc_ref[...]
        h = h_ref[...]
        a, xhat, rstd, n = _normmod(h, ngv, sh, sc)
        a_ext = _extend(_normmod(hp_ref[...], ngv, sh, sc)[0], a, _normmod(hn_ref[...], ngv, sh, sc)[0],
                        has_prev, has_next)
        dh = dh_ref[...]
        dgate = _colsum(dh * y_ref[...].astype(F32))
        dy = gate * dh
        dy_ext = _extend(gate * dhp_ref[...], dy, gate * dhn_ref[...], has_prev, has_next)
        dyp_ext = (dy_ext * scale_v).astype(BF16)
        dyp = (dy * scale_v).astype(BF16)
        pos = pos0 + lax.broadcasted_iota(jnp.int32, (tm, 1), 0)
        inv = _window_inv_counts(pos, ls)
        pos_e = pos0 - POOL_HALO + lax.broadcasted_iota(jnp.int32, (kx, 1), 0)
        inv_e = _window_inv_counts(pos_e, ls)

        @pl.when(i == 0)
        def _():
            dw_ref[...] = jnp.zeros_like(dw_ref)
        das, dscale = [], []
        for g in range(POOL_GROUPS):
            cols = slice(g * gw, (g + 1) * gw)
            hi, lo = _split_bf16(a_ext[:, cols])
            pg = ((_dot(bf_ref[g], hi) + _dot(bf_ref[g], lo)) * inv[g] - a[:, cols]).astype(BF16)
            dscale.append(_colsum(dy[:, cols] * _dot(pg, w_ref[g])))
            dyp_g = dyp_ext[:, cols]
            dw_ref[g] += _dot_tn(pg, dyp[:, cols])
            dp_ext = _dot_nt(dyp_g, w_ref[g])
            hi, lo = _split_bf16(dp_ext * inv_e[g])
            das.append(_dot(bb_ref[g], hi) + _dot(bb_ref[g], lo) - dp_ext[POOL_HALO:POOL_HALO + tm, :])
        da = jnp.concatenate(das, axis=1)
        dhn, dsh, dsc, dng = _normmod_bwd(da, xhat, rstd, n, ngv, sc)
        dho_ref[...] = dh + dhn
        _acc_rows(st_ref, _first_of_stream(i, nct), [dsh, dsc, dgate, dng, jnp.concatenate(dscale, axis=1)])

    outs, rode = _compute_call(
        body, name="pool_bwd", grid=(n_tiles,), rider=rider,
        operands=(dh1, dh1, dh1, h, h, h, y, mods, ng, w, scale, bands[0], bands[1]),
        out_shape=[jax.ShapeDtypeStruct((out_rows, d), F32),
                   jax.ShapeDtypeStruct((POOL_GROUPS, gw, gw), F32),
                   jax.ShapeDtypeStruct((_n_streams(nct), 8, d), F32)],
        in_specs=_pool_specs(tm, d, r) + _pool_specs(tm, d, r) + [
            pl.BlockSpec((tm, d), lambda i: (i, 0)),
            pl.BlockSpec((None, 8, d), lambda i: (_sid(i, nct), 0, 0)),
            _const_spec((1, d)), _const_spec(w.shape), _const_spec((1, d)),
            _const_spec((4, tm, kx)), _const_spec((4, tm, kx))],
        out_specs=[pl.BlockSpec((tm, d), lambda i: (jnp.maximum(i - out_off, 0), 0)),
                   pl.BlockSpec((POOL_GROUPS, gw, gw), lambda i: (0, 0, 0)),
                   pl.BlockSpec((None, 8, d), lambda i: (_stat_sid(i, nct), 0, 0))])
    return (*outs, rode)


def _rope_tables(c_len, l_len):
    half = HEAD_DIM // 2
    t = np.arange(l_len)
    row = (t // GRID_W).astype(np.float32)
    col = (t % GRID_W).astype(np.float32)
    inv = (np.float32(ROPE_BASE) ** (-np.arange(0, half, 2, dtype=np.float32) / np.float32(half))).astype(np.float32)
    ang_r = row[:, None] * inv[None, :]
    ang_c = col[:, None] * inv[None, :]
    cos = np.concatenate([np.cos(ang_r), np.cos(ang_r), np.cos(ang_c), np.cos(ang_c)], axis=1)
    sin = np.concatenate([-np.sin(ang_r), np.sin(ang_r), -np.sin(ang_c), np.sin(ang_c)], axis=1)
    cos = np.concatenate([np.ones((c_len, HEAD_DIM), np.float32), cos.astype(np.float32)], axis=0)
    sin = np.concatenate([np.zeros((c_len, HEAD_DIM), np.float32), sin.astype(np.float32)], axis=0)
    return jnp.asarray(cos, F32), jnp.asarray(sin, F32)


def _swap_pairs(x):
    lane = lax.broadcasted_iota(jnp.int32, x.shape, 1)
    return jnp.where((lane % 64) < 32, pltpu.roll(x, 96, 1), pltpu.roll(x, 32, 1))


def _head_norm(x, g):
    rstd = lax.rsqrt(jnp.mean(x * x, axis=-1, keepdims=True) + EPS)
    xhat = x * rstd
    return xhat * g, xhat, rstd


def _qkv_fwd(h, mods, ng, w, qg, kg, cos, sin, n_heads, n_kv, nct, tm):
    t_rows, d = h.shape
    qw, kw = n_heads * HEAD_DIM, n_kv * HEAD_DIM

    def body(h_ref, mod_ref, ng_ref, w_ref, qg_ref, kg_ref, cos_ref, sin_ref, q_ref, k_ref, v_ref):
        a = _normmod(h_ref[...], ng_ref[...], mod_ref[0:1, :], mod_ref[1:2, :])[0]
        qkv = _dot(a.astype(BF16), w_ref[...])
        cosv, sinv = cos_ref[...], sin_ref[...]
        ones = jnp.ones((tm, HEAD_DIM), BF16)
        for hd in range(n_heads + n_kv):
            cols = slice(hd * HEAD_DIM, (hd + 1) * HEAD_DIM)
            xn = _head_norm(qkv[:, cols], qg_ref[...] if hd < n_heads else kg_ref[...])[0]
            xr = xn * cosv + _swap_pairs(xn) * sinv
            if hd < n_heads:
                q_ref[:, cols] = (xr * Q_SCALE).astype(BF16)
            else:
                k_ref[:, (hd - n_heads) * HEAD_DIM:(hd - n_heads + 1) * HEAD_DIM] = xr.astype(BF16)
        for g in range(n_kv):
            v_ref[:, (2 * g) * HEAD_DIM:(2 * g + 1) * HEAD_DIM] = (
                qkv[:, qw + kw + g * HEAD_DIM:qw + kw + (g + 1) * HEAD_DIM].astype(BF16))
            v_ref[:, (2 * g + 1) * HEAD_DIM:(2 * g + 2) * HEAD_DIM] = ones

    return pl.pallas_call(
        body, name="qkv_fwd", grid=(t_rows // tm,),
        out_shape=[jax.ShapeDtypeStruct((t_rows - nct * tm, qw), BF16), jax.ShapeDtypeStruct((t_rows, kw), BF16),
                   jax.ShapeDtypeStruct((t_rows, 2 * kw), BF16)],
        in_specs=[pl.BlockSpec((tm, d), lambda i: (i, 0)),
                  pl.BlockSpec((None, 8, d), lambda i: (_sid(i, nct), 0, 0)),
                  _const_spec((1, d)), _const_spec(w.shape), _const_spec((1, HEAD_DIM)),
                  _const_spec((1, HEAD_DIM)),
                  pl.BlockSpec((tm, HEAD_DIM), lambda i: (i, 0)), pl.BlockSpec((tm, HEAD_DIM), lambda i: (i, 0))],
        out_specs=[pl.BlockSpec((tm, qw), lambda i: (jnp.maximum(i - nct, 0), 0)),
                   pl.BlockSpec((tm, kw), lambda i: (i, 0)), pl.BlockSpec((tm, 2 * kw), lambda i: (i, 0))],
        compiler_params=_params(("arbitrary",)),
    )(h, mods, ng, w, qg, kg, cos, sin)


def _flash_tk(t_rows, tm):
    best = tm
    k = tm
    while k <= FLASH_TK_CAP:
        if t_rows % k == 0:
            best = k
        k += tm
    return best


def _flash_fwd(q, k, v1, n_kv, tq, tm, rider=None):
    t_rows = k.shape[0]
    l_rows = q.shape[0]
    tk = _flash_tk(t_rows, tm)
    nk = t_rows // tk
    gq = 2 * HEAD_DIM

    def body(q_ref, k_ref, v_ref, o_ref, lse_ref, m_s, acc_s, s_s):
        ki = pl.program_id(2)

        @pl.when(ki == 0)
        def _():
            m_s[...] = jnp.full_like(m_s, -jnp.inf)
            acc_s[...] = jnp.zeros_like(acc_s)
        kk, vv = k_ref[...], v_ref[...]
        for hh in range(2):
            s_s[hh] = _dot_nt(q_ref[:, hh * HEAD_DIM:(hh + 1) * HEAD_DIM], kk)
        for hh in range(2):
            s = s_s[hh]
            m_prev = m_s[hh]
            m_new = jnp.maximum(m_prev, jnp.max(s, axis=-1, keepdims=True))
            alpha = jnp.exp2(m_prev - m_new)
            p = jnp.exp2(s - jnp.tile(m_new, (1, tk // HEAD_DIM)))
            acc_s[hh] = jnp.tile(alpha, (1, 2)) * acc_s[hh] + _dot(p.astype(BF16), vv)
            m_s[hh] = m_new

        @pl.when(ki == nk - 1)
        def _():
            for hh in range(2):
                acc = acc_s[hh]
                l = acc[:, HEAD_DIM:]
                o_ref[:, hh * HEAD_DIM:(hh + 1) * HEAD_DIM] = (acc[:, :HEAD_DIM] / l).astype(BF16)
                lse_ref[:, hh:hh + 1] = (m_s[hh] + jnp.log2(l))[:, 0:1]

    (o, lse), rode = _compute_call(
        body, name="flash_fwd", grid=(n_kv, l_rows // tq, nk), operands=(q, k, v1), rider=rider,
        out_shape=[jax.ShapeDtypeStruct((l_rows, n_kv * gq), BF16),
                   jax.ShapeDtypeStruct((n_kv, l_rows, 2), F32)],
        in_specs=[pl.BlockSpec((tq, gq), lambda g, i, j: (i, g)),
                  pl.BlockSpec((tk, HEAD_DIM), lambda g, i, j: (j, g)),
                  pl.BlockSpec((tk, gq), lambda g, i, j: (j, g))],
        out_specs=[pl.BlockSpec((tq, gq), lambda g, i, j: (i, g)),
                   pl.BlockSpec((None, tq, 2), lambda g, i, j: (g, i, 0))],
        scratch_shapes=[pltpu.VMEM((2, tq, HEAD_DIM), F32), pltpu.VMEM((2, tq, gq), F32),
                        pltpu.VMEM((2, tq, tk), F32)])
    return o, lse, rode


def _flash_bwd(q, k, v1, do, lse, delta, n_kv, tq, tm, rider=None):
    t_rows = k.shape[0]
    l_rows = q.shape[0]
    tk = _flash_tk(t_rows, tm)
    nq = l_rows // tq
    gq = 2 * HEAD_DIM

    def body(q_ref, k_ref, v_ref, do_ref, lse_ref, dl_ref, dq_ref, dk_ref, dv_ref):
        ki, qi = pl.program_id(1), pl.program_id(2)
        rows = pl.ds(pl.multiple_of(qi * tq, tq), tq)

        @pl.when(qi == 0)
        def _():
            dk_ref[...] = jnp.zeros_like(dk_ref)
            dv_ref[...] = jnp.zeros_like(dv_ref)

        @pl.when(ki == 0)
        def _():
            dq_ref[rows, :] = jnp.zeros((tq, gq), F32)
        kk, vv = k_ref[...], v_ref[:, :HEAD_DIM]
        dk_parts, dv_parts = [], []
        for hh in range(2):
            cols = slice(hh * HEAD_DIM, (hh + 1) * HEAD_DIM)
            qh, doh = q_ref[:, cols], do_ref[:, cols]
            p = jnp.exp2(_dot_nt(qh, kk) - lse_ref[:, hh:hh + 1])
            ds = (p * (_dot_nt(doh, vv) - dl_ref[:, hh:hh + 1])).astype(BF16)
            dv_parts.append(_dot_tn(p.astype(BF16), doh))
            dk_parts.append(_dot_tn(ds, qh))
            dq_ref[rows, cols] += _dot(ds, kk)
        dv_ref[...] += dv_parts[0] + dv_parts[1]
        dk_ref[...] += dk_parts[0] + dk_parts[1]

    (dq, dk, dv), rode = _compute_call(
        body, name="flash_bwd", grid=(n_kv, t_rows // tk, nq), operands=(q, k, v1, do, lse, delta), rider=rider,
        out_shape=[jax.ShapeDtypeStruct((l_rows, n_kv * gq), F32),
                   jax.ShapeDtypeStruct((t_rows, n_kv * HEAD_DIM), F32),
                   jax.ShapeDtypeStruct((t_rows, n_kv * HEAD_DIM), F32)],
        in_specs=[pl.BlockSpec((tq, gq), lambda g, j, i: (i, g)),
                  pl.BlockSpec((tk, HEAD_DIM), lambda g, j, i: (j, g)),
                  pl.BlockSpec((tk, gq), lambda g, j, i: (j, g)),
                  pl.BlockSpec((tq, gq), lambda g, j, i: (i, g)),
                  pl.BlockSpec((None, tq, 2), lambda g, j, i: (g, i, 0)),
                  pl.BlockSpec((None, tq, 2), lambda g, j, i: (g, i, 0))],
        out_specs=[pl.BlockSpec((l_rows, gq), lambda g, j, i: (0, g)),
                   pl.BlockSpec((tk, HEAD_DIM), lambda g, j, i: (j, g)),
                   pl.BlockSpec((tk, HEAD_DIM), lambda g, j, i: (j, g))])
    return dq, dk, dv, rode


def _wo_fwd(h, o, wo, mods, nct, tm, rider=None):
    l_rows, z = o.shape
    d = h.shape[1]

    def body(h_ref, o_ref, w_ref, mod_ref, y_ref, h1_ref):
        y = _dot(o_ref[...], w_ref[...])
        y_ref[...] = y.astype(BF16)
        h1_ref[...] = h_ref[...] + mod_ref[2:3, :] * y

    (y, h1), rode = _compute_call(
        body, name="wo_fwd", grid=(l_rows // tm,), operands=(h, o, wo, mods), rider=rider,
        out_shape=[jax.ShapeDtypeStruct((l_rows, d), BF16), jax.ShapeDtypeStruct((l_rows, d), F32)],
        in_specs=[pl.BlockSpec((tm, d), lambda i: (i + nct, 0)), pl.BlockSpec((tm, z), lambda i: (i, 0)),
                  _const_spec(wo.shape), pl.BlockSpec((None, 8, d), lambda i: (1, 0, 0))],
        out_specs=[pl.BlockSpec((tm, d), lambda i: (i, 0)), pl.BlockSpec((tm, d), lambda i: (i, 0))])
    return y, h1, rode


def _wo_bwd(dh1, y, o, wo, mods, n_kv, tm):
    l_rows, z = o.shape
    d = dh1.shape[1]

    def body(dh_ref, y_ref, o_ref, w_ref, mod_ref, dy_ref, do_ref, dl_ref, st_ref):
        i = pl.program_id(0)
        dh = dh_ref[...]
        dgate = _colsum(dh * y_ref[...].astype(F32))
        dy = (mod_ref[2:3, :] * dh).astype(BF16)
        dy_ref[...] = dy
        do = _dot_nt(dy, w_ref[...])
        do_ref[...] = do.astype(BF16)
        prod = do * o_ref[...].astype(F32)
        for g in range(n_kv):
            d0 = jnp.sum(prod[:, (2 * g) * HEAD_DIM:(2 * g + 1) * HEAD_DIM], axis=-1, keepdims=True)
            d1 = jnp.sum(prod[:, (2 * g + 1) * HEAD_DIM:(2 * g + 2) * HEAD_DIM], axis=-1, keepdims=True)
            dl_ref[g] = jnp.concatenate([d0, d1], axis=1)
        zero = jnp.zeros((1, d), F32)
        _acc_rows(st_ref, i == 0, [zero, zero, dgate])

    return pl.pallas_call(
        body, name="wo_bwd", grid=(l_rows // tm,),
        out_shape=[jax.ShapeDtypeStruct((l_rows, d), BF16), jax.ShapeDtypeStruct((l_rows, z), BF16),
                   jax.ShapeDtypeStruct((n_kv, l_rows, 2), F32), jax.ShapeDtypeStruct((8, d), F32)],
        in_specs=[pl.BlockSpec((tm, d), lambda i: (i, 0)), pl.BlockSpec((tm, d), lambda i: (i, 0)),
                  pl.BlockSpec((tm, z), lambda i: (i, 0)), _const_spec(wo.shape),
                  pl.BlockSpec((None, 8, d), lambda i: (1, 0, 0))],
        out_specs=[pl.BlockSpec((tm, d), lambda i: (i, 0)), pl.BlockSpec((tm, z), lambda i: (i, 0)),
                   pl.BlockSpec((n_kv, tm, 2), lambda i: (0, i, 0)), pl.BlockSpec((8, d), lambda i: (0, 0))],
        compiler_params=_params(("arbitrary",)),
    )(dh1, y, o, wo, mods)


def _qkv_bwd(h, dh_lat, dq, dk, dv, mods, ng, w, qg, kg, cos, sin, n_heads, n_kv, nct, tm):
    t_rows, d = h.shape
    qw, kw = n_heads * HEAD_DIM, n_kv * HEAD_DIM
    scale = HEAD_DIM ** -0.5

    def body(h_ref, dhl_ref, dq_ref, dk_ref, dv_ref, mod_ref, ng_ref, w_ref, qg_ref, kg_ref, cos_ref,
             sin_ref, dh_ref, a_ref, dqkv_ref, st_ref, dg_ref):
        i = pl.program_id(0)
        lat = (i >= nct).astype(F32)
        ngv, sc = ng_ref[...], mod_ref[1:2, :]
        a, xhat, rstd, n = _normmod(h_ref[...], ngv, mod_ref[0:1, :], sc)
        ab = a.astype(BF16)
        a_ref[...] = ab
        qkv = _dot(ab, w_ref[...])
        cosv, sinv = cos_ref[...], sin_ref[...]
        dqg = jnp.zeros((1, HEAD_DIM), F32)
        dkg = jnp.zeros((1, HEAD_DIM), F32)
        for hd in range(n_heads + n_kv):
            cols = slice(hd * HEAD_DIM, (hd + 1) * HEAD_DIM)
            is_q = hd < n_heads
            g = qg_ref[...] if is_q else kg_ref[...]
            _, hx, hr = _head_norm(qkv[:, cols], g)
            if is_q:
                dxr = dq_ref[:, cols] * (scale * lat)
            else:
                dxr = dk_ref[:, (hd - n_heads) * HEAD_DIM:(hd - n_heads + 1) * HEAD_DIM] * (1.0 / LOG2E)
            dxn = dxr * cosv + _swap_pairs(dxr * sinv)
            if is_q:
                dqg = dqg + _colsum(dxn * hx)
            else:
                dkg = dkg + _colsum(dxn * hx)
            dxh = dxn * g
            dx = hr * (dxh - hx * jnp.mean(dxh * hx, axis=-1, keepdims=True))
            dqkv_ref[:, cols] = dx.astype(BF16)
        dqkv_ref[:, qw + kw:] = dv_ref[...].astype(BF16)
        da = _dot_nt(dqkv_ref[...], w_ref[...])
        dhn, dsh, dsc, dng = _normmod_bwd(da, xhat, rstd, n, ngv, sc)
        dh_ref[...] = dhl_ref[...] * lat + dhn
        _acc_rows(st_ref, _first_of_stream(i, nct), [dsh, dsc, jnp.zeros((1, d), F32), dng])
        _acc_rows(dg_ref, i == 0, [dqg, dkg])

    lat_map = lambda i: (jnp.maximum(i - nct, 0), 0)
    return pl.pallas_call(
        body, name="qkv_bwd", grid=(t_rows // tm,),
        out_shape=[jax.ShapeDtypeStruct((t_rows, d), F32), jax.ShapeDtypeStruct((t_rows, d), BF16),
                   jax.ShapeDtypeStruct((t_rows, qw + 2 * kw), BF16), jax.ShapeDtypeStruct((2, 8, d), F32),
                   jax.ShapeDtypeStruct((8, HEAD_DIM), F32)],
        in_specs=[pl.BlockSpec((tm, d), lambda i: (i, 0)), pl.BlockSpec((tm, d), lat_map),
                  pl.BlockSpec((tm, qw), lat_map), pl.BlockSpec((tm, kw), lambda i: (i, 0)),
                  pl.BlockSpec((tm, kw), lambda i: (i, 0)),
                  pl.BlockSpec((None, 8, d), lambda i: (_sid(i, nct), 0, 0)),
                  _const_spec((1, d)), _const_spec(w.shape), _const_spec((1, HEAD_DIM)),
                  _const_spec((1, HEAD_DIM)),
                  pl.BlockSpec((tm, HEAD_DIM), lambda i: (i, 0)), pl.BlockSpec((tm, HEAD_DIM), lambda i: (i, 0))],
        out_specs=[pl.BlockSpec((tm, d), lambda i: (i, 0)), pl.BlockSpec((tm, d), lambda i: (i, 0)),
                   pl.BlockSpec((tm, qw + 2 * kw), lambda i: (i, 0)),
                   pl.BlockSpec((None, 8, d), lambda i: (_sid(i, nct), 0, 0)),
                   pl.BlockSpec((8, HEAD_DIM), lambda i: (0, 0))],
        compiler_params=_params(("arbitrary",)),
    )(h, dh_lat, dq, dk, dv, mods, ng, w, qg, kg, cos, sin)


def _gmlp_core(a_bf, win_ref, lng, lnb, ws_ref, bst_ref, tm, half, with_grad=False):
    z = _dot(a_bf, win_ref[...])
    zu, zv = z[:, :half], z[:, half:]
    if with_grad:
        (u, zu), (v, zv) = _gelu_and_grad(zu), _gelu_and_grad(zv)
    else:
        u, v, zu, zv = _gelu(zu), _gelu(zv), None, None
    mu = jnp.mean(v, axis=-1, keepdims=True)
    vc = v - mu
    rstd_v = lax.rsqrt(jnp.mean(vc * vc, axis=-1, keepdims=True) + EPS)
    vhat = vc * rstd_v
    vln = (vhat * lng + lnb).astype(BF16)
    gw = half // GMLP_GROUPS
    rows = []
    for ch in range(tm // CHUNK):
        rs = slice(ch * CHUNK, (ch + 1) * CHUNK)
        cols = []
        for g in range(GMLP_GROUPS):
            cs = slice(g * gw, (g + 1) * gw)
            cols.append(_dot(ws_ref[g], vln[rs, cs]) + bst_ref[:, g:g + 1])
        rows.append(jnp.concatenate(cols, axis=1))
    sv = rows[0] if len(rows) == 1 else jnp.concatenate(rows, axis=0)
    return zu, zv, u, vhat, rstd_v, vln, sv


def _gmlp_fwd(h, mods, ng, win, lng, lnb, ws, bst, wout, tm):
    l_rows, d = h.shape
    half = wout.shape[0]

    def body(h_ref, mod_ref, ng_ref, win_ref, lng_ref, lnb_ref, ws_ref, bst_ref, wout_ref, y_ref, h1_ref):
        hv = h_ref[...]
        a = _normmod(hv, ng_ref[...], mod_ref[0:1, :], mod_ref[1:2, :])[0]
        _, _, u, _, _, _, sv = _gmlp_core(a.astype(BF16), win_ref, lng_ref[...], lnb_ref[...], ws_ref,
                                          bst_ref, tm, half)
        y = _dot((u * sv).astype(BF16), wout_ref[...])
        y_ref[...] = y.astype(BF16)
        h1_ref[...] = hv + mod_ref[2:3, :] * y

    return pl.pallas_call(
        body, name="gmlp_fwd", grid=(l_rows // tm,),
        out_shape=[jax.ShapeDtypeStruct((l_rows, d), BF16), jax.ShapeDtypeStruct((l_rows, d), F32)],
        in_specs=[pl.BlockSpec((tm, d), lambda i: (i, 0)), pl.BlockSpec((None, 8, d), lambda i: (1, 0, 0)),
                  _const_spec((1, d)), _const_spec(win.shape), _const_spec((1, half)), _const_spec((1, half)),
                  _const_spec(ws.shape), _const_spec(bst.shape), _const_spec(wout.shape)],
        out_specs=[pl.BlockSpec((tm, d), lambda i: (i, 0)), pl.BlockSpec((tm, d), lambda i: (i, 0))],
        compiler_params=_params(("arbitrary",)),
    )(h, mods, ng, win, lng, lnb, ws, bst, wout)


def _gmlp_bwd(dh1, h, y, mods, ng, win, lng, lnb, ws, wst, bst, wout, tm):
    l_rows, d = h.shape
    half = wout.shape[0]
    gw = half // GMLP_GROUPS

    def body(dh_ref, h_ref, y_ref, mod_ref, ng_ref, win_ref, lng_ref, lnb_ref, ws_ref, wst_ref, bst_ref,
             wout_ref, dho_ref, a_ref, dz_ref, gt_ref, dy_ref, st_ref, ln_ref, dws_ref, dbs_ref):
        i = pl.program_id(0)
        ngv, sc = ng_ref[...], mod_ref[1:2, :]
        lngv = lng_ref[...]
        a, xhat, rstd, n = _normmod(h_ref[...], ngv, mod_ref[0:1, :], sc)
        ab = a.astype(BF16)
        a_ref[...] = ab
        gu, gv, u, vhat, rstd_v, vln, sv = _gmlp_core(ab, win_ref, lngv, lnb_ref[...], ws_ref, bst_ref,
                                                      tm, half, with_grad=True)
        gt_ref[...] = (u * sv).astype(BF16)
        dh = dh_ref[...]
        dgate = _colsum(dh * y_ref[...].astype(F32))
        dy = (mod_ref[2:3, :] * dh).astype(BF16)
        dy_ref[...] = dy
        dgated = _dot_nt(dy, wout_ref[...])
        du = dgated * sv
        dsv = (dgated * u).astype(BF16)

        @pl.when(i == 0)
        def _():
            dws_ref[...] = jnp.zeros_like(dws_ref)
            dbs_ref[...] = jnp.zeros_like(dbs_ref)
        lane = lax.broadcasted_iota(jnp.int32, (CHUNK, 128), 1)
        dbs = jnp.zeros((CHUNK, 128), F32)
        rows = []
        for ch in range(tm // CHUNK):
            rs = slice(ch * CHUNK, (ch + 1) * CHUNK)
            cols = []
            for g in range(GMLP_GROUPS):
                cs = slice(g * gw, (g + 1) * gw)
                dsv_cg = dsv[rs, cs]
                dws_ref[g] += _dot_nt(dsv_cg, vln[rs, cs])
                cols.append(_dot(wst_ref[g], dsv_cg))
                dbs = dbs + jnp.where(lane == g, jnp.sum(dsv_cg.astype(F32), axis=-1, keepdims=True), 0.0)
            rows.append(jnp.concatenate(cols, axis=1))
        dbs_ref[...] += dbs
        dvln = rows[0] if len(rows) == 1 else jnp.concatenate(rows, axis=0)
        dlng = _colsum(dvln * vhat)
        dlnb = _colsum(dvln)
        dvh = dvln * lngv
        dv = rstd_v * (dvh - jnp.mean(dvh, axis=-1, keepdims=True)
                       - vhat * jnp.mean(dvh * vhat, axis=-1, keepdims=True))
        dz_ref[:, :half] = (du * gu).astype(BF16)
        dz_ref[:, half:] = (dv * gv).astype(BF16)
        da = _dot_nt(dz_ref[...], win_ref[...])
        dhn, dsh, dsc, dng = _normmod_bwd(da, xhat, rstd, n, ngv, sc)
        dho_ref[...] = dh + dhn
        _acc_rows(st_ref, i == 0, [dsh, dsc, dgate, dng])
        _acc_rows(ln_ref, i == 0, [dlng, dlnb])

    row = lambda w: pl.BlockSpec((tm, w), lambda i: (i, 0))
    return pl.pallas_call(
        body, name="gmlp_bwd", grid=(l_rows // tm,),
        out_shape=[jax.ShapeDtypeStruct((l_rows, d), F32), jax.ShapeDtypeStruct((l_rows, d), BF16),
                   jax.ShapeDtypeStruct((l_rows, 2 * half), BF16), jax.ShapeDtypeStruct((l_rows, half), BF16),
                   jax.ShapeDtypeStruct((l_rows, d), BF16), jax.ShapeDtypeStruct((8, d), F32),
                   jax.ShapeDtypeStruct((8, half), F32), jax.ShapeDtypeStruct(ws.shape, F32),
                   jax.ShapeDtypeStruct((CHUNK, 128), F32)],
        in_specs=[row(d), row(d), row(d), pl.BlockSpec((None, 8, d), lambda i: (1, 0, 0)),
                  _const_spec((1, d)), _const_spec(win.shape), _const_spec((1, half)), _const_spec((1, half)),
                  _const_spec(ws.shape), _const_spec(ws.shape), _const_spec(bst.shape), _const_spec(wout.shape)],
        out_specs=[row(d), row(d), row(2 * half), row(half), row(d),
                   pl.BlockSpec((8, d), lambda i: (0, 0)), pl.BlockSpec((8, half), lambda i: (0, 0)),
                   pl.BlockSpec(ws.shape, lambda i: (0, 0, 0)), pl.BlockSpec((CHUNK, 128), lambda i: (0, 0))],
        compiler_params=_params(("arbitrary",)),
    )(dh1, h, y, mods, ng, win, lng, lnb, ws, wst, bst, wout)


def _head(h, final_g, target, tm):
    l_rows, d = h.shape
    n_tiles = l_rows // tm

    def body(h_ref, g_ref, t_ref, dh_ref, loss_ref, dg_ref, acc_ref):
        i = pl.program_id(0)
        g = g_ref[...]
        hv = h_ref[...]
        rstd = lax.rsqrt(jnp.mean(hv * hv, axis=-1, keepdims=True) + EPS)
        xhat = hv * rstd
        e = xhat * g - t_ref[...]
        dout = e * (1.0 / d)
        dxhat = dout * g
        dh_ref[...] = rstd * (dxhat - xhat * jnp.mean(dxhat * xhat, axis=-1, keepdims=True))
        _acc_rows(dg_ref, i == 0, [_colsum(dout * xhat)])
        _acc_rows(acc_ref, i == 0, [_colsum(e * e)])

        @pl.when(i == n_tiles - 1)
        def _():
            total = jnp.sum(acc_ref[0:1, :], axis=-1, keepdims=True) * (0.5 / d)
            loss_ref[...] = jnp.broadcast_to(total, loss_ref.shape)

    return pl.pallas_call(
        body, name="loss_head", grid=(n_tiles,),
        out_shape=[jax.ShapeDtypeStruct((l_rows, d), F32), jax.ShapeDtypeStruct((8, 128), F32),
                   jax.ShapeDtypeStruct((8, d), F32)],
        in_specs=[pl.BlockSpec((tm, d), lambda i: (i, 0)), _const_spec((1, d)),
                  pl.BlockSpec((tm, d), lambda i: (i, 0))],
        out_specs=[pl.BlockSpec((tm, d), lambda i: (i, 0)), pl.BlockSpec((8, 128), lambda i: (0, 0)),
                   pl.BlockSpec((8, d), lambda i: (0, 0))],
        scratch_shapes=[pltpu.VMEM((8, d), F32)],
        compiler_params=_params(("arbitrary",)),
    )(h, final_g, target)


def _adamw(w, gparts, m, v, name):
    shape = w.shape
    cols = shape[-1]
    rows = int(np.prod(shape[:-1])) if len(shape) > 1 else 1
    pieces = list(gparts) if isinstance(gparts, (list, tuple)) else [gparts]
    n_pieces = len(pieces)
    nparts = pieces[0].shape[0]
    piece_rows = rows // n_pieces
    w2, m2, v2 = (t.reshape(rows, cols) for t in (w, m, v))
    pieces = [g.reshape(nparts, piece_rows, cols) for g in pieces]
    tr = piece_rows
    part_bytes = nparts * cols * pieces[0].dtype.itemsize
    for cand in (1024, 512, 256, 128, 64, 32, 16, 8):
        if piece_rows * max(part_bytes, cols * 4) <= (2 << 20):
            break
        if piece_rows % cand == 0 and cand < piece_rows:
            tr = cand
            if cand * max(part_bytes, cols * 4) <= (2 << 20):
                break
    per_piece = piece_rows // tr
    c1 = 1.0 - ADAM_B1 ** ADAM_STEP
    c2 = 1.0 - ADAM_B2 ** ADAM_STEP

    def update(w_ref, g_ref, m_ref, v_ref, go_ref, d_ref, mo_ref, vo_ref):
        g = g_ref[0].astype(F32)
        for k in range(1, nparts):
            g = g + g_ref[k].astype(F32)
        mn = ADAM_B1 * m_ref[...] + (1.0 - ADAM_B1) * g
        vn = ADAM_B2 * v_ref[...] + (1.0 - ADAM_B2) * (g * g)
        go_ref[...] = g
        mo_ref[...] = mn
        vo_ref[...] = vn
        d_ref[...] = -ADAM_LR * ((mn / c1) / (jnp.sqrt(vn / c2) + ADAM_EPS) + ADAM_WD * w_ref[...])

    def body(w_ref, *refs):
        g_refs, (m_ref, v_ref, go_ref, d_ref, mo_ref, vo_ref) = refs[:n_pieces], refs[n_pieces:]
        if n_pieces == 1:
            update(w_ref, g_refs[0], m_ref, v_ref, go_ref, d_ref, mo_ref, vo_ref)
        else:
            piece = pl.program_id(0) // per_piece
            for k in range(n_pieces):
                pl.when(piece == k)(functools.partial(update, w_ref, g_refs[k], m_ref, v_ref, go_ref, d_ref, mo_ref, vo_ref))

    def piece_spec(k):
        return pl.BlockSpec((nparts, tr, cols), lambda i: (0, jnp.clip(i - k * per_piece, 0, per_piece - 1), 0))

    spec = pl.BlockSpec((tr, cols), lambda i: (i, 0))
    outs = pl.pallas_call(
        body, name=name, grid=(rows // tr,),
        out_shape=[jax.ShapeDtypeStruct((rows, cols), F32)] * 4,
        in_specs=[spec] + [piece_spec(k) for k in range(n_pieces)] + [spec, spec],
        out_specs=[spec] * 4,
        compiler_params=_params(("arbitrary",)),
    )(w2, *pieces, m2, v2)
    return tuple(o.reshape(shape) for o in outs)


def _natural_cols(g):
    return jnp.moveaxis(g, 0, -2).reshape(g.shape[1:-1] + (N_DEV * g.shape[-1],))


def _natural_rows(g):
    return jnp.moveaxis(g, 0, -3).reshape(g.shape[1:-2] + (N_DEV * g.shape[-2], g.shape[-1]))


def _shard_rows(full):
    r = full.shape[-2] // N_DEV
    return jnp.moveaxis(full.reshape(full.shape[:-2] + (N_DEV, r, full.shape[-1])), -3, 0)


def _my_cols(gathered, me, n):
    return lax.dynamic_slice_in_dim(gathered, me * n, n, axis=gathered.ndim - 1)


def kernel(x, c, ctx, c_ctx, ada_w, ada_b, norm_g, mlp_w1, mlp_w2, pool_w, pool_scale, attn_w_qkv, attn_w_o, attn_q_g, attn_k_g, gm_w_in, gm_ln_g, gm_ln_b, gm_ws, gm_bs, gm_w_out, final_g, loss_target, m_c_ctx, m_ada_w, m_ada_b, m_norm_g, m_mlp_w1, m_mlp_w2, m_pool_w, m_pool_scale, m_attn_w_qkv, m_attn_w_o, m_attn_q_g, m_attn_k_g, m_gm_w_in, m_gm_ln_g, m_gm_ln_b, m_gm_ws, m_gm_bs, m_gm_w_out, m_final_g, v_c_ctx, v_ada_w, v_ada_b, v_norm_g, v_mlp_w1, v_mlp_w2, v_pool_w, v_pool_scale, v_attn_w_qkv, v_attn_w_o, v_attn_q_g, v_attn_k_g, v_gm_w_in, v_gm_ln_g, v_gm_ln_b, v_gm_ws, v_gm_bs, v_gm_w_out, v_final_g):
    l_len, d = x.shape[1], x.shape[2]
    c_len = ctx.shape[1]
    n_layers = ada_w.shape[0]
    assert n_layers == 4 and x.shape[0] == 1
    n_heads = d // HEAD_DIM
    n_kv = n_heads // 2
    half = gm_w_out.shape[1] * N_DEV
    tm = c_len if c_len <= 256 else 256
    assert c_len % tm == 0 and l_len % tm == 0 and tm % CHUNK == 0 and l_len % GRID_W == 0
    nct = c_len // tm
    me = _dev_index(*_coords())
    n_ada = ada_w.shape[-1]

    first = [t.astype(BF16) for t in (mlp_w1[0], mlp_w2[0], pool_w, attn_w_qkv[0])]
    small = [c, norm_g.reshape(n_layers * 2, -1), pool_scale, gm_ln_g, gm_ln_b]
    w1_0g, w2_0g, pool_g, qkv_g, c_all, ng_g, ps_g, lng_g, lnb_g = _all_gather(first + small, "gather_first")
    c_all = c_all.reshape(N_DEV, d)
    later = _GatherAcrossChips([t.astype(BF16) for t in
                                (mlp_w1[1], mlp_w1[2], mlp_w1[3], mlp_w2[1], mlp_w2[2], mlp_w2[3],
                                 attn_w_o[0], gm_w_in[0], gm_w_out[0])])
    pool_wf = _natural_rows(pool_g)
    wqkv = _natural_cols(qkv_g)
    ng_full = _natural_cols(ng_g.reshape(N_DEV, n_layers * 2, 1, -1)).reshape(n_layers, 2, 1, d)
    ps_full = _natural_cols(ps_g.reshape(N_DEV, 2, 1, -1))
    lng_full = _natural_cols(lng_g.reshape(N_DEV, 1, -1))
    lnb_full = _natural_cols(lnb_g.reshape(N_DEV, 1, -1))

    c_ctx2 = c_ctx.reshape(1, d)
    ada_b_loc = lax.dynamic_slice_in_dim(ada_b, me * n_ada, n_ada, axis=1).reshape(n_layers, 1, n_ada)
    (mod_g,) = _all_gather([_mods_local(c_all, c_ctx2, ada_w, ada_b_loc)], "gather_mods")
    mod_full = jnp.moveaxis(mod_g, 0, 2).reshape(n_layers, 16, 6, d)
    mod_lat = lax.dynamic_index_in_dim(mod_full, me, axis=1, keepdims=False)
    mod_ctx = mod_full[:, 8]
    mods = jnp.stack([mod_ctx, mod_lat], axis=1)
    mods = jnp.concatenate([mods, jnp.zeros((n_layers, 2, 2, d), F32)], axis=2)

    bands = _pool_bands(tm)
    cos, sin = _rope_tables(c_len, l_len)
    ws_bf = gm_ws[0].astype(BF16)
    wst_bf = jnp.swapaxes(gm_ws[0], 1, 2).astype(BF16)
    bst = jnp.zeros((CHUNK, 128), F32).at[:, :GMLP_GROUPS].set(gm_bs[0].T)
    ng = lambda i, j: ng_full[i, j]

    h0 = jnp.concatenate([ctx[0], x[0]], axis=0)
    y0, h1 = _pool_fwd(h0, mods[0], ng(0, 0), pool_wf[0].astype(BF16), ps_full[0], bands, nct, tm, c_len, l_len)
    h2, p0, ym0 = _mlp_fwd(h1, mods[0], ng(0, 1), w1_0g, w2_0g, nct, tm)
    q, k, v1 = _qkv_fwd(h2, mods[1], ng(1, 0), wqkv, attn_q_g, attn_k_g, cos, sin, n_heads, n_kv, nct, tm)
    o, lse, later_g = _flash_fwd(q, k, v1, n_kv, 4 * tm, tm, rider=later)
    (wo_g,) = _exchange_call(_ForwardToSibling(later_g[6:7]), "forward_wo")
    wo = _natural_rows(wo_g)
    y1, h3, rest_g = _wo_fwd(h2, o, wo, mods[1], nct, tm, rider=_ForwardToSibling(later_g[0:6] + later_g[7:9]))
    w1 = [w1_0g] + rest_g[0:3]
    w2 = [w2_0g] + rest_g[3:6]
    win = _natural_cols(rest_g[6])
    wout = _natural_rows(rest_g[7])
    tm_lat = 2 * tm
    h4, p1, ym1 = _mlp_fwd(h3, mods[1], ng(1, 1), w1[1], w2[1], 0, tm_lat)
    y2, h5 = _gmlp_fwd(h4, mods[2], ng(2, 0), win, lng_full, lnb_full, ws_bf, bst, wout, tm)
    h6, p2, ym2 = _mlp_fwd(h5, mods[2], ng(2, 1), w1[2], w2[2], 0, tm_lat)
    y3, h7 = _pool_fwd(h6, mods[3], ng(3, 0), pool_wf[1].astype(BF16), ps_full[1], bands, 0, tm, c_len, l_len)
    h8, p3, ym3 = _mlp_fwd(h7, mods[3], ng(3, 1), w1[3], w2[3], 0, tm_lat)
    dh, loss_part, dfinal = _head(h8, final_g.reshape(1, d), loss_target[0], tm_lat)

    dw1, dw2, st_mlp = [None] * 4, [None] * 4, [None] * 4

    def mlp_back(i, dh, h_in, p, ym, nct_i):
        dh_in, m_bf, du, dacc, st, _ = _mlp_bwd(dh, h_in, p, ym, mods[i], ng(i, 1), w1[i], w2[i], nct_i, tm_lat)
        dw1[i] = _tn_matmul(m_bf, du, "tn_w1", col_shards=True)
        dw2[i] = _shard_rows(_tn_matmul(p, dacc, "tn_w2", square_x=True))
        st_mlp[i] = st
        return dh_in

    dh = mlp_back(3, dh, h7, p3, ym3, 0)
    dh, dpw1, st_pool3, _ = _pool_bwd(dh, h6, y3, mods[3], ng(3, 0), pool_wf[1].astype(BF16), ps_full[1], bands,
                                      0, tm, c_len, l_len, False)
    dh = mlp_back(2, dh, h5, p2, ym2, 0)
    dh, a_bf, dz, gated, dy, st_g, st_ln, dws, dbst = _gmlp_bwd(
        dh, h4, y2, mods[2], ng(2, 0), win, lng_full, lnb_full, ws_bf, wst_bf, bst, wout, tm)
    dwin = _tn_matmul(a_bf, dz, "tn_gm_in", col_shards=True)
    dwout = _tn_matmul(gated, dy, "tn_gm_out")
    dh = mlp_back(1, dh, h3, p1, ym1, 0)
    dy1, do, delta, st_wo = _wo_bwd(dh, y1, o, wo, mods[1], n_kv, tm)
    dwo = _tn_matmul(o, dy1, "tn_wo")
    grads_mid = _AllToAll(dw1[1:] + dw2[1:] + [_shard_rows(dpw1.astype(BF16)), _shard_rows(dwo), dwin,
                                               _shard_rows(dwout)])
    dq, dk, dv, rode = _flash_bwd(q, k, v1, do, lse, delta, n_kv, 2 * tm, tm, rider=grads_mid)
    g_w1, g_w2, (g_pool1, g_wo, g_gin, g_gout) = [None] + rode[0:3], [None] + rode[3:6], rode[6:]
    dh, a_bf, dqkv, st_q, dgains = _qkv_bwd(h2, dh, dq, dk, dv, mods[1], ng(1, 0), wqkv, attn_q_g, attn_k_g,
                                            cos, sin, n_heads, n_kv, nct, tm)
    dwqkv = _tn_matmul(a_bf, dqkv, "tn_qkv", col_shards=True)
    dh, m_bf, du, dacc, st_mlp[0], (g_qkv,) = _mlp_bwd(dh, h1, p0, ym0, mods[0], ng(0, 1), w1[0], w2[0], nct, tm,
                                                       rider=_AllToAll([dwqkv]))
    dw1_0 = _tn_matmul(m_bf, du, "tn_w1", col_shards=True)
    dw2_0, (g_w1[0],) = _tn_matmul(p0, dacc, "tn_w2", square_x=True, rider=_AllToAll([dw1_0]))
    grad_x, dpw0, st_pool0, (g_w2[0],) = _pool_bwd(dh, h0, y0, mods[0], ng(0, 0), pool_wf[0].astype(BF16),
                                                   ps_full[0], bands, nct, tm, c_len, l_len, True,
                                                   rider=_AllToAll([_shard_rows(dw2_0)]))

    mix_lat = [st_pool0[-1], st_q[1] + st_wo, st_g, st_pool3[-1]]
    mlp_lat = [st[-1] for st in st_mlp]
    dmod_lat = jnp.stack([jnp.concatenate([mix_lat[i][0:3], mlp_lat[i][0:3]]) for i in range(n_layers)])
    dmod_ctx = jnp.stack([jnp.concatenate([st_pool0[0][0:3], st_mlp[0][0][0:3]]),
                          jnp.concatenate([st_q[0][0:2], jnp.zeros((4, d), F32)]),
                          jnp.zeros((6, d), F32), jnp.zeros((6, d), F32)])
    dng_part = jnp.stack([jnp.stack([mix_lat[0][3] + st_pool0[0][3], mlp_lat[0][3] + st_mlp[0][0][3]]),
                          jnp.stack([mix_lat[1][3] + st_q[0][3], mlp_lat[1][3]]),
                          jnp.stack([mix_lat[2][3], mlp_lat[2][3]]),
                          jnp.stack([mix_lat[3][3], mlp_lat[3][3]])])
    dps_part = jnp.stack([mix_lat[0][4] + st_pool0[0][4], mix_lat[3][4]])
    small_parts = [dmod_lat.reshape(n_layers * 6, d), dmod_ctx.reshape(n_layers * 6, d),
                   dng_part.reshape(n_layers * 2, d), dps_part, st_ln, dgains, dws.reshape(-1, CHUNK),
                   dbst, dfinal, loss_part]
    (gm_lat, gm_ctx, g_ng, g_ps, g_ln, g_gains, g_ws, g_bst, g_final, loss_all, g_pool0) = _all_gather(
        small_parts, "gather_small_grads", extra=_AllToAll([_shard_rows(dpw0.astype(BF16))]))
    g_pool = jnp.stack([g_pool0, g_pool1], axis=1)

    gm_lat4 = gm_lat.reshape(N_DEV, n_layers, 6 * d)
    gm_ctx4 = gm_ctx.reshape(N_DEV, n_layers, 6 * d)
    dm_lat_loc = jnp.moveaxis(_my_cols(gm_lat4, me, n_ada), 0, 1)
    dm_ctx_loc = jnp.moveaxis(_my_cols(gm_ctx4, me, n_ada), 0, 1)
    g_ada_w, ds_part = _ada_grads(c_all, c_ctx2, ada_w, dm_lat_loc, dm_ctx_loc)
    (ds_all,) = _all_gather([ds_part], "gather_dsctx")
    g_c_ctx, loss_sum = _cctx_grad_and_loss(ds_all, c_ctx2, loss_all)
    g_c_ctx = g_c_ctx.reshape(d)

    n_ng = norm_g.shape[-1]
    n_ps = pool_scale.shape[-1]
    n_ln = gm_ln_g.shape[-1]
    gparts = {
        "c_ctx": g_c_ctx[None],
        "ada_w": g_ada_w[None],
        "ada_b": jnp.concatenate([gm_lat4, gm_ctx4], axis=0),
        "norm_g": _my_cols(g_ng.reshape(N_DEV, n_layers, 2, d), me, n_ng),
        "mlp_w1": g_w1, "mlp_w2": g_w2, "pool_w": g_pool,
        "pool_scale": _my_cols(g_ps, me, n_ps),
        "attn_w_qkv": g_qkv[:, None], "attn_w_o": g_wo[:, None],
        "attn_q_g": g_gains[:, 0:1], "attn_k_g": g_gains[:, 1:2],
        "gm_w_in": g_gin[:, None],
        "gm_ln_g": _my_cols(g_ln[:, 0:1], me, n_ln), "gm_ln_b": _my_cols(g_ln[:, 1:2], me, n_ln),
        "gm_ws": g_ws.reshape((N_DEV,) + gm_ws.shape),
        "gm_bs": jnp.swapaxes(g_bst[:, :, :GMLP_GROUPS], 1, 2)[:, None],
        "gm_w_out": g_gout[:, None],
        "final_g": g_final[:, 0],
    }
    weights = dict(c_ctx=(c_ctx, m_c_ctx, v_c_ctx), ada_w=(ada_w, m_ada_w, v_ada_w), ada_b=(ada_b, m_ada_b, v_ada_b),
                   norm_g=(norm_g, m_norm_g, v_norm_g), mlp_w1=(mlp_w1, m_mlp_w1, v_mlp_w1),
                   mlp_w2=(mlp_w2, m_mlp_w2, v_mlp_w2), pool_w=(pool_w, m_pool_w, v_pool_w),
                   pool_scale=(pool_scale, m_pool_scale, v_pool_scale),
                   attn_w_qkv=(attn_w_qkv, m_attn_w_qkv, v_attn_w_qkv), attn_w_o=(attn_w_o, m_attn_w_o, v_attn_w_o),
                   attn_q_g=(attn_q_g, m_attn_q_g, v_attn_q_g), attn_k_g=(attn_k_g, m_attn_k_g, v_attn_k_g),
                   gm_w_in=(gm_w_in, m_gm_w_in, v_gm_w_in), gm_ln_g=(gm_ln_g, m_gm_ln_g, v_gm_ln_g),
                   gm_ln_b=(gm_ln_b, m_gm_ln_b, v_gm_ln_b), gm_ws=(gm_ws, m_gm_ws, v_gm_ws),
                   gm_bs=(gm_bs, m_gm_bs, v_gm_bs), gm_w_out=(gm_w_out, m_gm_w_out, v_gm_w_out),
                   final_g=(final_g, m_final_g, v_final_g))
    grads, deltas, new_m, new_v = [], [], [], []
    for wname, (w_, m_, v_) in weights.items():
        g_, d_, nm_, nv_ = _adamw(w_, gparts[wname], m_, v_, "adamw_" + wname)
        grads.append(g_)
        deltas.append(d_)
        new_m.append(nm_)
        new_v.append(nv_)

    return (loss_sum[0, 0], grad_x[None], *grads, *deltas, *new_m, *new_v)
```

```python
import functools
import math

import numpy as np
import jax
import jax.numpy as jnp
from jax import lax
from jax.experimental import pallas as pl
from jax.experimental.pallas import tpu as pltpu

F32 = jnp.float32
BF16 = jnp.bfloat16
MESH_ID = pl.DeviceIdType.MESH

N_DEV = 8
EPS = 1e-6
HEAD_DIM = 128
GRID_W = 64
ROPE_BASE = 10000.0
CHUNK = 128
POOL_WINDOWS = (2, 4, 8, 16)
POOL_GROUPS = 4
POOL_HALO = 8
GMLP_GROUPS = 8
ADAM_LR, ADAM_B1, ADAM_B2, ADAM_EPS, ADAM_WD, ADAM_STEP = 0.001, 0.9, 0.999, 1e-08, 0.01, 10

V7X_VMEM_BYTES = 64 << 20
VMEM_LIMIT_BIG = V7X_VMEM_BYTES - (8 << 20)
FLASH_TK_CAP = 768
LOG2E = math.log2(math.e)
Q_SCALE = HEAD_DIM ** -0.5 * LOG2E


def _params(sem, vmem=VMEM_LIMIT_BIG):
    return pltpu.CompilerParams(dimension_semantics=sem, vmem_limit_bytes=vmem)


def _const_spec(shape):
    nd = len(shape)
    return pl.BlockSpec(shape, lambda *_: (0,) * nd, pipeline_mode=pl.Buffered(1))


def _dot(a, b):
    return jnp.dot(a, b, preferred_element_type=F32)


def _dot_nt(a, b):
    return lax.dot_general(a, b, (((1,), (1,)), ((), ())), preferred_element_type=F32)


def _dot_tn(a, b):
    return lax.dot_general(a, b, (((0,), (0,)), ((), ())), preferred_element_type=F32)


def _colsum(x):
    return jnp.sum(x, axis=0, keepdims=True)


def _sid(i, nct):
    if nct == 0:
        return 1
    return jnp.where(i >= nct, 1, 0)


def _n_streams(nct):
    return 2 if nct else 1


def _stat_sid(i, nct):
    return _sid(i, nct) if nct else 0


def _first_of_stream(i, nct):
    if nct == 0:
        return i == 0
    return jnp.logical_or(i == 0, i == nct)


def _normmod(h, ng, sh, sc):
    rstd = lax.rsqrt(jnp.mean(h * h, axis=-1, keepdims=True) + EPS)
    xhat = h * rstd
    n = xhat * ng
    return n * (1.0 + sc) + sh, xhat, rstd, n


def _normmod_bwd(da, xhat, rstd, n, ng, sc):
    dsh = _colsum(da)
    dsc = _colsum(da * n)
    dn = da * (1.0 + sc)
    dng = _colsum(dn * xhat)
    dxhat = dn * ng
    dh = rstd * (dxhat - xhat * jnp.mean(dxhat * xhat, axis=-1, keepdims=True))
    return dh, dsh, dsc, dng


def _acc_rows(ref, first, rows):
    @pl.when(first)
    def _():
        ref[...] = jnp.zeros_like(ref)
    for r, val in enumerate(rows):
        ref[r:r + 1, :] = ref[r:r + 1, :] + val


_GELU_C = math.sqrt(2.0 / math.pi)


def _gelu(x):
    t = jnp.tanh((_GELU_C * x) * (1.0 + 0.044715 * (x * x)))
    hx = 0.5 * x
    return hx + hx * t


def _gelu_and_grad(x):
    x2 = x * x
    t = jnp.tanh((_GELU_C * x) * (1.0 + 0.044715 * x2))
    hx = 0.5 * x
    g = hx + hx * t
    dg = (0.5 + 0.5 * t) + (hx * (1.0 - t * t)) * (_GELU_C + (3.0 * 0.044715 * _GELU_C) * x2)
    return g, dg


def _coords():
    return lax.axis_index("x"), lax.axis_index("y"), lax.axis_index("c")


def _dev_index(px, py, pc):
    return 4 * px + 2 * py + pc


def _all_gather(xs, name, extra=None):
    n = len(xs)
    e_in, e_in_specs, e_out, e_out_specs, e_scratch = _rider_parts(extra)
    ne = len(e_in)

    def body(*refs):
        x_refs, e_x = refs[:n], refs[n:n + ne]
        o_refs, e_o = refs[n + ne:2 * n + ne], refs[2 * n + ne:2 * n + 2 * ne]
        send_sems, recv_sems, local_sems = refs[2 * n + 2 * ne:2 * n + 2 * ne + 3]
        e_sems = refs[2 * n + 2 * ne + 3:]
        if extra is not None:
            extra.start(e_x, e_o, e_sems)
        x, y, c = _coords()
        me, sibling = (x, y, c), (x, y, 1 - c)
        chips = [(1 - x, y), (x, 1 - y), (1 - x, 1 - y)]

        def copy(a, k, block, to, src=None):
            dst = o_refs[a].at[_dev_index(*block)]
            return pltpu.make_async_remote_copy(
                src_ref=dst if src is None else src, dst_ref=dst,
                send_sem=send_sems.at[7 * a + k], recv_sem=recv_sems.at[7 * a + k],
                device_id=to, device_id_type=MESH_ID)

        mine = [pltpu.make_async_copy(x_refs[a], o_refs[a].at[_dev_index(*me)], local_sems.at[a])
                for a in range(n)]
        for cp in mine:
            cp.start()
        first = []
        for a in range(n):
            first.append(copy(a, 0, me, sibling, src=x_refs[a]))
            first += [copy(a, 1 + j, me, (*chip, c), src=x_refs[a]) for j, chip in enumerate(chips)]
        for cp in first:
            cp.start()
        passed = []
        for a in range(n):
            for j, chip in enumerate(chips):
                copy(a, 1 + j, (*chip, c), me).wait_recv()
                fwd = copy(a, 4 + j, (*chip, c), sibling)
                fwd.start()
                passed.append(fwd)
        for a in range(n):
            copy(a, 0, sibling, me).wait_recv()
            for j, chip in enumerate(chips):
                copy(a, 4 + j, (*chip, 1 - c), me).wait_recv()
        for cp in first + passed:
            cp.wait_send()
        for cp in mine:
            cp.wait()
        if extra is not None:
            extra.finish(e_x, e_o, e_sems)

    any_spec = pl.BlockSpec(memory_space=pl.ANY)
    outs = pl.pallas_call(
        body, name=name,
        out_shape=[jax.ShapeDtypeStruct((N_DEV,) + x.shape, x.dtype) for x in xs] + e_out,
        in_specs=[any_spec] * n + e_in_specs, out_specs=[any_spec] * n + e_out_specs,
        scratch_shapes=[pltpu.SemaphoreType.DMA((7 * n,)), pltpu.SemaphoreType.DMA((7 * n,)),
                        pltpu.SemaphoreType.DMA((n,))] + e_scratch,
    )(*xs, *e_in)
    return list(outs)


class _Exchange:
    per_array = 0
    in_place = False

    def __init__(self, xs):
        self.xs = list(xs)
        n = len(self.xs)
        self.out_shapes = self._out_shapes()
        self.scratch = [pltpu.SemaphoreType.DMA((self.per_array * n,)),
                        pltpu.SemaphoreType.DMA((self.per_array * n,)),
                        pltpu.SemaphoreType.DMA((n,))]

    def _out_shapes(self):
        raise NotImplementedError

    def _copies(self, x_refs, o_refs, sems):
        raise NotImplementedError

    def start(self, x_refs, o_refs, sems):
        mine, sends, _ = self._copies(x_refs, o_refs, sems)
        for cp in mine + sends:
            cp.start()

    def finish(self, x_refs, o_refs, sems):
        mine, sends, arrivals = self._copies(x_refs, o_refs, sems)
        for make in arrivals:
            make().wait_recv()
        for cp in sends:
            cp.wait_send()
        for cp in mine:
            cp.wait()


def _remote(src, dst, sems, k, to):
    return pltpu.make_async_remote_copy(src_ref=src, dst_ref=dst, send_sem=sems[0].at[k], recv_sem=sems[1].at[k],
                                        device_id=to, device_id_type=MESH_ID)


class _GatherAcrossChips(_Exchange):
    per_array = 4

    def _out_shapes(self):
        return [jax.ShapeDtypeStruct((N_DEV,) + x.shape, x.dtype) for x in self.xs]

    def _copies(self, x_refs, o_refs, sems):
        x, y, c = _coords()
        targets = [(x, y, 1 - c), (1 - x, y, c), (x, 1 - y, c), (1 - x, 1 - y, c)]
        mine, sends, arrivals = [], [], []
        for a, (x_ref, o_ref) in enumerate(zip(x_refs, o_refs)):
            own = o_ref.at[_dev_index(x, y, c)]
            mine.append(pltpu.make_async_copy(x_ref, own, sems[2].at[a]))
            for k, to in enumerate(targets):
                sends.append(_remote(x_ref, own, sems, 4 * a + k, to))
                arrivals.append(functools.partial(_remote, x_ref, o_ref.at[_dev_index(*to)], sems, 4 * a + k, to))
        return mine, sends, arrivals


class _ForwardToSibling(_Exchange):
    per_array = 3
    in_place = True

    def _out_shapes(self):
        return [jax.ShapeDtypeStruct(x.shape, x.dtype) for x in self.xs]

    def _copies(self, x_refs, o_refs, sems):
        x, y, c = _coords()
        chips = [(1 - x, y), (x, 1 - y), (1 - x, 1 - y)]
        sends, arrivals = [], []
        for a, (x_ref, o_ref) in enumerate(zip(x_refs, o_refs)):
            for j, chip in enumerate(chips):
                held = _dev_index(*chip, c)
                sends.append(_remote(x_ref.at[held], o_ref.at[held], sems, 3 * a + j, (x, y, 1 - c)))
                theirs = _dev_index(*chip, 1 - c)
                arrivals.append(functools.partial(_remote, x_ref.at[theirs], o_ref.at[theirs], sems, 3 * a + j,
                                                  (x, y, 1 - c)))
        return [], sends, arrivals


class _AllToAll(_Exchange):
    per_array = 7

    def _out_shapes(self):
        return [jax.ShapeDtypeStruct(x.shape, x.dtype) for x in self.xs]

    def _copies(self, x_refs, o_refs, sems):
        x, y, c = _coords()
        me_i = _dev_index(x, y, c)
        mine, sends, arrivals = [], [], []
        for a, (x_ref, o_ref) in enumerate(zip(x_refs, o_refs)):
            mine.append(pltpu.make_async_copy(x_ref.at[me_i], o_ref.at[me_i], sems[2].at[a]))
            for r in range(1, 8):
                to = (1 - x if r & 4 else x, 1 - y if r & 2 else y, 1 - c if r & 1 else c)
                to_i = _dev_index(*to)
                sends.append(_remote(x_ref.at[to_i], o_ref.at[me_i], sems, 7 * a + r - 1, to))
                arrivals.append(functools.partial(_remote, x_ref.at[to_i], o_ref.at[to_i], sems, 7 * a + r - 1, to))
        return mine, sends, arrivals


def _exchange_call(ex, name):
    n = len(ex.xs)

    def body(*refs):
        x_refs, o_refs, sems = refs[:n], refs[n:2 * n], refs[2 * n:]
        ex.start(x_refs, o_refs, sems)
        ex.finish(x_refs, o_refs, sems)

    any_spec = pl.BlockSpec(memory_space=pl.ANY)
    outs = pl.pallas_call(
        body, name=name, out_shape=ex.out_shapes, in_specs=[any_spec] * n, out_specs=[any_spec] * n,
        scratch_shapes=ex.scratch, input_output_aliases={a: a for a in range(n)} if ex.in_place else {},
    )(*ex.xs)
    return list(outs)


def _rider_parts(rider):
    if rider is None:
        return [], [], [], [], []
    any_spec = pl.BlockSpec(memory_space=pl.ANY)
    n = len(rider.xs)
    return rider.xs, [any_spec] * n, rider.out_shapes, [any_spec] * n, rider.scratch


def _compute_call(body, *, name, grid, in_specs, out_specs, out_shape, operands, scratch_shapes=(), rider=None):
    in_specs, out_specs, out_shape, scratch_shapes = list(in_specs), list(out_specs), list(out_shape), list(scratch_shapes)
    r_in, r_in_specs, r_out, r_out_specs, r_scratch = _rider_parts(rider)
    n_in, n_out, n_scr, nr = len(operands), len(out_shape), len(scratch_shapes), len(r_in)

    def riding_body(*refs):
        ins, refs = refs[:n_in], refs[n_in:]
        r_x, refs = refs[:nr], refs[nr:]
        outs, refs = refs[:n_out], refs[n_out:]
        r_o, refs = refs[:nr], refs[nr:]
        scratch, r_sems = refs[:n_scr], refs[n_scr:]
        if rider is not None:
            first, last = _grid_ends(grid)
            pl.when(first)(lambda: rider.start(r_x, r_o, r_sems))
        body(*ins, *outs, *scratch)
        if rider is not None:
            pl.when(last)(lambda: rider.finish(r_x, r_o, r_sems))

    in_place = rider is not None and rider.in_place
    res = pl.pallas_call(
        riding_body, name=name, grid=grid, out_shape=out_shape + r_out,
        in_specs=in_specs + r_in_specs, out_specs=out_specs + r_out_specs,
        scratch_shapes=scratch_shapes + r_scratch,
        input_output_aliases={n_in + a: n_out + a for a in range(nr)} if in_place else {},
        compiler_params=_params(("arbitrary",) * len(grid)),
    )(*operands, *r_in)
    return list(res[:n_out]), list(res[n_out:])


def _grid_ends(grid):
    first = pl.program_id(0) == 0
    last = pl.program_id(0) == grid[0] - 1
    for ax in range(1, len(grid)):
        first = jnp.logical_and(first, pl.program_id(ax) == 0)
        last = jnp.logical_and(last, pl.program_id(ax) == grid[ax] - 1)
    return first, last


def _silu(x):
    return x * (1.0 / (1.0 + jnp.exp(-x)))


def _cond_rows(c_all, c_ctx):
    d = c_all.shape[-1]
    s = jnp.concatenate([c_all, jnp.zeros((8, d), F32)], axis=0)
    row = lax.broadcasted_iota(jnp.int32, (16, d), 0)
    s = jnp.where(row == 8, c_ctx, s)
    return jnp.where(row <= 8, _silu(s), 0.0)


def _mods_local(c_all, c_ctx, ada_w, ada_b_loc):
    nl, d, n = ada_w.shape

    def body(c_ref, cc_ref, w_ref, b_ref, o_ref):
        s = _cond_rows(c_ref[...], cc_ref[...])
        o_ref[...] = jnp.dot(s, w_ref[...], preferred_element_type=F32,
                             precision=lax.Precision.HIGHEST) + b_ref[...]

    return pl.pallas_call(
        body, name="mods_local", grid=(nl,),
        out_shape=jax.ShapeDtypeStruct((nl, 16, n), F32),
        in_specs=[pl.BlockSpec((8, d), lambda i: (0, 0)), pl.BlockSpec((1, d), lambda i: (0, 0)),
                  pl.BlockSpec((None, d, n), lambda i: (i, 0, 0)),
                  pl.BlockSpec((None, 1, n), lambda i: (i, 0, 0))],
        out_specs=pl.BlockSpec((None, 16, n), lambda i: (i, 0, 0)),
        compiler_params=_params(("arbitrary",)),
    )(c_all, c_ctx, ada_w, ada_b_loc)


def _ada_grads(c_all, c_ctx, ada_w, dm_lat, dm_ctx):
    nl, d, n = ada_w.shape

    def body(c_ref, cc_ref, w_ref, dml_ref, dmc_ref, gw_ref, ds_ref):
        i = pl.program_id(0)
        s = _cond_rows(c_ref[...], cc_ref[...])
        csum = dmc_ref[0:1, :]
        for k in range(1, N_DEV):
            csum = csum + dmc_ref[k:k + 1, :]
        row = lax.broadcasted_iota(jnp.int32, (8, n), 0)
        dm_c = jnp.where(row == 0, csum, 0.0)
        dm = jnp.concatenate([dml_ref[...], dm_c], axis=0)
        gw_ref[...] = lax.dot_general(s, dm, (((0,), (0,)), ((), ())), preferred_element_type=F32,
                                      precision=lax.Precision.HIGHEST)
        ds = lax.dot_general(dm_c, w_ref[...], (((1,), (1,)), ((), ())),
                             preferred_element_type=F32, precision=lax.Precision.HIGHEST)

        @pl.when(i == 0)
        def _():
            ds_ref[...] = jnp.zeros_like(ds_ref)
        ds_ref[...] += ds

    return pl.pallas_call(
        body, name="ada_grads", grid=(nl,),
        out_shape=[jax.ShapeDtypeStruct((nl, d, n), F32), jax.ShapeDtypeStruct((8, d), F32)],
        in_specs=[pl.BlockSpec((8, d), lambda i: (0, 0)), pl.BlockSpec((1, d), lambda i: (0, 0)),
                  pl.BlockSpec((None, d, n), lambda i: (i, 0, 0)),
                  pl.BlockSpec((None, 8, n), lambda i: (i, 0, 0)),
                  pl.BlockSpec((None, 8, n), lambda i: (i, 0, 0))],
        out_specs=[pl.BlockSpec((None, d, n), lambda i: (i, 0, 0)),
                   pl.BlockSpec((8, d), lambda i: (0, 0))],
        compiler_params=_params(("arbitrary",)),
    )(c_all, c_ctx, ada_w, dm_lat, dm_ctx)


def _cctx_grad_and_loss(ds_parts, c_ctx, loss_parts):
    d = c_ctx.shape[-1]

    def body(p_ref, c_ref, l_ref, o_ref, lo_ref):
        ds, loss = p_ref[0], l_ref[0]
        for k in range(1, N_DEV):
            ds = ds + p_ref[k]
            loss = loss + l_ref[k]
        x = c_ref[...]
        sg = 1.0 / (1.0 + jnp.exp(-x))
        o_ref[...] = ds[0:1, :] * (sg * (1.0 + x * (1.0 - sg)))
        lo_ref[...] = loss

    return pl.pallas_call(body, name="cctx_grad", out_shape=[jax.ShapeDtypeStruct((1, d), F32),
                                                             jax.ShapeDtypeStruct((8, 128), F32)])(ds_parts, c_ctx, loss_parts)


def _mlp_fwd(h1, mods, ng, w1, w2, nct, tm, row_off=0):
    d = h1.shape[1]
    r = h1.shape[0] - row_off * tm
    fc = w1.shape[2]
    f = N_DEV * fc

    def body(h_ref, mod_ref, ng_ref, w1_ref, w2_ref, h2_ref, p_ref, y_ref):
        h = h_ref[...]
        a, _, _, _ = _normmod(h, ng_ref[...], mod_ref[3:4, :], mod_ref[4:5, :])
        ab = a.astype(BF16)
        acc = jnp.zeros((tm, d), F32)
        for j in range(N_DEV):
            sl = slice(j * fc, (j + 1) * fc)
            p = jnp.maximum(_dot(ab, w1_ref[j]), 0.0)
            p_ref[:, sl] = p.astype(BF16)
            acc = acc + _dot((p * p).astype(BF16), w2_ref[j])
        y_ref[...] = acc.astype(BF16)
        h2_ref[...] = h + mod_ref[5:6, :] * acc

    return pl.pallas_call(
        body, name="mlp_fwd", grid=(r // tm,),
        out_shape=[jax.ShapeDtypeStruct((r, d), F32), jax.ShapeDtypeStruct((r, f), BF16),
                   jax.ShapeDtypeStruct((r, d), BF16)],
        in_specs=[pl.BlockSpec((tm, d), lambda i: (i + row_off, 0)),
                  pl.BlockSpec((None, 8, d), lambda i: (_sid(i, nct), 0, 0)),
                  _const_spec((1, d)), _const_spec(w1.shape), _const_spec(w2.shape)],
        out_specs=[pl.BlockSpec((tm, d), lambda i: (i, 0)), pl.BlockSpec((tm, f), lambda i: (i, 0)),
                   pl.BlockSpec((tm, d), lambda i: (i, 0))],
        compiler_params=_params(("arbitrary",)),
    )(h1, mods, ng, w1, w2)


def _mlp_bwd(dh2, h1, p, y, mods, ng, w1, w2, nct, tm, row_off=0, rider=None):
    r_rows, d = dh2.shape
    fc = w1.shape[2]
    f = N_DEV * fc

    def body(dh_ref, h_ref, p_ref, y_ref, mod_ref, ng_ref, w1_ref, w2_ref,
             dh1_ref, m_ref, du_ref, dacc_ref, st_ref):
        i = pl.program_id(0)
        dh = dh_ref[...]
        ngv, sc, gate = ng_ref[...], mod_ref[4:5, :], mod_ref[5:6, :]
        a, xhat, rstd, n = _normmod(h_ref[...], ngv, mod_ref[3:4, :], sc)
        m_ref[...] = a.astype(BF16)
        dgate = _colsum(dh * y_ref[...].astype(F32))
        dacc = (gate * dh).astype(BF16)
        dacc_ref[...] = dacc
        dm = jnp.zeros((tm, d), F32)
        for j in range(N_DEV):
            sl = slice(j * fc, (j + 1) * fc)
            pj = p_ref[:, sl].astype(F32)
            du = (_dot_nt(dacc, w2_ref[j]) * (2.0 * pj)).astype(BF16)
            du_ref[:, sl] = du
            dm = dm + _dot_nt(du, w1_ref[j])
        dhn, dsh, dsc, dng = _normmod_bwd(dm, xhat, rstd, n, ngv, sc)
        dh1_ref[...] = dh + dhn
        _acc_rows(st_ref, _first_of_stream(i, nct), [dsh, dsc, dgate, dng])

    outs, rode = _compute_call(
        body, name="mlp_bwd", grid=(r_rows // tm,), operands=(dh2, h1, p, y, mods, ng, w1, w2), rider=rider,
        out_shape=[jax.ShapeDtypeStruct((r_rows, d), F32), jax.ShapeDtypeStruct((r_rows, d), BF16),
                   jax.ShapeDtypeStruct((r_rows, f), BF16),
                   jax.ShapeDtypeStruct((r_rows, d), BF16), jax.ShapeDtypeStruct((_n_streams(nct), 8, d), F32)],
        in_specs=[pl.BlockSpec((tm, d), lambda i: (i, 0)),
                  pl.BlockSpec((tm, d), lambda i: (i + row_off, 0)),
                  pl.BlockSpec((tm, f), lambda i: (i, 0)), pl.BlockSpec((tm, d), lambda i: (i, 0)),
                  pl.BlockSpec((None, 8, d), lambda i: (_sid(i, nct), 0, 0)),
                  _const_spec((1, d)), _const_spec(w1.shape), _const_spec(w2.shape)],
        out_specs=[pl.BlockSpec((tm, d), lambda i: (i, 0)), pl.BlockSpec((tm, d), lambda i: (i, 0)),
                   pl.BlockSpec((tm, f), lambda i: (i, 0)),
                   pl.BlockSpec((tm, d), lambda i: (i, 0)),
                   pl.BlockSpec((None, 8, d), lambda i: (_stat_sid(i, nct), 0, 0))])
    return (*outs, rode)


def _pick(n, cands):
    for cand in cands:
        if n % cand == 0:
            return cand
    return n


def _tn_matmul(x, y, name, col_shards=False, square_x=False, rider=None):
    rows, k1 = x.shape
    k2 = y.shape[1]
    bt = _pick(rows, (1024, 768, 512, 384, 256, 128))
    bk1, bk2 = min(k1, 1024), min(k2, 1024)
    grid = (k1 // bk1, k2 // bk2, rows // bt)
    nt = rows // bt
    n = k2 // N_DEV
    if col_shards:
        assert bk2 % n == 0
        per = bk2 // n
        out_shape = jax.ShapeDtypeStruct((N_DEV, k1, n), BF16)
        out_spec = pl.BlockSpec((per, bk1, n), lambda i, j, t: (j, i, 0))
    else:
        out_shape = jax.ShapeDtypeStruct((k1, k2), BF16)
        out_spec = pl.BlockSpec((bk1, bk2), lambda i, j, t: (i, j))

    def body(x_ref, y_ref, o_ref, acc_ref):
        t = pl.program_id(2)

        @pl.when(t == 0)
        def _():
            acc_ref[...] = jnp.zeros_like(acc_ref)
        xv = x_ref[...]
        acc_ref[...] += _dot_tn(xv * xv if square_x else xv, y_ref[...])

        @pl.when(t == nt - 1)
        def _():
            if col_shards:
                for s in range(per):
                    o_ref[s] = acc_ref[:, s * n:(s + 1) * n].astype(BF16)
            else:
                o_ref[...] = acc_ref[...].astype(BF16)

    (out,), rode = _compute_call(
        body, name=name, grid=grid, operands=(x, y), rider=rider, out_shape=[out_shape],
        in_specs=[pl.BlockSpec((bt, bk1), lambda i, j, t: (t, i)), pl.BlockSpec((bt, bk2), lambda i, j, t: (t, j))],
        out_specs=[out_spec], scratch_shapes=[pltpu.VMEM((bk1, bk2), F32)])
    return out if rider is None else (out, rode)


def _pool_bands(tm):
    k = tm + 128
    t = np.arange(tm)[:, None]
    e = np.arange(k)[None, :]
    fwd, bwd = [], []
    for w in POOL_WINDOWS:
        lo = POOL_HALO + t - w // 2
        fwd.append(((e >= lo) & (e <= lo + w - 1)).astype(np.float32))
        lo_t = POOL_HALO + t - w // 2 + 1
        bwd.append(((e >= lo_t) & (e <= lo_t + w - 1)).astype(np.float32))
    return jnp.asarray(np.stack(fwd), BF16), jnp.asarray(np.stack(bwd), BF16)


def _pool_geometry(i, nct, n_tiles, tm, c_len, l_len):
    if nct == 0:
        pos0 = i * tm
        ls = l_len
        has_prev = i > 0
        has_next = i < n_tiles - 1
    else:
        in_ctx = i < nct
        pos0 = jnp.where(in_ctx, i, i - nct) * tm
        ls = jnp.where(in_ctx, c_len, l_len)
        has_prev = jnp.logical_and(i != 0, i != nct)
        has_next = jnp.logical_and(i != nct - 1, i != n_tiles - 1)
    return pos0, ls, has_prev, has_next


def _window_inv_counts(pos, ls):
    out = []
    for w in POOL_WINDOWS:
        lo = jnp.maximum(pos - w // 2, 0)
        hi = jnp.minimum(pos + w - w // 2, ls)
        cnt = jnp.maximum(hi - lo, 1).astype(F32)
        out.append(1.0 / cnt)
    return out


def _split_bf16(x):
    hi = x.astype(BF16)
    return hi, (x - hi.astype(F32)).astype(BF16)


def _extend(prev, tile, nxt, has_prev, has_next):
    w = tile.shape[1]
    prev = jnp.where(has_prev, prev, 0.0)
    nxt = jnp.where(has_next, nxt, 0.0)
    return jnp.concatenate([prev, tile, nxt, jnp.zeros((128 - 2 * POOL_HALO, w), F32)], axis=0)


def _pool_specs(tm, d, n_rows):
    last8 = n_rows // POOL_HALO - 1
    per = tm // POOL_HALO
    return [pl.BlockSpec((tm, d), lambda i: (i, 0)),
            pl.BlockSpec((POOL_HALO, d), lambda i: (jnp.maximum(i * per - 1, 0), 0)),
            pl.BlockSpec((POOL_HALO, d), lambda i: (jnp.minimum((i + 1) * per, last8), 0))]


def _pool_fwd(h, mods, ng, w, scale, bands, nct, tm, c_len, l_len):
    r, d = h.shape
    gw = d // POOL_GROUPS
    n_tiles = r // tm
    kx = tm + 128

    def body(h_ref, hp_ref, hn_ref, mod_ref, ng_ref, w_ref, sc_ref, band_ref, y_ref, h1_ref):
        i = pl.program_id(0)
        pos0, ls, has_prev, has_next = _pool_geometry(i, nct, n_tiles, tm, c_len, l_len)
        ngv, sh, sc = ng_ref[...], mod_ref[0:1, :], mod_ref[1:2, :]
        h = h_ref[...]
        a = _normmod(h, ngv, sh, sc)[0]
        a_ext = _extend(_normmod(hp_ref[...], ngv, sh, sc)[0], a, _normmod(hn_ref[...], ngv, sh, sc)[0],
                        has_prev, has_next)
        pos = pos0 + lax.broadcasted_iota(jnp.int32, (tm, 1), 0)
        inv = _window_inv_counts(pos, ls)
        ys = []
        for g in range(POOL_GROUPS):
            cols = slice(g * gw, (g + 1) * gw)
            hi, lo = _split_bf16(a_ext[:, cols])
            s = _dot(band_ref[g], hi) + _dot(band_ref[g], lo)
            pg = s * inv[g] - a[:, cols]
            ys.append(_dot(pg.astype(BF16), w_ref[g]))
        y = jnp.concatenate(ys, axis=1) * sc_ref[...]
        y_ref[...] = y.astype(BF16)
        h1_ref[...] = h + mod_ref[2:3, :] * y

    return pl.pallas_call(
        body, name="pool_fwd", grid=(n_tiles,),
        out_shape=[jax.ShapeDtypeStruct((r, d), BF16), jax.ShapeDtypeStruct((r, d), F32)],
        in_specs=_pool_specs(tm, d, r) + [
            pl.BlockSpec((None, 8, d), lambda i: (_sid(i, nct), 0, 0)),
            _const_spec((1, d)), _const_spec(w.shape), _const_spec((1, d)), _const_spec((4, tm, kx))],
        out_specs=[pl.BlockSpec((tm, d), lambda i: (i, 0)), pl.BlockSpec((tm, d), lambda i: (i, 0))],
        compiler_params=_params(("arbitrary",)),
    )(h, h, h, mods, ng, w, scale, bands[0])


def _pool_bwd(dh1, h, y, mods, ng, w, scale, bands, nct, tm, c_len, l_len, latent_out, rider=None):
    r, d = h.shape
    gw = d // POOL_GROUPS
    n_tiles = r // tm
    kx = tm + 128
    out_rows = l_len if latent_out else r
    out_off = nct if latent_out else 0

    def body(dh_ref, dhp_ref, dhn_ref, h_ref, hp_ref, hn_ref, y_ref, mod_ref, ng_ref, w_ref, sc_ref,
             bf_ref, bb_ref, dho_ref, dw_ref, st_ref):
        i = pl.program_id(0)
        pos0, ls, has_prev, has_next = _pool_geometry(i, nct, n_tiles, tm, c_len, l_len)
        ngv, sh, sc, gate = ng_ref[...], mod_ref[0:1, :], mod_ref[1:2, :], mod_ref[2:3, :]
        scale_v = sc_ref[...]
        h = h_ref[...]
        a, xhat, rstd, n = _normmod(h, ngv, sh, sc)
        a_ext = _extend(_normmod(hp_ref[...], ngv, sh, sc)[0], a, _normmod(hn_ref[...], ngv, sh, sc)[0],
                        has_prev, has_next)
        dh = dh_ref[...]
        dgate = _colsum(dh * y_ref[...].astype(F32))
        dy = gate * dh
        dy_ext = _extend(gate * dhp_ref[...], dy, gate * dhn_ref[...], has_prev, has_next)
        dyp_ext = (dy_ext * scale_v).astype(BF16)
        dyp = (dy * scale_v).astype(BF16)
        pos = pos0 + lax.broadcasted_iota(jnp.int32, (tm, 1), 0)
        inv = _window_inv_counts(pos, ls)
        pos_e = pos0 - POOL_HALO + lax.broadcasted_iota(jnp.int32, (kx, 1), 0)
        inv_e = _window_inv_counts(pos_e, ls)

        @pl.when(i == 0)
        def _():
            dw_ref[...] = jnp.zeros_like(dw_ref)
        das, dscale = [], []
        for g in range(POOL_GROUPS):
            cols = slice(g * gw, (g + 1) * gw)
            hi, lo = _split_bf16(a_ext[:, cols])
            pg = ((_dot(bf_ref[g], hi) + _dot(bf_ref[g], lo)) * inv[g] - a[:, cols]).astype(BF16)
            dscale.append(_colsum(dy[:, cols] * _dot(pg, w_ref[g])))
            dyp_g = dyp_ext[:, cols]
            dw_ref[g] += _dot_tn(pg, dyp[:, cols])
            dp_ext = _dot_nt(dyp_g, w_ref[g])
            hi, lo = _split_bf16(dp_ext * inv_e[g])
            das.append(_dot(bb_ref[g], hi) + _dot(bb_ref[g], lo) - dp_ext[POOL_HALO:POOL_HALO + tm, :])
        da = jnp.concatenate(das, axis=1)
        dhn, dsh, dsc, dng = _normmod_bwd(da, xhat, rstd, n, ngv, sc)
        dho_ref[...] = dh + dhn
        _acc_rows(st_ref, _first_of_stream(i, nct), [dsh, dsc, dgate, dng, jnp.concatenate(dscale, axis=1)])

    outs, rode = _compute_call(
        body, name="pool_bwd", grid=(n_tiles,), rider=rider,
        operands=(dh1, dh1, dh1, h, h, h, y, mods, ng, w, scale, bands[0], bands[1]),
        out_shape=[jax.ShapeDtypeStruct((out_rows, d), F32),
                   jax.ShapeDtypeStruct((POOL_GROUPS, gw, gw), F32),
                   jax.ShapeDtypeStruct((_n_streams(nct), 8, d), F32)],
        in_specs=_pool_specs(tm, d, r) + _pool_specs(tm, d, r) + [
            pl.BlockSpec((tm, d), lambda i: (i, 0)),
            pl.BlockSpec((None, 8, d), lambda i: (_sid(i, nct), 0, 0)),
            _const_spec((1, d)), _const_spec(w.shape), _const_spec((1, d)),
            _const_spec((4, tm, kx)), _const_spec((4, tm, kx))],
        out_specs=[pl.BlockSpec((tm, d), lambda i: (jnp.maximum(i - out_off, 0), 0)),
                   pl.BlockSpec((POOL_GROUPS, gw, gw), lambda i: (0, 0, 0)),
                   pl.BlockSpec((None, 8, d), lambda i: (_stat_sid(i, nct), 0, 0))])
    return (*outs, rode)


def _rope_tables(c_len, l_len):
    half = HEAD_DIM // 2
    t = np.arange(l_len)
    row = (t // GRID_W).astype(np.float32)
    col = (t % GRID_W).astype(np.float32)
    inv = (np.float32(ROPE_BASE) ** (-np.arange(0, half, 2, dtype=np.float32) / np.float32(half))).astype(np.float32)
    ang_r = row[:, None] * inv[None, :]
    ang_c = col[:, None] * inv[None, :]
    cos = np.concatenate([np.cos(ang_r), np.cos(ang_r), np.cos(ang_c), np.cos(ang_c)], axis=1)
    sin = np.concatenate([-np.sin(ang_r), np.sin(ang_r), -np.sin(ang_c), np.sin(ang_c)], axis=1)
    cos = np.concatenate([np.ones((c_len, HEAD_DIM), np.float32), cos.astype(np.float32)], axis=0)
    sin = np.concatenate([np.zeros((c_len, HEAD_DIM), np.float32), sin.astype(np.float32)], axis=0)
    return jnp.asarray(cos, F32), jnp.asarray(sin, F32)


def _swap_pairs(x):
    lane = lax.broadcasted_iota(jnp.int32, x.shape, 1)
    return jnp.where((lane % 64) < 32, pltpu.roll(x, 96, 1), pltpu.roll(x, 32, 1))


def _head_norm(x, g):
    rstd = lax.rsqrt(jnp.mean(x * x, axis=-1, keepdims=True) + EPS)
    xhat = x * rstd
    return xhat * g, xhat, rstd


def _qkv_fwd(h, mods, ng, w, qg, kg, cos, sin, n_heads, n_kv, nct, tm):
    t_rows, d = h.shape
    qw, kw = n_heads * HEAD_DIM, n_kv * HEAD_DIM

    def body(h_ref, mod_ref, ng_ref, w_ref, qg_ref, kg_ref, cos_ref, sin_ref, q_ref, k_ref, v_ref, qt_ref):
        a = _normmod(h_ref[...], ng_ref[...], mod_ref[0:1, :], mod_ref[1:2, :])[0]
        qkv = _dot(a.astype(BF16), w_ref[...])
        cosv, sinv = cos_ref[...], sin_ref[...]
        ones = jnp.ones((tm, HEAD_DIM), BF16)
        for hd in range(n_heads + n_kv):
            cols = slice(hd * HEAD_DIM, (hd + 1) * HEAD_DIM)
            xn = _head_norm(qkv[:, cols], qg_ref[...] if hd < n_heads else kg_ref[...])[0]
            xr = xn * cosv + _swap_pairs(xn) * sinv
            if hd < n_heads:
                qs = xr * Q_SCALE
                q_ref[:, cols] = qs.astype(BF16)
                qt_ref[cols, :] = qs.T.astype(BF16)
            else:
                k_ref[:, (hd - n_heads) * HEAD_DIM:(hd - n_heads + 1) * HEAD_DIM] = xr.astype(BF16)
        for g in range(n_kv):
            v_ref[:, (2 * g) * HEAD_DIM:(2 * g + 1) * HEAD_DIM] = (
                qkv[:, qw + kw + g * HEAD_DIM:qw + kw + (g + 1) * HEAD_DIM].astype(BF16))
            v_ref[:, (2 * g + 1) * HEAD_DIM:(2 * g + 2) * HEAD_DIM] = ones

    return pl.pallas_call(
        body, name="qkv_fwd", grid=(t_rows // tm,),
        out_shape=[jax.ShapeDtypeStruct((t_rows - nct * tm, qw), BF16), jax.ShapeDtypeStruct((t_rows, kw), BF16),
                   jax.ShapeDtypeStruct((t_rows, 2 * kw), BF16), jax.ShapeDtypeStruct((qw, t_rows - nct * tm), BF16)],
        in_specs=[pl.BlockSpec((tm, d), lambda i: (i, 0)),
                  pl.BlockSpec((None, 8, d), lambda i: (_sid(i, nct), 0, 0)),
                  _const_spec((1, d)), _const_spec(w.shape), _const_spec((1, HEAD_DIM)),
                  _const_spec((1, HEAD_DIM)),
                  pl.BlockSpec((tm, HEAD_DIM), lambda i: (i, 0)), pl.BlockSpec((tm, HEAD_DIM), lambda i: (i, 0))],
        out_specs=[pl.BlockSpec((tm, qw), lambda i: (jnp.maximum(i - nct, 0), 0)),
                   pl.BlockSpec((tm, kw), lambda i: (i, 0)), pl.BlockSpec((tm, 2 * kw), lambda i: (i, 0)),
                   pl.BlockSpec((qw, tm), lambda i: (0, jnp.maximum(i - nct, 0)))],
        compiler_params=_params(("arbitrary",)),
    )(h, mods, ng, w, qg, kg, cos, sin)


def _flash_tk(t_rows, tm):
    best = tm
    k = tm
    while k <= FLASH_TK_CAP:
        if t_rows % k == 0:
            best = k
        k += tm
    return best


def _flash_fwd(q, k, v1, n_kv, tq, tm, rider=None):
    t_rows = k.shape[0]
    l_rows = q.shape[0]
    tk = _flash_tk(t_rows, tm)
    nk = t_rows // tk
    gq = 2 * HEAD_DIM

    def body(q_ref, k_ref, v_ref, o_ref, lse_ref, m_s, acc_s, s_s):
        ki = pl.program_id(2)

        @pl.when(ki == 0)
        def _():
            m_s[...] = jnp.full_like(m_s, -jnp.inf)
            acc_s[...] = jnp.zeros_like(acc_s)
        kk, vv = k_ref[...], v_ref[...]
        for hh in range(2):
            s_s[hh] = _dot_nt(q_ref[:, hh * HEAD_DIM:(hh + 1) * HEAD_DIM], kk)
        for hh in range(2):
            s = s_s[hh]
            m_prev = m_s[hh]
            m_new = jnp.maximum(m_prev, jnp.max(s, axis=-1, keepdims=True))
            alpha = jnp.exp2(m_prev - m_new)
            p = jnp.exp2(s - jnp.tile(m_new, (1, tk // HEAD_DIM)))
            acc_s[hh] = jnp.tile(alpha, (1, 2)) * acc_s[hh] + _dot(p.astype(BF16), vv)
            m_s[hh] = m_new

        @pl.when(ki == nk - 1)
        def _():
            for hh in range(2):
                acc = acc_s[hh]
                l = acc[:, HEAD_DIM:]
                o_ref[:, hh * HEAD_DIM:(hh + 1) * HEAD_DIM] = (acc[:, :HEAD_DIM] / l).astype(BF16)
                lse_ref[:, hh:hh + 1] = (m_s[hh] + jnp.log2(l))[:, 0:1]

    (o, lse), rode = _compute_call(
        body, name="flash_fwd", grid=(n_kv, l_rows // tq, nk), operands=(q, k, v1), rider=rider,
        out_shape=[jax.ShapeDtypeStruct((l_rows, n_kv * gq), BF16),
                   jax.ShapeDtypeStruct((n_kv, l_rows, 2), F32)],
        in_specs=[pl.BlockSpec((tq, gq), lambda g, i, j: (i, g)),
                  pl.BlockSpec((tk, HEAD_DIM), lambda g, i, j: (j, g)),
                  pl.BlockSpec((tk, gq), lambda g, i, j: (j, g))],
        out_specs=[pl.BlockSpec((tq, gq), lambda g, i, j: (i, g)),
                   pl.BlockSpec((None, tq, 2), lambda g, i, j: (g, i, 0))],
        scratch_shapes=[pltpu.VMEM((2, tq, HEAD_DIM), F32), pltpu.VMEM((2, tq, gq), F32),
                        pltpu.VMEM((2, tq, tk), F32)])
    return o, lse, rode


def _flash_bwd(q, qt, k, v1, do, dot, lse, delta, n_kv, tq, tm, rider=None):
    t_rows = k.shape[0]
    l_rows = q.shape[0]
    tk = _flash_tk(t_rows, tm)
    nq = l_rows // tq
    gq = 2 * HEAD_DIM

    def body(q_ref, qt_ref, k_ref, v_ref, do_ref, dot_ref, lse_ref, dl_ref, dq_ref, dkt_ref, dvt_ref):
        ki, qi = pl.program_id(1), pl.program_id(2)
        rows = pl.ds(pl.multiple_of(qi * tq, tq), tq)

        @pl.when(qi == 0)
        def _():
            dkt_ref[...] = jnp.zeros_like(dkt_ref)
            dvt_ref[...] = jnp.zeros_like(dvt_ref)

        @pl.when(ki == 0)
        def _():
            dq_ref[rows, :] = jnp.zeros((tq, gq), F32)
        kk, vv = k_ref[...], v_ref[:, :HEAD_DIM]
        dkt_parts, dvt_parts = [], []
        for hh in range(2):
            cols = slice(hh * HEAD_DIM, (hh + 1) * HEAD_DIM)
            p = jnp.exp2(_dot_nt(q_ref[:, cols], kk) - lse_ref[:, hh:hh + 1])
            ds = (p * (_dot_nt(do_ref[:, cols], vv) - dl_ref[:, hh:hh + 1])).astype(BF16)
            dvt_parts.append(_dot(dot_ref[cols, :], p.astype(BF16)))
            dkt_parts.append(_dot(qt_ref[cols, :], ds))
            dq_ref[rows, cols] += _dot(ds, kk)
        dvt_ref[...] += dvt_parts[0] + dvt_parts[1]
        dkt_ref[...] += dkt_parts[0] + dkt_parts[1]

    (dq, dkt, dvt), rode = _compute_call(
        body, name="flash_bwd", grid=(n_kv, t_rows // tk, nq), operands=(q, qt, k, v1, do, dot, lse, delta),
        rider=rider,
        out_shape=[jax.ShapeDtypeStruct((l_rows, n_kv * gq), F32),
                   jax.ShapeDtypeStruct((n_kv * HEAD_DIM, t_rows), F32),
                   jax.ShapeDtypeStruct((n_kv * HEAD_DIM, t_rows), F32)],
        in_specs=[pl.BlockSpec((tq, gq), lambda g, j, i: (i, g)),
                  pl.BlockSpec((gq, tq), lambda g, j, i: (g, i)),
                  pl.BlockSpec((tk, HEAD_DIM), lambda g, j, i: (j, g)),
                  pl.BlockSpec((tk, gq), lambda g, j, i: (j, g)),
                  pl.BlockSpec((tq, gq), lambda g, j, i: (i, g)),
                  pl.BlockSpec((gq, tq), lambda g, j, i: (g, i)),
                  pl.BlockSpec((None, tq, 2), lambda g, j, i: (g, i, 0)),
                  pl.BlockSpec((None, tq, 2), lambda g, j, i: (g, i, 0))],
        out_specs=[pl.BlockSpec((l_rows, gq), lambda g, j, i: (0, g)),
                   pl.BlockSpec((HEAD_DIM, tk), lambda g, j, i: (g, j)),
                   pl.BlockSpec((HEAD_DIM, tk), lambda g, j, i: (g, j))])
    return dq, dkt, dvt, rode


def _wo_fwd(h, o, wo, mods, nct, tm, rider=None):
    l_rows, z = o.shape
    d = h.shape[1]

    def body(h_ref, o_ref, w_ref, mod_ref, y_ref, h1_ref):
        y = _dot(o_ref[...], w_ref[...])
        y_ref[...] = y.astype(BF16)
        h1_ref[...] = h_ref[...] + mod_ref[2:3, :] * y

    (y, h1), rode = _compute_call(
        body, name="wo_fwd", grid=(l_rows // tm,), operands=(h, o, wo, mods), rider=rider,
        out_shape=[jax.ShapeDtypeStruct((l_rows, d), BF16), jax.ShapeDtypeStruct((l_rows, d), F32)],
        in_specs=[pl.BlockSpec((tm, d), lambda i: (i + nct, 0)), pl.BlockSpec((tm, z), lambda i: (i, 0)),
                  _const_spec(wo.shape), pl.BlockSpec((None, 8, d), lambda i: (1, 0, 0))],
        out_specs=[pl.BlockSpec((tm, d), lambda i: (i, 0)), pl.BlockSpec((tm, d), lambda i: (i, 0))])
    return y, h1, rode


def _wo_bwd(dh1, y, o, wo, mods, n_kv, tm):
    l_rows, z = o.shape
    d = dh1.shape[1]

    def body(dh_ref, y_ref, o_ref, w_ref, mod_ref, dy_ref, do_ref, dot_ref, dl_ref, st_ref):
        i = pl.program_id(0)
        dh = dh_ref[...]
        dgate = _colsum(dh * y_ref[...].astype(F32))
        dy = (mod_ref[2:3, :] * dh).astype(BF16)
        dy_ref[...] = dy
        do = _dot_nt(dy, w_ref[...])
        do_ref[...] = do.astype(BF16)
        dot_ref[...] = do.T.astype(BF16)
        prod = do * o_ref[...].astype(F32)
        for g in range(n_kv):
            d0 = jnp.sum(prod[:, (2 * g) * HEAD_DIM:(2 * g + 1) * HEAD_DIM], axis=-1, keepdims=True)
            d1 = jnp.sum(prod[:, (2 * g + 1) * HEAD_DIM:(2 * g + 2) * HEAD_DIM], axis=-1, keepdims=True)
            dl_ref[g] = jnp.concatenate([d0, d1], axis=1)
        zero = jnp.zeros((1, d), F32)
        _acc_rows(st_ref, i == 0, [zero, zero, dgate])

    return pl.pallas_call(
        body, name="wo_bwd", grid=(l_rows // tm,),
        out_shape=[jax.ShapeDtypeStruct((l_rows, d), BF16), jax.ShapeDtypeStruct((l_rows, z), BF16),
                   jax.ShapeDtypeStruct((z, l_rows), BF16),
                   jax.ShapeDtypeStruct((n_kv, l_rows, 2), F32), jax.ShapeDtypeStruct((8, d), F32)],
        in_specs=[pl.BlockSpec((tm, d), lambda i: (i, 0)), pl.BlockSpec((tm, d), lambda i: (i, 0)),
                  pl.BlockSpec((tm, z), lambda i: (i, 0)), _const_spec(wo.shape),
                  pl.BlockSpec((None, 8, d), lambda i: (1, 0, 0))],
        out_specs=[pl.BlockSpec((tm, d), lambda i: (i, 0)), pl.BlockSpec((tm, z), lambda i: (i, 0)),
                   pl.BlockSpec((z, tm), lambda i: (0, i)),
                   pl.BlockSpec((n_kv, tm, 2), lambda i: (0, i, 0)), pl.BlockSpec((8, d), lambda i: (0, 0))],
        compiler_params=_params(("arbitrary",)),
    )(dh1, y, o, wo, mods)


def _qkv_bwd(h, dh_lat, dq, dkt, dvt, mods, ng, w, qg, kg, cos, sin, n_heads, n_kv, nct, tm):
    t_rows, d = h.shape
    qw, kw = n_heads * HEAD_DIM, n_kv * HEAD_DIM
    scale = HEAD_DIM ** -0.5

    def body(h_ref, dhl_ref, dq_ref, dkt_ref, dvt_ref, mod_ref, ng_ref, w_ref, qg_ref, kg_ref, cos_ref,
             sin_ref, dh_ref, a_ref, dqkv_ref, st_ref, dg_ref):
        i = pl.program_id(0)
        lat = (i >= nct).astype(F32)
        dk_t = dkt_ref[...].T * (1.0 / LOG2E)
        ngv, sc = ng_ref[...], mod_ref[1:2, :]
        a, xhat, rstd, n = _normmod(h_ref[...], ngv, mod_ref[0:1, :], sc)
        ab = a.astype(BF16)
        a_ref[...] = ab
        qkv = _dot(ab, w_ref[...])
        cosv, sinv = cos_ref[...], sin_ref[...]
        dqg = jnp.zeros((1, HEAD_DIM), F32)
        dkg = jnp.zeros((1, HEAD_DIM), F32)
        for hd in range(n_heads + n_kv):
            cols = slice(hd * HEAD_DIM, (hd + 1) * HEAD_DIM)
            is_q = hd < n_heads
            g = qg_ref[...] if is_q else kg_ref[...]
            _, hx, hr = _head_norm(qkv[:, cols], g)
            if is_q:
                dxr = dq_ref[:, cols] * (scale * lat)
            else:
                dxr = dk_t[:, (hd - n_heads) * HEAD_DIM:(hd - n_heads + 1) * HEAD_DIM]
            dxn = dxr * cosv + _swap_pairs(dxr * sinv)
            if is_q:
                dqg = dqg + _colsum(dxn * hx)
            else:
                dkg = dkg + _colsum(dxn * hx)
            dxh = dxn * g
            dx = hr * (dxh - hx * jnp.mean(dxh * hx, axis=-1, keepdims=True))
            dqkv_ref[:, cols] = dx.astype(BF16)
        dqkv_ref[:, qw + kw:] = dvt_ref[...].T.astype(BF16)
        da = _dot_nt(dqkv_ref[...], w_ref[...])
        dhn, dsh, dsc, dng = _normmod_bwd(da, xhat, rstd, n, ngv, sc)
        dh_ref[...] = dhl_ref[...] * lat + dhn
        _acc_rows(st_ref, _first_of_stream(i, nct), [dsh, dsc, jnp.zeros((1, d), F32), dng])
        _acc_rows(dg_ref, i == 0, [dqg, dkg])

    lat_map = lambda i: (jnp.maximum(i - nct, 0), 0)
    return pl.pallas_call(
        body, name="qkv_bwd", grid=(t_rows // tm,),
        out_shape=[jax.ShapeDtypeStruct((t_rows, d), F32), jax.ShapeDtypeStruct((t_rows, d), BF16),
                   jax.ShapeDtypeStruct((t_rows, qw + 2 * kw), BF16), jax.ShapeDtypeStruct((2, 8, d), F32),
                   jax.ShapeDtypeStruct((8, HEAD_DIM), F32)],
        in_specs=[pl.BlockSpec((tm, d), lambda i: (i, 0)), pl.BlockSpec((tm, d), lat_map),
                  pl.BlockSpec((tm, qw), lat_map), pl.BlockSpec((kw, tm), lambda i: (0, i)),
                  pl.BlockSpec((kw, tm), lambda i: (0, i)),
                  pl.BlockSpec((None, 8, d), lambda i: (_sid(i, nct), 0, 0)),
                  _const_spec((1, d)), _const_spec(w.shape), _const_spec((1, HEAD_DIM)),
                  _const_spec((1, HEAD_DIM)),
                  pl.BlockSpec((tm, HEAD_DIM), lambda i: (i, 0)), pl.BlockSpec((tm, HEAD_DIM), lambda i: (i, 0))],
        out_specs=[pl.BlockSpec((tm, d), lambda i: (i, 0)), pl.BlockSpec((tm, d), lambda i: (i, 0)),
                   pl.BlockSpec((tm, qw + 2 * kw), lambda i: (i, 0)),
                   pl.BlockSpec((None, 8, d), lambda i: (_sid(i, nct), 0, 0)),
                   pl.BlockSpec((8, HEAD_DIM), lambda i: (0, 0))],
        compiler_params=_params(("arbitrary",)),
    )(h, dh_lat, dq, dkt, dvt, mods, ng, w, qg, kg, cos, sin)


def _gmlp_core(a_bf, win_ref, lng, lnb, ws_ref, bst_ref, tm, half, with_grad=False):
    z = _dot(a_bf, win_ref[...])
    zu, zv = z[:, :half], z[:, half:]
    if with_grad:
        (u, zu), (v, zv) = _gelu_and_grad(zu), _gelu_and_grad(zv)
    else:
        u, v, zu, zv = _gelu(zu), _gelu(zv), None, None
    mu = jnp.mean(v, axis=-1, keepdims=True)
    vc = v - mu
    rstd_v = lax.rsqrt(jnp.mean(vc * vc, axis=-1, keepdims=True) + EPS)
    vhat = vc * rstd_v
    vln = (vhat * lng + lnb).astype(BF16)
    gw = half // GMLP_GROUPS
    rows = []
    for ch in range(tm // CHUNK):
        rs = slice(ch * CHUNK, (ch + 1) * CHUNK)
        cols = []
        for g in range(GMLP_GROUPS):
            cs = slice(g * gw, (g + 1) * gw)
            cols.append(_dot(ws_ref[g], vln[rs, cs]) + bst_ref[:, g:g + 1])
        rows.append(jnp.concatenate(cols, axis=1))
    sv = rows[0] if len(rows) == 1 else jnp.concatenate(rows, axis=0)
    return zu, zv, u, vhat, rstd_v, vln, sv


def _gmlp_fwd(h, mods, ng, win, lng, lnb, ws, bst, wout, tm):
    l_rows, d = h.shape
    half = wout.shape[0]

    def body(h_ref, mod_ref, ng_ref, win_ref, lng_ref, lnb_ref, ws_ref, bst_ref, wout_ref, y_ref, h1_ref):
        hv = h_ref[...]
        a = _normmod(hv, ng_ref[...], mod_ref[0:1, :], mod_ref[1:2, :])[0]
        _, _, u, _, _, _, sv = _gmlp_core(a.astype(BF16), win_ref, lng_ref[...], lnb_ref[...], ws_ref,
                                          bst_ref, tm, half)
        y = _dot((u * sv).astype(BF16), wout_ref[...])
        y_ref[...] = y.astype(BF16)
        h1_ref[...] = hv + mod_ref[2:3, :] * y

    return pl.pallas_call(
        body, name="gmlp_fwd", grid=(l_rows // tm,),
        out_shape=[jax.ShapeDtypeStruct((l_rows, d), BF16), jax.ShapeDtypeStruct((l_rows, d), F32)],
        in_specs=[pl.BlockSpec((tm, d), lambda i: (i, 0)), pl.BlockSpec((None, 8, d), lambda i: (1, 0, 0)),
                  _const_spec((1, d)), _const_spec(win.shape), _const_spec((1, half)), _const_spec((1, half)),
                  _const_spec(ws.shape), _const_spec(bst.shape), _const_spec(wout.shape)],
        out_specs=[pl.BlockSpec((tm, d), lambda i: (i, 0)), pl.BlockSpec((tm, d), lambda i: (i, 0))],
        compiler_params=_params(("arbitrary",)),
    )(h, mods, ng, win, lng, lnb, ws, bst, wout)


def _gmlp_bwd(dh1, h, y, mods, ng, win, lng, lnb, ws, wst, bst, wout, tm):
    l_rows, d = h.shape
    half = wout.shape[0]
    gw = half // GMLP_GROUPS

    def body(dh_ref, h_ref, y_ref, mod_ref, ng_ref, win_ref, lng_ref, lnb_ref, ws_ref, wst_ref, bst_ref,
             wout_ref, dho_ref, a_ref, dz_ref, gt_ref, dy_ref, st_ref, ln_ref, dws_ref, dbs_ref):
        i = pl.program_id(0)
        ngv, sc = ng_ref[...], mod_ref[1:2, :]
        lngv = lng_ref[...]
        a, xhat, rstd, n = _normmod(h_ref[...], ngv, mod_ref[0:1, :], sc)
        ab = a.astype(BF16)
        a_ref[...] = ab
        gu, gv, u, vhat, rstd_v, vln, sv = _gmlp_core(ab, win_ref, lngv, lnb_ref[...], ws_ref, bst_ref,
                                                      tm, half, with_grad=True)
        gt_ref[...] = (u * sv).astype(BF16)
        dh = dh_ref[...]
        dgate = _colsum(dh * y_ref[...].astype(F32))
        dy = (mod_ref[2:3, :] * dh).astype(BF16)
        dy_ref[...] = dy
        dgated = _dot_nt(dy, wout_ref[...])
        du = dgated * sv
        dsv = (dgated * u).astype(BF16)

        @pl.when(i == 0)
        def _():
            dws_ref[...] = jnp.zeros_like(dws_ref)
            dbs_ref[...] = jnp.zeros_like(dbs_ref)
        lane = lax.broadcasted_iota(jnp.int32, (CHUNK, 128), 1)
        dbs = jnp.zeros((CHUNK, 128), F32)
        rows = []
        for ch in range(tm // CHUNK):
            rs = slice(ch * CHUNK, (ch + 1) * CHUNK)
            cols = []
            for g in range(GMLP_GROUPS):
                cs = slice(g * gw, (g + 1) * gw)
                dsv_cg = dsv[rs, cs]
                dws_ref[g] += _dot_nt(dsv_cg, vln[rs, cs])
                cols.append(_dot(wst_ref[g], dsv_cg))
                dbs = dbs + jnp.where(lane == g, jnp.sum(dsv_cg.astype(F32), axis=-1, keepdims=True), 0.0)
            rows.append(jnp.concatenate(cols, axis=1))
        dbs_ref[...] += dbs
        dvln = rows[0] if len(rows) == 1 else jnp.concatenate(rows, axis=0)
        dlng = _colsum(dvln * vhat)
        dlnb = _colsum(dvln)
        dvh = dvln * lngv
        dv = rstd_v * (dvh - jnp.mean(dvh, axis=-1, keepdims=True)
                       - vhat * jnp.mean(dvh * vhat, axis=-1, keepdims=True))
        dz_ref[:, :half] = (du * gu).astype(BF16)
        dz_ref[:, half:] = (dv * gv).astype(BF16)
        da = _dot_nt(dz_ref[...], win_ref[...])
        dhn, dsh, dsc, dng = _normmod_bwd(da, xhat, rstd, n, ngv, sc)
        dho_ref[...] = dh + dhn
        _acc_rows(st_ref, i == 0, [dsh, dsc, dgate, dng])
        _acc_rows(ln_ref, i == 0, [dlng, dlnb])

    row = lambda w: pl.BlockSpec((tm, w), lambda i: (i, 0))
    return pl.pallas_call(
        body, name="gmlp_bwd", grid=(l_rows // tm,),
        out_shape=[jax.ShapeDtypeStruct((l_rows, d), F32), jax.ShapeDtypeStruct((l_rows, d), BF16),
                   jax.ShapeDtypeStruct((l_rows, 2 * half), BF16), jax.ShapeDtypeStruct((l_rows, half), BF16),
                   jax.ShapeDtypeStruct((l_rows, d), BF16), jax.ShapeDtypeStruct((8, d), F32),
                   jax.ShapeDtypeStruct((8, half), F32), jax.ShapeDtypeStruct(ws.shape, F32),
                   jax.ShapeDtypeStruct((CHUNK, 128), F32)],
        in_specs=[row(d), row(d), row(d), pl.BlockSpec((None, 8, d), lambda i: (1, 0, 0)),
                  _const_spec((1, d)), _const_spec(win.shape), _const_spec((1, half)), _const_spec((1, half)),
                  _const_spec(ws.shape), _const_spec(ws.shape), _const_spec(bst.shape), _const_spec(wout.shape)],
        out_specs=[row(d), row(d), row(2 * half), row(half), row(d),
                   pl.BlockSpec((8, d), lambda i: (0, 0)), pl.BlockSpec((8, half), lambda i: (0, 0)),
                   pl.BlockSpec(ws.shape, lambda i: (0, 0, 0)), pl.BlockSpec((CHUNK, 128), lambda i: (0, 0))],
        compiler_params=_params(("arbitrary",)),
    )(dh1, h, y, mods, ng, win, lng, lnb, ws, wst, bst, wout)


def _head(h, final_g, target, tm):
    l_rows, d = h.shape
    n_tiles = l_rows // tm

    def body(h_ref, g_ref, t_ref, dh_ref, loss_ref, dg_ref, acc_ref):
        i = pl.program_id(0)
        g = g_ref[...]
        hv = h_ref[...]
        rstd = lax.rsqrt(jnp.mean(hv * hv, axis=-1, keepdims=True) + EPS)
        xhat = hv * rstd
        e = xhat * g - t_ref[...]
        dout = e * (1.0 / d)
        dxhat = dout * g
        dh_ref[...] = rstd * (dxhat - xhat * jnp.mean(dxhat * xhat, axis=-1, keepdims=True))
        _acc_rows(dg_ref, i == 0, [_colsum(dout * xhat)])
        _acc_rows(acc_ref, i == 0, [_colsum(e * e)])

        @pl.when(i == n_tiles - 1)
        def _():
            total = jnp.sum(acc_ref[0:1, :], axis=-1, keepdims=True) * (0.5 / d)
            loss_ref[...] = jnp.broadcast_to(total, loss_ref.shape)

    return pl.pallas_call(
        body, name="loss_head", grid=(n_tiles,),
        out_shape=[jax.ShapeDtypeStruct((l_rows, d), F32), jax.ShapeDtypeStruct((8, 128), F32),
                   jax.ShapeDtypeStruct((8, d), F32)],
        in_specs=[pl.BlockSpec((tm, d), lambda i: (i, 0)), _const_spec((1, d)),
                  pl.BlockSpec((tm, d), lambda i: (i, 0))],
        out_specs=[pl.BlockSpec((tm, d), lambda i: (i, 0)), pl.BlockSpec((8, 128), lambda i: (0, 0)),
                   pl.BlockSpec((8, d), lambda i: (0, 0))],
        scratch_shapes=[pltpu.VMEM((8, d), F32)],
        compiler_params=_params(("arbitrary",)),
    )(h, final_g, target)


def _adamw(w, gparts, m, v, name):
    shape = w.shape
    cols = shape[-1]
    rows = int(np.prod(shape[:-1])) if len(shape) > 1 else 1
    pieces = list(gparts) if isinstance(gparts, (list, tuple)) else [gparts]
    n_pieces = len(pieces)
    nparts = pieces[0].shape[0]
    piece_rows = rows // n_pieces
    w2, m2, v2 = (t.reshape(rows, cols) for t in (w, m, v))
    pieces = [g.reshape(nparts, piece_rows, cols) for g in pieces]
    tr = piece_rows
    part_bytes = nparts * cols * pieces[0].dtype.itemsize
    for cand in (1024, 512, 256, 128, 64, 32, 16, 8):
        if piece_rows * max(part_bytes, cols * 4) <= (2 << 20):
            break
        if piece_rows % cand == 0 and cand < piece_rows:
            tr = cand
            if cand * max(part_bytes, cols * 4) <= (2 << 20):
                break
    per_piece = piece_rows // tr
    c1 = 1.0 - ADAM_B1 ** ADAM_STEP
    c2 = 1.0 - ADAM_B2 ** ADAM_STEP

    def update(w_ref, g_ref, m_ref, v_ref, go_ref, d_ref, mo_ref, vo_ref):
        g = g_ref[0].astype(F32)
        for k in range(1, nparts):
            g = g + g_ref[k].astype(F32)
        mn = ADAM_B1 * m_ref[...] + (1.0 - ADAM_B1) * g
        vn = ADAM_B2 * v_ref[...] + (1.0 - ADAM_B2) * (g * g)
        go_ref[...] = g
        mo_ref[...] = mn
        vo_ref[...] = vn
        d_ref[...] = -ADAM_LR * ((mn / c1) / (jnp.sqrt(vn / c2) + ADAM_EPS) + ADAM_WD * w_ref[...])

    def body(w_ref, *refs):
        g_refs, (m_ref, v_ref, go_ref, d_ref, mo_ref, vo_ref) = refs[:n_pieces], refs[n_pieces:]
        if n_pieces == 1:
            update(w_ref, g_refs[0], m_ref, v_ref, go_ref, d_ref, mo_ref, vo_ref)
        else:
            piece = pl.program_id(0) // per_piece
            for k in range(n_pieces):
                pl.when(piece == k)(functools.partial(update, w_ref, g_refs[k], m_ref, v_ref, go_ref, d_ref, mo_ref, vo_ref))

    def piece_spec(k):
        return pl.BlockSpec((nparts, tr, cols), lambda i: (0, jnp.clip(i - k * per_piece, 0, per_piece - 1), 0))

    spec = pl.BlockSpec((tr, cols), lambda i: (i, 0))
    outs = pl.pallas_call(
        body, name=name, grid=(rows // tr,),
        out_shape=[jax.ShapeDtypeStruct((rows, cols), F32)] * 4,
        in_specs=[spec] + [piece_spec(k) for k in range(n_pieces)] + [spec, spec],
        out_specs=[spec] * 4,
        compiler_params=_params(("arbitrary",)),
    )(w2, *pieces, m2, v2)
    return tuple(o.reshape(shape) for o in outs)


def _natural_cols(g):
    return jnp.moveaxis(g, 0, -2).reshape(g.shape[1:-1] + (N_DEV * g.shape[-1],))


def _natural_rows(g):
    return jnp.moveaxis(g, 0, -3).reshape(g.shape[1:-2] + (N_DEV * g.shape[-2], g.shape[-1]))


def _shard_rows(full):
    r = full.shape[-2] // N_DEV
    return jnp.moveaxis(full.reshape(full.shape[:-2] + (N_DEV, r, full.shape[-1])), -3, 0)


def _my_cols(gathered, me, n):
    return lax.dynamic_slice_in_dim(gathered, me * n, n, axis=gathered.ndim - 1)


def kernel(x, c, ctx, c_ctx, ada_w, ada_b, norm_g, mlp_w1, mlp_w2, pool_w, pool_scale, attn_w_qkv, attn_w_o, attn_q_g, attn_k_g, gm_w_in, gm_ln_g, gm_ln_b, gm_ws, gm_bs, gm_w_out, final_g, loss_target, m_c_ctx, m_ada_w, m_ada_b, m_norm_g, m_mlp_w1, m_mlp_w2, m_pool_w, m_pool_scale, m_attn_w_qkv, m_attn_w_o, m_attn_q_g, m_attn_k_g, m_gm_w_in, m_gm_ln_g, m_gm_ln_b, m_gm_ws, m_gm_bs, m_gm_w_out, m_final_g, v_c_ctx, v_ada_w, v_ada_b, v_norm_g, v_mlp_w1, v_mlp_w2, v_pool_w, v_pool_scale, v_attn_w_qkv, v_attn_w_o, v_attn_q_g, v_attn_k_g, v_gm_w_in, v_gm_ln_g, v_gm_ln_b, v_gm_ws, v_gm_bs, v_gm_w_out, v_final_g):
    l_len, d = x.shape[1], x.shape[2]
    c_len = ctx.shape[1]
    n_layers = ada_w.shape[0]
    assert n_layers == 4 and x.shape[0] == 1
    n_heads = d // HEAD_DIM
    n_kv = n_heads // 2
    half = gm_w_out.shape[1] * N_DEV
    tm = c_len if c_len <= 256 else 256
    assert c_len % tm == 0 and l_len % tm == 0 and tm % CHUNK == 0 and l_len % GRID_W == 0
    nct = c_len // tm
    me = _dev_index(*_coords())
    n_ada = ada_w.shape[-1]

    first = [t.astype(BF16) for t in (mlp_w1[0], mlp_w2[0], pool_w, attn_w_qkv[0])]
    small = [c, norm_g.reshape(n_layers * 2, -1), pool_scale, gm_ln_g, gm_ln_b]
    w1_0g, w2_0g, pool_g, qkv_g, c_all, ng_g, ps_g, lng_g, lnb_g = _all_gather(first + small, "gather_first")
    c_all = c_all.reshape(N_DEV, d)
    later = _GatherAcrossChips([t.astype(BF16) for t in
                                (mlp_w1[1], mlp_w1[2], mlp_w1[3], mlp_w2[1], mlp_w2[2], mlp_w2[3],
                                 attn_w_o[0], gm_w_in[0], gm_w_out[0])])
    pool_wf = _natural_rows(pool_g)
    wqkv = _natural_cols(qkv_g)
    ng_full = _natural_cols(ng_g.reshape(N_DEV, n_layers * 2, 1, -1)).reshape(n_layers, 2, 1, d)
    ps_full = _natural_cols(ps_g.reshape(N_DEV, 2, 1, -1))
    lng_full = _natural_cols(lng_g.reshape(N_DEV, 1, -1))
    lnb_full = _natural_cols(lnb_g.reshape(N_DEV, 1, -1))

    c_ctx2 = c_ctx.reshape(1, d)
    ada_b_loc = lax.dynamic_slice_in_dim(ada_b, me * n_ada, n_ada, axis=1).reshape(n_layers, 1, n_ada)
    (mod_g,) = _all_gather([_mods_local(c_all, c_ctx2, ada_w, ada_b_loc)], "gather_mods")
    mod_full = jnp.moveaxis(mod_g, 0, 2).reshape(n_layers, 16, 6, d)
    mod_lat = lax.dynamic_index_in_dim(mod_full, me, axis=1, keepdims=False)
    mod_ctx = mod_full[:, 8]
    mods = jnp.stack([mod_ctx, mod_lat], axis=1)
    mods = jnp.concatenate([mods, jnp.zeros((n_layers, 2, 2, d), F32)], axis=2)

    bands = _pool_bands(tm)
    cos, sin = _rope_tables(c_len, l_len)
    ws_bf = gm_ws[0].astype(BF16)
    wst_bf = jnp.swapaxes(gm_ws[0], 1, 2).astype(BF16)
    bst = jnp.zeros((CHUNK, 128), F32).at[:, :GMLP_GROUPS].set(gm_bs[0].T)
    ng = lambda i, j: ng_full[i, j]

    h0 = jnp.concatenate([ctx[0], x[0]], axis=0)
    y0, h1 = _pool_fwd(h0, mods[0], ng(0, 0), pool_wf[0].astype(BF16), ps_full[0], bands, nct, tm, c_len, l_len)
    h2, p0, ym0 = _mlp_fwd(h1, mods[0], ng(0, 1), w1_0g, w2_0g, nct, tm)
    q, k, v1, qt = _qkv_fwd(h2, mods[1], ng(1, 0), wqkv, attn_q_g, attn_k_g, cos, sin, n_heads, n_kv, nct, tm)
    o, lse, later_g = _flash_fwd(q, k, v1, n_kv, 4 * tm, tm, rider=later)
    (wo_g,) = _exchange_call(_ForwardToSibling(later_g[6:7]), "forward_wo")
    wo = _natural_rows(wo_g)
    y1, h3, rest_g = _wo_fwd(h2, o, wo, mods[1], nct, tm, rider=_ForwardToSibling(later_g[0:6] + later_g[7:9]))
    w1 = [w1_0g] + rest_g[0:3]
    w2 = [w2_0g] + rest_g[3:6]
    win = _natural_cols(rest_g[6])
    wout = _natural_rows(rest_g[7])
    tm_lat = 2 * tm
    h4, p1, ym1 = _mlp_fwd(h3, mods[1], ng(1, 1), w1[1], w2[1], 0, tm_lat)
    y2, h5 = _gmlp_fwd(h4, mods[2], ng(2, 0), win, lng_full, lnb_full, ws_bf, bst, wout, tm)
    h6, p2, ym2 = _mlp_fwd(h5, mods[2], ng(2, 1), w1[2], w2[2], 0, tm_lat)
    y3, h7 = _pool_fwd(h6, mods[3], ng(3, 0), pool_wf[1].astype(BF16), ps_full[1], bands, 0, tm, c_len, l_len)
    h8, p3, ym3 = _mlp_fwd(h7, mods[3], ng(3, 1), w1[3], w2[3], 0, tm_lat)
    dh, loss_part, dfinal = _head(h8, final_g.reshape(1, d), loss_target[0], tm_lat)

    dw1, dw2, st_mlp = [None] * 4, [None] * 4, [None] * 4

    def mlp_back(i, dh, h_in, p, ym, nct_i):
        dh_in, m_bf, du, dacc, st, _ = _mlp_bwd(dh, h_in, p, ym, mods[i], ng(i, 1), w1[i], w2[i], nct_i, tm_lat)
        dw1[i] = _tn_matmul(m_bf, du, "tn_w1", col_shards=True)
        dw2[i] = _shard_rows(_tn_matmul(p, dacc, "tn_w2", square_x=True))
        st_mlp[i] = st
        return dh_in

    dh = mlp_back(3, dh, h7, p3, ym3, 0)
    dh, dpw1, st_pool3, _ = _pool_bwd(dh, h6, y3, mods[3], ng(3, 0), pool_wf[1].astype(BF16), ps_full[1], bands,
                                      0, tm, c_len, l_len, False)
    dh = mlp_back(2, dh, h5, p2, ym2, 0)
    dh, a_bf, dz, gated, dy, st_g, st_ln, dws, dbst = _gmlp_bwd(
        dh, h4, y2, mods[2], ng(2, 0), win, lng_full, lnb_full, ws_bf, wst_bf, bst, wout, tm)
    dwin = _tn_matmul(a_bf, dz, "tn_gm_in", col_shards=True)
    dwout = _tn_matmul(gated, dy, "tn_gm_out")
    dh = mlp_back(1, dh, h3, p1, ym1, 0)
    dy1, do, dot, delta, st_wo = _wo_bwd(dh, y1, o, wo, mods[1], n_kv, tm)
    dwo = _tn_matmul(o, dy1, "tn_wo")
    grads_mid = _AllToAll(dw1[1:] + dw2[1:] + [_shard_rows(dpw1.astype(BF16)), _shard_rows(dwo), dwin,
                                               _shard_rows(dwout)])
    dq, dkt, dvt, rode = _flash_bwd(q, qt, k, v1, do, dot, lse, delta, n_kv, 2 * tm, tm, rider=grads_mid)
    g_w1, g_w2, (g_pool1, g_wo, g_gin, g_gout) = [None] + rode[0:3], [None] + rode[3:6], rode[6:]
    dh, a_bf, dqkv, st_q, dgains = _qkv_bwd(h2, dh, dq, dkt, dvt, mods[1], ng(1, 0), wqkv, attn_q_g, attn_k_g,
                                            cos, sin, n_heads, n_kv, nct, tm)
    dwqkv = _tn_matmul(a_bf, dqkv, "tn_qkv", col_shards=True)
    dh, m_bf, du, dacc, st_mlp[0], (g_qkv,) = _mlp_bwd(dh, h1, p0, ym0, mods[0], ng(0, 1), w1[0], w2[0], nct, tm,
                                                       rider=_AllToAll([dwqkv]))
    dw1_0 = _tn_matmul(m_bf, du, "tn_w1", col_shards=True)
    dw2_0, (g_w1[0],) = _tn_matmul(p0, dacc, "tn_w2", square_x=True, rider=_AllToAll([dw1_0]))
    grad_x, dpw0, st_pool0, (g_w2[0],) = _pool_bwd(dh, h0, y0, mods[0], ng(0, 0), pool_wf[0].astype(BF16),
                                                   ps_full[0], bands, nct, tm, c_len, l_len, True,
                                                   rider=_AllToAll([_shard_rows(dw2_0)]))

    mix_lat = [st_pool0[-1], st_q[1] + st_wo, st_g, st_pool3[-1]]
    mlp_lat = [st[-1] for st in st_mlp]
    dmod_lat = jnp.stack([jnp.concatenate([mix_lat[i][0:3], mlp_lat[i][0:3]]) for i in range(n_layers)])
    dmod_ctx = jnp.stack([jnp.concatenate([st_pool0[0][0:3], st_mlp[0][0][0:3]]),
                          jnp.concatenate([st_q[0][0:2], jnp.zeros((4, d), F32)]),
                          jnp.zeros((6, d), F32), jnp.zeros((6, d), F32)])
    dng_part = jnp.stack([jnp.stack([mix_lat[0][3] + st_pool0[0][3], mlp_lat[0][3] + st_mlp[0][0][3]]),
                          jnp.stack([mix_lat[1][3] + st_q[0][3], mlp_lat[1][3]]),
                          jnp.stack([mix_lat[2][3], mlp_lat[2][3]]),
                          jnp.stack([mix_lat[3][3], mlp_lat[3][3]])])
    dps_part = jnp.stack([mix_lat[0][4] + st_pool0[0][4], mix_lat[3][4]])
    small_parts = [dmod_lat.reshape(n_layers * 6, d), dmod_ctx.reshape(n_layers * 6, d),
                   dng_part.reshape(n_layers * 2, d), dps_part, st_ln, dgains, dws.reshape(-1, CHUNK),
                   dbst, dfinal, loss_part]
    (gm_lat, gm_ctx, g_ng, g_ps, g_ln, g_gains, g_ws, g_bst, g_final, loss_all, g_pool0) = _all_gather(
        small_parts, "gather_small_grads", extra=_AllToAll([_shard_rows(dpw0.astype(BF16))]))
    g_pool = jnp.stack([g_pool0, g_pool1], axis=1)

    gm_lat4 = gm_lat.reshape(N_DEV, n_layers, 6 * d)
    gm_ctx4 = gm_ctx.reshape(N_DEV, n_layers, 6 * d)
    dm_lat_loc = jnp.moveaxis(_my_cols(gm_lat4, me, n_ada), 0, 1)
    dm_ctx_loc = jnp.moveaxis(_my_cols(gm_ctx4, me, n_ada), 0, 1)
    g_ada_w, ds_part = _ada_grads(c_all, c_ctx2, ada_w, dm_lat_loc, dm_ctx_loc)
    (ds_all,) = _all_gather([ds_part], "gather_dsctx")
    g_c_ctx, loss_sum = _cctx_grad_and_loss(ds_all, c_ctx2, loss_all)
    g_c_ctx = g_c_ctx.reshape(d)

    n_ng = norm_g.shape[-1]
    n_ps = pool_scale.shape[-1]
    n_ln = gm_ln_g.shape[-1]
    gparts = {
        "c_ctx": g_c_ctx[None],
        "ada_w": g_ada_w[None],
        "ada_b": jnp.concatenate([gm_lat4, gm_ctx4], axis=0),
        "norm_g": _my_cols(g_ng.reshape(N_DEV, n_layers, 2, d), me, n_ng),
        "mlp_w1": g_w1, "mlp_w2": g_w2, "pool_w": g_pool,
        "pool_scale": _my_cols(g_ps, me, n_ps),
        "attn_w_qkv": g_qkv[:, None], "attn_w_o": g_wo[:, None],
        "attn_q_g": g_gains[:, 0:1], "attn_k_g": g_gains[:, 1:2],
        "gm_w_in": g_gin[:, None],
        "gm_ln_g": _my_cols(g_ln[:, 0:1], me, n_ln), "gm_ln_b": _my_cols(g_ln[:, 1:2], me, n_ln),
        "gm_ws": g_ws.reshape((N_DEV,) + gm_ws.shape),
        "gm_bs": jnp.swapaxes(g_bst[:, :, :GMLP_GROUPS], 1, 2)[:, None],
        "gm_w_out": g_gout[:, None],
        "final_g": g_final[:, 0],
    }
    weights = dict(c_ctx=(c_ctx, m_c_ctx, v_c_ctx), ada_w=(ada_w, m_ada_w, v_ada_w), ada_b=(ada_b, m_ada_b, v_ada_b),
                   norm_g=(norm_g, m_norm_g, v_norm_g), mlp_w1=(mlp_w1, m_mlp_w1, v_mlp_w1),
                   mlp_w2=(mlp_w2, m_mlp_w2, v_mlp_w2), pool_w=(pool_w, m_pool_w, v_pool_w),
                   pool_scale=(pool_scale, m_pool_scale, v_pool_scale),
                   attn_w_qkv=(attn_w_qkv, m_attn_w_qkv, v_attn_w_qkv), attn_w_o=(attn_w_o, m_attn_w_o, v_attn_w_o),
                   attn_q_g=(attn_q_g, m_attn_q_g, v_attn_q_g), attn_k_g=(attn_k_g, m_attn_k_g, v_attn_k_g),
                   gm_w_in=(gm_w_in, m_gm_w_in, v_gm_w_in), gm_ln_g=(gm_ln_g, m_gm_ln_g, v_gm_ln_g),
                   gm_ln_b=(gm_ln_b, m_gm_ln_b, v_gm_ln_b), gm_ws=(gm_ws, m_gm_ws, v_gm_ws),
                   gm_bs=(gm_bs, m_gm_bs, v_gm_bs), gm_w_out=(gm_w_out, m_gm_w_out, v_gm_w_out),
                   final_g=(final_g, m_final_g, v_final_g))
    grads, deltas, new_m, new_v = [], [], [], []
    for wname, (w_, m_, v_) in weights.items():
        g_, d_, nm_, nv_ = _adamw(w_, gparts[wname], m_, v_, "adamw_" + wname)
        grads.append(g_)
        deltas.append(d_)
        new_m.append(nm_)
        new_v.append(nv_)

    return (loss_sum[0, 0], grad_x[None], *grads, *deltas, *new_m, *new_v)
```

```python
import functools
import math

import numpy as np
import jax
import jax.numpy as jnp
from jax import lax
from jax.experimental import pallas as pl
from jax.experimental.pallas import tpu as pltpu

F32 = jnp.float32
BF16 = jnp.bfloat16
MESH_ID = pl.DeviceIdType.MESH

N_DEV = 8
EPS = 1e-6
HEAD_DIM = 128
GRID_W = 64
ROPE_BASE = 10000.0
CHUNK = 128
POOL_WINDOWS = (2, 4, 8, 16)
POOL_GROUPS = 4
POOL_HALO = 8
GMLP_GROUPS = 8
ADAM_LR, ADAM_B1, ADAM_B2, ADAM_EPS, ADAM_WD, ADAM_STEP = 0.001, 0.9, 0.999, 1e-08, 0.01, 10

V7X_VMEM_BYTES = 64 << 20
VMEM_LIMIT_BIG = V7X_VMEM_BYTES - (8 << 20)
FLASH_TK_CAP = 768
LOG2E = math.log2(math.e)
Q_SCALE = HEAD_DIM ** -0.5 * LOG2E


def _params(sem, vmem=VMEM_LIMIT_BIG):
    return pltpu.CompilerParams(dimension_semantics=sem, vmem_limit_bytes=vmem)


def _const_spec(shape):
    nd = len(shape)
    return pl.BlockSpec(shape, lambda *_: (0,) * nd, pipeline_mode=pl.Buffered(1))


def _dot(a, b):
    return jnp.dot(a, b, preferred_element_type=F32)


def _dot_nt(a, b):
    return lax.dot_general(a, b, (((1,), (1,)), ((), ())), preferred_element_type=F32)


def _dot_tn(a, b):
    return lax.dot_general(a, b, (((0,), (0,)), ((), ())), preferred_element_type=F32)


def _colsum(x):
    return jnp.sum(x, axis=0, keepdims=True)


def _sid(i, nct):
    if nct == 0:
        return 1
    return jnp.where(i >= nct, 1, 0)


def _n_streams(nct):
    return 2 if nct else 1


def _stat_sid(i, nct):
    return _sid(i, nct) if nct else 0


def _first_of_stream(i, nct):
    if nct == 0:
        return i == 0
    return jnp.logical_or(i == 0, i == nct)


def _normmod(h, ng, sh, sc):
    rstd = lax.rsqrt(jnp.mean(h * h, axis=-1, keepdims=True) + EPS)
    xhat = h * rstd
    n = xhat * ng
    return n * (1.0 + sc) + sh, xhat, rstd, n


def _normmod_bwd(da, xhat, rstd, n, ng, sc):
    dsh = _colsum(da)
    dsc = _colsum(da * n)
    dn = da * (1.0 + sc)
    dng = _colsum(dn * xhat)
    dxhat = dn * ng
    dh = rstd * (dxhat - xhat * jnp.mean(dxhat * xhat, axis=-1, keepdims=True))
    return dh, dsh, dsc, dng


def _acc_rows(ref, first, rows):
    @pl.when(first)
    def _():
        ref[...] = jnp.zeros_like(ref)
    for r, val in enumerate(rows):
        ref[r:r + 1, :] = ref[r:r + 1, :] + val


_GELU_C = math.sqrt(2.0 / math.pi)


def _gelu(x):
    t = jnp.tanh((_GELU_C * x) * (1.0 + 0.044715 * (x * x)))
    hx = 0.5 * x
    return hx + hx * t


def _gelu_and_grad(x):
    x2 = x * x
    t = jnp.tanh((_GELU_C * x) * (1.0 + 0.044715 * x2))
    hx = 0.5 * x
    g = hx + hx * t
    dg = (0.5 + 0.5 * t) + (hx * (1.0 - t * t)) * (_GELU_C + (3.0 * 0.044715 * _GELU_C) * x2)
    return g, dg


def _coords():
    return lax.axis_index("x"), lax.axis_index("y"), lax.axis_index("c")


def _dev_index(px, py, pc):
    return 4 * px + 2 * py + pc


def _all_gather(xs, name, extra=None):
    n = len(xs)
    e_in, e_in_specs, e_out, e_out_specs, e_scratch = _rider_parts(extra)
    ne = len(e_in)

    def body(*refs):
        x_refs, e_x = refs[:n], refs[n:n + ne]
        o_refs, e_o = refs[n + ne:2 * n + ne], refs[2 * n + ne:2 * n + 2 * ne]
        send_sems, recv_sems, local_sems = refs[2 * n + 2 * ne:2 * n + 2 * ne + 3]
        e_sems = refs[2 * n + 2 * ne + 3:]
        if extra is not None:
            extra.start(e_x, e_o, e_sems)
        x, y, c = _coords()
        me, sibling = (x, y, c), (x, y, 1 - c)
        chips = [(1 - x, y), (x, 1 - y), (1 - x, 1 - y)]

        def copy(a, k, block, to, src=None):
            dst = o_refs[a].at[_dev_index(*block)]
            return pltpu.make_async_remote_copy(
                src_ref=dst if src is None else src, dst_ref=dst,
                send_sem=send_sems.at[7 * a + k], recv_sem=recv_sems.at[7 * a + k],
                device_id=to, device_id_type=MESH_ID)

        mine = [pltpu.make_async_copy(x_refs[a], o_refs[a].at[_dev_index(*me)], local_sems.at[a])
                for a in range(n)]
        for cp in mine:
            cp.start()
        first = []
        for a in range(n):
            first.append(copy(a, 0, me, sibling, src=x_refs[a]))
            first += [copy(a, 1 + j, me, (*chip, c), src=x_refs[a]) for j, chip in enumerate(chips)]
        for cp in first:
            cp.start()
        passed = []
        for a in range(n):
            for j, chip in enumerate(chips):
                copy(a, 1 + j, (*chip, c), me).wait_recv()
                fwd = copy(a, 4 + j, (*chip, c), sibling)
                fwd.start()
                passed.append(fwd)
        for a in range(n):
            copy(a, 0, sibling, me).wait_recv()
            for j, chip in enumerate(chips):
                copy(a, 4 + j, (*chip, 1 - c), me).wait_recv()
        for cp in first + passed:
            cp.wait_send()
        for cp in mine:
            cp.wait()
        if extra is not None:
            extra.finish(e_x, e_o, e_sems)

    any_spec = pl.BlockSpec(memory_space=pl.ANY)
    outs = pl.pallas_call(
        body, name=name,
        out_shape=[jax.ShapeDtypeStruct((N_DEV,) + x.shape, x.dtype) for x in xs] + e_out,
        in_specs=[any_spec] * n + e_in_specs, out_specs=[any_spec] * n + e_out_specs,
        scratch_shapes=[pltpu.SemaphoreType.DMA((7 * n,)), pltpu.SemaphoreType.DMA((7 * n,)),
                        pltpu.SemaphoreType.DMA((n,))] + e_scratch,
    )(*xs, *e_in)
    return list(outs)


class _Exchange:
    per_array = 0
    in_place = False

    def __init__(self, xs):
        self.xs = list(xs)
        n = len(self.xs)
        self.out_shapes = self._out_shapes()
        self.scratch = [pltpu.SemaphoreType.DMA((self.per_array * n,)),
                        pltpu.SemaphoreType.DMA((self.per_array * n,)),
                        pltpu.SemaphoreType.DMA((n,))]

    def _out_shapes(self):
        raise NotImplementedError

    def _copies(self, x_refs, o_refs, sems):
        raise NotImplementedError

    def start(self, x_refs, o_refs, sems):
        mine, sends, _ = self._copies(x_refs, o_refs, sems)
        for cp in mine + sends:
            cp.start()

    def finish(self, x_refs, o_refs, sems):
        mine, sends, arrivals = self._copies(x_refs, o_refs, sems)
        for make in arrivals:
            make().wait_recv()
        for cp in sends:
            cp.wait_send()
        for cp in mine:
            cp.wait()


def _remote(src, dst, sems, k, to):
    return pltpu.make_async_remote_copy(src_ref=src, dst_ref=dst, send_sem=sems[0].at[k], recv_sem=sems[1].at[k],
                                        device_id=to, device_id_type=MESH_ID)


class _GatherAcrossChips(_Exchange):
    per_array = 4

    def _out_shapes(self):
        return [jax.ShapeDtypeStruct((N_DEV,) + x.shape, x.dtype) for x in self.xs]

    def _copies(self, x_refs, o_refs, sems):
        x, y, c = _coords()
        targets = [(x, y, 1 - c), (1 - x, y, c), (x, 1 - y, c), (1 - x, 1 - y, c)]
        mine, sends, arrivals = [], [], []
        for a, (x_ref, o_ref) in enumerate(zip(x_refs, o_refs)):
            own = o_ref.at[_dev_index(x, y, c)]
            mine.append(pltpu.make_async_copy(x_ref, own, sems[2].at[a]))
            for k, to in enumerate(targets):
                sends.append(_remote(x_ref, own, sems, 4 * a + k, to))
                arrivals.append(functools.partial(_remote, x_ref, o_ref.at[_dev_index(*to)], sems, 4 * a + k, to))
        return mine, sends, arrivals


class _ForwardToSibling(_Exchange):
    per_array = 3
    in_place = True

    def _out_shapes(self):
        return [jax.ShapeDtypeStruct(x.shape, x.dtype) for x in self.xs]

    def _copies(self, x_refs, o_refs, sems):
        x, y, c = _coords()
        chips = [(1 - x, y), (x, 1 - y), (1 - x, 1 - y)]
        sends, arrivals = [], []
        for a, (x_ref, o_ref) in enumerate(zip(x_refs, o_refs)):
            for j, chip in enumerate(chips):
                held = _dev_index(*chip, c)
                sends.append(_remote(x_ref.at[held], o_ref.at[held], sems, 3 * a + j, (x, y, 1 - c)))
                theirs = _dev_index(*chip, 1 - c)
                arrivals.append(functools.partial(_remote, x_ref.at[theirs], o_ref.at[theirs], sems, 3 * a + j,
                                                  (x, y, 1 - c)))
        return [], sends, arrivals


class _AllToAll(_Exchange):
    per_array = 7

    def _out_shapes(self):
        return [jax.ShapeDtypeStruct(x.shape, x.dtype) for x in self.xs]

    def _copies(self, x_refs, o_refs, sems):
        x, y, c = _coords()
        me_i = _dev_index(x, y, c)
        mine, sends, arrivals = [], [], []
        for a, (x_ref, o_ref) in enumerate(zip(x_refs, o_refs)):
            mine.append(pltpu.make_async_copy(x_ref.at[me_i], o_ref.at[me_i], sems[2].at[a]))
            for r in range(1, 8):
                to = (1 - x if r & 4 else x, 1 - y if r & 2 else y, 1 - c if r & 1 else c)
                to_i = _dev_index(*to)
                sends.append(_remote(x_ref.at[to_i], o_ref.at[me_i], sems, 7 * a + r - 1, to))
                arrivals.append(functools.partial(_remote, x_ref.at[to_i], o_ref.at[to_i], sems, 7 * a + r - 1, to))
        return mine, sends, arrivals


def _exchange_call(ex, name):
    n = len(ex.xs)

    def body(*refs):
        x_refs, o_refs, sems = refs[:n], refs[n:2 * n], refs[2 * n:]
        ex.start(x_refs, o_refs, sems)
        ex.finish(x_refs, o_refs, sems)

    any_spec = pl.BlockSpec(memory_space=pl.ANY)
    outs = pl.pallas_call(
        body, name=name, out_shape=ex.out_shapes, in_specs=[any_spec] * n, out_specs=[any_spec] * n,
        scratch_shapes=ex.scratch, input_output_aliases={a: a for a in range(n)} if ex.in_place else {},
    )(*ex.xs)
    return list(outs)


def _rider_parts(rider):
    if rider is None:
        return [], [], [], [], []
    any_spec = pl.BlockSpec(memory_space=pl.ANY)
    n = len(rider.xs)
    return rider.xs, [any_spec] * n, rider.out_shapes, [any_spec] * n, rider.scratch


def _compute_call(body, *, name, grid, in_specs, out_specs, out_shape, operands, scratch_shapes=(), rider=None):
    in_specs, out_specs, out_shape, scratch_shapes = list(in_specs), list(out_specs), list(out_shape), list(scratch_shapes)
    r_in, r_in_specs, r_out, r_out_specs, r_scratch = _rider_parts(rider)
    n_in, n_out, n_scr, nr = len(operands), len(out_shape), len(scratch_shapes), len(r_in)

    def riding_body(*refs):
        ins, refs = refs[:n_in], refs[n_in:]
        r_x, refs = refs[:nr], refs[nr:]
        outs, refs = refs[:n_out], refs[n_out:]
        r_o, refs = refs[:nr], refs[nr:]
        scratch, r_sems = refs[:n_scr], refs[n_scr:]
        if rider is not None:
            first, last = _grid_ends(grid)
            pl.when(first)(lambda: rider.start(r_x, r_o, r_sems))
        body(*ins, *outs, *scratch)
        if rider is not None:
            pl.when(last)(lambda: rider.finish(r_x, r_o, r_sems))

    in_place = rider is not None and rider.in_place
    res = pl.pallas_call(
        riding_body, name=name, grid=grid, out_shape=out_shape + r_out,
        in_specs=in_specs + r_in_specs, out_specs=out_specs + r_out_specs,
        scratch_shapes=scratch_shapes + r_scratch,
        input_output_aliases={n_in + a: n_out + a for a in range(nr)} if in_place else {},
        compiler_params=_params(("arbitrary",) * len(grid)),
    )(*operands, *r_in)
    return list(res[:n_out]), list(res[n_out:])


def _grid_ends(grid):
    first = pl.program_id(0) == 0
    last = pl.program_id(0) == grid[0] - 1
    for ax in range(1, len(grid)):
        first = jnp.logical_and(first, pl.program_id(ax) == 0)
        last = jnp.logical_and(last, pl.program_id(ax) == grid[ax] - 1)
    return first, last


def _silu(x):
    return x * (1.0 / (1.0 + jnp.exp(-x)))


def _cond_rows(c_all, c_ctx):
    d = c_all.shape[-1]
    s = jnp.concatenate([c_all, jnp.zeros((8, d), F32)], axis=0)
    row = lax.broadcasted_iota(jnp.int32, (16, d), 0)
    s = jnp.where(row == 8, c_ctx, s)
    return jnp.where(row <= 8, _silu(s), 0.0)


def _mods_local(c_all, c_ctx, ada_w, ada_b_loc):
    nl, d, n = ada_w.shape

    def body(c_ref, cc_ref, w_ref, b_ref, o_ref):
        s = _cond_rows(c_ref[...], cc_ref[...])
        o_ref[...] = jnp.dot(s, w_ref[...], preferred_element_type=F32,
                             precision=lax.Precision.HIGHEST) + b_ref[...]

    return pl.pallas_call(
        body, name="mods_local", grid=(nl,),
        out_shape=jax.ShapeDtypeStruct((nl, 16, n), F32),
        in_specs=[pl.BlockSpec((8, d), lambda i: (0, 0)), pl.BlockSpec((1, d), lambda i: (0, 0)),
                  pl.BlockSpec((None, d, n), lambda i: (i, 0, 0)),
                  pl.BlockSpec((None, 1, n), lambda i: (i, 0, 0))],
        out_specs=pl.BlockSpec((None, 16, n), lambda i: (i, 0, 0)),
        compiler_params=_params(("arbitrary",)),
    )(c_all, c_ctx, ada_w, ada_b_loc)


def _ada_grads(c_all, c_ctx, ada_w, dm_lat, dm_ctx):
    nl, d, n = ada_w.shape

    def body(c_ref, cc_ref, w_ref, dml_ref, dmc_ref, gw_ref, ds_ref):
        i = pl.program_id(0)
        s = _cond_rows(c_ref[...], cc_ref[...])
        csum = dmc_ref[0:1, :]
        for k in range(1, N_DEV):
            csum = csum + dmc_ref[k:k + 1, :]
        row = lax.broadcasted_iota(jnp.int32, (8, n), 0)
        dm_c = jnp.where(row == 0, csum, 0.0)
        dm = jnp.concatenate([dml_ref[...], dm_c], axis=0)
        gw_ref[...] = lax.dot_general(s, dm, (((0,), (0,)), ((), ())), preferred_element_type=F32,
                                      precision=lax.Precision.HIGHEST)
        ds = lax.dot_general(dm_c, w_ref[...], (((1,), (1,)), ((), ())),
                             preferred_element_type=F32, precision=lax.Precision.HIGHEST)

        @pl.when(i == 0)
        def _():
            ds_ref[...] = jnp.zeros_like(ds_ref)
        ds_ref[...] += ds

    return pl.pallas_call(
        body, name="ada_grads", grid=(nl,),
        out_shape=[jax.ShapeDtypeStruct((nl, d, n), F32), jax.ShapeDtypeStruct((8, d), F32)],
        in_specs=[pl.BlockSpec((8, d), lambda i: (0, 0)), pl.BlockSpec((1, d), lambda i: (0, 0)),
                  pl.BlockSpec((None, d, n), lambda i: (i, 0, 0)),
                  pl.BlockSpec((None, 8, n), lambda i: (i, 0, 0)),
                  pl.BlockSpec((None, 8, n), lambda i: (i, 0, 0))],
        out_specs=[pl.BlockSpec((None, d, n), lambda i: (i, 0, 0)),
                   pl.BlockSpec((8, d), lambda i: (0, 0))],
        compiler_params=_params(("arbitrary",)),
    )(c_all, c_ctx, ada_w, dm_lat, dm_ctx)


def _cctx_grad_and_loss(ds_parts, c_ctx, loss_parts):
    d = c_ctx.shape[-1]

    def body(p_ref, c_ref, l_ref, o_ref, lo_ref):
        ds, loss = p_ref[0], l_ref[0]
        for k in range(1, N_DEV):
            ds = ds + p_ref[k]
            loss = loss + l_ref[k]
        x = c_ref[...]
        sg = 1.0 / (1.0 + jnp.exp(-x))
        o_ref[...] = ds[0:1, :] * (sg * (1.0 + x * (1.0 - sg)))
        lo_ref[...] = loss

    return pl.pallas_call(body, name="cctx_grad", out_shape=[jax.ShapeDtypeStruct((1, d), F32),
                                                             jax.ShapeDtypeStruct((8, 128), F32)])(ds_parts, c_ctx, loss_parts)


def _mlp_fwd(h1, mods, ng, w1, w2, nct, tm, row_off=0):
    d = h1.shape[1]
    r = h1.shape[0] - row_off * tm
    fc = w1.shape[2]
    f = N_DEV * fc

    def body(h_ref, mod_ref, ng_ref, w1_ref, w2_ref, h2_ref, p_ref, y_ref):
        h = h_ref[...]
        a, _, _, _ = _normmod(h, ng_ref[...], mod_ref[3:4, :], mod_ref[4:5, :])
        ab = a.astype(BF16)
        acc = jnp.zeros((tm, d), F32)
        for j in range(N_DEV):
            sl = slice(j * fc, (j + 1) * fc)
            p = jnp.maximum(_dot(ab, w1_ref[j]), 0.0)
            p_ref[:, sl] = p.astype(BF16)
            acc = acc + _dot((p * p).astype(BF16), w2_ref[j])
        y_ref[...] = acc.astype(BF16)
        h2_ref[...] = h + mod_ref[5:6, :] * acc

    return pl.pallas_call(
        body, name="mlp_fwd", grid=(r // tm,),
        out_shape=[jax.ShapeDtypeStruct((r, d), F32), jax.ShapeDtypeStruct((r, f), BF16),
                   jax.ShapeDtypeStruct((r, d), BF16)],
        in_specs=[pl.BlockSpec((tm, d), lambda i: (i + row_off, 0)),
                  pl.BlockSpec((None, 8, d), lambda i: (_sid(i, nct), 0, 0)),
                  _const_spec((1, d)), _const_spec(w1.shape), _const_spec(w2.shape)],
        out_specs=[pl.BlockSpec((tm, d), lambda i: (i, 0)), pl.BlockSpec((tm, f), lambda i: (i, 0)),
                   pl.BlockSpec((tm, d), lambda i: (i, 0))],
        compiler_params=_params(("arbitrary",)),
    )(h1, mods, ng, w1, w2)


def _mlp_bwd(dh2, h1, p, y, mods, ng, w1, w2, nct, tm, row_off=0, rider=None):
    r_rows, d = dh2.shape
    fc = w1.shape[2]
    f = N_DEV * fc

    def body(dh_ref, h_ref, p_ref, y_ref, mod_ref, ng_ref, w1_ref, w2_ref,
             dh1_ref, m_ref, du_ref, dacc_ref, st_ref):
        i = pl.program_id(0)
        dh = dh_ref[...]
        ngv, sc, gate = ng_ref[...], mod_ref[4:5, :], mod_ref[5:6, :]
        a, xhat, rstd, n = _normmod(h_ref[...], ngv, mod_ref[3:4, :], sc)
        m_ref[...] = a.astype(BF16)
        dgate = _colsum(dh * y_ref[...].astype(F32))
        dacc = (gate * dh).astype(BF16)
        dacc_ref[...] = dacc
        dm = jnp.zeros((tm, d), F32)
        for j in range(N_DEV):
            sl = slice(j * fc, (j + 1) * fc)
            pj = p_ref[:, sl].astype(F32)
            du = (_dot_nt(dacc, w2_ref[j]) * (2.0 * pj)).astype(BF16)
            du_ref[:, sl] = du
            dm = dm + _dot_nt(du, w1_ref[j])
        dhn, dsh, dsc, dng = _normmod_bwd(dm, xhat, rstd, n, ngv, sc)
        dh1_ref[...] = dh + dhn
        _acc_rows(st_ref, _first_of_stream(i, nct), [dsh, dsc, dgate, dng])

    outs, rode = _compute_call(
        body, name="mlp_bwd", grid=(r_rows // tm,), operands=(dh2, h1, p, y, mods, ng, w1, w2), rider=rider,
        out_shape=[jax.ShapeDtypeStruct((r_rows, d), F32), jax.ShapeDtypeStruct((r_rows, d), BF16),
                   jax.ShapeDtypeStruct((r_rows, f), BF16),
                   jax.ShapeDtypeStruct((r_rows, d), BF16), jax.ShapeDtypeStruct((_n_streams(nct), 8, d), F32)],
        in_specs=[pl.BlockSpec((tm, d), lambda i: (i, 0)),
                  pl.BlockSpec((tm, d), lambda i: (i + row_off, 0)),
                  pl.BlockSpec((tm, f), lambda i: (i, 0)), pl.BlockSpec((tm, d), lambda i: (i, 0)),
                  pl.BlockSpec((None, 8, d), lambda i: (_sid(i, nct), 0, 0)),
                  _const_spec((1, d)), _const_spec(w1.shape), _const_spec(w2.shape)],
        out_specs=[pl.BlockSpec((tm, d), lambda i: (i, 0)), pl.BlockSpec((tm, d), lambda i: (i, 0)),
                   pl.BlockSpec((tm, f), lambda i: (i, 0)),
                   pl.BlockSpec((tm, d), lambda i: (i, 0)),
                   pl.BlockSpec((None, 8, d), lambda i: (_stat_sid(i, nct), 0, 0))])
    return (*outs, rode)


def _pick(n, cands):
    for cand in cands:
        if n % cand == 0:
            return cand
    return n


def _tn_matmul(x, y, name, col_shards=False, square_x=False, rider=None):
    rows, k1 = x.shape
    k2 = y.shape[1]
    bt = _pick(rows, (1024, 768, 512, 384, 256, 128))
    bk1, bk2 = min(k1, 1024), min(k2, 1024)
    grid = (k1 // bk1, k2 // bk2, rows // bt)
    nt = rows // bt
    n = k2 // N_DEV
    if col_shards:
        assert bk2 % n == 0
        per = bk2 // n
        out_shape = jax.ShapeDtypeStruct((N_DEV, k1, n), BF16)
        out_spec = pl.BlockSpec((per, bk1, n), lambda i, j, t: (j, i, 0))
    else:
        out_shape = jax.ShapeDtypeStruct((k1, k2), BF16)
        out_spec = pl.BlockSpec((bk1, bk2), lambda i, j, t: (i, j))

    def body(x_ref, y_ref, o_ref, acc_ref):
        t = pl.program_id(2)

        @pl.when(t == 0)
        def _():
            acc_ref[...] = jnp.zeros_like(acc_ref)
        xv = x_ref[...]
        acc_ref[...] += _dot_tn(xv * xv if square_x else xv, y_ref[...])

        @pl.when(t == nt - 1)
        def _():
            if col_shards:
                for s in range(per):
                    o_ref[s] = acc_ref[:, s * n:(s + 1) * n].astype(BF16)
            else:
                o_ref[...] = acc_ref[...].astype(BF16)

    (out,), rode = _compute_call(
        body, name=name, grid=grid, operands=(x, y), rider=rider, out_shape=[out_shape],
        in_specs=[pl.BlockSpec((bt, bk1), lambda i, j, t: (t, i)), pl.BlockSpec((bt, bk2), lambda i, j, t: (t, j))],
        out_specs=[out_spec], scratch_shapes=[pltpu.VMEM((bk1, bk2), F32)])
    return out if rider is None else (out, rode)


def _pool_bands(tm):
    k = tm + 128
    t = np.arange(tm)[:, None]
    e = np.arange(k)[None, :]
    fwd, bwd = [], []
    for w in POOL_WINDOWS:
        lo = POOL_HALO + t - w // 2
        fwd.append(((e >= lo) & (e <= lo + w - 1)).astype(np.float32))
        lo_t = POOL_HALO + t - w // 2 + 1
        bwd.append(((e >= lo_t) & (e <= lo_t + w - 1)).astype(np.float32))
    return jnp.asarray(np.stack(fwd), BF16), jnp.asarray(np.stack(bwd), BF16)


def _pool_geometry(i, nct, n_tiles, tm, c_len, l_len):
    if nct == 0:
        pos0 = i * tm
        ls = l_len
        has_prev = i > 0
        has_next = i < n_tiles - 1
    else:
        in_ctx = i < nct
        pos0 = jnp.where(in_ctx, i, i - nct) * tm
        ls = jnp.where(in_ctx, c_len, l_len)
        has_prev = jnp.logical_and(i != 0, i != nct)
        has_next = jnp.logical_and(i != nct - 1, i != n_tiles - 1)
    return pos0, ls, has_prev, has_next


def _window_inv_counts(pos, ls):
    out = []
    for w in POOL_WINDOWS:
        lo = jnp.maximum(pos - w // 2, 0)
        hi = jnp.minimum(pos + w - w // 2, ls)
        cnt = jnp.maximum(hi - lo, 1).astype(F32)
        out.append(1.0 / cnt)
    return out


def _split_bf16(x):
    hi = x.astype(BF16)
    return hi, (x - hi.astype(F32)).astype(BF16)


def _extend(prev, tile, nxt, has_prev, has_next):
    w = tile.shape[1]
    prev = jnp.where(has_prev, prev, 0.0)
    nxt = jnp.where(has_next, nxt, 0.0)
    return jnp.concatenate([prev, tile, nxt, jnp.zeros((128 - 2 * POOL_HALO, w), F32)], axis=0)


def _pool_specs(tm, d, n_rows):
    last8 = n_rows // POOL_HALO - 1
    per = tm // POOL_HALO
    return [pl.BlockSpec((tm, d), lambda i: (i, 0)),
            pl.BlockSpec((POOL_HALO, d), lambda i: (jnp.maximum(i * per - 1, 0), 0)),
            pl.BlockSpec((POOL_HALO, d), lambda i: (jnp.minimum((i + 1) * per, last8), 0))]


def _pool_fwd(h, mods, ng, w, scale, bands, nct, tm, c_len, l_len):
    r, d = h.shape
    gw = d // POOL_GROUPS
    n_tiles = r // tm
    kx = tm + 128

    def body(h_ref, hp_ref, hn_ref, mod_ref, ng_ref, w_ref, sc_ref, band_ref, y_ref, h1_ref):
        i = pl.program_id(0)
        pos0, ls, has_prev, has_next = _pool_geometry(i, nct, n_tiles, tm, c_len, l_len)
        ngv, sh, sc = ng_ref[...], mod_ref[0:1, :], mod_ref[1:2, :]
        h = h_ref[...]
        a = _normmod(h, ngv, sh, sc)[0]
        a_ext = _extend(_normmod(hp_ref[...], ngv, sh, sc)[0], a, _normmod(hn_ref[...], ngv, sh, sc)[0],
                        has_prev, has_next)
        pos = pos0 + lax.broadcasted_iota(jnp.int32, (tm, 1), 0)
        inv = _window_inv_counts(pos, ls)
        ys = []
        for g in range(POOL_GROUPS):
            cols = slice(g * gw, (g + 1) * gw)
            hi, lo = _split_bf16(a_ext[:, cols])
            s = _dot(band_ref[g], hi) + _dot(band_ref[g], lo)
            pg = s * inv[g] - a[:, cols]
            ys.append(_dot(pg.astype(BF16), w_ref[g]))
        y = jnp.concatenate(ys, axis=1) * sc_ref[...]
        y_ref[...] = y.astype(BF16)
        h1_ref[...] = h + mod_ref[2:3, :] * y

    return pl.pallas_call(
        body, name="pool_fwd", grid=(n_tiles,),
        out_shape=[jax.ShapeDtypeStruct((r, d), BF16), jax.ShapeDtypeStruct((r, d), F32)],
        in_specs=_pool_specs(tm, d, r) + [
            pl.BlockSpec((None, 8, d), lambda i: (_sid(i, nct), 0, 0)),
            _const_spec((1, d)), _const_spec(w.shape), _const_spec((1, d)), _const_spec((4, tm, kx))],
        out_specs=[pl.BlockSpec((tm, d), lambda i: (i, 0)), pl.BlockSpec((tm, d), lambda i: (i, 0))],
        compiler_params=_params(("arbitrary",)),
    )(h, h, h, mods, ng, w, scale, bands[0])


def _pool_bwd(dh1, h, y, mods, ng, w, scale, bands, nct, tm, c_len, l_len, latent_out, rider=None):
    r, d = h.shape
    gw = d // POOL_GROUPS
    n_tiles = r // tm
    kx = tm + 128
    out_rows = l_len if latent_out else r
    out_off = nct if latent_out else 0

    def body(dh_ref, dhp_ref, dhn_ref, h_ref, hp_ref, hn_ref, y_ref, mod_ref, ng_ref, w_ref, sc_ref,
             bf_ref, bb_ref, dho_ref, dw_ref, st_ref):
        i = pl.program_id(0)
        pos0, ls, has_prev, has_next = _pool_geometry(i, nct, n_tiles, tm, c_len, l_len)
        ngv, sh, sc, gate = ng_ref[...], mod_ref[0:1, :], mod_ref[1:2, :], mod_ref[2:3, :]
        scale_v = sc_ref[...]
        h = h_ref[...]
        a, xhat, rstd, n = _normmod(h, ngv, sh, sc)
        a_ext = _extend(_normmod(hp_ref[...], ngv, sh, sc)[0], a, _normmod(hn_ref[...], ngv, sh, sc)[0],
                        has_prev, has_next)
        dh = dh_ref[...]
        dgate = _colsum(dh * y_ref[...].astype(F32))
        dy = gate * dh
        dy_ext = _extend(gate * dhp_ref[...], dy, gate * dhn_ref[...], has_prev, has_next)
        dyp_ext = (dy_ext * scale_v).astype(BF16)
        dyp = (dy * scale_v).astype(BF16)
        pos = pos0 + lax.broadcasted_iota(jnp.int32, (tm, 1), 0)
        inv = _window_inv_counts(pos, ls)
        pos_e = pos0 - POOL_HALO + lax.broadcasted_iota(jnp.int32, (kx, 1), 0)
        inv_e = _window_inv_counts(pos_e, ls)

        @pl.when(i == 0)
        def _():
            dw_ref[...] = jnp.zeros_like(dw_ref)
        das, dscale = [], []
        for g in range(POOL_GROUPS):
            cols = slice(g * gw, (g + 1) * gw)
            hi, lo = _split_bf16(a_ext[:, cols])
            pg = ((_dot(bf_ref[g], hi) + _dot(bf_ref[g], lo)) * inv[g] - a[:, cols]).astype(BF16)
            dscale.append(_colsum(dy[:, cols] * _dot(pg, w_ref[g])))
            dyp_g = dyp_ext[:, cols]
            dw_ref[g] += _dot_tn(pg, dyp[:, cols])
            dp_ext = _dot_nt(dyp_g, w_ref[g])
            hi, lo = _split_bf16(dp_ext * inv_e[g])
            das.append(_dot(bb_ref[g], hi) + _dot(bb_ref[g], lo) - dp_ext[POOL_HALO:POOL_HALO + tm, :])
        da = jnp.concatenate(das, axis=1)
        dhn, dsh, dsc, dng = _normmod_bwd(da, xhat, rstd, n, ngv, sc)
        dho_ref[...] = dh + dhn
        _acc_rows(st_ref, _first_of_stream(i, nct), [dsh, dsc, dgate, dng, jnp.concatenate(dscale, axis=1)])

    outs, rode = _compute_call(
        body, name="pool_bwd", grid=(n_tiles,), rider=rider,
        operands=(dh1, dh1, dh1, h, h, h, y, mods, ng, w, scale, bands[0], bands[1]),
        out_shape=[jax.ShapeDtypeStruct((out_rows, d), F32),
                   jax.ShapeDtypeStruct((POOL_GROUPS, gw, gw), F32),
                   jax.ShapeDtypeStruct((_n_streams(nct), 8, d), F32)],
        in_specs=_pool_specs(tm, d, r) + _pool_specs(tm, d, r) + [
            pl.BlockSpec((tm, d), lambda i: (i, 0)),
            pl.BlockSpec((None, 8, d), lambda i: (_sid(i, nct), 0, 0)),
            _const_spec((1, d)), _const_spec(w.shape), _const_spec((1, d)),
            _const_spec((4, tm, kx)), _const_spec((4, tm, kx))],
        out_specs=[pl.BlockSpec((tm, d), lambda i: (jnp.maximum(i - out_off, 0), 0)),
                   pl.BlockSpec((POOL_GROUPS, gw, gw), lambda i: (0, 0, 0)),
                   pl.BlockSpec((None, 8, d), lambda i: (_stat_sid(i, nct), 0, 0))])
    return (*outs, rode)


def _rope_tables(c_len, l_len):
    half = HEAD_DIM // 2
    t = np.arange(l_len)
    row = (t // GRID_W).astype(np.float32)
    col = (t % GRID_W).astype(np.float32)
    inv = (np.float32(ROPE_BASE) ** (-np.arange(0, half, 2, dtype=np.float32) / np.float32(half))).astype(np.float32)
    ang_r = row[:, None] * inv[None, :]
    ang_c = col[:, None] * inv[None, :]
    cos = np.concatenate([np.cos(ang_r), np.cos(ang_r), np.cos(ang_c), np.cos(ang_c)], axis=1)
    sin = np.concatenate([-np.sin(ang_r), np.sin(ang_r), -np.sin(ang_c), np.sin(ang_c)], axis=1)
    cos = np.concatenate([np.ones((c_len, HEAD_DIM), np.float32), cos.astype(np.float32)], axis=0)
    sin = np.concatenate([np.zeros((c_len, HEAD_DIM), np.float32), sin.astype(np.float32)], axis=0)
    return jnp.asarray(cos, F32), jnp.asarray(sin, F32)


def _swap_pairs(x):
    lane = lax.broadcasted_iota(jnp.int32, x.shape, 1)
    return jnp.where((lane % 64) < 32, pltpu.roll(x, 96, 1), pltpu.roll(x, 32, 1))


def _head_norm(x, g):
    rstd = lax.rsqrt(jnp.mean(x * x, axis=-1, keepdims=True) + EPS)
    xhat = x * rstd
    return xhat * g, xhat, rstd


def _qkv_fwd(h, mods, ng, w, qg, kg, cos, sin, n_heads, n_kv, nct, tm):
    t_rows, d = h.shape
    qw, kw = n_heads * HEAD_DIM, n_kv * HEAD_DIM

    def body(h_ref, mod_ref, ng_ref, w_ref, qg_ref, kg_ref, cos_ref, sin_ref, q_ref, k_ref, v_ref, qt_ref):
        a = _normmod(h_ref[...], ng_ref[...], mod_ref[0:1, :], mod_ref[1:2, :])[0]
        qkv = _dot(a.astype(BF16), w_ref[...])
        cosv, sinv = cos_ref[...], sin_ref[...]
        ones = jnp.ones((tm, HEAD_DIM), BF16)
        for hd in range(n_heads + n_kv):
            cols = slice(hd * HEAD_DIM, (hd + 1) * HEAD_DIM)
            xn = _head_norm(qkv[:, cols], qg_ref[...] if hd < n_heads else kg_ref[...])[0]
            xr = xn * cosv + _swap_pairs(xn) * sinv
            if hd < n_heads:
                qs = xr * Q_SCALE
                q_ref[:, cols] = qs.astype(BF16)
                qt_ref[cols, :] = qs.T.astype(BF16)
            else:
                k_ref[:, (hd - n_heads) * HEAD_DIM:(hd - n_heads + 1) * HEAD_DIM] = xr.astype(BF16)
        for g in range(n_kv):
            v_ref[:, (2 * g) * HEAD_DIM:(2 * g + 1) * HEAD_DIM] = (
                qkv[:, qw + kw + g * HEAD_DIM:qw + kw + (g + 1) * HEAD_DIM].astype(BF16))
            v_ref[:, (2 * g + 1) * HEAD_DIM:(2 * g + 2) * HEAD_DIM] = ones

    return pl.pallas_call(
        body, name="qkv_fwd", grid=(t_rows // tm,),
        out_shape=[jax.ShapeDtypeStruct((t_rows - nct * tm, qw), BF16), jax.ShapeDtypeStruct((t_rows, kw), BF16),
                   jax.ShapeDtypeStruct((t_rows, 2 * kw), BF16), jax.ShapeDtypeStruct((qw, t_rows - nct * tm), BF16)],
        in_specs=[pl.BlockSpec((tm, d), lambda i: (i, 0)),
                  pl.BlockSpec((None, 8, d), lambda i: (_sid(i, nct), 0, 0)),
                  _const_spec((1, d)), _const_spec(w.shape), _const_spec((1, HEAD_DIM)),
                  _const_spec((1, HEAD_DIM)),
                  pl.BlockSpec((tm, HEAD_DIM), lambda i: (i, 0)), pl.BlockSpec((tm, HEAD_DIM), lambda i: (i, 0))],
        out_specs=[pl.BlockSpec((tm, qw), lambda i: (jnp.maximum(i - nct, 0), 0)),
                   pl.BlockSpec((tm, kw), lambda i: (i, 0)), pl.BlockSpec((tm, 2 * kw), lambda i: (i, 0)),
                   pl.BlockSpec((qw, tm), lambda i: (0, jnp.maximum(i - nct, 0)))],
        compiler_params=_params(("arbitrary",)),
    )(h, mods, ng, w, qg, kg, cos, sin)


def _flash_tk(t_rows, tm):
    best = tm
    k = tm
    while k <= FLASH_TK_CAP:
        if t_rows % k == 0:
            best = k
        k += tm
    return best


def _flash_fwd(q, k, v1, n_kv, tq, tm, rider=None):
    t_rows = k.shape[0]
    l_rows = q.shape[0]
    tk = _flash_tk(t_rows, tm)
    nk = t_rows // tk
    gq = 2 * HEAD_DIM

    def body(q_ref, k_ref, v_ref, o_ref, lse_ref, m_s, acc_s, s_s):
        ki = pl.program_id(2)

        @pl.when(ki == 0)
        def _():
            m_s[...] = jnp.full_like(m_s, -jnp.inf)
            acc_s[...] = jnp.zeros_like(acc_s)
        kk, vv = k_ref[...], v_ref[...]
        for hh in range(2):
            s_s[hh] = _dot_nt(q_ref[:, hh * HEAD_DIM:(hh + 1) * HEAD_DIM], kk)
        for hh in range(2):
            s = s_s[hh]
            m_prev = m_s[hh]
            m_new = jnp.maximum(m_prev, jnp.max(s, axis=-1, keepdims=True))
            alpha = jnp.exp2(m_prev - m_new)
            p = jnp.exp2(s - jnp.tile(m_new, (1, tk // HEAD_DIM)))
            acc_s[hh] = jnp.tile(alpha, (1, 2)) * acc_s[hh] + _dot(p.astype(BF16), vv)
            m_s[hh] = m_new

        @pl.when(ki == nk - 1)
        def _():
            for hh in range(2):
                acc = acc_s[hh]
                l = acc[:, HEAD_DIM:]
                o_ref[:, hh * HEAD_DIM:(hh + 1) * HEAD_DIM] = (acc[:, :HEAD_DIM] / l).astype(BF16)
                lse_ref[:, hh:hh + 1] = (m_s[hh] + jnp.log2(l))[:, 0:1]

    (o, lse), rode = _compute_call(
        body, name="flash_fwd", grid=(n_kv, l_rows // tq, nk), operands=(q, k, v1), rider=rider,
        out_shape=[jax.ShapeDtypeStruct((l_rows, n_kv * gq), BF16),
                   jax.ShapeDtypeStruct((n_kv, l_rows, 2), F32)],
        in_specs=[pl.BlockSpec((tq, gq), lambda g, i, j: (i, g)),
                  pl.BlockSpec((tk, HEAD_DIM), lambda g, i, j: (j, g)),
                  pl.BlockSpec((tk, gq), lambda g, i, j: (j, g))],
        out_specs=[pl.BlockSpec((tq, gq), lambda g, i, j: (i, g)),
                   pl.BlockSpec((None, tq, 2), lambda g, i, j: (g, i, 0))],
        scratch_shapes=[pltpu.VMEM((2, tq, HEAD_DIM), F32), pltpu.VMEM((2, tq, gq), F32),
                        pltpu.VMEM((2, tq, tk), F32)])
    return o, lse, rode


def _flash_bwd(q, qt, k, v1, do, dot, lse, delta, n_kv, tq, tm, rider=None):
    t_rows = k.shape[0]
    l_rows = q.shape[0]
    tk = _flash_tk(t_rows, tm)
    nq = l_rows // tq
    gq = 2 * HEAD_DIM

    def body(q_ref, qt_ref, k_ref, v_ref, do_ref, dot_ref, lse_ref, dl_ref, dq_ref, dkt_ref, dvt_ref):
        ki, qi = pl.program_id(1), pl.program_id(2)
        rows = pl.ds(pl.multiple_of(qi * tq, tq), tq)

        @pl.when(qi == 0)
        def _():
            dkt_ref[...] = jnp.zeros_like(dkt_ref)
            dvt_ref[...] = jnp.zeros_like(dvt_ref)

        @pl.when(ki == 0)
        def _():
            dq_ref[rows, :] = jnp.zeros((tq, gq), F32)
        kk, vv = k_ref[...], v_ref[:, :HEAD_DIM]
        dkt_parts, dvt_parts = [], []
        for hh in range(2):
            cols = slice(hh * HEAD_DIM, (hh + 1) * HEAD_DIM)
            p = jnp.exp2(_dot_nt(q_ref[:, cols], kk) - lse_ref[:, hh:hh + 1])
            ds = (p * (_dot_nt(do_ref[:, cols], vv) - dl_ref[:, hh:hh + 1])).astype(BF16)
            dvt_parts.append(_dot(dot_ref[cols, :], p.astype(BF16)))
            dkt_parts.append(_dot(qt_ref[cols, :], ds))
            dq_ref[rows, cols] += _dot(ds, kk)
        dvt_ref[...] += dvt_parts[0] + dvt_parts[1]
        dkt_ref[...] += dkt_parts[0] + dkt_parts[1]

    (dq, dkt, dvt), rode = _compute_call(
        body, name="flash_bwd", grid=(n_kv, t_rows // tk, nq), operands=(q, qt, k, v1, do, dot, lse, delta),
        rider=rider,
        out_shape=[jax.ShapeDtypeStruct((l_rows, n_kv * gq), F32),
                   jax.ShapeDtypeStruct((n_kv * HEAD_DIM, t_rows), F32),
                   jax.ShapeDtypeStruct((n_kv * HEAD_DIM, t_rows), F32)],
        in_specs=[pl.BlockSpec((tq, gq), lambda g, j, i: (i, g)),
                  pl.BlockSpec((gq, tq), lambda g, j, i: (g, i)),
                  pl.BlockSpec((tk, HEAD_DIM), lambda g, j, i: (j, g)),
                  pl.BlockSpec((tk, gq), lambda g, j, i: (j, g)),
                  pl.BlockSpec((tq, gq), lambda g, j, i: (i, g)),
                  pl.BlockSpec((gq, tq), lambda g, j, i: (g, i)),
                  pl.BlockSpec((None, tq, 2), lambda g, j, i: (g, i, 0)),
                  pl.BlockSpec((None, tq, 2), lambda g, j, i: (g, i, 0))],
        out_specs=[pl.BlockSpec((l_rows, gq), lambda g, j, i: (0, g)),
                   pl.BlockSpec((HEAD_DIM, tk), lambda g, j, i: (g, j)),
                   pl.BlockSpec((HEAD_DIM, tk), lambda g, j, i: (g, j))])
    return dq, dkt, dvt, rode


def _wo_fwd(h, o, wo, mods, nct, tm, rider=None):
    l_rows, z = o.shape
    d = h.shape[1]

    def body(h_ref, o_ref, w_ref, mod_ref, y_ref, h1_ref):
        y = _dot(o_ref[...], w_ref[...])
        y_ref[...] = y.astype(BF16)
        h1_ref[...] = h_ref[...] + mod_ref[2:3, :] * y

    (y, h1), rode = _compute_call(
        body, name="wo_fwd", grid=(l_rows // tm,), operands=(h, o, wo, mods), rider=rider,
        out_shape=[jax.ShapeDtypeStruct((l_rows, d), BF16), jax.ShapeDtypeStruct((l_rows, d), F32)],
        in_specs=[pl.BlockSpec((tm, d), lambda i: (i + nct, 0)), pl.BlockSpec((tm, z), lambda i: (i, 0)),
                  _const_spec(wo.shape), pl.BlockSpec((None, 8, d), lambda i: (1, 0, 0))],
        out_specs=[pl.BlockSpec((tm, d), lambda i: (i, 0)), pl.BlockSpec((tm, d), lambda i: (i, 0))])
    return y, h1, rode


def _wo_bwd(dh1, y, o, wo, mods, n_kv, tm):
    l_rows, z = o.shape
    d = dh1.shape[1]

    def body(dh_ref, y_ref, o_ref, w_ref, mod_ref, dy_ref, do_ref, dot_ref, dl_ref, st_ref):
        i = pl.program_id(0)
        dh = dh_ref[...]
        dgate = _colsum(dh * y_ref[...].astype(F32))
        dy = (mod_ref[2:3, :] * dh).astype(BF16)
        dy_ref[...] = dy
        do = _dot_nt(dy, w_ref[...])
        do_ref[...] = do.astype(BF16)
        dot_ref[...] = do.T.astype(BF16)
        prod = do * o_ref[...].astype(F32)
        for g in range(n_kv):
            d0 = jnp.sum(prod[:, (2 * g) * HEAD_DIM:(2 * g + 1) * HEAD_DIM], axis=-1, keepdims=True)
            d1 = jnp.sum(prod[:, (2 * g + 1) * HEAD_DIM:(2 * g + 2) * HEAD_DIM], axis=-1, keepdims=True)
            dl_ref[g] = jnp.concatenate([d0, d1], axis=1)
        zero = jnp.zeros((1, d), F32)
        _acc_rows(st_ref, i == 0, [zero, zero, dgate])

    return pl.pallas_call(
        body, name="wo_bwd", grid=(l_rows // tm,),
        out_shape=[jax.ShapeDtypeStruct((l_rows, d), BF16), jax.ShapeDtypeStruct((l_rows, z), BF16),
                   jax.ShapeDtypeStruct((z, l_rows), BF16),
                   jax.ShapeDtypeStruct((n_kv, l_rows, 2), F32), jax.ShapeDtypeStruct((8, d), F32)],
        in_specs=[pl.BlockSpec((tm, d), lambda i: (i, 0)), pl.BlockSpec((tm, d), lambda i: (i, 0)),
                  pl.BlockSpec((tm, z), lambda i: (i, 0)), _const_spec(wo.shape),
                  pl.BlockSpec((None, 8, d), lambda i: (1, 0, 0))],
        out_specs=[pl.BlockSpec((tm, d), lambda i: (i, 0)), pl.BlockSpec((tm, z), lambda i: (i, 0)),
                   pl.BlockSpec((z, tm), lambda i: (0, i)),
                   pl.BlockSpec((n_kv, tm, 2), lambda i: (0, i, 0)), pl.BlockSpec((8, d), lambda i: (0, 0))],
        compiler_params=_params(("arbitrary",)),
    )(dh1, y, o, wo, mods)


def _qkv_bwd(h, dh_lat, dq, dkt, dvt, mods, ng, w, qg, kg, cos, sin, n_heads, n_kv, nct, tm):
    t_rows, d = h.shape
    qw, kw = n_heads * HEAD_DIM, n_kv * HEAD_DIM
    scale = HEAD_DIM ** -0.5

    def body(h_ref, dhl_ref, dq_ref, dkt_ref, dvt_ref, mod_ref, ng_ref, w_ref, qg_ref, kg_ref, cos_ref,
             sin_ref, dh_ref, a_ref, dqkv_ref, st_ref, dg_ref):
        i = pl.program_id(0)
        lat = (i >= nct).astype(F32)
        dk_t = dkt_ref[...].T * (1.0 / LOG2E)
        ngv, sc = ng_ref[...], mod_ref[1:2, :]
        a, xhat, rstd, n = _normmod(h_ref[...], ngv, mod_ref[0:1, :], sc)
        ab = a.astype(BF16)
        a_ref[...] = ab
        qkv = _dot(ab, w_ref[...])
        cosv, sinv = cos_ref[...], sin_ref[...]
        dqg = jnp.zeros((1, HEAD_DIM), F32)
        dkg = jnp.zeros((1, HEAD_DIM), F32)
        for hd in range(n_heads + n_kv):
            cols = slice(hd * HEAD_DIM, (hd + 1) * HEAD_DIM)
            is_q = hd < n_heads
            g = qg_ref[...] if is_q else kg_ref[...]
            _, hx, hr = _head_norm(qkv[:, cols], g)
            if is_q:
                dxr = dq_ref[:, cols] * (scale * lat)
            else:
                dxr = dk_t[:, (hd - n_heads) * HEAD_DIM:(hd - n_heads + 1) * HEAD_DIM]
            dxn = dxr * cosv + _swap_pairs(dxr * sinv)
            if is_q:
                dqg = dqg + _colsum(dxn * hx)
            else:
                dkg = dkg + _colsum(dxn * hx)
            dxh = dxn * g
            dx = hr * (dxh - hx * jnp.mean(dxh * hx, axis=-1, keepdims=True))
            dqkv_ref[:, cols] = dx.astype(BF16)
        dqkv_ref[:, qw + kw:] = dvt_ref[...].T.astype(BF16)
        da = _dot_nt(dqkv_ref[...], w_ref[...])
        dhn, dsh, dsc, dng = _normmod_bwd(da, xhat, rstd, n, ngv, sc)
        dh_ref[...] = dhl_ref[...] * lat + dhn
        _acc_rows(st_ref, _first_of_stream(i, nct), [dsh, dsc, jnp.zeros((1, d), F32), dng])
        _acc_rows(dg_ref, i == 0, [dqg, dkg])

    lat_map = lambda i: (jnp.maximum(i - nct, 0), 0)
    return pl.pallas_call(
        body, name="qkv_bwd", grid=(t_rows // tm,),
        out_shape=[jax.ShapeDtypeStruct((t_rows, d), F32), jax.ShapeDtypeStruct((t_rows, d), BF16),
                   jax.ShapeDtypeStruct((t_rows, qw + 2 * kw), BF16), jax.ShapeDtypeStruct((2, 8, d), F32),
                   jax.ShapeDtypeStruct((8, HEAD_DIM), F32)],
        in_specs=[pl.BlockSpec((tm, d), lambda i: (i, 0)), pl.BlockSpec((tm, d), lat_map),
                  pl.BlockSpec((tm, qw), lat_map), pl.BlockSpec((kw, tm), lambda i: (0, i)),
                  pl.BlockSpec((kw, tm), lambda i: (0, i)),
                  pl.BlockSpec((None, 8, d), lambda i: (_sid(i, nct), 0, 0)),
                  _const_spec((1, d)), _const_spec(w.shape), _const_spec((1, HEAD_DIM)),
                  _const_spec((1, HEAD_DIM)),
                  pl.BlockSpec((tm, HEAD_DIM), lambda i: (i, 0)), pl.BlockSpec((tm, HEAD_DIM), lambda i: (i, 0))],
        out_specs=[pl.BlockSpec((tm, d), lambda i: (i, 0)), pl.BlockSpec((tm, d), lambda i: (i, 0)),
                   pl.BlockSpec((tm, qw + 2 * kw), lambda i: (i, 0)),
                   pl.BlockSpec((None, 8, d), lambda i: (_sid(i, nct), 0, 0)),
                   pl.BlockSpec((8, HEAD_DIM), lambda i: (0, 0))],
        compiler_params=_params(("arbitrary",)),
    )(h, dh_lat, dq, dkt, dvt, mods, ng, w, qg, kg, cos, sin)


def _gmlp_core(a_bf, win_ref, lng, lnb, ws_ref, bst_ref, tm, half, with_grad=False):
    blocks = [_dot(a_bf, win_ref[j]) for j in range(N_DEV)]
    zu = jnp.concatenate(blocks[:N_DEV // 2], axis=1)
    zv = jnp.concatenate(blocks[N_DEV // 2:], axis=1)
    if with_grad:
        (u, zu), (v, zv) = _gelu_and_grad(zu), _gelu_and_grad(zv)
    else:
        u, v, zu, zv = _gelu(zu), _gelu(zv), None, None
    mu = jnp.mean(v, axis=-1, keepdims=True)
    vc = v - mu
    rstd_v = lax.rsqrt(jnp.mean(vc * vc, axis=-1, keepdims=True) + EPS)
    vhat = vc * rstd_v
    vln = (vhat * lng + lnb).astype(BF16)
    gw = half // GMLP_GROUPS
    rows = []
    for ch in range(tm // CHUNK):
        rs = slice(ch * CHUNK, (ch + 1) * CHUNK)
        cols = []
        for g in range(GMLP_GROUPS):
            cs = slice(g * gw, (g + 1) * gw)
            cols.append(_dot(ws_ref[g], vln[rs, cs]) + bst_ref[:, g:g + 1])
        rows.append(jnp.concatenate(cols, axis=1))
    sv = rows[0] if len(rows) == 1 else jnp.concatenate(rows, axis=0)
    return zu, zv, u, vhat, rstd_v, vln, sv


def _gmlp_fwd(h, mods, ng, win, lng, lnb, ws, bst, wout, tm):
    l_rows, d = h.shape
    half = wout.shape[0]

    def body(h_ref, mod_ref, ng_ref, win_ref, lng_ref, lnb_ref, ws_ref, bst_ref, wout_ref, y_ref, h1_ref):
        hv = h_ref[...]
        a = _normmod(hv, ng_ref[...], mod_ref[0:1, :], mod_ref[1:2, :])[0]
        _, _, u, _, _, _, sv = _gmlp_core(a.astype(BF16), win_ref, lng_ref[...], lnb_ref[...], ws_ref,
                                          bst_ref, tm, half)
        y = _dot((u * sv).astype(BF16), wout_ref[...])
        y_ref[...] = y.astype(BF16)
        h1_ref[...] = hv + mod_ref[2:3, :] * y

    return pl.pallas_call(
        body, name="gmlp_fwd", grid=(l_rows // tm,),
        out_shape=[jax.ShapeDtypeStruct((l_rows, d), BF16), jax.ShapeDtypeStruct((l_rows, d), F32)],
        in_specs=[pl.BlockSpec((tm, d), lambda i: (i, 0)), pl.BlockSpec((None, 8, d), lambda i: (1, 0, 0)),
                  _const_spec((1, d)), _const_spec(win.shape), _const_spec((1, half)), _const_spec((1, half)),
                  _const_spec(ws.shape), _const_spec(bst.shape), _const_spec(wout.shape)],
        out_specs=[pl.BlockSpec((tm, d), lambda i: (i, 0)), pl.BlockSpec((tm, d), lambda i: (i, 0))],
        compiler_params=_params(("arbitrary",)),
    )(h, mods, ng, win, lng, lnb, ws, bst, wout)


def _gmlp_bwd(dh1, h, y, mods, ng, win, lng, lnb, ws, wst, bst, wout, tm):
    l_rows, d = h.shape
    half = wout.shape[0]
    gw = half // GMLP_GROUPS

    def body(dh_ref, h_ref, y_ref, mod_ref, ng_ref, win_ref, lng_ref, lnb_ref, ws_ref, wst_ref, bst_ref,
             wout_ref, dho_ref, a_ref, dz_ref, gt_ref, dy_ref, st_ref, ln_ref, dws_ref, dbs_ref):
        i = pl.program_id(0)
        ngv, sc = ng_ref[...], mod_ref[1:2, :]
        lngv = lng_ref[...]
        a, xhat, rstd, n = _normmod(h_ref[...], ngv, mod_ref[0:1, :], sc)
        ab = a.astype(BF16)
        a_ref[...] = ab
        gu, gv, u, vhat, rstd_v, vln, sv = _gmlp_core(ab, win_ref, lngv, lnb_ref[...], ws_ref, bst_ref,
                                                      tm, half, with_grad=True)
        gt_ref[...] = (u * sv).astype(BF16)
        dh = dh_ref[...]
        dgate = _colsum(dh * y_ref[...].astype(F32))
        dy = (mod_ref[2:3, :] * dh).astype(BF16)
        dy_ref[...] = dy
        dgated = _dot_nt(dy, wout_ref[...])
        du = dgated * sv
        dsv = (dgated * u).astype(BF16)

        @pl.when(i == 0)
        def _():
            dws_ref[...] = jnp.zeros_like(dws_ref)
            dbs_ref[...] = jnp.zeros_like(dbs_ref)
        lane = lax.broadcasted_iota(jnp.int32, (CHUNK, 128), 1)
        dbs = jnp.zeros((CHUNK, 128), F32)
        rows = []
        for ch in range(tm // CHUNK):
            rs = slice(ch * CHUNK, (ch + 1) * CHUNK)
            cols = []
            for g in range(GMLP_GROUPS):
                cs = slice(g * gw, (g + 1) * gw)
                dsv_cg = dsv[rs, cs]
                dws_ref[g] += _dot_nt(dsv_cg, vln[rs, cs])
                cols.append(_dot(wst_ref[g], dsv_cg))
                dbs = dbs + jnp.where(lane == g, jnp.sum(dsv_cg.astype(F32), axis=-1, keepdims=True), 0.0)
            rows.append(jnp.concatenate(cols, axis=1))
        dbs_ref[...] += dbs
        dvln = rows[0] if len(rows) == 1 else jnp.concatenate(rows, axis=0)
        dlng = _colsum(dvln * vhat)
        dlnb = _colsum(dvln)
        dvh = dvln * lngv
        dv = rstd_v * (dvh - jnp.mean(dvh, axis=-1, keepdims=True)
                       - vhat * jnp.mean(dvh * vhat, axis=-1, keepdims=True))
        dz_ref[:, :half] = (du * gu).astype(BF16)
        dz_ref[:, half:] = (dv * gv).astype(BF16)
        nb = 2 * half // N_DEV
        da = _dot_nt(dz_ref[:, 0:nb], win_ref[0])
        for j in range(1, N_DEV):
            da = da + _dot_nt(dz_ref[:, j * nb:(j + 1) * nb], win_ref[j])
        dhn, dsh, dsc, dng = _normmod_bwd(da, xhat, rstd, n, ngv, sc)
        dho_ref[...] = dh + dhn
        _acc_rows(st_ref, i == 0, [dsh, dsc, dgate, dng])
        _acc_rows(ln_ref, i == 0, [dlng, dlnb])

    row = lambda w: pl.BlockSpec((tm, w), lambda i: (i, 0))
    return pl.pallas_call(
        body, name="gmlp_bwd", grid=(l_rows // tm,),
        out_shape=[jax.ShapeDtypeStruct((l_rows, d), F32), jax.ShapeDtypeStruct((l_rows, d), BF16),
                   jax.ShapeDtypeStruct((l_rows, 2 * half), BF16), jax.ShapeDtypeStruct((l_rows, half), BF16),
                   jax.ShapeDtypeStruct((l_rows, d), BF16), jax.ShapeDtypeStruct((8, d), F32),
                   jax.ShapeDtypeStruct((8, half), F32), jax.ShapeDtypeStruct(ws.shape, F32),
                   jax.ShapeDtypeStruct((CHUNK, 128), F32)],
        in_specs=[row(d), row(d), row(d), pl.BlockSpec((None, 8, d), lambda i: (1, 0, 0)),
                  _const_spec((1, d)), _const_spec(win.shape), _const_spec((1, half)), _const_spec((1, half)),
                  _const_spec(ws.shape), _const_spec(ws.shape), _const_spec(bst.shape), _const_spec(wout.shape)],
        out_specs=[row(d), row(d), row(2 * half), row(half), row(d),
                   pl.BlockSpec((8, d), lambda i: (0, 0)), pl.BlockSpec((8, half), lambda i: (0, 0)),
                   pl.BlockSpec(ws.shape, lambda i: (0, 0, 0)), pl.BlockSpec((CHUNK, 128), lambda i: (0, 0))],
        compiler_params=_params(("arbitrary",)),
    )(dh1, h, y, mods, ng, win, lng, lnb, ws, wst, bst, wout)


def _head(h, final_g, target, tm):
    l_rows, d = h.shape
    n_tiles = l_rows // tm

    def body(h_ref, g_ref, t_ref, dh_ref, loss_ref, dg_ref, acc_ref):
        i = pl.program_id(0)
        g = g_ref[...]
        hv = h_ref[...]
        rstd = lax.rsqrt(jnp.mean(hv * hv, axis=-1, keepdims=True) + EPS)
        xhat = hv * rstd
        e = xhat * g - t_ref[...]
        dout = e * (1.0 / d)
        dxhat = dout * g
        dh_ref[...] = rstd * (dxhat - xhat * jnp.mean(dxhat * xhat, axis=-1, keepdims=True))
        _acc_rows(dg_ref, i == 0, [_colsum(dout * xhat)])
        _acc_rows(acc_ref, i == 0, [_colsum(e * e)])

        @pl.when(i == n_tiles - 1)
        def _():
            total = jnp.sum(acc_ref[0:1, :], axis=-1, keepdims=True) * (0.5 / d)
            loss_ref[...] = jnp.broadcast_to(total, loss_ref.shape)

    return pl.pallas_call(
        body, name="loss_head", grid=(n_tiles,),
        out_shape=[jax.ShapeDtypeStruct((l_rows, d), F32), jax.ShapeDtypeStruct((8, 128), F32),
                   jax.ShapeDtypeStruct((8, d), F32)],
        in_specs=[pl.BlockSpec((tm, d), lambda i: (i, 0)), _const_spec((1, d)),
                  pl.BlockSpec((tm, d), lambda i: (i, 0))],
        out_specs=[pl.BlockSpec((tm, d), lambda i: (i, 0)), pl.BlockSpec((8, 128), lambda i: (0, 0)),
                   pl.BlockSpec((8, d), lambda i: (0, 0))],
        scratch_shapes=[pltpu.VMEM((8, d), F32)],
        compiler_params=_params(("arbitrary",)),
    )(h, final_g, target)


def _adamw(w, gparts, m, v, name):
    shape = w.shape
    cols = shape[-1]
    rows = int(np.prod(shape[:-1])) if len(shape) > 1 else 1
    pieces = list(gparts) if isinstance(gparts, (list, tuple)) else [gparts]
    n_pieces = len(pieces)
    nparts = pieces[0].shape[0]
    piece_rows = rows // n_pieces
    w2, m2, v2 = (t.reshape(rows, cols) for t in (w, m, v))
    pieces = [g.reshape(nparts, piece_rows, cols) for g in pieces]
    tr = piece_rows
    part_bytes = nparts * cols * pieces[0].dtype.itemsize
    for cand in (1024, 512, 256, 128, 64, 32, 16, 8):
        if piece_rows * max(part_bytes, cols * 4) <= (2 << 20):
            break
        if piece_rows % cand == 0 and cand < piece_rows:
            tr = cand
            if cand * max(part_bytes, cols * 4) <= (2 << 20):
                break
    per_piece = piece_rows // tr
    c1 = 1.0 - ADAM_B1 ** ADAM_STEP
    c2 = 1.0 - ADAM_B2 ** ADAM_STEP

    def update(w_ref, g_ref, m_ref, v_ref, go_ref, d_ref, mo_ref, vo_ref):
        g = g_ref[0].astype(F32)
        for k in range(1, nparts):
            g = g + g_ref[k].astype(F32)
        mn = ADAM_B1 * m_ref[...] + (1.0 - ADAM_B1) * g
        vn = ADAM_B2 * v_ref[...] + (1.0 - ADAM_B2) * (g * g)
        go_ref[...] = g
        mo_ref[...] = mn
        vo_ref[...] = vn
        d_ref[...] = -ADAM_LR * ((mn / c1) / (jnp.sqrt(vn / c2) + ADAM_EPS) + ADAM_WD * w_ref[...])

    def body(w_ref, *refs):
        g_refs, (m_ref, v_ref, go_ref, d_ref, mo_ref, vo_ref) = refs[:n_pieces], refs[n_pieces:]
        if n_pieces == 1:
            update(w_ref, g_refs[0], m_ref, v_ref, go_ref, d_ref, mo_ref, vo_ref)
        else:
            piece = pl.program_id(0) // per_piece
            for k in range(n_pieces):
                pl.when(piece == k)(functools.partial(update, w_ref, g_refs[k], m_ref, v_ref, go_ref, d_ref, mo_ref, vo_ref))

    def piece_spec(k):
        return pl.BlockSpec((nparts, tr, cols), lambda i: (0, jnp.clip(i - k * per_piece, 0, per_piece - 1), 0))

    spec = pl.BlockSpec((tr, cols), lambda i: (i, 0))
    outs = pl.pallas_call(
        body, name=name, grid=(rows // tr,),
        out_shape=[jax.ShapeDtypeStruct((rows, cols), F32)] * 4,
        in_specs=[spec] + [piece_spec(k) for k in range(n_pieces)] + [spec, spec],
        out_specs=[spec] * 4,
        compiler_params=_params(("arbitrary",)),
    )(w2, *pieces, m2, v2)
    return tuple(o.reshape(shape) for o in outs)


def _natural_cols(g):
    return jnp.moveaxis(g, 0, -2).reshape(g.shape[1:-1] + (N_DEV * g.shape[-1],))


def _natural_rows(g):
    return jnp.moveaxis(g, 0, -3).reshape(g.shape[1:-2] + (N_DEV * g.shape[-2], g.shape[-1]))


def _shard_rows(full):
    r = full.shape[-2] // N_DEV
    return jnp.moveaxis(full.reshape(full.shape[:-2] + (N_DEV, r, full.shape[-1])), -3, 0)


def _my_cols(gathered, me, n):
    return lax.dynamic_slice_in_dim(gathered, me * n, n, axis=gathered.ndim - 1)


def kernel(x, c, ctx, c_ctx, ada_w, ada_b, norm_g, mlp_w1, mlp_w2, pool_w, pool_scale, attn_w_qkv, attn_w_o, attn_q_g, attn_k_g, gm_w_in, gm_ln_g, gm_ln_b, gm_ws, gm_bs, gm_w_out, final_g, loss_target, m_c_ctx, m_ada_w, m_ada_b, m_norm_g, m_mlp_w1, m_mlp_w2, m_pool_w, m_pool_scale, m_attn_w_qkv, m_attn_w_o, m_attn_q_g, m_attn_k_g, m_gm_w_in, m_gm_ln_g, m_gm_ln_b, m_gm_ws, m_gm_bs, m_gm_w_out, m_final_g, v_c_ctx, v_ada_w, v_ada_b, v_norm_g, v_mlp_w1, v_mlp_w2, v_pool_w, v_pool_scale, v_attn_w_qkv, v_attn_w_o, v_attn_q_g, v_attn_k_g, v_gm_w_in, v_gm_ln_g, v_gm_ln_b, v_gm_ws, v_gm_bs, v_gm_w_out, v_final_g):
    l_len, d = x.shape[1], x.shape[2]
    c_len = ctx.shape[1]
    n_layers = ada_w.shape[0]
    assert n_layers == 4 and x.shape[0] == 1
    n_heads = d // HEAD_DIM
    n_kv = n_heads // 2
    half = gm_w_out.shape[1] * N_DEV
    tm = c_len if c_len <= 256 else 256
    assert c_len % tm == 0 and l_len % tm == 0 and tm % CHUNK == 0 and l_len % GRID_W == 0
    nct = c_len // tm
    me = _dev_index(*_coords())
    n_ada = ada_w.shape[-1]

    first = [t.astype(BF16) for t in (mlp_w1[0], mlp_w2[0], pool_w, attn_w_qkv[0])]
    small = [c, norm_g.reshape(n_layers * 2, -1), pool_scale, gm_ln_g, gm_ln_b]
    w1_0g, w2_0g, pool_g, qkv_g, c_all, ng_g, ps_g, lng_g, lnb_g = _all_gather(first + small, "gather_first")
    c_all = c_all.reshape(N_DEV, d)
    later = _GatherAcrossChips([t.astype(BF16) for t in
                                (mlp_w1[1], mlp_w1[2], mlp_w1[3], mlp_w2[1], mlp_w2[2], mlp_w2[3],
                                 attn_w_o[0], gm_w_in[0], gm_w_out[0])])
    pool_wf = _natural_rows(pool_g)
    wqkv = _natural_cols(qkv_g)
    ng_full = _natural_cols(ng_g.reshape(N_DEV, n_layers * 2, 1, -1)).reshape(n_layers, 2, 1, d)
    ps_full = _natural_cols(ps_g.reshape(N_DEV, 2, 1, -1))
    lng_full = _natural_cols(lng_g.reshape(N_DEV, 1, -1))
    lnb_full = _natural_cols(lnb_g.reshape(N_DEV, 1, -1))

    c_ctx2 = c_ctx.reshape(1, d)
    ada_b_loc = lax.dynamic_slice_in_dim(ada_b, me * n_ada, n_ada, axis=1).reshape(n_layers, 1, n_ada)
    (mod_g,) = _all_gather([_mods_local(c_all, c_ctx2, ada_w, ada_b_loc)], "gather_mods")
    mod_full = jnp.moveaxis(mod_g, 0, 2).reshape(n_layers, 16, 6, d)
    mod_lat = lax.dynamic_index_in_dim(mod_full, me, axis=1, keepdims=False)
    mod_ctx = mod_full[:, 8]
    mods = jnp.stack([mod_ctx, mod_lat], axis=1)
    mods = jnp.concatenate([mods, jnp.zeros((n_layers, 2, 2, d), F32)], axis=2)

    bands = _pool_bands(tm)
    cos, sin = _rope_tables(c_len, l_len)
    ws_bf = gm_ws[0].astype(BF16)
    wst_bf = jnp.swapaxes(gm_ws[0], 1, 2).astype(BF16)
    bst = jnp.zeros((CHUNK, 128), F32).at[:, :GMLP_GROUPS].set(gm_bs[0].T)
    ng = lambda i, j: ng_full[i, j]

    h0 = jnp.concatenate([ctx[0], x[0]], axis=0)
    y0, h1 = _pool_fwd(h0, mods[0], ng(0, 0), pool_wf[0].astype(BF16), ps_full[0], bands, nct, tm, c_len, l_len)
    h2, p0, ym0 = _mlp_fwd(h1, mods[0], ng(0, 1), w1_0g, w2_0g, nct, tm)
    q, k, v1, qt = _qkv_fwd(h2, mods[1], ng(1, 0), wqkv, attn_q_g, attn_k_g, cos, sin, n_heads, n_kv, nct, tm)
    o, lse, later_g = _flash_fwd(q, k, v1, n_kv, 4 * tm, tm, rider=later)
    (wo_g,) = _exchange_call(_ForwardToSibling(later_g[6:7]), "forward_wo")
    wo = _natural_rows(wo_g)
    y1, h3, rest_g = _wo_fwd(h2, o, wo, mods[1], nct, tm, rider=_ForwardToSibling(later_g[0:6] + later_g[7:9]))
    w1 = [w1_0g] + rest_g[0:3]
    w2 = [w2_0g] + rest_g[3:6]
    win = rest_g[6]
    wout = _natural_rows(rest_g[7])
    tm_lat = 2 * tm
    h4, p1, ym1 = _mlp_fwd(h3, mods[1], ng(1, 1), w1[1], w2[1], 0, tm_lat)
    y2, h5 = _gmlp_fwd(h4, mods[2], ng(2, 0), win, lng_full, lnb_full, ws_bf, bst, wout, tm)
    h6, p2, ym2 = _mlp_fwd(h5, mods[2], ng(2, 1), w1[2], w2[2], 0, tm_lat)
    y3, h7 = _pool_fwd(h6, mods[3], ng(3, 0), pool_wf[1].astype(BF16), ps_full[1], bands, 0, tm, c_len, l_len)
    h8, p3, ym3 = _mlp_fwd(h7, mods[3], ng(3, 1), w1[3], w2[3], 0, tm_lat)
    dh, loss_part, dfinal = _head(h8, final_g.reshape(1, d), loss_target[0], tm_lat)

    dw1, dw2, st_mlp = [None] * 4, [None] * 4, [None] * 4

    def mlp_back(i, dh, h_in, p, ym, nct_i):
        dh_in, m_bf, du, dacc, st, _ = _mlp_bwd(dh, h_in, p, ym, mods[i], ng(i, 1), w1[i], w2[i], nct_i, tm_lat)
        dw1[i] = _tn_matmul(m_bf, du, "tn_w1", col_shards=True)
        dw2[i] = _shard_rows(_tn_matmul(p, dacc, "tn_w2", square_x=True))
        st_mlp[i] = st
        return dh_in

    dh = mlp_back(3, dh, h7, p3, ym3, 0)
    dh, dpw1, st_pool3, _ = _pool_bwd(dh, h6, y3, mods[3], ng(3, 0), pool_wf[1].astype(BF16), ps_full[1], bands,
                                      0, tm, c_len, l_len, False)
    dh = mlp_back(2, dh, h5, p2, ym2, 0)
    dh, a_bf, dz, gated, dy, st_g, st_ln, dws, dbst = _gmlp_bwd(
        dh, h4, y2, mods[2], ng(2, 0), win, lng_full, lnb_full, ws_bf, wst_bf, bst, wout, tm)
    dwin = _tn_matmul(a_bf, dz, "tn_gm_in", col_shards=True)
    dwout = _tn_matmul(gated, dy, "tn_gm_out")
    dh = mlp_back(1, dh, h3, p1, ym1, 0)
    dy1, do, dot, delta, st_wo = _wo_bwd(dh, y1, o, wo, mods[1], n_kv, tm)
    dwo = _tn_matmul(o, dy1, "tn_wo")
    grads_mid = _AllToAll(dw1[1:] + dw2[1:] + [_shard_rows(dpw1.astype(BF16)), _shard_rows(dwo), dwin,
                                               _shard_rows(dwout)])
    dq, dkt, dvt, rode = _flash_bwd(q, qt, k, v1, do, dot, lse, delta, n_kv, 4 * tm, tm, rider=grads_mid)
    g_w1, g_w2, (g_pool1, g_wo, g_gin, g_gout) = [None] + rode[0:3], [None] + rode[3:6], rode[6:]
    dh, a_bf, dqkv, st_q, dgains = _qkv_bwd(h2, dh, dq, dkt, dvt, mods[1], ng(1, 0), wqkv, attn_q_g, attn_k_g,
                                            cos, sin, n_heads, n_kv, nct, tm)
    dwqkv = _tn_matmul(a_bf, dqkv, "tn_qkv", col_shards=True)
    dh, m_bf, du, dacc, st_mlp[0], (g_qkv,) = _mlp_bwd(dh, h1, p0, ym0, mods[0], ng(0, 1), w1[0], w2[0], nct, tm,
                                                       rider=_AllToAll([dwqkv]))
    dw1_0 = _tn_matmul(m_bf, du, "tn_w1", col_shards=True)
    dw2_0, (g_w1[0],) = _tn_matmul(p0, dacc, "tn_w2", square_x=True, rider=_AllToAll([dw1_0]))
    grad_x, dpw0, st_pool0, (g_w2[0],) = _pool_bwd(dh, h0, y0, mods[0], ng(0, 0), pool_wf[0].astype(BF16),
                                                   ps_full[0], bands, nct, tm, c_len, l_len, True,
                                                   rider=_AllToAll([_shard_rows(dw2_0)]))

    mix_lat = [st_pool0[-1], st_q[1] + st_wo, st_g, st_pool3[-1]]
    mlp_lat = [st[-1] for st in st_mlp]
    dmod_lat = jnp.stack([jnp.concatenate([mix_lat[i][0:3], mlp_lat[i][0:3]]) for i in range(n_layers)])
    dmod_ctx = jnp.stack([jnp.concatenate([st_pool0[0][0:3], st_mlp[0][0][0:3]]),
                          jnp.concatenate([st_q[0][0:2], jnp.zeros((4, d), F32)]),
                          jnp.zeros((6, d), F32), jnp.zeros((6, d), F32)])
    dng_part = jnp.stack([jnp.stack([mix_lat[0][3] + st_pool0[0][3], mlp_lat[0][3] + st_mlp[0][0][3]]),
                          jnp.stack([mix_lat[1][3] + st_q[0][3], mlp_lat[1][3]]),
                          jnp.stack([mix_lat[2][3], mlp_lat[2][3]]),
                          jnp.stack([mix_lat[3][3], mlp_lat[3][3]])])
    dps_part = jnp.stack([mix_lat[0][4] + st_pool0[0][4], mix_lat[3][4]])
    small_parts = [dmod_lat.reshape(n_layers * 6, d), dmod_ctx.reshape(n_layers * 6, d),
                   dng_part.reshape(n_layers * 2, d), dps_part, st_ln, dgains, dws.reshape(-1, CHUNK),
                   dbst, dfinal, loss_part]
    (gm_lat, gm_ctx, g_ng, g_ps, g_ln, g_gains, g_ws, g_bst, g_final, loss_all, g_pool0) = _all_gather(
        small_parts, "gather_small_grads", extra=_AllToAll([_shard_rows(dpw0.astype(BF16))]))
    g_pool = jnp.stack([g_pool0, g_pool1], axis=1)

    gm_lat4 = gm_lat.reshape(N_DEV, n_layers, 6 * d)
    gm_ctx4 = gm_ctx.reshape(N_DEV, n_layers, 6 * d)
    dm_lat_loc = jnp.moveaxis(_my_cols(gm_lat4, me, n_ada), 0, 1)
    dm_ctx_loc = jnp.moveaxis(_my_cols(gm_ctx4, me, n_ada), 0, 1)
    g_ada_w, ds_part = _ada_grads(c_all, c_ctx2, ada_w, dm_lat_loc, dm_ctx_loc)
    (ds_all,) = _all_gather([ds_part], "gather_dsctx")
    g_c_ctx, loss_sum = _cctx_grad_and_loss(ds_all, c_ctx2, loss_all)
    g_c_ctx = g_c_ctx.reshape(d)

    n_ng = norm_g.shape[-1]
    n_ps = pool_scale.shape[-1]
    n_ln = gm_ln_g.shape[-1]
    gparts = {
        "c_ctx": g_c_ctx[None],
        "ada_w": g_ada_w[None],
        "ada_b": jnp.concatenate([gm_lat4, gm_ctx4], axis=0),
        "norm_g": _my_cols(g_ng.reshape(N_DEV, n_layers, 2, d), me, n_ng),
        "mlp_w1": g_w1, "mlp_w2": g_w2, "pool_w": g_pool,
        "pool_scale": _my_cols(g_ps, me, n_ps),
        "attn_w_qkv": g_qkv[:, None], "attn_w_o": g_wo[:, None],
        "attn_q_g": g_gains[:, 0:1], "attn_k_g": g_gains[:, 1:2],
        "gm_w_in": g_gin[:, None],
        "gm_ln_g": _my_cols(g_ln[:, 0:1], me, n_ln), "gm_ln_b": _my_cols(g_ln[:, 1:2], me, n_ln),
        "gm_ws": g_ws.reshape((N_DEV,) + gm_ws.shape),
        "gm_bs": jnp.swapaxes(g_bst[:, :, :GMLP_GROUPS], 1, 2)[:, None],
        "gm_w_out": g_gout[:, None],
        "final_g": g_final[:, 0],
    }
    weights = dict(c_ctx=(c_ctx, m_c_ctx, v_c_ctx), ada_w=(ada_w, m_ada_w, v_ada_w), ada_b=(ada_b, m_ada_b, v_ada_b),
                   norm_g=(norm_g, m_norm_g, v_norm_g), mlp_w1=(mlp_w1, m_mlp_w1, v_mlp_w1),
                   mlp_w2=(mlp_w2, m_mlp_w2, v_mlp_w2), pool_w=(pool_w, m_pool_w, v_pool_w),
                   pool_scale=(pool_scale, m_pool_scale, v_pool_scale),
                   attn_w_qkv=(attn_w_qkv, m_attn_w_qkv, v_attn_w_qkv), attn_w_o=(attn_w_o, m_attn_w_o, v_attn_w_o),
                   attn_q_g=(attn_q_g, m_attn_q_g, v_attn_q_g), attn_k_g=(attn_k_g, m_attn_k_g, v_attn_k_g),
                   gm_w_in=(gm_w_in, m_gm_w_in, v_gm_w_in), gm_ln_g=(gm_ln_g, m_gm_ln_g, v_gm_ln_g),
                   gm_ln_b=(gm_ln_b, m_gm_ln_b, v_gm_ln_b), gm_ws=(gm_ws, m_gm_ws, v_gm_ws),
                   gm_bs=(gm_bs, m_gm_bs, v_gm_bs), gm_w_out=(gm_w_out, m_gm_w_out, v_gm_w_out),
                   final_g=(final_g, m_final_g, v_final_g))
    grads, deltas, new_m, new_v = [], [], [], []
    for wname, (w_, m_, v_) in weights.items():
        g_, d_, nm_, nv_ = _adamw(w_, gparts[wname], m_, v_, "adamw_" + wname)
        grads.append(g_)
        deltas.append(d_)
        new_m.append(nm_)
        new_v.append(nv_)

    return (loss_sum[0, 0], grad_x[None], *grads, *deltas, *new_m, *new_v)
```

```python
import functools
import math

import numpy as np
import jax
import jax.numpy as jnp
from jax import lax
from jax.experimental import pallas as pl
from jax.experimental.pallas import tpu as pltpu

F32 = jnp.float32
BF16 = jnp.bfloat16
MESH_ID = pl.DeviceIdType.MESH

N_DEV = 8
EPS = 1e-6
HEAD_DIM = 128
GRID_W = 64
ROPE_BASE = 10000.0
CHUNK = 128
POOL_WINDOWS = (2, 4, 8, 16)
POOL_GROUPS = 4
POOL_HALO = 8
GMLP_GROUPS = 8
ADAM_LR, ADAM_B1, ADAM_B2, ADAM_EPS, ADAM_WD, ADAM_STEP = 0.001, 0.9, 0.999, 1e-08, 0.01, 10

V7X_VMEM_BYTES = 64 << 20
VMEM_LIMIT_BIG = V7X_VMEM_BYTES - (8 << 20)
FLASH_TK_CAP = 768
LOG2E = math.log2(math.e)
Q_SCALE = HEAD_DIM ** -0.5 * LOG2E


def _params(sem, vmem=VMEM_LIMIT_BIG):
    return pltpu.CompilerParams(dimension_semantics=sem, vmem_limit_bytes=vmem)


def _const_spec(shape):
    nd = len(shape)
    return pl.BlockSpec(shape, lambda *_: (0,) * nd, pipeline_mode=pl.Buffered(1))


def _dot(a, b):
    return jnp.dot(a, b, preferred_element_type=F32)


def _dot_nt(a, b):
    return lax.dot_general(a, b, (((1,), (1,)), ((), ())), preferred_element_type=F32)


def _dot_tn(a, b):
    return lax.dot_general(a, b, (((0,), (0,)), ((), ())), preferred_element_type=F32)


def _colsum(x):
    return jnp.sum(x, axis=0, keepdims=True)


def _sid(i, nct):
    if nct == 0:
        return 1
    return jnp.where(i >= nct, 1, 0)


def _n_streams(nct):
    return 2 if nct else 1


def _stat_sid(i, nct):
    return _sid(i, nct) if nct else 0


def _first_of_stream(i, nct):
    if nct == 0:
        return i == 0
    return jnp.logical_or(i == 0, i == nct)


def _normmod(h, ng, sh, sc):
    rstd = lax.rsqrt(jnp.mean(h * h, axis=-1, keepdims=True) + EPS)
    xhat = h * rstd
    n = xhat * ng
    return n * (1.0 + sc) + sh, xhat, rstd, n


def _normmod_bwd(da, xhat, rstd, n, ng, sc):
    dsh = _colsum(da)
    dsc = _colsum(da * n)
    dn = da * (1.0 + sc)
    dng = _colsum(dn * xhat)
    dxhat = dn * ng
    dh = rstd * (dxhat - xhat * jnp.mean(dxhat * xhat, axis=-1, keepdims=True))
    return dh, dsh, dsc, dng


def _acc_rows(ref, first, rows):
    @pl.when(first)
    def _():
        ref[...] = jnp.zeros_like(ref)
    for r, val in enumerate(rows):
        ref[r:r + 1, :] = ref[r:r + 1, :] + val


_GELU_C = math.sqrt(2.0 / math.pi)


def _gelu(x):
    t = jnp.tanh((_GELU_C * x) * (1.0 + 0.044715 * (x * x)))
    hx = 0.5 * x
    return hx + hx * t


def _gelu_and_grad(x):
    x2 = x * x
    t = jnp.tanh((_GELU_C * x) * (1.0 + 0.044715 * x2))
    hx = 0.5 * x
    g = hx + hx * t
    dg = (0.5 + 0.5 * t) + (hx * (1.0 - t * t)) * (_GELU_C + (3.0 * 0.044715 * _GELU_C) * x2)
    return g, dg


def _coords():
    return lax.axis_index("x"), lax.axis_index("y"), lax.axis_index("c")


def _dev_index(px, py, pc):
    return 4 * px + 2 * py + pc


def _all_gather(xs, name, extra=None):
    n = len(xs)
    e_in, e_in_specs, e_out, e_out_specs, e_scratch = _rider_parts(extra)
    ne = len(e_in)

    def body(*refs):
        x_refs, e_x = refs[:n], refs[n:n + ne]
        o_refs, e_o = refs[n + ne:2 * n + ne], refs[2 * n + ne:2 * n + 2 * ne]
        send_sems, recv_sems, local_sems = refs[2 * n + 2 * ne:2 * n + 2 * ne + 3]
        e_sems = refs[2 * n + 2 * ne + 3:]
        if extra is not None:
            extra.start(e_x, e_o, e_sems)
        x, y, c = _coords()
        me, sibling = (x, y, c), (x, y, 1 - c)
        chips = [(1 - x, y), (x, 1 - y), (1 - x, 1 - y)]

        def copy(a, k, block, to, src=None):
            dst = o_refs[a].at[_dev_index(*block)]
            return pltpu.make_async_remote_copy(
                src_ref=dst if src is None else src, dst_ref=dst,
                send_sem=send_sems.at[7 * a + k], recv_sem=recv_sems.at[7 * a + k],
                device_id=to, device_id_type=MESH_ID)

        mine = [pltpu.make_async_copy(x_refs[a], o_refs[a].at[_dev_index(*me)], local_sems.at[a])
                for a in range(n)]
        for cp in mine:
            cp.start()
        first = []
        for a in range(n):
            first.append(copy(a, 0, me, sibling, src=x_refs[a]))
            first += [copy(a, 1 + j, me, (*chip, c), src=x_refs[a]) for j, chip in enumerate(chips)]
        for cp in first:
            cp.start()
        passed = []
        for a in range(n):
            for j, chip in enumerate(chips):
                copy(a, 1 + j, (*chip, c), me).wait_recv()
                fwd = copy(a, 4 + j, (*chip, c), sibling)
                fwd.start()
                passed.append(fwd)
        for a in range(n):
            copy(a, 0, sibling, me).wait_recv()
            for j, chip in enumerate(chips):
                copy(a, 4 + j, (*chip, 1 - c), me).wait_recv()
        for cp in first + passed:
            cp.wait_send()
        for cp in mine:
            cp.wait()
        if extra is not None:
            extra.finish(e_x, e_o, e_sems)

    any_spec = pl.BlockSpec(memory_space=pl.ANY)
    outs = pl.pallas_call(
        body, name=name,
        out_shape=[jax.ShapeDtypeStruct((N_DEV,) + x.shape, x.dtype) for x in xs] + e_out,
        in_specs=[any_spec] * n + e_in_specs, out_specs=[any_spec] * n + e_out_specs,
        scratch_shapes=[pltpu.SemaphoreType.DMA((7 * n,)), pltpu.SemaphoreType.DMA((7 * n,)),
                        pltpu.SemaphoreType.DMA((n,))] + e_scratch,
    )(*xs, *e_in)
    return list(outs)


class _Exchange:
    per_array = 0
    in_place = False

    def __init__(self, xs):
        self.xs = list(xs)
        n = len(self.xs)
        self.out_shapes = self._out_shapes()
        self.scratch = [pltpu.SemaphoreType.DMA((self.per_array * n,)),
                        pltpu.SemaphoreType.DMA((self.per_array * n,)),
                        pltpu.SemaphoreType.DMA((n,))]

    def _out_shapes(self):
        raise NotImplementedError

    def _copies(self, x_refs, o_refs, sems):
        raise NotImplementedError

    def start(self, x_refs, o_refs, sems):
        mine, sends, _ = self._copies(x_refs, o_refs, sems)
        for cp in mine + sends:
            cp.start()

    def finish(self, x_refs, o_refs, sems):
        mine, sends, arrivals = self._copies(x_refs, o_refs, sems)
        for make in arrivals:
            make().wait_recv()
        for cp in sends:
            cp.wait_send()
        for cp in mine:
            cp.wait()


def _remote(src, dst, sems, k, to):
    return pltpu.make_async_remote_copy(src_ref=src, dst_ref=dst, send_sem=sems[0].at[k], recv_sem=sems[1].at[k],
                                        device_id=to, device_id_type=MESH_ID)


class _GatherAcrossChips(_Exchange):
    per_array = 4

    def _out_shapes(self):
        return [jax.ShapeDtypeStruct((N_DEV,) + x.shape, x.dtype) for x in self.xs]

    def _copies(self, x_refs, o_refs, sems):
        x, y, c = _coords()
        targets = [(x, y, 1 - c), (1 - x, y, c), (x, 1 - y, c), (1 - x, 1 - y, c)]
        mine, sends, arrivals = [], [], []
        for a, (x_ref, o_ref) in enumerate(zip(x_refs, o_refs)):
            own = o_ref.at[_dev_index(x, y, c)]
            mine.append(pltpu.make_async_copy(x_ref, own, sems[2].at[a]))
            for k, to in enumerate(targets):
                sends.append(_remote(x_ref, own, sems, 4 * a + k, to))
                arrivals.append(functools.partial(_remote, x_ref, o_ref.at[_dev_index(*to)], sems, 4 * a + k, to))
        return mine, sends, arrivals


class _ForwardToSibling(_Exchange):
    per_array = 3
    in_place = True

    def _out_shapes(self):
        return [jax.ShapeDtypeStruct(x.shape, x.dtype) for x in self.xs]

    def _copies(self, x_refs, o_refs, sems):
        x, y, c = _coords()
        chips = [(1 - x, y), (x, 1 - y), (1 - x, 1 - y)]
        sends, arrivals = [], []
        for a, (x_ref, o_ref) in enumerate(zip(x_refs, o_refs)):
            for j, chip in enumerate(chips):
                held = _dev_index(*chip, c)
                sends.append(_remote(x_ref.at[held], o_ref.at[held], sems, 3 * a + j, (x, y, 1 - c)))
                theirs = _dev_index(*chip, 1 - c)
                arrivals.append(functools.partial(_remote, x_ref.at[theirs], o_ref.at[theirs], sems, 3 * a + j,
                                                  (x, y, 1 - c)))
        return [], sends, arrivals


class _AllToAll(_Exchange):
    per_array = 7

    def _out_shapes(self):
        return [jax.ShapeDtypeStruct(x.shape, x.dtype) for x in self.xs]

    def _copies(self, x_refs, o_refs, sems):
        x, y, c = _coords()
        me_i = _dev_index(x, y, c)
        mine, sends, arrivals = [], [], []
        for a, (x_ref, o_ref) in enumerate(zip(x_refs, o_refs)):
            mine.append(pltpu.make_async_copy(x_ref.at[me_i], o_ref.at[me_i], sems[2].at[a]))
            for r in range(1, 8):
                to = (1 - x if r & 4 else x, 1 - y if r & 2 else y, 1 - c if r & 1 else c)
                to_i = _dev_index(*to)
                sends.append(_remote(x_ref.at[to_i], o_ref.at[me_i], sems, 7 * a + r - 1, to))
                arrivals.append(functools.partial(_remote, x_ref.at[to_i], o_ref.at[to_i], sems, 7 * a + r - 1, to))
        return mine, sends, arrivals


def _exchange_call(ex, name):
    n = len(ex.xs)

    def body(*refs):
        x_refs, o_refs, sems = refs[:n], refs[n:2 * n], refs[2 * n:]
        ex.start(x_refs, o_refs, sems)
        ex.finish(x_refs, o_refs, sems)

    any_spec = pl.BlockSpec(memory_space=pl.ANY)
    outs = pl.pallas_call(
        body, name=name, out_shape=ex.out_shapes, in_specs=[any_spec] * n, out_specs=[any_spec] * n,
        scratch_shapes=ex.scratch, input_output_aliases={a: a for a in range(n)} if ex.in_place else {},
    )(*ex.xs)
    return list(outs)


def _rider_parts(rider):
    if rider is None:
        return [], [], [], [], []
    any_spec = pl.BlockSpec(memory_space=pl.ANY)
    n = len(rider.xs)
    return rider.xs, [any_spec] * n, rider.out_shapes, [any_spec] * n, rider.scratch


def _compute_call(body, *, name, grid, in_specs, out_specs, out_shape, operands, scratch_shapes=(), rider=None):
    in_specs, out_specs, out_shape, scratch_shapes = list(in_specs), list(out_specs), list(out_shape), list(scratch_shapes)
    r_in, r_in_specs, r_out, r_out_specs, r_scratch = _rider_parts(rider)
    n_in, n_out, n_scr, nr = len(operands), len(out_shape), len(scratch_shapes), len(r_in)

    def riding_body(*refs):
        ins, refs = refs[:n_in], refs[n_in:]
        r_x, refs = refs[:nr], refs[nr:]
        outs, refs = refs[:n_out], refs[n_out:]
        r_o, refs = refs[:nr], refs[nr:]
        scratch, r_sems = refs[:n_scr], refs[n_scr:]
        if rider is not None:
            first, last = _grid_ends(grid)
            pl.when(first)(lambda: rider.start(r_x, r_o, r_sems))
        body(*ins, *outs, *scratch)
        if rider is not None:
            pl.when(last)(lambda: rider.finish(r_x, r_o, r_sems))

    in_place = rider is not None and rider.in_place
    res = pl.pallas_call(
        riding_body, name=name, grid=grid, out_shape=out_shape + r_out,
        in_specs=in_specs + r_in_specs, out_specs=out_specs + r_out_specs,
        scratch_shapes=scratch_shapes + r_scratch,
        input_output_aliases={n_in + a: n_out + a for a in range(nr)} if in_place else {},
        compiler_params=_params(("arbitrary",) * len(grid)),
    )(*operands, *r_in)
    return list(res[:n_out]), list(res[n_out:])


def _grid_ends(grid):
    first = pl.program_id(0) == 0
    last = pl.program_id(0) == grid[0] - 1
    for ax in range(1, len(grid)):
        first = jnp.logical_and(first, pl.program_id(ax) == 0)
        last = jnp.logical_and(last, pl.program_id(ax) == grid[ax] - 1)
    return first, last


def _silu(x):
    return x * (1.0 / (1.0 + jnp.exp(-x)))


def _cond_rows(c_all, c_ctx):
    d = c_all.shape[-1]
    s = jnp.concatenate([c_all, jnp.zeros((8, d), F32)], axis=0)
    row = lax.broadcasted_iota(jnp.int32, (16, d), 0)
    s = jnp.where(row == 8, c_ctx, s)
    return jnp.where(row <= 8, _silu(s), 0.0)


def _mods_local(c_all, c_ctx, ada_w, ada_b_loc):
    nl, d, n = ada_w.shape

    def body(c_ref, cc_ref, w_ref, b_ref, o_ref):
        s = _cond_rows(c_ref[...], cc_ref[...])
        o_ref[...] = jnp.dot(s, w_ref[...], preferred_element_type=F32,
                             precision=lax.Precision.HIGHEST) + b_ref[...]

    return pl.pallas_call(
        body, name="mods_local", grid=(nl,),
        out_shape=jax.ShapeDtypeStruct((nl, 16, n), F32),
        in_specs=[pl.BlockSpec((8, d), lambda i: (0, 0)), pl.BlockSpec((1, d), lambda i: (0, 0)),
                  pl.BlockSpec((None, d, n), lambda i: (i, 0, 0)),
                  pl.BlockSpec((None, 1, n), lambda i: (i, 0, 0))],
        out_specs=pl.BlockSpec((None, 16, n), lambda i: (i, 0, 0)),
        compiler_params=_params(("arbitrary",)),
    )(c_all, c_ctx, ada_w, ada_b_loc)


def _ada_grads(c_all, c_ctx, ada_w, dm_lat, dm_ctx):
    nl, d, n = ada_w.shape

    def body(c_ref, cc_ref, w_ref, dml_ref, dmc_ref, gw_ref, ds_ref):
        i = pl.program_id(0)
        s = _cond_rows(c_ref[...], cc_ref[...])
        csum = dmc_ref[0:1, :]
        for k in range(1, N_DEV):
            csum = csum + dmc_ref[k:k + 1, :]
        row = lax.broadcasted_iota(jnp.int32, (8, n), 0)
        dm_c = jnp.where(row == 0, csum, 0.0)
        dm = jnp.concatenate([dml_ref[...], dm_c], axis=0)
        gw_ref[...] = lax.dot_general(s, dm, (((0,), (0,)), ((), ())), preferred_element_type=F32,
                                      precision=lax.Precision.HIGHEST)
        ds = lax.dot_general(dm_c, w_ref[...], (((1,), (1,)), ((), ())),
                             preferred_element_type=F32, precision=lax.Precision.HIGHEST)

        @pl.when(i == 0)
        def _():
            ds_ref[...] = jnp.zeros_like(ds_ref)
        ds_ref[...] += ds

    return pl.pallas_call(
        body, name="ada_grads", grid=(nl,),
        out_shape=[jax.ShapeDtypeStruct((nl, d, n), F32), jax.ShapeDtypeStruct((8, d), F32)],
        in_specs=[pl.BlockSpec((8, d), lambda i: (0, 0)), pl.BlockSpec((1, d), lambda i: (0, 0)),
                  pl.BlockSpec((None, d, n), lambda i: (i, 0, 0)),
                  pl.BlockSpec((None, 8, n), lambda i: (i, 0, 0)),
                  pl.BlockSpec((None, 8, n), lambda i: (i, 0, 0))],
        out_specs=[pl.BlockSpec((None, d, n), lambda i: (i, 0, 0)),
                   pl.BlockSpec((8, d), lambda i: (0, 0))],
        compiler_params=_params(("arbitrary",)),
    )(c_all, c_ctx, ada_w, dm_lat, dm_ctx)


def _cctx_grad_and_loss(ds_parts, c_ctx, loss_parts):
    d = c_ctx.shape[-1]

    def body(p_ref, c_ref, l_ref, o_ref, lo_ref):
        ds, loss = p_ref[0], l_ref[0]
        for k in range(1, N_DEV):
            ds = ds + p_ref[k]
            loss = loss + l_ref[k]
        x = c_ref[...]
        sg = 1.0 / (1.0 + jnp.exp(-x))
        o_ref[...] = ds[0:1, :] * (sg * (1.0 + x * (1.0 - sg)))
        lo_ref[...] = loss

    return pl.pallas_call(body, name="cctx_grad", out_shape=[jax.ShapeDtypeStruct((1, d), F32),
                                                             jax.ShapeDtypeStruct((8, 128), F32)])(ds_parts, c_ctx, loss_parts)


def _mlp_fwd(h1, mods, ng, w1, w2, nct, tm, row_off=0):
    d = h1.shape[1]
    r = h1.shape[0] - row_off * tm
    fc = w1.shape[2]
    f = N_DEV * fc

    def body(h_ref, mod_ref, ng_ref, w1_ref, w2_ref, h2_ref, p_ref, y_ref):
        h = h_ref[...]
        a, _, _, _ = _normmod(h, ng_ref[...], mod_ref[3:4, :], mod_ref[4:5, :])
        ab = a.astype(BF16)
        acc = jnp.zeros((tm, d), F32)
        for j in range(N_DEV):
            sl = slice(j * fc, (j + 1) * fc)
            p = jnp.maximum(_dot(ab, w1_ref[j]), 0.0)
            p_ref[:, sl] = p.astype(BF16)
            acc = acc + _dot((p * p).astype(BF16), w2_ref[j])
        y_ref[...] = acc.astype(BF16)
        h2_ref[...] = h + mod_ref[5:6, :] * acc

    return pl.pallas_call(
        body, name="mlp_fwd", grid=(r // tm,),
        out_shape=[jax.ShapeDtypeStruct((r, d), F32), jax.ShapeDtypeStruct((r, f), BF16),
                   jax.ShapeDtypeStruct((r, d), BF16)],
        in_specs=[pl.BlockSpec((tm, d), lambda i: (i + row_off, 0)),
                  pl.BlockSpec((None, 8, d), lambda i: (_sid(i, nct), 0, 0)),
                  _const_spec((1, d)), _const_spec(w1.shape), _const_spec(w2.shape)],
        out_specs=[pl.BlockSpec((tm, d), lambda i: (i, 0)), pl.BlockSpec((tm, f), lambda i: (i, 0)),
                   pl.BlockSpec((tm, d), lambda i: (i, 0))],
        compiler_params=_params(("arbitrary",)),
    )(h1, mods, ng, w1, w2)


def _mlp_bwd(dh2, h1, p, y, mods, ng, w1, w2, nct, tm, row_off=0, rider=None):
    r_rows, d = dh2.shape
    fc = w1.shape[2]
    f = N_DEV * fc

    def body(dh_ref, h_ref, p_ref, y_ref, mod_ref, ng_ref, w1_ref, w2_ref,
             dh1_ref, m_ref, du_ref, dacc_ref, st_ref):
        i = pl.program_id(0)
        dh = dh_ref[...]
        ngv, sc, gate = ng_ref[...], mod_ref[4:5, :], mod_ref[5:6, :]
        a, xhat, rstd, n = _normmod(h_ref[...], ngv, mod_ref[3:4, :], sc)
        m_ref[...] = a.astype(BF16)
        dgate = _colsum(dh * y_ref[...].astype(F32))
        dacc = (gate * dh).astype(BF16)
        dacc_ref[...] = dacc
        dm = jnp.zeros((tm, d), F32)
        for j in range(N_DEV):
            sl = slice(j * fc, (j + 1) * fc)
            pj = p_ref[:, sl].astype(F32)
            du = (_dot_nt(dacc, w2_ref[j]) * (2.0 * pj)).astype(BF16)
            du_ref[:, sl] = du
            dm = dm + _dot_nt(du, w1_ref[j])
        dhn, dsh, dsc, dng = _normmod_bwd(dm, xhat, rstd, n, ngv, sc)
        dh1_ref[...] = dh + dhn
        _acc_rows(st_ref, _first_of_stream(i, nct), [dsh, dsc, dgate, dng])

    outs, rode = _compute_call(
        body, name="mlp_bwd", grid=(r_rows // tm,), operands=(dh2, h1, p, y, mods, ng, w1, w2), rider=rider,
        out_shape=[jax.ShapeDtypeStruct((r_rows, d), F32), jax.ShapeDtypeStruct((r_rows, d), BF16),
                   jax.ShapeDtypeStruct((r_rows, f), BF16),
                   jax.ShapeDtypeStruct((r_rows, d), BF16), jax.ShapeDtypeStruct((_n_streams(nct), 8, d), F32)],
        in_specs=[pl.BlockSpec((tm, d), lambda i: (i, 0)),
                  pl.BlockSpec((tm, d), lambda i: (i + row_off, 0)),
                  pl.BlockSpec((tm, f), lambda i: (i, 0)), pl.BlockSpec((tm, d), lambda i: (i, 0)),
                  pl.BlockSpec((None, 8, d), lambda i: (_sid(i, nct), 0, 0)),
                  _const_spec((1, d)), _const_spec(w1.shape), _const_spec(w2.shape)],
        out_specs=[pl.BlockSpec((tm, d), lambda i: (i, 0)), pl.BlockSpec((tm, d), lambda i: (i, 0)),
                   pl.BlockSpec((tm, f), lambda i: (i, 0)),
                   pl.BlockSpec((tm, d), lambda i: (i, 0)),
                   pl.BlockSpec((None, 8, d), lambda i: (_stat_sid(i, nct), 0, 0))])
    return (*outs, rode)


def _pick(n, cands):
    for cand in cands:
        if n % cand == 0:
            return cand
    return n


def _tn_matmul(x, y, name, col_shards=False, square_x=False, rider=None):
    rows, k1 = x.shape
    k2 = y.shape[1]
    bt = _pick(rows, (1024, 768, 512, 384, 256, 128))
    bk1, bk2 = min(k1, 1024), min(k2, 1024)
    nt, nj = rows // bt, k2 // bk2
    n = k2 // N_DEV
    if col_shards:
        assert bk2 % n == 0
        per = bk2 // n
        out_shape = jax.ShapeDtypeStruct((N_DEV, k1, n), BF16)
        out_spec = pl.BlockSpec((N_DEV, bk1, n), lambda i, t, j: (0, i, 0))
    else:
        out_shape = jax.ShapeDtypeStruct((k1, k2), BF16)
        out_spec = pl.BlockSpec((bk1, k2), lambda i, t, j: (i, 0))

    def body(x_ref, y_ref, o_ref, acc_ref, xt_ref):
        t, j = pl.program_id(1), pl.program_id(2)

        @pl.when(j == 0)
        def _():
            xv = x_ref[...]
            xt_ref[...] = (xv * xv if square_x else xv).T
        part = _dot(xt_ref[...], y_ref[...])

        @pl.when(t == 0)
        def _():
            acc_ref[j] = part

        @pl.when(t > 0)
        def _():
            acc_ref[j] += part

        @pl.when(jnp.logical_and(t == nt - 1, j == nj - 1))
        def _():
            for b in range(nj):
                if col_shards:
                    for s in range(per):
                        o_ref[b * per + s] = acc_ref[b, :, s * n:(s + 1) * n].astype(BF16)
                else:
                    o_ref[:, b * bk2:(b + 1) * bk2] = acc_ref[b].astype(BF16)

    (out,), rode = _compute_call(
        body, name=name, grid=(k1 // bk1, nt, nj), operands=(x, y), rider=rider, out_shape=[out_shape],
        in_specs=[pl.BlockSpec((bt, bk1), lambda i, t, j: (t, i)), pl.BlockSpec((bt, bk2), lambda i, t, j: (t, j))],
        out_specs=[out_spec], scratch_shapes=[pltpu.VMEM((nj, bk1, bk2), F32), pltpu.VMEM((bk1, bt), BF16)])
    return out if rider is None else (out, rode)


def _pool_bands(tm):
    k = tm + 128
    t = np.arange(tm)[:, None]
    e = np.arange(k)[None, :]
    fwd, bwd = [], []
    for w in POOL_WINDOWS:
        lo = POOL_HALO + t - w // 2
        fwd.append(((e >= lo) & (e <= lo + w - 1)).astype(np.float32))
        lo_t = POOL_HALO + t - w // 2 + 1
        bwd.append(((e >= lo_t) & (e <= lo_t + w - 1)).astype(np.float32))
    return jnp.asarray(np.stack(fwd), BF16), jnp.asarray(np.stack(bwd), BF16)


def _pool_geometry(i, nct, n_tiles, tm, c_len, l_len):
    if nct == 0:
        pos0 = i * tm
        ls = l_len
        has_prev = i > 0
        has_next = i < n_tiles - 1
    else:
        in_ctx = i < nct
        pos0 = jnp.where(in_ctx, i, i - nct) * tm
        ls = jnp.where(in_ctx, c_len, l_len)
        has_prev = jnp.logical_and(i != 0, i != nct)
        has_next = jnp.logical_and(i != nct - 1, i != n_tiles - 1)
    return pos0, ls, has_prev, has_next


def _window_inv_counts(pos, ls):
    out = []
    for w in POOL_WINDOWS:
        lo = jnp.maximum(pos - w // 2, 0)
        hi = jnp.minimum(pos + w - w // 2, ls)
        cnt = jnp.maximum(hi - lo, 1).astype(F32)
        out.append(1.0 / cnt)
    return out


def _split_bf16(x):
    hi = x.astype(BF16)
    return hi, (x - hi.astype(F32)).astype(BF16)


def _extend(prev, tile, nxt, has_prev, has_next):
    w = tile.shape[1]
    prev = jnp.where(has_prev, prev, 0.0)
    nxt = jnp.where(has_next, nxt, 0.0)
    return jnp.concatenate([prev, tile, nxt, jnp.zeros((128 - 2 * POOL_HALO, w), F32)], axis=0)


def _pool_specs(tm, d, n_rows):
    last8 = n_rows // POOL_HALO - 1
    per = tm // POOL_HALO
    return [pl.BlockSpec((tm, d), lambda i: (i, 0)),
            pl.BlockSpec((POOL_HALO, d), lambda i: (jnp.maximum(i * per - 1, 0), 0)),
            pl.BlockSpec((POOL_HALO, d), lambda i: (jnp.minimum((i + 1) * per, last8), 0))]


def _pool_fwd(h, mods, ng, w, scale, bands, nct, tm, c_len, l_len):
    r, d = h.shape
    gw = d // POOL_GROUPS
    n_tiles = r // tm
    kx = tm + 128

    def body(h_ref, hp_ref, hn_ref, mod_ref, ng_ref, w_ref, sc_ref, band_ref, y_ref, h1_ref):
        i = pl.program_id(0)
        pos0, ls, has_prev, has_next = _pool_geometry(i, nct, n_tiles, tm, c_len, l_len)
        ngv, sh, sc = ng_ref[...], mod_ref[0:1, :], mod_ref[1:2, :]
        h = h_ref[...]
        a = _normmod(h, ngv, sh, sc)[0]
        a_ext = _extend(_normmod(hp_ref[...], ngv, sh, sc)[0], a, _normmod(hn_ref[...], ngv, sh, sc)[0],
                        has_prev, has_next)
        pos = pos0 + lax.broadcasted_iota(jnp.int32, (tm, 1), 0)
        inv = _window_inv_counts(pos, ls)
        ys = []
        for g in range(POOL_GROUPS):
            cols = slice(g * gw, (g + 1) * gw)
            hi, lo = _split_bf16(a_ext[:, cols])
            s = _dot(band_ref[g], hi) + _dot(band_ref[g], lo)
            pg = s * inv[g] - a[:, cols]
            ys.append(_dot(pg.astype(BF16), w_ref[g]))
        y = jnp.concatenate(ys, axis=1) * sc_ref[...]
        y_ref[...] = y.astype(BF16)
        h1_ref[...] = h + mod_ref[2:3, :] * y

    return pl.pallas_call(
        body, name="pool_fwd", grid=(n_tiles,),
        out_shape=[jax.ShapeDtypeStruct((r, d), BF16), jax.ShapeDtypeStruct((r, d), F32)],
        in_specs=_pool_specs(tm, d, r) + [
            pl.BlockSpec((None, 8, d), lambda i: (_sid(i, nct), 0, 0)),
            _const_spec((1, d)), _const_spec(w.shape), _const_spec((1, d)), _const_spec((4, tm, kx))],
        out_specs=[pl.BlockSpec((tm, d), lambda i: (i, 0)), pl.BlockSpec((tm, d), lambda i: (i, 0))],
        compiler_params=_params(("arbitrary",)),
    )(h, h, h, mods, ng, w, scale, bands[0])


def _pool_bwd(dh1, h, y, mods, ng, w, scale, bands, nct, tm, c_len, l_len, latent_out, rider=None):
    r, d = h.shape
    gw = d // POOL_GROUPS
    n_tiles = r // tm
    kx = tm + 128
    out_rows = l_len if latent_out else r
    out_off = nct if latent_out else 0

    def body(dh_ref, dhp_ref, dhn_ref, h_ref, hp_ref, hn_ref, y_ref, mod_ref, ng_ref, w_ref, sc_ref,
             bf_ref, bb_ref, dho_ref, dw_ref, st_ref):
        i = pl.program_id(0)
        pos0, ls, has_prev, has_next = _pool_geometry(i, nct, n_tiles, tm, c_len, l_len)
        ngv, sh, sc, gate = ng_ref[...], mod_ref[0:1, :], mod_ref[1:2, :], mod_ref[2:3, :]
        scale_v = sc_ref[...]
        h = h_ref[...]
        a, xhat, rstd, n = _normmod(h, ngv, sh, sc)
        a_ext = _extend(_normmod(hp_ref[...], ngv, sh, sc)[0], a, _normmod(hn_ref[...], ngv, sh, sc)[0],
                        has_prev, has_next)
        dh = dh_ref[...]
        dgate = _colsum(dh * y_ref[...].astype(F32))
        dy = gate * dh
        dy_ext = _extend(gate * dhp_ref[...], dy, gate * dhn_ref[...], has_prev, has_next)
        dyp_ext = (dy_ext * scale_v).astype(BF16)
        dyp = (dy * scale_v).astype(BF16)
        pos = pos0 + lax.broadcasted_iota(jnp.int32, (tm, 1), 0)
        inv = _window_inv_counts(pos, ls)
        pos_e = pos0 - POOL_HALO + lax.broadcasted_iota(jnp.int32, (kx, 1), 0)
        inv_e = _window_inv_counts(pos_e, ls)

        @pl.when(i == 0)
        def _():
            dw_ref[...] = jnp.zeros_like(dw_ref)
        das, dscale = [], []
        for g in range(POOL_GROUPS):
            cols = slice(g * gw, (g + 1) * gw)
            hi, lo = _split_bf16(a_ext[:, cols])
            pg = ((_dot(bf_ref[g], hi) + _dot(bf_ref[g], lo)) * inv[g] - a[:, cols]).astype(BF16)
            dscale.append(_colsum(dy[:, cols] * _dot(pg, w_ref[g])))
            dyp_g = dyp_ext[:, cols]
            dw_ref[g] += _dot_tn(pg, dyp[:, cols])
            dp_ext = _dot_nt(dyp_g, w_ref[g])
            hi, lo = _split_bf16(dp_ext * inv_e[g])
            das.append(_dot(bb_ref[g], hi) + _dot(bb_ref[g], lo) - dp_ext[POOL_HALO:POOL_HALO + tm, :])
        da = jnp.concatenate(das, axis=1)
        dhn, dsh, dsc, dng = _normmod_bwd(da, xhat, rstd, n, ngv, sc)
        dho_ref[...] = dh + dhn
        _acc_rows(st_ref, _first_of_stream(i, nct), [dsh, dsc, dgate, dng, jnp.concatenate(dscale, axis=1)])

    outs, rode = _compute_call(
        body, name="pool_bwd", grid=(n_tiles,), rider=rider,
        operands=(dh1, dh1, dh1, h, h, h, y, mods, ng, w, scale, bands[0], bands[1]),
        out_shape=[jax.ShapeDtypeStruct((out_rows, d), F32),
                   jax.ShapeDtypeStruct((POOL_GROUPS, gw, gw), F32),
                   jax.ShapeDtypeStruct((_n_streams(nct), 8, d), F32)],
        in_specs=_pool_specs(tm, d, r) + _pool_specs(tm, d, r) + [
            pl.BlockSpec((tm, d), lambda i: (i, 0)),
            pl.BlockSpec((None, 8, d), lambda i: (_sid(i, nct), 0, 0)),
            _const_spec((1, d)), _const_spec(w.shape), _const_spec((1, d)),
            _const_spec((4, tm, kx)), _const_spec((4, tm, kx))],
        out_specs=[pl.BlockSpec((tm, d), lambda i: (jnp.maximum(i - out_off, 0), 0)),
                   pl.BlockSpec((POOL_GROUPS, gw, gw), lambda i: (0, 0, 0)),
                   pl.BlockSpec((None, 8, d), lambda i: (_stat_sid(i, nct), 0, 0))])
    return (*outs, rode)


def _rope_tables(c_len, l_len):
    half = HEAD_DIM // 2
    t = np.arange(l_len)
    row = (t // GRID_W).astype(np.float32)
    col = (t % GRID_W).astype(np.float32)
    inv = (np.float32(ROPE_BASE) ** (-np.arange(0, half, 2, dtype=np.float32) / np.float32(half))).astype(np.float32)
    ang_r = row[:, None] * inv[None, :]
    ang_c = col[:, None] * inv[None, :]
    cos = np.concatenate([np.cos(ang_r), np.cos(ang_r), np.cos(ang_c), np.cos(ang_c)], axis=1)
    sin = np.concatenate([-np.sin(ang_r), np.sin(ang_r), -np.sin(ang_c), np.sin(ang_c)], axis=1)
    cos = np.concatenate([np.ones((c_len, HEAD_DIM), np.float32), cos.astype(np.float32)], axis=0)
    sin = np.concatenate([np.zeros((c_len, HEAD_DIM), np.float32), sin.astype(np.float32)], axis=0)
    return jnp.asarray(cos, F32), jnp.asarray(sin, F32)


def _swap_pairs(x):
    lane = lax.broadcasted_iota(jnp.int32, x.shape, 1)
    return jnp.where((lane % 64) < 32, pltpu.roll(x, 96, 1), pltpu.roll(x, 32, 1))


def _head_norm(x, g):
    rstd = lax.rsqrt(jnp.mean(x * x, axis=-1, keepdims=True) + EPS)
    xhat = x * rstd
    return xhat * g, xhat, rstd


def _qkv_fwd(h, mods, ng, w, qg, kg, cos, sin, n_heads, n_kv, nct, tm):
    t_rows, d = h.shape
    qw, kw = n_heads * HEAD_DIM, n_kv * HEAD_DIM

    def body(h_ref, mod_ref, ng_ref, w_ref, qg_ref, kg_ref, cos_ref, sin_ref, q_ref, k_ref, v_ref, qt_ref):
        a = _normmod(h_ref[...], ng_ref[...], mod_ref[0:1, :], mod_ref[1:2, :])[0]
        qkv = _dot(a.astype(BF16), w_ref[...])
        cosv, sinv = cos_ref[...], sin_ref[...]
        ones = jnp.ones((tm, HEAD_DIM), BF16)
        for hd in range(n_heads + n_kv):
            cols = slice(hd * HEAD_DIM, (hd + 1) * HEAD_DIM)
            xn = _head_norm(qkv[:, cols], qg_ref[...] if hd < n_heads else kg_ref[...])[0]
            xr = xn * cosv + _swap_pairs(xn) * sinv
            if hd < n_heads:
                qs = xr * Q_SCALE
                q_ref[:, cols] = qs.astype(BF16)
                qt_ref[cols, :] = qs.T.astype(BF16)
            else:
                k_ref[:, (hd - n_heads) * HEAD_DIM:(hd - n_heads + 1) * HEAD_DIM] = xr.astype(BF16)
        for g in range(n_kv):
            v_ref[:, (2 * g) * HEAD_DIM:(2 * g + 1) * HEAD_DIM] = (
                qkv[:, qw + kw + g * HEAD_DIM:qw + kw + (g + 1) * HEAD_DIM].astype(BF16))
            v_ref[:, (2 * g + 1) * HEAD_DIM:(2 * g + 2) * HEAD_DIM] = ones

    return pl.pallas_call(
        body, name="qkv_fwd", grid=(t_rows // tm,),
        out_shape=[jax.ShapeDtypeStruct((t_rows - nct * tm, qw), BF16), jax.ShapeDtypeStruct((t_rows, kw), BF16),
                   jax.ShapeDtypeStruct((t_rows, 2 * kw), BF16), jax.ShapeDtypeStruct((qw, t_rows - nct * tm), BF16)],
        in_specs=[pl.BlockSpec((tm, d), lambda i: (i, 0)),
                  pl.BlockSpec((None, 8, d), lambda i: (_sid(i, nct), 0, 0)),
                  _const_spec((1, d)), _const_spec(w.shape), _const_spec((1, HEAD_DIM)),
                  _const_spec((1, HEAD_DIM)),
                  pl.BlockSpec((tm, HEAD_DIM), lambda i: (i, 0)), pl.BlockSpec((tm, HEAD_DIM), lambda i: (i, 0))],
        out_specs=[pl.BlockSpec((tm, qw), lambda i: (jnp.maximum(i - nct, 0), 0)),
                   pl.BlockSpec((tm, kw), lambda i: (i, 0)), pl.BlockSpec((tm, 2 * kw), lambda i: (i, 0)),
                   pl.BlockSpec((qw, tm), lambda i: (0, jnp.maximum(i - nct, 0)))],
        compiler_params=_params(("arbitrary",)),
    )(h, mods, ng, w, qg, kg, cos, sin)


def _flash_tk(t_rows, tm):
    best = tm
    k = tm
    while k <= FLASH_TK_CAP:
        if t_rows % k == 0:
            best = k
        k += tm
    return best


def _flash_fwd(q, k, v1, n_kv, tq, tm, rider=None):
    t_rows = k.shape[0]
    l_rows = q.shape[0]
    tk = _flash_tk(t_rows, tm)
    nk = t_rows // tk
    gq = 2 * HEAD_DIM

    def body(q_ref, k_ref, v_ref, o_ref, lse_ref, m_s, acc_s, s_s):
        ki = pl.program_id(2)

        @pl.when(ki == 0)
        def _():
            m_s[...] = jnp.full_like(m_s, -jnp.inf)
            acc_s[...] = jnp.zeros_like(acc_s)
        kk, vv = k_ref[...], v_ref[...]
        for hh in range(2):
            s_s[hh] = _dot_nt(q_ref[:, hh * HEAD_DIM:(hh + 1) * HEAD_DIM], kk)
        for hh in range(2):
            s = s_s[hh]
            m_prev = m_s[hh]
            m_new = jnp.maximum(m_prev, jnp.max(s, axis=-1, keepdims=True))
            alpha = jnp.exp2(m_prev - m_new)
            p = jnp.exp2(s - jnp.tile(m_new, (1, tk // HEAD_DIM)))
            acc_s[hh] = jnp.tile(alpha, (1, 2)) * acc_s[hh] + _dot(p.astype(BF16), vv)
            m_s[hh] = m_new

        @pl.when(ki == nk - 1)
        def _():
            for hh in range(2):
                acc = acc_s[hh]
                l = acc[:, HEAD_DIM:]
                o_ref[:, hh * HEAD_DIM:(hh + 1) * HEAD_DIM] = (acc[:, :HEAD_DIM] / l).astype(BF16)
                lse_ref[:, hh:hh + 1] = (m_s[hh] + jnp.log2(l))[:, 0:1]

    (o, lse), rode = _compute_call(
        body, name="flash_fwd", grid=(n_kv, l_rows // tq, nk), operands=(q, k, v1), rider=rider,
        out_shape=[jax.ShapeDtypeStruct((l_rows, n_kv * gq), BF16),
                   jax.ShapeDtypeStruct((n_kv, l_rows, 2), F32)],
        in_specs=[pl.BlockSpec((tq, gq), lambda g, i, j: (i, g)),
                  pl.BlockSpec((tk, HEAD_DIM), lambda g, i, j: (j, g)),
                  pl.BlockSpec((tk, gq), lambda g, i, j: (j, g))],
        out_specs=[pl.BlockSpec((tq, gq), lambda g, i, j: (i, g)),
                   pl.BlockSpec((None, tq, 2), lambda g, i, j: (g, i, 0))],
        scratch_shapes=[pltpu.VMEM((2, tq, HEAD_DIM), F32), pltpu.VMEM((2, tq, gq), F32),
                        pltpu.VMEM((2, tq, tk), F32)])
    return o, lse, rode


def _flash_bwd(q, qt, k, v1, do, dot, lse, delta, n_kv, tq, tm, rider=None):
    t_rows = k.shape[0]
    l_rows = q.shape[0]
    tk = _flash_tk(t_rows, tm)
    nq = l_rows // tq
    gq = 2 * HEAD_DIM

    def body(q_ref, qt_ref, k_ref, v_ref, do_ref, dot_ref, lse_ref, dl_ref, dq_ref, dkt_ref, dvt_ref):
        ki, qi = pl.program_id(1), pl.program_id(2)
        rows = pl.ds(pl.multiple_of(qi * tq, tq), tq)

        @pl.when(qi == 0)
        def _():
            dkt_ref[...] = jnp.zeros_like(dkt_ref)
            dvt_ref[...] = jnp.zeros_like(dvt_ref)

        @pl.when(ki == 0)
        def _():
            dq_ref[rows, :] = jnp.zeros((tq, gq), F32)
        kk, vv = k_ref[...], v_ref[:, :HEAD_DIM]
        dkt_parts, dvt_parts = [], []
        for hh in range(2):
            cols = slice(hh * HEAD_DIM, (hh + 1) * HEAD_DIM)
            p = jnp.exp2(_dot_nt(q_ref[:, cols], kk) - lse_ref[:, hh:hh + 1])
            ds = (p * (_dot_nt(do_ref[:, cols], vv) - dl_ref[:, hh:hh + 1])).astype(BF16)
            dvt_parts.append(_dot(dot_ref[cols, :], p.astype(BF16)))
            dkt_parts.append(_dot(qt_ref[cols, :], ds))
            dq_ref[rows, cols] += _dot(ds, kk)
        dvt_ref[...] += dvt_parts[0] + dvt_parts[1]
        dkt_ref[...] += dkt_parts[0] + dkt_parts[1]

    (dq, dkt, dvt), rode = _compute_call(
        body, name="flash_bwd", grid=(n_kv, t_rows // tk, nq), operands=(q, qt, k, v1, do, dot, lse, delta),
        rider=rider,
        out_shape=[jax.ShapeDtypeStruct((l_rows, n_kv * gq), F32),
                   jax.ShapeDtypeStruct((n_kv * HEAD_DIM, t_rows), F32),
                   jax.ShapeDtypeStruct((n_kv * HEAD_DIM, t_rows), F32)],
        in_specs=[pl.BlockSpec((tq, gq), lambda g, j, i: (i, g)),
                  pl.BlockSpec((gq, tq), lambda g, j, i: (g, i)),
                  pl.BlockSpec((tk, HEAD_DIM), lambda g, j, i: (j, g)),
                  pl.BlockSpec((tk, gq), lambda g, j, i: (j, g)),
                  pl.BlockSpec((tq, gq), lambda g, j, i: (i, g)),
                  pl.BlockSpec((gq, tq), lambda g, j, i: (g, i)),
                  pl.BlockSpec((None, tq, 2), lambda g, j, i: (g, i, 0)),
                  pl.BlockSpec((None, tq, 2), lambda g, j, i: (g, i, 0))],
        out_specs=[pl.BlockSpec((l_rows, gq), lambda g, j, i: (0, g)),
                   pl.BlockSpec((HEAD_DIM, tk), lambda g, j, i: (g, j)),
                   pl.BlockSpec((HEAD_DIM, tk), lambda g, j, i: (g, j))])
    return dq, dkt, dvt, rode


def _wo_fwd(h, o, wo, mods, nct, tm, rider=None):
    l_rows, z = o.shape
    d = h.shape[1]

    def body(h_ref, o_ref, w_ref, mod_ref, y_ref, h1_ref):
        y = _dot(o_ref[...], w_ref[...])
        y_ref[...] = y.astype(BF16)
        h1_ref[...] = h_ref[...] + mod_ref[2:3, :] * y

    (y, h1), rode = _compute_call(
        body, name="wo_fwd", grid=(l_rows // tm,), operands=(h, o, wo, mods), rider=rider,
        out_shape=[jax.ShapeDtypeStruct((l_rows, d), BF16), jax.ShapeDtypeStruct((l_rows, d), F32)],
        in_specs=[pl.BlockSpec((tm, d), lambda i: (i + nct, 0)), pl.BlockSpec((tm, z), lambda i: (i, 0)),
                  _const_spec(wo.shape), pl.BlockSpec((None, 8, d), lambda i: (1, 0, 0))],
        out_specs=[pl.BlockSpec((tm, d), lambda i: (i, 0)), pl.BlockSpec((tm, d), lambda i: (i, 0))])
    return y, h1, rode


def _wo_bwd(dh1, y, o, wo, mods, n_kv, tm):
    l_rows, z = o.shape
    d = dh1.shape[1]

    def body(dh_ref, y_ref, o_ref, w_ref, mod_ref, dy_ref, do_ref, dot_ref, dl_ref, st_ref):
        i = pl.program_id(0)
        dh = dh_ref[...]
        dgate = _colsum(dh * y_ref[...].astype(F32))
        dy = (mod_ref[2:3, :] * dh).astype(BF16)
        dy_ref[...] = dy
        do = _dot_nt(dy, w_ref[...])
        do_ref[...] = do.astype(BF16)
        dot_ref[...] = do.T.astype(BF16)
        prod = do * o_ref[...].astype(F32)
        for g in range(n_kv):
            d0 = jnp.sum(prod[:, (2 * g) * HEAD_DIM:(2 * g + 1) * HEAD_DIM], axis=-1, keepdims=True)
            d1 = jnp.sum(prod[:, (2 * g + 1) * HEAD_DIM:(2 * g + 2) * HEAD_DIM], axis=-1, keepdims=True)
            dl_ref[g] = jnp.concatenate([d0, d1], axis=1)
        zero = jnp.zeros((1, d), F32)
        _acc_rows(st_ref, i == 0, [zero, zero, dgate])

    return pl.pallas_call(
        body, name="wo_bwd", grid=(l_rows // tm,),
        out_shape=[jax.ShapeDtypeStruct((l_rows, d), BF16), jax.ShapeDtypeStruct((l_rows, z), BF16),
                   jax.ShapeDtypeStruct((z, l_rows), BF16),
                   jax.ShapeDtypeStruct((n_kv, l_rows, 2), F32), jax.ShapeDtypeStruct((8, d), F32)],
        in_specs=[pl.BlockSpec((tm, d), lambda i: (i, 0)), pl.BlockSpec((tm, d), lambda i: (i, 0)),
                  pl.BlockSpec((tm, z), lambda i: (i, 0)), _const_spec(wo.shape),
                  pl.BlockSpec((None, 8, d), lambda i: (1, 0, 0))],
        out_specs=[pl.BlockSpec((tm, d), lambda i: (i, 0)), pl.BlockSpec((tm, z), lambda i: (i, 0)),
                   pl.BlockSpec((z, tm), lambda i: (0, i)),
                   pl.BlockSpec((n_kv, tm, 2), lambda i: (0, i, 0)), pl.BlockSpec((8, d), lambda i: (0, 0))],
        compiler_params=_params(("arbitrary",)),
    )(dh1, y, o, wo, mods)


def _qkv_bwd(h, dh_lat, dq, dkt, dvt, mods, ng, w, qg, kg, cos, sin, n_heads, n_kv, nct, tm):
    t_rows, d = h.shape
    qw, kw = n_heads * HEAD_DIM, n_kv * HEAD_DIM
    scale = HEAD_DIM ** -0.5

    def body(h_ref, dhl_ref, dq_ref, dkt_ref, dvt_ref, mod_ref, ng_ref, w_ref, qg_ref, kg_ref, cos_ref,
             sin_ref, dh_ref, a_ref, dqkv_ref, st_ref, dg_ref):
        i = pl.program_id(0)
        lat = (i >= nct).astype(F32)
        dk_t = dkt_ref[...].T * (1.0 / LOG2E)
        ngv, sc = ng_ref[...], mod_ref[1:2, :]
        a, xhat, rstd, n = _normmod(h_ref[...], ngv, mod_ref[0:1, :], sc)
        ab = a.astype(BF16)
        a_ref[...] = ab
        qkv = _dot(ab, w_ref[...])
        cosv, sinv = cos_ref[...], sin_ref[...]
        dqg = jnp.zeros((1, HEAD_DIM), F32)
        dkg = jnp.zeros((1, HEAD_DIM), F32)
        for hd in range(n_heads + n_kv):
            cols = slice(hd * HEAD_DIM, (hd + 1) * HEAD_DIM)
            is_q = hd < n_heads
            g = qg_ref[...] if is_q else kg_ref[...]
            _, hx, hr = _head_norm(qkv[:, cols], g)
            if is_q:
                dxr = dq_ref[:, cols] * (scale * lat)
            else:
                dxr = dk_t[:, (hd - n_heads) * HEAD_DIM:(hd - n_heads + 1) * HEAD_DIM]
            dxn = dxr * cosv + _swap_pairs(dxr * sinv)
            if is_q:
                dqg = dqg + _colsum(dxn * hx)
            else:
                dkg = dkg + _colsum(dxn * hx)
            dxh = dxn * g
            dx = hr * (dxh - hx * jnp.mean(dxh * hx, axis=-1, keepdims=True))
            dqkv_ref[:, cols] = dx.astype(BF16)
        dqkv_ref[:, qw + kw:] = dvt_ref[...].T.astype(BF16)
        da = _dot_nt(dqkv_ref[...], w_ref[...])
        dhn, dsh, dsc, dng = _normmod_bwd(da, xhat, rstd, n, ngv, sc)
        dh_ref[...] = dhl_ref[...] * lat + dhn
        _acc_rows(st_ref, _first_of_stream(i, nct), [dsh, dsc, jnp.zeros((1, d), F32), dng])
        _acc_rows(dg_ref, i == 0, [dqg, dkg])

    lat_map = lambda i: (jnp.maximum(i - nct, 0), 0)
    return pl.pallas_call(
        body, name="qkv_bwd", grid=(t_rows // tm,),
        out_shape=[jax.ShapeDtypeStruct((t_rows, d), F32), jax.ShapeDtypeStruct((t_rows, d), BF16),
                   jax.ShapeDtypeStruct((t_rows, qw + 2 * kw), BF16), jax.ShapeDtypeStruct((2, 8, d), F32),
                   jax.ShapeDtypeStruct((8, HEAD_DIM), F32)],
        in_specs=[pl.BlockSpec((tm, d), lambda i: (i, 0)), pl.BlockSpec((tm, d), lat_map),
                  pl.BlockSpec((tm, qw), lat_map), pl.BlockSpec((kw, tm), lambda i: (0, i)),
                  pl.BlockSpec((kw, tm), lambda i: (0, i)),
                  pl.BlockSpec((None, 8, d), lambda i: (_sid(i, nct), 0, 0)),
                  _const_spec((1, d)), _const_spec(w.shape), _const_spec((1, HEAD_DIM)),
                  _const_spec((1, HEAD_DIM)),
                  pl.BlockSpec((tm, HEAD_DIM), lambda i: (i, 0)), pl.BlockSpec((tm, HEAD_DIM), lambda i: (i, 0))],
        out_specs=[pl.BlockSpec((tm, d), lambda i: (i, 0)), pl.BlockSpec((tm, d), lambda i: (i, 0)),
                   pl.BlockSpec((tm, qw + 2 * kw), lambda i: (i, 0)),
                   pl.BlockSpec((None, 8, d), lambda i: (_sid(i, nct), 0, 0)),
                   pl.BlockSpec((8, HEAD_DIM), lambda i: (0, 0))],
        compiler_params=_params(("arbitrary",)),
    )(h, dh_lat, dq, dkt, dvt, mods, ng, w, qg, kg, cos, sin)


def _gmlp_core(a_bf, win_ref, lng, lnb, ws_ref, bst_ref, tm, half, with_grad=False):
    blocks = [_dot(a_bf, win_ref[j]) for j in range(N_DEV)]
    zu = jnp.concatenate(blocks[:N_DEV // 2], axis=1)
    zv = jnp.concatenate(blocks[N_DEV // 2:], axis=1)
    if with_grad:
        (u, zu), (v, zv) = _gelu_and_grad(zu), _gelu_and_grad(zv)
    else:
        u, v, zu, zv = _gelu(zu), _gelu(zv), None, None
    mu = jnp.mean(v, axis=-1, keepdims=True)
    vc = v - mu
    rstd_v = lax.rsqrt(jnp.mean(vc * vc, axis=-1, keepdims=True) + EPS)
    vhat = vc * rstd_v
    vln = (vhat * lng + lnb).astype(BF16)
    gw = half // GMLP_GROUPS
    rows = []
    for ch in range(tm // CHUNK):
        rs = slice(ch * CHUNK, (ch + 1) * CHUNK)
        cols = []
        for g in range(GMLP_GROUPS):
            cs = slice(g * gw, (g + 1) * gw)
            cols.append(_dot(ws_ref[g], vln[rs, cs]) + bst_ref[:, g:g + 1])
        rows.append(jnp.concatenate(cols, axis=1))
    sv = rows[0] if len(rows) == 1 else jnp.concatenate(rows, axis=0)
    return zu, zv, u, vhat, rstd_v, vln, sv


def _gmlp_fwd(h, mods, ng, win, lng, lnb, ws, bst, wout, tm):
    l_rows, d = h.shape
    half = wout.shape[0]

    def body(h_ref, mod_ref, ng_ref, win_ref, lng_ref, lnb_ref, ws_ref, bst_ref, wout_ref, y_ref, h1_ref):
        hv = h_ref[...]
        a = _normmod(hv, ng_ref[...], mod_ref[0:1, :], mod_ref[1:2, :])[0]
        _, _, u, _, _, _, sv = _gmlp_core(a.astype(BF16), win_ref, lng_ref[...], lnb_ref[...], ws_ref,
                                          bst_ref, tm, half)
        y = _dot((u * sv).astype(BF16), wout_ref[...])
        y_ref[...] = y.astype(BF16)
        h1_ref[...] = hv + mod_ref[2:3, :] * y

    return pl.pallas_call(
        body, name="gmlp_fwd", grid=(l_rows // tm,),
        out_shape=[jax.ShapeDtypeStruct((l_rows, d), BF16), jax.ShapeDtypeStruct((l_rows, d), F32)],
        in_specs=[pl.BlockSpec((tm, d), lambda i: (i, 0)), pl.BlockSpec((None, 8, d), lambda i: (1, 0, 0)),
                  _const_spec((1, d)), _const_spec(win.shape), _const_spec((1, half)), _const_spec((1, half)),
                  _const_spec(ws.shape), _const_spec(bst.shape), _const_spec(wout.shape)],
        out_specs=[pl.BlockSpec((tm, d), lambda i: (i, 0)), pl.BlockSpec((tm, d), lambda i: (i, 0))],
        compiler_params=_params(("arbitrary",)),
    )(h, mods, ng, win, lng, lnb, ws, bst, wout)


def _gmlp_bwd(dh1, h, y, mods, ng, win, lng, lnb, ws, wst, bst, wout, tm):
    l_rows, d = h.shape
    half = wout.shape[0]
    gw = half // GMLP_GROUPS

    def body(dh_ref, h_ref, y_ref, mod_ref, ng_ref, win_ref, lng_ref, lnb_ref, ws_ref, wst_ref, bst_ref,
             wout_ref, dho_ref, a_ref, dz_ref, gt_ref, dy_ref, st_ref, ln_ref, dws_ref, dbs_ref):
        i = pl.program_id(0)
        ngv, sc = ng_ref[...], mod_ref[1:2, :]
        lngv = lng_ref[...]
        a, xhat, rstd, n = _normmod(h_ref[...], ngv, mod_ref[0:1, :], sc)
        ab = a.astype(BF16)
        a_ref[...] = ab
        gu, gv, u, vhat, rstd_v, vln, sv = _gmlp_core(ab, win_ref, lngv, lnb_ref[...], ws_ref, bst_ref,
                                                      tm, half, with_grad=True)
        gt_ref[...] = (u * sv).astype(BF16)
        dh = dh_ref[...]
        dgate = _colsum(dh * y_ref[...].astype(F32))
        dy = (mod_ref[2:3, :] * dh).astype(BF16)
        dy_ref[...] = dy
        dgated = _dot_nt(dy, wout_ref[...])
        du = dgated * sv
        dsv = (dgated * u).astype(BF16)

        @pl.when(i == 0)
        def _():
            dws_ref[...] = jnp.zeros_like(dws_ref)
            dbs_ref[...] = jnp.zeros_like(dbs_ref)
        lane = lax.broadcasted_iota(jnp.int32, (CHUNK, 128), 1)
        dbs = jnp.zeros((CHUNK, 128), F32)
        rows = []
        for ch in range(tm // CHUNK):
            rs = slice(ch * CHUNK, (ch + 1) * CHUNK)
            cols = []
            for g in range(GMLP_GROUPS):
                cs = slice(g * gw, (g + 1) * gw)
                dsv_cg = dsv[rs, cs]
                dws_ref[g] += _dot_nt(dsv_cg, vln[rs, cs])
                cols.append(_dot(wst_ref[g], dsv_cg))
                dbs = dbs + jnp.where(lane == g, jnp.sum(dsv_cg.astype(F32), axis=-1, keepdims=True), 0.0)
            rows.append(jnp.concatenate(cols, axis=1))
        dbs_ref[...] += dbs
        dvln = rows[0] if len(rows) == 1 else jnp.concatenate(rows, axis=0)
        dlng = _colsum(dvln * vhat)
        dlnb = _colsum(dvln)
        dvh = dvln * lngv
        dv = rstd_v * (dvh - jnp.mean(dvh, axis=-1, keepdims=True)
                       - vhat * jnp.mean(dvh * vhat, axis=-1, keepdims=True))
        dz_ref[:, :half] = (du * gu).astype(BF16)
        dz_ref[:, half:] = (dv * gv).astype(BF16)
        nb = 2 * half // N_DEV
        da = _dot_nt(dz_ref[:, 0:nb], win_ref[0])
        for j in range(1, N_DEV):
            da = da + _dot_nt(dz_ref[:, j * nb:(j + 1) * nb], win_ref[j])
        dhn, dsh, dsc, dng = _normmod_bwd(da, xhat, rstd, n, ngv, sc)
        dho_ref[...] = dh + dhn
        _acc_rows(st_ref, i == 0, [dsh, dsc, dgate, dng])
        _acc_rows(ln_ref, i == 0, [dlng, dlnb])

    row = lambda w: pl.BlockSpec((tm, w), lambda i: (i, 0))
    return pl.pallas_call(
        body, name="gmlp_bwd", grid=(l_rows // tm,),
        out_shape=[jax.ShapeDtypeStruct((l_rows, d), F32), jax.ShapeDtypeStruct((l_rows, d), BF16),
                   jax.ShapeDtypeStruct((l_rows, 2 * half), BF16), jax.ShapeDtypeStruct((l_rows, half), BF16),
                   jax.ShapeDtypeStruct((l_rows, d), BF16), jax.ShapeDtypeStruct((8, d), F32),
                   jax.ShapeDtypeStruct((8, half), F32), jax.ShapeDtypeStruct(ws.shape, F32),
                   jax.ShapeDtypeStruct((CHUNK, 128), F32)],
        in_specs=[row(d), row(d), row(d), pl.BlockSpec((None, 8, d), lambda i: (1, 0, 0)),
                  _const_spec((1, d)), _const_spec(win.shape), _const_spec((1, half)), _const_spec((1, half)),
                  _const_spec(ws.shape), _const_spec(ws.shape), _const_spec(bst.shape), _const_spec(wout.shape)],
        out_specs=[row(d), row(d), row(2 * half), row(half), row(d),
                   pl.BlockSpec((8, d), lambda i: (0, 0)), pl.BlockSpec((8, half), lambda i: (0, 0)),
                   pl.BlockSpec(ws.shape, lambda i: (0, 0, 0)), pl.BlockSpec((CHUNK, 128), lambda i: (0, 0))],
        compiler_params=_params(("arbitrary",)),
    )(dh1, h, y, mods, ng, win, lng, lnb, ws, wst, bst, wout)


def _head(h, final_g, target, tm):
    l_rows, d = h.shape
    n_tiles = l_rows // tm

    def body(h_ref, g_ref, t_ref, dh_ref, loss_ref, dg_ref, acc_ref):
        i = pl.program_id(0)
        g = g_ref[...]
        hv = h_ref[...]
        rstd = lax.rsqrt(jnp.mean(hv * hv, axis=-1, keepdims=True) + EPS)
        xhat = hv * rstd
        e = xhat * g - t_ref[...]
        dout = e * (1.0 / d)
        dxhat = dout * g
        dh_ref[...] = rstd * (dxhat - xhat * jnp.mean(dxhat * xhat, axis=-1, keepdims=True))
        _acc_rows(dg_ref, i == 0, [_colsum(dout * xhat)])
        _acc_rows(acc_ref, i == 0, [_colsum(e * e)])

        @pl.when(i == n_tiles - 1)
        def _():
            total = jnp.sum(acc_ref[0:1, :], axis=-1, keepdims=True) * (0.5 / d)
            loss_ref[...] = jnp.broadcast_to(total, loss_ref.shape)

    return pl.pallas_call(
        body, name="loss_head", grid=(n_tiles,),
        out_shape=[jax.ShapeDtypeStruct((l_rows, d), F32), jax.ShapeDtypeStruct((8, 128), F32),
                   jax.ShapeDtypeStruct((8, d), F32)],
        in_specs=[pl.BlockSpec((tm, d), lambda i: (i, 0)), _const_spec((1, d)),
                  pl.BlockSpec((tm, d), lambda i: (i, 0))],
        out_specs=[pl.BlockSpec((tm, d), lambda i: (i, 0)), pl.BlockSpec((8, 128), lambda i: (0, 0)),
                   pl.BlockSpec((8, d), lambda i: (0, 0))],
        scratch_shapes=[pltpu.VMEM((8, d), F32)],
        compiler_params=_params(("arbitrary",)),
    )(h, final_g, target)


def _adamw(w, gparts, m, v, name):
    shape = w.shape
    cols = shape[-1]
    rows = int(np.prod(shape[:-1])) if len(shape) > 1 else 1
    pieces = list(gparts) if isinstance(gparts, (list, tuple)) else [gparts]
    n_pieces = len(pieces)
    nparts = pieces[0].shape[0]
    piece_rows = rows // n_pieces
    w2, m2, v2 = (t.reshape(rows, cols) for t in (w, m, v))
    pieces = [g.reshape(nparts, piece_rows, cols) for g in pieces]
    tr = piece_rows
    part_bytes = nparts * cols * pieces[0].dtype.itemsize
    for cand in (1024, 512, 256, 128, 64, 32, 16, 8):
        if piece_rows * max(part_bytes, cols * 4) <= (2 << 20):
            break
        if piece_rows % cand == 0 and cand < piece_rows:
            tr = cand
            if cand * max(part_bytes, cols * 4) <= (2 << 20):
                break
    per_piece = piece_rows // tr
    c1 = 1.0 - ADAM_B1 ** ADAM_STEP
    c2 = 1.0 - ADAM_B2 ** ADAM_STEP

    def update(w_ref, g_ref, m_ref, v_ref, go_ref, d_ref, mo_ref, vo_ref):
        g = g_ref[0].astype(F32)
        for k in range(1, nparts):
            g = g + g_ref[k].astype(F32)
        mn = ADAM_B1 * m_ref[...] + (1.0 - ADAM_B1) * g
        vn = ADAM_B2 * v_ref[...] + (1.0 - ADAM_B2) * (g * g)
        go_ref[...] = g
        mo_ref[...] = mn
        vo_ref[...] = vn
        d_ref[...] = -ADAM_LR * ((mn / c1) / (jnp.sqrt(vn / c2) + ADAM_EPS) + ADAM_WD * w_ref[...])

    def body(w_ref, *refs):
        g_refs, (m_ref, v_ref, go_ref, d_ref, mo_ref, vo_ref) = refs[:n_pieces], refs[n_pieces:]
        if n_pieces == 1:
            update(w_ref, g_refs[0], m_ref, v_ref, go_ref, d_ref, mo_ref, vo_ref)
        else:
            piece = pl.program_id(0) // per_piece
            for k in range(n_pieces):
                pl.when(piece == k)(functools.partial(update, w_ref, g_refs[k], m_ref, v_ref, go_ref, d_ref, mo_ref, vo_ref))

    def piece_spec(k):
        return pl.BlockSpec((nparts, tr, cols), lambda i: (0, jnp.clip(i - k * per_piece, 0, per_piece - 1), 0))

    spec = pl.BlockSpec((tr, cols), lambda i: (i, 0))
    outs = pl.pallas_call(
        body, name=name, grid=(rows // tr,),
        out_shape=[jax.ShapeDtypeStruct((rows, cols), F32)] * 4,
        in_specs=[spec] + [piece_spec(k) for k in range(n_pieces)] + [spec, spec],
        out_specs=[spec] * 4,
        compiler_params=_params(("arbitrary",)),
    )(w2, *pieces, m2, v2)
    return tuple(o.reshape(shape) for o in outs)


def _natural_cols(g):
    return jnp.moveaxis(g, 0, -2).reshape(g.shape[1:-1] + (N_DEV * g.shape[-1],))


def _natural_rows(g):
    return jnp.moveaxis(g, 0, -3).reshape(g.shape[1:-2] + (N_DEV * g.shape[-2], g.shape[-1]))


def _shard_rows(full):
    r = full.shape[-2] // N_DEV
    return jnp.moveaxis(full.reshape(full.shape[:-2] + (N_DEV, r, full.shape[-1])), -3, 0)


def _my_cols(gathered, me, n):
    return lax.dynamic_slice_in_dim(gathered, me * n, n, axis=gathered.ndim - 1)


def kernel(x, c, ctx, c_ctx, ada_w, ada_b, norm_g, mlp_w1, mlp_w2, pool_w, pool_scale, attn_w_qkv, attn_w_o, attn_q_g, attn_k_g, gm_w_in, gm_ln_g, gm_ln_b, gm_ws, gm_bs, gm_w_out, final_g, loss_target, m_c_ctx, m_ada_w, m_ada_b, m_norm_g, m_mlp_w1, m_mlp_w2, m_pool_w, m_pool_scale, m_attn_w_qkv, m_attn_w_o, m_attn_q_g, m_attn_k_g, m_gm_w_in, m_gm_ln_g, m_gm_ln_b, m_gm_ws, m_gm_bs, m_gm_w_out, m_final_g, v_c_ctx, v_ada_w, v_ada_b, v_norm_g, v_mlp_w1, v_mlp_w2, v_pool_w, v_pool_scale, v_attn_w_qkv, v_attn_w_o, v_attn_q_g, v_attn_k_g, v_gm_w_in, v_gm_ln_g, v_gm_ln_b, v_gm_ws, v_gm_bs, v_gm_w_out, v_final_g):
    l_len, d = x.shape[1], x.shape[2]
    c_len = ctx.shape[1]
    n_layers = ada_w.shape[0]
    assert n_layers == 4 and x.shape[0] == 1
    n_heads = d // HEAD_DIM
    n_kv = n_heads // 2
    half = gm_w_out.shape[1] * N_DEV
    tm = c_len if c_len <= 256 else 256
    assert c_len % tm == 0 and l_len % tm == 0 and tm % CHUNK == 0 and l_len % GRID_W == 0
    nct = c_len // tm
    me = _dev_index(*_coords())
    n_ada = ada_w.shape[-1]

    first = [t.astype(BF16) for t in (mlp_w1[0], mlp_w2[0], pool_w, attn_w_qkv[0])]
    small = [c, norm_g.reshape(n_layers * 2, -1), pool_scale, gm_ln_g, gm_ln_b]
    w1_0g, w2_0g, pool_g, qkv_g, c_all, ng_g, ps_g, lng_g, lnb_g = _all_gather(first + small, "gather_first")
    c_all = c_all.reshape(N_DEV, d)
    later = _GatherAcrossChips([t.astype(BF16) for t in
                                (mlp_w1[1], mlp_w1[2], mlp_w1[3], mlp_w2[1], mlp_w2[2], mlp_w2[3],
                                 attn_w_o[0], gm_w_in[0], gm_w_out[0])])
    pool_wf = _natural_rows(pool_g)
    wqkv = _natural_cols(qkv_g)
    ng_full = _natural_cols(ng_g.reshape(N_DEV, n_layers * 2, 1, -1)).reshape(n_layers, 2, 1, d)
    ps_full = _natural_cols(ps_g.reshape(N_DEV, 2, 1, -1))
    lng_full = _natural_cols(lng_g.reshape(N_DEV, 1, -1))
    lnb_full = _natural_cols(lnb_g.reshape(N_DEV, 1, -1))

    c_ctx2 = c_ctx.reshape(1, d)
    ada_b_loc = lax.dynamic_slice_in_dim(ada_b, me * n_ada, n_ada, axis=1).reshape(n_layers, 1, n_ada)
    (mod_g,) = _all_gather([_mods_local(c_all, c_ctx2, ada_w, ada_b_loc)], "gather_mods")
    mod_full = jnp.moveaxis(mod_g, 0, 2).reshape(n_layers, 16, 6, d)
    mod_lat = lax.dynamic_index_in_dim(mod_full, me, axis=1, keepdims=False)
    mod_ctx = mod_full[:, 8]
    mods = jnp.stack([mod_ctx, mod_lat], axis=1)
    mods = jnp.concatenate([mods, jnp.zeros((n_layers, 2, 2, d), F32)], axis=2)

    bands = _pool_bands(tm)
    cos, sin = _rope_tables(c_len, l_len)
    ws_bf = gm_ws[0].astype(BF16)
    wst_bf = jnp.swapaxes(gm_ws[0], 1, 2).astype(BF16)
    bst = jnp.zeros((CHUNK, 128), F32).at[:, :GMLP_GROUPS].set(gm_bs[0].T)
    ng = lambda i, j: ng_full[i, j]

    h0 = jnp.concatenate([ctx[0], x[0]], axis=0)
    y0, h1 = _pool_fwd(h0, mods[0], ng(0, 0), pool_wf[0].astype(BF16), ps_full[0], bands, nct, tm, c_len, l_len)
    h2, p0, ym0 = _mlp_fwd(h1, mods[0], ng(0, 1), w1_0g, w2_0g, nct, tm)
    q, k, v1, qt = _qkv_fwd(h2, mods[1], ng(1, 0), wqkv, attn_q_g, attn_k_g, cos, sin, n_heads, n_kv, nct, tm)
    o, lse, later_g = _flash_fwd(q, k, v1, n_kv, 4 * tm, tm, rider=later)
    (wo_g,) = _exchange_call(_ForwardToSibling(later_g[6:7]), "forward_wo")
    wo = _natural_rows(wo_g)
    y1, h3, rest_g = _wo_fwd(h2, o, wo, mods[1], nct, tm, rider=_ForwardToSibling(later_g[0:6] + later_g[7:9]))
    w1 = [w1_0g] + rest_g[0:3]
    w2 = [w2_0g] + rest_g[3:6]
    win = rest_g[6]
    wout = _natural_rows(rest_g[7])
    tm_lat = 2 * tm
    h4, p1, ym1 = _mlp_fwd(h3, mods[1], ng(1, 1), w1[1], w2[1], 0, tm_lat)
    y2, h5 = _gmlp_fwd(h4, mods[2], ng(2, 0), win, lng_full, lnb_full, ws_bf, bst, wout, tm)
    h6, p2, ym2 = _mlp_fwd(h5, mods[2], ng(2, 1), w1[2], w2[2], 0, tm_lat)
    y3, h7 = _pool_fwd(h6, mods[3], ng(3, 0), pool_wf[1].astype(BF16), ps_full[1], bands, 0, tm, c_len, l_len)
    h8, p3, ym3 = _mlp_fwd(h7, mods[3], ng(3, 1), w1[3], w2[3], 0, tm_lat)
    dh, loss_part, dfinal = _head(h8, final_g.reshape(1, d), loss_target[0], tm_lat)

    dw1, dw2, st_mlp = [None] * 4, [None] * 4, [None] * 4

    def mlp_back(i, dh, h_in, p, ym, nct_i):
        dh_in, m_bf, du, dacc, st, _ = _mlp_bwd(dh, h_in, p, ym, mods[i], ng(i, 1), w1[i], w2[i], nct_i, tm_lat)
        dw1[i] = _tn_matmul(m_bf, du, "tn_w1", col_shards=True)
        dw2[i] = _shard_rows(_tn_matmul(p, dacc, "tn_w2", square_x=True))
        st_mlp[i] = st
        return dh_in

    dh = mlp_back(3, dh, h7, p3, ym3, 0)
    dh, dpw1, st_pool3, _ = _pool_bwd(dh, h6, y3, mods[3], ng(3, 0), pool_wf[1].astype(BF16), ps_full[1], bands,
                                      0, tm, c_len, l_len, False)
    dh = mlp_back(2, dh, h5, p2, ym2, 0)
    dh, a_bf, dz, gated, dy, st_g, st_ln, dws, dbst = _gmlp_bwd(
        dh, h4, y2, mods[2], ng(2, 0), win, lng_full, lnb_full, ws_bf, wst_bf, bst, wout, tm)
    dwin = _tn_matmul(a_bf, dz, "tn_gm_in", col_shards=True)
    dwout = _tn_matmul(gated, dy, "tn_gm_out")
    dh = mlp_back(1, dh, h3, p1, ym1, 0)
    dy1, do, dot, delta, st_wo = _wo_bwd(dh, y1, o, wo, mods[1], n_kv, tm)
    dwo = _tn_matmul(o, dy1, "tn_wo")
    grads_mid = _AllToAll(dw1[1:] + dw2[1:] + [_shard_rows(dpw1.astype(BF16)), _shard_rows(dwo), dwin,
                                               _shard_rows(dwout)])
    dq, dkt, dvt, rode = _flash_bwd(q, qt, k, v1, do, dot, lse, delta, n_kv, 4 * tm, tm, rider=grads_mid)
    g_w1, g_w2, (g_pool1, g_wo, g_gin, g_gout) = [None] + rode[0:3], [None] + rode[3:6], rode[6:]
    dh, a_bf, dqkv, st_q, dgains = _qkv_bwd(h2, dh, dq, dkt, dvt, mods[1], ng(1, 0), wqkv, attn_q_g, attn_k_g,
                                            cos, sin, n_heads, n_kv, nct, tm)
    dwqkv = _tn_matmul(a_bf, dqkv, "tn_qkv", col_shards=True)
    dh, m_bf, du, dacc, st_mlp[0], (g_qkv,) = _mlp_bwd(dh, h1, p0, ym0, mods[0], ng(0, 1), w1[0], w2[0], nct, tm,
                                                       rider=_AllToAll([dwqkv]))
    dw1_0 = _tn_matmul(m_bf, du, "tn_w1", col_shards=True)
    dw2_0, (g_w1[0],) = _tn_matmul(p0, dacc, "tn_w2", square_x=True, rider=_AllToAll([dw1_0]))
    grad_x, dpw0, st_pool0, (g_w2[0],) = _pool_bwd(dh, h0, y0, mods[0], ng(0, 0), pool_wf[0].astype(BF16),
                                                   ps_full[0], bands, nct, tm, c_len, l_len, True,
                                                   rider=_AllToAll([_shard_rows(dw2_0)]))

    mix_lat = [st_pool0[-1], st_q[1] + st_wo, st_g, st_pool3[-1]]
    mlp_lat = [st[-1] for st in st_mlp]
    dmod_lat = jnp.stack([jnp.concatenate([mix_lat[i][0:3], mlp_lat[i][0:3]]) for i in range(n_layers)])
    dmod_ctx = jnp.stack([jnp.concatenate([st_pool0[0][0:3], st_mlp[0][0][0:3]]),
                          jnp.concatenate([st_q[0][0:2], jnp.zeros((4, d), F32)]),
                          jnp.zeros((6, d), F32), jnp.zeros((6, d), F32)])
    dng_part = jnp.stack([jnp.stack([mix_lat[0][3] + st_pool0[0][3], mlp_lat[0][3] + st_mlp[0][0][3]]),
                          jnp.stack([mix_lat[1][3] + st_q[0][3], mlp_lat[1][3]]),
                          jnp.stack([mix_lat[2][3], mlp_lat[2][3]]),
                          jnp.stack([mix_lat[3][3], mlp_lat[3][3]])])
    dps_part = jnp.stack([mix_lat[0][4] + st_pool0[0][4], mix_lat[3][4]])
    small_parts = [dmod_lat.reshape(n_layers * 6, d), dmod_ctx.reshape(n_layers * 6, d),
                   dng_part.reshape(n_layers * 2, d), dps_part, st_ln, dgains, dws.reshape(-1, CHUNK),
                   dbst, dfinal, loss_part]
    (gm_lat, gm_ctx, g_ng, g_ps, g_ln, g_gains, g_ws, g_bst, g_final, loss_all, g_pool0) = _all_gather(
        small_parts, "gather_small_grads", extra=_AllToAll([_shard_rows(dpw0.astype(BF16))]))
    g_pool = jnp.stack([g_pool0, g_pool1], axis=1)

    gm_lat4 = gm_lat.reshape(N_DEV, n_layers, 6 * d)
    gm_ctx4 = gm_ctx.reshape(N_DEV, n_layers, 6 * d)
    dm_lat_loc = jnp.moveaxis(_my_cols(gm_lat4, me, n_ada), 0, 1)
    dm_ctx_loc = jnp.moveaxis(_my_cols(gm_ctx4, me, n_ada), 0, 1)
    g_ada_w, ds_part = _ada_grads(c_all, c_ctx2, ada_w, dm_lat_loc, dm_ctx_loc)
    (ds_all,) = _all_gather([ds_part], "gather_dsctx")
    g_c_ctx, loss_sum = _cctx_grad_and_loss(ds_all, c_ctx2, loss_all)
    g_c_ctx = g_c_ctx.reshape(d)

    n_ng = norm_g.shape[-1]
    n_ps = pool_scale.shape[-1]
    n_ln = gm_ln_g.shape[-1]
    gparts = {
        "c_ctx": g_c_ctx[None],
        "ada_w": g_ada_w[None],
        "ada_b": jnp.concatenate([gm_lat4, gm_ctx4], axis=0),
        "norm_g": _my_cols(g_ng.reshape(N_DEV, n_layers, 2, d), me, n_ng),
        "mlp_w1": g_w1, "mlp_w2": g_w2, "pool_w": g_pool,
        "pool_scale": _my_cols(g_ps, me, n_ps),
        "attn_w_qkv": g_qkv[:, None], "attn_w_o": g_wo[:, None],
        "attn_q_g": g_gains[:, 0:1], "attn_k_g": g_gains[:, 1:2],
        "gm_w_in": g_gin[:, None],
        "gm_ln_g": _my_cols(g_ln[:, 0:1], me, n_ln), "gm_ln_b": _my_cols(g_ln[:, 1:2], me, n_ln),
        "gm_ws": g_ws.reshape((N_DEV,) + gm_ws.shape),
        "gm_bs": jnp.swapaxes(g_bst[:, :, :GMLP_GROUPS], 1, 2)[:, None],
        "gm_w_out": g_gout[:, None],
        "final_g": g_final[:, 0],
    }
    weights = dict(c_ctx=(c_ctx, m_c_ctx, v_c_ctx), ada_w=(ada_w, m_ada_w, v_ada_w), ada_b=(ada_b, m_ada_b, v_ada_b),
                   norm_g=(norm_g, m_norm_g, v_norm_g), mlp_w1=(mlp_w1, m_mlp_w1, v_mlp_w1),
                   mlp_w2=(mlp_w2, m_mlp_w2, v_mlp_w2), pool_w=(pool_w, m_pool_w, v_pool_w),
                   pool_scale=(pool_scale, m_pool_scale, v_pool_scale),
                   attn_w_qkv=(attn_w_qkv, m_attn_w_qkv, v_attn_w_qkv), attn_w_o=(attn_w_o, m_attn_w_o, v_attn_w_o),
                   attn_q_g=(attn_q_g, m_attn_q_g, v_attn_q_g), attn_k_g=(attn_k_g, m_attn_k_g, v_attn_k_g),
                   gm_w_in=(gm_w_in, m_gm_w_in, v_gm_w_in), gm_ln_g=(gm_ln_g, m_gm_ln_g, v_gm_ln_g),
                   gm_ln_b=(gm_ln_b, m_gm_ln_b, v_gm_ln_b), gm_ws=(gm_ws, m_gm_ws, v_gm_ws),
                   gm_bs=(gm_bs, m_gm_bs, v_gm_bs), gm_w_out=(gm_w_out, m_gm_w_out, v_gm_w_out),
                   final_g=(final_g, m_final_g, v_final_g))
    grads, deltas, new_m, new_v = [], [], [], []
    for wname, (w_, m_, v_) in weights.items():
        g_, d_, nm_, nv_ = _adamw(w_, gparts[wname], m_, v_, "adamw_" + wname)
        grads.append(g_)
        deltas.append(d_)
        new_m.append(nm_)
        new_v.append(nv_)

    return (loss_sum[0, 0], grad_x[None], *grads, *deltas, *new_m, *new_v)
```

```python
import functools
import math

import numpy as np
import jax
import jax.numpy as jnp
from jax import lax
from jax.experimental import pallas as pl
from jax.experimental.pallas import tpu as pltpu

F32 = jnp.float32
BF16 = jnp.bfloat16
MESH_ID = pl.DeviceIdType.MESH

N_DEV = 8
EPS = 1e-6
HEAD_DIM = 128
GRID_W = 64
ROPE_BASE = 10000.0
CHUNK = 128
POOL_WINDOWS = (2, 4, 8, 16)
POOL_GROUPS = 4
POOL_HALO = 8
GMLP_GROUPS = 8
ADAM_LR, ADAM_B1, ADAM_B2, ADAM_EPS, ADAM_WD, ADAM_STEP = 0.001, 0.9, 0.999, 1e-08, 0.01, 10

V7X_VMEM_BYTES = 64 << 20
VMEM_LIMIT_BIG = V7X_VMEM_BYTES - (8 << 20)
FLASH_TK_CAP = 768
LOG2E = math.log2(math.e)
Q_SCALE = HEAD_DIM ** -0.5 * LOG2E


def _params(sem, vmem=VMEM_LIMIT_BIG):
    return pltpu.CompilerParams(dimension_semantics=sem, vmem_limit_bytes=vmem)


def _const_spec(shape):
    nd = len(shape)
    return pl.BlockSpec(shape, lambda *_: (0,) * nd, pipeline_mode=pl.Buffered(1))


def _dot(a, b):
    return jnp.dot(a, b, preferred_element_type=F32)


def _dot_nt(a, b):
    return lax.dot_general(a, b, (((1,), (1,)), ((), ())), preferred_element_type=F32)


def _dot_tn(a, b):
    return lax.dot_general(a, b, (((0,), (0,)), ((), ())), preferred_element_type=F32)


def _colsum(x):
    return jnp.sum(x, axis=0, keepdims=True)


def _sid(i, nct):
    if nct == 0:
        return 1
    return jnp.where(i >= nct, 1, 0)


def _n_streams(nct):
    return 2 if nct else 1


def _stat_sid(i, nct):
    return _sid(i, nct) if nct else 0


def _first_of_stream(i, nct):
    if nct == 0:
        return i == 0
    return jnp.logical_or(i == 0, i == nct)


def _normmod(h, ng, sh, sc):
    rstd = lax.rsqrt(jnp.mean(h * h, axis=-1, keepdims=True) + EPS)
    xhat = h * rstd
    n = xhat * ng
    return n * (1.0 + sc) + sh, xhat, rstd, n


def _normmod_bwd(da, xhat, rstd, n, ng, sc):
    dsh = _colsum(da)
    dsc = _colsum(da * n)
    dn = da * (1.0 + sc)
    dng = _colsum(dn * xhat)
    dxhat = dn * ng
    dh = rstd * (dxhat - xhat * jnp.mean(dxhat * xhat, axis=-1, keepdims=True))
    return dh, dsh, dsc, dng


def _acc_rows(ref, first, rows):
    @pl.when(first)
    def _():
        ref[...] = jnp.zeros_like(ref)
    for r, val in enumerate(rows):
        ref[r:r + 1, :] = ref[r:r + 1, :] + val


_GELU_C = math.sqrt(2.0 / math.pi)


def _gelu(x):
    t = jnp.tanh((_GELU_C * x) * (1.0 + 0.044715 * (x * x)))
    hx = 0.5 * x
    return hx + hx * t


def _gelu_and_grad(x):
    x2 = x * x
    t = jnp.tanh((_GELU_C * x) * (1.0 + 0.044715 * x2))
    hx = 0.5 * x
    g = hx + hx * t
    dg = (0.5 + 0.5 * t) + (hx * (1.0 - t * t)) * (_GELU_C + (3.0 * 0.044715 * _GELU_C) * x2)
    return g, dg


def _coords():
    return lax.axis_index("x"), lax.axis_index("y"), lax.axis_index("c")


def _dev_index(px, py, pc):
    return 4 * px + 2 * py + pc


def _all_gather(xs, name, extra=None):
    n = len(xs)
    e_in, e_in_specs, e_out, e_out_specs, e_scratch = _rider_parts(extra)
    ne = len(e_in)

    def body(*refs):
        x_refs, e_x = refs[:n], refs[n:n + ne]
        o_refs, e_o = refs[n + ne:2 * n + ne], refs[2 * n + ne:2 * n + 2 * ne]
        send_sems, recv_sems, local_sems = refs[2 * n + 2 * ne:2 * n + 2 * ne + 3]
        e_sems = refs[2 * n + 2 * ne + 3:]
        if extra is not None:
            extra.start(e_x, e_o, e_sems)
        x, y, c = _coords()
        me, sibling = (x, y, c), (x, y, 1 - c)
        chips = [(1 - x, y), (x, 1 - y), (1 - x, 1 - y)]

        def copy(a, k, block, to, src=None):
            dst = o_refs[a].at[_dev_index(*block)]
            return pltpu.make_async_remote_copy(
                src_ref=dst if src is None else src, dst_ref=dst,
                send_sem=send_sems.at[7 * a + k], recv_sem=recv_sems.at[7 * a + k],
                device_id=to, device_id_type=MESH_ID)

        mine = [pltpu.make_async_copy(x_refs[a], o_refs[a].at[_dev_index(*me)], local_sems.at[a])
                for a in range(n)]
        for cp in mine:
            cp.start()
        first = []
        for a in range(n):
            first.append(copy(a, 0, me, sibling, src=x_refs[a]))
            first += [copy(a, 1 + j, me, (*chip, c), src=x_refs[a]) for j, chip in enumerate(chips)]
        for cp in first:
            cp.start()
        passed = []
        for a in range(n):
            for j, chip in enumerate(chips):
                copy(a, 1 + j, (*chip, c), me).wait_recv()
                fwd = copy(a, 4 + j, (*chip, c), sibling)
                fwd.start()
                passed.append(fwd)
        for a in range(n):
            copy(a, 0, sibling, me).wait_recv()
            for j, chip in enumerate(chips):
                copy(a, 4 + j, (*chip, 1 - c), me).wait_recv()
        for cp in first + passed:
            cp.wait_send()
        for cp in mine:
            cp.wait()
        if extra is not None:
            extra.finish(e_x, e_o, e_sems)

    any_spec = pl.BlockSpec(memory_space=pl.ANY)
    outs = pl.pallas_call(
        body, name=name,
        out_shape=[jax.ShapeDtypeStruct((N_DEV,) + x.shape, x.dtype) for x in xs] + e_out,
        in_specs=[any_spec] * n + e_in_specs, out_specs=[any_spec] * n + e_out_specs,
        scratch_shapes=[pltpu.SemaphoreType.DMA((7 * n,)), pltpu.SemaphoreType.DMA((7 * n,)),
                        pltpu.SemaphoreType.DMA((n,))] + e_scratch,
    )(*xs, *e_in)
    return list(outs)


class _Exchange:
    per_array = 0
    in_place = False

    def __init__(self, xs):
        self.xs = list(xs)
        n = len(self.xs)
        self.out_shapes = self._out_shapes()
        self.scratch = [pltpu.SemaphoreType.DMA((self.per_array * n,)),
                        pltpu.SemaphoreType.DMA((self.per_array * n,)),
                        pltpu.SemaphoreType.DMA((n,))]

    def _out_shapes(self):
        raise NotImplementedError

    def _copies(self, x_refs, o_refs, sems):
        raise NotImplementedError

    def start(self, x_refs, o_refs, sems):
        mine, sends, _ = self._copies(x_refs, o_refs, sems)
        for cp in mine + sends:
            cp.start()

    def finish(self, x_refs, o_refs, sems):
        mine, sends, arrivals = self._copies(x_refs, o_refs, sems)
        for make in arrivals:
            make().wait_recv()
        for cp in sends:
            cp.wait_send()
        for cp in mine:
            cp.wait()


def _remote(src, dst, sems, k, to):
    return pltpu.make_async_remote_copy(src_ref=src, dst_ref=dst, send_sem=sems[0].at[k], recv_sem=sems[1].at[k],
                                        device_id=to, device_id_type=MESH_ID)


class _GatherAcrossChips(_Exchange):
    per_array = 4

    def _out_shapes(self):
        return [jax.ShapeDtypeStruct((N_DEV,) + x.shape, x.dtype) for x in self.xs]

    def _copies(self, x_refs, o_refs, sems):
        x, y, c = _coords()
        targets = [(x, y, 1 - c), (1 - x, y, c), (x, 1 - y, c), (1 - x, 1 - y, c)]
        mine, sends, arrivals = [], [], []
        for a, (x_ref, o_ref) in enumerate(zip(x_refs, o_refs)):
            own = o_ref.at[_dev_index(x, y, c)]
            mine.append(pltpu.make_async_copy(x_ref, own, sems[2].at[a]))
            for k, to in enumerate(targets):
                sends.append(_remote(x_ref, own, sems, 4 * a + k, to))
                arrivals.append(functools.partial(_remote, x_ref, o_ref.at[_dev_index(*to)], sems, 4 * a + k, to))
        return mine, sends, arrivals


class _ForwardToSibling(_Exchange):
    per_array = 3
    in_place = True

    def _out_shapes(self):
        return [jax.ShapeDtypeStruct(x.shape, x.dtype) for x in self.xs]

    def _copies(self, x_refs, o_refs, sems):
        x, y, c = _coords()
        chips = [(1 - x, y), (x, 1 - y), (1 - x, 1 - y)]
        sends, arrivals = [], []
        for a, (x_ref, o_ref) in enumerate(zip(x_refs, o_refs)):
            for j, chip in enumerate(chips):
                held = _dev_index(*chip, c)
                sends.append(_remote(x_ref.at[held], o_ref.at[held], sems, 3 * a + j, (x, y, 1 - c)))
                theirs = _dev_index(*chip, 1 - c)
                arrivals.append(functools.partial(_remote, x_ref.at[theirs], o_ref.at[theirs], sems, 3 * a + j,
                                                  (x, y, 1 - c)))
        return [], sends, arrivals


class _AllToAll(_Exchange):
    per_array = 7

    def _out_shapes(self):
        return [jax.ShapeDtypeStruct(x.shape, x.dtype) for x in self.xs]

    def _copies(self, x_refs, o_refs, sems):
        x, y, c = _coords()
        me_i = _dev_index(x, y, c)
        mine, sends, arrivals = [], [], []
        for a, (x_ref, o_ref) in enumerate(zip(x_refs, o_refs)):
            mine.append(pltpu.make_async_copy(x_ref.at[me_i], o_ref.at[me_i], sems[2].at[a]))
            for r in range(1, 8):
                to = (1 - x if r & 4 else x, 1 - y if r & 2 else y, 1 - c if r & 1 else c)
                to_i = _dev_index(*to)
                sends.append(_remote(x_ref.at[to_i], o_ref.at[me_i], sems, 7 * a + r - 1, to))
                arrivals.append(functools.partial(_remote, x_ref.at[to_i], o_ref.at[to_i], sems, 7 * a + r - 1, to))
        return mine, sends, arrivals


class _Both(_Exchange):
    def __init__(self, first, second):
        self.parts = (first, second)
        self.xs = first.xs + second.xs
        self.out_shapes = first.out_shapes + second.out_shapes
        self.scratch = first.scratch + second.scratch

    def _each(self, x_refs, o_refs, sems):
        n = len(self.parts[0].xs)
        k = len(self.parts[0].scratch)
        return ((self.parts[0], x_refs[:n], o_refs[:n], sems[:k]),
                (self.parts[1], x_refs[n:], o_refs[n:], sems[k:]))

    def start(self, x_refs, o_refs, sems):
        for ex, xr, orf, sm in self._each(x_refs, o_refs, sems):
            ex.start(xr, orf, sm)

    def finish(self, x_refs, o_refs, sems):
        for ex, xr, orf, sm in self._each(x_refs, o_refs, sems):
            ex.finish(xr, orf, sm)


def _exchange_call(ex, name):
    n = len(ex.xs)

    def body(*refs):
        x_refs, o_refs, sems = refs[:n], refs[n:2 * n], refs[2 * n:]
        ex.start(x_refs, o_refs, sems)
        ex.finish(x_refs, o_refs, sems)

    any_spec = pl.BlockSpec(memory_space=pl.ANY)
    outs = pl.pallas_call(
        body, name=name, out_shape=ex.out_shapes, in_specs=[any_spec] * n, out_specs=[any_spec] * n,
        scratch_shapes=ex.scratch, input_output_aliases={a: a for a in range(n)} if ex.in_place else {},
    )(*ex.xs)
    return list(outs)


def _rider_parts(rider):
    if rider is None:
        return [], [], [], [], []
    any_spec = pl.BlockSpec(memory_space=pl.ANY)
    n = len(rider.xs)
    return rider.xs, [any_spec] * n, rider.out_shapes, [any_spec] * n, rider.scratch


def _compute_call(body, *, name, grid, in_specs, out_specs, out_shape, operands, scratch_shapes=(), rider=None):
    in_specs, out_specs, out_shape, scratch_shapes = list(in_specs), list(out_specs), list(out_shape), list(scratch_shapes)
    r_in, r_in_specs, r_out, r_out_specs, r_scratch = _rider_parts(rider)
    n_in, n_out, n_scr, nr = len(operands), len(out_shape), len(scratch_shapes), len(r_in)

    def riding_body(*refs):
        ins, refs = refs[:n_in], refs[n_in:]
        r_x, refs = refs[:nr], refs[nr:]
        outs, refs = refs[:n_out], refs[n_out:]
        r_o, refs = refs[:nr], refs[nr:]
        scratch, r_sems = refs[:n_scr], refs[n_scr:]
        if rider is not None:
            first, last = _grid_ends(grid)
            pl.when(first)(lambda: rider.start(r_x, r_o, r_sems))
        body(*ins, *outs, *scratch)
        if rider is not None:
            pl.when(last)(lambda: rider.finish(r_x, r_o, r_sems))

    in_place = rider is not None and rider.in_place
    res = pl.pallas_call(
        riding_body, name=name, grid=grid, out_shape=out_shape + r_out,
        in_specs=in_specs + r_in_specs, out_specs=out_specs + r_out_specs,
        scratch_shapes=scratch_shapes + r_scratch,
        input_output_aliases={n_in + a: n_out + a for a in range(nr)} if in_place else {},
        compiler_params=_params(("arbitrary",) * len(grid)),
    )(*operands, *r_in)
    return list(res[:n_out]), list(res[n_out:])


def _grid_ends(grid):
    first = pl.program_id(0) == 0
    last = pl.program_id(0) == grid[0] - 1
    for ax in range(1, len(grid)):
        first = jnp.logical_and(first, pl.program_id(ax) == 0)
        last = jnp.logical_and(last, pl.program_id(ax) == grid[ax] - 1)
    return first, last


def _silu(x):
    return x * (1.0 / (1.0 + jnp.exp(-x)))


def _cond_rows(c_all, c_ctx):
    d = c_all.shape[-1]
    s = jnp.concatenate([c_all, jnp.zeros((8, d), F32)], axis=0)
    row = lax.broadcasted_iota(jnp.int32, (16, d), 0)
    s = jnp.where(row == 8, c_ctx, s)
    return jnp.where(row <= 8, _silu(s), 0.0)


def _mods_local(c_all, c_ctx, ada_w, ada_b_loc):
    nl, d, n = ada_w.shape

    def body(c_ref, cc_ref, w_ref, b_ref, o_ref):
        s = _cond_rows(c_ref[...], cc_ref[...])
        o_ref[...] = jnp.dot(s, w_ref[...], preferred_element_type=F32,
                             precision=lax.Precision.HIGHEST) + b_ref[...]

    return pl.pallas_call(
        body, name="mods_local", grid=(nl,),
        out_shape=jax.ShapeDtypeStruct((nl, 16, n), F32),
        in_specs=[pl.BlockSpec((8, d), lambda i: (0, 0)), pl.BlockSpec((1, d), lambda i: (0, 0)),
                  pl.BlockSpec((None, d, n), lambda i: (i, 0, 0)),
                  pl.BlockSpec((None, 1, n), lambda i: (i, 0, 0))],
        out_specs=pl.BlockSpec((None, 16, n), lambda i: (i, 0, 0)),
        compiler_params=_params(("arbitrary",)),
    )(c_all, c_ctx, ada_w, ada_b_loc)


def _ada_grads(c_all, c_ctx, ada_w, dm_lat, dm_ctx):
    nl, d, n = ada_w.shape

    def body(c_ref, cc_ref, w_ref, dml_ref, dmc_ref, gw_ref, ds_ref):
        i = pl.program_id(0)
        s = _cond_rows(c_ref[...], cc_ref[...])
        csum = dmc_ref[0:1, :]
        for k in range(1, N_DEV):
            csum = csum + dmc_ref[k:k + 1, :]
        row = lax.broadcasted_iota(jnp.int32, (8, n), 0)
        dm_c = jnp.where(row == 0, csum, 0.0)
        dm = jnp.concatenate([dml_ref[...], dm_c], axis=0)
        gw_ref[...] = lax.dot_general(s, dm, (((0,), (0,)), ((), ())), preferred_element_type=F32,
                                      precision=lax.Precision.HIGHEST)
        ds = lax.dot_general(dm_c, w_ref[...], (((1,), (1,)), ((), ())),
                             preferred_element_type=F32, precision=lax.Precision.HIGHEST)

        @pl.when(i == 0)
        def _():
            ds_ref[...] = jnp.zeros_like(ds_ref)
        ds_ref[...] += ds

    return pl.pallas_call(
        body, name="ada_grads", grid=(nl,),
        out_shape=[jax.ShapeDtypeStruct((nl, d, n), F32), jax.ShapeDtypeStruct((8, d), F32)],
        in_specs=[pl.BlockSpec((8, d), lambda i: (0, 0)), pl.BlockSpec((1, d), lambda i: (0, 0)),
                  pl.BlockSpec((None, d, n), lambda i: (i, 0, 0)),
                  pl.BlockSpec((None, 8, n), lambda i: (i, 0, 0)),
                  pl.BlockSpec((None, 8, n), lambda i: (i, 0, 0))],
        out_specs=[pl.BlockSpec((None, d, n), lambda i: (i, 0, 0)),
                   pl.BlockSpec((8, d), lambda i: (0, 0))],
        compiler_params=_params(("arbitrary",)),
    )(c_all, c_ctx, ada_w, dm_lat, dm_ctx)


def _cctx_grad_and_loss(ds_parts, c_ctx, loss_parts):
    d = c_ctx.shape[-1]

    def body(p_ref, c_ref, l_ref, o_ref, lo_ref):
        ds, loss = p_ref[0], l_ref[0]
        for k in range(1, N_DEV):
            ds = ds + p_ref[k]
            loss = loss + l_ref[k]
        x = c_ref[...]
        sg = 1.0 / (1.0 + jnp.exp(-x))
        o_ref[...] = ds[0:1, :] * (sg * (1.0 + x * (1.0 - sg)))
        lo_ref[...] = loss

    return pl.pallas_call(body, name="cctx_grad", out_shape=[jax.ShapeDtypeStruct((1, d), F32),
                                                             jax.ShapeDtypeStruct((8, 128), F32)])(ds_parts, c_ctx, loss_parts)


def _mlp_fwd(h1, mods, ng, w1, w2, nct, tm, row_off=0):
    d = h1.shape[1]
    r = h1.shape[0] - row_off * tm
    fc = w1.shape[2]
    f = N_DEV * fc

    def body(h_ref, mod_ref, ng_ref, w1_ref, w2_ref, h2_ref, p_ref, y_ref):
        h = h_ref[...]
        a, _, _, _ = _normmod(h, ng_ref[...], mod_ref[3:4, :], mod_ref[4:5, :])
        ab = a.astype(BF16)
        acc = jnp.zeros((tm, d), F32)
        for j in range(N_DEV):
            sl = slice(j * fc, (j + 1) * fc)
            p = jnp.maximum(_dot(ab, w1_ref[j]), 0.0)
            p_ref[:, sl] = p.astype(BF16)
            acc = acc + _dot((p * p).astype(BF16), w2_ref[j])
        y_ref[...] = acc.astype(BF16)
        h2_ref[...] = h + mod_ref[5:6, :] * acc

    return pl.pallas_call(
        body, name="mlp_fwd", grid=(r // tm,),
        out_shape=[jax.ShapeDtypeStruct((r, d), F32), jax.ShapeDtypeStruct((r, f), BF16),
                   jax.ShapeDtypeStruct((r, d), BF16)],
        in_specs=[pl.BlockSpec((tm, d), lambda i: (i + row_off, 0)),
                  pl.BlockSpec((None, 8, d), lambda i: (_sid(i, nct), 0, 0)),
                  _const_spec((1, d)), _const_spec(w1.shape), _const_spec(w2.shape)],
        out_specs=[pl.BlockSpec((tm, d), lambda i: (i, 0)), pl.BlockSpec((tm, f), lambda i: (i, 0)),
                   pl.BlockSpec((tm, d), lambda i: (i, 0))],
        compiler_params=_params(("arbitrary",)),
    )(h1, mods, ng, w1, w2)


def _mlp_bwd(dh2, h1, p, y, mods, ng, w1, w2, nct, tm, row_off=0, rider=None):
    r_rows, d = dh2.shape
    fc = w1.shape[2]
    f = N_DEV * fc

    def body(dh_ref, h_ref, p_ref, y_ref, mod_ref, ng_ref, w1_ref, w2_ref,
             dh1_ref, m_ref, du_ref, dacc_ref, st_ref):
        i = pl.program_id(0)
        dh = dh_ref[...]
        ngv, sc, gate = ng_ref[...], mod_ref[4:5, :], mod_ref[5:6, :]
        a, xhat, rstd, n = _normmod(h_ref[...], ngv, mod_ref[3:4, :], sc)
        m_ref[...] = a.astype(BF16)
        dgate = _colsum(dh * y_ref[...].astype(F32))
        dacc = (gate * dh).astype(BF16)
        dacc_ref[...] = dacc
        dm = jnp.zeros((tm, d), F32)
        for j in range(N_DEV):
            sl = slice(j * fc, (j + 1) * fc)
            pj = p_ref[:, sl].astype(F32)
            du = (_dot_nt(dacc, w2_ref[j]) * (2.0 * pj)).astype(BF16)
            du_ref[:, sl] = du
            dm = dm + _dot_nt(du, w1_ref[j])
        dhn, dsh, dsc, dng = _normmod_bwd(dm, xhat, rstd, n, ngv, sc)
        dh1_ref[...] = dh + dhn
        _acc_rows(st_ref, _first_of_stream(i, nct), [dsh, dsc, dgate, dng])

    outs, rode = _compute_call(
        body, name="mlp_bwd", grid=(r_rows // tm,), operands=(dh2, h1, p, y, mods, ng, w1, w2), rider=rider,
        out_shape=[jax.ShapeDtypeStruct((r_rows, d), F32), jax.ShapeDtypeStruct((r_rows, d), BF16),
                   jax.ShapeDtypeStruct((r_rows, f), BF16),
                   jax.ShapeDtypeStruct((r_rows, d), BF16), jax.ShapeDtypeStruct((_n_streams(nct), 8, d), F32)],
        in_specs=[pl.BlockSpec((tm, d), lambda i: (i, 0)),
                  pl.BlockSpec((tm, d), lambda i: (i + row_off, 0)),
                  pl.BlockSpec((tm, f), lambda i: (i, 0)), pl.BlockSpec((tm, d), lambda i: (i, 0)),
                  pl.BlockSpec((None, 8, d), lambda i: (_sid(i, nct), 0, 0)),
                  _const_spec((1, d)), _const_spec(w1.shape), _const_spec(w2.shape)],
        out_specs=[pl.BlockSpec((tm, d), lambda i: (i, 0)), pl.BlockSpec((tm, d), lambda i: (i, 0)),
                   pl.BlockSpec((tm, f), lambda i: (i, 0)),
                   pl.BlockSpec((tm, d), lambda i: (i, 0)),
                   pl.BlockSpec((None, 8, d), lambda i: (_stat_sid(i, nct), 0, 0))])
    return (*outs, rode)


def _pick(n, cands):
    for cand in cands:
        if n % cand == 0:
            return cand
    return n


def _tn_matmul(x, y, name, col_shards=False, square_x=False, rider=None):
    rows, k1 = x.shape
    k2 = y.shape[1]
    bt = _pick(rows, (1024, 768, 512, 384, 256, 128))
    bk1, bk2 = min(k1, 1024), min(k2, 1024)
    grid = (k1 // bk1, k2 // bk2, rows // bt)
    nt = rows // bt
    n = k2 // N_DEV
    if col_shards:
        assert bk2 % n == 0
        per = bk2 // n
        out_shape = jax.ShapeDtypeStruct((N_DEV, k1, n), BF16)
        out_spec = pl.BlockSpec((per, bk1, n), lambda i, j, t: (j, i, 0))
    else:
        out_shape = jax.ShapeDtypeStruct((k1, k2), BF16)
        out_spec = pl.BlockSpec((bk1, bk2), lambda i, j, t: (i, j))

    def body(x_ref, y_ref, o_ref, acc_ref):
        t = pl.program_id(2)

        @pl.when(t == 0)
        def _():
            acc_ref[...] = jnp.zeros_like(acc_ref)
        xv = x_ref[...]
        acc_ref[...] += _dot_tn(xv * xv if square_x else xv, y_ref[...])

        @pl.when(t == nt - 1)
        def _():
            if col_shards:
                for s in range(per):
                    o_ref[s] = acc_ref[:, s * n:(s + 1) * n].astype(BF16)
            else:
                o_ref[...] = acc_ref[...].astype(BF16)

    (out,), rode = _compute_call(
        body, name=name, grid=grid, operands=(x, y), rider=rider, out_shape=[out_shape],
        in_specs=[pl.BlockSpec((bt, bk1), lambda i, j, t: (t, i)), pl.BlockSpec((bt, bk2), lambda i, j, t: (t, j))],
        out_specs=[out_spec], scratch_shapes=[pltpu.VMEM((bk1, bk2), F32)])
    return out if rider is None else (out, rode)


def _pool_bands(tm):
    k = tm + 128
    t = np.arange(tm)[:, None]
    e = np.arange(k)[None, :]
    fwd, bwd = [], []
    for w in POOL_WINDOWS:
        lo = POOL_HALO + t - w // 2
        fwd.append(((e >= lo) & (e <= lo + w - 1)).astype(np.float32))
        lo_t = POOL_HALO + t - w // 2 + 1
        bwd.append(((e >= lo_t) & (e <= lo_t + w - 1)).astype(np.float32))
    return jnp.asarray(np.stack(fwd), BF16), jnp.asarray(np.stack(bwd), BF16)


def _pool_geometry(i, nct, n_tiles, tm, c_len, l_len):
    if nct == 0:
        pos0 = i * tm
        ls = l_len
        has_prev = i > 0
        has_next = i < n_tiles - 1
    else:
        in_ctx = i < nct
        pos0 = jnp.where(in_ctx, i, i - nct) * tm
        ls = jnp.where(in_ctx, c_len, l_len)
        has_prev = jnp.logical_and(i != 0, i != nct)
        has_next = jnp.logical_and(i != nct - 1, i != n_tiles - 1)
    return pos0, ls, has_prev, has_next


def _window_inv_counts(pos, ls):
    out = []
    for w in POOL_WINDOWS:
        lo = jnp.maximum(pos - w // 2, 0)
        hi = jnp.minimum(pos + w - w // 2, ls)
        cnt = jnp.maximum(hi - lo, 1).astype(F32)
        out.append(1.0 / cnt)
    return out


def _split_bf16(x):
    hi = x.astype(BF16)
    return hi, (x - hi.astype(F32)).astype(BF16)


def _extend(prev, tile, nxt, has_prev, has_next):
    w = tile.shape[1]
    prev = jnp.where(has_prev, prev, 0.0)
    nxt = jnp.where(has_next, nxt, 0.0)
    return jnp.concatenate([prev, tile, nxt, jnp.zeros((128 - 2 * POOL_HALO, w), F32)], axis=0)


def _pool_specs(tm, d, n_rows):
    last8 = n_rows // POOL_HALO - 1
    per = tm // POOL_HALO
    return [pl.BlockSpec((tm, d), lambda i: (i, 0)),
            pl.BlockSpec((POOL_HALO, d), lambda i: (jnp.maximum(i * per - 1, 0), 0)),
            pl.BlockSpec((POOL_HALO, d), lambda i: (jnp.minimum((i + 1) * per, last8), 0))]


def _pool_fwd(h, mods, ng, w, scale, bands, nct, tm, c_len, l_len):
    r, d = h.shape
    gw = d // POOL_GROUPS
    n_tiles = r // tm
    kx = tm + 128

    def body(h_ref, hp_ref, hn_ref, mod_ref, ng_ref, w_ref, sc_ref, band_ref, y_ref, h1_ref):
        i = pl.program_id(0)
        pos0, ls, has_prev, has_next = _pool_geometry(i, nct, n_tiles, tm, c_len, l_len)
        ngv, sh, sc = ng_ref[...], mod_ref[0:1, :], mod_ref[1:2, :]
        h = h_ref[...]
        a = _normmod(h, ngv, sh, sc)[0]
        a_ext = _extend(_normmod(hp_ref[...], ngv, sh, sc)[0], a, _normmod(hn_ref[...], ngv, sh, sc)[0],
                        has_prev, has_next)
        pos = pos0 + lax.broadcasted_iota(jnp.int32, (tm, 1), 0)
        inv = _window_inv_counts(pos, ls)
        ys = []
        for g in range(POOL_GROUPS):
            cols = slice(g * gw, (g + 1) * gw)
            hi, lo = _split_bf16(a_ext[:, cols])
            s = _dot(band_ref[g], hi) + _dot(band_ref[g], lo)
            pg = s * inv[g] - a[:, cols]
            ys.append(_dot(pg.astype(BF16), w_ref[g]))
        y = jnp.concatenate(ys, axis=1) * sc_ref[...]
        y_ref[...] = y.astype(BF16)
        h1_ref[...] = h + mod_ref[2:3, :] * y

    return pl.pallas_call(
        body, name="pool_fwd", grid=(n_tiles,),
        out_shape=[jax.ShapeDtypeStruct((r, d), BF16), jax.ShapeDtypeStruct((r, d), F32)],
        in_specs=_pool_specs(tm, d, r) + [
            pl.BlockSpec((None, 8, d), lambda i: (_sid(i, nct), 0, 0)),
            _const_spec((1, d)), _const_spec(w.shape), _const_spec((1, d)), _const_spec((4, tm, kx))],
        out_specs=[pl.BlockSpec((tm, d), lambda i: (i, 0)), pl.BlockSpec((tm, d), lambda i: (i, 0))],
        compiler_params=_params(("arbitrary",)),
    )(h, h, h, mods, ng, w, scale, bands[0])


def _pool_bwd(dh1, h, y, mods, ng, w, scale, bands, nct, tm, c_len, l_len, latent_out, rider=None):
    r, d = h.shape
    gw = d // POOL_GROUPS
    n_tiles = r // tm
    kx = tm + 128
    out_rows = l_len if latent_out else r
    out_off = nct if latent_out else 0

    def body(dh_ref, dhp_ref, dhn_ref, h_ref, hp_ref, hn_ref, y_ref, mod_ref, ng_ref, w_ref, sc_ref,
             bf_ref, bb_ref, dho_ref, dw_ref, st_ref):
        i = pl.program_id(0)
        pos0, ls, has_prev, has_next = _pool_geometry(i, nct, n_tiles, tm, c_len, l_len)
        ngv, sh, sc, gate = ng_ref[...], mod_ref[0:1, :], mod_ref[1:2, :], mod_ref[2:3, :]
        scale_v = sc_ref[...]
        h = h_ref[...]
        a, xhat, rstd, n = _normmod(h, ngv, sh, sc)
        a_ext = _extend(_normmod(hp_ref[...], ngv, sh, sc)[0], a, _normmod(hn_ref[...], ngv, sh, sc)[0],
                        has_prev, has_next)
        dh = dh_ref[...]
        dgate = _colsum(dh * y_ref[...].astype(F32))
        dy = gate * dh
        dy_ext = _extend(gate * dhp_ref[...], dy, gate * dhn_ref[...], has_prev, has_next)
        dyp_ext = (dy_ext * scale_v).astype(BF16)
        dyp = (dy * scale_v).astype(BF16)
        pos = pos0 + lax.broadcasted_iota(jnp.int32, (tm, 1), 0)
        inv = _window_inv_counts(pos, ls)
        pos_e = pos0 - POOL_HALO + lax.broadcasted_iota(jnp.int32, (kx, 1), 0)
        inv_e = _window_inv_counts(pos_e, ls)

        @pl.when(i == 0)
        def _():
            dw_ref[...] = jnp.zeros_like(dw_ref)
        das, dscale = [], []
        for g in range(POOL_GROUPS):
            cols = slice(g * gw, (g + 1) * gw)
            hi, lo = _split_bf16(a_ext[:, cols])
            pg = ((_dot(bf_ref[g], hi) + _dot(bf_ref[g], lo)) * inv[g] - a[:, cols]).astype(BF16)
            dscale.append(_colsum(dy[:, cols] * _dot(pg, w_ref[g])))
            dyp_g = dyp_ext[:, cols]
            dw_ref[g] += _dot_tn(pg, dyp[:, cols])
            dp_ext = _dot_nt(dyp_g, w_ref[g])
            hi, lo = _split_bf16(dp_ext * inv_e[g])
            das.append(_dot(bb_ref[g], hi) + _dot(bb_ref[g], lo) - dp_ext[POOL_HALO:POOL_HALO + tm, :])
        da = jnp.concatenate(das, axis=1)
        dhn, dsh, dsc, dng = _normmod_bwd(da, xhat, rstd, n, ngv, sc)
        dho_ref[...] = dh + dhn
        _acc_rows(st_ref, _first_of_stream(i, nct), [dsh, dsc, dgate, dng, jnp.concatenate(dscale, axis=1)])

    outs, rode = _compute_call(
        body, name="pool_bwd", grid=(n_tiles,), rider=rider,
        operands=(dh1, dh1, dh1, h, h, h, y, mods, ng, w, scale, bands[0], bands[1]),
        out_shape=[jax.ShapeDtypeStruct((out_rows, d), F32),
                   jax.ShapeDtypeStruct((POOL_GROUPS, gw, gw), F32),
                   jax.ShapeDtypeStruct((_n_streams(nct), 8, d), F32)],
        in_specs=_pool_specs(tm, d, r) + _pool_specs(tm, d, r) + [
            pl.BlockSpec((tm, d), lambda i: (i, 0)),
            pl.BlockSpec((None, 8, d), lambda i: (_sid(i, nct), 0, 0)),
            _const_spec((1, d)), _const_spec(w.shape), _const_spec((1, d)),
            _const_spec((4, tm, kx)), _const_spec((4, tm, kx))],
        out_specs=[pl.BlockSpec((tm, d), lambda i: (jnp.maximum(i - out_off, 0), 0)),
                   pl.BlockSpec((POOL_GROUPS, gw, gw), lambda i: (0, 0, 0)),
                   pl.BlockSpec((None, 8, d), lambda i: (_stat_sid(i, nct), 0, 0))])
    return (*outs, rode)


def _rope_tables(c_len, l_len):
    half = HEAD_DIM // 2
    t = np.arange(l_len)
    row = (t // GRID_W).astype(np.float32)
    col = (t % GRID_W).astype(np.float32)
    inv = (np.float32(ROPE_BASE) ** (-np.arange(0, half, 2, dtype=np.float32) / np.float32(half))).astype(np.float32)
    ang_r = row[:, None] * inv[None, :]
    ang_c = col[:, None] * inv[None, :]
    cos = np.concatenate([np.cos(ang_r), np.cos(ang_r), np.cos(ang_c), np.cos(ang_c)], axis=1)
    sin = np.concatenate([-np.sin(ang_r), np.sin(ang_r), -np.sin(ang_c), np.sin(ang_c)], axis=1)
    cos = np.concatenate([np.ones((c_len, HEAD_DIM), np.float32), cos.astype(np.float32)], axis=0)
    sin = np.concatenate([np.zeros((c_len, HEAD_DIM), np.float32), sin.astype(np.float32)], axis=0)
    return jnp.asarray(cos, F32), jnp.asarray(sin, F32)


def _swap_pairs(x):
    lane = lax.broadcasted_iota(jnp.int32, x.shape, 1)
    return jnp.where((lane % 64) < 32, pltpu.roll(x, 96, 1), pltpu.roll(x, 32, 1))


def _head_norm(x, g):
    rstd = lax.rsqrt(jnp.mean(x * x, axis=-1, keepdims=True) + EPS)
    xhat = x * rstd
    return xhat * g, xhat, rstd


def _qkv_fwd(h, mods, ng, w, qg, kg, cos, sin, n_heads, n_kv, nct, tm):
    t_rows, d = h.shape
    qw, kw = n_heads * HEAD_DIM, n_kv * HEAD_DIM

    def body(h_ref, mod_ref, ng_ref, w_ref, qg_ref, kg_ref, cos_ref, sin_ref, q_ref, k_ref, v_ref, qt_ref):
        a = _normmod(h_ref[...], ng_ref[...], mod_ref[0:1, :], mod_ref[1:2, :])[0]
        qkv = _dot(a.astype(BF16), w_ref[...])
        cosv, sinv = cos_ref[...], sin_ref[...]
        ones = jnp.ones((tm, HEAD_DIM), BF16)
        for hd in range(n_heads + n_kv):
            cols = slice(hd * HEAD_DIM, (hd + 1) * HEAD_DIM)
            xn = _head_norm(qkv[:, cols], qg_ref[...] if hd < n_heads else kg_ref[...])[0]
            xr = xn * cosv + _swap_pairs(xn) * sinv
            if hd < n_heads:
                qs = xr * Q_SCALE
                q_ref[:, cols] = qs.astype(BF16)
                qt_ref[cols, :] = qs.T.astype(BF16)
            else:
                k_ref[:, (hd - n_heads) * HEAD_DIM:(hd - n_heads + 1) * HEAD_DIM] = xr.astype(BF16)
        for g in range(n_kv):
            v_ref[:, (2 * g) * HEAD_DIM:(2 * g + 1) * HEAD_DIM] = (
                qkv[:, qw + kw + g * HEAD_DIM:qw + kw + (g + 1) * HEAD_DIM].astype(BF16))
            v_ref[:, (2 * g + 1) * HEAD_DIM:(2 * g + 2) * HEAD_DIM] = ones

    return pl.pallas_call(
        body, name="qkv_fwd", grid=(t_rows // tm,),
        out_shape=[jax.ShapeDtypeStruct((t_rows - nct * tm, qw), BF16), jax.ShapeDtypeStruct((t_rows, kw), BF16),
                   jax.ShapeDtypeStruct((t_rows, 2 * kw), BF16), jax.ShapeDtypeStruct((qw, t_rows - nct * tm), BF16)],
        in_specs=[pl.BlockSpec((tm, d), lambda i: (i, 0)),
                  pl.BlockSpec((None, 8, d), lambda i: (_sid(i, nct), 0, 0)),
                  _const_spec((1, d)), _const_spec(w.shape), _const_spec((1, HEAD_DIM)),
                  _const_spec((1, HEAD_DIM)),
                  pl.BlockSpec((tm, HEAD_DIM), lambda i: (i, 0)), pl.BlockSpec((tm, HEAD_DIM), lambda i: (i, 0))],
        out_specs=[pl.BlockSpec((tm, qw), lambda i: (jnp.maximum(i - nct, 0), 0)),
                   pl.BlockSpec((tm, kw), lambda i: (i, 0)), pl.BlockSpec((tm, 2 * kw), lambda i: (i, 0)),
                   pl.BlockSpec((qw, tm), lambda i: (0, jnp.maximum(i - nct, 0)))],
        compiler_params=_params(("arbitrary",)),
    )(h, mods, ng, w, qg, kg, cos, sin)


def _flash_tk(t_rows, tm):
    best = tm
    k = tm
    while k <= FLASH_TK_CAP:
        if t_rows % k == 0:
            best = k
        k += tm
    return best


def _flash_fwd(q, k, v1, n_kv, tq, tm, rider=None):
    t_rows = k.shape[0]
    l_rows = q.shape[0]
    tk = _flash_tk(t_rows, tm)
    nk = t_rows // tk
    gq = 2 * HEAD_DIM

    def body(q_ref, k_ref, v_ref, o_ref, lse_ref, m_s, acc_s, s_s):
        ki = pl.program_id(2)

        @pl.when(ki == 0)
        def _():
            m_s[...] = jnp.full_like(m_s, -jnp.inf)
            acc_s[...] = jnp.zeros_like(acc_s)
        kk, vv = k_ref[...], v_ref[...]
        for hh in range(2):
            s_s[hh] = _dot_nt(q_ref[:, hh * HEAD_DIM:(hh + 1) * HEAD_DIM], kk)
        for hh in range(2):
            s = s_s[hh]
            m_prev = m_s[hh]
            m_new = jnp.maximum(m_prev, jnp.max(s, axis=-1, keepdims=True))
            alpha = jnp.exp2(m_prev - m_new)
            p = jnp.exp2(s - jnp.tile(m_new, (1, tk // HEAD_DIM)))
            acc_s[hh] = jnp.tile(alpha, (1, 2)) * acc_s[hh] + _dot(p.astype(BF16), vv)
            m_s[hh] = m_new

        @pl.when(ki == nk - 1)
        def _():
            for hh in range(2):
                acc = acc_s[hh]
                l = acc[:, HEAD_DIM:]
                o_ref[:, hh * HEAD_DIM:(hh + 1) * HEAD_DIM] = (acc[:, :HEAD_DIM] / l).astype(BF16)
                lse_ref[:, hh:hh + 1] = (m_s[hh] + jnp.log2(l))[:, 0:1]

    (o, lse), rode = _compute_call(
        body, name="flash_fwd", grid=(n_kv, l_rows // tq, nk), operands=(q, k, v1), rider=rider,
        out_shape=[jax.ShapeDtypeStruct((l_rows, n_kv * gq), BF16),
                   jax.ShapeDtypeStruct((n_kv, l_rows, 2), F32)],
        in_specs=[pl.BlockSpec((tq, gq), lambda g, i, j: (i, g)),
                  pl.BlockSpec((tk, HEAD_DIM), lambda g, i, j: (j, g)),
                  pl.BlockSpec((tk, gq), lambda g, i, j: (j, g))],
        out_specs=[pl.BlockSpec((tq, gq), lambda g, i, j: (i, g)),
                   pl.BlockSpec((None, tq, 2), lambda g, i, j: (g, i, 0))],
        scratch_shapes=[pltpu.VMEM((2, tq, HEAD_DIM), F32), pltpu.VMEM((2, tq, gq), F32),
                        pltpu.VMEM((2, tq, tk), F32)])
    return o, lse, rode


def _flash_bwd(q, qt, k, v1, do, dot, lse, delta, n_kv, tq, tm, rider=None):
    t_rows = k.shape[0]
    l_rows = q.shape[0]
    tk = _flash_tk(t_rows, tm)
    nq = l_rows // tq
    gq = 2 * HEAD_DIM

    def body(q_ref, qt_ref, k_ref, v_ref, do_ref, dot_ref, lse_ref, dl_ref, dq_ref, dkt_ref, dvt_ref):
        ki, qi = pl.program_id(1), pl.program_id(2)
        rows = pl.ds(pl.multiple_of(qi * tq, tq), tq)

        @pl.when(qi == 0)
        def _():
            dkt_ref[...] = jnp.zeros_like(dkt_ref)
            dvt_ref[...] = jnp.zeros_like(dvt_ref)

        @pl.when(ki == 0)
        def _():
            dq_ref[rows, :] = jnp.zeros((tq, gq), F32)
        kk, vv = k_ref[...], v_ref[:, :HEAD_DIM]
        dkt_parts, dvt_parts = [], []
        for hh in range(2):
            cols = slice(hh * HEAD_DIM, (hh + 1) * HEAD_DIM)
            p = jnp.exp2(_dot_nt(q_ref[:, cols], kk) - lse_ref[:, hh:hh + 1])
            ds = (p * (_dot_nt(do_ref[:, cols], vv) - dl_ref[:, hh:hh + 1])).astype(BF16)
            dvt_parts.append(_dot(dot_ref[cols, :], p.astype(BF16)))
            dkt_parts.append(_dot(qt_ref[cols, :], ds))
            dq_ref[rows, cols] += _dot(ds, kk)
        dvt_ref[...] += dvt_parts[0] + dvt_parts[1]
        dkt_ref[...] += dkt_parts[0] + dkt_parts[1]

    (dq, dkt, dvt), rode = _compute_call(
        body, name="flash_bwd", grid=(n_kv, t_rows // tk, nq), operands=(q, qt, k, v1, do, dot, lse, delta),
        rider=rider,
        out_shape=[jax.ShapeDtypeStruct((l_rows, n_kv * gq), F32),
                   jax.ShapeDtypeStruct((n_kv * HEAD_DIM, t_rows), F32),
                   jax.ShapeDtypeStruct((n_kv * HEAD_DIM, t_rows), F32)],
        in_specs=[pl.BlockSpec((tq, gq), lambda g, j, i: (i, g)),
                  pl.BlockSpec((gq, tq), lambda g, j, i: (g, i)),
                  pl.BlockSpec((tk, HEAD_DIM), lambda g, j, i: (j, g)),
                  pl.BlockSpec((tk, gq), lambda g, j, i: (j, g)),
                  pl.BlockSpec((tq, gq), lambda g, j, i: (i, g)),
                  pl.BlockSpec((gq, tq), lambda g, j, i: (g, i)),
                  pl.BlockSpec((None, tq, 2), lambda g, j, i: (g, i, 0)),
                  pl.BlockSpec((None, tq, 2), lambda g, j, i: (g, i, 0))],
        out_specs=[pl.BlockSpec((l_rows, gq), lambda g, j, i: (0, g)),
                   pl.BlockSpec((HEAD_DIM, tk), lambda g, j, i: (g, j)),
                   pl.BlockSpec((HEAD_DIM, tk), lambda g, j, i: (g, j))])
    return dq, dkt, dvt, rode


def _wo_fwd(h, o, wo, mods, nct, tm, rider=None):
    l_rows, z = o.shape
    d = h.shape[1]

    def body(h_ref, o_ref, w_ref, mod_ref, y_ref, h1_ref):
        y = _dot(o_ref[...], w_ref[...])
        y_ref[...] = y.astype(BF16)
        h1_ref[...] = h_ref[...] + mod_ref[2:3, :] * y

    (y, h1), rode = _compute_call(
        body, name="wo_fwd", grid=(l_rows // tm,), operands=(h, o, wo, mods), rider=rider,
        out_shape=[jax.ShapeDtypeStruct((l_rows, d), BF16), jax.ShapeDtypeStruct((l_rows, d), F32)],
        in_specs=[pl.BlockSpec((tm, d), lambda i: (i + nct, 0)), pl.BlockSpec((tm, z), lambda i: (i, 0)),
                  _const_spec(wo.shape), pl.BlockSpec((None, 8, d), lambda i: (1, 0, 0))],
        out_specs=[pl.BlockSpec((tm, d), lambda i: (i, 0)), pl.BlockSpec((tm, d), lambda i: (i, 0))])
    return y, h1, rode


def _wo_bwd(dh1, y, o, wo, mods, n_kv, tm):
    l_rows, z = o.shape
    d = dh1.shape[1]

    def body(dh_ref, y_ref, o_ref, w_ref, mod_ref, dy_ref, do_ref, dot_ref, dl_ref, st_ref):
        i = pl.program_id(0)
        dh = dh_ref[...]
        dgate = _colsum(dh * y_ref[...].astype(F32))
        dy = (mod_ref[2:3, :] * dh).astype(BF16)
        dy_ref[...] = dy
        do = _dot_nt(dy, w_ref[...])
        do_ref[...] = do.astype(BF16)
        dot_ref[...] = do.T.astype(BF16)
        prod = do * o_ref[...].astype(F32)
        for g in range(n_kv):
            d0 = jnp.sum(prod[:, (2 * g) * HEAD_DIM:(2 * g + 1) * HEAD_DIM], axis=-1, keepdims=True)
            d1 = jnp.sum(prod[:, (2 * g + 1) * HEAD_DIM:(2 * g + 2) * HEAD_DIM], axis=-1, keepdims=True)
            dl_ref[g] = jnp.concatenate([d0, d1], axis=1)
        zero = jnp.zeros((1, d), F32)
        _acc_rows(st_ref, i == 0, [zero, zero, dgate])

    return pl.pallas_call(
        body, name="wo_bwd", grid=(l_rows // tm,),
        out_shape=[jax.ShapeDtypeStruct((l_rows, d), BF16), jax.ShapeDtypeStruct((l_rows, z), BF16),
                   jax.ShapeDtypeStruct((z, l_rows), BF16),
                   jax.ShapeDtypeStruct((n_kv, l_rows, 2), F32), jax.ShapeDtypeStruct((8, d), F32)],
        in_specs=[pl.BlockSpec((tm, d), lambda i: (i, 0)), pl.BlockSpec((tm, d), lambda i: (i, 0)),
                  pl.BlockSpec((tm, z), lambda i: (i, 0)), _const_spec(wo.shape),
                  pl.BlockSpec((None, 8, d), lambda i: (1, 0, 0))],
        out_specs=[pl.BlockSpec((tm, d), lambda i: (i, 0)), pl.BlockSpec((tm, z), lambda i: (i, 0)),
                   pl.BlockSpec((z, tm), lambda i: (0, i)),
                   pl.BlockSpec((n_kv, tm, 2), lambda i: (0, i, 0)), pl.BlockSpec((8, d), lambda i: (0, 0))],
        compiler_params=_params(("arbitrary",)),
    )(dh1, y, o, wo, mods)


def _qkv_bwd(h, dh_lat, dq, dkt, dvt, mods, ng, w, qg, kg, cos, sin, n_heads, n_kv, nct, tm):
    t_rows, d = h.shape
    qw, kw = n_heads * HEAD_DIM, n_kv * HEAD_DIM
    scale = HEAD_DIM ** -0.5

    def body(h_ref, dhl_ref, dq_ref, dkt_ref, dvt_ref, mod_ref, ng_ref, w_ref, qg_ref, kg_ref, cos_ref,
             sin_ref, dh_ref, a_ref, dqkv_ref, st_ref, dg_ref):
        i = pl.program_id(0)
        lat = (i >= nct).astype(F32)
        dk_t = dkt_ref[...].T * (1.0 / LOG2E)
        ngv, sc = ng_ref[...], mod_ref[1:2, :]
        a, xhat, rstd, n = _normmod(h_ref[...], ngv, mod_ref[0:1, :], sc)
        ab = a.astype(BF16)
        a_ref[...] = ab
        qkv = _dot(ab, w_ref[...])
        cosv, sinv = cos_ref[...], sin_ref[...]
        dqg = jnp.zeros((1, HEAD_DIM), F32)
        dkg = jnp.zeros((1, HEAD_DIM), F32)
        for hd in range(n_heads + n_kv):
            cols = slice(hd * HEAD_DIM, (hd + 1) * HEAD_DIM)
            is_q = hd < n_heads
            g = qg_ref[...] if is_q else kg_ref[...]
            _, hx, hr = _head_norm(qkv[:, cols], g)
            if is_q:
                dxr = dq_ref[:, cols] * (scale * lat)
            else:
                dxr = dk_t[:, (hd - n_heads) * HEAD_DIM:(hd - n_heads + 1) * HEAD_DIM]
            dxn = dxr * cosv + _swap_pairs(dxr * sinv)
            if is_q:
                dqg = dqg + _colsum(dxn * hx)
            else:
                dkg = dkg + _colsum(dxn * hx)
            dxh = dxn * g
            dx = hr * (dxh - hx * jnp.mean(dxh * hx, axis=-1, keepdims=True))
            dqkv_ref[:, cols] = dx.astype(BF16)
        dqkv_ref[:, qw + kw:] = dvt_ref[...].T.astype(BF16)
        da = _dot_nt(dqkv_ref[...], w_ref[...])
        dhn, dsh, dsc, dng = _normmod_bwd(da, xhat, rstd, n, ngv, sc)
        dh_ref[...] = dhl_ref[...] * lat + dhn
        _acc_rows(st_ref, _first_of_stream(i, nct), [dsh, dsc, jnp.zeros((1, d), F32), dng])
        _acc_rows(dg_ref, i == 0, [dqg, dkg])

    lat_map = lambda i: (jnp.maximum(i - nct, 0), 0)
    return pl.pallas_call(
        body, name="qkv_bwd", grid=(t_rows // tm,),
        out_shape=[jax.ShapeDtypeStruct((t_rows, d), F32), jax.ShapeDtypeStruct((t_rows, d), BF16),
                   jax.ShapeDtypeStruct((t_rows, qw + 2 * kw), BF16), jax.ShapeDtypeStruct((2, 8, d), F32),
                   jax.ShapeDtypeStruct((8, HEAD_DIM), F32)],
        in_specs=[pl.BlockSpec((tm, d), lambda i: (i, 0)), pl.BlockSpec((tm, d), lat_map),
                  pl.BlockSpec((tm, qw), lat_map), pl.BlockSpec((kw, tm), lambda i: (0, i)),
                  pl.BlockSpec((kw, tm), lambda i: (0, i)),
                  pl.BlockSpec((None, 8, d), lambda i: (_sid(i, nct), 0, 0)),
                  _const_spec((1, d)), _const_spec(w.shape), _const_spec((1, HEAD_DIM)),
                  _const_spec((1, HEAD_DIM)),
                  pl.BlockSpec((tm, HEAD_DIM), lambda i: (i, 0)), pl.BlockSpec((tm, HEAD_DIM), lambda i: (i, 0))],
        out_specs=[pl.BlockSpec((tm, d), lambda i: (i, 0)), pl.BlockSpec((tm, d), lambda i: (i, 0)),
                   pl.BlockSpec((tm, qw + 2 * kw), lambda i: (i, 0)),
                   pl.BlockSpec((None, 8, d), lambda i: (_sid(i, nct), 0, 0)),
                   pl.BlockSpec((8, HEAD_DIM), lambda i: (0, 0))],
        compiler_params=_params(("arbitrary",)),
    )(h, dh_lat, dq, dkt, dvt, mods, ng, w, qg, kg, cos, sin)


def _gmlp_core(a_bf, win_ref, lng, lnb, ws_ref, bst_ref, tm, half, with_grad=False):
    blocks = [_dot(a_bf, win_ref[j]) for j in range(N_DEV)]
    zu = jnp.concatenate(blocks[:N_DEV // 2], axis=1)
    zv = jnp.concatenate(blocks[N_DEV // 2:], axis=1)
    if with_grad:
        (u, zu), (v, zv) = _gelu_and_grad(zu), _gelu_and_grad(zv)
    else:
        u, v, zu, zv = _gelu(zu), _gelu(zv), None, None
    mu = jnp.mean(v, axis=-1, keepdims=True)
    vc = v - mu
    rstd_v = lax.rsqrt(jnp.mean(vc * vc, axis=-1, keepdims=True) + EPS)
    vhat = vc * rstd_v
    vln = (vhat * lng + lnb).astype(BF16)
    gw = half // GMLP_GROUPS
    rows = []
    for ch in range(tm // CHUNK):
        rs = slice(ch * CHUNK, (ch + 1) * CHUNK)
        cols = []
        for g in range(GMLP_GROUPS):
            cs = slice(g * gw, (g + 1) * gw)
            cols.append(_dot(ws_ref[g], vln[rs, cs]) + bst_ref[:, g:g + 1])
        rows.append(jnp.concatenate(cols, axis=1))
    sv = rows[0] if len(rows) == 1 else jnp.concatenate(rows, axis=0)
    return zu, zv, u, vhat, rstd_v, vln, sv


def _gmlp_fwd(h, mods, ng, win, lng, lnb, ws, bst, wout, tm):
    l_rows, d = h.shape
    half = wout.shape[0]

    def body(h_ref, mod_ref, ng_ref, win_ref, lng_ref, lnb_ref, ws_ref, bst_ref, wout_ref, y_ref, h1_ref):
        hv = h_ref[...]
        a = _normmod(hv, ng_ref[...], mod_ref[0:1, :], mod_ref[1:2, :])[0]
        _, _, u, _, _, _, sv = _gmlp_core(a.astype(BF16), win_ref, lng_ref[...], lnb_ref[...], ws_ref,
                                          bst_ref, tm, half)
        y = _dot((u * sv).astype(BF16), wout_ref[...])
        y_ref[...] = y.astype(BF16)
        h1_ref[...] = hv + mod_ref[2:3, :] * y

    return pl.pallas_call(
        body, name="gmlp_fwd", grid=(l_rows // tm,),
        out_shape=[jax.ShapeDtypeStruct((l_rows, d), BF16), jax.ShapeDtypeStruct((l_rows, d), F32)],
        in_specs=[pl.BlockSpec((tm, d), lambda i: (i, 0)), pl.BlockSpec((None, 8, d), lambda i: (1, 0, 0)),
                  _const_spec((1, d)), _const_spec(win.shape), _const_spec((1, half)), _const_spec((1, half)),
                  _const_spec(ws.shape), _const_spec(bst.shape), _const_spec(wout.shape)],
        out_specs=[pl.BlockSpec((tm, d), lambda i: (i, 0)), pl.BlockSpec((tm, d), lambda i: (i, 0))],
        compiler_params=_params(("arbitrary",)),
    )(h, mods, ng, win, lng, lnb, ws, bst, wout)


def _gmlp_bwd(dh1, h, y, mods, ng, win, lng, lnb, ws, wst, bst, wout, tm):
    l_rows, d = h.shape
    half = wout.shape[0]
    gw = half // GMLP_GROUPS

    def body(dh_ref, h_ref, y_ref, mod_ref, ng_ref, win_ref, lng_ref, lnb_ref, ws_ref, wst_ref, bst_ref,
             wout_ref, dho_ref, a_ref, dz_ref, gt_ref, dy_ref, st_ref, ln_ref, dws_ref, dbs_ref):
        i = pl.program_id(0)
        ngv, sc = ng_ref[...], mod_ref[1:2, :]
        lngv = lng_ref[...]
        a, xhat, rstd, n = _normmod(h_ref[...], ngv, mod_ref[0:1, :], sc)
        ab = a.astype(BF16)
        a_ref[...] = ab
        gu, gv, u, vhat, rstd_v, vln, sv = _gmlp_core(ab, win_ref, lngv, lnb_ref[...], ws_ref, bst_ref,
                                                      tm, half, with_grad=True)
        gt_ref[...] = (u * sv).astype(BF16)
        dh = dh_ref[...]
        dgate = _colsum(dh * y_ref[...].astype(F32))
        dy = (mod_ref[2:3, :] * dh).astype(BF16)
        dy_ref[...] = dy
        dgated = _dot_nt(dy, wout_ref[...])
        du = dgated * sv
        dsv = (dgated * u).astype(BF16)

        @pl.when(i == 0)
        def _():
            dws_ref[...] = jnp.zeros_like(dws_ref)
            dbs_ref[...] = jnp.zeros_like(dbs_ref)
        lane = lax.broadcasted_iota(jnp.int32, (CHUNK, 128), 1)
        dbs = jnp.zeros((CHUNK, 128), F32)
        rows = []
        for ch in range(tm // CHUNK):
            rs = slice(ch * CHUNK, (ch + 1) * CHUNK)
            cols = []
            for g in range(GMLP_GROUPS):
                cs = slice(g * gw, (g + 1) * gw)
                dsv_cg = dsv[rs, cs]
                dws_ref[g] += _dot_nt(dsv_cg, vln[rs, cs])
                cols.append(_dot(wst_ref[g], dsv_cg))
                dbs = dbs + jnp.where(lane == g, jnp.sum(dsv_cg.astype(F32), axis=-1, keepdims=True), 0.0)
            rows.append(jnp.concatenate(cols, axis=1))
        dbs_ref[...] += dbs
        dvln = rows[0] if len(rows) == 1 else jnp.concatenate(rows, axis=0)
        dlng = _colsum(dvln * vhat)
        dlnb = _colsum(dvln)
        dvh = dvln * lngv
        dv = rstd_v * (dvh - jnp.mean(dvh, axis=-1, keepdims=True)
                       - vhat * jnp.mean(dvh * vhat, axis=-1, keepdims=True))
        dz_ref[:, :half] = (du * gu).astype(BF16)
        dz_ref[:, half:] = (dv * gv).astype(BF16)
        nb = 2 * half // N_DEV
        da = _dot_nt(dz_ref[:, 0:nb], win_ref[0])
        for j in range(1, N_DEV):
            da = da + _dot_nt(dz_ref[:, j * nb:(j + 1) * nb], win_ref[j])
        dhn, dsh, dsc, dng = _normmod_bwd(da, xhat, rstd, n, ngv, sc)
        dho_ref[...] = dh + dhn
        _acc_rows(st_ref, i == 0, [dsh, dsc, dgate, dng])
        _acc_rows(ln_ref, i == 0, [dlng, dlnb])

    row = lambda w: pl.BlockSpec((tm, w), lambda i: (i, 0))
    return pl.pallas_call(
        body, name="gmlp_bwd", grid=(l_rows // tm,),
        out_shape=[jax.ShapeDtypeStruct((l_rows, d), F32), jax.ShapeDtypeStruct((l_rows, d), BF16),
                   jax.ShapeDtypeStruct((l_rows, 2 * half), BF16), jax.ShapeDtypeStruct((l_rows, half), BF16),
                   jax.ShapeDtypeStruct((l_rows, d), BF16), jax.ShapeDtypeStruct((8, d), F32),
                   jax.ShapeDtypeStruct((8, half), F32), jax.ShapeDtypeStruct(ws.shape, F32),
                   jax.ShapeDtypeStruct((CHUNK, 128), F32)],
        in_specs=[row(d), row(d), row(d), pl.BlockSpec((None, 8, d), lambda i: (1, 0, 0)),
                  _const_spec((1, d)), _const_spec(win.shape), _const_spec((1, half)), _const_spec((1, half)),
                  _const_spec(ws.shape), _const_spec(ws.shape), _const_spec(bst.shape), _const_spec(wout.shape)],
        out_specs=[row(d), row(d), row(2 * half), row(half), row(d),
                   pl.BlockSpec((8, d), lambda i: (0, 0)), pl.BlockSpec((8, half), lambda i: (0, 0)),
                   pl.BlockSpec(ws.shape, lambda i: (0, 0, 0)), pl.BlockSpec((CHUNK, 128), lambda i: (0, 0))],
        compiler_params=_params(("arbitrary",)),
    )(dh1, h, y, mods, ng, win, lng, lnb, ws, wst, bst, wout)


def _head(h, final_g, target, tm):
    l_rows, d = h.shape
    n_tiles = l_rows // tm

    def body(h_ref, g_ref, t_ref, dh_ref, loss_ref, dg_ref, acc_ref):
        i = pl.program_id(0)
        g = g_ref[...]
        hv = h_ref[...]
        rstd = lax.rsqrt(jnp.mean(hv * hv, axis=-1, keepdims=True) + EPS)
        xhat = hv * rstd
        e = xhat * g - t_ref[...]
        dout = e * (1.0 / d)
        dxhat = dout * g
        dh_ref[...] = rstd * (dxhat - xhat * jnp.mean(dxhat * xhat, axis=-1, keepdims=True))
        _acc_rows(dg_ref, i == 0, [_colsum(dout * xhat)])
        _acc_rows(acc_ref, i == 0, [_colsum(e * e)])

        @pl.when(i == n_tiles - 1)
        def _():
            total = jnp.sum(acc_ref[0:1, :], axis=-1, keepdims=True) * (0.5 / d)
            loss_ref[...] = jnp.broadcast_to(total, loss_ref.shape)

    return pl.pallas_call(
        body, name="loss_head", grid=(n_tiles,),
        out_shape=[jax.ShapeDtypeStruct((l_rows, d), F32), jax.ShapeDtypeStruct((8, 128), F32),
                   jax.ShapeDtypeStruct((8, d), F32)],
        in_specs=[pl.BlockSpec((tm, d), lambda i: (i, 0)), _const_spec((1, d)),
                  pl.BlockSpec((tm, d), lambda i: (i, 0))],
        out_specs=[pl.BlockSpec((tm, d), lambda i: (i, 0)), pl.BlockSpec((8, 128), lambda i: (0, 0)),
                   pl.BlockSpec((8, d), lambda i: (0, 0))],
        scratch_shapes=[pltpu.VMEM((8, d), F32)],
        compiler_params=_params(("arbitrary",)),
    )(h, final_g, target)


def _adamw(w, gparts, m, v, name, rider=None):
    shape = w.shape
    cols = shape[-1]
    rows = int(np.prod(shape[:-1])) if len(shape) > 1 else 1
    pieces = list(gparts) if isinstance(gparts, (list, tuple)) else [gparts]
    n_pieces = len(pieces)
    nparts = pieces[0].shape[0]
    piece_rows = rows // n_pieces
    w2, m2, v2 = (t.reshape(rows, cols) for t in (w, m, v))
    pieces = [g.reshape(nparts, piece_rows, cols) for g in pieces]
    tr = piece_rows
    part_bytes = nparts * cols * pieces[0].dtype.itemsize
    for cand in (1024, 512, 256, 128, 64, 32, 16, 8):
        if piece_rows * max(part_bytes, cols * 4) <= (2 << 20):
            break
        if piece_rows % cand == 0 and cand < piece_rows:
            tr = cand
            if cand * max(part_bytes, cols * 4) <= (2 << 20):
                break
    per_piece = piece_rows // tr
    c1 = 1.0 - ADAM_B1 ** ADAM_STEP
    c2 = 1.0 - ADAM_B2 ** ADAM_STEP

    def update(w_ref, g_ref, m_ref, v_ref, go_ref, d_ref, mo_ref, vo_ref):
        g = g_ref[0].astype(F32)
        for k in range(1, nparts):
            g = g + g_ref[k].astype(F32)
        mn = ADAM_B1 * m_ref[...] + (1.0 - ADAM_B1) * g
        vn = ADAM_B2 * v_ref[...] + (1.0 - ADAM_B2) * (g * g)
        go_ref[...] = g
        mo_ref[...] = mn
        vo_ref[...] = vn
        d_ref[...] = -ADAM_LR * ((mn / c1) / (jnp.sqrt(vn / c2) + ADAM_EPS) + ADAM_WD * w_ref[...])

    def body(w_ref, *refs):
        g_refs, (m_ref, v_ref, go_ref, d_ref, mo_ref, vo_ref) = refs[:n_pieces], refs[n_pieces:]
        if n_pieces == 1:
            update(w_ref, g_refs[0], m_ref, v_ref, go_ref, d_ref, mo_ref, vo_ref)
        else:
            piece = pl.program_id(0) // per_piece
            for k in range(n_pieces):
                pl.when(piece == k)(functools.partial(update, w_ref, g_refs[k], m_ref, v_ref, go_ref, d_ref, mo_ref, vo_ref))

    def piece_spec(k):
        return pl.BlockSpec((nparts, tr, cols), lambda i: (0, jnp.clip(i - k * per_piece, 0, per_piece - 1), 0))

    spec = pl.BlockSpec((tr, cols), lambda i: (i, 0))
    outs, rode = _compute_call(
        body, name=name, grid=(rows // tr,), operands=(w2, *pieces, m2, v2), rider=rider,
        out_shape=[jax.ShapeDtypeStruct((rows, cols), F32)] * 4,
        in_specs=[spec] + [piece_spec(k) for k in range(n_pieces)] + [spec, spec],
        out_specs=[spec] * 4)
    outs = tuple(o.reshape(shape) for o in outs)
    return outs if rider is None else (outs, rode)


def _natural_cols(g):
    return jnp.moveaxis(g, 0, -2).reshape(g.shape[1:-1] + (N_DEV * g.shape[-1],))


def _natural_rows(g):
    return jnp.moveaxis(g, 0, -3).reshape(g.shape[1:-2] + (N_DEV * g.shape[-2], g.shape[-1]))


def _shard_rows(full):
    r = full.shape[-2] // N_DEV
    return jnp.moveaxis(full.reshape(full.shape[:-2] + (N_DEV, r, full.shape[-1])), -3, 0)


def _my_cols(gathered, me, n):
    return lax.dynamic_slice_in_dim(gathered, me * n, n, axis=gathered.ndim - 1)


def kernel(x, c, ctx, c_ctx, ada_w, ada_b, norm_g, mlp_w1, mlp_w2, pool_w, pool_scale, attn_w_qkv, attn_w_o, attn_q_g, attn_k_g, gm_w_in, gm_ln_g, gm_ln_b, gm_ws, gm_bs, gm_w_out, final_g, loss_target, m_c_ctx, m_ada_w, m_ada_b, m_norm_g, m_mlp_w1, m_mlp_w2, m_pool_w, m_pool_scale, m_attn_w_qkv, m_attn_w_o, m_attn_q_g, m_attn_k_g, m_gm_w_in, m_gm_ln_g, m_gm_ln_b, m_gm_ws, m_gm_bs, m_gm_w_out, m_final_g, v_c_ctx, v_ada_w, v_ada_b, v_norm_g, v_mlp_w1, v_mlp_w2, v_pool_w, v_pool_scale, v_attn_w_qkv, v_attn_w_o, v_attn_q_g, v_attn_k_g, v_gm_w_in, v_gm_ln_g, v_gm_ln_b, v_gm_ws, v_gm_bs, v_gm_w_out, v_final_g):
    l_len, d = x.shape[1], x.shape[2]
    c_len = ctx.shape[1]
    n_layers = ada_w.shape[0]
    assert n_layers == 4 and x.shape[0] == 1
    n_heads = d // HEAD_DIM
    n_kv = n_heads // 2
    half = gm_w_out.shape[1] * N_DEV
    tm = c_len if c_len <= 256 else 256
    assert c_len % tm == 0 and l_len % tm == 0 and tm % CHUNK == 0 and l_len % GRID_W == 0
    nct = c_len // tm
    me = _dev_index(*_coords())
    n_ada = ada_w.shape[-1]

    first = [t.astype(BF16) for t in (mlp_w1[0], mlp_w2[0], pool_w, attn_w_qkv[0])]
    small = [c, norm_g.reshape(n_layers * 2, -1), pool_scale, gm_ln_g, gm_ln_b]
    w1_0g, w2_0g, pool_g, qkv_g, c_all, ng_g, ps_g, lng_g, lnb_g = _all_gather(first + small, "gather_first")
    c_all = c_all.reshape(N_DEV, d)
    later = _GatherAcrossChips([t.astype(BF16) for t in
                                (mlp_w1[1], mlp_w1[2], mlp_w1[3], mlp_w2[1], mlp_w2[2], mlp_w2[3],
                                 attn_w_o[0], gm_w_in[0], gm_w_out[0])])
    pool_wf = _natural_rows(pool_g)
    wqkv = _natural_cols(qkv_g)
    ng_full = _natural_cols(ng_g.reshape(N_DEV, n_layers * 2, 1, -1)).reshape(n_layers, 2, 1, d)
    ps_full = _natural_cols(ps_g.reshape(N_DEV, 2, 1, -1))
    lng_full = _natural_cols(lng_g.reshape(N_DEV, 1, -1))
    lnb_full = _natural_cols(lnb_g.reshape(N_DEV, 1, -1))

    c_ctx2 = c_ctx.reshape(1, d)
    ada_b_loc = lax.dynamic_slice_in_dim(ada_b, me * n_ada, n_ada, axis=1).reshape(n_layers, 1, n_ada)
    (mod_g,) = _all_gather([_mods_local(c_all, c_ctx2, ada_w, ada_b_loc)], "gather_mods")
    mod_full = jnp.moveaxis(mod_g, 0, 2).reshape(n_layers, 16, 6, d)
    mod_lat = lax.dynamic_index_in_dim(mod_full, me, axis=1, keepdims=False)
    mod_ctx = mod_full[:, 8]
    mods = jnp.stack([mod_ctx, mod_lat], axis=1)
    mods = jnp.concatenate([mods, jnp.zeros((n_layers, 2, 2, d), F32)], axis=2)

    bands = _pool_bands(tm)
    cos, sin = _rope_tables(c_len, l_len)
    ws_bf = gm_ws[0].astype(BF16)
    wst_bf = jnp.swapaxes(gm_ws[0], 1, 2).astype(BF16)
    bst = jnp.zeros((CHUNK, 128), F32).at[:, :GMLP_GROUPS].set(gm_bs[0].T)
    ng = lambda i, j: ng_full[i, j]

    h0 = jnp.concatenate([ctx[0], x[0]], axis=0)
    y0, h1 = _pool_fwd(h0, mods[0], ng(0, 0), pool_wf[0].astype(BF16), ps_full[0], bands, nct, tm, c_len, l_len)
    h2, p0, ym0 = _mlp_fwd(h1, mods[0], ng(0, 1), w1_0g, w2_0g, nct, tm)
    q, k, v1, qt = _qkv_fwd(h2, mods[1], ng(1, 0), wqkv, attn_q_g, attn_k_g, cos, sin, n_heads, n_kv, nct, tm)
    o, lse, later_g = _flash_fwd(q, k, v1, n_kv, 4 * tm, tm, rider=later)
    (wo_g,) = _exchange_call(_ForwardToSibling(later_g[6:7]), "forward_wo")
    wo = _natural_rows(wo_g)
    y1, h3, rest_g = _wo_fwd(h2, o, wo, mods[1], nct, tm, rider=_ForwardToSibling(later_g[0:6] + later_g[7:9]))
    w1 = [w1_0g] + rest_g[0:3]
    w2 = [w2_0g] + rest_g[3:6]
    win = rest_g[6]
    wout = _natural_rows(rest_g[7])
    tm_lat = 2 * tm
    h4, p1, ym1 = _mlp_fwd(h3, mods[1], ng(1, 1), w1[1], w2[1], 0, tm_lat)
    y2, h5 = _gmlp_fwd(h4, mods[2], ng(2, 0), win, lng_full, lnb_full, ws_bf, bst, wout, tm)
    h6, p2, ym2 = _mlp_fwd(h5, mods[2], ng(2, 1), w1[2], w2[2], 0, tm_lat)
    y3, h7 = _pool_fwd(h6, mods[3], ng(3, 0), pool_wf[1].astype(BF16), ps_full[1], bands, 0, tm, c_len, l_len)
    h8, p3, ym3 = _mlp_fwd(h7, mods[3], ng(3, 1), w1[3], w2[3], 0, tm_lat)
    dh, loss_part, dfinal = _head(h8, final_g.reshape(1, d), loss_target[0], tm_lat)

    dw1, dw2, st_mlp = [None] * 4, [None] * 4, [None] * 4

    def mlp_back(i, dh, h_in, p, ym, nct_i):
        dh_in, m_bf, du, dacc, st, _ = _mlp_bwd(dh, h_in, p, ym, mods[i], ng(i, 1), w1[i], w2[i], nct_i, tm_lat)
        dw1[i] = _tn_matmul(m_bf, du, "tn_w1", col_shards=True)
        dw2[i] = _shard_rows(_tn_matmul(p, dacc, "tn_w2", square_x=True))
        st_mlp[i] = st
        return dh_in

    dh = mlp_back(3, dh, h7, p3, ym3, 0)
    dh, dpw1, st_pool3, _ = _pool_bwd(dh, h6, y3, mods[3], ng(3, 0), pool_wf[1].astype(BF16), ps_full[1], bands,
                                      0, tm, c_len, l_len, False)
    dh = mlp_back(2, dh, h5, p2, ym2, 0)
    dh, a_bf, dz, gated, dy, st_g, st_ln, dws, dbst = _gmlp_bwd(
        dh, h4, y2, mods[2], ng(2, 0), win, lng_full, lnb_full, ws_bf, wst_bf, bst, wout, tm)
    dwin = _tn_matmul(a_bf, dz, "tn_gm_in", col_shards=True)
    dwout = _tn_matmul(gated, dy, "tn_gm_out")
    dh = mlp_back(1, dh, h3, p1, ym1, 0)
    dy1, do, dot, delta, st_wo = _wo_bwd(dh, y1, o, wo, mods[1], n_kv, tm)
    dwo = _tn_matmul(o, dy1, "tn_wo")
    grads_mid = _AllToAll(dw1[1:] + dw2[1:] + [_shard_rows(dpw1.astype(BF16)), _shard_rows(dwo), dwin,
                                               _shard_rows(dwout)])
    dq, dkt, dvt, rode = _flash_bwd(q, qt, k, v1, do, dot, lse, delta, n_kv, 4 * tm, tm, rider=grads_mid)
    g_w1, g_w2, (g_pool1, g_wo, g_gin, g_gout) = [None] + rode[0:3], [None] + rode[3:6], rode[6:]
    dh, a_bf, dqkv, st_q, dgains = _qkv_bwd(h2, dh, dq, dkt, dvt, mods[1], ng(1, 0), wqkv, attn_q_g, attn_k_g,
                                            cos, sin, n_heads, n_kv, nct, tm)
    dwqkv = _tn_matmul(a_bf, dqkv, "tn_qkv", col_shards=True)
    dh, m_bf, du, dacc, st_mlp[0], (g_qkv,) = _mlp_bwd(dh, h1, p0, ym0, mods[0], ng(0, 1), w1[0], w2[0], nct, tm,
                                                       rider=_AllToAll([dwqkv]))
    dw1_0 = _tn_matmul(m_bf, du, "tn_w1", col_shards=True)
    dw2_0, (g_w1[0],) = _tn_matmul(p0, dacc, "tn_w2", square_x=True, rider=_AllToAll([dw1_0]))
    grad_x, dpw0, st_pool0, (g_w2[0],) = _pool_bwd(dh, h0, y0, mods[0], ng(0, 0), pool_wf[0].astype(BF16),
                                                   ps_full[0], bands, nct, tm, c_len, l_len, True,
                                                   rider=_AllToAll([_shard_rows(dw2_0)]))

    mix_lat = [st_pool0[-1], st_q[1] + st_wo, st_g, st_pool3[-1]]
    mlp_lat = [st[-1] for st in st_mlp]
    dmod_lat = jnp.stack([jnp.concatenate([mix_lat[i][0:3], mlp_lat[i][0:3]]) for i in range(n_layers)])
    dmod_ctx = jnp.stack([jnp.concatenate([st_pool0[0][0:3], st_mlp[0][0][0:3]]),
                          jnp.concatenate([st_q[0][0:2], jnp.zeros((4, d), F32)]),
                          jnp.zeros((6, d), F32), jnp.zeros((6, d), F32)])
    dng_part = jnp.stack([jnp.stack([mix_lat[0][3] + st_pool0[0][3], mlp_lat[0][3] + st_mlp[0][0][3]]),
                          jnp.stack([mix_lat[1][3] + st_q[0][3], mlp_lat[1][3]]),
                          jnp.stack([mix_lat[2][3], mlp_lat[2][3]]),
                          jnp.stack([mix_lat[3][3], mlp_lat[3][3]])])
    dps_part = jnp.stack([mix_lat[0][4] + st_pool0[0][4], mix_lat[3][4]])
    small_parts = [dmod_lat.reshape(n_layers * 6, d), dmod_ctx.reshape(n_layers * 6, d),
                   dng_part.reshape(n_layers * 2, d), dps_part, st_ln, dgains, dws.reshape(-1, CHUNK),
                   dbst, dfinal, loss_part]
    done = {}
    done["mlp_w1"], rode = _adamw(mlp_w1, g_w1, m_mlp_w1, v_mlp_w1, "adamw_mlp_w1", rider=_Both(
        _GatherAcrossChips(small_parts), _AllToAll([_shard_rows(dpw0.astype(BF16))])))
    g_pool0 = rode[-1]
    done["mlp_w2"], rode = _adamw(mlp_w2, g_w2, m_mlp_w2, v_mlp_w2, "adamw_mlp_w2",
                                  rider=_ForwardToSibling(rode[:-1]))
    gm_lat, gm_ctx, g_ng, g_ps, g_ln, g_gains, g_ws, g_bst, g_final, loss_all = rode
    g_pool = jnp.stack([g_pool0, g_pool1], axis=1)

    gm_lat4 = gm_lat.reshape(N_DEV, n_layers, 6 * d)
    gm_ctx4 = gm_ctx.reshape(N_DEV, n_layers, 6 * d)
    dm_lat_loc = jnp.moveaxis(_my_cols(gm_lat4, me, n_ada), 0, 1)
    dm_ctx_loc = jnp.moveaxis(_my_cols(gm_ctx4, me, n_ada), 0, 1)
    g_ada_w, ds_part = _ada_grads(c_all, c_ctx2, ada_w, dm_lat_loc, dm_ctx_loc)
    (ds_all,) = _all_gather([ds_part], "gather_dsctx")
    g_c_ctx, loss_sum = _cctx_grad_and_loss(ds_all, c_ctx2, loss_all)
    g_c_ctx = g_c_ctx.reshape(d)

    n_ng = norm_g.shape[-1]
    n_ps = pool_scale.shape[-1]
    n_ln = gm_ln_g.shape[-1]
    gparts = {
        "c_ctx": g_c_ctx[None],
        "ada_w": g_ada_w[None],
        "ada_b": jnp.concatenate([gm_lat4, gm_ctx4], axis=0),
        "norm_g": _my_cols(g_ng.reshape(N_DEV, n_layers, 2, d), me, n_ng),
        "pool_w": g_pool,
        "pool_scale": _my_cols(g_ps, me, n_ps),
        "attn_w_qkv": g_qkv[:, None], "attn_w_o": g_wo[:, None],
        "attn_q_g": g_gains[:, 0:1], "attn_k_g": g_gains[:, 1:2],
        "gm_w_in": g_gin[:, None],
        "gm_ln_g": _my_cols(g_ln[:, 0:1], me, n_ln), "gm_ln_b": _my_cols(g_ln[:, 1:2], me, n_ln),
        "gm_ws": g_ws.reshape((N_DEV,) + gm_ws.shape),
        "gm_bs": jnp.swapaxes(g_bst[:, :, :GMLP_GROUPS], 1, 2)[:, None],
        "gm_w_out": g_gout[:, None],
        "final_g": g_final[:, 0],
    }
    weights = dict(c_ctx=(c_ctx, m_c_ctx, v_c_ctx), ada_w=(ada_w, m_ada_w, v_ada_w), ada_b=(ada_b, m_ada_b, v_ada_b),
                   norm_g=(norm_g, m_norm_g, v_norm_g), mlp_w1=(mlp_w1, m_mlp_w1, v_mlp_w1),
                   mlp_w2=(mlp_w2, m_mlp_w2, v_mlp_w2), pool_w=(pool_w, m_pool_w, v_pool_w),
                   pool_scale=(pool_scale, m_pool_scale, v_pool_scale),
                   attn_w_qkv=(attn_w_qkv, m_attn_w_qkv, v_attn_w_qkv), attn_w_o=(attn_w_o, m_attn_w_o, v_attn_w_o),
                   attn_q_g=(attn_q_g, m_attn_q_g, v_attn_q_g), attn_k_g=(attn_k_g, m_attn_k_g, v_attn_k_g),
                   gm_w_in=(gm_w_in, m_gm_w_in, v_gm_w_in), gm_ln_g=(gm_ln_g, m_gm_ln_g, v_gm_ln_g),
                   gm_ln_b=(gm_ln_b, m_gm_ln_b, v_gm_ln_b), gm_ws=(gm_ws, m_gm_ws, v_gm_ws),
                   gm_bs=(gm_bs, m_gm_bs, v_gm_bs), gm_w_out=(gm_w_out, m_gm_w_out, v_gm_w_out),
                   final_g=(final_g, m_final_g, v_final_g))
    grads, deltas, new_m, new_v = [], [], [], []
    for wname, (w_, m_, v_) in weights.items():
        g_, d_, nm_, nv_ = done[wname] if wname in done else _adamw(w_, gparts[wname], m_, v_, "adamw_" + wname)
        grads.append(g_)
        deltas.append(d_)
        new_m.append(nm_)
        new_v.append(nv_)

    return (loss_sum[0, 0], grad_x[None], *grads, *deltas, *new_m, *new_v)
```

```python
import functools
import math

import numpy as np
import jax
import jax.numpy as jnp
from jax import lax
from jax.experimental import pallas as pl
from jax.experimental.pallas import tpu as pltpu

F32 = jnp.float32
BF16 = jnp.bfloat16
MESH_ID = pl.DeviceIdType.MESH

N_DEV = 8
EPS = 1e-6
HEAD_DIM = 128
GRID_W = 64
ROPE_BASE = 10000.0
CHUNK = 128
POOL_WINDOWS = (2, 4, 8, 16)
POOL_GROUPS = 4
POOL_HALO = 8
GMLP_GROUPS = 8
ADAM_LR, ADAM_B1, ADAM_B2, ADAM_EPS, ADAM_WD, ADAM_STEP = 0.001, 0.9, 0.999, 1e-08, 0.01, 10

V7X_VMEM_BYTES = 64 << 20
VMEM_LIMIT_BIG = V7X_VMEM_BYTES - (8 << 20)
FLASH_TK_CAP = 768
LOG2E = math.log2(math.e)
Q_SCALE = HEAD_DIM ** -0.5 * LOG2E


def _params(sem, vmem=VMEM_LIMIT_BIG):
    return pltpu.CompilerParams(dimension_semantics=sem, vmem_limit_bytes=vmem)


def _const_spec(shape):
    nd = len(shape)
    return pl.BlockSpec(shape, lambda *_: (0,) * nd, pipeline_mode=pl.Buffered(1))


def _dot(a, b):
    return jnp.dot(a, b, preferred_element_type=F32)


def _dot_nt(a, b):
    return lax.dot_general(a, b, (((1,), (1,)), ((), ())), preferred_element_type=F32)


def _dot_tn(a, b):
    return lax.dot_general(a, b, (((0,), (0,)), ((), ())), preferred_element_type=F32)


def _colsum(x):
    return jnp.sum(x, axis=0, keepdims=True)


def _sid(i, nct):
    if nct == 0:
        return 1
    return jnp.where(i >= nct, 1, 0)


def _n_streams(nct):
    return 2 if nct else 1


def _stat_sid(i, nct):
    return _sid(i, nct) if nct else 0


def _first_of_stream(i, nct):
    if nct == 0:
        return i == 0
    return jnp.logical_or(i == 0, i == nct)


def _normmod(h, ng, sh, sc):
    rstd = lax.rsqrt(jnp.mean(h * h, axis=-1, keepdims=True) + EPS)
    xhat = h * rstd
    n = xhat * ng
    return n * (1.0 + sc) + sh, xhat, rstd, n


def _normmod_bwd(da, xhat, rstd, n, ng, sc):
    dsh = _colsum(da)
    dsc = _colsum(da * n)
    dn = da * (1.0 + sc)
    dng = _colsum(dn * xhat)
    dxhat = dn * ng
    dh = rstd * (dxhat - xhat * jnp.mean(dxhat * xhat, axis=-1, keepdims=True))
    return dh, dsh, dsc, dng


def _acc_rows(ref, first, rows):
    @pl.when(first)
    def _():
        ref[...] = jnp.zeros_like(ref)
    for r, val in enumerate(rows):
        ref[r:r + 1, :] = ref[r:r + 1, :] + val


_GELU_C = math.sqrt(2.0 / math.pi)


def _gelu(x):
    t = jnp.tanh((_GELU_C * x) * (1.0 + 0.044715 * (x * x)))
    hx = 0.5 * x
    return hx + hx * t


def _gelu_and_grad(x):
    x2 = x * x
    t = jnp.tanh((_GELU_C * x) * (1.0 + 0.044715 * x2))
    hx = 0.5 * x
    g = hx + hx * t
    dg = (0.5 + 0.5 * t) + (hx * (1.0 - t * t)) * (_GELU_C + (3.0 * 0.044715 * _GELU_C) * x2)
    return g, dg


def _coords():
    return lax.axis_index("x"), lax.axis_index("y"), lax.axis_index("c")


def _dev_index(px, py, pc):
    return 4 * px + 2 * py + pc


def _all_gather(xs, name, extra=None):
    n = len(xs)
    e_in, e_in_specs, e_out, e_out_specs, e_scratch = _rider_parts(extra)
    ne = len(e_in)

    def body(*refs):
        x_refs, e_x = refs[:n], refs[n:n + ne]
        o_refs, e_o = refs[n + ne:2 * n + ne], refs[2 * n + ne:2 * n + 2 * ne]
        send_sems, recv_sems, local_sems = refs[2 * n + 2 * ne:2 * n + 2 * ne + 3]
        e_sems = refs[2 * n + 2 * ne + 3:]
        if extra is not None:
            extra.start(e_x, e_o, e_sems)
        x, y, c = _coords()
        me, sibling = (x, y, c), (x, y, 1 - c)
        chips = [(1 - x, y), (x, 1 - y), (1 - x, 1 - y)]

        def copy(a, k, block, to, src=None):
            dst = o_refs[a].at[_dev_index(*block)]
            return pltpu.make_async_remote_copy(
                src_ref=dst if src is None else src, dst_ref=dst,
                send_sem=send_sems.at[7 * a + k], recv_sem=recv_sems.at[7 * a + k],
                device_id=to, device_id_type=MESH_ID)

        mine = [pltpu.make_async_copy(x_refs[a], o_refs[a].at[_dev_index(*me)], local_sems.at[a])
                for a in range(n)]
        for cp in mine:
            cp.start()
        first = []
        for a in range(n):
            first.append(copy(a, 0, me, sibling, src=x_refs[a]))
            first += [copy(a, 1 + j, me, (*chip, c), src=x_refs[a]) for j, chip in enumerate(chips)]
        for cp in first:
            cp.start()
        passed = []
        for a in range(n):
            for j, chip in enumerate(chips):
                copy(a, 1 + j, (*chip, c), me).wait_recv()
                fwd = copy(a, 4 + j, (*chip, c), sibling)
                fwd.start()
                passed.append(fwd)
        for a in range(n):
            copy(a, 0, sibling, me).wait_recv()
            for j, chip in enumerate(chips):
                copy(a, 4 + j, (*chip, 1 - c), me).wait_recv()
        for cp in first + passed:
            cp.wait_send()
        for cp in mine:
            cp.wait()
        if extra is not None:
            extra.finish(e_x, e_o, e_sems)

    any_spec = pl.BlockSpec(memory_space=pl.ANY)
    outs = pl.pallas_call(
        body, name=name,
        out_shape=[jax.ShapeDtypeStruct((N_DEV,) + x.shape, x.dtype) for x in xs] + e_out,
        in_specs=[any_spec] * n + e_in_specs, out_specs=[any_spec] * n + e_out_specs,
        scratch_shapes=[pltpu.SemaphoreType.DMA((7 * n,)), pltpu.SemaphoreType.DMA((7 * n,)),
                        pltpu.SemaphoreType.DMA((n,))] + e_scratch,
    )(*xs, *e_in)
    return list(outs)


class _Exchange:
    per_array = 0
    in_place = False

    def __init__(self, xs):
        self.xs = list(xs)
        n = len(self.xs)
        self.out_shapes = self._out_shapes()
        self.scratch = [pltpu.SemaphoreType.DMA((self.per_array * n,)),
                        pltpu.SemaphoreType.DMA((self.per_array * n,)),
                        pltpu.SemaphoreType.DMA((n,))]

    def _out_shapes(self):
        raise NotImplementedError

    def _copies(self, x_refs, o_refs, sems):
        raise NotImplementedError

    def start(self, x_refs, o_refs, sems):
        mine, sends, _ = self._copies(x_refs, o_refs, sems)
        for cp in mine + sends:
            cp.start()

    def finish(self, x_refs, o_refs, sems):
        mine, sends, arrivals = self._copies(x_refs, o_refs, sems)
        for make in arrivals:
            make().wait_recv()
        for cp in sends:
            cp.wait_send()
        for cp in mine:
            cp.wait()


def _remote(src, dst, sems, k, to):
    return pltpu.make_async_remote_copy(src_ref=src, dst_ref=dst, send_sem=sems[0].at[k], recv_sem=sems[1].at[k],
                                        device_id=to, device_id_type=MESH_ID)


class _GatherAcrossChips(_Exchange):
    per_array = 4

    def _out_shapes(self):
        return [jax.ShapeDtypeStruct((N_DEV,) + x.shape, x.dtype) for x in self.xs]

    def _copies(self, x_refs, o_refs, sems):
        x, y, c = _coords()
        targets = [(x, y, 1 - c), (1 - x, y, c), (x, 1 - y, c), (1 - x, 1 - y, c)]
        mine, sends, arrivals = [], [], []
        for a, (x_ref, o_ref) in enumerate(zip(x_refs, o_refs)):
            own = o_ref.at[_dev_index(x, y, c)]
            mine.append(pltpu.make_async_copy(x_ref, own, sems[2].at[a]))
            for k, to in enumerate(targets):
                sends.append(_remote(x_ref, own, sems, 4 * a + k, to))
                arrivals.append(functools.partial(_remote, x_ref, o_ref.at[_dev_index(*to)], sems, 4 * a + k, to))
        return mine, sends, arrivals


class _ForwardToSibling(_Exchange):
    per_array = 3
    in_place = True

    def _out_shapes(self):
        return [jax.ShapeDtypeStruct(x.shape, x.dtype) for x in self.xs]

    def _copies(self, x_refs, o_refs, sems):
        x, y, c = _coords()
        chips = [(1 - x, y), (x, 1 - y), (1 - x, 1 - y)]
        sends, arrivals = [], []
        for a, (x_ref, o_ref) in enumerate(zip(x_refs, o_refs)):
            for j, chip in enumerate(chips):
                held = _dev_index(*chip, c)
                sends.append(_remote(x_ref.at[held], o_ref.at[held], sems, 3 * a + j, (x, y, 1 - c)))
                theirs = _dev_index(*chip, 1 - c)
                arrivals.append(functools.partial(_remote, x_ref.at[theirs], o_ref.at[theirs], sems, 3 * a + j,
                                                  (x, y, 1 - c)))
        return [], sends, arrivals


class _AllToAll(_Exchange):
    per_array = 7

    def _out_shapes(self):
        return [jax.ShapeDtypeStruct(x.shape, x.dtype) for x in self.xs]

    def _copies(self, x_refs, o_refs, sems):
        x, y, c = _coords()
        me_i = _dev_index(x, y, c)
        mine, sends, arrivals = [], [], []
        for a, (x_ref, o_ref) in enumerate(zip(x_refs, o_refs)):
            mine.append(pltpu.make_async_copy(x_ref.at[me_i], o_ref.at[me_i], sems[2].at[a]))
            for r in range(1, 8):
                to = (1 - x if r & 4 else x, 1 - y if r & 2 else y, 1 - c if r & 1 else c)
                to_i = _dev_index(*to)
                sends.append(_remote(x_ref.at[to_i], o_ref.at[me_i], sems, 7 * a + r - 1, to))
                arrivals.append(functools.partial(_remote, x_ref.at[to_i], o_ref.at[to_i], sems, 7 * a + r - 1, to))
        return mine, sends, arrivals


def _exchange_call(ex, name):
    n = len(ex.xs)

    def body(*refs):
        x_refs, o_refs, sems = refs[:n], refs[n:2 * n], refs[2 * n:]
        ex.start(x_refs, o_refs, sems)
        ex.finish(x_refs, o_refs, sems)

    any_spec = pl.BlockSpec(memory_space=pl.ANY)
    outs = pl.pallas_call(
        body, name=name, out_shape=ex.out_shapes, in_specs=[any_spec] * n, out_specs=[any_spec] * n,
        scratch_shapes=ex.scratch, input_output_aliases={a: a for a in range(n)} if ex.in_place else {},
    )(*ex.xs)
    return list(outs)


def _rider_parts(rider):
    if rider is None:
        return [], [], [], [], []
    any_spec = pl.BlockSpec(memory_space=pl.ANY)
    n = len(rider.xs)
    return rider.xs, [any_spec] * n, rider.out_shapes, [any_spec] * n, rider.scratch


def _compute_call(body, *, name, grid, in_specs, out_specs, out_shape, operands, scratch_shapes=(), rider=None):
    in_specs, out_specs, out_shape, scratch_shapes = list(in_specs), list(out_specs), list(out_shape), list(scratch_shapes)
    r_in, r_in_specs, r_out, r_out_specs, r_scratch = _rider_parts(rider)
    n_in, n_out, n_scr, nr = len(operands), len(out_shape), len(scratch_shapes), len(r_in)

    def riding_body(*refs):
        ins, refs = refs[:n_in], refs[n_in:]
        r_x, refs = refs[:nr], refs[nr:]
        outs, refs = refs[:n_out], refs[n_out:]
        r_o, refs = refs[:nr], refs[nr:]
        scratch, r_sems = refs[:n_scr], refs[n_scr:]
        if rider is not None:
            first, last = _grid_ends(grid)
            pl.when(first)(lambda: rider.start(r_x, r_o, r_sems))
        body(*ins, *outs, *scratch)
        if rider is not None:
            pl.when(last)(lambda: rider.finish(r_x, r_o, r_sems))

    in_place = rider is not None and rider.in_place
    res = pl.pallas_call(
        riding_body, name=name, grid=grid, out_shape=out_shape + r_out,
        in_specs=in_specs + r_in_specs, out_specs=out_specs + r_out_specs,
        scratch_shapes=scratch_shapes + r_scratch,
        input_output_aliases={n_in + a: n_out + a for a in range(nr)} if in_place else {},
        compiler_params=_params(("arbitrary",) * len(grid)),
    )(*operands, *r_in)
    return list(res[:n_out]), list(res[n_out:])


def _grid_ends(grid):
    first = pl.program_id(0) == 0
    last = pl.program_id(0) == grid[0] - 1
    for ax in range(1, len(grid)):
        first = jnp.logical_and(first, pl.program_id(ax) == 0)
        last = jnp.logical_and(last, pl.program_id(ax) == grid[ax] - 1)
    return first, last


def _silu(x):
    return x * (1.0 / (1.0 + jnp.exp(-x)))


def _cond_rows(c_all, c_ctx):
    d = c_all.shape[-1]
    s = jnp.concatenate([c_all, jnp.zeros((8, d), F32)], axis=0)
    row = lax.broadcasted_iota(jnp.int32, (16, d), 0)
    s = jnp.where(row == 8, c_ctx, s)
    return jnp.where(row <= 8, _silu(s), 0.0)


def _mods_local(c_all, c_ctx, ada_w, ada_b_loc):
    nl, d, n = ada_w.shape

    def body(c_ref, cc_ref, w_ref, b_ref, o_ref):
        s = _cond_rows(c_ref[...], cc_ref[...])
        o_ref[...] = jnp.dot(s, w_ref[...], preferred_element_type=F32,
                             precision=lax.Precision.HIGHEST) + b_ref[...]

    return pl.pallas_call(
        body, name="mods_local", grid=(nl,),
        out_shape=jax.ShapeDtypeStruct((nl, 16, n), F32),
        in_specs=[pl.BlockSpec((8, d), lambda i: (0, 0)), pl.BlockSpec((1, d), lambda i: (0, 0)),
                  pl.BlockSpec((None, d, n), lambda i: (i, 0, 0)),
                  pl.BlockSpec((None, 1, n), lambda i: (i, 0, 0))],
        out_specs=pl.BlockSpec((None, 16, n), lambda i: (i, 0, 0)),
        compiler_params=_params(("arbitrary",)),
    )(c_all, c_ctx, ada_w, ada_b_loc)


def _ada_grads(c_all, c_ctx, ada_w, dm_lat, dm_ctx):
    nl, d, n = ada_w.shape

    def body(c_ref, cc_ref, w_ref, dml_ref, dmc_ref, gw_ref, ds_ref):
        i = pl.program_id(0)
        s = _cond_rows(c_ref[...], cc_ref[...])
        csum = dmc_ref[0:1, :]
        for k in range(1, N_DEV):
            csum = csum + dmc_ref[k:k + 1, :]
        row = lax.broadcasted_iota(jnp.int32, (8, n), 0)
        dm_c = jnp.where(row == 0, csum, 0.0)
        dm = jnp.concatenate([dml_ref[...], dm_c], axis=0)
        gw_ref[...] = lax.dot_general(s, dm, (((0,), (0,)), ((), ())), preferred_element_type=F32,
                                      precision=lax.Precision.HIGHEST)
        ds = lax.dot_general(dm_c, w_ref[...], (((1,), (1,)), ((), ())),
                             preferred_element_type=F32, precision=lax.Precision.HIGHEST)

        @pl.when(i == 0)
        def _():
            ds_ref[...] = jnp.zeros_like(ds_ref)
        ds_ref[...] += ds

    return pl.pallas_call(
        body, name="ada_grads", grid=(nl,),
        out_shape=[jax.ShapeDtypeStruct((nl, d, n), F32), jax.ShapeDtypeStruct((8, d), F32)],
        in_specs=[pl.BlockSpec((8, d), lambda i: (0, 0)), pl.BlockSpec((1, d), lambda i: (0, 0)),
                  pl.BlockSpec((None, d, n), lambda i: (i, 0, 0)),
                  pl.BlockSpec((None, 8, n), lambda i: (i, 0, 0)),
                  pl.BlockSpec((None, 8, n), lambda i: (i, 0, 0))],
        out_specs=[pl.BlockSpec((None, d, n), lambda i: (i, 0, 0)),
                   pl.BlockSpec((8, d), lambda i: (0, 0))],
        compiler_params=_params(("arbitrary",)),
    )(c_all, c_ctx, ada_w, dm_lat, dm_ctx)


def _cctx_grad_and_loss(ds_parts, c_ctx, loss_parts):
    d = c_ctx.shape[-1]

    def body(p_ref, c_ref, l_ref, o_ref, lo_ref):
        ds, loss = p_ref[0], l_ref[0]
        for k in range(1, N_DEV):
            ds = ds + p_ref[k]
            loss = loss + l_ref[k]
        x = c_ref[...]
        sg = 1.0 / (1.0 + jnp.exp(-x))
        o_ref[...] = ds[0:1, :] * (sg * (1.0 + x * (1.0 - sg)))
        lo_ref[...] = loss

    return pl.pallas_call(body, name="cctx_grad", out_shape=[jax.ShapeDtypeStruct((1, d), F32),
                                                             jax.ShapeDtypeStruct((8, 128), F32)])(ds_parts, c_ctx, loss_parts)


def _mlp_fwd(h1, mods, ng, w1, w2, nct, tm, row_off=0):
    d = h1.shape[1]
    r = h1.shape[0] - row_off * tm
    fc = w1.shape[2]
    f = N_DEV * fc

    def body(h_ref, mod_ref, ng_ref, w1_ref, w2_ref, h2_ref, p_ref, y_ref):
        h = h_ref[...]
        a, _, _, _ = _normmod(h, ng_ref[...], mod_ref[3:4, :], mod_ref[4:5, :])
        ab = a.astype(BF16)
        acc = jnp.zeros((tm, d), F32)
        for j in range(N_DEV):
            sl = slice(j * fc, (j + 1) * fc)
            p = jnp.maximum(_dot(ab, w1_ref[j]), 0.0)
            p_ref[:, sl] = p.astype(BF16)
            acc = acc + _dot((p * p).astype(BF16), w2_ref[j])
        y_ref[...] = acc.astype(BF16)
        h2_ref[...] = h + mod_ref[5:6, :] * acc

    return pl.pallas_call(
        body, name="mlp_fwd", grid=(r // tm,),
        out_shape=[jax.ShapeDtypeStruct((r, d), F32), jax.ShapeDtypeStruct((r, f), BF16),
                   jax.ShapeDtypeStruct((r, d), BF16)],
        in_specs=[pl.BlockSpec((tm, d), lambda i: (i + row_off, 0)),
                  pl.BlockSpec((None, 8, d), lambda i: (_sid(i, nct), 0, 0)),
                  _const_spec((1, d)), _const_spec(w1.shape), _const_spec(w2.shape)],
        out_specs=[pl.BlockSpec((tm, d), lambda i: (i, 0)), pl.BlockSpec((tm, f), lambda i: (i, 0)),
                   pl.BlockSpec((tm, d), lambda i: (i, 0))],
        compiler_params=_params(("arbitrary",)),
    )(h1, mods, ng, w1, w2)


def _mlp_bwd(dh2, h1, p, y, mods, ng, w1, w2, nct, tm, row_off=0, rider=None):
    r_rows, d = dh2.shape
    fc = w1.shape[2]
    f = N_DEV * fc

    def body(dh_ref, h_ref, p_ref, y_ref, mod_ref, ng_ref, w1_ref, w2_ref,
             dh1_ref, m_ref, du_ref, dacc_ref, st_ref):
        i = pl.program_id(0)
        dh = dh_ref[...]
        ngv, sc, gate = ng_ref[...], mod_ref[4:5, :], mod_ref[5:6, :]
        a, xhat, rstd, n = _normmod(h_ref[...], ngv, mod_ref[3:4, :], sc)
        m_ref[...] = a.astype(BF16)
        dgate = _colsum(dh * y_ref[...].astype(F32))
        dacc = (gate * dh).astype(BF16)
        dacc_ref[...] = dacc
        dm = jnp.zeros((tm, d), F32)
        for j in range(N_DEV):
            sl = slice(j * fc, (j + 1) * fc)
            pj = p_ref[:, sl].astype(F32)
            du = (_dot_nt(dacc, w2_ref[j]) * (2.0 * pj)).astype(BF16)
            du_ref[:, sl] = du
            dm = dm + _dot_nt(du, w1_ref[j])
        dhn, dsh, dsc, dng = _normmod_bwd(dm, xhat, rstd, n, ngv, sc)
        dh1_ref[...] = dh + dhn
        _acc_rows(st_ref, _first_of_stream(i, nct), [dsh, dsc, dgate, dng])

    outs, rode = _compute_call(
        body, name="mlp_bwd", grid=(r_rows // tm,), operands=(dh2, h1, p, y, mods, ng, w1, w2), rider=rider,
        out_shape=[jax.ShapeDtypeStruct((r_rows, d), F32), jax.ShapeDtypeStruct((r_rows, d), BF16),
                   jax.ShapeDtypeStruct((r_rows, f), BF16),
                   jax.ShapeDtypeStruct((r_rows, d), BF16), jax.ShapeDtypeStruct((_n_streams(nct), 8, d), F32)],
        in_specs=[pl.BlockSpec((tm, d), lambda i: (i, 0)),
                  pl.BlockSpec((tm, d), lambda i: (i + row_off, 0)),
                  pl.BlockSpec((tm, f), lambda i: (i, 0)), pl.BlockSpec((tm, d), lambda i: (i, 0)),
                  pl.BlockSpec((None, 8, d), lambda i: (_sid(i, nct), 0, 0)),
                  _const_spec((1, d)), _const_spec(w1.shape), _const_spec(w2.shape)],
        out_specs=[pl.BlockSpec((tm, d), lambda i: (i, 0)), pl.BlockSpec((tm, d), lambda i: (i, 0)),
                   pl.BlockSpec((tm, f), lambda i: (i, 0)),
                   pl.BlockSpec((tm, d), lambda i: (i, 0)),
                   pl.BlockSpec((None, 8, d), lambda i: (_stat_sid(i, nct), 0, 0))])
    return (*outs, rode)


def _pick(n, cands):
    for cand in cands:
        if n % cand == 0:
            return cand
    return n


def _tn_matmul(x, y, name, col_shards=False, square_x=False, rider=None):
    rows, k1 = x.shape
    k2 = y.shape[1]
    bt = _pick(rows, (1024, 768, 512, 384, 256, 128))
    bk1, bk2 = min(k1, 1024), min(k2, 1024)
    grid = (k1 // bk1, k2 // bk2, rows // bt)
    nt = rows // bt
    n = k2 // N_DEV
    if col_shards:
        assert bk2 % n == 0
        per = bk2 // n
        out_shape = jax.ShapeDtypeStruct((N_DEV, k1, n), BF16)
        out_spec = pl.BlockSpec((per, bk1, n), lambda i, j, t: (j, i, 0))
    else:
        out_shape = jax.ShapeDtypeStruct((k1, k2), BF16)
        out_spec = pl.BlockSpec((bk1, bk2), lambda i, j, t: (i, j))

    def body(x_ref, y_ref, o_ref, acc_ref):
        t = pl.program_id(2)

        @pl.when(t == 0)
        def _():
            acc_ref[...] = jnp.zeros_like(acc_ref)
        xv = x_ref[...]
        acc_ref[...] += _dot_tn(xv * xv if square_x else xv, y_ref[...])

        @pl.when(t == nt - 1)
        def _():
            if col_shards:
                for s in range(per):
                    o_ref[s] = acc_ref[:, s * n:(s + 1) * n].astype(BF16)
            else:
                o_ref[...] = acc_ref[...].astype(BF16)

    (out,), rode = _compute_call(
        body, name=name, grid=grid, operands=(x, y), rider=rider, out_shape=[out_shape],
        in_specs=[pl.BlockSpec((bt, bk1), lambda i, j, t: (t, i)), pl.BlockSpec((bt, bk2), lambda i, j, t: (t, j))],
        out_specs=[out_spec], scratch_shapes=[pltpu.VMEM((bk1, bk2), F32)])
    return out if rider is None else (out, rode)


def _pool_bands(tm):
    k = tm + 128
    t = np.arange(tm)[:, None]
    e = np.arange(k)[None, :]
    fwd, bwd = [], []
    for w in POOL_WINDOWS:
        lo = POOL_HALO + t - w // 2
        fwd.append(((e >= lo) & (e <= lo + w - 1)).astype(np.float32))
        lo_t = POOL_HALO + t - w // 2 + 1
        bwd.append(((e >= lo_t) & (e <= lo_t + w - 1)).astype(np.float32))
    return jnp.asarray(np.stack(fwd), BF16), jnp.asarray(np.stack(bwd), BF16)


def _pool_geometry(i, nct, n_tiles, tm, c_len, l_len):
    if nct == 0:
        pos0 = i * tm
        ls = l_len
        has_prev = i > 0
        has_next = i < n_tiles - 1
    else:
        in_ctx = i < nct
        pos0 = jnp.where(in_ctx, i, i - nct) * tm
        ls = jnp.where(in_ctx, c_len, l_len)
        has_prev = jnp.logical_and(i != 0, i != nct)
        has_next = jnp.logical_and(i != nct - 1, i != n_tiles - 1)
    return pos0, ls, has_prev, has_next


def _window_inv_counts(pos, ls):
    out = []
    for w in POOL_WINDOWS:
        lo = jnp.maximum(pos - w // 2, 0)
        hi = jnp.minimum(pos + w - w // 2, ls)
        cnt = jnp.maximum(hi - lo, 1).astype(F32)
        out.append(1.0 / cnt)
    return out


def _split_bf16(x):
    hi = x.astype(BF16)
    return hi, (x - hi.astype(F32)).astype(BF16)


def _extend(prev, tile, nxt, has_prev, has_next):
    w = tile.shape[1]
    prev = jnp.where(has_prev, prev, 0.0)
    nxt = jnp.where(has_next, nxt, 0.0)
    return jnp.concatenate([prev, tile, nxt, jnp.zeros((128 - 2 * POOL_HALO, w), F32)], axis=0)


def _pool_specs(tm, d, n_rows):
    last8 = n_rows // POOL_HALO - 1
    per = tm // POOL_HALO
    return [pl.BlockSpec((tm, d), lambda i: (i, 0)),
            pl.BlockSpec((POOL_HALO, d), lambda i: (jnp.maximum(i * per - 1, 0), 0)),
            pl.BlockSpec((POOL_HALO, d), lambda i: (jnp.minimum((i + 1) * per, last8), 0))]


def _pool_fwd(h, mods, ng, w, scale, bands, nct, tm, c_len, l_len):
    r, d = h.shape
    gw = d // POOL_GROUPS
    n_tiles = r // tm
    kx = tm + 128

    def body(h_ref, hp_ref, hn_ref, mod_ref, ng_ref, w_ref, sc_ref, band_ref, y_ref, h1_ref):
        i = pl.program_id(0)
        pos0, ls, has_prev, has_next = _pool_geometry(i, nct, n_tiles, tm, c_len, l_len)
        ngv, sh, sc = ng_ref[...], mod_ref[0:1, :], mod_ref[1:2, :]
        h = h_ref[...]
        a = _normmod(h, ngv, sh, sc)[0]
        a_ext = _extend(_normmod(hp_ref[...], ngv, sh, sc)[0], a, _normmod(hn_ref[...], ngv, sh, sc)[0],
                        has_prev, has_next)
        pos = pos0 + lax.broadcasted_iota(jnp.int32, (tm, 1), 0)
        inv = _window_inv_counts(pos, ls)
        ys = []
        for g in range(POOL_GROUPS):
            cols = slice(g * gw, (g + 1) * gw)
            hi, lo = _split_bf16(a_ext[:, cols])
            s = _dot(band_ref[g], hi) + _dot(band_ref[g], lo)
            pg = s * inv[g] - a[:, cols]
            ys.append(_dot(pg.astype(BF16), w_ref[g]))
        y = jnp.concatenate(ys, axis=1) * sc_ref[...]
        y_ref[...] = y.astype(BF16)
        h1_ref[...] = h + mod_ref[2:3, :] * y

    return pl.pallas_call(
        body, name="pool_fwd", grid=(n_tiles,),
        out_shape=[jax.ShapeDtypeStruct((r, d), BF16), jax.ShapeDtypeStruct((r, d), F32)],
        in_specs=_pool_specs(tm, d, r) + [
            pl.BlockSpec((None, 8, d), lambda i: (_sid(i, nct), 0, 0)),
            _const_spec((1, d)), _const_spec(w.shape), _const_spec((1, d)), _const_spec((4, tm, kx))],
        out_specs=[pl.BlockSpec((tm, d), lambda i: (i, 0)), pl.BlockSpec((tm, d), lambda i: (i, 0))],
        compiler_params=_params(("arbitrary",)),
    )(h, h, h, mods, ng, w, scale, bands[0])


def _pool_bwd(dh1, h, y, mods, ng, w, scale, bands, nct, tm, c_len, l_len, latent_out, rider=None):
    r, d = h.shape
    gw = d // POOL_GROUPS
    n_tiles = r // tm
    kx = tm + 128
    out_rows = l_len if latent_out else r
    out_off = nct if latent_out else 0

    def body(dh_ref, dhp_ref, dhn_ref, h_ref, hp_ref, hn_ref, y_ref, mod_ref, ng_ref, w_ref, sc_ref,
             bf_ref, bb_ref, dho_ref, dw_ref, st_ref):
        i = pl.program_id(0)
        pos0, ls, has_prev, has_next = _pool_geometry(i, nct, n_tiles, tm, c_len, l_len)
        ngv, sh, sc, gate = ng_ref[...], mod_ref[0:1, :], mod_ref[1:2, :], mod_ref[2:3, :]
        scale_v = sc_ref[...]
        h = h_ref[...]
        a, xhat, rstd, n = _normmod(h, ngv, sh, sc)
        a_ext = _extend(_normmod(hp_ref[...], ngv, sh, sc)[0], a, _normmod(hn_ref[...], ngv, sh, sc)[0],
                        has_prev, has_next)
        dh = dh_ref[...]
        dgate = _colsum(dh * y_ref[...].astype(F32))
        dy = gate * dh
        dy_ext = _extend(gate * dhp_ref[...], dy, gate * dhn_ref[...], has_prev, has_next)
        dyp_ext = (dy_ext * scale_v).astype(BF16)
        dyp = (dy * scale_v).astype(BF16)
        pos = pos0 + lax.broadcasted_iota(jnp.int32, (tm, 1), 0)
        inv = _window_inv_counts(pos, ls)
        pos_e = pos0 - POOL_HALO + lax.broadcasted_iota(jnp.int32, (kx, 1), 0)
        inv_e = _window_inv_counts(pos_e, ls)

        @pl.when(i == 0)
        def _():
            dw_ref[...] = jnp.zeros_like(dw_ref)
        das, dscale = [], []
        for g in range(POOL_GROUPS):
            cols = slice(g * gw, (g + 1) * gw)
            hi, lo = _split_bf16(a_ext[:, cols])
            pg = ((_dot(bf_ref[g], hi) + _dot(bf_ref[g], lo)) * inv[g] - a[:, cols]).astype(BF16)
            dscale.append(_colsum(dy[:, cols] * _dot(pg, w_ref[g])))
            dyp_g = dyp_ext[:, cols]
            dw_ref[g] += _dot_tn(pg, dyp[:, cols])
            dp_ext = _dot_nt(dyp_g, w_ref[g])
            hi, lo = _split_bf16(dp_ext * inv_e[g])
            das.append(_dot(bb_ref[g], hi) + _dot(bb_ref[g], lo) - dp_ext[POOL_HALO:POOL_HALO + tm, :])
        da = jnp.concatenate(das, axis=1)
        dhn, dsh, dsc, dng = _normmod_bwd(da, xhat, rstd, n, ngv, sc)
        dho_ref[...] = dh + dhn
        _acc_rows(st_ref, _first_of_stream(i, nct), [dsh, dsc, dgate, dng, jnp.concatenate(dscale, axis=1)])

    outs, rode = _compute_call(
        body, name="pool_bwd", grid=(n_tiles,), rider=rider,
        operands=(dh1, dh1, dh1, h, h, h, y, mods, ng, w, scale, bands[0], bands[1]),
        out_shape=[jax.ShapeDtypeStruct((out_rows, d), F32),
                   jax.ShapeDtypeStruct((POOL_GROUPS, gw, gw), F32),
                   jax.ShapeDtypeStruct((_n_streams(nct), 8, d), F32)],
        in_specs=_pool_specs(tm, d, r) + _pool_specs(tm, d, r) + [
            pl.BlockSpec((tm, d), lambda i: (i, 0)),
            pl.BlockSpec((None, 8, d), lambda i: (_sid(i, nct), 0, 0)),
            _const_spec((1, d)), _const_spec(w.shape), _const_spec((1, d)),
            _const_spec((4, tm, kx)), _const_spec((4, tm, kx))],
        out_specs=[pl.BlockSpec((tm, d), lambda i: (jnp.maximum(i - out_off, 0), 0)),
                   pl.BlockSpec((POOL_GROUPS, gw, gw), lambda i: (0, 0, 0)),
                   pl.BlockSpec((None, 8, d), lambda i: (_stat_sid(i, nct), 0, 0))])
    return (*outs, rode)


def _rope_tables(c_len, l_len):
    half = HEAD_DIM // 2
    t = np.arange(l_len)
    row = (t // GRID_W).astype(np.float32)
    col = (t % GRID_W).astype(np.float32)
    inv = (np.float32(ROPE_BASE) ** (-np.arange(0, half, 2, dtype=np.float32) / np.float32(half))).astype(np.float32)
    ang_r = row[:, None] * inv[None, :]
    ang_c = col[:, None] * inv[None, :]
    cos = np.concatenate([np.cos(ang_r), np.cos(ang_r), np.cos(ang_c), np.cos(ang_c)], axis=1)
    sin = np.concatenate([-np.sin(ang_r), np.sin(ang_r), -np.sin(ang_c), np.sin(ang_c)], axis=1)
    cos = np.concatenate([np.ones((c_len, HEAD_DIM), np.float32), cos.astype(np.float32)], axis=0)
    sin = np.concatenate([np.zeros((c_len, HEAD_DIM), np.float32), sin.astype(np.float32)], axis=0)
    return jnp.asarray(cos, F32), jnp.asarray(sin, F32)


def _swap_pairs(x):
    lane = lax.broadcasted_iota(jnp.int32, x.shape, 1)
    return jnp.where((lane % 64) < 32, pltpu.roll(x, 96, 1), pltpu.roll(x, 32, 1))


def _head_norm(x, g):
    rstd = lax.rsqrt(jnp.mean(x * x, axis=-1, keepdims=True) + EPS)
    xhat = x * rstd
    return xhat * g, xhat, rstd


def _qkv_fwd(h, mods, ng, w, qg, kg, cos, sin, n_heads, n_kv, nct, tm):
    t_rows, d = h.shape
    qw, kw = n_heads * HEAD_DIM, n_kv * HEAD_DIM

    def body(h_ref, mod_ref, ng_ref, w_ref, qg_ref, kg_ref, cos_ref, sin_ref, q_ref, k_ref, v_ref, qt_ref):
        a = _normmod(h_ref[...], ng_ref[...], mod_ref[0:1, :], mod_ref[1:2, :])[0]
        qkv = _dot(a.astype(BF16), w_ref[...])
        cosv, sinv = cos_ref[...], sin_ref[...]
        ones = jnp.ones((tm, HEAD_DIM), BF16)
        for hd in range(n_heads + n_kv):
            cols = slice(hd * HEAD_DIM, (hd + 1) * HEAD_DIM)
            xn = _head_norm(qkv[:, cols], qg_ref[...] if hd < n_heads else kg_ref[...])[0]
            xr = xn * cosv + _swap_pairs(xn) * sinv
            if hd < n_heads:
                qs = xr * Q_SCALE
                q_ref[:, cols] = qs.astype(BF16)
                qt_ref[cols, :] = qs.T.astype(BF16)
            else:
                k_ref[:, (hd - n_heads) * HEAD_DIM:(hd - n_heads + 1) * HEAD_DIM] = xr.astype(BF16)
        for g in range(n_kv):
            v_ref[:, (2 * g) * HEAD_DIM:(2 * g + 1) * HEAD_DIM] = (
                qkv[:, qw + kw + g * HEAD_DIM:qw + kw + (g + 1) * HEAD_DIM].astype(BF16))
            v_ref[:, (2 * g + 1) * HEAD_DIM:(2 * g + 2) * HEAD_DIM] = ones

    return pl.pallas_call(
        body, name="qkv_fwd", grid=(t_rows // tm,),
        out_shape=[jax.ShapeDtypeStruct((t_rows - nct * tm, qw), BF16), jax.ShapeDtypeStruct((t_rows, kw), BF16),
                   jax.ShapeDtypeStruct((t_rows, 2 * kw), BF16), jax.ShapeDtypeStruct((qw, t_rows - nct * tm), BF16)],
        in_specs=[pl.BlockSpec((tm, d), lambda i: (i, 0)),
                  pl.BlockSpec((None, 8, d), lambda i: (_sid(i, nct), 0, 0)),
                  _const_spec((1, d)), _const_spec(w.shape), _const_spec((1, HEAD_DIM)),
                  _const_spec((1, HEAD_DIM)),
                  pl.BlockSpec((tm, HEAD_DIM), lambda i: (i, 0)), pl.BlockSpec((tm, HEAD_DIM), lambda i: (i, 0))],
        out_specs=[pl.BlockSpec((tm, qw), lambda i: (jnp.maximum(i - nct, 0), 0)),
                   pl.BlockSpec((tm, kw), lambda i: (i, 0)), pl.BlockSpec((tm, 2 * kw), lambda i: (i, 0)),
                   pl.BlockSpec((qw, tm), lambda i: (0, jnp.maximum(i - nct, 0)))],
        compiler_params=_params(("arbitrary",)),
    )(h, mods, ng, w, qg, kg, cos, sin)


def _flash_tk(t_rows, tm):
    best = tm
    k = tm
    while k <= FLASH_TK_CAP:
        if t_rows % k == 0:
            best = k
        k += tm
    return best


def _flash_fwd(q, k, v1, n_kv, tq, tm, rider=None):
    t_rows = k.shape[0]
    l_rows = q.shape[0]
    tk = _flash_tk(t_rows, tm)
    nk = t_rows // tk
    gq = 2 * HEAD_DIM

    def body(q_ref, k_ref, v_ref, o_ref, lse_ref, m_s, acc_s, s_s):
        ki = pl.program_id(2)

        @pl.when(ki == 0)
        def _():
            m_s[...] = jnp.full_like(m_s, -jnp.inf)
            acc_s[...] = jnp.zeros_like(acc_s)
        kk, vv = k_ref[...], v_ref[...]
        for hh in range(2):
            s_s[hh] = _dot_nt(q_ref[:, hh * HEAD_DIM:(hh + 1) * HEAD_DIM], kk)
        for hh in range(2):
            s = s_s[hh]
            m_prev = m_s[hh]
            m_new = jnp.maximum(m_prev, jnp.max(s, axis=-1, keepdims=True))
            alpha = jnp.exp2(m_prev - m_new)
            p = jnp.exp2(s - jnp.tile(m_new, (1, tk // HEAD_DIM)))
            acc_s[hh] = jnp.tile(alpha, (1, 2)) * acc_s[hh] + _dot(p.astype(BF16), vv)
            m_s[hh] = m_new

        @pl.when(ki == nk - 1)
        def _():
            for hh in range(2):
                acc = acc_s[hh]
                l = acc[:, HEAD_DIM:]
                o_ref[:, hh * HEAD_DIM:(hh + 1) * HEAD_DIM] = (acc[:, :HEAD_DIM] / l).astype(BF16)
                lse_ref[:, hh:hh + 1] = (m_s[hh] + jnp.log2(l))[:, 0:1]

    (o, lse), rode = _compute_call(
        body, name="flash_fwd", grid=(n_kv, l_rows // tq, nk), operands=(q, k, v1), rider=rider,
        out_shape=[jax.ShapeDtypeStruct((l_rows, n_kv * gq), BF16),
                   jax.ShapeDtypeStruct((n_kv, l_rows, 2), F32)],
        in_specs=[pl.BlockSpec((tq, gq), lambda g, i, j: (i, g)),
                  pl.BlockSpec((tk, HEAD_DIM), lambda g, i, j: (j, g)),
                  pl.BlockSpec((tk, gq), lambda g, i, j: (j, g))],
        out_specs=[pl.BlockSpec((tq, gq), lambda g, i, j: (i, g)),
                   pl.BlockSpec((None, tq, 2), lambda g, i, j: (g, i, 0))],
        scratch_shapes=[pltpu.VMEM((2, tq, HEAD_DIM), F32), pltpu.VMEM((2, tq, gq), F32),
                        pltpu.VMEM((2, tq, tk), F32)])
    return o, lse, rode


def _flash_bwd(q, qt, k, v1, do, dot, lse, delta, n_kv, tq, tm, rider=None):
    t_rows = k.shape[0]
    l_rows = q.shape[0]
    tk = _flash_tk(t_rows, tm)
    nq = l_rows // tq
    gq = 2 * HEAD_DIM

    def body(q_ref, qt_ref, k_ref, v_ref, do_ref, dot_ref, lse_ref, dl_ref, dq_ref, dkt_ref, dvt_ref):
        ki, qi = pl.program_id(1), pl.program_id(2)
        rows = pl.ds(pl.multiple_of(qi * tq, tq), tq)

        @pl.when(qi == 0)
        def _():
            dkt_ref[...] = jnp.zeros_like(dkt_ref)
            dvt_ref[...] = jnp.zeros_like(dvt_ref)

        @pl.when(ki == 0)
        def _():
            dq_ref[rows, :] = jnp.zeros((tq, gq), F32)
        kk, vv = k_ref[...], v_ref[:, :HEAD_DIM]
        dkt_parts, dvt_parts = [], []
        for hh in range(2):
            cols = slice(hh * HEAD_DIM, (hh + 1) * HEAD_DIM)
            p = jnp.exp2(_dot_nt(q_ref[:, cols], kk) - lse_ref[:, hh:hh + 1])
            ds = (p * (_dot_nt(do_ref[:, cols], vv) - dl_ref[:, hh:hh + 1])).astype(BF16)
            dvt_parts.append(_dot(dot_ref[cols, :], p.astype(BF16)))
            dkt_parts.append(_dot(qt_ref[cols, :], ds))
            dq_ref[rows, cols] += _dot(ds, kk)
        dvt_ref[...] += dvt_parts[0] + dvt_parts[1]
        dkt_ref[...] += dkt_parts[0] + dkt_parts[1]

    (dq, dkt, dvt), rode = _compute_call(
        body, name="flash_bwd", grid=(n_kv, t_rows // tk, nq), operands=(q, qt, k, v1, do, dot, lse, delta),
        rider=rider,
        out_shape=[jax.ShapeDtypeStruct((l_rows, n_kv * gq), F32),
                   jax.ShapeDtypeStruct((n_kv * HEAD_DIM, t_rows), F32),
                   jax.ShapeDtypeStruct((n_kv * HEAD_DIM, t_rows), F32)],
        in_specs=[pl.BlockSpec((tq, gq), lambda g, j, i: (i, g)),
                  pl.BlockSpec((gq, tq), lambda g, j, i: (g, i)),
                  pl.BlockSpec((tk, HEAD_DIM), lambda g, j, i: (j, g)),
                  pl.BlockSpec((tk, gq), lambda g, j, i: (j, g)),
                  pl.BlockSpec((tq, gq), lambda g, j, i: (i, g)),
                  pl.BlockSpec((gq, tq), lambda g, j, i: (g, i)),
                  pl.BlockSpec((None, tq, 2), lambda g, j, i: (g, i, 0)),
                  pl.BlockSpec((None, tq, 2), lambda g, j, i: (g, i, 0))],
        out_specs=[pl.BlockSpec((l_rows, gq), lambda g, j, i: (0, g)),
                   pl.BlockSpec((HEAD_DIM, tk), lambda g, j, i: (g, j)),
                   pl.BlockSpec((HEAD_DIM, tk), lambda g, j, i: (g, j))])
    return dq, dkt, dvt, rode


def _wo_fwd(h, o, wo, mods, nct, tm, rider=None):
    l_rows, z = o.shape
    d = h.shape[1]

    def body(h_ref, o_ref, w_ref, mod_ref, y_ref, h1_ref):
        y = _dot(o_ref[...], w_ref[...])
        y_ref[...] = y.astype(BF16)
        h1_ref[...] = h_ref[...] + mod_ref[2:3, :] * y

    (y, h1), rode = _compute_call(
        body, name="wo_fwd", grid=(l_rows // tm,), operands=(h, o, wo, mods), rider=rider,
        out_shape=[jax.ShapeDtypeStruct((l_rows, d), BF16), jax.ShapeDtypeStruct((l_rows, d), F32)],
        in_specs=[pl.BlockSpec((tm, d), lambda i: (i + nct, 0)), pl.BlockSpec((tm, z), lambda i: (i, 0)),
                  _const_spec(wo.shape), pl.BlockSpec((None, 8, d), lambda i: (1, 0, 0))],
        out_specs=[pl.BlockSpec((tm, d), lambda i: (i, 0)), pl.BlockSpec((tm, d), lambda i: (i, 0))])
    return y, h1, rode


def _wo_bwd(dh1, y, o, wo, mods, n_kv, tm):
    l_rows, z = o.shape
    d = dh1.shape[1]

    def body(dh_ref, y_ref, o_ref, w_ref, mod_ref, dy_ref, do_ref, dot_ref, dl_ref, st_ref):
        i = pl.program_id(0)
        dh = dh_ref[...]
        dgate = _colsum(dh * y_ref[...].astype(F32))
        dy = (mod_ref[2:3, :] * dh).astype(BF16)
        dy_ref[...] = dy
        do = _dot_nt(dy, w_ref[...])
        do_ref[...] = do.astype(BF16)
        dot_ref[...] = do.T.astype(BF16)
        prod = do * o_ref[...].astype(F32)
        for g in range(n_kv):
            d0 = jnp.sum(prod[:, (2 * g) * HEAD_DIM:(2 * g + 1) * HEAD_DIM], axis=-1, keepdims=True)
            d1 = jnp.sum(prod[:, (2 * g + 1) * HEAD_DIM:(2 * g + 2) * HEAD_DIM], axis=-1, keepdims=True)
            dl_ref[g] = jnp.concatenate([d0, d1], axis=1)
        zero = jnp.zeros((1, d), F32)
        _acc_rows(st_ref, i == 0, [zero, zero, dgate])

    return pl.pallas_call(
        body, name="wo_bwd", grid=(l_rows // tm,),
        out_shape=[jax.ShapeDtypeStruct((l_rows, d), BF16), jax.ShapeDtypeStruct((l_rows, z), BF16),
                   jax.ShapeDtypeStruct((z, l_rows), BF16),
                   jax.ShapeDtypeStruct((n_kv, l_rows, 2), F32), jax.ShapeDtypeStruct((8, d), F32)],
        in_specs=[pl.BlockSpec((tm, d), lambda i: (i, 0)), pl.BlockSpec((tm, d), lambda i: (i, 0)),
                  pl.BlockSpec((tm, z), lambda i: (i, 0)), _const_spec(wo.shape),
                  pl.BlockSpec((None, 8, d), lambda i: (1, 0, 0))],
        out_specs=[pl.BlockSpec((tm, d), lambda i: (i, 0)), pl.BlockSpec((tm, z), lambda i: (i, 0)),
                   pl.BlockSpec((z, tm), lambda i: (0, i)),
                   pl.BlockSpec((n_kv, tm, 2), lambda i: (0, i, 0)), pl.BlockSpec((8, d), lambda i: (0, 0))],
        compiler_params=_params(("arbitrary",)),
    )(dh1, y, o, wo, mods)


def _qkv_bwd(h, dh_lat, dq, dkt, dvt, mods, ng, w, qg, kg, cos, sin, n_heads, n_kv, nct, tm):
    t_rows, d = h.shape
    qw, kw = n_heads * HEAD_DIM, n_kv * HEAD_DIM
    scale = HEAD_DIM ** -0.5

    def body(h_ref, dhl_ref, dq_ref, dkt_ref, dvt_ref, mod_ref, ng_ref, w_ref, qg_ref, kg_ref, cos_ref,
             sin_ref, dh_ref, a_ref, dqkv_ref, st_ref, dg_ref):
        i = pl.program_id(0)
        lat = (i >= nct).astype(F32)
        dk_t = dkt_ref[...].T * (1.0 / LOG2E)
        ngv, sc = ng_ref[...], mod_ref[1:2, :]
        a, xhat, rstd, n = _normmod(h_ref[...], ngv, mod_ref[0:1, :], sc)
        ab = a.astype(BF16)
        a_ref[...] = ab
        qkv = _dot(ab, w_ref[...])
        cosv, sinv = cos_ref[...], sin_ref[...]
        dqg = jnp.zeros((1, HEAD_DIM), F32)
        dkg = jnp.zeros((1, HEAD_DIM), F32)
        for hd in range(n_heads + n_kv):
            cols = slice(hd * HEAD_DIM, (hd + 1) * HEAD_DIM)
            is_q = hd < n_heads
            g = qg_ref[...] if is_q else kg_ref[...]
            _, hx, hr = _head_norm(qkv[:, cols], g)
            if is_q:
                dxr = dq_ref[:, cols] * (scale * lat)
            else:
                dxr = dk_t[:, (hd - n_heads) * HEAD_DIM:(hd - n_heads + 1) * HEAD_DIM]
            dxn = dxr * cosv + _swap_pairs(dxr * sinv)
            if is_q:
                dqg = dqg + _colsum(dxn * hx)
            else:
                dkg = dkg + _colsum(dxn * hx)
            dxh = dxn * g
            dx = hr * (dxh - hx * jnp.mean(dxh * hx, axis=-1, keepdims=True))
            dqkv_ref[:, cols] = dx.astype(BF16)
        dqkv_ref[:, qw + kw:] = dvt_ref[...].T.astype(BF16)
        da = _dot_nt(dqkv_ref[...], w_ref[...])
        dhn, dsh, dsc, dng = _normmod_bwd(da, xhat, rstd, n, ngv, sc)
        dh_ref[...] = dhl_ref[...] * lat + dhn
        _acc_rows(st_ref, _first_of_stream(i, nct), [dsh, dsc, jnp.zeros((1, d), F32), dng])
        _acc_rows(dg_ref, i == 0, [dqg, dkg])

    lat_map = lambda i: (jnp.maximum(i - nct, 0), 0)
    return pl.pallas_call(
        body, name="qkv_bwd", grid=(t_rows // tm,),
        out_shape=[jax.ShapeDtypeStruct((t_rows, d), F32), jax.ShapeDtypeStruct((t_rows, d), BF16),
                   jax.ShapeDtypeStruct((t_rows, qw + 2 * kw), BF16), jax.ShapeDtypeStruct((2, 8, d), F32),
                   jax.ShapeDtypeStruct((8, HEAD_DIM), F32)],
        in_specs=[pl.BlockSpec((tm, d), lambda i: (i, 0)), pl.BlockSpec((tm, d), lat_map),
                  pl.BlockSpec((tm, qw), lat_map), pl.BlockSpec((kw, tm), lambda i: (0, i)),
                  pl.BlockSpec((kw, tm), lambda i: (0, i)),
                  pl.BlockSpec((None, 8, d), lambda i: (_sid(i, nct), 0, 0)),
                  _const_spec((1, d)), _const_spec(w.shape), _const_spec((1, HEAD_DIM)),
                  _const_spec((1, HEAD_DIM)),
                  pl.BlockSpec((tm, HEAD_DIM), lambda i: (i, 0)), pl.BlockSpec((tm, HEAD_DIM), lambda i: (i, 0))],
        out_specs=[pl.BlockSpec((tm, d), lambda i: (i, 0)), pl.BlockSpec((tm, d), lambda i: (i, 0)),
                   pl.BlockSpec((tm, qw + 2 * kw), lambda i: (i, 0)),
                   pl.BlockSpec((None, 8, d), lambda i: (_sid(i, nct), 0, 0)),
                   pl.BlockSpec((8, HEAD_DIM), lambda i: (0, 0))],
        compiler_params=_params(("arbitrary",)),
    )(h, dh_lat, dq, dkt, dvt, mods, ng, w, qg, kg, cos, sin)


def _gmlp_core(a_bf, win_ref, lng, lnb, ws_ref, bst_ref, tm, half, with_grad=False):
    blocks = [_dot(a_bf, win_ref[j]) for j in range(N_DEV)]
    zu = jnp.concatenate(blocks[:N_DEV // 2], axis=1)
    zv = jnp.concatenate(blocks[N_DEV // 2:], axis=1)
    if with_grad:
        (u, zu), (v, zv) = _gelu_and_grad(zu), _gelu_and_grad(zv)
    else:
        u, v, zu, zv = _gelu(zu), _gelu(zv), None, None
    mu = jnp.mean(v, axis=-1, keepdims=True)
    vc = v - mu
    rstd_v = lax.rsqrt(jnp.mean(vc * vc, axis=-1, keepdims=True) + EPS)
    vhat = vc * rstd_v
    vln = (vhat * lng + lnb).astype(BF16)
    gw = half // GMLP_GROUPS
    rows = []
    for ch in range(tm // CHUNK):
        rs = slice(ch * CHUNK, (ch + 1) * CHUNK)
        cols = []
        for g in range(GMLP_GROUPS):
            cs = slice(g * gw, (g + 1) * gw)
            cols.append(_dot(ws_ref[g], vln[rs, cs]) + bst_ref[:, g:g + 1])
        rows.append(jnp.concatenate(cols, axis=1))
    sv = rows[0] if len(rows) == 1 else jnp.concatenate(rows, axis=0)
    return zu, zv, u, vhat, rstd_v, vln, sv


def _gmlp_fwd(h, mods, ng, win, lng, lnb, ws, bst, wout, tm):
    l_rows, d = h.shape
    half = wout.shape[0]

    def body(h_ref, mod_ref, ng_ref, win_ref, lng_ref, lnb_ref, ws_ref, bst_ref, wout_ref, y_ref, h1_ref):
        hv = h_ref[...]
        a = _normmod(hv, ng_ref[...], mod_ref[0:1, :], mod_ref[1:2, :])[0]
        _, _, u, _, _, _, sv = _gmlp_core(a.astype(BF16), win_ref, lng_ref[...], lnb_ref[...], ws_ref,
                                          bst_ref, tm, half)
        y = _dot((u * sv).astype(BF16), wout_ref[...])
        y_ref[...] = y.astype(BF16)
        h1_ref[...] = hv + mod_ref[2:3, :] * y

    return pl.pallas_call(
        body, name="gmlp_fwd", grid=(l_rows // tm,),
        out_shape=[jax.ShapeDtypeStruct((l_rows, d), BF16), jax.ShapeDtypeStruct((l_rows, d), F32)],
        in_specs=[pl.BlockSpec((tm, d), lambda i: (i, 0)), pl.BlockSpec((None, 8, d), lambda i: (1, 0, 0)),
                  _const_spec((1, d)), _const_spec(win.shape), _const_spec((1, half)), _const_spec((1, half)),
                  _const_spec(ws.shape), _const_spec(bst.shape), _const_spec(wout.shape)],
        out_specs=[pl.BlockSpec((tm, d), lambda i: (i, 0)), pl.BlockSpec((tm, d), lambda i: (i, 0))],
        compiler_params=_params(("arbitrary",)),
    )(h, mods, ng, win, lng, lnb, ws, bst, wout)


def _gmlp_bwd(dh1, h, y, mods, ng, win, lng, lnb, ws, wst, bst, wout, tm):
    l_rows, d = h.shape
    half = wout.shape[0]
    gw = half // GMLP_GROUPS

    def body(dh_ref, h_ref, y_ref, mod_ref, ng_ref, win_ref, lng_ref, lnb_ref, ws_ref, wst_ref, bst_ref,
             wout_ref, dho_ref, a_ref, dz_ref, gt_ref, dy_ref, st_ref, ln_ref, dws_ref, dbs_ref):
        i = pl.program_id(0)
        ngv, sc = ng_ref[...], mod_ref[1:2, :]
        lngv = lng_ref[...]
        a, xhat, rstd, n = _normmod(h_ref[...], ngv, mod_ref[0:1, :], sc)
        ab = a.astype(BF16)
        a_ref[...] = ab
        gu, gv, u, vhat, rstd_v, vln, sv = _gmlp_core(ab, win_ref, lngv, lnb_ref[...], ws_ref, bst_ref,
                                                      tm, half, with_grad=True)
        gt_ref[...] = (u * sv).astype(BF16)
        dh = dh_ref[...]
        dgate = _colsum(dh * y_ref[...].astype(F32))
        dy = (mod_ref[2:3, :] * dh).astype(BF16)
        dy_ref[...] = dy
        dgated = _dot_nt(dy, wout_ref[...])
        du = dgated * sv
        dsv = (dgated * u).astype(BF16)

        @pl.when(i == 0)
        def _():
            dws_ref[...] = jnp.zeros_like(dws_ref)
            dbs_ref[...] = jnp.zeros_like(dbs_ref)
        lane = lax.broadcasted_iota(jnp.int32, (CHUNK, 128), 1)
        dbs = jnp.zeros((CHUNK, 128), F32)
        rows = []
        for ch in range(tm // CHUNK):
            rs = slice(ch * CHUNK, (ch + 1) * CHUNK)
            cols = []
            for g in range(GMLP_GROUPS):
                cs = slice(g * gw, (g + 1) * gw)
                dsv_cg = dsv[rs, cs]
                dws_ref[g] += _dot_nt(dsv_cg, vln[rs, cs])
                cols.append(_dot(wst_ref[g], dsv_cg))
                dbs = dbs + jnp.where(lane == g, jnp.sum(dsv_cg.astype(F32), axis=-1, keepdims=True), 0.0)
            rows.append(jnp.concatenate(cols, axis=1))
        dbs_ref[...] += dbs
        dvln = rows[0] if len(rows) == 1 else jnp.concatenate(rows, axis=0)
        dlng = _colsum(dvln * vhat)
        dlnb = _colsum(dvln)
        dvh = dvln * lngv
        dv = rstd_v * (dvh - jnp.mean(dvh, axis=-1, keepdims=True)
                       - vhat * jnp.mean(dvh * vhat, axis=-1, keepdims=True))
        dz_ref[:, :half] = (du * gu).astype(BF16)
        dz_ref[:, half:] = (dv * gv).astype(BF16)
        nb = 2 * half // N_DEV
        da = _dot_nt(dz_ref[:, 0:nb], win_ref[0])
        for j in range(1, N_DEV):
            da = da + _dot_nt(dz_ref[:, j * nb:(j + 1) * nb], win_ref[j])
        dhn, dsh, dsc, dng = _normmod_bwd(da, xhat, rstd, n, ngv, sc)
        dho_ref[...] = dh + dhn
        _acc_rows(st_ref, i == 0, [dsh, dsc, dgate, dng])
        _acc_rows(ln_ref, i == 0, [dlng, dlnb])

    row = lambda w: pl.BlockSpec((tm, w), lambda i: (i, 0))
    return pl.pallas_call(
        body, name="gmlp_bwd", grid=(l_rows // tm,),
        out_shape=[jax.ShapeDtypeStruct((l_rows, d), F32), jax.ShapeDtypeStruct((l_rows, d), BF16),
                   jax.ShapeDtypeStruct((l_rows, 2 * half), BF16), jax.ShapeDtypeStruct((l_rows, half), BF16),
                   jax.ShapeDtypeStruct((l_rows, d), BF16), jax.ShapeDtypeStruct((8, d), F32),
                   jax.ShapeDtypeStruct((8, half), F32), jax.ShapeDtypeStruct(ws.shape, F32),
                   jax.ShapeDtypeStruct((CHUNK, 128), F32)],
        in_specs=[row(d), row(d), row(d), pl.BlockSpec((None, 8, d), lambda i: (1, 0, 0)),
                  _const_spec((1, d)), _const_spec(win.shape), _const_spec((1, half)), _const_spec((1, half)),
                  _const_spec(ws.shape), _const_spec(ws.shape), _const_spec(bst.shape), _const_spec(wout.shape)],
        out_specs=[row(d), row(d), row(2 * half), row(half), row(d),
                   pl.BlockSpec((8, d), lambda i: (0, 0)), pl.BlockSpec((8, half), lambda i: (0, 0)),
                   pl.BlockSpec(ws.shape, lambda i: (0, 0, 0)), pl.BlockSpec((CHUNK, 128), lambda i: (0, 0))],
        compiler_params=_params(("arbitrary",)),
    )(dh1, h, y, mods, ng, win, lng, lnb, ws, wst, bst, wout)


def _head(h, final_g, target, tm):
    l_rows, d = h.shape
    n_tiles = l_rows // tm

    def body(h_ref, g_ref, t_ref, dh_ref, loss_ref, dg_ref, acc_ref):
        i = pl.program_id(0)
        g = g_ref[...]
        hv = h_ref[...]
        rstd = lax.rsqrt(jnp.mean(hv * hv, axis=-1, keepdims=True) + EPS)
        xhat = hv * rstd
        e = xhat * g - t_ref[...]
        dout = e * (1.0 / d)
        dxhat = dout * g
        dh_ref[...] = rstd * (dxhat - xhat * jnp.mean(dxhat * xhat, axis=-1, keepdims=True))
        _acc_rows(dg_ref, i == 0, [_colsum(dout * xhat)])
        _acc_rows(acc_ref, i == 0, [_colsum(e * e)])

        @pl.when(i == n_tiles - 1)
        def _():
            total = jnp.sum(acc_ref[0:1, :], axis=-1, keepdims=True) * (0.5 / d)
            loss_ref[...] = jnp.broadcast_to(total, loss_ref.shape)

    return pl.pallas_call(
        body, name="loss_head", grid=(n_tiles,),
        out_shape=[jax.ShapeDtypeStruct((l_rows, d), F32), jax.ShapeDtypeStruct((8, 128), F32),
                   jax.ShapeDtypeStruct((8, d), F32)],
        in_specs=[pl.BlockSpec((tm, d), lambda i: (i, 0)), _const_spec((1, d)),
                  pl.BlockSpec((tm, d), lambda i: (i, 0))],
        out_specs=[pl.BlockSpec((tm, d), lambda i: (i, 0)), pl.BlockSpec((8, 128), lambda i: (0, 0)),
                   pl.BlockSpec((8, d), lambda i: (0, 0))],
        scratch_shapes=[pltpu.VMEM((8, d), F32)],
        compiler_params=_params(("arbitrary",)),
    )(h, final_g, target)


def _adamw(w, gparts, m, v, name):
    shape = w.shape
    cols = shape[-1]
    rows = int(np.prod(shape[:-1])) if len(shape) > 1 else 1
    pieces = list(gparts) if isinstance(gparts, (list, tuple)) else [gparts]
    n_pieces = len(pieces)
    nparts = pieces[0].shape[0]
    piece_rows = rows // n_pieces
    w2, m2, v2 = (t.reshape(rows, cols) for t in (w, m, v))
    pieces = [g.reshape(nparts, piece_rows, cols) for g in pieces]
    tr = piece_rows
    part_bytes = nparts * cols * pieces[0].dtype.itemsize
    for cand in (1024, 512, 256, 128, 64, 32, 16, 8):
        if piece_rows * max(part_bytes, cols * 4) <= (2 << 20):
            break
        if piece_rows % cand == 0 and cand < piece_rows:
            tr = cand
            if cand * max(part_bytes, cols * 4) <= (2 << 20):
                break
    per_piece = piece_rows // tr
    c1 = 1.0 - ADAM_B1 ** ADAM_STEP
    c2 = 1.0 - ADAM_B2 ** ADAM_STEP

    def update(w_ref, g_ref, m_ref, v_ref, go_ref, d_ref, mo_ref, vo_ref):
        g = g_ref[0].astype(F32)
        for k in range(1, nparts):
            g = g + g_ref[k].astype(F32)
        mn = ADAM_B1 * m_ref[...] + (1.0 - ADAM_B1) * g
        vn = ADAM_B2 * v_ref[...] + (1.0 - ADAM_B2) * (g * g)
        go_ref[...] = g
        mo_ref[...] = mn
        vo_ref[...] = vn
        d_ref[...] = -ADAM_LR * ((mn / c1) / (jnp.sqrt(vn / c2) + ADAM_EPS) + ADAM_WD * w_ref[...])

    def body(w_ref, *refs):
        g_refs, (m_ref, v_ref, go_ref, d_ref, mo_ref, vo_ref) = refs[:n_pieces], refs[n_pieces:]
        if n_pieces == 1:
            update(w_ref, g_refs[0], m_ref, v_ref, go_ref, d_ref, mo_ref, vo_ref)
        else:
            piece = pl.program_id(0) // per_piece
            for k in range(n_pieces):
                pl.when(piece == k)(functools.partial(update, w_ref, g_refs[k], m_ref, v_ref, go_ref, d_ref, mo_ref, vo_ref))

    def piece_spec(k):
        return pl.BlockSpec((nparts, tr, cols), lambda i: (0, jnp.clip(i - k * per_piece, 0, per_piece - 1), 0))

    spec = pl.BlockSpec((tr, cols), lambda i: (i, 0))
    outs = pl.pallas_call(
        body, name=name, grid=(rows // tr,),
        out_shape=[jax.ShapeDtypeStruct((rows, cols), F32)] * 4,
        in_specs=[spec] + [piece_spec(k) for k in range(n_pieces)] + [spec, spec],
        out_specs=[spec] * 4,
        compiler_params=_params(("arbitrary",)),
    )(w2, *pieces, m2, v2)
    return tuple(o.reshape(shape) for o in outs)


def _natural_cols(g):
    return jnp.moveaxis(g, 0, -2).reshape(g.shape[1:-1] + (N_DEV * g.shape[-1],))


def _natural_rows(g):
    return jnp.moveaxis(g, 0, -3).reshape(g.shape[1:-2] + (N_DEV * g.shape[-2], g.shape[-1]))


def _shard_rows(full):
    r = full.shape[-2] // N_DEV
    return jnp.moveaxis(full.reshape(full.shape[:-2] + (N_DEV, r, full.shape[-1])), -3, 0)


def _my_cols(gathered, me, n):
    return lax.dynamic_slice_in_dim(gathered, me * n, n, axis=gathered.ndim - 1)


def kernel(x, c, ctx, c_ctx, ada_w, ada_b, norm_g, mlp_w1, mlp_w2, pool_w, pool_scale, attn_w_qkv, attn_w_o, attn_q_g, attn_k_g, gm_w_in, gm_ln_g, gm_ln_b, gm_ws, gm_bs, gm_w_out, final_g, loss_target, m_c_ctx, m_ada_w, m_ada_b, m_norm_g, m_mlp_w1, m_mlp_w2, m_pool_w, m_pool_scale, m_attn_w_qkv, m_attn_w_o, m_attn_q_g, m_attn_k_g, m_gm_w_in, m_gm_ln_g, m_gm_ln_b, m_gm_ws, m_gm_bs, m_gm_w_out, m_final_g, v_c_ctx, v_ada_w, v_ada_b, v_norm_g, v_mlp_w1, v_mlp_w2, v_pool_w, v_pool_scale, v_attn_w_qkv, v_attn_w_o, v_attn_q_g, v_attn_k_g, v_gm_w_in, v_gm_ln_g, v_gm_ln_b, v_gm_ws, v_gm_bs, v_gm_w_out, v_final_g):
    l_len, d = x.shape[1], x.shape[2]
    c_len = ctx.shape[1]
    n_layers = ada_w.shape[0]
    assert n_layers == 4 and x.shape[0] == 1
    n_heads = d // HEAD_DIM
    n_kv = n_heads // 2
    half = gm_w_out.shape[1] * N_DEV
    tm = c_len if c_len <= 256 else 256
    assert c_len % tm == 0 and l_len % tm == 0 and tm % CHUNK == 0 and l_len % GRID_W == 0
    nct = c_len // tm
    me = _dev_index(*_coords())
    n_ada = ada_w.shape[-1]

    first = [t.astype(BF16) for t in (mlp_w1[0], mlp_w2[0], pool_w, attn_w_qkv[0])]
    small = [c, norm_g.reshape(n_layers * 2, -1), pool_scale, gm_ln_g, gm_ln_b]
    w1_0g, w2_0g, pool_g, qkv_g, c_all, ng_g, ps_g, lng_g, lnb_g = _all_gather(first + small, "gather_first")
    c_all = c_all.reshape(N_DEV, d)
    later = _GatherAcrossChips([t.astype(BF16) for t in
                                (mlp_w1[1], mlp_w1[2], mlp_w1[3], mlp_w2[1], mlp_w2[2], mlp_w2[3],
                                 attn_w_o[0], gm_w_in[0], gm_w_out[0])])
    pool_wf = _natural_rows(pool_g)
    wqkv = _natural_cols(qkv_g)
    ng_full = _natural_cols(ng_g.reshape(N_DEV, n_layers * 2, 1, -1)).reshape(n_layers, 2, 1, d)
    ps_full = _natural_cols(ps_g.reshape(N_DEV, 2, 1, -1))
    lng_full = _natural_cols(lng_g.reshape(N_DEV, 1, -1))
    lnb_full = _natural_cols(lnb_g.reshape(N_DEV, 1, -1))

    c_ctx2 = c_ctx.reshape(1, d)
    ada_b_loc = lax.dynamic_slice_in_dim(ada_b, me * n_ada, n_ada, axis=1).reshape(n_layers, 1, n_ada)
    (mod_g,) = _all_gather([_mods_local(c_all, c_ctx2, ada_w, ada_b_loc)], "gather_mods")
    mod_full = jnp.moveaxis(mod_g, 0, 2).reshape(n_layers, 16, 6, d)
    mod_lat = lax.dynamic_index_in_dim(mod_full, me, axis=1, keepdims=False)
    mod_ctx = mod_full[:, 8]
    mods = jnp.stack([mod_ctx, mod_lat], axis=1)
    mods = jnp.concatenate([mods, jnp.zeros((n_layers, 2, 2, d), F32)], axis=2)

    bands = _pool_bands(tm)
    cos, sin = _rope_tables(c_len, l_len)
    ws_bf = gm_ws[0].astype(BF16)
    wst_bf = jnp.swapaxes(gm_ws[0], 1, 2).astype(BF16)
    bst = jnp.zeros((CHUNK, 128), F32).at[:, :GMLP_GROUPS].set(gm_bs[0].T)
    ng = lambda i, j: ng_full[i, j]

    h0 = jnp.concatenate([ctx[0], x[0]], axis=0)
    y0, h1 = _pool_fwd(h0, mods[0], ng(0, 0), pool_wf[0].astype(BF16), ps_full[0], bands, nct, tm, c_len, l_len)
    h2, p0, ym0 = _mlp_fwd(h1, mods[0], ng(0, 1), w1_0g, w2_0g, nct, tm)
    q, k, v1, qt = _qkv_fwd(h2, mods[1], ng(1, 0), wqkv, attn_q_g, attn_k_g, cos, sin, n_heads, n_kv, nct, tm)
    o, lse, later_g = _flash_fwd(q, k, v1, n_kv, 4 * tm, tm, rider=later)
    (wo_g,) = _exchange_call(_ForwardToSibling(later_g[6:7]), "forward_wo")
    wo = _natural_rows(wo_g)
    y1, h3, rest_g = _wo_fwd(h2, o, wo, mods[1], nct, tm, rider=_ForwardToSibling(later_g[0:6] + later_g[7:9]))
    w1 = [w1_0g] + rest_g[0:3]
    w2 = [w2_0g] + rest_g[3:6]
    win = rest_g[6]
    wout = _natural_rows(rest_g[7])
    tm_lat = 2 * tm
    h4, p1, ym1 = _mlp_fwd(h3, mods[1], ng(1, 1), w1[1], w2[1], 0, tm_lat)
    y2, h5 = _gmlp_fwd(h4, mods[2], ng(2, 0), win, lng_full, lnb_full, ws_bf, bst, wout, tm)
    h6, p2, ym2 = _mlp_fwd(h5, mods[2], ng(2, 1), w1[2], w2[2], 0, tm_lat)
    y3, h7 = _pool_fwd(h6, mods[3], ng(3, 0), pool_wf[1].astype(BF16), ps_full[1], bands, 0, tm, c_len, l_len)
    h8, p3, ym3 = _mlp_fwd(h7, mods[3], ng(3, 1), w1[3], w2[3], 0, tm_lat)
    dh, loss_part, dfinal = _head(h8, final_g.reshape(1, d), loss_target[0], tm_lat)

    dw1, dw2, st_mlp = [None] * 4, [None] * 4, [None] * 4

    def mlp_back(i, dh, h_in, p, ym, nct_i):
        dh_in, m_bf, du, dacc, st, _ = _mlp_bwd(dh, h_in, p, ym, mods[i], ng(i, 1), w1[i], w2[i], nct_i, tm_lat)
        dw1[i] = _tn_matmul(m_bf, du, "tn_w1", col_shards=True)
        dw2[i] = _shard_rows(_tn_matmul(p, dacc, "tn_w2", square_x=True))
        st_mlp[i] = st
        return dh_in

    dh = mlp_back(3, dh, h7, p3, ym3, 0)
    dh, dpw1, st_pool3, _ = _pool_bwd(dh, h6, y3, mods[3], ng(3, 0), pool_wf[1].astype(BF16), ps_full[1], bands,
                                      0, tm, c_len, l_len, False)
    dh = mlp_back(2, dh, h5, p2, ym2, 0)
    dh, a_bf, dz, gated, dy, st_g, st_ln, dws, dbst = _gmlp_bwd(
        dh, h4, y2, mods[2], ng(2, 0), win, lng_full, lnb_full, ws_bf, wst_bf, bst, wout, tm)
    dwin = _tn_matmul(a_bf, dz, "tn_gm_in", col_shards=True)
    dwout = _tn_matmul(gated, dy, "tn_gm_out")
    dh = mlp_back(1, dh, h3, p1, ym1, 0)
    dy1, do, dot, delta, st_wo = _wo_bwd(dh, y1, o, wo, mods[1], n_kv, tm)
    dwo = _tn_matmul(o, dy1, "tn_wo")
    grads_mid = _AllToAll(dw1[1:] + dw2[1:] + [_shard_rows(dpw1.astype(BF16)), _shard_rows(dwo), dwin,
                                               _shard_rows(dwout)])
    dq, dkt, dvt, rode = _flash_bwd(q, qt, k, v1, do, dot, lse, delta, n_kv, min(8 * tm, l_len), tm,
                                    rider=grads_mid)
    g_w1, g_w2, (g_pool1, g_wo, g_gin, g_gout) = [None] + rode[0:3], [None] + rode[3:6], rode[6:]
    dh, a_bf, dqkv, st_q, dgains = _qkv_bwd(h2, dh, dq, dkt, dvt, mods[1], ng(1, 0), wqkv, attn_q_g, attn_k_g,
                                            cos, sin, n_heads, n_kv, nct, tm)
    dwqkv = _tn_matmul(a_bf, dqkv, "tn_qkv", col_shards=True)
    dh, m_bf, du, dacc, st_mlp[0], (g_qkv,) = _mlp_bwd(dh, h1, p0, ym0, mods[0], ng(0, 1), w1[0], w2[0], nct, tm,
                                                       rider=_AllToAll([dwqkv]))
    dw1_0 = _tn_matmul(m_bf, du, "tn_w1", col_shards=True)
    dw2_0, (g_w1[0],) = _tn_matmul(p0, dacc, "tn_w2", square_x=True, rider=_AllToAll([dw1_0]))
    grad_x, dpw0, st_pool0, (g_w2[0],) = _pool_bwd(dh, h0, y0, mods[0], ng(0, 0), pool_wf[0].astype(BF16),
                                                   ps_full[0], bands, nct, tm, c_len, l_len, True,
                                                   rider=_AllToAll([_shard_rows(dw2_0)]))

    mix_lat = [st_pool0[-1], st_q[1] + st_wo, st_g, st_pool3[-1]]
    mlp_lat = [st[-1] for st in st_mlp]
    dmod_lat = jnp.stack([jnp.concatenate([mix_lat[i][0:3], mlp_lat[i][0:3]]) for i in range(n_layers)])
    dmod_ctx = jnp.stack([jnp.concatenate([st_pool0[0][0:3], st_mlp[0][0][0:3]]),
                          jnp.concatenate([st_q[0][0:2], jnp.zeros((4, d), F32)]),
                          jnp.zeros((6, d), F32), jnp.zeros((6, d), F32)])
    dng_part = jnp.stack([jnp.stack([mix_lat[0][3] + st_pool0[0][3], mlp_lat[0][3] + st_mlp[0][0][3]]),
                          jnp.stack([mix_lat[1][3] + st_q[0][3], mlp_lat[1][3]]),
                          jnp.stack([mix_lat[2][3], mlp_lat[2][3]]),
                          jnp.stack([mix_lat[3][3], mlp_lat[3][3]])])
    dps_part = jnp.stack([mix_lat[0][4] + st_pool0[0][4], mix_lat[3][4]])
    small_parts = [dmod_lat.reshape(n_layers * 6, d), dmod_ctx.reshape(n_layers * 6, d),
                   dng_part.reshape(n_layers * 2, d), dps_part, st_ln, dgains, dws.reshape(-1, CHUNK),
                   dbst, dfinal, loss_part]
    (gm_lat, gm_ctx, g_ng, g_ps, g_ln, g_gains, g_ws, g_bst, g_final, loss_all, g_pool0) = _all_gather(
        small_parts, "gather_small_grads", extra=_AllToAll([_shard_rows(dpw0.astype(BF16))]))
    g_pool = jnp.stack([g_pool0, g_pool1], axis=1)

    gm_lat4 = gm_lat.reshape(N_DEV, n_layers, 6 * d)
    gm_ctx4 = gm_ctx.reshape(N_DEV, n_layers, 6 * d)
    dm_lat_loc = jnp.moveaxis(_my_cols(gm_lat4, me, n_ada), 0, 1)
    dm_ctx_loc = jnp.moveaxis(_my_cols(gm_ctx4, me, n_ada), 0, 1)
    g_ada_w, ds_part = _ada_grads(c_all, c_ctx2, ada_w, dm_lat_loc, dm_ctx_loc)
    (ds_all,) = _all_gather([ds_part], "gather_dsctx")
    g_c_ctx, loss_sum = _cctx_grad_and_loss(ds_all, c_ctx2, loss_all)
    g_c_ctx = g_c_ctx.reshape(d)

    n_ng = norm_g.shape[-1]
    n_ps = pool_scale.shape[-1]
    n_ln = gm_ln_g.shape[-1]
    gparts = {
        "c_ctx": g_c_ctx[None],
        "ada_w": g_ada_w[None],
        "ada_b": jnp.concatenate([gm_lat4, gm_ctx4], axis=0),
        "norm_g": _my_cols(g_ng.reshape(N_DEV, n_layers, 2, d), me, n_ng),
        "mlp_w1": g_w1, "mlp_w2": g_w2, "pool_w": g_pool,
        "pool_scale": _my_cols(g_ps, me, n_ps),
        "attn_w_qkv": g_qkv[:, None], "attn_w_o": g_wo[:, None],
        "attn_q_g": g_gains[:, 0:1], "attn_k_g": g_gains[:, 1:2],
        "gm_w_in": g_gin[:, None],
        "gm_ln_g": _my_cols(g_ln[:, 0:1], me, n_ln), "gm_ln_b": _my_cols(g_ln[:, 1:2], me, n_ln),
        "gm_ws": g_ws.reshape((N_DEV,) + gm_ws.shape),
        "gm_bs": jnp.swapaxes(g_bst[:, :, :GMLP_GROUPS], 1, 2)[:, None],
        "gm_w_out": g_gout[:, None],
        "final_g": g_final[:, 0],
    }
    weights = dict(c_ctx=(c_ctx, m_c_ctx, v_c_ctx), ada_w=(ada_w, m_ada_w, v_ada_w), ada_b=(ada_b, m_ada_b, v_ada_b),
                   norm_g=(norm_g, m_norm_g, v_norm_g), mlp_w1=(mlp_w1, m_mlp_w1, v_mlp_w1),
                   mlp_w2=(mlp_w2, m_mlp_w2, v_mlp_w2), pool_w=(pool_w, m_pool_w, v_pool_w),
                   pool_scale=(pool_scale, m_pool_scale, v_pool_scale),
                   attn_w_qkv=(attn_w_qkv, m_attn_w_qkv, v_attn_w_qkv), attn_w_o=(attn_w_o, m_attn_w_o, v_attn_w_o),
                   attn_q_g=(attn_q_g, m_attn_q_g, v_attn_q_g), attn_k_g=(attn_k_g, m_attn_k_g, v_attn_k_g),
                   gm_w_in=(gm_w_in, m_gm_w_in, v_gm_w_in), gm_ln_g=(gm_ln_g, m_gm_ln_g, v_gm_ln_g),
                   gm_ln_b=(gm_ln_b, m_gm_ln_b, v_gm_ln_b), gm_ws=(gm_ws, m_gm_ws, v_gm_ws),
                   gm_bs=(gm_bs, m_gm_bs, v_gm_bs), gm_w_out=(gm_w_out, m_gm_w_out, v_gm_w_out),
                   final_g=(final_g, m_final_g, v_final_g))
    grads, deltas, new_m, new_v = [], [], [], []
    for wname, (w_, m_, v_) in weights.items():
        g_, d_, nm_, nv_ = _adamw(w_, gparts[wname], m_, v_, "adamw_" + wname)
        grads.append(g_)
        deltas.append(d_)
        new_m.append(nm_)
        new_v.append(nv_)

    return (loss_sum[0, 0], grad_x[None], *grads, *deltas, *new_m, *new_v)
```

```python
import functools
import math

import numpy as np
import jax
import jax.numpy as jnp
from jax import lax
from jax.experimental import pallas as pl
from jax.experimental.pallas import tpu as pltpu

F32 = jnp.float32
BF16 = jnp.bfloat16
MESH_ID = pl.DeviceIdType.MESH

N_DEV = 8
EPS = 1e-6
HEAD_DIM = 128
GRID_W = 64
ROPE_BASE = 10000.0
CHUNK = 128
POOL_WINDOWS = (2, 4, 8, 16)
POOL_GROUPS = 4
POOL_HALO = 8
GMLP_GROUPS = 8
ADAM_LR, ADAM_B1, ADAM_B2, ADAM_EPS, ADAM_WD, ADAM_STEP = 0.001, 0.9, 0.999, 1e-08, 0.01, 10

V7X_VMEM_BYTES = 64 << 20
VMEM_LIMIT_BIG = V7X_VMEM_BYTES - (8 << 20)
FLASH_TK_CAP = 768
LOG2E = math.log2(math.e)
Q_SCALE = HEAD_DIM ** -0.5 * LOG2E


def _params(sem, vmem=VMEM_LIMIT_BIG):
    return pltpu.CompilerParams(dimension_semantics=sem, vmem_limit_bytes=vmem)


def _const_spec(shape):
    nd = len(shape)
    return pl.BlockSpec(shape, lambda *_: (0,) * nd, pipeline_mode=pl.Buffered(1))


def _dot(a, b):
    return jnp.dot(a, b, preferred_element_type=F32)


def _dot_nt(a, b):
    return lax.dot_general(a, b, (((1,), (1,)), ((), ())), preferred_element_type=F32)


def _dot_tn(a, b):
    return lax.dot_general(a, b, (((0,), (0,)), ((), ())), preferred_element_type=F32)


def _colsum(x):
    return jnp.sum(x, axis=0, keepdims=True)


def _sid(i, nct):
    if nct == 0:
        return 1
    return jnp.where(i >= nct, 1, 0)


def _n_streams(nct):
    return 2 if nct else 1


def _stat_sid(i, nct):
    return _sid(i, nct) if nct else 0


def _first_of_stream(i, nct):
    if nct == 0:
        return i == 0
    return jnp.logical_or(i == 0, i == nct)


def _normmod(h, ng, sh, sc):
    rstd = lax.rsqrt(jnp.mean(h * h, axis=-1, keepdims=True) + EPS)
    xhat = h * rstd
    n = xhat * ng
    return n * (1.0 + sc) + sh, xhat, rstd, n


def _normmod_bwd(da, xhat, rstd, n, ng, sc):
    dsh = _colsum(da)
    dsc = _colsum(da * n)
    dn = da * (1.0 + sc)
    dng = _colsum(dn * xhat)
    dxhat = dn * ng
    dh = rstd * (dxhat - xhat * jnp.mean(dxhat * xhat, axis=-1, keepdims=True))
    return dh, dsh, dsc, dng


def _acc_rows(ref, first, rows):
    @pl.when(first)
    def _():
        ref[...] = jnp.zeros_like(ref)
    for r, val in enumerate(rows):
        ref[r:r + 1, :] = ref[r:r + 1, :] + val


_GELU_C = math.sqrt(2.0 / math.pi)


def _gelu(x):
    t = jnp.tanh((_GELU_C * x) * (1.0 + 0.044715 * (x * x)))
    hx = 0.5 * x
    return hx + hx * t


def _gelu_and_grad(x):
    x2 = x * x
    t = jnp.tanh((_GELU_C * x) * (1.0 + 0.044715 * x2))
    hx = 0.5 * x
    g = hx + hx * t
    dg = (0.5 + 0.5 * t) + (hx * (1.0 - t * t)) * (_GELU_C + (3.0 * 0.044715 * _GELU_C) * x2)
    return g, dg


def _coords():
    return lax.axis_index("x"), lax.axis_index("y"), lax.axis_index("c")


def _dev_index(px, py, pc):
    return 4 * px + 2 * py + pc


def _all_gather(xs, name, extra=None):
    n = len(xs)
    e_in, e_in_specs, e_out, e_out_specs, e_scratch = _rider_parts(extra)
    ne = len(e_in)

    def body(*refs):
        x_refs, e_x = refs[:n], refs[n:n + ne]
        o_refs, e_o = refs[n + ne:2 * n + ne], refs[2 * n + ne:2 * n + 2 * ne]
        send_sems, recv_sems, local_sems = refs[2 * n + 2 * ne:2 * n + 2 * ne + 3]
        e_sems = refs[2 * n + 2 * ne + 3:]
        if extra is not None:
            extra.start(e_x, e_o, e_sems)
        x, y, c = _coords()
        me, sibling = (x, y, c), (x, y, 1 - c)
        chips = [(1 - x, y), (x, 1 - y), (1 - x, 1 - y)]

        def copy(a, k, block, to, src=None):
            dst = o_refs[a].at[_dev_index(*block)]
            return pltpu.make_async_remote_copy(
                src_ref=dst if src is None else src, dst_ref=dst,
                send_sem=send_sems.at[7 * a + k], recv_sem=recv_sems.at[7 * a + k],
                device_id=to, device_id_type=MESH_ID)

        mine = [pltpu.make_async_copy(x_refs[a], o_refs[a].at[_dev_index(*me)], local_sems.at[a])
                for a in range(n)]
        for cp in mine:
            cp.start()
        first = []
        for a in range(n):
            first.append(copy(a, 0, me, sibling, src=x_refs[a]))
            first += [copy(a, 1 + j, me, (*chip, c), src=x_refs[a]) for j, chip in enumerate(chips)]
        for cp in first:
            cp.start()
        passed = []
        for a in range(n):
            for j, chip in enumerate(chips):
                copy(a, 1 + j, (*chip, c), me).wait_recv()
                fwd = copy(a, 4 + j, (*chip, c), sibling)
                fwd.start()
                passed.append(fwd)
        for a in range(n):
            copy(a, 0, sibling, me).wait_recv()
            for j, chip in enumerate(chips):
                copy(a, 4 + j, (*chip, 1 - c), me).wait_recv()
        for cp in first + passed:
            cp.wait_send()
        for cp in mine:
            cp.wait()
        if extra is not None:
            extra.finish(e_x, e_o, e_sems)

    any_spec = pl.BlockSpec(memory_space=pl.ANY)
    outs = pl.pallas_call(
        body, name=name,
        out_shape=[jax.ShapeDtypeStruct((N_DEV,) + x.shape, x.dtype) for x in xs] + e_out,
        in_specs=[any_spec] * n + e_in_specs, out_specs=[any_spec] * n + e_out_specs,
        scratch_shapes=[pltpu.SemaphoreType.DMA((7 * n,)), pltpu.SemaphoreType.DMA((7 * n,)),
                        pltpu.SemaphoreType.DMA((n,))] + e_scratch,
    )(*xs, *e_in)
    return list(outs)


class _Exchange:
    per_array = 0
    in_place = False

    def __init__(self, xs):
        self.xs = list(xs)
        n = len(self.xs)
        self.out_shapes = self._out_shapes()
        self.scratch = [pltpu.SemaphoreType.DMA((self.per_array * n,)),
                        pltpu.SemaphoreType.DMA((self.per_array * n,)),
                        pltpu.SemaphoreType.DMA((n,))]

    def _out_shapes(self):
        raise NotImplementedError

    def _copies(self, x_refs, o_refs, sems):
        raise NotImplementedError

    def start(self, x_refs, o_refs, sems):
        mine, sends, _ = self._copies(x_refs, o_refs, sems)
        for cp in mine + sends:
            cp.start()

    def finish(self, x_refs, o_refs, sems):
        mine, sends, arrivals = self._copies(x_refs, o_refs, sems)
        for make in arrivals:
            make().wait_recv()
        for cp in sends:
            cp.wait_send()
        for cp in mine:
            cp.wait()


def _remote(src, dst, sems, k, to):
    return pltpu.make_async_remote_copy(src_ref=src, dst_ref=dst, send_sem=sems[0].at[k], recv_sem=sems[1].at[k],
                                        device_id=to, device_id_type=MESH_ID)


class _GatherAcrossChips(_Exchange):
    per_array = 4

    def _out_shapes(self):
        return [jax.ShapeDtypeStruct((N_DEV,) + x.shape, x.dtype) for x in self.xs]

    def _copies(self, x_refs, o_refs, sems):
        x, y, c = _coords()
        targets = [(x, y, 1 - c), (1 - x, y, c), (x, 1 - y, c), (1 - x, 1 - y, c)]
        mine, sends, arrivals = [], [], []
        for a, (x_ref, o_ref) in enumerate(zip(x_refs, o_refs)):
            own = o_ref.at[_dev_index(x, y, c)]
            mine.append(pltpu.make_async_copy(x_ref, own, sems[2].at[a]))
            for k, to in enumerate(targets):
                sends.append(_remote(x_ref, own, sems, 4 * a + k, to))
                arrivals.append(functools.partial(_remote, x_ref, o_ref.at[_dev_index(*to)], sems, 4 * a + k, to))
        return mine, sends, arrivals


class _ForwardToSibling(_Exchange):
    per_array = 3
    in_place = True

    def _out_shapes(self):
        return [jax.ShapeDtypeStruct(x.shape, x.dtype) for x in self.xs]

    def _copies(self, x_refs, o_refs, sems):
        x, y, c = _coords()
        chips = [(1 - x, y), (x, 1 - y), (1 - x, 1 - y)]
        sends, arrivals = [], []
        for a, (x_ref, o_ref) in enumerate(zip(x_refs, o_refs)):
            for j, chip in enumerate(chips):
                held = _dev_index(*chip, c)
                sends.append(_remote(x_ref.at[held], o_ref.at[held], sems, 3 * a + j, (x, y, 1 - c)))
                theirs = _dev_index(*chip, 1 - c)
                arrivals.append(functools.partial(_remote, x_ref.at[theirs], o_ref.at[theirs], sems, 3 * a + j,
                                                  (x, y, 1 - c)))
        return [], sends, arrivals


class _AllToAll(_Exchange):
    per_array = 7

    def _out_shapes(self):
        return [jax.ShapeDtypeStruct(x.shape, x.dtype) for x in self.xs]

    def _copies(self, x_refs, o_refs, sems):
        x, y, c = _coords()
        me_i = _dev_index(x, y, c)
        mine, sends, arrivals = [], [], []
        for a, (x_ref, o_ref) in enumerate(zip(x_refs, o_refs)):
            mine.append(pltpu.make_async_copy(x_ref.at[me_i], o_ref.at[me_i], sems[2].at[a]))
            for r in range(1, 8):
                to = (1 - x if r & 4 else x, 1 - y if r & 2 else y, 1 - c if r & 1 else c)
                to_i = _dev_index(*to)
                sends.append(_remote(x_ref.at[to_i], o_ref.at[me_i], sems, 7 * a + r - 1, to))
                arrivals.append(functools.partial(_remote, x_ref.at[to_i], o_ref.at[to_i], sems, 7 * a + r - 1, to))
        return mine, sends, arrivals


def _exchange_call(ex, name):
    n = len(ex.xs)

    def body(*refs):
        x_refs, o_refs, sems = refs[:n], refs[n:2 * n], refs[2 * n:]
        ex.start(x_refs, o_refs, sems)
        ex.finish(x_refs, o_refs, sems)

    any_spec = pl.BlockSpec(memory_space=pl.ANY)
    outs = pl.pallas_call(
        body, name=name, out_shape=ex.out_shapes, in_specs=[any_spec] * n, out_specs=[any_spec] * n,
        scratch_shapes=ex.scratch, input_output_aliases={a: a for a in range(n)} if ex.in_place else {},
    )(*ex.xs)
    return list(outs)


def _rider_parts(rider):
    if rider is None:
        return [], [], [], [], []
    any_spec = pl.BlockSpec(memory_space=pl.ANY)
    n = len(rider.xs)
    return rider.xs, [any_spec] * n, rider.out_shapes, [any_spec] * n, rider.scratch


def _compute_call(body, *, name, grid, in_specs, out_specs, out_shape, operands, scratch_shapes=(), rider=None):
    in_specs, out_specs, out_shape, scratch_shapes = list(in_specs), list(out_specs), list(out_shape), list(scratch_shapes)
    r_in, r_in_specs, r_out, r_out_specs, r_scratch = _rider_parts(rider)
    n_in, n_out, n_scr, nr = len(operands), len(out_shape), len(scratch_shapes), len(r_in)

    def riding_body(*refs):
        ins, refs = refs[:n_in], refs[n_in:]
        r_x, refs = refs[:nr], refs[nr:]
        outs, refs = refs[:n_out], refs[n_out:]
        r_o, refs = refs[:nr], refs[nr:]
        scratch, r_sems = refs[:n_scr], refs[n_scr:]
        if rider is not None:
            first, last = _grid_ends(grid)
            pl.when(first)(lambda: rider.start(r_x, r_o, r_sems))
        body(*ins, *outs, *scratch)
        if rider is not None:
            pl.when(last)(lambda: rider.finish(r_x, r_o, r_sems))

    in_place = rider is not None and rider.in_place
    res = pl.pallas_call(
        riding_body, name=name, grid=grid, out_shape=out_shape + r_out,
        in_specs=in_specs + r_in_specs, out_specs=out_specs + r_out_specs,
        scratch_shapes=scratch_shapes + r_scratch,
        input_output_aliases={n_in + a: n_out + a for a in range(nr)} if in_place else {},
        compiler_params=_params(("arbitrary",) * len(grid)),
    )(*operands, *r_in)
    return list(res[:n_out]), list(res[n_out:])


def _grid_ends(grid):
    first = pl.program_id(0) == 0
    last = pl.program_id(0) == grid[0] - 1
    for ax in range(1, len(grid)):
        first = jnp.logical_and(first, pl.program_id(ax) == 0)
        last = jnp.logical_and(last, pl.program_id(ax) == grid[ax] - 1)
    return first, last


def _silu(x):
    return x * (1.0 / (1.0 + jnp.exp(-x)))


def _cond_rows(c_all, c_ctx):
    d = c_all.shape[-1]
    s = jnp.concatenate([c_all, jnp.zeros((8, d), F32)], axis=0)
    row = lax.broadcasted_iota(jnp.int32, (16, d), 0)
    s = jnp.where(row == 8, c_ctx, s)
    return jnp.where(row <= 8, _silu(s), 0.0)


def _mods_local(c_all, c_ctx, ada_w, ada_b_loc):
    nl, d, n = ada_w.shape

    def body(c_ref, cc_ref, w_ref, b_ref, o_ref):
        s = _cond_rows(c_ref[...], cc_ref[...])
        o_ref[...] = jnp.dot(s, w_ref[...], preferred_element_type=F32,
                             precision=lax.Precision.HIGHEST) + b_ref[...]

    return pl.pallas_call(
        body, name="mods_local", grid=(nl,),
        out_shape=jax.ShapeDtypeStruct((nl, 16, n), F32),
        in_specs=[pl.BlockSpec((8, d), lambda i: (0, 0)), pl.BlockSpec((1, d), lambda i: (0, 0)),
                  pl.BlockSpec((None, d, n), lambda i: (i, 0, 0)),
                  pl.BlockSpec((None, 1, n), lambda i: (i, 0, 0))],
        out_specs=pl.BlockSpec((None, 16, n), lambda i: (i, 0, 0)),
        compiler_params=_params(("arbitrary",)),
    )(c_all, c_ctx, ada_w, ada_b_loc)


def _ada_grads(c_all, c_ctx, ada_w, dm_lat, dm_ctx):
    nl, d, n = ada_w.shape

    def body(c_ref, cc_ref, w_ref, dml_ref, dmc_ref, gw_ref, ds_ref):
        i = pl.program_id(0)
        s = _cond_rows(c_ref[...], cc_ref[...])
        csum = dmc_ref[0:1, :]
        for k in range(1, N_DEV):
            csum = csum + dmc_ref[k:k + 1, :]
        row = lax.broadcasted_iota(jnp.int32, (8, n), 0)
        dm_c = jnp.where(row == 0, csum, 0.0)
        dm = jnp.concatenate([dml_ref[...], dm_c], axis=0)
        gw_ref[...] = lax.dot_general(s, dm, (((0,), (0,)), ((), ())), preferred_element_type=F32,
                                      precision=lax.Precision.HIGHEST)
        ds = lax.dot_general(dm_c, w_ref[...], (((1,), (1,)), ((), ())),
                             preferred_element_type=F32, precision=lax.Precision.HIGHEST)

        @pl.when(i == 0)
        def _():
            ds_ref[...] = jnp.zeros_like(ds_ref)
        ds_ref[...] += ds

    return pl.pallas_call(
        body, name="ada_grads", grid=(nl,),
        out_shape=[jax.ShapeDtypeStruct((nl, d, n), F32), jax.ShapeDtypeStruct((8, d), F32)],
        in_specs=[pl.BlockSpec((8, d), lambda i: (0, 0)), pl.BlockSpec((1, d), lambda i: (0, 0)),
                  pl.BlockSpec((None, d, n), lambda i: (i, 0, 0)),
                  pl.BlockSpec((None, 8, n), lambda i: (i, 0, 0)),
                  pl.BlockSpec((None, 8, n), lambda i: (i, 0, 0))],
        out_specs=[pl.BlockSpec((None, d, n), lambda i: (i, 0, 0)),
                   pl.BlockSpec((8, d), lambda i: (0, 0))],
        compiler_params=_params(("arbitrary",)),
    )(c_all, c_ctx, ada_w, dm_lat, dm_ctx)


def _cctx_grad_and_loss(ds_parts, c_ctx, loss_parts):
    d = c_ctx.shape[-1]

    def body(p_ref, c_ref, l_ref, o_ref, lo_ref):
        ds, loss = p_ref[0], l_ref[0]
        for k in range(1, N_DEV):
            ds = ds + p_ref[k]
            loss = loss + l_ref[k]
        x = c_ref[...]
        sg = 1.0 / (1.0 + jnp.exp(-x))
        o_ref[...] = ds[0:1, :] * (sg * (1.0 + x * (1.0 - sg)))
        lo_ref[...] = loss

    return pl.pallas_call(body, name="cctx_grad", out_shape=[jax.ShapeDtypeStruct((1, d), F32),
                                                             jax.ShapeDtypeStruct((8, 128), F32)])(ds_parts, c_ctx, loss_parts)


def _mlp_fwd(h1, mods, ng, w1, w2, nct, tm, row_off=0):
    d = h1.shape[1]
    r = h1.shape[0] - row_off * tm
    fc = w1.shape[2]
    f = N_DEV * fc

    def body(h_ref, mod_ref, ng_ref, w1_ref, w2_ref, h2_ref, p_ref, y_ref):
        h = h_ref[...]
        a, _, _, _ = _normmod(h, ng_ref[...], mod_ref[3:4, :], mod_ref[4:5, :])
        ab = a.astype(BF16)
        acc = jnp.zeros((tm, d), F32)
        for j in range(N_DEV):
            sl = slice(j * fc, (j + 1) * fc)
            p = jnp.maximum(_dot(ab, w1_ref[j]), 0.0)
            p_ref[:, sl] = p.astype(BF16)
            acc = acc + _dot((p * p).astype(BF16), w2_ref[j])
        y_ref[...] = acc.astype(BF16)
        h2_ref[...] = h + mod_ref[5:6, :] * acc

    return pl.pallas_call(
        body, name="mlp_fwd", grid=(r // tm,),
        out_shape=[jax.ShapeDtypeStruct((r, d), F32), jax.ShapeDtypeStruct((r, f), BF16),
                   jax.ShapeDtypeStruct((r, d), BF16)],
        in_specs=[pl.BlockSpec((tm, d), lambda i: (i + row_off, 0)),
                  pl.BlockSpec((None, 8, d), lambda i: (_sid(i, nct), 0, 0)),
                  _const_spec((1, d)), _const_spec(w1.shape), _const_spec(w2.shape)],
        out_specs=[pl.BlockSpec((tm, d), lambda i: (i, 0)), pl.BlockSpec((tm, f), lambda i: (i, 0)),
                   pl.BlockSpec((tm, d), lambda i: (i, 0))],
        compiler_params=_params(("arbitrary",)),
    )(h1, mods, ng, w1, w2)


def _mlp_bwd(dh2, h1, p, y, mods, ng, w1, w2, nct, tm, row_off=0, rider=None):
    r_rows, d = dh2.shape
    fc = w1.shape[2]
    f = N_DEV * fc

    def body(dh_ref, h_ref, p_ref, y_ref, mod_ref, ng_ref, w1_ref, w2_ref,
             dh1_ref, m_ref, du_ref, dacc_ref, st_ref):
        i = pl.program_id(0)
        dh = dh_ref[...]
        ngv, sc, gate = ng_ref[...], mod_ref[4:5, :], mod_ref[5:6, :]
        a, xhat, rstd, n = _normmod(h_ref[...], ngv, mod_ref[3:4, :], sc)
        m_ref[...] = a.astype(BF16)
        dgate = _colsum(dh * y_ref[...].astype(F32))
        dacc = (gate * dh).astype(BF16)
        dacc_ref[...] = dacc
        dm = jnp.zeros((tm, d), F32)
        for j in range(N_DEV):
            sl = slice(j * fc, (j + 1) * fc)
            pj = p_ref[:, sl].astype(F32)
            du = (_dot_nt(dacc, w2_ref[j]) * (2.0 * pj)).astype(BF16)
            du_ref[:, sl] = du
            dm = dm + _dot_nt(du, w1_ref[j])
        dhn, dsh, dsc, dng = _normmod_bwd(dm, xhat, rstd, n, ngv, sc)
        dh1_ref[...] = dh + dhn
        _acc_rows(st_ref, _first_of_stream(i, nct), [dsh, dsc, dgate, dng])

    outs, rode = _compute_call(
        body, name="mlp_bwd", grid=(r_rows // tm,), operands=(dh2, h1, p, y, mods, ng, w1, w2), rider=rider,
        out_shape=[jax.ShapeDtypeStruct((r_rows, d), F32), jax.ShapeDtypeStruct((r_rows, d), BF16),
                   jax.ShapeDtypeStruct((r_rows, f), BF16),
                   jax.ShapeDtypeStruct((r_rows, d), BF16), jax.ShapeDtypeStruct((_n_streams(nct), 8, d), F32)],
        in_specs=[pl.BlockSpec((tm, d), lambda i: (i, 0)),
                  pl.BlockSpec((tm, d), lambda i: (i + row_off, 0)),
                  pl.BlockSpec((tm, f), lambda i: (i, 0)), pl.BlockSpec((tm, d), lambda i: (i, 0)),
                  pl.BlockSpec((None, 8, d), lambda i: (_sid(i, nct), 0, 0)),
                  _const_spec((1, d)), _const_spec(w1.shape), _const_spec(w2.shape)],
        out_specs=[pl.BlockSpec((tm, d), lambda i: (i, 0)), pl.BlockSpec((tm, d), lambda i: (i, 0)),
                   pl.BlockSpec((tm, f), lambda i: (i, 0)),
                   pl.BlockSpec((tm, d), lambda i: (i, 0)),
                   pl.BlockSpec((None, 8, d), lambda i: (_stat_sid(i, nct), 0, 0))])
    return (*outs, rode)


def _pick(n, cands):
    for cand in cands:
        if n % cand == 0:
            return cand
    return n


def _tn_matmul(x, y, name, col_shards=False, square_x=False, rider=None):
    rows, k1 = x.shape
    k2 = y.shape[1]
    bt = _pick(rows, (1024, 768, 512, 384, 256, 128))
    bk1, bk2 = min(k1, 1024), min(k2, 1024)
    grid = (k1 // bk1, k2 // bk2, rows // bt)
    nt = rows // bt
    n = k2 // N_DEV
    if col_shards:
        assert bk2 % n == 0
        per = bk2 // n
        out_shape = jax.ShapeDtypeStruct((N_DEV, k1, n), BF16)
        out_spec = pl.BlockSpec((per, bk1, n), lambda i, j, t: (j, i, 0))
    else:
        out_shape = jax.ShapeDtypeStruct((k1, k2), BF16)
        out_spec = pl.BlockSpec((bk1, bk2), lambda i, j, t: (i, j))

    nj = k2 // bk2
    n_steps = grid[0] * nj * nt
    ring = 3

    def body(x_hbm, y_hbm, o_ref, acc_ref, xbuf, ybuf, xsem, ysem):
        t = pl.program_id(2)
        step = (pl.program_id(0) * nj + pl.program_id(1)) * nt + t

        def tiles(s):
            slot = s % ring
            ti, rest = s % nt, s // nt
            ji, ii = rest % nj, rest // nj
            r0 = pl.multiple_of(ti * bt, bt)
            return (pltpu.make_async_copy(x_hbm.at[pl.ds(r0, bt), pl.ds(pl.multiple_of(ii * bk1, bk1), bk1)],
                                          xbuf.at[slot], xsem.at[slot]),
                    pltpu.make_async_copy(y_hbm.at[pl.ds(r0, bt), pl.ds(pl.multiple_of(ji * bk2, bk2), bk2)],
                                          ybuf.at[slot], ysem.at[slot]))

        @pl.when(step == 0)
        def _():
            for s in range(min(ring - 1, n_steps)):
                for cp in tiles(s):
                    cp.start()

        @pl.when(step + ring - 1 < n_steps)
        def _():
            for cp in tiles(step + ring - 1):
                cp.start()
        for cp in tiles(step):
            cp.wait()

        @pl.when(t == 0)
        def _():
            acc_ref[...] = jnp.zeros_like(acc_ref)
        slot = step % ring
        xv = xbuf[slot]
        acc_ref[...] += _dot_tn(xv * xv if square_x else xv, ybuf[slot])

        @pl.when(t == nt - 1)
        def _():
            if col_shards:
                for s in range(per):
                    o_ref[s] = acc_ref[:, s * n:(s + 1) * n].astype(BF16)
            else:
                o_ref[...] = acc_ref[...].astype(BF16)

    any_spec = pl.BlockSpec(memory_space=pl.ANY)
    (out,), rode = _compute_call(
        body, name=name, grid=grid, operands=(x, y), rider=rider, out_shape=[out_shape],
        in_specs=[any_spec, any_spec], out_specs=[out_spec],
        scratch_shapes=[pltpu.VMEM((bk1, bk2), F32), pltpu.VMEM((ring, bt, bk1), BF16),
                        pltpu.VMEM((ring, bt, bk2), BF16), pltpu.SemaphoreType.DMA((ring,)),
                        pltpu.SemaphoreType.DMA((ring,))])
    return out if rider is None else (out, rode)


def _pool_bands(tm):
    k = tm + 128
    t = np.arange(tm)[:, None]
    e = np.arange(k)[None, :]
    fwd, bwd = [], []
    for w in POOL_WINDOWS:
        lo = POOL_HALO + t - w // 2
        fwd.append(((e >= lo) & (e <= lo + w - 1)).astype(np.float32))
        lo_t = POOL_HALO + t - w // 2 + 1
        bwd.append(((e >= lo_t) & (e <= lo_t + w - 1)).astype(np.float32))
    return jnp.asarray(np.stack(fwd), BF16), jnp.asarray(np.stack(bwd), BF16)


def _pool_geometry(i, nct, n_tiles, tm, c_len, l_len):
    if nct == 0:
        pos0 = i * tm
        ls = l_len
        has_prev = i > 0
        has_next = i < n_tiles - 1
    else:
        in_ctx = i < nct
        pos0 = jnp.where(in_ctx, i, i - nct) * tm
        ls = jnp.where(in_ctx, c_len, l_len)
        has_prev = jnp.logical_and(i != 0, i != nct)
        has_next = jnp.logical_and(i != nct - 1, i != n_tiles - 1)
    return pos0, ls, has_prev, has_next


def _window_inv_counts(pos, ls):
    out = []
    for w in POOL_WINDOWS:
        lo = jnp.maximum(pos - w // 2, 0)
        hi = jnp.minimum(pos + w - w // 2, ls)
        cnt = jnp.maximum(hi - lo, 1).astype(F32)
        out.append(1.0 / cnt)
    return out


def _split_bf16(x):
    hi = x.astype(BF16)
    return hi, (x - hi.astype(F32)).astype(BF16)


def _extend(prev, tile, nxt, has_prev, has_next):
    w = tile.shape[1]
    prev = jnp.where(has_prev, prev, 0.0)
    nxt = jnp.where(has_next, nxt, 0.0)
    return jnp.concatenate([prev, tile, nxt, jnp.zeros((128 - 2 * POOL_HALO, w), F32)], axis=0)


def _pool_specs(tm, d, n_rows):
    last8 = n_rows // POOL_HALO - 1
    per = tm // POOL_HALO
    return [pl.BlockSpec((tm, d), lambda i: (i, 0)),
            pl.BlockSpec((POOL_HALO, d), lambda i: (jnp.maximum(i * per - 1, 0), 0)),
            pl.BlockSpec((POOL_HALO, d), lambda i: (jnp.minimum((i + 1) * per, last8), 0))]


def _pool_fwd(h, mods, ng, w, scale, bands, nct, tm, c_len, l_len):
    r, d = h.shape
    gw = d // POOL_GROUPS
    n_tiles = r // tm
    kx = tm + 128

    def body(h_ref, hp_ref, hn_ref, mod_ref, ng_ref, w_ref, sc_ref, band_ref, y_ref, h1_ref):
        i = pl.program_id(0)
        pos0, ls, has_prev, has_next = _pool_geometry(i, nct, n_tiles, tm, c_len, l_len)
        ngv, sh, sc = ng_ref[...], mod_ref[0:1, :], mod_ref[1:2, :]
        h = h_ref[...]
        a = _normmod(h, ngv, sh, sc)[0]
        a_ext = _extend(_normmod(hp_ref[...], ngv, sh, sc)[0], a, _normmod(hn_ref[...], ngv, sh, sc)[0],
                        has_prev, has_next)
        pos = pos0 + lax.broadcasted_iota(jnp.int32, (tm, 1), 0)
        inv = _window_inv_counts(pos, ls)
        ys = []
        for g in range(POOL_GROUPS):
            cols = slice(g * gw, (g + 1) * gw)
            hi, lo = _split_bf16(a_ext[:, cols])
            s = _dot(band_ref[g], hi) + _dot(band_ref[g], lo)
            pg = s * inv[g] - a[:, cols]
            ys.append(_dot(pg.astype(BF16), w_ref[g]))
        y = jnp.concatenate(ys, axis=1) * sc_ref[...]
        y_ref[...] = y.astype(BF16)
        h1_ref[...] = h + mod_ref[2:3, :] * y

    return pl.pallas_call(
        body, name="pool_fwd", grid=(n_tiles,),
        out_shape=[jax.ShapeDtypeStruct((r, d), BF16), jax.ShapeDtypeStruct((r, d), F32)],
        in_specs=_pool_specs(tm, d, r) + [
            pl.BlockSpec((None, 8, d), lambda i: (_sid(i, nct), 0, 0)),
            _const_spec((1, d)), _const_spec(w.shape), _const_spec((1, d)), _const_spec((4, tm, kx))],
        out_specs=[pl.BlockSpec((tm, d), lambda i: (i, 0)), pl.BlockSpec((tm, d), lambda i: (i, 0))],
        compiler_params=_params(("arbitrary",)),
    )(h, h, h, mods, ng, w, scale, bands[0])


def _pool_bwd(dh1, h, y, mods, ng, w, scale, bands, nct, tm, c_len, l_len, latent_out, rider=None):
    r, d = h.shape
    gw = d // POOL_GROUPS
    n_tiles = r // tm
    kx = tm + 128
    out_rows = l_len if latent_out else r
    out_off = nct if latent_out else 0

    def body(dh_ref, dhp_ref, dhn_ref, h_ref, hp_ref, hn_ref, y_ref, mod_ref, ng_ref, w_ref, sc_ref,
             bf_ref, bb_ref, dho_ref, dw_ref, st_ref):
        i = pl.program_id(0)
        pos0, ls, has_prev, has_next = _pool_geometry(i, nct, n_tiles, tm, c_len, l_len)
        ngv, sh, sc, gate = ng_ref[...], mod_ref[0:1, :], mod_ref[1:2, :], mod_ref[2:3, :]
        scale_v = sc_ref[...]
        h = h_ref[...]
        a, xhat, rstd, n = _normmod(h, ngv, sh, sc)
        a_ext = _extend(_normmod(hp_ref[...], ngv, sh, sc)[0], a, _normmod(hn_ref[...], ngv, sh, sc)[0],
                        has_prev, has_next)
        dh = dh_ref[...]
        dgate = _colsum(dh * y_ref[...].astype(F32))
        dy = gate * dh
        dy_ext = _extend(gate * dhp_ref[...], dy, gate * dhn_ref[...], has_prev, has_next)
        dyp_ext = (dy_ext * scale_v).astype(BF16)
        dyp = (dy * scale_v).astype(BF16)
        pos = pos0 + lax.broadcasted_iota(jnp.int32, (tm, 1), 0)
        inv = _window_inv_counts(pos, ls)
        pos_e = pos0 - POOL_HALO + lax.broadcasted_iota(jnp.int32, (kx, 1), 0)
        inv_e = _window_inv_counts(pos_e, ls)

        @pl.when(i == 0)
        def _():
            dw_ref[...] = jnp.zeros_like(dw_ref)
        das, dscale = [], []
        for g in range(POOL_GROUPS):
            cols = slice(g * gw, (g + 1) * gw)
            hi, lo = _split_bf16(a_ext[:, cols])
            pg = ((_dot(bf_ref[g], hi) + _dot(bf_ref[g], lo)) * inv[g] - a[:, cols]).astype(BF16)
            dscale.append(_colsum(dy[:, cols] * _dot(pg, w_ref[g])))
            dyp_g = dyp_ext[:, cols]
            dw_ref[g] += _dot_tn(pg, dyp[:, cols])
            dp_ext = _dot_nt(dyp_g, w_ref[g])
            hi, lo = _split_bf16(dp_ext * inv_e[g])
            das.append(_dot(bb_ref[g], hi) + _dot(bb_ref[g], lo) - dp_ext[POOL_HALO:POOL_HALO + tm, :])
        da = jnp.concatenate(das, axis=1)
        dhn, dsh, dsc, dng = _normmod_bwd(da, xhat, rstd, n, ngv, sc)
        dho_ref[...] = dh + dhn
        _acc_rows(st_ref, _first_of_stream(i, nct), [dsh, dsc, dgate, dng, jnp.concatenate(dscale, axis=1)])

    outs, rode = _compute_call(
        body, name="pool_bwd", grid=(n_tiles,), rider=rider,
        operands=(dh1, dh1, dh1, h, h, h, y, mods, ng, w, scale, bands[0], bands[1]),
        out_shape=[jax.ShapeDtypeStruct((out_rows, d), F32),
                   jax.ShapeDtypeStruct((POOL_GROUPS, gw, gw), F32),
                   jax.ShapeDtypeStruct((_n_streams(nct), 8, d), F32)],
        in_specs=_pool_specs(tm, d, r) + _pool_specs(tm, d, r) + [
            pl.BlockSpec((tm, d), lambda i: (i, 0)),
            pl.BlockSpec((None, 8, d), lambda i: (_sid(i, nct), 0, 0)),
            _const_spec((1, d)), _const_spec(w.shape), _const_spec((1, d)),
            _const_spec((4, tm, kx)), _const_spec((4, tm, kx))],
        out_specs=[pl.BlockSpec((tm, d), lambda i: (jnp.maximum(i - out_off, 0), 0)),
                   pl.BlockSpec((POOL_GROUPS, gw, gw), lambda i: (0, 0, 0)),
                   pl.BlockSpec((None, 8, d), lambda i: (_stat_sid(i, nct), 0, 0))])
    return (*outs, rode)


def _rope_tables(c_len, l_len):
    half = HEAD_DIM // 2
    t = np.arange(l_len)
    row = (t // GRID_W).astype(np.float32)
    col = (t % GRID_W).astype(np.float32)
    inv = (np.float32(ROPE_BASE) ** (-np.arange(0, half, 2, dtype=np.float32) / np.float32(half))).astype(np.float32)
    ang_r = row[:, None] * inv[None, :]
    ang_c = col[:, None] * inv[None, :]
    cos = np.concatenate([np.cos(ang_r), np.cos(ang_r), np.cos(ang_c), np.cos(ang_c)], axis=1)
    sin = np.concatenate([-np.sin(ang_r), np.sin(ang_r), -np.sin(ang_c), np.sin(ang_c)], axis=1)
    cos = np.concatenate([np.ones((c_len, HEAD_DIM), np.float32), cos.astype(np.float32)], axis=0)
    sin = np.concatenate([np.zeros((c_len, HEAD_DIM), np.float32), sin.astype(np.float32)], axis=0)
    return jnp.asarray(cos, F32), jnp.asarray(sin, F32)


def _swap_pairs(x):
    lane = lax.broadcasted_iota(jnp.int32, x.shape, 1)
    return jnp.where((lane % 64) < 32, pltpu.roll(x, 96, 1), pltpu.roll(x, 32, 1))


def _head_norm(x, g):
    rstd = lax.rsqrt(jnp.mean(x * x, axis=-1, keepdims=True) + EPS)
    xhat = x * rstd
    return xhat * g, xhat, rstd


def _qkv_fwd(h, mods, ng, w, qg, kg, cos, sin, n_heads, n_kv, nct, tm):
    t_rows, d = h.shape
    qw, kw = n_heads * HEAD_DIM, n_kv * HEAD_DIM

    def body(h_ref, mod_ref, ng_ref, w_ref, qg_ref, kg_ref, cos_ref, sin_ref, q_ref, k_ref, v_ref, qt_ref):
        a = _normmod(h_ref[...], ng_ref[...], mod_ref[0:1, :], mod_ref[1:2, :])[0]
        qkv = _dot(a.astype(BF16), w_ref[...])
        cosv, sinv = cos_ref[...], sin_ref[...]
        ones = jnp.ones((tm, HEAD_DIM), BF16)
        for hd in range(n_heads + n_kv):
            cols = slice(hd * HEAD_DIM, (hd + 1) * HEAD_DIM)
            xn = _head_norm(qkv[:, cols], qg_ref[...] if hd < n_heads else kg_ref[...])[0]
            xr = xn * cosv + _swap_pairs(xn) * sinv
            if hd < n_heads:
                qs = xr * Q_SCALE
                q_ref[:, cols] = qs.astype(BF16)
                qt_ref[cols, :] = qs.T.astype(BF16)
            else:
                k_ref[:, (hd - n_heads) * HEAD_DIM:(hd - n_heads + 1) * HEAD_DIM] = xr.astype(BF16)
        for g in range(n_kv):
            v_ref[:, (2 * g) * HEAD_DIM:(2 * g + 1) * HEAD_DIM] = (
                qkv[:, qw + kw + g * HEAD_DIM:qw + kw + (g + 1) * HEAD_DIM].astype(BF16))
            v_ref[:, (2 * g + 1) * HEAD_DIM:(2 * g + 2) * HEAD_DIM] = ones

    return pl.pallas_call(
        body, name="qkv_fwd", grid=(t_rows // tm,),
        out_shape=[jax.ShapeDtypeStruct((t_rows - nct * tm, qw), BF16), jax.ShapeDtypeStruct((t_rows, kw), BF16),
                   jax.ShapeDtypeStruct((t_rows, 2 * kw), BF16), jax.ShapeDtypeStruct((qw, t_rows - nct * tm), BF16)],
        in_specs=[pl.BlockSpec((tm, d), lambda i: (i, 0)),
                  pl.BlockSpec((None, 8, d), lambda i: (_sid(i, nct), 0, 0)),
                  _const_spec((1, d)), _const_spec(w.shape), _const_spec((1, HEAD_DIM)),
                  _const_spec((1, HEAD_DIM)),
                  pl.BlockSpec((tm, HEAD_DIM), lambda i: (i, 0)), pl.BlockSpec((tm, HEAD_DIM), lambda i: (i, 0))],
        out_specs=[pl.BlockSpec((tm, qw), lambda i: (jnp.maximum(i - nct, 0), 0)),
                   pl.BlockSpec((tm, kw), lambda i: (i, 0)), pl.BlockSpec((tm, 2 * kw), lambda i: (i, 0)),
                   pl.BlockSpec((qw, tm), lambda i: (0, jnp.maximum(i - nct, 0)))],
        compiler_params=_params(("arbitrary",)),
    )(h, mods, ng, w, qg, kg, cos, sin)


def _flash_tk(t_rows, tm):
    best = tm
    k = tm
    while k <= FLASH_TK_CAP:
        if t_rows % k == 0:
            best = k
        k += tm
    return best


def _flash_fwd(q, k, v1, n_kv, tq, tm, rider=None):
    t_rows = k.shape[0]
    l_rows = q.shape[0]
    tk = _flash_tk(t_rows, tm)
    nk = t_rows // tk
    gq = 2 * HEAD_DIM

    def body(q_ref, k_ref, v_ref, o_ref, lse_ref, m_s, acc_s, s_s):
        ki = pl.program_id(2)

        @pl.when(ki == 0)
        def _():
            m_s[...] = jnp.full_like(m_s, -jnp.inf)
            acc_s[...] = jnp.zeros_like(acc_s)
        kk, vv = k_ref[...], v_ref[...]
        for hh in range(2):
            s_s[hh] = _dot_nt(q_ref[:, hh * HEAD_DIM:(hh + 1) * HEAD_DIM], kk)
        for hh in range(2):
            s = s_s[hh]
            m_prev = m_s[hh]
            m_new = jnp.maximum(m_prev, jnp.max(s, axis=-1, keepdims=True))
            alpha = jnp.exp2(m_prev - m_new)
            p = jnp.exp2(s - jnp.tile(m_new, (1, tk // HEAD_DIM)))
            acc_s[hh] = jnp.tile(alpha, (1, 2)) * acc_s[hh] + _dot(p.astype(BF16), vv)
            m_s[hh] = m_new

        @pl.when(ki == nk - 1)
        def _():
            for hh in range(2):
                acc = acc_s[hh]
                l = acc[:, HEAD_DIM:]
                o_ref[:, hh * HEAD_DIM:(hh + 1) * HEAD_DIM] = (acc[:, :HEAD_DIM] / l).astype(BF16)
                lse_ref[:, hh:hh + 1] = (m_s[hh] + jnp.log2(l))[:, 0:1]

    (o, lse), rode = _compute_call(
        body, name="flash_fwd", grid=(n_kv, l_rows // tq, nk), operands=(q, k, v1), rider=rider,
        out_shape=[jax.ShapeDtypeStruct((l_rows, n_kv * gq), BF16),
                   jax.ShapeDtypeStruct((n_kv, l_rows, 2), F32)],
        in_specs=[pl.BlockSpec((tq, gq), lambda g, i, j: (i, g)),
                  pl.BlockSpec((tk, HEAD_DIM), lambda g, i, j: (j, g)),
                  pl.BlockSpec((tk, gq), lambda g, i, j: (j, g))],
        out_specs=[pl.BlockSpec((tq, gq), lambda g, i, j: (i, g)),
                   pl.BlockSpec((None, tq, 2), lambda g, i, j: (g, i, 0))],
        scratch_shapes=[pltpu.VMEM((2, tq, HEAD_DIM), F32), pltpu.VMEM((2, tq, gq), F32),
                        pltpu.VMEM((2, tq, tk), F32)])
    return o, lse, rode


def _flash_bwd(q, qt, k, v1, do, dot, lse, delta, n_kv, tq, tm, rider=None):
    t_rows = k.shape[0]
    l_rows = q.shape[0]
    tk = _flash_tk(t_rows, tm)
    nq = l_rows // tq
    gq = 2 * HEAD_DIM

    def body(q_ref, qt_ref, k_ref, v_ref, do_ref, dot_ref, lse_ref, dl_ref, dq_ref, dkt_ref, dvt_ref):
        ki, qi = pl.program_id(1), pl.program_id(2)
        rows = pl.ds(pl.multiple_of(qi * tq, tq), tq)

        @pl.when(qi == 0)
        def _():
            dkt_ref[...] = jnp.zeros_like(dkt_ref)
            dvt_ref[...] = jnp.zeros_like(dvt_ref)

        @pl.when(ki == 0)
        def _():
            dq_ref[rows, :] = jnp.zeros((tq, gq), F32)
        kk, vv = k_ref[...], v_ref[:, :HEAD_DIM]
        dkt_parts, dvt_parts = [], []
        for hh in range(2):
            cols = slice(hh * HEAD_DIM, (hh + 1) * HEAD_DIM)
            p = jnp.exp2(_dot_nt(q_ref[:, cols], kk) - lse_ref[:, hh:hh + 1])
            ds = (p * (_dot_nt(do_ref[:, cols], vv) - dl_ref[:, hh:hh + 1])).astype(BF16)
            dvt_parts.append(_dot(dot_ref[cols, :], p.astype(BF16)))
            dkt_parts.append(_dot(qt_ref[cols, :], ds))
            dq_ref[rows, cols] += _dot(ds, kk)
        dvt_ref[...] += dvt_parts[0] + dvt_parts[1]
        dkt_ref[...] += dkt_parts[0] + dkt_parts[1]

    (dq, dkt, dvt), rode = _compute_call(
        body, name="flash_bwd", grid=(n_kv, t_rows // tk, nq), operands=(q, qt, k, v1, do, dot, lse, delta),
        rider=rider,
        out_shape=[jax.ShapeDtypeStruct((l_rows, n_kv * gq), F32),
                   jax.ShapeDtypeStruct((n_kv * HEAD_DIM, t_rows), F32),
                   jax.ShapeDtypeStruct((n_kv * HEAD_DIM, t_rows), F32)],
        in_specs=[pl.BlockSpec((tq, gq), lambda g, j, i: (i, g)),
                  pl.BlockSpec((gq, tq), lambda g, j, i: (g, i)),
                  pl.BlockSpec((tk, HEAD_DIM), lambda g, j, i: (j, g)),
                  pl.BlockSpec((tk, gq), lambda g, j, i: (j, g)),
                  pl.BlockSpec((tq, gq), lambda g, j, i: (i, g)),
                  pl.BlockSpec((gq, tq), lambda g, j, i: (g, i)),
                  pl.BlockSpec((None, tq, 2), lambda g, j, i: (g, i, 0)),
                  pl.BlockSpec((None, tq, 2), lambda g, j, i: (g, i, 0))],
        out_specs=[pl.BlockSpec((l_rows, gq), lambda g, j, i: (0, g)),
                   pl.BlockSpec((HEAD_DIM, tk), lambda g, j, i: (g, j)),
                   pl.BlockSpec((HEAD_DIM, tk), lambda g, j, i: (g, j))])
    return dq, dkt, dvt, rode


def _wo_fwd(h, o, wo, mods, nct, tm, rider=None):
    l_rows, z = o.shape
    d = h.shape[1]

    def body(h_ref, o_ref, w_ref, mod_ref, y_ref, h1_ref):
        y = _dot(o_ref[...], w_ref[...])
        y_ref[...] = y.astype(BF16)
        h1_ref[...] = h_ref[...] + mod_ref[2:3, :] * y

    (y, h1), rode = _compute_call(
        body, name="wo_fwd", grid=(l_rows // tm,), operands=(h, o, wo, mods), rider=rider,
        out_shape=[jax.ShapeDtypeStruct((l_rows, d), BF16), jax.ShapeDtypeStruct((l_rows, d), F32)],
        in_specs=[pl.BlockSpec((tm, d), lambda i: (i + nct, 0)), pl.BlockSpec((tm, z), lambda i: (i, 0)),
                  _const_spec(wo.shape), pl.BlockSpec((None, 8, d), lambda i: (1, 0, 0))],
        out_specs=[pl.BlockSpec((tm, d), lambda i: (i, 0)), pl.BlockSpec((tm, d), lambda i: (i, 0))])
    return y, h1, rode


def _wo_bwd(dh1, y, o, wo, mods, n_kv, tm):
    l_rows, z = o.shape
    d = dh1.shape[1]

    def body(dh_ref, y_ref, o_ref, w_ref, mod_ref, dy_ref, do_ref, dot_ref, dl_ref, st_ref):
        i = pl.program_id(0)
        dh = dh_ref[...]
        dgate = _colsum(dh * y_ref[...].astype(F32))
        dy = (mod_ref[2:3, :] * dh).astype(BF16)
        dy_ref[...] = dy
        do = _dot_nt(dy, w_ref[...])
        do_ref[...] = do.astype(BF16)
        dot_ref[...] = do.T.astype(BF16)
        prod = do * o_ref[...].astype(F32)
        for g in range(n_kv):
            d0 = jnp.sum(prod[:, (2 * g) * HEAD_DIM:(2 * g + 1) * HEAD_DIM], axis=-1, keepdims=True)
            d1 = jnp.sum(prod[:, (2 * g + 1) * HEAD_DIM:(2 * g + 2) * HEAD_DIM], axis=-1, keepdims=True)
            dl_ref[g] = jnp.concatenate([d0, d1], axis=1)
        zero = jnp.zeros((1, d), F32)
        _acc_rows(st_ref, i == 0, [zero, zero, dgate])

    return pl.pallas_call(
        body, name="wo_bwd", grid=(l_rows // tm,),
        out_shape=[jax.ShapeDtypeStruct((l_rows, d), BF16), jax.ShapeDtypeStruct((l_rows, z), BF16),
                   jax.ShapeDtypeStruct((z, l_rows), BF16),
                   jax.ShapeDtypeStruct((n_kv, l_rows, 2), F32), jax.ShapeDtypeStruct((8, d), F32)],
        in_specs=[pl.BlockSpec((tm, d), lambda i: (i, 0)), pl.BlockSpec((tm, d), lambda i: (i, 0)),
                  pl.BlockSpec((tm, z), lambda i: (i, 0)), _const_spec(wo.shape),
                  pl.BlockSpec((None, 8, d), lambda i: (1, 0, 0))],
        out_specs=[pl.BlockSpec((tm, d), lambda i: (i, 0)), pl.BlockSpec((tm, z), lambda i: (i, 0)),
                   pl.BlockSpec((z, tm), lambda i: (0, i)),
                   pl.BlockSpec((n_kv, tm, 2), lambda i: (0, i, 0)), pl.BlockSpec((8, d), lambda i: (0, 0))],
        compiler_params=_params(("arbitrary",)),
    )(dh1, y, o, wo, mods)


def _qkv_bwd(h, dh_lat, dq, dkt, dvt, mods, ng, w, qg, kg, cos, sin, n_heads, n_kv, nct, tm):
    t_rows, d = h.shape
    qw, kw = n_heads * HEAD_DIM, n_kv * HEAD_DIM
    scale = HEAD_DIM ** -0.5

    def body(h_ref, dhl_ref, dq_ref, dkt_ref, dvt_ref, mod_ref, ng_ref, w_ref, qg_ref, kg_ref, cos_ref,
             sin_ref, dh_ref, a_ref, dqkv_ref, st_ref, dg_ref):
        i = pl.program_id(0)
        lat = (i >= nct).astype(F32)
        dk_t = dkt_ref[...].T * (1.0 / LOG2E)
        ngv, sc = ng_ref[...], mod_ref[1:2, :]
        a, xhat, rstd, n = _normmod(h_ref[...], ngv, mod_ref[0:1, :], sc)
        ab = a.astype(BF16)
        a_ref[...] = ab
        qkv = _dot(ab, w_ref[...])
        cosv, sinv = cos_ref[...], sin_ref[...]
        dqg = jnp.zeros((1, HEAD_DIM), F32)
        dkg = jnp.zeros((1, HEAD_DIM), F32)
        for hd in range(n_heads + n_kv):
            cols = slice(hd * HEAD_DIM, (hd + 1) * HEAD_DIM)
            is_q = hd < n_heads
            g = qg_ref[...] if is_q else kg_ref[...]
            _, hx, hr = _head_norm(qkv[:, cols], g)
            if is_q:
                dxr = dq_ref[:, cols] * (scale * lat)
            else:
                dxr = dk_t[:, (hd - n_heads) * HEAD_DIM:(hd - n_heads + 1) * HEAD_DIM]
            dxn = dxr * cosv + _swap_pairs(dxr * sinv)
            if is_q:
                dqg = dqg + _colsum(dxn * hx)
            else:
                dkg = dkg + _colsum(dxn * hx)
            dxh = dxn * g
            dx = hr * (dxh - hx * jnp.mean(dxh * hx, axis=-1, keepdims=True))
            dqkv_ref[:, cols] = dx.astype(BF16)
        dqkv_ref[:, qw + kw:] = dvt_ref[...].T.astype(BF16)
        da = _dot_nt(dqkv_ref[...], w_ref[...])
        dhn, dsh, dsc, dng = _normmod_bwd(da, xhat, rstd, n, ngv, sc)
        dh_ref[...] = dhl_ref[...] * lat + dhn
        _acc_rows(st_ref, _first_of_stream(i, nct), [dsh, dsc, jnp.zeros((1, d), F32), dng])
        _acc_rows(dg_ref, i == 0, [dqg, dkg])

    lat_map = lambda i: (jnp.maximum(i - nct, 0), 0)
    return pl.pallas_call(
        body, name="qkv_bwd", grid=(t_rows // tm,),
        out_shape=[jax.ShapeDtypeStruct((t_rows, d), F32), jax.ShapeDtypeStruct((t_rows, d), BF16),
                   jax.ShapeDtypeStruct((t_rows, qw + 2 * kw), BF16), jax.ShapeDtypeStruct((2, 8, d), F32),
                   jax.ShapeDtypeStruct((8, HEAD_DIM), F32)],
        in_specs=[pl.BlockSpec((tm, d), lambda i: (i, 0)), pl.BlockSpec((tm, d), lat_map),
                  pl.BlockSpec((tm, qw), lat_map), pl.BlockSpec((kw, tm), lambda i: (0, i)),
                  pl.BlockSpec((kw, tm), lambda i: (0, i)),
                  pl.BlockSpec((None, 8, d), lambda i: (_sid(i, nct), 0, 0)),
                  _const_spec((1, d)), _const_spec(w.shape), _const_spec((1, HEAD_DIM)),
                  _const_spec((1, HEAD_DIM)),
                  pl.BlockSpec((tm, HEAD_DIM), lambda i: (i, 0)), pl.BlockSpec((tm, HEAD_DIM), lambda i: (i, 0))],
        out_specs=[pl.BlockSpec((tm, d), lambda i: (i, 0)), pl.BlockSpec((tm, d), lambda i: (i, 0)),
                   pl.BlockSpec((tm, qw + 2 * kw), lambda i: (i, 0)),
                   pl.BlockSpec((None, 8, d), lambda i: (_sid(i, nct), 0, 0)),
                   pl.BlockSpec((8, HEAD_DIM), lambda i: (0, 0))],
        compiler_params=_params(("arbitrary",)),
    )(h, dh_lat, dq, dkt, dvt, mods, ng, w, qg, kg, cos, sin)


def _gmlp_core(a_bf, win_ref, lng, lnb, ws_ref, bst_ref, tm, half, with_grad=False):
    blocks = [_dot(a_bf, win_ref[j]) for j in range(N_DEV)]
    zu = jnp.concatenate(blocks[:N_DEV // 2], axis=1)
    zv = jnp.concatenate(blocks[N_DEV // 2:], axis=1)
    if with_grad:
        (u, zu), (v, zv) = _gelu_and_grad(zu), _gelu_and_grad(zv)
    else:
        u, v, zu, zv = _gelu(zu), _gelu(zv), None, None
    mu = jnp.mean(v, axis=-1, keepdims=True)
    vc = v - mu
    rstd_v = lax.rsqrt(jnp.mean(vc * vc, axis=-1, keepdims=True) + EPS)
    vhat = vc * rstd_v
    vln = (vhat * lng + lnb).astype(BF16)
    gw = half // GMLP_GROUPS
    rows = []
    for ch in range(tm // CHUNK):
        rs = slice(ch * CHUNK, (ch + 1) * CHUNK)
        cols = []
        for g in range(GMLP_GROUPS):
            cs = slice(g * gw, (g + 1) * gw)
            cols.append(_dot(ws_ref[g], vln[rs, cs]) + bst_ref[:, g:g + 1])
        rows.append(jnp.concatenate(cols, axis=1))
    sv = rows[0] if len(rows) == 1 else jnp.concatenate(rows, axis=0)
    return zu, zv, u, vhat, rstd_v, vln, sv


def _gmlp_fwd(h, mods, ng, win, lng, lnb, ws, bst, wout, tm):
    l_rows, d = h.shape
    half = wout.shape[0]

    def body(h_ref, mod_ref, ng_ref, win_ref, lng_ref, lnb_ref, ws_ref, bst_ref, wout_ref, y_ref, h1_ref):
        hv = h_ref[...]
        a = _normmod(hv, ng_ref[...], mod_ref[0:1, :], mod_ref[1:2, :])[0]
        _, _, u, _, _, _, sv = _gmlp_core(a.astype(BF16), win_ref, lng_ref[...], lnb_ref[...], ws_ref,
                                          bst_ref, tm, half)
        y = _dot((u * sv).astype(BF16), wout_ref[...])
        y_ref[...] = y.astype(BF16)
        h1_ref[...] = hv + mod_ref[2:3, :] * y

    return pl.pallas_call(
        body, name="gmlp_fwd", grid=(l_rows // tm,),
        out_shape=[jax.ShapeDtypeStruct((l_rows, d), BF16), jax.ShapeDtypeStruct((l_rows, d), F32)],
        in_specs=[pl.BlockSpec((tm, d), lambda i: (i, 0)), pl.BlockSpec((None, 8, d), lambda i: (1, 0, 0)),
                  _const_spec((1, d)), _const_spec(win.shape), _const_spec((1, half)), _const_spec((1, half)),
                  _const_spec(ws.shape), _const_spec(bst.shape), _const_spec(wout.shape)],
        out_specs=[pl.BlockSpec((tm, d), lambda i: (i, 0)), pl.BlockSpec((tm, d), lambda i: (i, 0))],
        compiler_params=_params(("arbitrary",)),
    )(h, mods, ng, win, lng, lnb, ws, bst, wout)


def _gmlp_bwd(dh1, h, y, mods, ng, win, lng, lnb, ws, wst, bst, wout, tm):
    l_rows, d = h.shape
    half = wout.shape[0]
    gw = half // GMLP_GROUPS

    def body(dh_ref, h_ref, y_ref, mod_ref, ng_ref, win_ref, lng_ref, lnb_ref, ws_ref, wst_ref, bst_ref,
             wout_ref, dho_ref, a_ref, dz_ref, gt_ref, dy_ref, st_ref, ln_ref, dws_ref, dbs_ref):
        i = pl.program_id(0)
        ngv, sc = ng_ref[...], mod_ref[1:2, :]
        lngv = lng_ref[...]
        a, xhat, rstd, n = _normmod(h_ref[...], ngv, mod_ref[0:1, :], sc)
        ab = a.astype(BF16)
        a_ref[...] = ab
        gu, gv, u, vhat, rstd_v, vln, sv = _gmlp_core(ab, win_ref, lngv, lnb_ref[...], ws_ref, bst_ref,
                                                      tm, half, with_grad=True)
        gt_ref[...] = (u * sv).astype(BF16)
        dh = dh_ref[...]
        dgate = _colsum(dh * y_ref[...].astype(F32))
        dy = (mod_ref[2:3, :] * dh).astype(BF16)
        dy_ref[...] = dy
        dgated = _dot_nt(dy, wout_ref[...])
        du = dgated * sv
        dsv = (dgated * u).astype(BF16)

        @pl.when(i == 0)
        def _():
            dws_ref[...] = jnp.zeros_like(dws_ref)
            dbs_ref[...] = jnp.zeros_like(dbs_ref)
        lane = lax.broadcasted_iota(jnp.int32, (CHUNK, 128), 1)
        dbs = jnp.zeros((CHUNK, 128), F32)
        rows = []
        for ch in range(tm // CHUNK):
            rs = slice(ch * CHUNK, (ch + 1) * CHUNK)
            cols = []
            for g in range(GMLP_GROUPS):
                cs = slice(g * gw, (g + 1) * gw)
                dsv_cg = dsv[rs, cs]
                dws_ref[g] += _dot_nt(dsv_cg, vln[rs, cs])
                cols.append(_dot(wst_ref[g], dsv_cg))
                dbs = dbs + jnp.where(lane == g, jnp.sum(dsv_cg.astype(F32), axis=-1, keepdims=True), 0.0)
            rows.append(jnp.concatenate(cols, axis=1))
        dbs_ref[...] += dbs
        dvln = rows[0] if len(rows) == 1 else jnp.concatenate(rows, axis=0)
        dlng = _colsum(dvln * vhat)
        dlnb = _colsum(dvln)
        dvh = dvln * lngv
        dv = rstd_v * (dvh - jnp.mean(dvh, axis=-1, keepdims=True)
                       - vhat * jnp.mean(dvh * vhat, axis=-1, keepdims=True))
        dz_ref[:, :half] = (du * gu).astype(BF16)
        dz_ref[:, half:] = (dv * gv).astype(BF16)
        nb = 2 * half // N_DEV
        da = _dot_nt(dz_ref[:, 0:nb], win_ref[0])
        for j in range(1, N_DEV):
            da = da + _dot_nt(dz_ref[:, j * nb:(j + 1) * nb], win_ref[j])
        dhn, dsh, dsc, dng = _normmod_bwd(da, xhat, rstd, n, ngv, sc)
        dho_ref[...] = dh + dhn
        _acc_rows(st_ref, i == 0, [dsh, dsc, dgate, dng])
        _acc_rows(ln_ref, i == 0, [dlng, dlnb])

    row = lambda w: pl.BlockSpec((tm, w), lambda i: (i, 0))
    return pl.pallas_call(
        body, name="gmlp_bwd", grid=(l_rows // tm,),
        out_shape=[jax.ShapeDtypeStruct((l_rows, d), F32), jax.ShapeDtypeStruct((l_rows, d), BF16),
                   jax.ShapeDtypeStruct((l_rows, 2 * half), BF16), jax.ShapeDtypeStruct((l_rows, half), BF16),
                   jax.ShapeDtypeStruct((l_rows, d), BF16), jax.ShapeDtypeStruct((8, d), F32),
                   jax.ShapeDtypeStruct((8, half), F32), jax.ShapeDtypeStruct(ws.shape, F32),
                   jax.ShapeDtypeStruct((CHUNK, 128), F32)],
        in_specs=[row(d), row(d), row(d), pl.BlockSpec((None, 8, d), lambda i: (1, 0, 0)),
                  _const_spec((1, d)), _const_spec(win.shape), _const_spec((1, half)), _const_spec((1, half)),
                  _const_spec(ws.shape), _const_spec(ws.shape), _const_spec(bst.shape), _const_spec(wout.shape)],
        out_specs=[row(d), row(d), row(2 * half), row(half), row(d),
                   pl.BlockSpec((8, d), lambda i: (0, 0)), pl.BlockSpec((8, half), lambda i: (0, 0)),
                   pl.BlockSpec(ws.shape, lambda i: (0, 0, 0)), pl.BlockSpec((CHUNK, 128), lambda i: (0, 0))],
        compiler_params=_params(("arbitrary",)),
    )(dh1, h, y, mods, ng, win, lng, lnb, ws, wst, bst, wout)


def _head(h, final_g, target, tm):
    l_rows, d = h.shape
    n_tiles = l_rows // tm

    def body(h_ref, g_ref, t_ref, dh_ref, loss_ref, dg_ref, acc_ref):
        i = pl.program_id(0)
        g = g_ref[...]
        hv = h_ref[...]
        rstd = lax.rsqrt(jnp.mean(hv * hv, axis=-1, keepdims=True) + EPS)
        xhat = hv * rstd
        e = xhat * g - t_ref[...]
        dout = e * (1.0 / d)
        dxhat = dout * g
        dh_ref[...] = rstd * (dxhat - xhat * jnp.mean(dxhat * xhat, axis=-1, keepdims=True))
        _acc_rows(dg_ref, i == 0, [_colsum(dout * xhat)])
        _acc_rows(acc_ref, i == 0, [_colsum(e * e)])

        @pl.when(i == n_tiles - 1)
        def _():
            total = jnp.sum(acc_ref[0:1, :], axis=-1, keepdims=True) * (0.5 / d)
            loss_ref[...] = jnp.broadcast_to(total, loss_ref.shape)

    return pl.pallas_call(
        body, name="loss_head", grid=(n_tiles,),
        out_shape=[jax.ShapeDtypeStruct((l_rows, d), F32), jax.ShapeDtypeStruct((8, 128), F32),
                   jax.ShapeDtypeStruct((8, d), F32)],
        in_specs=[pl.BlockSpec((tm, d), lambda i: (i, 0)), _const_spec((1, d)),
                  pl.BlockSpec((tm, d), lambda i: (i, 0))],
        out_specs=[pl.BlockSpec((tm, d), lambda i: (i, 0)), pl.BlockSpec((8, 128), lambda i: (0, 0)),
                   pl.BlockSpec((8, d), lambda i: (0, 0))],
        scratch_shapes=[pltpu.VMEM((8, d), F32)],
        compiler_params=_params(("arbitrary",)),
    )(h, final_g, target)


def _adamw(w, gparts, m, v, name):
    shape = w.shape
    cols = shape[-1]
    rows = int(np.prod(shape[:-1])) if len(shape) > 1 else 1
    pieces = list(gparts) if isinstance(gparts, (list, tuple)) else [gparts]
    n_pieces = len(pieces)
    nparts = pieces[0].shape[0]
    piece_rows = rows // n_pieces
    w2, m2, v2 = (t.reshape(rows, cols) for t in (w, m, v))
    pieces = [g.reshape(nparts, piece_rows, cols) for g in pieces]
    tr = piece_rows
    part_bytes = nparts * cols * pieces[0].dtype.itemsize
    for cand in (1024, 512, 256, 128, 64, 32, 16, 8):
        if piece_rows * max(part_bytes, cols * 4) <= (2 << 20):
            break
        if piece_rows % cand == 0 and cand < piece_rows:
            tr = cand
            if cand * max(part_bytes, cols * 4) <= (2 << 20):
                break
    per_piece = piece_rows // tr
    c1 = 1.0 - ADAM_B1 ** ADAM_STEP
    c2 = 1.0 - ADAM_B2 ** ADAM_STEP

    def update(w_ref, g_ref, m_ref, v_ref, go_ref, d_ref, mo_ref, vo_ref):
        g = g_ref[0].astype(F32)
        for k in range(1, nparts):
            g = g + g_ref[k].astype(F32)
        mn = ADAM_B1 * m_ref[...] + (1.0 - ADAM_B1) * g
        vn = ADAM_B2 * v_ref[...] + (1.0 - ADAM_B2) * (g * g)
        go_ref[...] = g
        mo_ref[...] = mn
        vo_ref[...] = vn
        d_ref[...] = -ADAM_LR * ((mn / c1) / (jnp.sqrt(vn / c2) + ADAM_EPS) + ADAM_WD * w_ref[...])

    def body(w_ref, *refs):
        g_refs, (m_ref, v_ref, go_ref, d_ref, mo_ref, vo_ref) = refs[:n_pieces], refs[n_pieces:]
        if n_pieces == 1:
            update(w_ref, g_refs[0], m_ref, v_ref, go_ref, d_ref, mo_ref, vo_ref)
        else:
            piece = pl.program_id(0) // per_piece
            for k in range(n_pieces):
                pl.when(piece == k)(functools.partial(update, w_ref, g_refs[k], m_ref, v_ref, go_ref, d_ref, mo_ref, vo_ref))

    def piece_spec(k):
        return pl.BlockSpec((nparts, tr, cols), lambda i: (0, jnp.clip(i - k * per_piece, 0, per_piece - 1), 0))

    spec = pl.BlockSpec((tr, cols), lambda i: (i, 0))
    outs = pl.pallas_call(
        body, name=name, grid=(rows // tr,),
        out_shape=[jax.ShapeDtypeStruct((rows, cols), F32)] * 4,
        in_specs=[spec] + [piece_spec(k) for k in range(n_pieces)] + [spec, spec],
        out_specs=[spec] * 4,
        compiler_params=_params(("arbitrary",)),
    )(w2, *pieces, m2, v2)
    return tuple(o.reshape(shape) for o in outs)


def _natural_cols(g):
    return jnp.moveaxis(g, 0, -2).reshape(g.shape[1:-1] + (N_DEV * g.shape[-1],))


def _natural_rows(g):
    return jnp.moveaxis(g, 0, -3).reshape(g.shape[1:-2] + (N_DEV * g.shape[-2], g.shape[-1]))


def _shard_rows(full):
    r = full.shape[-2] // N_DEV
    return jnp.moveaxis(full.reshape(full.shape[:-2] + (N_DEV, r, full.shape[-1])), -3, 0)


def _my_cols(gathered, me, n):
    return lax.dynamic_slice_in_dim(gathered, me * n, n, axis=gathered.ndim - 1)


def kernel(x, c, ctx, c_ctx, ada_w, ada_b, norm_g, mlp_w1, mlp_w2, pool_w, pool_scale, attn_w_qkv, attn_w_o, attn_q_g, attn_k_g, gm_w_in, gm_ln_g, gm_ln_b, gm_ws, gm_bs, gm_w_out, final_g, loss_target, m_c_ctx, m_ada_w, m_ada_b, m_norm_g, m_mlp_w1, m_mlp_w2, m_pool_w, m_pool_scale, m_attn_w_qkv, m_attn_w_o, m_attn_q_g, m_attn_k_g, m_gm_w_in, m_gm_ln_g, m_gm_ln_b, m_gm_ws, m_gm_bs, m_gm_w_out, m_final_g, v_c_ctx, v_ada_w, v_ada_b, v_norm_g, v_mlp_w1, v_mlp_w2, v_pool_w, v_pool_scale, v_attn_w_qkv, v_attn_w_o, v_attn_q_g, v_attn_k_g, v_gm_w_in, v_gm_ln_g, v_gm_ln_b, v_gm_ws, v_gm_bs, v_gm_w_out, v_final_g):
    l_len, d = x.shape[1], x.shape[2]
    c_len = ctx.shape[1]
    n_layers = ada_w.shape[0]
    assert n_layers == 4 and x.shape[0] == 1
    n_heads = d // HEAD_DIM
    n_kv = n_heads // 2
    half = gm_w_out.shape[1] * N_DEV
    tm = c_len if c_len <= 256 else 256
    assert c_len % tm == 0 and l_len % tm == 0 and tm % CHUNK == 0 and l_len % GRID_W == 0
    nct = c_len // tm
    me = _dev_index(*_coords())
    n_ada = ada_w.shape[-1]

    first = [t.astype(BF16) for t in (mlp_w1[0], mlp_w2[0], pool_w, attn_w_qkv[0])]
    small = [c, norm_g.reshape(n_layers * 2, -1), pool_scale, gm_ln_g, gm_ln_b]
    w1_0g, w2_0g, pool_g, qkv_g, c_all, ng_g, ps_g, lng_g, lnb_g = _all_gather(first + small, "gather_first")
    c_all = c_all.reshape(N_DEV, d)
    later = _GatherAcrossChips([t.astype(BF16) for t in
                                (mlp_w1[1], mlp_w1[2], mlp_w1[3], mlp_w2[1], mlp_w2[2], mlp_w2[3],
                                 attn_w_o[0], gm_w_in[0], gm_w_out[0])])
    pool_wf = _natural_rows(pool_g)
    wqkv = _natural_cols(qkv_g)
    ng_full = _natural_cols(ng_g.reshape(N_DEV, n_layers * 2, 1, -1)).reshape(n_layers, 2, 1, d)
    ps_full = _natural_cols(ps_g.reshape(N_DEV, 2, 1, -1))
    lng_full = _natural_cols(lng_g.reshape(N_DEV, 1, -1))
    lnb_full = _natural_cols(lnb_g.reshape(N_DEV, 1, -1))

    c_ctx2 = c_ctx.reshape(1, d)
    ada_b_loc = lax.dynamic_slice_in_dim(ada_b, me * n_ada, n_ada, axis=1).reshape(n_layers, 1, n_ada)
    (mod_g,) = _all_gather([_mods_local(c_all, c_ctx2, ada_w, ada_b_loc)], "gather_mods")
    mod_full = jnp.moveaxis(mod_g, 0, 2).reshape(n_layers, 16, 6, d)
    mod_lat = lax.dynamic_index_in_dim(mod_full, me, axis=1, keepdims=False)
    mod_ctx = mod_full[:, 8]
    mods = jnp.stack([mod_ctx, mod_lat], axis=1)
    mods = jnp.concatenate([mods, jnp.zeros((n_layers, 2, 2, d), F32)], axis=2)

    bands = _pool_bands(tm)
    cos, sin = _rope_tables(c_len, l_len)
    ws_bf = gm_ws[0].astype(BF16)
    wst_bf = jnp.swapaxes(gm_ws[0], 1, 2).astype(BF16)
    bst = jnp.zeros((CHUNK, 128), F32).at[:, :GMLP_GROUPS].set(gm_bs[0].T)
    ng = lambda i, j: ng_full[i, j]

    h0 = jnp.concatenate([ctx[0], x[0]], axis=0)
    y0, h1 = _pool_fwd(h0, mods[0], ng(0, 0), pool_wf[0].astype(BF16), ps_full[0], bands, nct, tm, c_len, l_len)
    h2, p0, ym0 = _mlp_fwd(h1, mods[0], ng(0, 1), w1_0g, w2_0g, nct, tm)
    q, k, v1, qt = _qkv_fwd(h2, mods[1], ng(1, 0), wqkv, attn_q_g, attn_k_g, cos, sin, n_heads, n_kv, nct, tm)
    o, lse, later_g = _flash_fwd(q, k, v1, n_kv, 4 * tm, tm, rider=later)
    (wo_g,) = _exchange_call(_ForwardToSibling(later_g[6:7]), "forward_wo")
    wo = _natural_rows(wo_g)
    y1, h3, rest_g = _wo_fwd(h2, o, wo, mods[1], nct, tm, rider=_ForwardToSibling(later_g[0:6] + later_g[7:9]))
    w1 = [w1_0g] + rest_g[0:3]
    w2 = [w2_0g] + rest_g[3:6]
    win = rest_g[6]
    wout = _natural_rows(rest_g[7])
    tm_lat = 2 * tm
    h4, p1, ym1 = _mlp_fwd(h3, mods[1], ng(1, 1), w1[1], w2[1], 0, tm_lat)
    y2, h5 = _gmlp_fwd(h4, mods[2], ng(2, 0), win, lng_full, lnb_full, ws_bf, bst, wout, tm)
    h6, p2, ym2 = _mlp_fwd(h5, mods[2], ng(2, 1), w1[2], w2[2], 0, tm_lat)
    y3, h7 = _pool_fwd(h6, mods[3], ng(3, 0), pool_wf[1].astype(BF16), ps_full[1], bands, 0, tm, c_len, l_len)
    h8, p3, ym3 = _mlp_fwd(h7, mods[3], ng(3, 1), w1[3], w2[3], 0, tm_lat)
    dh, loss_part, dfinal = _head(h8, final_g.reshape(1, d), loss_target[0], tm_lat)

    dw1, dw2, st_mlp = [None] * 4, [None] * 4, [None] * 4

    def mlp_back(i, dh, h_in, p, ym, nct_i):
        dh_in, m_bf, du, dacc, st, _ = _mlp_bwd(dh, h_in, p, ym, mods[i], ng(i, 1), w1[i], w2[i], nct_i, tm_lat)
        dw1[i] = _tn_matmul(m_bf, du, "tn_w1", col_shards=True)
        dw2[i] = _shard_rows(_tn_matmul(p, dacc, "tn_w2", square_x=True))
        st_mlp[i] = st
        return dh_in

    dh = mlp_back(3, dh, h7, p3, ym3, 0)
    dh, dpw1, st_pool3, _ = _pool_bwd(dh, h6, y3, mods[3], ng(3, 0), pool_wf[1].astype(BF16), ps_full[1], bands,
                                      0, tm, c_len, l_len, False)
    dh = mlp_back(2, dh, h5, p2, ym2, 0)
    dh, a_bf, dz, gated, dy, st_g, st_ln, dws, dbst = _gmlp_bwd(
        dh, h4, y2, mods[2], ng(2, 0), win, lng_full, lnb_full, ws_bf, wst_bf, bst, wout, tm)
    dwin = _tn_matmul(a_bf, dz, "tn_gm_in", col_shards=True)
    dwout = _tn_matmul(gated, dy, "tn_gm_out")
    dh = mlp_back(1, dh, h3, p1, ym1, 0)
    dy1, do, dot, delta, st_wo = _wo_bwd(dh, y1, o, wo, mods[1], n_kv, tm)
    dwo = _tn_matmul(o, dy1, "tn_wo")
    grads_mid = _AllToAll(dw1[1:] + dw2[1:] + [_shard_rows(dpw1.astype(BF16)), _shard_rows(dwo), dwin,
                                               _shard_rows(dwout)])
    dq, dkt, dvt, rode = _flash_bwd(q, qt, k, v1, do, dot, lse, delta, n_kv, min(8 * tm, l_len), tm,
                                    rider=grads_mid)
    g_w1, g_w2, (g_pool1, g_wo, g_gin, g_gout) = [None] + rode[0:3], [None] + rode[3:6], rode[6:]
    dh, a_bf, dqkv, st_q, dgains = _qkv_bwd(h2, dh, dq, dkt, dvt, mods[1], ng(1, 0), wqkv, attn_q_g, attn_k_g,
                                            cos, sin, n_heads, n_kv, nct, tm)
    dwqkv = _tn_matmul(a_bf, dqkv, "tn_qkv", col_shards=True)
    dh, m_bf, du, dacc, st_mlp[0], (g_qkv,) = _mlp_bwd(dh, h1, p0, ym0, mods[0], ng(0, 1), w1[0], w2[0], nct, tm,
                                                       rider=_AllToAll([dwqkv]))
    dw1_0 = _tn_matmul(m_bf, du, "tn_w1", col_shards=True)
    dw2_0, (g_w1[0],) = _tn_matmul(p0, dacc, "tn_w2", square_x=True, rider=_AllToAll([dw1_0]))
    grad_x, dpw0, st_pool0, (g_w2[0],) = _pool_bwd(dh, h0, y0, mods[0], ng(0, 0), pool_wf[0].astype(BF16),
                                                   ps_full[0], bands, nct, tm, c_len, l_len, True,
                                                   rider=_AllToAll([_shard_rows(dw2_0)]))

    mix_lat = [st_pool0[-1], st_q[1] + st_wo, st_g, st_pool3[-1]]
    mlp_lat = [st[-1] for st in st_mlp]
    dmod_lat = jnp.stack([jnp.concatenate([mix_lat[i][0:3], mlp_lat[i][0:3]]) for i in range(n_layers)])
    dmod_ctx = jnp.stack([jnp.concatenate([st_pool0[0][0:3], st_mlp[0][0][0:3]]),
                          jnp.concatenate([st_q[0][0:2], jnp.zeros((4, d), F32)]),
                          jnp.zeros((6, d), F32), jnp.zeros((6, d), F32)])
    dng_part = jnp.stack([jnp.stack([mix_lat[0][3] + st_pool0[0][3], mlp_lat[0][3] + st_mlp[0][0][3]]),
                          jnp.stack([mix_lat[1][3] + st_q[0][3], mlp_lat[1][3]]),
                          jnp.stack([mix_lat[2][3], mlp_lat[2][3]]),
                          jnp.stack([mix_lat[3][3], mlp_lat[3][3]])])
    dps_part = jnp.stack([mix_lat[0][4] + st_pool0[0][4], mix_lat[3][4]])
    small_parts = [dmod_lat.reshape(n_layers * 6, d), dmod_ctx.reshape(n_layers * 6, d),
                   dng_part.reshape(n_layers * 2, d), dps_part, st_ln, dgains, dws.reshape(-1, CHUNK),
                   dbst, dfinal, loss_part]
    (gm_lat, gm_ctx, g_ng, g_ps, g_ln, g_gains, g_ws, g_bst, g_final, loss_all, g_pool0) = _all_gather(
        small_parts, "gather_small_grads", extra=_AllToAll([_shard_rows(dpw0.astype(BF16))]))
    g_pool = jnp.stack([g_pool0, g_pool1], axis=1)

    gm_lat4 = gm_lat.reshape(N_DEV, n_layers, 6 * d)
    gm_ctx4 = gm_ctx.reshape(N_DEV, n_layers, 6 * d)
    dm_lat_loc = jnp.moveaxis(_my_cols(gm_lat4, me, n_ada), 0, 1)
    dm_ctx_loc = jnp.moveaxis(_my_cols(gm_ctx4, me, n_ada), 0, 1)
    g_ada_w, ds_part = _ada_grads(c_all, c_ctx2, ada_w, dm_lat_loc, dm_ctx_loc)
    (ds_all,) = _all_gather([ds_part], "gather_dsctx")
    g_c_ctx, loss_sum = _cctx_grad_and_loss(ds_all, c_ctx2, loss_all)
    g_c_ctx = g_c_ctx.reshape(d)

    n_ng = norm_g.shape[-1]
    n_ps = pool_scale.shape[-1]
    n_ln = gm_ln_g.shape[-1]
    gparts = {
        "c_ctx": g_c_ctx[None],
        "ada_w": g_ada_w[None],
        "ada_b": jnp.concatenate([gm_lat4, gm_ctx4], axis=0),
        "norm_g": _my_cols(g_ng.reshape(N_DEV, n_layers, 2, d), me, n_ng),
        "mlp_w1": g_w1, "mlp_w2": g_w2, "pool_w": g_pool,
        "pool_scale": _my_cols(g_ps, me, n_ps),
        "attn_w_qkv": g_qkv[:, None], "attn_w_o": g_wo[:, None],
        "attn_q_g": g_gains[:, 0:1], "attn_k_g": g_gains[:, 1:2],
        "gm_w_in": g_gin[:, None],
        "gm_ln_g": _my_cols(g_ln[:, 0:1], me, n_ln), "gm_ln_b": _my_cols(g_ln[:, 1:2], me, n_ln),
        "gm_ws": g_ws.reshape((N_DEV,) + gm_ws.shape),
        "gm_bs": jnp.swapaxes(g_bst[:, :, :GMLP_GROUPS], 1, 2)[:, None],
        "gm_w_out": g_gout[:, None],
        "final_g": g_final[:, 0],
    }
    weights = dict(c_ctx=(c_ctx, m_c_ctx, v_c_ctx), ada_w=(ada_w, m_ada_w, v_ada_w), ada_b=(ada_b, m_ada_b, v_ada_b),
                   norm_g=(norm_g, m_norm_g, v_norm_g), mlp_w1=(mlp_w1, m_mlp_w1, v_mlp_w1),
                   mlp_w2=(mlp_w2, m_mlp_w2, v_mlp_w2), pool_w=(pool_w, m_pool_w, v_pool_w),
                   pool_scale=(pool_scale, m_pool_scale, v_pool_scale),
                   attn_w_qkv=(attn_w_qkv, m_attn_w_qkv, v_attn_w_qkv), attn_w_o=(attn_w_o, m_attn_w_o, v_attn_w_o),
                   attn_q_g=(attn_q_g, m_attn_q_g, v_attn_q_g), attn_k_g=(attn_k_g, m_attn_k_g, v_attn_k_g),
                   gm_w_in=(gm_w_in, m_gm_w_in, v_gm_w_in), gm_ln_g=(gm_ln_g, m_gm_ln_g, v_gm_ln_g),
                   gm_ln_b=(gm_ln_b, m_gm_ln_b, v_gm_ln_b), gm_ws=(gm_ws, m_gm_ws, v_gm_ws),
                   gm_bs=(gm_bs, m_gm_bs, v_gm_bs), gm_w_out=(gm_w_out, m_gm_w_out, v_gm_w_out),
                   final_g=(final_g, m_final_g, v_final_g))
    grads, deltas, new_m, new_v = [], [], [], []
    for wname, (w_, m_, v_) in weights.items():
        g_, d_, nm_, nv_ = _adamw(w_, gparts[wname], m_, v_, "adamw_" + wname)
        grads.append(g_)
        deltas.append(d_)
        new_m.append(nm_)
        new_v.append(nv_)

    return (loss_sum[0, 0], grad_x[None], *grads, *deltas, *new_m, *new_v)
```
